```python
import jax, jax.numpy as jnp
from jax import lax
import numpy as np

D_MODEL = 1024
BATCH = 8
SEQ = 4096
DEPTH = 4

HEAD_DIM = 64
D_A = D_MODEL // 4
D_B = 3 * D_MODEL // 8
D_C = D_MODEL - D_A - D_B
N_HEADS_A = D_A // HEAD_DIM
N_HEADS_B = D_B // HEAD_DIM
N_HEADS_C = D_C // HEAD_DIM
D_MIX = D_A + D_B + D_C
D_IN = 3 * D_A + 2 * D_B + 2 * D_C
SHORT_CONV = 3
CHUNK = 128
CONF_CONV = 31
FFN_CONV = 3
D_FF = ((8 * D_MODEL // 3) + 255) // 256 * 256
EPS = 1e-6

kernel_name = "hybrid_conv_sgu_conformer_trunk"


def rms_norm(x, g):
    xf = x.astype(jnp.float32)
    y = xf * lax.rsqrt(jnp.mean(xf * xf, axis=-1, keepdims=True) + EPS)
    return (y * g.astype(jnp.float32)).astype(x.dtype)


def layer_norm(x, g, b):
    xf = x.astype(jnp.float32)
    mu = jnp.mean(xf, axis=-1, keepdims=True)
    xc = xf - mu
    y = xc * lax.rsqrt(jnp.mean(xc * xc, axis=-1, keepdims=True) + EPS)
    return (y * g.astype(jnp.float32) + b.astype(jnp.float32)).astype(x.dtype)


def causal_dwconv(x, w):
    K, C = w.shape
    return lax.conv_general_dilated(
        x, w[:, None, :].astype(x.dtype),
        window_strides=(1,), padding=[(K - 1, 0)],
        dimension_numbers=("NWC", "WIO", "NWC"),
        feature_group_count=C)


def short_gated_conv(za, conv_w):
    b_gate, c_gate, xa = jnp.split(za, 3, axis=-1)
    return b_gate * causal_dwconv(c_gate * xa, conv_w)


def chunk_spatial_gate(zb, ln_g, ln_b, w_s, b_s):
    z = jax.nn.gelu(zb, approximate=False)
    u, v = jnp.split(z, 2, axis=-1)
    v = layer_norm(v, ln_g, ln_b)
    bsz, t, _ = v.shape
    v = v.reshape(bsz, t // CHUNK, CHUNK, N_HEADS_B, HEAD_DIM)
    mask = jnp.tril(jnp.ones((CHUNK, CHUNK), dtype=bool))
    w = jnp.where(mask[None], w_s, jnp.zeros((), w_s.dtype))
    s = jnp.einsum("hts,bnshd->bnthd", w, v) + b_s.T[None, None, :, :, None]
    return u * s.reshape(bsz, t, D_B)


def conformer_conv(zc, conv_w, conv_b, ln_g, ln_b):
    a, g = jnp.split(zc, 2, axis=-1)
    y = a * jax.nn.sigmoid(g)
    y = causal_dwconv(y, conv_w) + conv_b
    return jax.nn.silu(layer_norm(y, ln_g, ln_b))


def _fwd_setup_inputs(seed: int = 0) -> dict:
    key = jax.random.key(seed)
    ks = jax.random.split(key, 24)
    L = DEPTH
    f32 = jnp.float32

    def nrm(k, shape, scale):
        return jax.random.normal(k, shape, f32) * scale

    def gain(k, shape):
        return 1.0 + 0.02 * jax.random.normal(k, shape, f32)

    return {
        "x": jax.random.normal(ks[0], (BATCH, SEQ, D_MODEL), f32),
        "pre_mix_g": gain(ks[1], (L, D_MODEL)),
        "w_in": nrm(ks[2], (L, D_MODEL, D_IN), D_MODEL ** -0.5),
        "conv_a_w": nrm(ks[3], (L, SHORT_CONV, D_A), SHORT_CONV ** -0.5),
        "sgu_ln_g": gain(ks[4], (L, D_B)),
        "sgu_ln_b": nrm(ks[5], (L, D_B), 0.02),
        "sgu_w": nrm(ks[6], (L, N_HEADS_B, CHUNK, CHUNK), CHUNK ** -0.5),
        "sgu_b": gain(ks[7], (L, N_HEADS_B, CHUNK)),
        "conv_c_w": nrm(ks[8], (L, CONF_CONV, D_C), CONF_CONV ** -0.5),
        "conv_c_b": nrm(ks[9], (L, D_C), 0.02),
        "conv_ln_g": gain(ks[10], (L, D_C)),
        "conv_ln_b": nrm(ks[11], (L, D_C), 0.02),
        "grp_norm_g": gain(ks[12], (L, D_MIX)),
        "w_out": nrm(ks[13], (L, D_MIX, D_MODEL), D_MIX ** -0.5),
        "post_mix_g": gain(ks[14], (L, D_MODEL)),
        "pre_ffn_g": gain(ks[15], (L, D_MODEL)),
        "w_up": nrm(ks[16], (L, D_MODEL, 2 * D_FF), D_MODEL ** -0.5),
        "ffn_conv_w": nrm(ks[17], (L, FFN_CONV, 2 * D_FF), FFN_CONV ** -0.5),
        "w_down": nrm(ks[18], (L, D_FF, D_MODEL), D_FF ** -0.5),
        "post_ffn_g": gain(ks[19], (L, D_MODEL)),
    }


def _fwd_reference(x, pre_mix_g, w_in, conv_a_w, sgu_ln_g, sgu_ln_b, sgu_w, sgu_b,
              conv_c_w, conv_c_b, conv_ln_g, conv_ln_b, grp_norm_g, w_out,
              post_mix_g, pre_ffn_g, w_up, ffn_conv_w, w_down, post_ffn_g):
    for l in range(DEPTH):
        h = rms_norm(x, pre_mix_g[l])
        z = h @ w_in[l]
        za, zb, zc = jnp.split(z, [3 * D_A, 3 * D_A + 2 * D_B], axis=-1)
        ya = short_gated_conv(za, conv_a_w[l])
        yb = chunk_spatial_gate(zb, sgu_ln_g[l], sgu_ln_b[l], sgu_w[l], sgu_b[l])
        yc = conformer_conv(zc, conv_c_w[l], conv_c_b[l], conv_ln_g[l], conv_ln_b[l])
        ga, gb, gc = jnp.split(grp_norm_g[l], [D_A, D_A + D_B])
        y = jnp.concatenate([rms_norm(ya, ga), rms_norm(yb, gb), rms_norm(yc, gc)], axis=-1)
        x = x + rms_norm(y @ w_out[l], post_mix_g[l])
        h = rms_norm(x, pre_ffn_g[l])
        up = causal_dwconv(h @ w_up[l], ffn_conv_w[l])
        gate, val = jnp.split(up, 2, axis=-1)
        x = x + rms_norm((jax.nn.silu(gate) * val) @ w_down[l], post_ffn_g[l])
    return x


import jax as _jax
import jax.numpy as _jnp

TWIN_FORMAT = 'train_step'
FWD_PARAMS = ['x', 'pre_mix_g', 'w_in', 'conv_a_w', 'sgu_ln_g', 'sgu_ln_b', 'sgu_w', 'sgu_b', 'conv_c_w', 'conv_c_b', 'conv_ln_g', 'conv_ln_b', 'grp_norm_g', 'w_out', 'post_mix_g', 'pre_ffn_g', 'w_up', 'ffn_conv_w', 'w_down', 'post_ffn_g']
TWIN_WEIGHTS = ['pre_mix_g', 'w_in', 'conv_a_w', 'sgu_ln_g', 'sgu_ln_b', 'sgu_w', 'sgu_b', 'conv_c_w', 'conv_c_b', 'conv_ln_g', 'conv_ln_b', 'grp_norm_g', 'w_out', 'post_mix_g', 'pre_ffn_g', 'w_up', 'ffn_conv_w', 'w_down', 'post_ffn_g']
TWIN_DIFF_INPUT = 'x'
TWIN_INPUTS = ['x', 'pre_mix_g', 'w_in', 'conv_a_w', 'sgu_ln_g', 'sgu_ln_b', 'sgu_w', 'sgu_b', 'conv_c_w', 'conv_c_b', 'conv_ln_g', 'conv_ln_b', 'grp_norm_g', 'w_out', 'post_mix_g', 'pre_ffn_g', 'w_up', 'ffn_conv_w', 'w_down', 'post_ffn_g', 'loss_target', 'm_pre_mix_g', 'm_w_in', 'm_conv_a_w', 'm_sgu_ln_g', 'm_sgu_ln_b', 'm_sgu_w', 'm_sgu_b', 'm_conv_c_w', 'm_conv_c_b', 'm_conv_ln_g', 'm_conv_ln_b', 'm_grp_norm_g', 'm_w_out', 'm_post_mix_g', 'm_pre_ffn_g', 'm_w_up', 'm_ffn_conv_w', 'm_w_down', 'm_post_ffn_g', 'v_pre_mix_g', 'v_w_in', 'v_conv_a_w', 'v_sgu_ln_g', 'v_sgu_ln_b', 'v_sgu_w', 'v_sgu_b', 'v_conv_c_w', 'v_conv_c_b', 'v_conv_ln_g', 'v_conv_ln_b', 'v_grp_norm_g', 'v_w_out', 'v_post_mix_g', 'v_pre_ffn_g', 'v_w_up', 'v_ffn_conv_w', 'v_w_down', 'v_post_ffn_g']
TWIN_OUTPUTS = ['loss', 'grad_x', 'grad_pre_mix_g', 'grad_w_in', 'grad_conv_a_w', 'grad_sgu_ln_g', 'grad_sgu_ln_b', 'grad_sgu_w', 'grad_sgu_b', 'grad_conv_c_w', 'grad_conv_c_b', 'grad_conv_ln_g', 'grad_conv_ln_b', 'grad_grp_norm_g', 'grad_w_out', 'grad_post_mix_g', 'grad_pre_ffn_g', 'grad_w_up', 'grad_ffn_conv_w', 'grad_w_down', 'grad_post_ffn_g', 'delta_pre_mix_g', 'delta_w_in', 'delta_conv_a_w', 'delta_sgu_ln_g', 'delta_sgu_ln_b', 'delta_sgu_w', 'delta_sgu_b', 'delta_conv_c_w', 'delta_conv_c_b', 'delta_conv_ln_g', 'delta_conv_ln_b', 'delta_grp_norm_g', 'delta_w_out', 'delta_post_mix_g', 'delta_pre_ffn_g', 'delta_w_up', 'delta_ffn_conv_w', 'delta_w_down', 'delta_post_ffn_g', 'new_m_pre_mix_g', 'new_m_w_in', 'new_m_conv_a_w', 'new_m_sgu_ln_g', 'new_m_sgu_ln_b', 'new_m_sgu_w', 'new_m_sgu_b', 'new_m_conv_c_w', 'new_m_conv_c_b', 'new_m_conv_ln_g', 'new_m_conv_ln_b', 'new_m_grp_norm_g', 'new_m_w_out', 'new_m_post_mix_g', 'new_m_pre_ffn_g', 'new_m_w_up', 'new_m_ffn_conv_w', 'new_m_w_down', 'new_m_post_ffn_g', 'new_v_pre_mix_g', 'new_v_w_in', 'new_v_conv_a_w', 'new_v_sgu_ln_g', 'new_v_sgu_ln_b', 'new_v_sgu_w', 'new_v_sgu_b', 'new_v_conv_c_w', 'new_v_conv_c_b', 'new_v_conv_ln_g', 'new_v_conv_ln_b', 'new_v_grp_norm_g', 'new_v_w_out', 'new_v_post_mix_g', 'new_v_pre_ffn_g', 'new_v_w_up', 'new_v_ffn_conv_w', 'new_v_w_down', 'new_v_post_ffn_g']
TWIN_LEAF_KINDS = {'loss': 'loss', 'grad_x': 'grad_x', 'grad_pre_mix_g': 'grad_w', 'grad_w_in': 'grad_w', 'grad_conv_a_w': 'grad_w', 'grad_sgu_ln_g': 'grad_w', 'grad_sgu_ln_b': 'grad_w', 'grad_sgu_w': 'grad_w', 'grad_sgu_b': 'grad_w', 'grad_conv_c_w': 'grad_w', 'grad_conv_c_b': 'grad_w', 'grad_conv_ln_g': 'grad_w', 'grad_conv_ln_b': 'grad_w', 'grad_grp_norm_g': 'grad_w', 'grad_w_out': 'grad_w', 'grad_post_mix_g': 'grad_w', 'grad_pre_ffn_g': 'grad_w', 'grad_w_up': 'grad_w', 'grad_ffn_conv_w': 'grad_w', 'grad_w_down': 'grad_w', 'grad_post_ffn_g': 'grad_w', 'delta_pre_mix_g': 'delta_w', 'delta_w_in': 'delta_w', 'delta_conv_a_w': 'delta_w', 'delta_sgu_ln_g': 'delta_w', 'delta_sgu_ln_b': 'delta_w', 'delta_sgu_w': 'delta_w', 'delta_sgu_b': 'delta_w', 'delta_conv_c_w': 'delta_w', 'delta_conv_c_b': 'delta_w', 'delta_conv_ln_g': 'delta_w', 'delta_conv_ln_b': 'delta_w', 'delta_grp_norm_g': 'delta_w', 'delta_w_out': 'delta_w', 'delta_post_mix_g': 'delta_w', 'delta_pre_ffn_g': 'delta_w', 'delta_w_up': 'delta_w', 'delta_ffn_conv_w': 'delta_w', 'delta_w_down': 'delta_w', 'delta_post_ffn_g': 'delta_w', 'new_m_pre_mix_g': 'new_m', 'new_m_w_in': 'new_m', 'new_m_conv_a_w': 'new_m', 'new_m_sgu_ln_g': 'new_m', 'new_m_sgu_ln_b': 'new_m', 'new_m_sgu_w': 'new_m', 'new_m_sgu_b': 'new_m', 'new_m_conv_c_w': 'new_m', 'new_m_conv_c_b': 'new_m', 'new_m_conv_ln_g': 'new_m', 'new_m_conv_ln_b': 'new_m', 'new_m_grp_norm_g': 'new_m', 'new_m_w_out': 'new_m', 'new_m_post_mix_g': 'new_m', 'new_m_pre_ffn_g': 'new_m', 'new_m_w_up': 'new_m', 'new_m_ffn_conv_w': 'new_m', 'new_m_w_down': 'new_m', 'new_m_post_ffn_g': 'new_m', 'new_v_pre_mix_g': 'new_v', 'new_v_w_in': 'new_v', 'new_v_conv_a_w': 'new_v', 'new_v_sgu_ln_g': 'new_v', 'new_v_sgu_ln_b': 'new_v', 'new_v_sgu_w': 'new_v', 'new_v_sgu_b': 'new_v', 'new_v_conv_c_w': 'new_v', 'new_v_conv_c_b': 'new_v', 'new_v_conv_ln_g': 'new_v', 'new_v_conv_ln_b': 'new_v', 'new_v_grp_norm_g': 'new_v', 'new_v_w_out': 'new_v', 'new_v_post_mix_g': 'new_v', 'new_v_pre_ffn_g': 'new_v', 'new_v_w_up': 'new_v', 'new_v_ffn_conv_w': 'new_v', 'new_v_w_down': 'new_v', 'new_v_post_ffn_g': 'new_v'}


def _forward(args):
    return _fwd_reference(*[args[k] for k in FWD_PARAMS])


def _output_shape():
    def fwd():
        inp = _fwd_setup_inputs(0)
        return _fwd_reference(*[inp[k] for k in FWD_PARAMS])
    out = _jax.eval_shape(fwd)
    return out.shape, out.dtype

N_MICROBATCH = 1
ADAM_LR = 0.001
ADAM_B1 = 0.9
ADAM_B2 = 0.999
ADAM_EPS = 1e-08
ADAM_WD = 0.01
ADAM_STEP = 10
PER_EXAMPLE_BATCH_AXIS = {'x': 0, 'loss_target': 0}
SHARED_INPUTS = []
_WEIGHT_DTYPES = {'pre_mix_g': _jnp.float32, 'w_in': _jnp.float32, 'conv_a_w': _jnp.float32, 'sgu_ln_g': _jnp.float32, 'sgu_ln_b': _jnp.float32, 'sgu_w': _jnp.float32, 'sgu_b': _jnp.float32, 'conv_c_w': _jnp.float32, 'conv_c_b': _jnp.float32, 'conv_ln_g': _jnp.float32, 'conv_ln_b': _jnp.float32, 'grp_norm_g': _jnp.float32, 'w_out': _jnp.float32, 'post_mix_g': _jnp.float32, 'pre_ffn_g': _jnp.float32, 'w_up': _jnp.float32, 'ffn_conv_w': _jnp.float32, 'w_down': _jnp.float32, 'post_ffn_g': _jnp.float32}
MOMENT_SCALE = {'pre_mix_g': 2.562837e+00, 'w_in': 1.655347e+00, 'conv_a_w': 1.338302e+00, 'sgu_ln_g': 7.125138e-01, 'sgu_ln_b': 7.557495e-01, 'sgu_w': 4.964497e-01, 'sgu_b': 8.156274e-01, 'conv_c_w': 3.018559e+00, 'conv_c_b': 3.121671e+01, 'conv_ln_g': 1.199421e+01, 'conv_ln_b': 1.782693e+01, 'grp_norm_g': 6.366656e+00, 'w_out': 6.494362e+00, 'post_mix_g': 3.285873e+01, 'pre_ffn_g': 2.406911e+00, 'w_up': 1.027546e+00, 'ffn_conv_w': 1.188803e+00, 'w_down': 2.069473e+00, 'post_ffn_g': 3.203364e+01}


def _to_microbatches(a, axis):
    t = _jnp.moveaxis(a, axis, 0)
    t = t.reshape((N_MICROBATCH, t.shape[0] // N_MICROBATCH) + t.shape[1:])
    return _jnp.moveaxis(t, 1, axis + 1)


def setup_inputs(seed: int = 0) -> dict:
    inp = _fwd_setup_inputs(seed)
    key = _jax.random.fold_in(_jax.random.key(seed), 7919)
    shape, _ = _output_shape()
    out = dict(inp)
    out["loss_target"] = _jax.random.normal(_jax.random.fold_in(key, 0), shape, _jnp.float32)
    for i, name in enumerate(TWIN_WEIGHTS):
        w = inp[name].astype(_jnp.float32)
        if MOMENT_SCALE is None:
            s = _jnp.sqrt(_jnp.mean(_jnp.square(w)) + 1e-30)
        else:
            s = MOMENT_SCALE[name]
        km, kv = _jax.random.split(_jax.random.fold_in(key, i + 1))
        out[name] = w
        out["m_" + name] = s * _jax.random.normal(km, w.shape, _jnp.float32)
        out["v_" + name] = (s * s) * _jax.random.uniform(kv, w.shape, _jnp.float32, 0.5, 1.5)
    if N_MICROBATCH > 1:
        for name, axis in PER_EXAMPLE_BATCH_AXIS.items():
            out[name] = _to_microbatches(out[name], axis)
    return {'x': out['x'], 'pre_mix_g': out['pre_mix_g'], 'w_in': out['w_in'], 'conv_a_w': out['conv_a_w'], 'sgu_ln_g': out['sgu_ln_g'], 'sgu_ln_b': out['sgu_ln_b'], 'sgu_w': out['sgu_w'], 'sgu_b': out['sgu_b'], 'conv_c_w': out['conv_c_w'], 'conv_c_b': out['conv_c_b'], 'conv_ln_g': out['conv_ln_g'], 'conv_ln_b': out['conv_ln_b'], 'grp_norm_g': out['grp_norm_g'], 'w_out': out['w_out'], 'post_mix_g': out['post_mix_g'], 'pre_ffn_g': out['pre_ffn_g'], 'w_up': out['w_up'], 'ffn_conv_w': out['ffn_conv_w'], 'w_down': out['w_down'], 'post_ffn_g': out['post_ffn_g'], 'loss_target': out['loss_target'], 'm_pre_mix_g': out['m_pre_mix_g'], 'm_w_in': out['m_w_in'], 'm_conv_a_w': out['m_conv_a_w'], 'm_sgu_ln_g': out['m_sgu_ln_g'], 'm_sgu_ln_b': out['m_sgu_ln_b'], 'm_sgu_w': out['m_sgu_w'], 'm_sgu_b': out['m_sgu_b'], 'm_conv_c_w': out['m_conv_c_w'], 'm_conv_c_b': out['m_conv_c_b'], 'm_conv_ln_g': out['m_conv_ln_g'], 'm_conv_ln_b': out['m_conv_ln_b'], 'm_grp_norm_g': out['m_grp_norm_g'], 'm_w_out': out['m_w_out'], 'm_post_mix_g': out['m_post_mix_g'], 'm_pre_ffn_g': out['m_pre_ffn_g'], 'm_w_up': out['m_w_up'], 'm_ffn_conv_w': out['m_ffn_conv_w'], 'm_w_down': out['m_w_down'], 'm_post_ffn_g': out['m_post_ffn_g'], 'v_pre_mix_g': out['v_pre_mix_g'], 'v_w_in': out['v_w_in'], 'v_conv_a_w': out['v_conv_a_w'], 'v_sgu_ln_g': out['v_sgu_ln_g'], 'v_sgu_ln_b': out['v_sgu_ln_b'], 'v_sgu_w': out['v_sgu_w'], 'v_sgu_b': out['v_sgu_b'], 'v_conv_c_w': out['v_conv_c_w'], 'v_conv_c_b': out['v_conv_c_b'], 'v_conv_ln_g': out['v_conv_ln_g'], 'v_conv_ln_b': out['v_conv_ln_b'], 'v_grp_norm_g': out['v_grp_norm_g'], 'v_w_out': out['v_w_out'], 'v_post_mix_g': out['v_post_mix_g'], 'v_pre_ffn_g': out['v_pre_ffn_g'], 'v_w_up': out['v_w_up'], 'v_ffn_conv_w': out['v_ffn_conv_w'], 'v_w_down': out['v_w_down'], 'v_post_ffn_g': out['v_post_ffn_g']}


def _loss(weights, diff, rest, loss_target):
    with _jax.named_scope("forward"):
        args = {**rest, TWIN_DIFF_INPUT: diff, **{k: w.astype(_WEIGHT_DTYPES[k]) for k, w in weights.items()}}
        y = _forward(args)
    with _jax.named_scope("loss_head"):
        err = _jnp.square(y.astype(_jnp.float32) - loss_target)
        return 0.5 * _jnp.sum(_jnp.mean(err, axis=-1)) if err.ndim else 0.5 * err


def _adamw(w, g, m, v):
    m = ADAM_B1 * m + (1.0 - ADAM_B1) * g
    v = ADAM_B2 * v + (1.0 - ADAM_B2) * _jnp.square(g)
    m_hat = m / (1.0 - ADAM_B1 ** ADAM_STEP)
    v_hat = v / (1.0 - ADAM_B2 ** ADAM_STEP)
    delta = -ADAM_LR * (m_hat / (_jnp.sqrt(v_hat) + ADAM_EPS) + ADAM_WD * w)
    return delta, m, v


def reference(x, pre_mix_g, w_in, conv_a_w, sgu_ln_g, sgu_ln_b, sgu_w, sgu_b, conv_c_w, conv_c_b, conv_ln_g, conv_ln_b, grp_norm_g, w_out, post_mix_g, pre_ffn_g, w_up, ffn_conv_w, w_down, post_ffn_g, loss_target, m_pre_mix_g, m_w_in, m_conv_a_w, m_sgu_ln_g, m_sgu_ln_b, m_sgu_w, m_sgu_b, m_conv_c_w, m_conv_c_b, m_conv_ln_g, m_conv_ln_b, m_grp_norm_g, m_w_out, m_post_mix_g, m_pre_ffn_g, m_w_up, m_ffn_conv_w, m_w_down, m_post_ffn_g, v_pre_mix_g, v_w_in, v_conv_a_w, v_sgu_ln_g, v_sgu_ln_b, v_sgu_w, v_sgu_b, v_conv_c_w, v_conv_c_b, v_conv_ln_g, v_conv_ln_b, v_grp_norm_g, v_w_out, v_post_mix_g, v_pre_ffn_g, v_w_up, v_ffn_conv_w, v_w_down, v_post_ffn_g):
    given = dict(x=x, pre_mix_g=pre_mix_g, w_in=w_in, conv_a_w=conv_a_w, sgu_ln_g=sgu_ln_g, sgu_ln_b=sgu_ln_b, sgu_w=sgu_w, sgu_b=sgu_b, conv_c_w=conv_c_w, conv_c_b=conv_c_b, conv_ln_g=conv_ln_g, conv_ln_b=conv_ln_b, grp_norm_g=grp_norm_g, w_out=w_out, post_mix_g=post_mix_g, pre_ffn_g=pre_ffn_g, w_up=w_up, ffn_conv_w=ffn_conv_w, w_down=w_down, post_ffn_g=post_ffn_g, loss_target=loss_target, m_pre_mix_g=m_pre_mix_g, m_w_in=m_w_in, m_conv_a_w=m_conv_a_w, m_sgu_ln_g=m_sgu_ln_g, m_sgu_ln_b=m_sgu_ln_b, m_sgu_w=m_sgu_w, m_sgu_b=m_sgu_b, m_conv_c_w=m_conv_c_w, m_conv_c_b=m_conv_c_b, m_conv_ln_g=m_conv_ln_g, m_conv_ln_b=m_conv_ln_b, m_grp_norm_g=m_grp_norm_g, m_w_out=m_w_out, m_post_mix_g=m_post_mix_g, m_pre_ffn_g=m_pre_ffn_g, m_w_up=m_w_up, m_ffn_conv_w=m_ffn_conv_w, m_w_down=m_w_down, m_post_ffn_g=m_post_ffn_g, v_pre_mix_g=v_pre_mix_g, v_w_in=v_w_in, v_conv_a_w=v_conv_a_w, v_sgu_ln_g=v_sgu_ln_g, v_sgu_ln_b=v_sgu_ln_b, v_sgu_w=v_sgu_w, v_sgu_b=v_sgu_b, v_conv_c_w=v_conv_c_w, v_conv_c_b=v_conv_c_b, v_conv_ln_g=v_conv_ln_g, v_conv_ln_b=v_conv_ln_b, v_grp_norm_g=v_grp_norm_g, v_w_out=v_w_out, v_post_mix_g=v_post_mix_g, v_pre_ffn_g=v_pre_ffn_g, v_w_up=v_w_up, v_ffn_conv_w=v_ffn_conv_w, v_w_down=v_w_down, v_post_ffn_g=v_post_ffn_g)
    weights = {n: given[n] for n in TWIN_WEIGHTS}
    shared = {n: given[n] for n in SHARED_INPUTS}
    per_example = {n: given[n] for n in ['x']}
    grad_fn = _jax.value_and_grad(_loss, argnums=(0, 1))

    def one_microbatch(ex, loss_target):
        ex = dict(ex)
        diff = ex.pop(TWIN_DIFF_INPUT)
        return grad_fn(weights, diff, {**shared, **ex}, loss_target)

    if N_MICROBATCH == 1:
        loss, (grad_w, grad_x) = one_microbatch(per_example, given["loss_target"])
    else:
        def body(carry, xs):
            loss_sum, grad_sum = carry
            l_k, (gw_k, gx_k) = one_microbatch(xs[0], xs[1])
            with _jax.named_scope("update"):
                return (loss_sum + l_k, _jax.tree.map(_jnp.add, grad_sum, gw_k)), gx_k

        init = (_jnp.zeros((), _jnp.float32), _jax.tree.map(_jnp.zeros_like, weights))
        (loss, grad_w), grad_x = _jax.lax.scan(body, init, (per_example, given["loss_target"]))
    with _jax.named_scope("update"):
        delta_w, new_m, new_v = {}, {}, {}
        for n in TWIN_WEIGHTS:
            delta_w[n], new_m[n], new_v[n] = _adamw(weights[n], grad_w[n], given["m_" + n], given["v_" + n])
    return (loss, grad_x, *[grad_w[n] for n in TWIN_WEIGHTS], *[delta_w[n] for n in TWIN_WEIGHTS],
            *[new_m[n] for n in TWIN_WEIGHTS], *[new_v[n] for n in TWIN_WEIGHTS])
```

```python
import functools
import math

import jax
import jax.numpy as jnp
from jax import lax
from jax.experimental import pallas as pl
from jax.experimental.pallas import tpu as pltpu

F32 = jnp.float32
BF16 = jnp.bfloat16
MM_DTYPE = BF16

D_MODEL = 1024
SEQ = 4096
DEPTH = 4
D_A = 256
D_B = 384
D_C = 384
D_IN = 3 * D_A + 2 * D_B + 2 * D_C
D_FF = 2816
K_A = 3
K_C = 31
K_F = 3
CHUNK = 128
HEAD = 64
N_HEADS_B = D_B // HEAD
EPS = 1e-6
N_CHIPS = 4

ADAM_LR = 0.001
ADAM_B1 = 0.9
ADAM_B2 = 0.999
ADAM_EPS = 1e-08
ADAM_WD = 0.01
ADAM_STEP = 10

HALO_A = 8
HALO_C = 32
HALO_F = 8
LANES = 1024
VMEM_LIMIT = 56 * 1024 * 1024

MESH_ID = pl.DeviceIdType.MESH


def _params(sem=None):
    return pltpu.CompilerParams(dimension_semantics=sem, vmem_limit_bytes=VMEM_LIMIT)


def _const_spec(shape):
    nd = len(shape)
    return pl.BlockSpec(shape, lambda *_: (0,) * nd, pipeline_mode=pl.Buffered(1))


def _rowsum8(a):
    r, c = a.shape
    return jnp.sum(a.reshape(r // 8, 8, c), axis=0)


def _rstd(x):
    return lax.rsqrt(jnp.mean(x * x, axis=-1, keepdims=True) + EPS)


def _rms_bwd(x, r, g, dy):
    gdy = g * dy
    return r * gdy - x * (r * r * r) * jnp.mean(gdy * x, axis=-1, keepdims=True)


def _ln_fwd(x):
    mu = jnp.mean(x, axis=-1, keepdims=True)
    xc = x - mu
    r = lax.rsqrt(jnp.mean(xc * xc, axis=-1, keepdims=True) + EPS)
    return xc * r, r


def _ln_bwd(xh, r, dxh):
    return r * (dxh - jnp.mean(dxh, axis=-1, keepdims=True) - xh * jnp.mean(dxh * xh, axis=-1, keepdims=True))


def _gelu(x):
    return 0.5 * x * (1.0 + lax.erf(x * (1.0 / math.sqrt(2.0))))


def _gelu_grad(x):
    cdf = 0.5 * (1.0 + lax.erf(x * (1.0 / math.sqrt(2.0))))
    pdf = jnp.exp(-0.5 * x * x) * (1.0 / math.sqrt(2.0 * math.pi))
    return cdf + x * pdf


def _dot(a, b):
    return jnp.dot(a, b, preferred_element_type=F32)


def _dot_nt(a, b):
    return lax.dot_general(a, b, (((1,), (1,)), ((), ())), preferred_element_type=F32)


def _dot_tn(a, b):
    return lax.dot_general(a, b, (((0,), (0,)), ((), ())), preferred_element_type=F32)


def _col_chunk(n):
    for c in (1408, 1024, 768, 512, 256, 128):
        if n % c == 0:
            return c
    raise ValueError(n)


def norm_matmul(x, g, w, tm):
    t, d = x.shape
    assert t % tm == 0, (t, tm)
    n = w.shape[1]
    cn = _col_chunk(n)

    def body(x_ref, g_ref, w_ref, o_ref, h_ref):
        xv = x_ref[...]
        h = (xv * _rstd(xv) * g_ref[...]).astype(MM_DTYPE)
        h_ref[...] = h
        for c0 in range(0, n, cn):
            o_ref[:, c0:c0 + cn] = _dot(h, w_ref[:, c0:c0 + cn])

    return pl.pallas_call(
        body, name="norm_matmul", grid=(t // tm,),
        in_specs=[pl.BlockSpec((tm, d), lambda i: (i, 0)), _const_spec((1, d)), _const_spec((d, n))],
        out_specs=[pl.BlockSpec((tm, n), lambda i: (i, 0)), pl.BlockSpec((tm, d), lambda i: (i, 0))],
        out_shape=[jax.ShapeDtypeStruct((t, n), F32), jax.ShapeDtypeStruct((t, d), MM_DTYPE)],
        compiler_params=_params(("arbitrary",)),
    )(x, g, w)


def matmul_nt_norm_bwd(gy, w, x, g, dres, tm):
    t, n = gy.shape
    assert t % tm == 0, (t, tm)
    d = w.shape[0]
    cn = _col_chunk(n)
    steps = t // tm

    def body(gy_ref, w_ref, x_ref, g_ref, dres_ref, dx_ref, dg_ref, acc_ref):
        i = pl.program_id(0)

        @pl.when(i == 0)
        def _():
            acc_ref[...] = jnp.zeros_like(acc_ref)

        dh = jnp.zeros((tm, d), F32)
        for c0 in range(0, n, cn):
            dh = dh + _dot_nt(gy_ref[:, c0:c0 + cn], w_ref[:, c0:c0 + cn])
        xv = x_ref[...]
        r = _rstd(xv)
        gv = g_ref[...]
        dx_ref[...] = dres_ref[...] + _rms_bwd(xv, r, gv, dh)
        acc_ref[...] += _rowsum8(dh * xv * r)

        @pl.when(i == steps - 1)
        def _():
            dg_ref[...] = jnp.sum(acc_ref[...], axis=0, keepdims=True)

    return pl.pallas_call(
        body, name="matmul_nt_norm_bwd", grid=(steps,),
        in_specs=[pl.BlockSpec((tm, n), lambda i: (i, 0)), _const_spec((d, n)), pl.BlockSpec((tm, d), lambda i: (i, 0)),
                  _const_spec((1, d)), pl.BlockSpec((tm, d), lambda i: (i, 0))],
        out_specs=[pl.BlockSpec((tm, d), lambda i: (i, 0)), pl.BlockSpec((1, d), lambda i: (0, 0))],
        out_shape=[jax.ShapeDtypeStruct((t, d), F32), jax.ShapeDtypeStruct((1, d), F32)],
        scratch_shapes=[pltpu.VMEM((8, d), F32)],
        compiler_params=_params(("arbitrary",)),
    )(gy, w, x, g, dres)


def matmul_tn(a, b, tm, tn, tk):
    t, m = a.shape
    n = b.shape[1]
    assert t % tk == 0 and m % tm == 0 and n % tn == 0, (a.shape, b.shape, tm, tn, tk)

    def body(a_ref, b_ref, o_ref):
        @pl.when(pl.program_id(2) == 0)
        def _():
            o_ref[...] = jnp.zeros_like(o_ref)

        o_ref[...] += _dot_tn(a_ref[...], b_ref[...])

    return pl.pallas_call(
        body, name="matmul_tn", grid=(m // tm, n // tn, t // tk),
        in_specs=[pl.BlockSpec((tk, tm), lambda i, j, k: (k, i)), pl.BlockSpec((tk, tn), lambda i, j, k: (k, j))],
        out_specs=pl.BlockSpec((tm, tn), lambda i, j, k: (i, j)),
        out_shape=jax.ShapeDtypeStruct((m, n), F32),
        compiler_params=_params(("arbitrary", "arbitrary", "arbitrary")),
    )(a, b)


def _head_select(parts):
    head = lax.broadcasted_iota(jnp.int32, parts[0].shape, 1) // HEAD
    acc = parts[0]
    for h in range(1, N_HEADS_B):
        acc = jnp.where(head == h, parts[h], acc)
    return acc


def _mixer_forward(z, prm, pa_ext, yg_ext, tt):
    caw, lng, lnb, wm, bias_e, ccw, ccb, clg, clb = prm
    bg = z[:, 0:D_A]
    cg = z[:, D_A:2 * D_A]
    xa = z[:, 2 * D_A:3 * D_A]
    pa_ext[HALO_A:HALO_A + tt, :] = cg * xa
    q = jnp.zeros((tt, D_A), F32)
    for k in range(K_A):
        q = q + caw[k:k + 1, :] * pa_ext[pl.ds(HALO_A - (K_A - 1) + k, tt), :]
    ya = bg * q
    o_b = 3 * D_A
    zu = z[:, o_b:o_b + D_B]
    zv = z[:, o_b + D_B:o_b + 2 * D_B]
    u = _gelu(zu)
    vh, rv = _ln_fwd(_gelu(zv))
    vnb = (vh * lng + lnb).astype(MM_DTYPE)
    s_parts = []
    for c in range(tt // CHUNK):
        vc = vnb[c * CHUNK:(c + 1) * CHUNK, :]
        s_parts.append(_head_select([_dot(wm[h], vc) for h in range(N_HEADS_B)]) + bias_e)
    s = jnp.concatenate(s_parts, axis=0) if len(s_parts) > 1 else s_parts[0]
    yb = u * s
    o_c = o_b + 2 * D_B
    a = z[:, o_c:o_c + D_C]
    sg = jax.nn.sigmoid(z[:, o_c + D_C:o_c + 2 * D_C])
    yg_ext[HALO_C:HALO_C + tt, :] = a * sg
    yc = jnp.zeros((tt, D_C), F32) + ccb
    for k in range(K_C):
        yc = yc + ccw[k:k + 1, :] * yg_ext[pl.ds(HALO_C - (K_C - 1) + k, tt), :]
    yh, rc = _ln_fwd(yc)
    l = yh * clg + clb
    sl = jax.nn.sigmoid(l)
    return dict(bg=bg, cg=cg, xa=xa, q=q, ya=ya, zu=zu, zv=zv, u=u, vh=vh, rv=rv, vnb=vnb, s=s, yb=yb,
                a=a, sg=sg, yh=yh, rc=rc, l=l, sl=sl, yo=l * sl)


def _group_norm(f, gg):
    ya, yb, yo = f["ya"], f["yb"], f["yo"]
    ra, rb, ro = _rstd(ya), _rstd(yb), _rstd(yo)
    yn = jnp.concatenate([ya * ra * gg[:, 0:D_A], yb * rb * gg[:, D_A:D_A + D_B], yo * ro * gg[:, D_A + D_B:]], axis=1)
    return yn, (ra, rb, ro)


def _mixer_prm(refs):
    caw_ref, lng_ref, lnb_ref, wm_ref, bias_ref, ccw_ref, ccb_ref, clg_ref, clb_ref = refs
    wm = [wm_ref[h] for h in range(N_HEADS_B)]
    return (caw_ref[...], lng_ref[...], lnb_ref[...], wm, bias_ref[...], ccw_ref[...], ccb_ref[...], clg_ref[...],
            clb_ref[...])


def _mixer_param_specs():
    return [_const_spec((8, D_A)), _const_spec((1, D_B)), _const_spec((1, D_B)), _const_spec((N_HEADS_B, CHUNK, CHUNK)),
            _const_spec((CHUNK, D_B)), _const_spec((32, D_C)), _const_spec((1, D_C)), _const_spec((1, D_C)),
            _const_spec((1, D_C))]


def mixer_fwd(z, x, mp, grp_g, w_out, post_g, tt):
    t = z.shape[0]
    assert t % tt == 0 and tt % CHUNK == 0, (t, tt)

    def body(z_ref, x_ref, *rest):
        prm_refs, (gg_ref, wo_ref, pg_ref, o_ref, x1_ref, pa_ext, yg_ext) = rest[:9], rest[9:]
        i = pl.program_id(0)

        @pl.when(i == 0)
        def _():
            pa_ext[0:HALO_A, :] = jnp.zeros((HALO_A, D_A), F32)
            yg_ext[0:HALO_C, :] = jnp.zeros((HALO_C, D_C), F32)

        f = _mixer_forward(z_ref[...], _mixer_prm(prm_refs), pa_ext, yg_ext, tt)
        yn, _ = _group_norm(f, gg_ref[...])
        o = _dot(yn.astype(MM_DTYPE), wo_ref[...])
        o_ref[...] = o
        x1_ref[...] = x_ref[...] + o * _rstd(o) * pg_ref[...]
        pa_ext[0:HALO_A, :] = pa_ext[tt:tt + HALO_A, :]
        yg_ext[0:HALO_C, :] = yg_ext[tt:tt + HALO_C, :]

    row = lambda c: pl.BlockSpec((tt, c), lambda i: (i, 0))
    return pl.pallas_call(
        body, name="mixer_fwd", grid=(t // tt,),
        in_specs=[row(D_IN), row(D_MODEL)] + _mixer_param_specs()
        + [_const_spec((1, D_MODEL)), _const_spec((D_MODEL, D_MODEL)), _const_spec((1, D_MODEL))],
        out_specs=[row(D_MODEL), row(D_MODEL)],
        out_shape=[jax.ShapeDtypeStruct((t, D_MODEL), F32), jax.ShapeDtypeStruct((t, D_MODEL), F32)],
        scratch_shapes=[pltpu.VMEM((HALO_A + tt, D_A), F32), pltpu.VMEM((HALO_C + tt, D_C), F32)],
        compiler_params=_params(("arbitrary",)),
    )(z, x, *mp, grp_g, w_out, post_g)


def mixer_bwd(dx1, o, z, mp, wmt, grp_g, w_out, post_g, tt):
    t = z.shape[0]
    assert t % tt == 0 and tt % CHUNK == 0, (t, tt)
    steps = t // tt
    hb = tt // HALO_C

    def body(dx1_ref, o_ref, z_ref, zh_ref, *rest):
        prm_refs = rest[:9]
        (wmt_ref, gg_ref, wo_ref, pg_ref,
         dz_ref, do_ref, yn_ref, dpg_ref, dgg_ref, dcaw_ref, dlng_ref, dlnb_ref, dwm_ref, dbias_ref, dccw_ref, dccb_ref,
         dclg_ref, dclb_ref,
         pa_ext, yg_ext, dq_ext, dyc_ext, a_pg, a_gg, a_caw, a_lng, a_lnb, a_ccw, a_ccb, a_clg, a_clb) = rest[9:]
        i = pl.program_id(0)
        tile = steps - 1 - i
        prm = _mixer_prm(prm_refs)
        caw, lng, lnb, wm, bias_e, ccw, ccb, clg, clb = prm
        small = (a_pg, a_gg, a_caw, a_lng, a_lnb, a_ccw, a_ccb, a_clg, a_clb)

        @pl.when(i == 0)
        def _():
            for ref in small + (dwm_ref, dbias_ref):
                ref[...] = jnp.zeros_like(ref)
            dq_ext[tt:tt + HALO_A, :] = jnp.zeros((HALO_A, D_A), F32)
            dyc_ext[tt:tt + HALO_C, :] = jnp.zeros((HALO_C, D_C), F32)

        zh = zh_ref[...]
        inside = tile > 0
        pa_ext[0:HALO_A, :] = jnp.where(
            inside, zh[HALO_C - HALO_A:, D_A:2 * D_A] * zh[HALO_C - HALO_A:, 2 * D_A:3 * D_A], 0.0)
        o_c = 3 * D_A + 2 * D_B
        yg_ext[0:HALO_C, :] = jnp.where(
            inside, zh[:, o_c:o_c + D_C] * jax.nn.sigmoid(zh[:, o_c + D_C:o_c + 2 * D_C]), 0.0)

        f = _mixer_forward(z_ref[...], prm, pa_ext, yg_ext, tt)
        gg = gg_ref[...]
        yn, (ra, rb, ro) = _group_norm(f, gg)
        yn_ref[...] = yn.astype(MM_DTYPE)

        ov = o_ref[...]
        dx1v = dx1_ref[...]
        r_o = _rstd(ov)
        pg = pg_ref[...]
        a_pg[...] += _rowsum8(dx1v * ov * r_o)
        do = _rms_bwd(ov, r_o, pg, dx1v).astype(MM_DTYPE)
        do_ref[...] = do
        dyn = _dot_nt(do, wo_ref[...])

        dyn_a, dyn_b, dyn_c = dyn[:, 0:D_A], dyn[:, D_A:D_A + D_B], dyn[:, D_A + D_B:]
        ga, gb, gc = gg[:, 0:D_A], gg[:, D_A:D_A + D_B], gg[:, D_A + D_B:]
        a_gg[...] += _rowsum8(jnp.concatenate([dyn_a * f["ya"] * ra, dyn_b * f["yb"] * rb, dyn_c * f["yo"] * ro], axis=1))
        dya = _rms_bwd(f["ya"], ra, ga, dyn_a)
        dyb = _rms_bwd(f["yb"], rb, gb, dyn_b)
        dyo = _rms_bwd(f["yo"], ro, gc, dyn_c)

        dbg = dya * f["q"]
        dq = dya * f["bg"]
        dq_ext[0:tt, :] = dq
        dp = jnp.zeros((tt, D_A), F32)
        for k in range(K_A):
            dp = dp + caw[k:k + 1, :] * dq_ext[pl.ds(K_A - 1 - k, tt), :]
            a_caw[k] += _rowsum8(dq * pa_ext[pl.ds(HALO_A - (K_A - 1) + k, tt), :])
        dcg = dp * f["xa"]
        dxa = dp * f["cg"]

        du = dyb * f["s"]
        ds = dyb * f["u"]
        dsb = ds.astype(MM_DTYPE)
        head = lax.broadcasted_iota(jnp.int32, (CHUNK, D_B), 1) // HEAD
        dvn_parts = []
        dbias = jnp.zeros((CHUNK, D_B), F32)
        for c in range(tt // CHUNK):
            rows = slice(c * CHUNK, (c + 1) * CHUNK)
            dsc = dsb[rows, :]
            vc = f["vnb"][rows, :]
            dbias = dbias + ds[rows, :]
            parts = []
            for h in range(N_HEADS_B):
                dwm_ref[h] += _dot_nt(jnp.where(head == h, dsc, jnp.zeros_like(dsc)), vc)
                parts.append(_dot(wmt_ref[h], dsc))
            dvn_parts.append(_head_select(parts))
        dbias_ref[...] += dbias
        dvn = jnp.concatenate(dvn_parts, axis=0) if len(dvn_parts) > 1 else dvn_parts[0]
        a_lng[...] += _rowsum8(dvn * f["vh"])
        a_lnb[...] += _rowsum8(dvn)
        dv = _ln_bwd(f["vh"], f["rv"], dvn * lng)
        dzu = du * _gelu_grad(f["zu"])
        dzv = dv * _gelu_grad(f["zv"])

        l, sl = f["l"], f["sl"]
        dl = dyo * (sl * (1.0 + l * (1.0 - sl)))
        a_clg[...] += _rowsum8(dl * f["yh"])
        a_clb[...] += _rowsum8(dl)
        dyc = _ln_bwd(f["yh"], f["rc"], dl * clg)
        a_ccb[...] += _rowsum8(dyc)
        dyc_ext[0:tt, :] = dyc
        dy = jnp.zeros((tt, D_C), F32)
        for k in range(K_C):
            dy = dy + ccw[k:k + 1, :] * dyc_ext[pl.ds(K_C - 1 - k, tt), :]
            a_ccw[k] += _rowsum8(dyc * yg_ext[pl.ds(HALO_C - (K_C - 1) + k, tt), :])
        da = dy * f["sg"]
        dg = dy * f["a"] * f["sg"] * (1.0 - f["sg"])

        dz_ref[...] = jnp.concatenate([dbg, dcg, dxa, dzu, dzv, da, dg], axis=1).astype(MM_DTYPE)

        dq_ext[tt:tt + HALO_A, :] = dq_ext[0:HALO_A, :]
        dyc_ext[tt:tt + HALO_C, :] = dyc_ext[0:HALO_C, :]

        @pl.when(i == steps - 1)
        def _():
            red = lambda ref: jnp.sum(ref[...], axis=0, keepdims=True)
            dpg_ref[...] = red(a_pg)
            dgg_ref[...] = red(a_gg)
            dlng_ref[...] = red(a_lng)
            dlnb_ref[...] = red(a_lnb)
            dccb_ref[...] = red(a_ccb)
            dclg_ref[...] = red(a_clg)
            dclb_ref[...] = red(a_clb)
            dcaw_ref[...] = jnp.sum(a_caw[...], axis=1)
            dccw_ref[...] = jnp.sum(a_ccw[...], axis=1)

    rev = lambda c: pl.BlockSpec((tt, c), lambda i: (steps - 1 - i, 0))
    halo = pl.BlockSpec((HALO_C, D_IN), lambda i: (jnp.maximum((steps - 1 - i) * hb - 1, 0), 0))
    full = lambda shape: pl.BlockSpec(shape, lambda i: (0,) * len(shape))
    sds = jax.ShapeDtypeStruct
    outs = pl.pallas_call(
        body, name="mixer_bwd", grid=(steps,),
        in_specs=[rev(D_MODEL), rev(D_MODEL), rev(D_IN), halo] + _mixer_param_specs()
        + [_const_spec((N_HEADS_B, CHUNK, CHUNK)), _const_spec((1, D_MODEL)), _const_spec((D_MODEL, D_MODEL)),
           _const_spec((1, D_MODEL))],
        out_specs=[rev(D_IN), rev(D_MODEL), rev(D_MODEL), full((1, D_MODEL)), full((1, D_MODEL)), full((8, D_A)),
                   full((1, D_B)), full((1, D_B)), full((N_HEADS_B, CHUNK, CHUNK)), full((CHUNK, D_B)), full((32, D_C)),
                   full((1, D_C)), full((1, D_C)), full((1, D_C))],
        out_shape=[sds((t, D_IN), MM_DTYPE), sds((t, D_MODEL), MM_DTYPE), sds((t, D_MODEL), MM_DTYPE),
                   sds((1, D_MODEL), F32), sds((1, D_MODEL), F32), sds((8, D_A), F32), sds((1, D_B), F32), sds((1, D_B), F32),
                   sds((N_HEADS_B, CHUNK, CHUNK), F32), sds((CHUNK, D_B), F32), sds((32, D_C), F32), sds((1, D_C), F32),
                   sds((1, D_C), F32), sds((1, D_C), F32)],
        scratch_shapes=[pltpu.VMEM((HALO_A + tt, D_A), F32), pltpu.VMEM((HALO_C + tt, D_C), F32),
                        pltpu.VMEM((tt + HALO_A, D_A), F32), pltpu.VMEM((tt + HALO_C, D_C), F32),
                        pltpu.VMEM((8, D_MODEL), F32), pltpu.VMEM((8, D_MODEL), F32), pltpu.VMEM((8, 8, D_A), F32),
                        pltpu.VMEM((8, D_B), F32), pltpu.VMEM((8, D_B), F32), pltpu.VMEM((32, 8, D_C), F32),
                        pltpu.VMEM((8, D_C), F32), pltpu.VMEM((8, D_C), F32), pltpu.VMEM((8, D_C), F32)],
        compiler_params=_params(("arbitrary",)),
    )(dx1, o, z, z, *mp, wmt, grp_g, w_out, post_g)
    return outs


def _ffn_conv(ext, cw, c0, cn, tt):
    acc = jnp.zeros((tt, cn), F32)
    for k in range(K_F):
        acc = acc + cw[k:k + 1, c0:c0 + cn] * ext[pl.ds(HALO_F - (K_F - 1) + k, tt), c0:c0 + cn]
    return acc


def ffn_fwd(up0, x1, cw, w_down, post_g, tt):
    t = up0.shape[0]
    assert t % tt == 0, (t, tt)
    cn = _col_chunk(D_FF)

    def body(up0_ref, x1_ref, cw_ref, wd_ref, pg_ref, d_ref, x2_ref, ext):
        i = pl.program_id(0)

        @pl.when(i == 0)
        def _():
            ext[0:HALO_F, :] = jnp.zeros((HALO_F, 2 * D_FF), F32)

        ext[HALO_F:HALO_F + tt, :] = up0_ref[...]
        cwv = cw_ref[...]
        d = jnp.zeros((tt, D_MODEL), F32)
        for c0 in range(0, D_FF, cn):
            gate = _ffn_conv(ext, cwv, c0, cn, tt)
            val = _ffn_conv(ext, cwv, D_FF + c0, cn, tt)
            act = (gate * jax.nn.sigmoid(gate) * val).astype(MM_DTYPE)
            d = d + _dot(act, wd_ref[c0:c0 + cn, :])
        d_ref[...] = d
        x2_ref[...] = x1_ref[...] + d * _rstd(d) * pg_ref[...]
        ext[0:HALO_F, :] = ext[tt:tt + HALO_F, :]

    row = lambda c: pl.BlockSpec((tt, c), lambda i: (i, 0))
    return pl.pallas_call(
        body, name="ffn_fwd", grid=(t // tt,),
        in_specs=[row(2 * D_FF), row(D_MODEL), _const_spec((8, 2 * D_FF)), _const_spec((D_FF, D_MODEL)),
                  _const_spec((1, D_MODEL))],
        out_specs=[row(D_MODEL), row(D_MODEL)],
        out_shape=[jax.ShapeDtypeStruct((t, D_MODEL), F32), jax.ShapeDtypeStruct((t, D_MODEL), F32)],
        scratch_shapes=[pltpu.VMEM((HALO_F + tt, 2 * D_FF), F32)],
        compiler_params=_params(("arbitrary",)),
    )(up0, x1, cw, w_down, post_g)


def ffn_bwd(dx2, d, up0, cw, w_down, post_g, tt):
    t = up0.shape[0]
    assert t % tt == 0, (t, tt)
    steps = t // tt
    hb = tt // HALO_F
    cn = _col_chunk(D_FF)

    def body(dx2_ref, d_ref, up0_ref, uh_ref, cw_ref, wd_ref, pg_ref,
             dd_ref, act_ref, dup0_ref, dpg_ref, dcw_ref, ext, dup_ext, a_pg, a_cw):
        i = pl.program_id(0)
        tile = steps - 1 - i

        @pl.when(i == 0)
        def _():
            a_pg[...] = jnp.zeros_like(a_pg)
            a_cw[...] = jnp.zeros_like(a_cw)
            dup_ext[tt:tt + HALO_F, :] = jnp.zeros((HALO_F, 2 * D_FF), F32)

        ext[0:HALO_F, :] = jnp.where(tile > 0, uh_ref[...], 0.0)
        ext[HALO_F:HALO_F + tt, :] = up0_ref[...]
        cwv = cw_ref[...]
        dv = d_ref[...]
        dx2v = dx2_ref[...]
        r = _rstd(dv)
        a_pg[...] += _rowsum8(dx2v * dv * r)
        dd = _rms_bwd(dv, r, pg_ref[...], dx2v).astype(MM_DTYPE)
        dd_ref[...] = dd
        for c0 in range(0, D_FF, cn):
            gate = _ffn_conv(ext, cwv, c0, cn, tt)
            val = _ffn_conv(ext, cwv, D_FF + c0, cn, tt)
            sg = jax.nn.sigmoid(gate)
            sl = gate * sg
            act_ref[:, c0:c0 + cn] = (sl * val).astype(MM_DTYPE)
            da = _dot_nt(dd, wd_ref[c0:c0 + cn, :])
            dup_ext[0:tt, c0:c0 + cn] = da * val * (sg * (1.0 + gate * (1.0 - sg)))
            dup_ext[0:tt, D_FF + c0:D_FF + c0 + cn] = da * sl
        for c0 in range(0, 2 * D_FF, cn):
            dup = dup_ext[0:tt, c0:c0 + cn]
            acc = jnp.zeros((tt, cn), F32)
            for k in range(K_F):
                acc = acc + cwv[k:k + 1, c0:c0 + cn] * dup_ext[pl.ds(K_F - 1 - k, tt), c0:c0 + cn]
                a_cw[k, :, c0:c0 + cn] += _rowsum8(dup * ext[pl.ds(HALO_F - (K_F - 1) + k, tt), c0:c0 + cn])
            dup0_ref[:, c0:c0 + cn] = acc.astype(MM_DTYPE)
        dup_ext[tt:tt + HALO_F, :] = dup_ext[0:HALO_F, :]

        @pl.when(i == steps - 1)
        def _():
            dpg_ref[...] = jnp.sum(a_pg[...], axis=0, keepdims=True)
            dcw_ref[...] = jnp.sum(a_cw[...], axis=1)

    rev = lambda c: pl.BlockSpec((tt, c), lambda i: (steps - 1 - i, 0))
    halo = pl.BlockSpec((HALO_F, 2 * D_FF), lambda i: (jnp.maximum((steps - 1 - i) * hb - 1, 0), 0))
    full = lambda shape: pl.BlockSpec(shape, lambda i: (0,) * len(shape))
    sds = jax.ShapeDtypeStruct
    return pl.pallas_call(
        body, name="ffn_bwd", grid=(steps,),
        in_specs=[rev(D_MODEL), rev(D_MODEL), rev(2 * D_FF), halo, _const_spec((8, 2 * D_FF)),
                  _const_spec((D_FF, D_MODEL)), _const_spec((1, D_MODEL))],
        out_specs=[rev(D_MODEL), rev(D_FF), rev(2 * D_FF), full((1, D_MODEL)), full((8, 2 * D_FF))],
        out_shape=[sds((t, D_MODEL), MM_DTYPE), sds((t, D_FF), MM_DTYPE), sds((t, 2 * D_FF), MM_DTYPE),
                   sds((1, D_MODEL), F32), sds((8, 2 * D_FF), F32)],
        scratch_shapes=[pltpu.VMEM((HALO_F + tt, 2 * D_FF), F32), pltpu.VMEM((tt + HALO_F, 2 * D_FF), F32),
                        pltpu.VMEM((8, D_MODEL), F32), pltpu.VMEM((8, 8, 2 * D_FF), F32)],
        compiler_params=_params(("arbitrary",)),
    )(dx2, d, up0, up0, cw, w_down, post_g)


def loss_head(y, target, tm):
    t, d = y.shape
    assert t % tm == 0, (t, tm)
    steps = t // tm

    def body(y_ref, t_ref, dy_ref, loss_ref, acc):
        i = pl.program_id(0)

        @pl.when(i == 0)
        def _():
            acc[...] = jnp.zeros_like(acc)

        diff = y_ref[...] - t_ref[...]
        dy_ref[...] = diff * (1.0 / d)
        acc[...] += _rowsum8(diff * diff)

        @pl.when(i == steps - 1)
        def _():
            loss_ref[...] = (0.5 / d) * jnp.sum(jnp.sum(acc[...], axis=0, keepdims=True), axis=1, keepdims=True)

    row = pl.BlockSpec((tm, d), lambda i: (i, 0))
    return pl.pallas_call(
        body, name="loss_head", grid=(steps,), in_specs=[row, row],
        out_specs=[row, pl.BlockSpec((1, 1), lambda i: (0, 0))],
        out_shape=[jax.ShapeDtypeStruct((t, d), F32), jax.ShapeDtypeStruct((1, 1), F32)],
        scratch_shapes=[pltpu.VMEM((8, d), F32)],
        compiler_params=_params(("arbitrary",)),
    )(y, target)


def adamw(w, g, m, v):
    shape = w.shape
    cols = shape[-1]
    rows = w.size // cols
    tr = next((r for r in (512, 256, 128) if rows % r == 0 and rows > r), rows)
    c1 = 1.0 - ADAM_B1 ** ADAM_STEP
    c2 = 1.0 - ADAM_B2 ** ADAM_STEP

    def body(w_ref, g_ref, m_ref, v_ref, d_ref, nm_ref, nv_ref):
        gv = g_ref[...]
        nm = ADAM_B1 * m_ref[...] + (1.0 - ADAM_B1) * gv
        nv = ADAM_B2 * v_ref[...] + (1.0 - ADAM_B2) * (gv * gv)
        nm_ref[...] = nm
        nv_ref[...] = nv
        d_ref[...] = -ADAM_LR * ((nm / c1) / (jnp.sqrt(nv / c2) + ADAM_EPS) + ADAM_WD * w_ref[...])

    spec = pl.BlockSpec((tr, cols), lambda i: (i, 0))
    out = jax.ShapeDtypeStruct((rows, cols), F32)
    res = pl.pallas_call(
        body, name="adamw", grid=(rows // tr,), in_specs=[spec] * 4, out_specs=[spec] * 3, out_shape=[out] * 3,
        compiler_params=_params(("arbitrary",)),
    )(*[a.reshape(rows, cols) for a in (w, g, m, v)])
    return tuple(r.reshape(shape) for r in res)


def _place():
    return lax.axis_index("x"), lax.axis_index("y"), lax.axis_index("c")


def _other_chips(x, y):
    return [(1 - x, y, 2 * (1 - x) + y), (x, 1 - y, 2 * x + 1 - y), (1 - x, 1 - y, 2 * (1 - x) + 1 - y)]


def gather_weights(big, small):
    rows = big.shape[0]
    half = rows // 2

    def body(big_ref, small_ref, bigs_ref, smalls_ref, send, recv, fsend, frecv, ssend, srecv, local):
        x, y, c = _place()
        me = 2 * x + y
        chips = _other_chips(x, y)
        mine = pl.ds(pl.multiple_of(c * half, 8), half)
        theirs = pl.ds(pl.multiple_of((1 - c) * half, 8), half)

        own_big = pltpu.make_async_copy(big_ref, bigs_ref.at[me], local.at[0])
        own_small = pltpu.make_async_copy(small_ref, smalls_ref.at[me], local.at[1])
        own_big.start()
        own_small.start()

        def to_chip(k, px, py):
            return pltpu.make_async_remote_copy(
                src_ref=big_ref.at[mine, :], dst_ref=bigs_ref.at[me, mine, :], send_sem=send.at[k], recv_sem=recv.at[k],
                device_id=(px, py, c), device_id_type=MESH_ID)

        def small_to_chip(k, px, py):
            return pltpu.make_async_remote_copy(
                src_ref=small_ref, dst_ref=smalls_ref.at[me], send_sem=ssend.at[k], recv_sem=srecv.at[k],
                device_id=(px, py, c), device_id_type=MESH_ID)

        def from_chip(k, pj):
            return pltpu.make_async_remote_copy(
                src_ref=big_ref.at[mine, :], dst_ref=bigs_ref.at[pj, mine, :], send_sem=send.at[k], recv_sem=recv.at[k],
                device_id=(x, y, c), device_id_type=MESH_ID)

        def small_from_chip(k, pj):
            return pltpu.make_async_remote_copy(
                src_ref=small_ref, dst_ref=smalls_ref.at[pj], send_sem=ssend.at[k], recv_sem=srecv.at[k],
                device_id=(x, y, c), device_id_type=MESH_ID)

        def pass_on(k, pj):
            return pltpu.make_async_remote_copy(
                src_ref=bigs_ref.at[pj, mine, :], dst_ref=bigs_ref.at[pj, mine, :], send_sem=fsend.at[k],
                recv_sem=frecv.at[k], device_id=(x, y, 1 - c), device_id_type=MESH_ID)

        def passed(k, pj):
            return pltpu.make_async_remote_copy(
                src_ref=bigs_ref.at[pj, theirs, :], dst_ref=bigs_ref.at[pj, theirs, :], send_sem=fsend.at[k],
                recv_sem=frecv.at[k], device_id=(x, y, 1 - c), device_id_type=MESH_ID)

        sends = []
        for k, (px, py, _) in enumerate(chips):
            sends.append(to_chip(k, px, py))
            sends.append(small_to_chip(k, px, py))
        for cp in sends:
            cp.start()
        forwards = []
        for k, (_, _, pj) in enumerate(chips):
            from_chip(k, pj).wait_recv()
            fw = pass_on(k, pj)
            fw.start()
            forwards.append(fw)
        for k, (_, _, pj) in enumerate(chips):
            small_from_chip(k, pj).wait_recv()
            passed(k, pj).wait_recv()
        for cp in sends + forwards:
            cp.wait_send()
        own_big.wait()
        own_small.wait()

    any_spec = pl.BlockSpec(memory_space=pl.ANY)
    dma3 = pltpu.SemaphoreType.DMA((3,))
    return pl.pallas_call(
        body, name="gather_weights", in_specs=[any_spec, any_spec], out_specs=[any_spec, any_spec],
        out_shape=[jax.ShapeDtypeStruct((N_CHIPS,) + big.shape, big.dtype),
                   jax.ShapeDtypeStruct((N_CHIPS,) + small.shape, small.dtype)],
        scratch_shapes=[dma3, dma3, dma3, dma3, dma3, dma3, pltpu.SemaphoreType.DMA((2,))],
        compiler_params=pltpu.CompilerParams(has_side_effects=True),
    )(big, small)


def swap_halves(g):
    n, rows, lanes = g.shape
    half = rows // 2

    def body(g_ref, got_ref, send, recv):
        x, y, c = _place()
        theirs = pl.ds(pl.multiple_of((1 - c) * half, 8), half)
        cp = pltpu.make_async_remote_copy(
            src_ref=g_ref.at[:, theirs, :], dst_ref=got_ref, send_sem=send.at[0], recv_sem=recv.at[0],
            device_id=(x, y, 1 - c), device_id_type=MESH_ID)
        cp.start()
        cp.wait()

    any_spec = pl.BlockSpec(memory_space=pl.ANY)
    return pl.pallas_call(
        body, name="swap_halves", in_specs=[any_spec], out_specs=any_spec,
        out_shape=jax.ShapeDtypeStruct((n, half, lanes), g.dtype),
        scratch_shapes=[pltpu.SemaphoreType.DMA((1,)), pltpu.SemaphoreType.DMA((1,))],
        compiler_params=pltpu.CompilerParams(has_side_effects=True),
    )(g)


def add_halves(g, got, tr):
    n, rows, lanes = g.shape
    half = rows // 2
    assert half % tr == 0, (half, tr)
    hb = half // tr
    c = lax.axis_index("c")

    def body(c_ref, g_ref, got_ref, s_ref, sb_ref):
        s = g_ref[...] + got_ref[...]
        s_ref[...] = s
        sb_ref[...] = s.astype(BF16)

    grid_spec = pltpu.PrefetchScalarGridSpec(
        num_scalar_prefetch=1, grid=(n, hb),
        in_specs=[pl.BlockSpec((1, tr, lanes), lambda j, i, c_ref: (j, c_ref[0] * hb + i, 0)),
                  pl.BlockSpec((1, tr, lanes), lambda j, i, c_ref: (j, i, 0))],
        out_specs=[pl.BlockSpec((1, tr, lanes), lambda j, i, c_ref: (j, i, 0))] * 2)
    return pl.pallas_call(
        body, name="add_halves", grid_spec=grid_spec,
        out_shape=[jax.ShapeDtypeStruct((n, half, lanes), F32), jax.ShapeDtypeStruct((n, half, lanes), BF16)],
        compiler_params=_params(("arbitrary", "arbitrary")),
    )(jnp.reshape(c, (1,)).astype(jnp.int32), g, got)


def scatter_to_chips(sb):
    def body(sb_ref, got_ref, send, recv, local):
        x, y, c = _place()
        me = 2 * x + y
        chips = _other_chips(x, y)
        own = pltpu.make_async_copy(sb_ref.at[me], got_ref.at[me], local.at[0])
        own.start()
        sends = []
        for k, (px, py, pj) in enumerate(chips):
            cp = pltpu.make_async_remote_copy(
                src_ref=sb_ref.at[pj], dst_ref=got_ref.at[me], send_sem=send.at[k], recv_sem=recv.at[k],
                device_id=(px, py, c), device_id_type=MESH_ID)
            cp.start()
            sends.append(cp)
        for k, (_, _, pj) in enumerate(chips):
            pltpu.make_async_remote_copy(
                src_ref=sb_ref.at[pj], dst_ref=got_ref.at[pj], send_sem=send.at[k], recv_sem=recv.at[k],
                device_id=(x, y, c), device_id_type=MESH_ID).wait_recv()
        for cp in sends:
            cp.wait_send()
        own.wait()

    any_spec = pl.BlockSpec(memory_space=pl.ANY)
    return pl.pallas_call(
        body, name="scatter_to_chips", in_specs=[any_spec], out_specs=any_spec,
        out_shape=jax.ShapeDtypeStruct(sb.shape, sb.dtype),
        scratch_shapes=[pltpu.SemaphoreType.DMA((3,)), pltpu.SemaphoreType.DMA((3,)), pltpu.SemaphoreType.DMA((1,))],
        compiler_params=pltpu.CompilerParams(has_side_effects=True),
    )(sb)


def add_chips(s, got, tr):
    n, half, lanes = s.shape
    assert half % tr == 0, (half, tr)
    me = 2 * lax.axis_index("x") + lax.axis_index("y")

    def body(me_ref, s_ref, g0_ref, g1_ref, g2_ref, g3_ref, o_ref):
        mine = me_ref[0]
        acc = jnp.zeros((tr, lanes), F32)
        for j, ref in enumerate((g0_ref, g1_ref, g2_ref, g3_ref)):
            acc = acc + jnp.where(mine == j, s_ref[0], ref[0].astype(F32))
        o_ref[...] = acc

    def other(j):
        return pl.BlockSpec((1, tr, lanes), lambda i, me_ref: (j, i, 0))

    grid_spec = pltpu.PrefetchScalarGridSpec(
        num_scalar_prefetch=1, grid=(half // tr,),
        in_specs=[pl.BlockSpec((1, tr, lanes), lambda i, me_ref: (me_ref[0], i, 0))] + [other(j) for j in range(n)],
        out_specs=pl.BlockSpec((tr, lanes), lambda i, me_ref: (i, 0)))
    return pl.pallas_call(
        body, name="add_chips", grid_spec=grid_spec, out_shape=jax.ShapeDtypeStruct((half, lanes), F32),
        compiler_params=_params(("arbitrary",)),
    )(jnp.reshape(me, (1,)).astype(jnp.int32), s, got, got, got, got)


def join_halves(f):
    half, lanes = f.shape

    def body(f_ref, o_ref, send, recv, local):
        x, y, c = _place()
        mine = pl.ds(pl.multiple_of(c * half, 8), half)
        own = pltpu.make_async_copy(f_ref, o_ref.at[mine, :], local.at[0])
        own.start()
        cp = pltpu.make_async_remote_copy(
            src_ref=f_ref, dst_ref=o_ref.at[mine, :], send_sem=send.at[0], recv_sem=recv.at[0],
            device_id=(x, y, 1 - c), device_id_type=MESH_ID)
        cp.start()
        theirs = pl.ds(pl.multiple_of((1 - c) * half, 8), half)
        pltpu.make_async_remote_copy(
            src_ref=f_ref, dst_ref=o_ref.at[theirs, :], send_sem=send.at[0], recv_sem=recv.at[0],
            device_id=(x, y, 1 - c), device_id_type=MESH_ID).wait_recv()
        cp.wait_send()
        own.wait()

    any_spec = pl.BlockSpec(memory_space=pl.ANY)
    dma1 = pltpu.SemaphoreType.DMA((1,))
    return pl.pallas_call(
        body, name="join_halves", in_specs=[any_spec], out_specs=any_spec,
        out_shape=jax.ShapeDtypeStruct((2 * half, lanes), f.dtype), scratch_shapes=[dma1, dma1, dma1],
        compiler_params=pltpu.CompilerParams(has_side_effects=True),
    )(f)


def gather_all(v):
    def body(v_ref, o_ref, send, recv, local):
        x, y, c = _place()
        me = 4 * x + 2 * y + c
        own = pltpu.make_async_copy(v_ref, o_ref.at[me], local.at[0])
        own.start()
        sends = []
        k = 0
        for fx in (0, 1):
            for fy in (0, 1):
                for fc in (0, 1):
                    if fx or fy or fc:
                        cp = pltpu.make_async_remote_copy(
                            src_ref=v_ref, dst_ref=o_ref.at[me], send_sem=send.at[k], recv_sem=recv.at[k],
                            device_id=(x ^ fx, y ^ fy, c ^ fc), device_id_type=MESH_ID)
                        cp.start()
                        sends.append((cp, k, 4 * (x ^ fx) + 2 * (y ^ fy) + (c ^ fc)))
                        k += 1
        for cp, k, peer in sends:
            pltpu.make_async_remote_copy(
                src_ref=v_ref, dst_ref=o_ref.at[peer], send_sem=send.at[k], recv_sem=recv.at[k],
                device_id=(x, y, c), device_id_type=MESH_ID).wait_recv()
        for cp, _, _ in sends:
            cp.wait_send()
        own.wait()

    any_spec = pl.BlockSpec(memory_space=pl.ANY)
    return pl.pallas_call(
        body, name="gather_all", in_specs=[any_spec], out_specs=any_spec,
        out_shape=jax.ShapeDtypeStruct((8,) + v.shape, v.dtype),
        scratch_shapes=[pltpu.SemaphoreType.DMA((7,)), pltpu.SemaphoreType.DMA((7,)), pltpu.SemaphoreType.DMA((1,))],
        compiler_params=pltpu.CompilerParams(has_side_effects=True),
    )(v)


def sum_devices(v8, tr):
    n, rows, lanes = v8.shape
    assert rows % tr == 0, (rows, tr)

    def body(v_ref, o_ref):
        acc = v_ref[0]
        for j in range(1, n):
            acc = acc + v_ref[j]
        o_ref[...] = acc

    return pl.pallas_call(
        body, name="sum_devices", grid=(rows // tr,),
        in_specs=[pl.BlockSpec((n, tr, lanes), lambda i: (0, i, 0))], out_specs=pl.BlockSpec((tr, lanes), lambda i: (i, 0)),
        out_shape=jax.ShapeDtypeStruct((rows, lanes), F32), compiler_params=_params(("arbitrary",)),
    )(v8)


def _pack(arrays, rows):
    flat = jnp.concatenate([a.reshape(-1) for a in arrays])
    return jnp.pad(flat, (0, rows * LANES - flat.size)).reshape(rows, LANES)


def _unpack(buf, shapes):
    flat = buf.reshape(-1)
    out, at = [], 0
    for s in shapes:
        n = math.prod(s)
        out.append(flat[at:at + n].reshape(s))
        at += n
    return out


BIG_SHARDS = [(DEPTH, D_MODEL, D_IN // N_CHIPS), (DEPTH, D_MODEL // N_CHIPS, D_MODEL), (DEPTH, D_MODEL, 2 * D_FF // N_CHIPS),
              (DEPTH, D_FF // N_CHIPS, D_MODEL)]
BIG_ROWS = sum(math.prod(s) for s in BIG_SHARDS) // LANES
CONV_SHARDS = [(DEPTH, K_A, D_A // N_CHIPS), (DEPTH, K_C, D_C // N_CHIPS), (DEPTH, K_F, 2 * D_FF // N_CHIPS)]
CONV_ROWS = 32
SMALL_ROWS = 576


def _join_cols(g):
    n, l, r, c = g.shape
    return jnp.transpose(g, (1, 2, 0, 3)).reshape(l, r, n * c)


def _join_rows(g):
    n, l, r, c = g.shape
    return jnp.transpose(g, (1, 0, 2, 3)).reshape(l, n * r, c)


def _split_cols(w):
    l, r, c = w.shape
    return jnp.transpose(w.reshape(l, r, N_CHIPS, c // N_CHIPS), (2, 0, 1, 3))


def _split_rows(w):
    l, r, c = w.shape
    return jnp.transpose(w.reshape(l, N_CHIPS, r // N_CHIPS, c), (1, 0, 2, 3))


TILE_MM = 512
TILE_EW = 256
TILE_SUM = 256


def _pad_rows(a, rows):
    return jnp.pad(a, ((0, rows - a.shape[0]), (0, 0)))


def _layer_params(wl):
    tril = jnp.tril(jnp.ones((CHUNK, CHUNK), bool))
    wm = jnp.where(tril[None], wl["sgu_w"], 0.0)
    bias_e = jnp.repeat(wl["sgu_b"].T, HEAD, axis=1)
    row = lambda a: a.reshape(1, -1)
    mp = (_pad_rows(wl["conv_a_w"], 8), row(wl["sgu_ln_g"]), row(wl["sgu_ln_b"]), wm.astype(MM_DTYPE), bias_e,
          _pad_rows(wl["conv_c_w"], 32), row(wl["conv_c_b"]), row(wl["conv_ln_g"]), row(wl["conv_ln_b"]))
    wmt = jnp.transpose(wm, (0, 2, 1)).astype(MM_DTYPE)
    return mp, wmt, tril


def layer_fwd(x, wl, tm=TILE_MM, tt=TILE_EW):
    row = lambda a: a.reshape(1, -1)
    mp, _, _ = _layer_params(wl)
    z, h = norm_matmul(x, row(wl["pre_mix_g"]), wl["w_in"], tm)
    o, x1 = mixer_fwd(z, x, mp, row(wl["grp_norm_g"]), wl["w_out"], row(wl["post_mix_g"]), tt)
    up0, h2 = norm_matmul(x1, row(wl["pre_ffn_g"]), wl["w_up"], tt)
    d, x2 = ffn_fwd(up0, x1, _pad_rows(wl["ffn_conv_w"], 8), wl["w_down"], row(wl["post_ffn_g"]), tt)
    return x2, dict(x=x, z=z, h=h, o=o, x1=x1, up0=up0, h2=h2, d=d)


def layer_bwd(dx2, wl, sv, tm=TILE_MM, tt=TILE_EW):
    row = lambda a: a.reshape(1, -1)
    mp, wmt, tril = _layer_params(wl)
    t = dx2.shape[0]
    tk = min(512, t)
    g = {}
    dd, act, dup0, dpg, dcw = ffn_bwd(dx2, sv["d"], sv["up0"], _pad_rows(wl["ffn_conv_w"], 8), wl["w_down"],
                                      row(wl["post_ffn_g"]), tt)
    g["post_ffn_g"] = dpg[0]
    g["ffn_conv_w"] = dcw[:K_F]
    g["w_down"] = matmul_tn(act, dd, D_FF // 2, D_MODEL, tk)
    g["w_up"] = matmul_tn(sv["h2"], dup0, D_MODEL, 2 * D_FF // 4, tk)
    dx1, dg = matmul_nt_norm_bwd(dup0, wl["w_up"], sv["x1"], row(wl["pre_ffn_g"]), dx2, tm)
    g["pre_ffn_g"] = dg[0]
    (dz, do, yn, dpg, dgg, dcaw, dlng, dlnb, dwm, dbias, dccw, dccb, dclg, dclb) = mixer_bwd(
        dx1, sv["o"], sv["z"], mp, wmt, row(wl["grp_norm_g"]), wl["w_out"], row(wl["post_mix_g"]), tt)
    g["post_mix_g"] = dpg[0]
    g["grp_norm_g"] = dgg[0]
    g["conv_a_w"] = dcaw[:K_A]
    g["sgu_ln_g"] = dlng[0]
    g["sgu_ln_b"] = dlnb[0]
    g["sgu_w"] = jnp.where(tril[None], dwm, 0.0)
    g["sgu_b"] = jnp.sum(dbias.reshape(CHUNK, N_HEADS_B, HEAD), axis=2).T
    g["conv_c_w"] = dccw[:K_C]
    g["conv_c_b"] = dccb[0]
    g["conv_ln_g"] = dclg[0]
    g["conv_ln_b"] = dclb[0]
    g["w_out"] = matmul_tn(yn, do, D_MODEL, D_MODEL, tk)
    g["w_in"] = matmul_tn(sv["h"], dz, D_MODEL, D_IN // 3, tk)
    dx, dg = matmul_nt_norm_bwd(dz, wl["w_in"], sv["x"], row(wl["pre_mix_g"]), dx1, tm)
    g["pre_mix_g"] = dg[0]
    return dx, g


BIG = ["w_in", "w_out", "w_up", "w_down"]
CONV = ["conv_a_w", "conv_c_w", "ffn_conv_w"]
REPL = ["pre_mix_g", "sgu_ln_g", "sgu_ln_b", "sgu_w", "sgu_b", "conv_c_b", "conv_ln_g", "conv_ln_b", "grp_norm_g",
        "post_mix_g", "pre_ffn_g", "post_ffn_g"]
WEIGHTS = ["pre_mix_g", "w_in", "conv_a_w", "sgu_ln_g", "sgu_ln_b", "sgu_w", "sgu_b", "conv_c_w", "conv_c_b", "conv_ln_g",
           "conv_ln_b", "grp_norm_g", "w_out", "post_mix_g", "pre_ffn_g", "w_up", "ffn_conv_w", "w_down", "post_ffn_g"]


def kernel(x, pre_mix_g, w_in, conv_a_w, sgu_ln_g, sgu_ln_b, sgu_w, sgu_b, conv_c_w, conv_c_b, conv_ln_g, conv_ln_b, grp_norm_g, w_out, post_mix_g, pre_ffn_g, w_up, ffn_conv_w, w_down, post_ffn_g, loss_target, m_pre_mix_g, m_w_in, m_conv_a_w, m_sgu_ln_g, m_sgu_ln_b, m_sgu_w, m_sgu_b, m_conv_c_w, m_conv_c_b, m_conv_ln_g, m_conv_ln_b, m_grp_norm_g, m_w_out, m_post_mix_g, m_pre_ffn_g, m_w_up, m_ffn_conv_w, m_w_down, m_post_ffn_g, v_pre_mix_g, v_w_in, v_conv_a_w, v_sgu_ln_g, v_sgu_ln_b, v_sgu_w, v_sgu_b, v_conv_c_w, v_conv_c_b, v_conv_ln_g, v_conv_ln_b, v_grp_norm_g, v_w_out, v_post_mix_g, v_pre_ffn_g, v_w_up, v_ffn_conv_w, v_w_down, v_post_ffn_g):
    w = dict(pre_mix_g=pre_mix_g, w_in=w_in, conv_a_w=conv_a_w, sgu_ln_g=sgu_ln_g, sgu_ln_b=sgu_ln_b, sgu_w=sgu_w, sgu_b=sgu_b,
             conv_c_w=conv_c_w, conv_c_b=conv_c_b, conv_ln_g=conv_ln_g, conv_ln_b=conv_ln_b, grp_norm_g=grp_norm_g,
             w_out=w_out, post_mix_g=post_mix_g, pre_ffn_g=pre_ffn_g, w_up=w_up, ffn_conv_w=ffn_conv_w, w_down=w_down,
             post_ffn_g=post_ffn_g)
    m = dict(pre_mix_g=m_pre_mix_g, w_in=m_w_in, conv_a_w=m_conv_a_w, sgu_ln_g=m_sgu_ln_g, sgu_ln_b=m_sgu_ln_b,
             sgu_w=m_sgu_w, sgu_b=m_sgu_b, conv_c_w=m_conv_c_w, conv_c_b=m_conv_c_b, conv_ln_g=m_conv_ln_g,
             conv_ln_b=m_conv_ln_b, grp_norm_g=m_grp_norm_g, w_out=m_w_out, post_mix_g=m_post_mix_g,
             pre_ffn_g=m_pre_ffn_g, w_up=m_w_up, ffn_conv_w=m_ffn_conv_w, w_down=m_w_down, post_ffn_g=m_post_ffn_g)
    v = dict(pre_mix_g=v_pre_mix_g, w_in=v_w_in, conv_a_w=v_conv_a_w, sgu_ln_g=v_sgu_ln_g, sgu_ln_b=v_sgu_ln_b,
             sgu_w=v_sgu_w, sgu_b=v_sgu_b, conv_c_w=v_conv_c_w, conv_c_b=v_conv_c_b, conv_ln_g=v_conv_ln_g,
             conv_ln_b=v_conv_ln_b, grp_norm_g=v_grp_norm_g, w_out=v_w_out, post_mix_g=v_post_mix_g,
             pre_ffn_g=v_pre_ffn_g, w_up=v_w_up, ffn_conv_w=v_ffn_conv_w, w_down=v_w_down, post_ffn_g=v_post_ffn_g)
    chip = 2 * lax.axis_index("x") + lax.axis_index("y")

    big = _pack([w[n].astype(MM_DTYPE) for n in BIG], BIG_ROWS)
    conv = _pack([w[n] for n in CONV], CONV_ROWS)
    bigs, convs = gather_weights(big, conv)
    parts = [_unpack(bigs[j], BIG_SHARDS) for j in range(N_CHIPS)]
    cparts = [_unpack(convs[j], CONV_SHARDS) for j in range(N_CHIPS)]
    stack = lambda ps, i: jnp.stack([p[i] for p in ps])
    full = dict(w)
    full["w_in"] = _join_cols(stack(parts, 0))
    full["w_out"] = _join_rows(stack(parts, 1))
    full["w_up"] = _join_cols(stack(parts, 2))
    full["w_down"] = _join_rows(stack(parts, 3))
    for i, n in enumerate(CONV):
        full[n] = _join_cols(stack(cparts, i))

    def fwd(xc, wl):
        return layer_fwd(xc, wl)

    y, saved = lax.scan(fwd, x[0], full)
    dy, loss_part = loss_head(y, loss_target[0], TILE_MM)
    loss = lax.psum(loss_part[0, 0], ("x", "y", "c"))

    def bwd(dxc, ws):
        return layer_bwd(dxc, ws[0], ws[1])

    dx, grads = lax.scan(bwd, dy, (full, saved), reverse=True)

    gp = jnp.stack([_pack([s[j] for s in (_split_cols(grads["w_in"]), _split_rows(grads["w_out"]),
                                          _split_cols(grads["w_up"]), _split_rows(grads["w_down"]))], BIG_ROWS)
                    for j in range(N_CHIPS)])
    s32, s16 = add_halves(gp, swap_halves(gp), TILE_SUM)
    gshard = join_halves(add_chips(s32, scatter_to_chips(s16), TILE_SUM))
    out_g = dict(zip(BIG, _unpack(gshard, BIG_SHARDS)))

    small = sum_devices(gather_all(_pack([grads[n] for n in REPL + CONV], SMALL_ROWS)), 192)
    shapes = [grads[n].shape for n in REPL + CONV]
    for n, gfull in zip(REPL + CONV, _unpack(small, shapes)):
        if n in CONV:
            width = gfull.shape[-1] // N_CHIPS
            gfull = lax.dynamic_slice_in_dim(gfull, chip * width, width, axis=2)
        out_g[n] = gfull

    deltas, new_m, new_v = {}, {}, {}
    for n in WEIGHTS:
        deltas[n], new_m[n], new_v[n] = adamw(w[n], out_g[n], m[n], v[n])
    return (loss, dx[None], *[out_g[n] for n in WEIGHTS], *[deltas[n] for n in WEIGHTS], *[new_m[n] for n in WEIGHTS],
            *[new_v[n] for n in WEIGHTS])
```

```python
import functools
import math

import jax
import jax.numpy as jnp
from jax import lax
from jax.experimental import pallas as pl
from jax.experimental.pallas import tpu as pltpu

F32 = jnp.float32
BF16 = jnp.bfloat16
MM_DTYPE = BF16

D_MODEL = 1024
SEQ = 4096
DEPTH = 4
D_A = 256
D_B = 384
D_C = 384
D_IN = 3 * D_A + 2 * D_B + 2 * D_C
D_FF = 2816
K_A = 3
K_C = 31
K_F = 3
CHUNK = 128
HEAD = 64
N_HEADS_B = D_B // HEAD
EPS = 1e-6
N_CHIPS = 4

ADAM_LR = 0.001
ADAM_B1 = 0.9
ADAM_B2 = 0.999
ADAM_EPS = 1e-08
ADAM_WD = 0.01
ADAM_STEP = 10

HALO_A = 8
HALO_C = 32
HALO_F = 8
LANES = 1024
VMEM_LIMIT = 56 * 1024 * 1024

MESH_ID = pl.DeviceIdType.MESH


def _params(sem=None):
    return pltpu.CompilerParams(dimension_semantics=sem, vmem_limit_bytes=VMEM_LIMIT)


def _const_spec(shape):
    nd = len(shape)
    return pl.BlockSpec(shape, lambda *_: (0,) * nd, pipeline_mode=pl.Buffered(1))


def _rowsum8(a):
    r, c = a.shape
    return jnp.sum(a.reshape(r // 8, 8, c), axis=0)


def _rstd(x):
    return lax.rsqrt(jnp.mean(x * x, axis=-1, keepdims=True) + EPS)


def _rms_bwd(x, r, g, dy):
    gdy = g * dy
    return r * gdy - x * (r * r * r) * jnp.mean(gdy * x, axis=-1, keepdims=True)


def _ln_fwd(x):
    mu = jnp.mean(x, axis=-1, keepdims=True)
    xc = x - mu
    r = lax.rsqrt(jnp.mean(xc * xc, axis=-1, keepdims=True) + EPS)
    return xc * r, r


def _ln_bwd(xh, r, dxh):
    return r * (dxh - jnp.mean(dxh, axis=-1, keepdims=True) - xh * jnp.mean(dxh * xh, axis=-1, keepdims=True))


def _gelu(x):
    return 0.5 * x * (1.0 + lax.erf(x * (1.0 / math.sqrt(2.0))))


def _gelu_grad(x):
    cdf = 0.5 * (1.0 + lax.erf(x * (1.0 / math.sqrt(2.0))))
    pdf = jnp.exp(-0.5 * x * x) * (1.0 / math.sqrt(2.0 * math.pi))
    return cdf + x * pdf


def _dot(a, b):
    return jnp.dot(a, b, preferred_element_type=F32)


def _dot_nt(a, b):
    return lax.dot_general(a, b, (((1,), (1,)), ((), ())), preferred_element_type=F32)


def _dot_tn(a, b):
    return lax.dot_general(a, b, (((0,), (0,)), ((), ())), preferred_element_type=F32)


def _col_chunk(n):
    for c in (1408, 1024, 768, 512, 256, 128):
        if n % c == 0:
            return c
    raise ValueError(n)


def _layer_spec(wg, layer):
    _, _, r, c = wg.shape
    return pl.BlockSpec((N_CHIPS, None, r, c), lambda *_: (0, layer, 0, 0), pipeline_mode=pl.Buffered(1))


def _join_col_blocks(w_ref, w_scr):
    c = w_ref.shape[2]
    for j in range(N_CHIPS):
        w_scr[:, c * j:c * (j + 1)] = w_ref[j]


def norm_matmul(x, g, wg, layer, tm):
    t, d = x.shape
    assert t % tm == 0, (t, tm)
    cw = wg.shape[3]
    n = N_CHIPS * cw
    aligned = cw % 128 == 0
    cn = cw if aligned else _col_chunk(n)

    def body(x_ref, g_ref, w_ref, o_ref, h_ref, *scr):
        if not aligned:
            @pl.when(pl.program_id(0) == 0)
            def _():
                _join_col_blocks(w_ref, scr[0])

        xv = x_ref[...]
        h = (xv * _rstd(xv) * g_ref[...]).astype(MM_DTYPE)
        h_ref[...] = h
        for j, c0 in enumerate(range(0, n, cn)):
            wv = w_ref[j] if aligned else scr[0][:, c0:c0 + cn]
            o_ref[:, c0:c0 + cn] = _dot(h, wv)

    return pl.pallas_call(
        body, name="norm_matmul", grid=(t // tm,),
        in_specs=[pl.BlockSpec((tm, d), lambda i: (i, 0)), _const_spec((1, d)), _layer_spec(wg, layer)],
        out_specs=[pl.BlockSpec((tm, n), lambda i: (i, 0)), pl.BlockSpec((tm, d), lambda i: (i, 0))],
        out_shape=[jax.ShapeDtypeStruct((t, n), F32), jax.ShapeDtypeStruct((t, d), MM_DTYPE)],
        scratch_shapes=[] if aligned else [pltpu.VMEM((d, n), MM_DTYPE)],
        compiler_params=_params(("arbitrary",)),
    )(x, g, wg)


def matmul_nt_norm_bwd(gy, wg, layer, x, g, dres, tm):
    t, n = gy.shape
    assert t % tm == 0, (t, tm)
    d, cw = wg.shape[2], wg.shape[3]
    aligned = cw % 128 == 0
    cn = cw if aligned else _col_chunk(n)
    steps = t // tm

    def body(gy_ref, w_ref, x_ref, g_ref, dres_ref, dx_ref, dg_ref, acc_ref, *scr):
        i = pl.program_id(0)

        @pl.when(i == 0)
        def _():
            acc_ref[...] = jnp.zeros_like(acc_ref)
            if not aligned:
                _join_col_blocks(w_ref, scr[0])

        dh = jnp.zeros((tm, d), F32)
        for j, c0 in enumerate(range(0, n, cn)):
            wv = w_ref[j] if aligned else scr[0][:, c0:c0 + cn]
            dh = dh + _dot_nt(gy_ref[:, c0:c0 + cn], wv)
        xv = x_ref[...]
        r = _rstd(xv)
        gv = g_ref[...]
        dx_ref[...] = dres_ref[...] + _rms_bwd(xv, r, gv, dh)
        acc_ref[...] += _rowsum8(dh * xv * r)

        @pl.when(i == steps - 1)
        def _():
            dg_ref[...] = jnp.sum(acc_ref[...], axis=0, keepdims=True)

    return pl.pallas_call(
        body, name="matmul_nt_norm_bwd", grid=(steps,),
        in_specs=[pl.BlockSpec((tm, n), lambda i: (i, 0)), _layer_spec(wg, layer), pl.BlockSpec((tm, d), lambda i: (i, 0)),
                  _const_spec((1, d)), pl.BlockSpec((tm, d), lambda i: (i, 0))],
        out_specs=[pl.BlockSpec((tm, d), lambda i: (i, 0)), pl.BlockSpec((1, d), lambda i: (0, 0))],
        out_shape=[jax.ShapeDtypeStruct((t, d), F32), jax.ShapeDtypeStruct((1, d), F32)],
        scratch_shapes=[pltpu.VMEM((8, d), F32)] + ([] if aligned else [pltpu.VMEM((d, n), MM_DTYPE)]),
        compiler_params=_params(("arbitrary",)),
    )(gy, wg, x, g, dres)


_ANY = pl.BlockSpec(memory_space=pl.ANY)


def matmul_tn_blocks(a, b, buf, layer, tk, by_rows):
    t = a.shape[0]
    _, _, r, c = buf.shape
    assert t % tk == 0 and r % 8 == 0 and c % 128 == 0, (a.shape, b.shape, buf.shape, tk)

    def body(a_ref, b_ref, buf_ref, o_ref):
        del buf_ref

        @pl.when(pl.program_id(1) == 0)
        def _():
            o_ref[...] = jnp.zeros_like(o_ref)

        o_ref[...] += _dot_tn(a_ref[...], b_ref[...])

    a_spec = pl.BlockSpec((tk, r), (lambda j, k: (k, j)) if by_rows else (lambda j, k: (k, 0)))
    b_spec = pl.BlockSpec((tk, c), (lambda j, k: (k, 0)) if by_rows else (lambda j, k: (k, j)))
    return pl.pallas_call(
        body, name="matmul_tn_blocks", grid=(N_CHIPS, t // tk), in_specs=[a_spec, b_spec, _ANY],
        out_specs=pl.BlockSpec((None, None, r, c), lambda j, k: (j, layer, 0, 0)),
        out_shape=jax.ShapeDtypeStruct(buf.shape, F32), input_output_aliases={2: 0},
        compiler_params=_params(("arbitrary", "arbitrary")),
    )(a, b, buf)


def matmul_tn_down(act, dd, buf, layer, tk):
    t = act.shape[0]
    _, _, _, r, c = buf.shape
    assert t % tk == 0, (t, tk)
    steps = t // tk

    def body(a_ref, b_ref, buf_ref, o_ref, acc):
        del buf_ref
        k = pl.program_id(1)

        @pl.when(k == 0)
        def _():
            acc[...] = jnp.zeros_like(acc)

        acc[...] += _dot_tn(a_ref[...], b_ref[...])

        @pl.when(k == steps - 1)
        def _():
            o_ref[0] = acc[0:r, :]
            o_ref[1] = acc[r:2 * r, :]

    return pl.pallas_call(
        body, name="matmul_tn_down", grid=(2, steps),
        in_specs=[pl.BlockSpec((tk, 2 * r), lambda p, k: (k, p)), pl.BlockSpec((tk, c), lambda p, k: (k, 0)), _ANY],
        out_specs=pl.BlockSpec((None, 2, None, r, c), lambda p, k: (p, 0, layer, 0, 0)),
        out_shape=jax.ShapeDtypeStruct(buf.shape, F32), input_output_aliases={2: 0},
        scratch_shapes=[pltpu.VMEM((2 * r, c), F32)],
        compiler_params=_params(("arbitrary", "arbitrary")),
    )(act, dd, buf)


def matmul_tn_in(h, dz, buf, layer, tk):
    t, d = h.shape
    n = dz.shape[1]
    _, _, r, c = buf.shape
    assert t % tk == 0 and r == d and N_CHIPS * c == n, (h.shape, dz.shape, buf.shape)
    steps = t // tk

    def body(a_ref, b_ref, buf_ref, o_ref, acc):
        del buf_ref
        k = pl.program_id(0)

        @pl.when(k == 0)
        def _():
            acc[...] = jnp.zeros_like(acc)

        acc[...] += _dot_tn(a_ref[...], b_ref[...])

        @pl.when(k == steps - 1)
        def _():
            for j in range(N_CHIPS):
                o_ref[j] = acc[:, c * j:c * (j + 1)]

    return pl.pallas_call(
        body, name="matmul_tn_in", grid=(steps,),
        in_specs=[pl.BlockSpec((tk, d), lambda k: (k, 0)), pl.BlockSpec((tk, n), lambda k: (k, 0)), _ANY],
        out_specs=pl.BlockSpec((N_CHIPS, None, r, c), lambda k: (0, layer, 0, 0)),
        out_shape=jax.ShapeDtypeStruct(buf.shape, F32), input_output_aliases={2: 0},
        scratch_shapes=[pltpu.VMEM((d, n), F32)],
        compiler_params=_params(("arbitrary",)),
    )(h, dz, buf)


def _project_rows(y, w_ref):
    r = w_ref.shape[1]
    acc = _dot(y[:, 0:r], w_ref[0])
    for j in range(1, N_CHIPS):
        acc = acc + _dot(y[:, r * j:r * (j + 1)], w_ref[j])
    return acc


def _head_select(parts):
    head = lax.broadcasted_iota(jnp.int32, parts[0].shape, 1) // HEAD
    acc = parts[0]
    for h in range(1, N_HEADS_B):
        acc = jnp.where(head == h, parts[h], acc)
    return acc


def _mixer_forward(z, prm, pa_ext, yg_ext, tt):
    caw, lng, lnb, wm, bias_e, ccw, ccb, clg, clb = prm
    bg = z[:, 0:D_A]
    cg = z[:, D_A:2 * D_A]
    xa = z[:, 2 * D_A:3 * D_A]
    pa_ext[HALO_A:HALO_A + tt, :] = cg * xa
    q = jnp.zeros((tt, D_A), F32)
    for k in range(K_A):
        q = q + caw[k:k + 1, :] * pa_ext[pl.ds(HALO_A - (K_A - 1) + k, tt), :]
    ya = bg * q
    o_b = 3 * D_A
    zu = z[:, o_b:o_b + D_B]
    zv = z[:, o_b + D_B:o_b + 2 * D_B]
    u = _gelu(zu)
    vh, rv = _ln_fwd(_gelu(zv))
    vnb = (vh * lng + lnb).astype(MM_DTYPE)
    s_parts = []
    for c in range(tt // CHUNK):
        vc = vnb[c * CHUNK:(c + 1) * CHUNK, :]
        s_parts.append(_head_select([_dot(wm[h], vc) for h in range(N_HEADS_B)]) + bias_e)
    s = jnp.concatenate(s_parts, axis=0) if len(s_parts) > 1 else s_parts[0]
    yb = u * s
    o_c = o_b + 2 * D_B
    a = z[:, o_c:o_c + D_C]
    sg = jax.nn.sigmoid(z[:, o_c + D_C:o_c + 2 * D_C])
    yg_ext[HALO_C:HALO_C + tt, :] = a * sg
    yc = jnp.zeros((tt, D_C), F32) + ccb
    for k in range(K_C):
        yc = yc + ccw[k:k + 1, :] * yg_ext[pl.ds(HALO_C - (K_C - 1) + k, tt), :]
    yh, rc = _ln_fwd(yc)
    l = yh * clg + clb
    sl = jax.nn.sigmoid(l)
    return dict(bg=bg, cg=cg, xa=xa, q=q, ya=ya, zu=zu, zv=zv, u=u, vh=vh, rv=rv, vnb=vnb, s=s, yb=yb,
                a=a, sg=sg, yh=yh, rc=rc, l=l, sl=sl, yo=l * sl)


def _group_norm(f, gg):
    ya, yb, yo = f["ya"], f["yb"], f["yo"]
    ra, rb, ro = _rstd(ya), _rstd(yb), _rstd(yo)
    yn = jnp.concatenate([ya * ra * gg[:, 0:D_A], yb * rb * gg[:, D_A:D_A + D_B], yo * ro * gg[:, D_A + D_B:]], axis=1)
    return yn, (ra, rb, ro)


def _mixer_prm(refs):
    caw_ref, lng_ref, lnb_ref, wm_ref, bias_ref, ccw_ref, ccb_ref, clg_ref, clb_ref = refs
    wm = [wm_ref[h] for h in range(N_HEADS_B)]
    return (caw_ref[...], lng_ref[...], lnb_ref[...], wm, bias_ref[...], ccw_ref[...], ccb_ref[...], clg_ref[...],
            clb_ref[...])


def _mixer_param_specs():
    return [_const_spec((8, D_A)), _const_spec((1, D_B)), _const_spec((1, D_B)), _const_spec((N_HEADS_B, CHUNK, CHUNK)),
            _const_spec((CHUNK, D_B)), _const_spec((32, D_C)), _const_spec((1, D_C)), _const_spec((1, D_C)),
            _const_spec((1, D_C))]


def mixer_fwd(z, x, mp, grp_g, wog, layer, post_g, tt):
    t = z.shape[0]
    assert t % tt == 0 and tt % CHUNK == 0, (t, tt)

    def body(z_ref, x_ref, *rest):
        prm_refs, (gg_ref, wo_ref, pg_ref, o_ref, x1_ref, pa_ext, yg_ext) = rest[:9], rest[9:]
        i = pl.program_id(0)

        @pl.when(i == 0)
        def _():
            pa_ext[0:HALO_A, :] = jnp.zeros((HALO_A, D_A), F32)
            yg_ext[0:HALO_C, :] = jnp.zeros((HALO_C, D_C), F32)

        f = _mixer_forward(z_ref[...], _mixer_prm(prm_refs), pa_ext, yg_ext, tt)
        yn, _ = _group_norm(f, gg_ref[...])
        o = _project_rows(yn.astype(MM_DTYPE), wo_ref)
        o_ref[...] = o
        x1_ref[...] = x_ref[...] + o * _rstd(o) * pg_ref[...]
        pa_ext[0:HALO_A, :] = pa_ext[tt:tt + HALO_A, :]
        yg_ext[0:HALO_C, :] = yg_ext[tt:tt + HALO_C, :]

    row = lambda c: pl.BlockSpec((tt, c), lambda i: (i, 0))
    return pl.pallas_call(
        body, name="mixer_fwd", grid=(t // tt,),
        in_specs=[row(D_IN), row(D_MODEL)] + _mixer_param_specs()
        + [_const_spec((1, D_MODEL)), _layer_spec(wog, layer), _const_spec((1, D_MODEL))],
        out_specs=[row(D_MODEL), row(D_MODEL)],
        out_shape=[jax.ShapeDtypeStruct((t, D_MODEL), F32), jax.ShapeDtypeStruct((t, D_MODEL), F32)],
        scratch_shapes=[pltpu.VMEM((HALO_A + tt, D_A), F32), pltpu.VMEM((HALO_C + tt, D_C), F32)],
        compiler_params=_params(("arbitrary",)),
    )(z, x, *mp, grp_g, wog, post_g)


def mixer_bwd(dx1, o, z, mp, wmt, grp_g, wog, layer, post_g, tt):
    t = z.shape[0]
    assert t % tt == 0 and tt % CHUNK == 0, (t, tt)
    steps = t // tt
    hb = tt // HALO_C

    def body(dx1_ref, o_ref, z_ref, zh_ref, *rest):
        prm_refs = rest[:9]
        (wmt_ref, gg_ref, wo_ref, pg_ref,
         dz_ref, do_ref, yn_ref, dpg_ref, dgg_ref, dcaw_ref, dlng_ref, dlnb_ref, dwm_ref, dbias_ref, dccw_ref, dccb_ref,
         dclg_ref, dclb_ref,
         pa_ext, yg_ext, dq_ext, dyc_ext, a_pg, a_gg, a_caw, a_lng, a_lnb, a_ccw, a_ccb, a_clg, a_clb) = rest[9:]
        i = pl.program_id(0)
        tile = steps - 1 - i
        prm = _mixer_prm(prm_refs)
        caw, lng, lnb, wm, bias_e, ccw, ccb, clg, clb = prm
        small = (a_pg, a_gg, a_caw, a_lng, a_lnb, a_ccw, a_ccb, a_clg, a_clb)

        @pl.when(i == 0)
        def _():
            for ref in small + (dwm_ref, dbias_ref):
                ref[...] = jnp.zeros_like(ref)
            dq_ext[tt:tt + HALO_A, :] = jnp.zeros((HALO_A, D_A), F32)
            dyc_ext[tt:tt + HALO_C, :] = jnp.zeros((HALO_C, D_C), F32)

        zh = zh_ref[...]
        inside = tile > 0
        pa_ext[0:HALO_A, :] = jnp.where(
            inside, zh[HALO_C - HALO_A:, D_A:2 * D_A] * zh[HALO_C - HALO_A:, 2 * D_A:3 * D_A], 0.0)
        o_c = 3 * D_A + 2 * D_B
        yg_ext[0:HALO_C, :] = jnp.where(
            inside, zh[:, o_c:o_c + D_C] * jax.nn.sigmoid(zh[:, o_c + D_C:o_c + 2 * D_C]), 0.0)

        f = _mixer_forward(z_ref[...], prm, pa_ext, yg_ext, tt)
        gg = gg_ref[...]
        yn, (ra, rb, ro) = _group_norm(f, gg)
        yn_ref[...] = yn.astype(MM_DTYPE)

        ov = o_ref[...]
        dx1v = dx1_ref[...]
        r_o = _rstd(ov)
        pg = pg_ref[...]
        a_pg[...] += _rowsum8(dx1v * ov * r_o)
        do = _rms_bwd(ov, r_o, pg, dx1v).astype(MM_DTYPE)
        do_ref[...] = do
        dyn = jnp.concatenate([_dot_nt(do, wo_ref[j]) for j in range(N_CHIPS)], axis=1)

        dyn_a, dyn_b, dyn_c = dyn[:, 0:D_A], dyn[:, D_A:D_A + D_B], dyn[:, D_A + D_B:]
        ga, gb, gc = gg[:, 0:D_A], gg[:, D_A:D_A + D_B], gg[:, D_A + D_B:]
        a_gg[...] += _rowsum8(jnp.concatenate([dyn_a * f["ya"] * ra, dyn_b * f["yb"] * rb, dyn_c * f["yo"] * ro], axis=1))
        dya = _rms_bwd(f["ya"], ra, ga, dyn_a)
        dyb = _rms_bwd(f["yb"], rb, gb, dyn_b)
        dyo = _rms_bwd(f["yo"], ro, gc, dyn_c)

        dbg = dya * f["q"]
        dq = dya * f["bg"]
        dq_ext[0:tt, :] = dq
        dp = jnp.zeros((tt, D_A), F32)
        for k in range(K_A):
            dp = dp + caw[k:k + 1, :] * dq_ext[pl.ds(K_A - 1 - k, tt), :]
            a_caw[k] += _rowsum8(dq * pa_ext[pl.ds(HALO_A - (K_A - 1) + k, tt), :])
        dcg = dp * f["xa"]
        dxa = dp * f["cg"]

        du = dyb * f["s"]
        ds = dyb * f["u"]
        dsb = ds.astype(MM_DTYPE)
        head = lax.broadcasted_iota(jnp.int32, (CHUNK, D_B), 1) // HEAD
        dvn_parts = []
        dbias = jnp.zeros((CHUNK, D_B), F32)
        for c in range(tt // CHUNK):
            rows = slice(c * CHUNK, (c + 1) * CHUNK)
            dsc = dsb[rows, :]
            vc = f["vnb"][rows, :]
            dbias = dbias + ds[rows, :]
            parts = []
            for h in range(N_HEADS_B):
                dwm_ref[h] += _dot_nt(jnp.where(head == h, dsc, jnp.zeros_like(dsc)), vc)
                parts.append(_dot(wmt_ref[h], dsc))
            dvn_parts.append(_head_select(parts))
        dbias_ref[...] += dbias
        dvn = jnp.concatenate(dvn_parts, axis=0) if len(dvn_parts) > 1 else dvn_parts[0]
        a_lng[...] += _rowsum8(dvn * f["vh"])
        a_lnb[...] += _rowsum8(dvn)
        dv = _ln_bwd(f["vh"], f["rv"], dvn * lng)
        dzu = du * _gelu_grad(f["zu"])
        dzv = dv * _gelu_grad(f["zv"])

        l, sl = f["l"], f["sl"]
        dl = dyo * (sl * (1.0 + l * (1.0 - sl)))
        a_clg[...] += _rowsum8(dl * f["yh"])
        a_clb[...] += _rowsum8(dl)
        dyc = _ln_bwd(f["yh"], f["rc"], dl * clg)
        a_ccb[...] += _rowsum8(dyc)
        dyc_ext[0:tt, :] = dyc
        dy = jnp.zeros((tt, D_C), F32)
        for k in range(K_C):
            dy = dy + ccw[k:k + 1, :] * dyc_ext[pl.ds(K_C - 1 - k, tt), :]
            a_ccw[k] += _rowsum8(dyc * yg_ext[pl.ds(HALO_C - (K_C - 1) + k, tt), :])
        da = dy * f["sg"]
        dg = dy * f["a"] * f["sg"] * (1.0 - f["sg"])

        dz_ref[...] = jnp.concatenate([dbg, dcg, dxa, dzu, dzv, da, dg], axis=1).astype(MM_DTYPE)

        dq_ext[tt:tt + HALO_A, :] = dq_ext[0:HALO_A, :]
        dyc_ext[tt:tt + HALO_C, :] = dyc_ext[0:HALO_C, :]

        @pl.when(i == steps - 1)
        def _():
            red = lambda ref: jnp.sum(ref[...], axis=0, keepdims=True)
            dpg_ref[...] = red(a_pg)
            dgg_ref[...] = red(a_gg)
            dlng_ref[...] = red(a_lng)
            dlnb_ref[...] = red(a_lnb)
            dccb_ref[...] = red(a_ccb)
            dclg_ref[...] = red(a_clg)
            dclb_ref[...] = red(a_clb)
            dcaw_ref[...] = jnp.sum(a_caw[...], axis=1)
            dccw_ref[...] = jnp.sum(a_ccw[...], axis=1)

    rev = lambda c: pl.BlockSpec((tt, c), lambda i: (steps - 1 - i, 0))
    halo = pl.BlockSpec((HALO_C, D_IN), lambda i: (jnp.maximum((steps - 1 - i) * hb - 1, 0), 0))
    full = lambda shape: pl.BlockSpec(shape, lambda i: (0,) * len(shape))
    sds = jax.ShapeDtypeStruct
    outs = pl.pallas_call(
        body, name="mixer_bwd", grid=(steps,),
        in_specs=[rev(D_MODEL), rev(D_MODEL), rev(D_IN), halo] + _mixer_param_specs()
        + [_const_spec((N_HEADS_B, CHUNK, CHUNK)), _const_spec((1, D_MODEL)), _layer_spec(wog, layer),
           _const_spec((1, D_MODEL))],
        out_specs=[rev(D_IN), rev(D_MODEL), rev(D_MODEL), full((1, D_MODEL)), full((1, D_MODEL)), full((8, D_A)),
                   full((1, D_B)), full((1, D_B)), full((N_HEADS_B, CHUNK, CHUNK)), full((CHUNK, D_B)), full((32, D_C)),
                   full((1, D_C)), full((1, D_C)), full((1, D_C))],
        out_shape=[sds((t, D_IN), MM_DTYPE), sds((t, D_MODEL), MM_DTYPE), sds((t, D_MODEL), MM_DTYPE),
                   sds((1, D_MODEL), F32), sds((1, D_MODEL), F32), sds((8, D_A), F32), sds((1, D_B), F32), sds((1, D_B), F32),
                   sds((N_HEADS_B, CHUNK, CHUNK), F32), sds((CHUNK, D_B), F32), sds((32, D_C), F32), sds((1, D_C), F32),
                   sds((1, D_C), F32), sds((1, D_C), F32)],
        scratch_shapes=[pltpu.VMEM((HALO_A + tt, D_A), F32), pltpu.VMEM((HALO_C + tt, D_C), F32),
                        pltpu.VMEM((tt + HALO_A, D_A), F32), pltpu.VMEM((tt + HALO_C, D_C), F32),
                        pltpu.VMEM((8, D_MODEL), F32), pltpu.VMEM((8, D_MODEL), F32), pltpu.VMEM((8, 8, D_A), F32),
                        pltpu.VMEM((8, D_B), F32), pltpu.VMEM((8, D_B), F32), pltpu.VMEM((32, 8, D_C), F32),
                        pltpu.VMEM((8, D_C), F32), pltpu.VMEM((8, D_C), F32), pltpu.VMEM((8, D_C), F32)],
        compiler_params=_params(("arbitrary",)),
    )(dx1, o, z, z, *mp, wmt, grp_g, wog, post_g)
    return outs


def _fetch_row_blocks(wg_ref, layer, w_scr, sems):
    r = wg_ref.shape[2]
    copies = [pltpu.make_async_copy(wg_ref.at[j, layer], w_scr.at[pl.ds(r * j, r), :], sems.at[j]) for j in range(N_CHIPS)]
    for cp in copies:
        cp.start()
    for cp in copies:
        cp.wait()


def _ffn_conv(ext, cw, c0, cn, tt):
    acc = jnp.zeros((tt, cn), F32)
    for k in range(K_F):
        acc = acc + cw[k:k + 1, c0:c0 + cn] * ext[pl.ds(HALO_F - (K_F - 1) + k, tt), c0:c0 + cn]
    return acc


def ffn_fwd(up0, x1, cw, wdg, layer, post_g, tt):
    t = up0.shape[0]
    assert t % tt == 0, (t, tt)
    cn = _col_chunk(D_FF)

    def body(up0_ref, x1_ref, cw_ref, wdg_ref, pg_ref, d_ref, x2_ref, ext, wd_ref, sems):
        i = pl.program_id(0)

        @pl.when(i == 0)
        def _():
            _fetch_row_blocks(wdg_ref, layer, wd_ref, sems)
            ext[0:HALO_F, :] = jnp.zeros((HALO_F, 2 * D_FF), F32)

        ext[HALO_F:HALO_F + tt, :] = up0_ref[...]
        cwv = cw_ref[...]
        d = jnp.zeros((tt, D_MODEL), F32)
        for c0 in range(0, D_FF, cn):
            gate = _ffn_conv(ext, cwv, c0, cn, tt)
            val = _ffn_conv(ext, cwv, D_FF + c0, cn, tt)
            act = (gate * jax.nn.sigmoid(gate) * val).astype(MM_DTYPE)
            d = d + _dot(act, wd_ref[c0:c0 + cn, :])
        d_ref[...] = d
        x2_ref[...] = x1_ref[...] + d * _rstd(d) * pg_ref[...]
        ext[0:HALO_F, :] = ext[tt:tt + HALO_F, :]

    row = lambda c: pl.BlockSpec((tt, c), lambda i: (i, 0))
    return pl.pallas_call(
        body, name="ffn_fwd", grid=(t // tt,),
        in_specs=[row(2 * D_FF), row(D_MODEL), _const_spec((8, 2 * D_FF)), _ANY, _const_spec((1, D_MODEL))],
        out_specs=[row(D_MODEL), row(D_MODEL)],
        out_shape=[jax.ShapeDtypeStruct((t, D_MODEL), F32), jax.ShapeDtypeStruct((t, D_MODEL), F32)],
        scratch_shapes=[pltpu.VMEM((HALO_F + tt, 2 * D_FF), F32), pltpu.VMEM((D_FF, D_MODEL), MM_DTYPE),
                        pltpu.SemaphoreType.DMA((N_CHIPS,))],
        compiler_params=_params(("arbitrary",)),
    )(up0, x1, cw, wdg, post_g)


def ffn_bwd(dx2, d, up0, cw, wdg, layer, post_g, tt):
    t = up0.shape[0]
    assert t % tt == 0, (t, tt)
    steps = t // tt
    hb = tt // HALO_F
    cn = _col_chunk(D_FF)

    def body(dx2_ref, d_ref, up0_ref, uh_ref, cw_ref, wdg_ref, pg_ref,
             dd_ref, act_ref, dup0_ref, dpg_ref, dcw_ref, ext, dup_ext, a_pg, a_cw, wd_ref, sems):
        i = pl.program_id(0)
        tile = steps - 1 - i

        @pl.when(i == 0)
        def _():
            _fetch_row_blocks(wdg_ref, layer, wd_ref, sems)
            a_pg[...] = jnp.zeros_like(a_pg)
            a_cw[...] = jnp.zeros_like(a_cw)
            dup_ext[tt:tt + HALO_F, :] = jnp.zeros((HALO_F, 2 * D_FF), F32)

        ext[0:HALO_F, :] = jnp.where(tile > 0, uh_ref[...], 0.0)
        ext[HALO_F:HALO_F + tt, :] = up0_ref[...]
        cwv = cw_ref[...]
        dv = d_ref[...]
        dx2v = dx2_ref[...]
        r = _rstd(dv)
        a_pg[...] += _rowsum8(dx2v * dv * r)
        dd = _rms_bwd(dv, r, pg_ref[...], dx2v).astype(MM_DTYPE)
        dd_ref[...] = dd
        for c0 in range(0, D_FF, cn):
            gate = _ffn_conv(ext, cwv, c0, cn, tt)
            val = _ffn_conv(ext, cwv, D_FF + c0, cn, tt)
            sg = jax.nn.sigmoid(gate)
            sl = gate * sg
            act_ref[:, c0:c0 + cn] = (sl * val).astype(MM_DTYPE)
            da = _dot_nt(dd, wd_ref[c0:c0 + cn, :])
            dup_ext[0:tt, c0:c0 + cn] = da * val * (sg * (1.0 + gate * (1.0 - sg)))
            dup_ext[0:tt, D_FF + c0:D_FF + c0 + cn] = da * sl
        for c0 in range(0, 2 * D_FF, cn):
            dup = dup_ext[0:tt, c0:c0 + cn]
            acc = jnp.zeros((tt, cn), F32)
            for k in range(K_F):
                acc = acc + cwv[k:k + 1, c0:c0 + cn] * dup_ext[pl.ds(K_F - 1 - k, tt), c0:c0 + cn]
                a_cw[k, :, c0:c0 + cn] += _rowsum8(dup * ext[pl.ds(HALO_F - (K_F - 1) + k, tt), c0:c0 + cn])
            dup0_ref[:, c0:c0 + cn] = acc.astype(MM_DTYPE)
        dup_ext[tt:tt + HALO_F, :] = dup_ext[0:HALO_F, :]

        @pl.when(i == steps - 1)
        def _():
            dpg_ref[...] = jnp.sum(a_pg[...], axis=0, keepdims=True)
            dcw_ref[...] = jnp.sum(a_cw[...], axis=1)

    rev = lambda c: pl.BlockSpec((tt, c), lambda i: (steps - 1 - i, 0))
    halo = pl.BlockSpec((HALO_F, 2 * D_FF), lambda i: (jnp.maximum((steps - 1 - i) * hb - 1, 0), 0))
    full = lambda shape: pl.BlockSpec(shape, lambda i: (0,) * len(shape))
    sds = jax.ShapeDtypeStruct
    return pl.pallas_call(
        body, name="ffn_bwd", grid=(steps,),
        in_specs=[rev(D_MODEL), rev(D_MODEL), rev(2 * D_FF), halo, _const_spec((8, 2 * D_FF)), _ANY,
                  _const_spec((1, D_MODEL))],
        out_specs=[rev(D_MODEL), rev(D_FF), rev(2 * D_FF), full((1, D_MODEL)), full((8, 2 * D_FF))],
        out_shape=[sds((t, D_MODEL), MM_DTYPE), sds((t, D_FF), MM_DTYPE), sds((t, 2 * D_FF), MM_DTYPE),
                   sds((1, D_MODEL), F32), sds((8, 2 * D_FF), F32)],
        scratch_shapes=[pltpu.VMEM((HALO_F + tt, 2 * D_FF), F32), pltpu.VMEM((tt + HALO_F, 2 * D_FF), F32),
                        pltpu.VMEM((8, D_MODEL), F32), pltpu.VMEM((8, 8, 2 * D_FF), F32),
                        pltpu.VMEM((D_FF, D_MODEL), MM_DTYPE), pltpu.SemaphoreType.DMA((N_CHIPS,))],
        compiler_params=_params(("arbitrary",)),
    )(dx2, d, up0, up0, cw, wdg, post_g)


def loss_head(y, target, tm):
    t, d = y.shape
    assert t % tm == 0, (t, tm)
    steps = t // tm

    def body(y_ref, t_ref, dy_ref, loss_ref, acc):
        i = pl.program_id(0)

        @pl.when(i == 0)
        def _():
            acc[...] = jnp.zeros_like(acc)

        diff = y_ref[...] - t_ref[...]
        dy_ref[...] = diff * (1.0 / d)
        acc[...] += _rowsum8(diff * diff)

        @pl.when(i == steps - 1)
        def _():
            loss_ref[...] = (0.5 / d) * jnp.sum(jnp.sum(acc[...], axis=0, keepdims=True), axis=1, keepdims=True)

    row = pl.BlockSpec((tm, d), lambda i: (i, 0))
    return pl.pallas_call(
        body, name="loss_head", grid=(steps,), in_specs=[row, row],
        out_specs=[row, pl.BlockSpec((1, 1), lambda i: (0, 0))],
        out_shape=[jax.ShapeDtypeStruct((t, d), F32), jax.ShapeDtypeStruct((1, 1), F32)],
        scratch_shapes=[pltpu.VMEM((8, d), F32)],
        compiler_params=_params(("arbitrary",)),
    )(y, target)


def adamw(w, g, m, v):
    shape = w.shape
    cols = shape[-1]
    rows = w.size // cols
    tr = next((r for r in (512, 256, 128) if rows % r == 0 and rows > r), rows)
    c1 = 1.0 - ADAM_B1 ** ADAM_STEP
    c2 = 1.0 - ADAM_B2 ** ADAM_STEP

    def body(w_ref, g_ref, m_ref, v_ref, d_ref, nm_ref, nv_ref):
        gv = g_ref[...]
        nm = ADAM_B1 * m_ref[...] + (1.0 - ADAM_B1) * gv
        nv = ADAM_B2 * v_ref[...] + (1.0 - ADAM_B2) * (gv * gv)
        nm_ref[...] = nm
        nv_ref[...] = nv
        d_ref[...] = -ADAM_LR * ((nm / c1) / (jnp.sqrt(nv / c2) + ADAM_EPS) + ADAM_WD * w_ref[...])

    spec = pl.BlockSpec((tr, cols), lambda i: (i, 0))
    out = jax.ShapeDtypeStruct((rows, cols), F32)
    res = pl.pallas_call(
        body, name="adamw", grid=(rows // tr,), in_specs=[spec] * 4, out_specs=[spec] * 3, out_shape=[out] * 3,
        compiler_params=_params(("arbitrary",)),
    )(*[a.reshape(rows, cols) for a in (w, g, m, v)])
    return tuple(r.reshape(shape) for r in res)


HALF = DEPTH // 2


def _place():
    return lax.axis_index("x"), lax.axis_index("y"), lax.axis_index("c")


def _other_chips(x, y):
    return [(1 - x, y, 2 * (1 - x) + y), (x, 1 - y, 2 * x + 1 - y), (1 - x, 1 - y, 2 * (1 - x) + 1 - y)]


def _sem_specs(*counts):
    return [pltpu.SemaphoreType.DMA((n,)) for n in counts]


def cast_shard(w, chip):
    l, r, c = w.shape

    def body(chip_ref, w_ref, o_ref):
        del chip_ref
        o_ref[...] = w_ref[...].astype(MM_DTYPE)

    grid_spec = pltpu.PrefetchScalarGridSpec(
        num_scalar_prefetch=1, grid=(l,), in_specs=[pl.BlockSpec((None, r, c), lambda i, chip_ref: (i, 0, 0))],
        out_specs=pl.BlockSpec((None, None, r, c), lambda i, chip_ref: (chip_ref[0], i, 0, 0)))
    return pl.pallas_call(
        body, name="cast_shard", grid_spec=grid_spec, out_shape=jax.ShapeDtypeStruct((N_CHIPS, l, r, c), MM_DTYPE),
        compiler_params=_params(("arbitrary",)),
    )(jnp.reshape(chip, (1,)).astype(jnp.int32), w)


def gather_weights(shards, small):
    n = len(shards)

    def body(*refs):
        small_ref = refs[n]
        outs = refs[n + 1:2 * n + 1]
        smalls_ref = refs[2 * n + 1]
        send, recv, fsend, frecv, ssend, srecv, local = refs[2 * n + 2:]
        x, y, c = _place()
        me = 2 * x + y
        chips = _other_chips(x, y)
        mine = pl.ds(c * HALF, HALF)
        theirs = pl.ds((1 - c) * HALF, HALF)

        own_small = pltpu.make_async_copy(small_ref, smalls_ref.at[me], local.at[0])
        own_small.start()

        def block(ref, chip, layers, sems, k, to):
            part = ref.at[chip, layers]
            return pltpu.make_async_remote_copy(src_ref=part, dst_ref=part, send_sem=sems[0].at[k], recv_sem=sems[1].at[k],
                                                device_id=to, device_id_type=MESH_ID)

        def small_copy(k, chip, to):
            return pltpu.make_async_remote_copy(src_ref=small_ref, dst_ref=smalls_ref.at[chip], send_sem=ssend.at[k],
                                                recv_sem=srecv.at[k], device_id=to, device_id_type=MESH_ID)

        sends = []
        for k, (px, py, _) in enumerate(chips):
            for i, ref in enumerate(outs):
                sends.append(block(ref, me, mine, (send, recv), n * k + i, (px, py, c)))
            sends.append(small_copy(k, me, (px, py, c)))
        for cp in sends:
            cp.start()
        forwards = []
        for k, (_, _, pj) in enumerate(chips):
            for i, ref in enumerate(outs):
                block(ref, pj, mine, (send, recv), n * k + i, (x, y, c)).wait_recv()
                fw = block(ref, pj, mine, (fsend, frecv), n * k + i, (x, y, 1 - c))
                fw.start()
                forwards.append(fw)
        for k, (_, _, pj) in enumerate(chips):
            small_copy(k, pj, (x, y, c)).wait_recv()
            for i, ref in enumerate(outs):
                block(ref, pj, theirs, (fsend, frecv), n * k + i, (x, y, 1 - c)).wait_recv()
        for cp in sends + forwards:
            cp.wait_send()
        own_small.wait()

    res = pl.pallas_call(
        body, name="gather_weights", in_specs=[_ANY] * (n + 1), out_specs=[_ANY] * (n + 1),
        out_shape=[jax.ShapeDtypeStruct(s.shape, s.dtype) for s in shards]
        + [jax.ShapeDtypeStruct((N_CHIPS,) + small.shape, small.dtype)],
        input_output_aliases={i: i for i in range(n)},
        scratch_shapes=_sem_specs(3 * n, 3 * n, 3 * n, 3 * n, 3, 3, 1),
        compiler_params=pltpu.CompilerParams(has_side_effects=True),
    )(*shards, small)
    return res[:n], res[n]


def swap_halves(gs):
    n = len(gs)

    def body(*refs):
        g_refs, got_refs, (send, recv) = refs[:n], refs[n:2 * n], refs[2 * n:]
        x, y, c = _place()
        theirs = pl.ds((1 - c) * HALF, HALF)
        copies = [pltpu.make_async_remote_copy(
            src_ref=g.at[:, theirs], dst_ref=got, send_sem=send.at[i], recv_sem=recv.at[i],
            device_id=(x, y, 1 - c), device_id_type=MESH_ID) for i, (g, got) in enumerate(zip(g_refs, got_refs))]
        for cp in copies:
            cp.start()
        for cp in copies:
            cp.wait()

    return pl.pallas_call(
        body, name="swap_halves", in_specs=[_ANY] * n, out_specs=[_ANY] * n,
        out_shape=[jax.ShapeDtypeStruct((g.shape[0], HALF) + g.shape[2:], g.dtype) for g in gs],
        scratch_shapes=_sem_specs(n, n), compiler_params=pltpu.CompilerParams(has_side_effects=True),
    )(*gs)


def _row_tile(r):
    return next(t for t in (256, 352, 128) if r % t == 0)


def add_halves(g, got):
    n, _, r, cols = g.shape
    tr = _row_tile(r)
    c = lax.axis_index("c")

    def body(c_ref, g_ref, got_ref, s_ref, sb_ref):
        del c_ref
        s = g_ref[...] + got_ref[...]
        s_ref[...] = s
        sb_ref[...] = s.astype(BF16)

    blk = (None, None, tr, cols)
    grid_spec = pltpu.PrefetchScalarGridSpec(
        num_scalar_prefetch=1, grid=(n, HALF, r // tr),
        in_specs=[pl.BlockSpec(blk, lambda j, l, i, c_ref: (j, c_ref[0] * HALF + l, i, 0)),
                  pl.BlockSpec(blk, lambda j, l, i, c_ref: (j, l, i, 0))],
        out_specs=[pl.BlockSpec(blk, lambda j, l, i, c_ref: (j, l, i, 0))] * 2)
    return pl.pallas_call(
        body, name="add_halves", grid_spec=grid_spec,
        out_shape=[jax.ShapeDtypeStruct(got.shape, F32), jax.ShapeDtypeStruct(got.shape, BF16)],
        compiler_params=_params(("arbitrary",) * 3),
    )(jnp.reshape(c, (1,)).astype(jnp.int32), g, got)


def scatter_to_chips(sbs):
    n = len(sbs)

    def body(*refs):
        sb_refs, got_refs, (send, recv) = refs[:n], refs[n:2 * n], refs[2 * n:]
        x, y, c = _place()
        me = 2 * x + y
        chips = _other_chips(x, y)
        sends = []
        for k, (px, py, pj) in enumerate(chips):
            for i, (sb, got) in enumerate(zip(sb_refs, got_refs)):
                sends.append(pltpu.make_async_remote_copy(
                    src_ref=sb.at[pj], dst_ref=got.at[me], send_sem=send.at[n * k + i], recv_sem=recv.at[n * k + i],
                    device_id=(px, py, c), device_id_type=MESH_ID))
        for cp in sends:
            cp.start()
        for k, (_, _, pj) in enumerate(chips):
            for i, (sb, got) in enumerate(zip(sb_refs, got_refs)):
                pltpu.make_async_remote_copy(
                    src_ref=sb.at[pj], dst_ref=got.at[pj], send_sem=send.at[n * k + i], recv_sem=recv.at[n * k + i],
                    device_id=(x, y, c), device_id_type=MESH_ID).wait_recv()
        for cp in sends:
            cp.wait_send()

    return pl.pallas_call(
        body, name="scatter_to_chips", in_specs=[_ANY] * n, out_specs=[_ANY] * n,
        out_shape=[jax.ShapeDtypeStruct(sb.shape, sb.dtype) for sb in sbs],
        scratch_shapes=_sem_specs(3 * n, 3 * n), compiler_params=pltpu.CompilerParams(has_side_effects=True),
    )(*sbs)


def add_chips(s, got):
    n, _, r, cols = s.shape
    tr = _row_tile(r)
    x, y, c = _place()
    me = 2 * x + y

    def body(p_ref, s_ref, g1_ref, g2_ref, g3_ref, o_ref):
        del p_ref
        o_ref[...] = s_ref[...] + g1_ref[...].astype(F32) + g2_ref[...].astype(F32) + g3_ref[...].astype(F32)

    blk = (None, None, tr, cols)

    def other(k):
        return pl.BlockSpec(blk, lambda l, i, p_ref: ((p_ref[0] + k) % n, l, i, 0))

    grid_spec = pltpu.PrefetchScalarGridSpec(
        num_scalar_prefetch=1, grid=(HALF, r // tr),
        in_specs=[pl.BlockSpec(blk, lambda l, i, p_ref: (p_ref[0], l, i, 0)), other(1), other(2), other(3)],
        out_specs=pl.BlockSpec((None, tr, cols), lambda l, i, p_ref: (p_ref[1] * HALF + l, i, 0)))
    return pl.pallas_call(
        body, name="add_chips", grid_spec=grid_spec, out_shape=jax.ShapeDtypeStruct((DEPTH, r, cols), F32),
        compiler_params=_params(("arbitrary",) * 2),
    )(jnp.stack([me, c]).astype(jnp.int32), s, got, got, got)


def join_halves(fs):
    n = len(fs)

    def body(*refs):
        f_refs, (send, recv) = refs[n:2 * n], refs[2 * n:]
        x, y, c = _place()
        mine = pl.ds(c * HALF, HALF)
        theirs = pl.ds((1 - c) * HALF, HALF)
        sends = [pltpu.make_async_remote_copy(
            src_ref=f.at[mine], dst_ref=f.at[mine], send_sem=send.at[i], recv_sem=recv.at[i],
            device_id=(x, y, 1 - c), device_id_type=MESH_ID) for i, f in enumerate(f_refs)]
        for cp in sends:
            cp.start()
        for i, f in enumerate(f_refs):
            pltpu.make_async_remote_copy(
                src_ref=f.at[theirs], dst_ref=f.at[theirs], send_sem=send.at[i], recv_sem=recv.at[i],
                device_id=(x, y, 1 - c), device_id_type=MESH_ID).wait_recv()
        for cp in sends:
            cp.wait_send()

    return pl.pallas_call(
        body, name="join_halves", in_specs=[_ANY] * n, out_specs=[_ANY] * n,
        out_shape=[jax.ShapeDtypeStruct(f.shape, f.dtype) for f in fs], input_output_aliases={i: i for i in range(n)},
        scratch_shapes=_sem_specs(n, n), compiler_params=pltpu.CompilerParams(has_side_effects=True),
    )(*fs)


def gather_all(v):
    def body(v_ref, o_ref, send, recv, local):
        x, y, c = _place()
        me = 4 * x + 2 * y + c
        own = pltpu.make_async_copy(v_ref, o_ref.at[me], local.at[0])
        own.start()
        sends = []
        k = 0
        for fx in (0, 1):
            for fy in (0, 1):
                for fc in (0, 1):
                    if fx or fy or fc:
                        cp = pltpu.make_async_remote_copy(
                            src_ref=v_ref, dst_ref=o_ref.at[me], send_sem=send.at[k], recv_sem=recv.at[k],
                            device_id=(x ^ fx, y ^ fy, c ^ fc), device_id_type=MESH_ID)
                        cp.start()
                        sends.append((cp, k, 4 * (x ^ fx) + 2 * (y ^ fy) + (c ^ fc)))
                        k += 1
        for cp, k, peer in sends:
            pltpu.make_async_remote_copy(
                src_ref=v_ref, dst_ref=o_ref.at[peer], send_sem=send.at[k], recv_sem=recv.at[k],
                device_id=(x, y, c), device_id_type=MESH_ID).wait_recv()
        for cp, _, _ in sends:
            cp.wait_send()
        own.wait()

    return pl.pallas_call(
        body, name="gather_all", in_specs=[_ANY], out_specs=_ANY,
        out_shape=jax.ShapeDtypeStruct((8,) + v.shape, v.dtype), scratch_shapes=_sem_specs(7, 7, 1),
        compiler_params=pltpu.CompilerParams(has_side_effects=True),
    )(v)


def sum_devices(v8, tr):
    n, rows, lanes = v8.shape
    assert rows % tr == 0, (rows, tr)

    def body(v_ref, o_ref):
        acc = v_ref[0]
        for j in range(1, n):
            acc = acc + v_ref[j]
        o_ref[...] = acc

    return pl.pallas_call(
        body, name="sum_devices", grid=(rows // tr,),
        in_specs=[pl.BlockSpec((n, tr, lanes), lambda i: (0, i, 0))], out_specs=pl.BlockSpec((tr, lanes), lambda i: (i, 0)),
        out_shape=jax.ShapeDtypeStruct((rows, lanes), F32), compiler_params=_params(("arbitrary",)),
    )(v8)


def _pack(arrays, rows):
    flat = jnp.concatenate([a.reshape(-1) for a in arrays])
    return jnp.pad(flat, (0, rows * LANES - flat.size)).reshape(rows, LANES)


def _unpack(buf, shapes):
    flat = buf.reshape(-1)
    out, at = [], 0
    for s in shapes:
        n = math.prod(s)
        out.append(flat[at:at + n].reshape(s))
        at += n
    return out


CONV_SHARDS = [(DEPTH, K_A, D_A // N_CHIPS), (DEPTH, K_C, D_C // N_CHIPS), (DEPTH, K_F, 2 * D_FF // N_CHIPS)]
CONV_ROWS = 32
SMALL_ROWS = 576


def _join_cols(g):
    n, l, r, c = g.shape
    return jnp.transpose(g, (1, 2, 0, 3)).reshape(l, r, n * c)


TILE_MM = 512
TILE_EW = 256


def _pad_rows(a, rows):
    return jnp.pad(a, ((0, rows - a.shape[0]), (0, 0)))


def _row(a):
    return a.reshape(1, -1)


def _layer_params(wl):
    tril = jnp.tril(jnp.ones((CHUNK, CHUNK), bool))
    wm = jnp.where(tril[None], wl["sgu_w"], 0.0)
    bias_e = jnp.repeat(wl["sgu_b"].T, HEAD, axis=1)
    mp = (_pad_rows(wl["conv_a_w"], 8), _row(wl["sgu_ln_g"]), _row(wl["sgu_ln_b"]), wm.astype(MM_DTYPE), bias_e,
          _pad_rows(wl["conv_c_w"], 32), _row(wl["conv_c_b"]), _row(wl["conv_ln_g"]), _row(wl["conv_ln_b"]))
    wmt = jnp.transpose(wm, (0, 2, 1)).astype(MM_DTYPE)
    return mp, wmt, tril


def layer_fwd(x, wl, gw, layer, tm=TILE_MM, tt=TILE_EW):
    mp, _, _ = _layer_params(wl)
    z, h = norm_matmul(x, _row(wl["pre_mix_g"]), gw["w_in"], layer, tm)
    o, x1 = mixer_fwd(z, x, mp, _row(wl["grp_norm_g"]), gw["w_out"], layer, _row(wl["post_mix_g"]), tt)
    up0, h2 = norm_matmul(x1, _row(wl["pre_ffn_g"]), gw["w_up"], layer, tt)
    d, x2 = ffn_fwd(up0, x1, _pad_rows(wl["ffn_conv_w"], 8), gw["w_down"], layer, _row(wl["post_ffn_g"]), tt)
    return x2, dict(x=x, z=z, h=h, o=o, x1=x1, up0=up0, h2=h2, d=d)


def layer_bwd(dx2, wl, gw, layer, sv, gbuf, tm=TILE_MM, tt=TILE_EW):
    mp, wmt, tril = _layer_params(wl)
    tk = min(512, dx2.shape[0])
    g = {}
    gbuf = dict(gbuf)
    dd, act, dup0, dpg, dcw = ffn_bwd(dx2, sv["d"], sv["up0"], _pad_rows(wl["ffn_conv_w"], 8), gw["w_down"], layer,
                                      _row(wl["post_ffn_g"]), tt)
    g["post_ffn_g"] = dpg[0]
    g["ffn_conv_w"] = dcw[:K_F]
    gbuf["w_down"] = matmul_tn_down(act, dd, gbuf["w_down"], layer, tk)
    gbuf["w_up"] = matmul_tn_blocks(sv["h2"], dup0, gbuf["w_up"], layer, tk, by_rows=False)
    dx1, dg = matmul_nt_norm_bwd(dup0, gw["w_up"], layer, sv["x1"], _row(wl["pre_ffn_g"]), dx2, tm)
    g["pre_ffn_g"] = dg[0]
    (dz, do, yn, dpg, dgg, dcaw, dlng, dlnb, dwm, dbias, dccw, dccb, dclg, dclb) = mixer_bwd(
        dx1, sv["o"], sv["z"], mp, wmt, _row(wl["grp_norm_g"]), gw["w_out"], layer, _row(wl["post_mix_g"]), tt)
    g["post_mix_g"] = dpg[0]
    g["grp_norm_g"] = dgg[0]
    g["conv_a_w"] = dcaw[:K_A]
    g["sgu_ln_g"] = dlng[0]
    g["sgu_ln_b"] = dlnb[0]
    g["sgu_w"] = jnp.where(tril[None], dwm, 0.0)
    g["sgu_b"] = jnp.sum(dbias.reshape(CHUNK, N_HEADS_B, HEAD), axis=2).T
    g["conv_c_w"] = dccw[:K_C]
    g["conv_c_b"] = dccb[0]
    g["conv_ln_g"] = dclg[0]
    g["conv_ln_b"] = dclb[0]
    gbuf["w_out"] = matmul_tn_blocks(yn, do, gbuf["w_out"], layer, tk, by_rows=True)
    gbuf["w_in"] = matmul_tn_in(sv["h"], dz, gbuf["w_in"], layer, tk)
    dx, dg = matmul_nt_norm_bwd(dz, gw["w_in"], layer, sv["x"], _row(wl["pre_mix_g"]), dx1, tm)
    g["pre_mix_g"] = dg[0]
    return dx, g, gbuf


def grad_buffers():
    sds = lambda *s: lax.empty(s, F32)
    return dict(w_in=sds(N_CHIPS, DEPTH, D_MODEL, D_IN // N_CHIPS), w_out=sds(N_CHIPS, DEPTH, D_MODEL // N_CHIPS, D_MODEL),
                w_up=sds(N_CHIPS, DEPTH, D_MODEL, 2 * D_FF // N_CHIPS), w_down=sds(2, 2, DEPTH, D_FF // N_CHIPS, D_MODEL))


BIG = ["w_in", "w_out", "w_up", "w_down"]
CONV = ["conv_a_w", "conv_c_w", "ffn_conv_w"]
REPL = ["pre_mix_g", "sgu_ln_g", "sgu_ln_b", "sgu_w", "sgu_b", "conv_c_b", "conv_ln_g", "conv_ln_b", "grp_norm_g",
        "post_mix_g", "pre_ffn_g", "post_ffn_g"]
WEIGHTS = ["pre_mix_g", "w_in", "conv_a_w", "sgu_ln_g", "sgu_ln_b", "sgu_w", "sgu_b", "conv_c_w", "conv_c_b", "conv_ln_g",
           "conv_ln_b", "grp_norm_g", "w_out", "post_mix_g", "pre_ffn_g", "w_up", "ffn_conv_w", "w_down", "post_ffn_g"]


def kernel(x, pre_mix_g, w_in, conv_a_w, sgu_ln_g, sgu_ln_b, sgu_w, sgu_b, conv_c_w, conv_c_b, conv_ln_g, conv_ln_b, grp_norm_g, w_out, post_mix_g, pre_ffn_g, w_up, ffn_conv_w, w_down, post_ffn_g, loss_target, m_pre_mix_g, m_w_in, m_conv_a_w, m_sgu_ln_g, m_sgu_ln_b, m_sgu_w, m_sgu_b, m_conv_c_w, m_conv_c_b, m_conv_ln_g, m_conv_ln_b, m_grp_norm_g, m_w_out, m_post_mix_g, m_pre_ffn_g, m_w_up, m_ffn_conv_w, m_w_down, m_post_ffn_g, v_pre_mix_g, v_w_in, v_conv_a_w, v_sgu_ln_g, v_sgu_ln_b, v_sgu_w, v_sgu_b, v_conv_c_w, v_conv_c_b, v_conv_ln_g, v_conv_ln_b, v_grp_norm_g, v_w_out, v_post_mix_g, v_pre_ffn_g, v_w_up, v_ffn_conv_w, v_w_down, v_post_ffn_g):
    w = dict(pre_mix_g=pre_mix_g, w_in=w_in, conv_a_w=conv_a_w, sgu_ln_g=sgu_ln_g, sgu_ln_b=sgu_ln_b, sgu_w=sgu_w, sgu_b=sgu_b,
             conv_c_w=conv_c_w, conv_c_b=conv_c_b, conv_ln_g=conv_ln_g, conv_ln_b=conv_ln_b, grp_norm_g=grp_norm_g,
             w_out=w_out, post_mix_g=post_mix_g, pre_ffn_g=pre_ffn_g, w_up=w_up, ffn_conv_w=ffn_conv_w, w_down=w_down,
             post_ffn_g=post_ffn_g)
    m = dict(pre_mix_g=m_pre_mix_g, w_in=m_w_in, conv_a_w=m_conv_a_w, sgu_ln_g=m_sgu_ln_g, sgu_ln_b=m_sgu_ln_b,
             sgu_w=m_sgu_w, sgu_b=m_sgu_b, conv_c_w=m_conv_c_w, conv_c_b=m_conv_c_b, conv_ln_g=m_conv_ln_g,
             conv_ln_b=m_conv_ln_b, grp_norm_g=m_grp_norm_g, w_out=m_w_out, post_mix_g=m_post_mix_g,
             pre_ffn_g=m_pre_ffn_g, w_up=m_w_up, ffn_conv_w=m_ffn_conv_w, w_down=m_w_down, post_ffn_g=m_post_ffn_g)
    v = dict(pre_mix_g=v_pre_mix_g, w_in=v_w_in, conv_a_w=v_conv_a_w, sgu_ln_g=v_sgu_ln_g, sgu_ln_b=v_sgu_ln_b,
             sgu_w=v_sgu_w, sgu_b=v_sgu_b, conv_c_w=v_conv_c_w, conv_c_b=v_conv_c_b, conv_ln_g=v_conv_ln_g,
             conv_ln_b=v_conv_ln_b, grp_norm_g=v_grp_norm_g, w_out=v_w_out, post_mix_g=v_post_mix_g,
             pre_ffn_g=v_pre_ffn_g, w_up=v_w_up, ffn_conv_w=v_ffn_conv_w, w_down=v_w_down, post_ffn_g=v_post_ffn_g)
    chip = 2 * lax.axis_index("x") + lax.axis_index("y")

    shards, convs = gather_weights([cast_shard(w[n], chip) for n in BIG], _pack([w[n] for n in CONV], CONV_ROWS))
    gw = dict(zip(BIG, shards))
    cparts = [_unpack(convs[j], CONV_SHARDS) for j in range(N_CHIPS)]
    full = dict(w)
    for i, n in enumerate(CONV):
        full[n] = _join_cols(jnp.stack([p[i] for p in cparts]))

    xc = x[0]
    saved = []
    for layer in range(DEPTH):
        xc, sv = layer_fwd(xc, {n: full[n][layer] for n in REPL + CONV}, gw, layer)
        saved.append(sv)
    dxc, loss_part = loss_head(xc, loss_target[0], TILE_MM)
    loss = lax.psum(loss_part[0, 0], ("x", "y", "c"))
    gbuf = grad_buffers()
    small = [None] * DEPTH
    for layer in reversed(range(DEPTH)):
        dxc, small[layer], gbuf = layer_bwd(dxc, {n: full[n][layer] for n in REPL + CONV}, gw, layer, saved[layer], gbuf)
    grads = {n: jnp.stack([small[layer][n] for layer in range(DEPTH)]) for n in REPL + CONV}

    gbuf["w_down"] = gbuf["w_down"].reshape(N_CHIPS, DEPTH, D_FF // N_CHIPS, D_MODEL)
    gs = [gbuf[n] for n in BIG]
    sums = [add_halves(g, got) for g, got in zip(gs, swap_halves(gs))]
    arrived = scatter_to_chips([s16 for _, s16 in sums])
    out_g = dict(zip(BIG, join_halves([add_chips(s32, got) for (s32, _), got in zip(sums, arrived)])))

    tot = sum_devices(gather_all(_pack([grads[n] for n in REPL + CONV], SMALL_ROWS)), 192)
    shapes = [grads[n].shape for n in REPL + CONV]
    for n, gfull in zip(REPL + CONV, _unpack(tot, shapes)):
        if n in CONV:
            width = gfull.shape[-1] // N_CHIPS
            gfull = lax.dynamic_slice_in_dim(gfull, chip * width, width, axis=2)
        out_g[n] = gfull

    deltas, new_m, new_v = {}, {}, {}
    for n in WEIGHTS:
        deltas[n], new_m[n], new_v[n] = adamw(w[n], out_g[n], m[n], v[n])
    return (loss, dxc[None], *[out_g[n] for n in WEIGHTS], *[deltas[n] for n in WEIGHTS], *[new_m[n] for n in WEIGHTS],
            *[new_v[n] for n in WEIGHTS])
```

```python
import functools
import math

import jax
import jax.numpy as jnp
from jax import lax
from jax.experimental import pallas as pl
from jax.experimental.pallas import tpu as pltpu

F32 = jnp.float32
BF16 = jnp.bfloat16
MM_DTYPE = BF16

D_MODEL = 1024
SEQ = 4096
DEPTH = 4
D_A = 256
D_B = 384
D_C = 384
D_IN = 3 * D_A + 2 * D_B + 2 * D_C
D_FF = 2816
K_A = 3
K_C = 31
K_F = 3
CHUNK = 128
HEAD = 64
N_HEADS_B = D_B // HEAD
EPS = 1e-6
N_CHIPS = 4

ADAM_LR = 0.001
ADAM_B1 = 0.9
ADAM_B2 = 0.999
ADAM_EPS = 1e-08
ADAM_WD = 0.01
ADAM_STEP = 10

LANES = 1024
VMEM_LIMIT = 56 * 1024 * 1024

MESH_ID = pl.DeviceIdType.MESH


def _params(sem=None):
    return pltpu.CompilerParams(dimension_semantics=sem, vmem_limit_bytes=VMEM_LIMIT)


def _const_spec(shape):
    nd = len(shape)
    return pl.BlockSpec(shape, lambda *_: (0,) * nd, pipeline_mode=pl.Buffered(1))


def _rowsum8(a):
    r, c = a.shape
    return jnp.sum(a.reshape(r // 8, 8, c), axis=0)


def _rstd(x):
    return lax.rsqrt(jnp.mean(x * x, axis=-1, keepdims=True) + EPS)


def _rms_bwd(x, r, g, dy):
    gdy = g * dy
    return r * gdy - x * (r * r * r) * jnp.mean(gdy * x, axis=-1, keepdims=True)


def _ln_fwd(x):
    mu = jnp.mean(x, axis=-1, keepdims=True)
    xc = x - mu
    r = lax.rsqrt(jnp.mean(xc * xc, axis=-1, keepdims=True) + EPS)
    return xc * r, r


def _ln_bwd(xh, r, dxh):
    return r * (dxh - jnp.mean(dxh, axis=-1, keepdims=True) - xh * jnp.mean(dxh * xh, axis=-1, keepdims=True))


def _gelu(x):
    return 0.5 * x * (1.0 + lax.erf(x * (1.0 / math.sqrt(2.0))))


def _gelu_grad(x):
    cdf = 0.5 * (1.0 + lax.erf(x * (1.0 / math.sqrt(2.0))))
    pdf = jnp.exp(-0.5 * x * x) * (1.0 / math.sqrt(2.0 * math.pi))
    return cdf + x * pdf


def _dot(a, b):
    return jnp.dot(a, b, preferred_element_type=F32)


def _dot_nt(a, b):
    return lax.dot_general(a, b, (((1,), (1,)), ((), ())), preferred_element_type=F32)


def _dot_tn(a, b):
    return lax.dot_general(a, b, (((0,), (0,)), ((), ())), preferred_element_type=F32)


def _col_chunk(n):
    for c in (1408, 1024, 768, 512, 256, 128):
        if n % c == 0:
            return c
    raise ValueError(n)


def _layer_spec(wg, layer):
    _, _, r, c = wg.shape
    return pl.BlockSpec((N_CHIPS, None, r, c), lambda *_: (0, layer, 0, 0), pipeline_mode=pl.Buffered(1))


def _join_col_blocks(w_ref, w_scr):
    c = w_ref.shape[2]
    for j in range(N_CHIPS):
        w_scr[:, c * j:c * (j + 1)] = w_ref[j]


def norm_matmul(x, g, wg, layer, tm):
    t, d = x.shape
    assert t % tm == 0, (t, tm)
    cw = wg.shape[3]
    n = N_CHIPS * cw
    aligned = cw % 128 == 0
    cn = cw if aligned else _col_chunk(n)

    def body(x_ref, g_ref, w_ref, o_ref, h_ref, *scr):
        if not aligned:
            @pl.when(pl.program_id(0) == 0)
            def _():
                _join_col_blocks(w_ref, scr[0])

        xv = x_ref[...]
        h = (xv * _rstd(xv) * g_ref[...]).astype(MM_DTYPE)
        h_ref[...] = h
        for j, c0 in enumerate(range(0, n, cn)):
            wv = w_ref[j] if aligned else scr[0][:, c0:c0 + cn]
            o_ref[:, c0:c0 + cn] = _dot(h, wv)

    return pl.pallas_call(
        body, name="norm_matmul", grid=(t // tm,),
        in_specs=[pl.BlockSpec((tm, d), lambda i: (i, 0)), _const_spec((1, d)), _layer_spec(wg, layer)],
        out_specs=[pl.BlockSpec((tm, n), lambda i: (i, 0)), pl.BlockSpec((tm, d), lambda i: (i, 0))],
        out_shape=[jax.ShapeDtypeStruct((t, n), F32), jax.ShapeDtypeStruct((t, d), MM_DTYPE)],
        scratch_shapes=[] if aligned else [pltpu.VMEM((d, n), MM_DTYPE)],
        compiler_params=_params(("arbitrary",)),
    )(x, g, wg)


def matmul_nt_norm_bwd(gy, wg, layer, x, g, dres, tm):
    t, n = gy.shape
    assert t % tm == 0, (t, tm)
    d, cw = wg.shape[2], wg.shape[3]
    aligned = cw % 128 == 0
    cn = cw if aligned else _col_chunk(n)
    steps = t // tm

    def body(gy_ref, w_ref, x_ref, g_ref, dres_ref, dx_ref, dg_ref, acc_ref, *scr):
        i = pl.program_id(0)

        @pl.when(i == 0)
        def _():
            acc_ref[...] = jnp.zeros_like(acc_ref)
            if not aligned:
                _join_col_blocks(w_ref, scr[0])

        dh = jnp.zeros((tm, d), F32)
        for j, c0 in enumerate(range(0, n, cn)):
            wv = w_ref[j] if aligned else scr[0][:, c0:c0 + cn]
            dh = dh + _dot_nt(gy_ref[:, c0:c0 + cn], wv)
        xv = x_ref[...]
        r = _rstd(xv)
        gv = g_ref[...]
        dx_ref[...] = dres_ref[...] + _rms_bwd(xv, r, gv, dh)
        acc_ref[...] += _rowsum8(dh * xv * r)

        @pl.when(i == steps - 1)
        def _():
            dg_ref[...] = jnp.sum(acc_ref[...], axis=0, keepdims=True)

    return pl.pallas_call(
        body, name="matmul_nt_norm_bwd", grid=(steps,),
        in_specs=[pl.BlockSpec((tm, n), lambda i: (i, 0)), _layer_spec(wg, layer), pl.BlockSpec((tm, d), lambda i: (i, 0)),
                  _const_spec((1, d)), pl.BlockSpec((tm, d), lambda i: (i, 0))],
        out_specs=[pl.BlockSpec((tm, d), lambda i: (i, 0)), pl.BlockSpec((1, d), lambda i: (0, 0))],
        out_shape=[jax.ShapeDtypeStruct((t, d), F32), jax.ShapeDtypeStruct((1, d), F32)],
        scratch_shapes=[pltpu.VMEM((8, d), F32)] + ([] if aligned else [pltpu.VMEM((d, n), MM_DTYPE)]),
        compiler_params=_params(("arbitrary",)),
    )(gy, wg, x, g, dres)


_ANY = pl.BlockSpec(memory_space=pl.ANY)


def matmul_tn_blocks(a, b, buf, layer, tk, by_rows):
    t = a.shape[0]
    _, _, r, c = buf.shape
    assert t % tk == 0 and r % 8 == 0 and c % 128 == 0, (a.shape, b.shape, buf.shape, tk)

    def body(a_ref, b_ref, buf_ref, o_ref):
        del buf_ref

        @pl.when(pl.program_id(1) == 0)
        def _():
            o_ref[...] = jnp.zeros_like(o_ref)

        o_ref[...] += _dot_tn(a_ref[...], b_ref[...])

    a_spec = pl.BlockSpec((tk, r), (lambda j, k: (k, j)) if by_rows else (lambda j, k: (k, 0)))
    b_spec = pl.BlockSpec((tk, c), (lambda j, k: (k, 0)) if by_rows else (lambda j, k: (k, j)))
    return pl.pallas_call(
        body, name="matmul_tn_blocks", grid=(N_CHIPS, t // tk), in_specs=[a_spec, b_spec, _ANY],
        out_specs=pl.BlockSpec((None, None, r, c), lambda j, k: (j, layer, 0, 0)),
        out_shape=jax.ShapeDtypeStruct(buf.shape, F32), input_output_aliases={2: 0},
        compiler_params=_params(("arbitrary", "arbitrary")),
    )(a, b, buf)


def matmul_tn_down(act, dd, buf, layer, tk):
    t = act.shape[0]
    _, _, _, r, c = buf.shape
    assert t % tk == 0, (t, tk)
    steps = t // tk

    def body(a_ref, b_ref, buf_ref, o_ref, acc):
        del buf_ref
        k = pl.program_id(1)

        @pl.when(k == 0)
        def _():
            acc[...] = jnp.zeros_like(acc)

        acc[...] += _dot_tn(a_ref[...], b_ref[...])

        @pl.when(k == steps - 1)
        def _():
            o_ref[0] = acc[0:r, :]
            o_ref[1] = acc[r:2 * r, :]

    return pl.pallas_call(
        body, name="matmul_tn_down", grid=(2, steps),
        in_specs=[pl.BlockSpec((tk, 2 * r), lambda p, k: (k, p)), pl.BlockSpec((tk, c), lambda p, k: (k, 0)), _ANY],
        out_specs=pl.BlockSpec((None, 2, None, r, c), lambda p, k: (p, 0, layer, 0, 0)),
        out_shape=jax.ShapeDtypeStruct(buf.shape, F32), input_output_aliases={2: 0},
        scratch_shapes=[pltpu.VMEM((2 * r, c), F32)],
        compiler_params=_params(("arbitrary", "arbitrary")),
    )(act, dd, buf)


def matmul_tn_in(h, dz, buf, layer, tk):
    t, d = h.shape
    n = dz.shape[1]
    _, _, r, c = buf.shape
    assert t % tk == 0 and r == d and N_CHIPS * c == n, (h.shape, dz.shape, buf.shape)
    steps = t // tk

    def body(a_ref, b_ref, buf_ref, o_ref, acc):
        del buf_ref
        k = pl.program_id(0)

        @pl.when(k == 0)
        def _():
            acc[...] = jnp.zeros_like(acc)

        acc[...] += _dot_tn(a_ref[...], b_ref[...])

        @pl.when(k == steps - 1)
        def _():
            for j in range(N_CHIPS):
                o_ref[j] = acc[:, c * j:c * (j + 1)]

    return pl.pallas_call(
        body, name="matmul_tn_in", grid=(steps,),
        in_specs=[pl.BlockSpec((tk, d), lambda k: (k, 0)), pl.BlockSpec((tk, n), lambda k: (k, 0)), _ANY],
        out_specs=pl.BlockSpec((N_CHIPS, None, r, c), lambda k: (0, layer, 0, 0)),
        out_shape=jax.ShapeDtypeStruct(buf.shape, F32), input_output_aliases={2: 0},
        scratch_shapes=[pltpu.VMEM((d, n), F32)],
        compiler_params=_params(("arbitrary",)),
    )(h, dz, buf)


def to_tiles(a, tt):
    t = a.shape[0]
    return a.reshape((t // tt, 8, tt // 8) + a.shape[1:]).swapaxes(1, 2).reshape(a.shape)


def from_tiles(a, tt):
    t = a.shape[0]
    return a.reshape((t // tt, tt // 8, 8) + a.shape[1:]).swapaxes(1, 2).reshape(a.shape)


def _roll_sublanes(a, shift):
    n = a.shape[0] // 8
    return pltpu.roll(a.reshape(n, 8, a.shape[1]), shift, 1).reshape(a.shape)


def _halo_before(cur_last, prev_last):
    sub = lax.broadcasted_iota(jnp.int32, cur_last.shape, 0) % 8
    return jnp.where(sub == 0, _roll_sublanes(prev_last, 1), _roll_sublanes(cur_last, 1))


def _halo_after(cur_first, next_first):
    sub = lax.broadcasted_iota(jnp.int32, cur_first.shape, 0) % 8
    return jnp.where(sub == 7, _roll_sublanes(next_first, 7), _roll_sublanes(cur_first, 7))


def _conv_causal(ext, cur, prev_last, w, taps, tt, cols=None):
    hr = 8 * (taps - 1)
    cs = slice(None) if cols is None else cols
    ext[hr:hr + tt, cs] = cur
    ext[0:hr, cs] = _halo_before(cur[tt - hr:, :], prev_last)
    acc = w[0:1, :] * ext[0:tt, cs]
    for k in range(1, taps):
        acc = acc + w[k:k + 1, :] * ext[8 * k:8 * k + tt, cs]
    return acc


def _conv_anticausal(ext, cur, next_first, w, taps, tt, x=None, acc_w=None, cols=None):
    hr = 8 * (taps - 1)
    cs = slice(None) if cols is None else cols
    ext[0:tt, cs] = cur
    ext[tt:tt + hr, cs] = _halo_after(cur[0:hr, :], next_first)
    acc = None
    for k in range(taps):
        off = 8 * (taps - 1 - k)
        ld = ext[off:off + tt, cs]
        term = w[k:k + 1, :] * ld
        acc = term if acc is None else acc + term
        if x is not None:
            acc_w[k, :, cs] += _rowsum8(ld * x)
    return acc


def _project_rows(y, w_ref):
    r = w_ref.shape[1]
    acc = _dot(y[:, 0:r], w_ref[0])
    for j in range(1, N_CHIPS):
        acc = acc + _dot(y[:, r * j:r * (j + 1)], w_ref[j])
    return acc


def _head_select(parts):
    head = lax.broadcasted_iota(jnp.int32, parts[0].shape, 1) // HEAD
    acc = parts[0]
    for h in range(1, N_HEADS_B):
        acc = jnp.where(head == h, parts[h], acc)
    return acc


def _mixer_forward(z, prm, q, yc):
    _, lng, lnb, wm, bias_p, _, _, clg, clb = prm
    bg = z[:, 0:D_A]
    ya = bg * q
    o_b = 3 * D_A
    zu = z[:, o_b:o_b + D_B]
    zv = z[:, o_b + D_B:o_b + 2 * D_B]
    u = _gelu(zu)
    vh, rv = _ln_fwd(_gelu(zv))
    vnb = (vh * lng + lnb).astype(MM_DTYPE)
    s = _head_select([_dot(wm[h], vnb) for h in range(N_HEADS_B)]) + bias_p
    yb = u * s
    yh, rc = _ln_fwd(yc)
    l = yh * clg + clb
    sl = jax.nn.sigmoid(l)
    return dict(bg=bg, q=q, ya=ya, zu=zu, zv=zv, u=u, vh=vh, rv=rv, vnb=vnb, s=s, yb=yb, yh=yh, rc=rc, l=l, sl=sl,
                yo=l * sl)


def _conv_inputs(z):
    o_c = 3 * D_A + 2 * D_B
    a = z[:, o_c:o_c + D_C]
    sg = jax.nn.sigmoid(z[:, o_c + D_C:o_c + 2 * D_C])
    return z[:, D_A:2 * D_A] * z[:, 2 * D_A:3 * D_A], a * sg, a, sg


def _group_norm(f, gg):
    ya, yb, yo = f["ya"], f["yb"], f["yo"]
    ra, rb, ro = _rstd(ya), _rstd(yb), _rstd(yo)
    yn = jnp.concatenate([ya * ra * gg[:, 0:D_A], yb * rb * gg[:, D_A:D_A + D_B], yo * ro * gg[:, D_A + D_B:]], axis=1)
    return yn, (ra, rb, ro)


def _mixer_prm(refs):
    caw_ref, lng_ref, lnb_ref, wm_ref, bias_ref, ccw_ref, ccb_ref, clg_ref, clb_ref = refs
    wm = [wm_ref[h] for h in range(N_HEADS_B)]
    return (caw_ref[...], lng_ref[...], lnb_ref[...], wm, bias_ref[...], ccw_ref[...], ccb_ref[...], clg_ref[...],
            clb_ref[...])


def _mixer_param_specs(tt):
    return [_const_spec((8, D_A)), _const_spec((1, D_B)), _const_spec((1, D_B)), _const_spec((N_HEADS_B, tt, tt)),
            _const_spec((tt, D_B)), _const_spec((32, D_C)), _const_spec((1, D_C)), _const_spec((1, D_C)),
            _const_spec((1, D_C))]


HR_A = 8 * (K_A - 1)
HR_C = 8 * (K_C - 1)
HR_F = 8 * (K_F - 1)


def mixer_fwd(z, x, mp, grp_g, wog, layer, post_g, tt):
    t = z.shape[0]
    assert t % tt == 0 and tt % CHUNK == 0 and tt >= HR_C, (t, tt)

    def body(z_ref, x_ref, *rest):
        prm_refs, (gg_ref, wo_ref, pg_ref, o_ref, x1_ref, cv_ref, pa_ext, yg_ext, pa_last, yg_last) = rest[:9], rest[9:]
        i = pl.program_id(0)

        @pl.when(i == 0)
        def _():
            pa_last[...] = jnp.zeros_like(pa_last)
            yg_last[...] = jnp.zeros_like(yg_last)

        zv = z_ref[...]
        prm = _mixer_prm(prm_refs)
        pa, yg, _, _ = _conv_inputs(zv)
        q = _conv_causal(pa_ext, pa, pa_last[...], prm[0], K_A, tt)
        yc = _conv_causal(yg_ext, yg, yg_last[...], prm[5], K_C, tt) + prm[6]
        pa_last[...] = pa[tt - HR_A:, :]
        yg_last[...] = yg[tt - HR_C:, :]
        cv_ref[:, 0:D_A] = q
        cv_ref[:, D_A:] = yc
        f = _mixer_forward(zv, prm, q, yc)
        yn, _ = _group_norm(f, gg_ref[...])
        o = _project_rows(yn.astype(MM_DTYPE), wo_ref)
        o_ref[...] = o
        x1_ref[...] = x_ref[...] + o * _rstd(o) * pg_ref[...]

    row = lambda c: pl.BlockSpec((tt, c), lambda i: (i, 0))
    return pl.pallas_call(
        body, name="mixer_fwd", grid=(t // tt,),
        in_specs=[row(D_IN), row(D_MODEL)] + _mixer_param_specs(tt)
        + [_const_spec((1, D_MODEL)), _layer_spec(wog, layer), _const_spec((1, D_MODEL))],
        out_specs=[row(D_MODEL), row(D_MODEL), row(D_A + D_C)],
        out_shape=[jax.ShapeDtypeStruct((t, D_MODEL), F32), jax.ShapeDtypeStruct((t, D_MODEL), F32),
                   jax.ShapeDtypeStruct((t, D_A + D_C), F32)],
        scratch_shapes=[pltpu.VMEM((HR_A + tt, D_A), F32), pltpu.VMEM((HR_C + tt, D_C), F32),
                        pltpu.VMEM((HR_A, D_A), F32), pltpu.VMEM((HR_C, D_C), F32)],
        compiler_params=_params(("arbitrary",)),
    )(z, x, *mp, grp_g, wog, post_g)


def mixer_bwd(dx1, o, z, cv, mp, wmt, grp_g, wog, layer, post_g, tt):
    t = z.shape[0]
    assert t % tt == 0 and tt % CHUNK == 0 and tt >= HR_C, (t, tt)
    steps = t // tt

    def body(dx1_ref, o_ref, z_ref, cv_ref, *rest):
        prm_refs = rest[:9]
        (wmt_ref, gg_ref, wo_ref, pg_ref,
         dz_ref, do_ref, yn_ref, dpg_ref, dgg_ref, dcaw_ref, dlng_ref, dlnb_ref, dwm_ref, dbias_ref, dccw_ref, dccb_ref,
         dclg_ref, dclb_ref,
         dq_ext, dyc_ext, dq_first, dyc_first, a_pg, a_gg, a_caw, a_lng, a_lnb, a_ccw, a_ccb, a_clg, a_clb) = rest[9:]
        i = pl.program_id(0)
        prm = _mixer_prm(prm_refs)
        caw, lng, lnb, wm, bias_p, ccw, ccb, clg, clb = prm
        small = (a_pg, a_gg, a_caw, a_lng, a_lnb, a_ccw, a_ccb, a_clg, a_clb)

        @pl.when(i == 0)
        def _():
            for ref in small + (dwm_ref, dbias_ref, dq_first, dyc_first):
                ref[...] = jnp.zeros_like(ref)

        zv = z_ref[...]
        pa, yg, a, sg = _conv_inputs(zv)
        f = _mixer_forward(zv, prm, cv_ref[:, 0:D_A], cv_ref[:, D_A:])
        gg = gg_ref[...]
        yn, (ra, rb, ro) = _group_norm(f, gg)
        yn_ref[...] = yn.astype(MM_DTYPE)

        ov = o_ref[...]
        dx1v = dx1_ref[...]
        r_o = _rstd(ov)
        pg = pg_ref[...]
        a_pg[...] += _rowsum8(dx1v * ov * r_o)
        do = _rms_bwd(ov, r_o, pg, dx1v).astype(MM_DTYPE)
        do_ref[...] = do
        dyn = jnp.concatenate([_dot_nt(do, wo_ref[j]) for j in range(N_CHIPS)], axis=1)

        dyn_a, dyn_b, dyn_c = dyn[:, 0:D_A], dyn[:, D_A:D_A + D_B], dyn[:, D_A + D_B:]
        ga, gb, gc = gg[:, 0:D_A], gg[:, D_A:D_A + D_B], gg[:, D_A + D_B:]
        a_gg[...] += _rowsum8(jnp.concatenate([dyn_a * f["ya"] * ra, dyn_b * f["yb"] * rb, dyn_c * f["yo"] * ro], axis=1))
        dya = _rms_bwd(f["ya"], ra, ga, dyn_a)
        dyb = _rms_bwd(f["yb"], rb, gb, dyn_b)
        dyo = _rms_bwd(f["yo"], ro, gc, dyn_c)

        dbg = dya * f["q"]
        dq = dya * f["bg"]
        dp = _conv_anticausal(dq_ext, dq, dq_first[...], caw, K_A, tt, x=pa, acc_w=a_caw)
        dq_first[...] = dq[0:HR_A, :]
        dcg = dp * zv[:, 2 * D_A:3 * D_A]
        dxa = dp * zv[:, D_A:2 * D_A]

        du = dyb * f["s"]
        ds = dyb * f["u"]
        dsb = ds.astype(MM_DTYPE)
        head = lax.broadcasted_iota(jnp.int32, (tt, D_B), 1) // HEAD
        dbias_ref[...] += ds
        parts = []
        for h in range(N_HEADS_B):
            dwm_ref[h] += _dot_nt(jnp.where(head == h, dsb, jnp.zeros_like(dsb)), f["vnb"])
            parts.append(_dot(wmt_ref[h], dsb))
        dvn = _head_select(parts)
        a_lng[...] += _rowsum8(dvn * f["vh"])
        a_lnb[...] += _rowsum8(dvn)
        dv = _ln_bwd(f["vh"], f["rv"], dvn * lng)
        dzu = du * _gelu_grad(f["zu"])
        dzv = dv * _gelu_grad(f["zv"])

        l, sl = f["l"], f["sl"]
        dl = dyo * (sl * (1.0 + l * (1.0 - sl)))
        a_clg[...] += _rowsum8(dl * f["yh"])
        a_clb[...] += _rowsum8(dl)
        dyc = _ln_bwd(f["yh"], f["rc"], dl * clg)
        a_ccb[...] += _rowsum8(dyc)
        dy = _conv_anticausal(dyc_ext, dyc, dyc_first[...], ccw, K_C, tt, x=yg, acc_w=a_ccw)
        dyc_first[...] = dyc[0:HR_C, :]
        da = dy * sg
        dg = dy * a * sg * (1.0 - sg)

        dz_ref[...] = jnp.concatenate([dbg, dcg, dxa, dzu, dzv, da, dg], axis=1).astype(MM_DTYPE)

        @pl.when(i == steps - 1)
        def _():
            red = lambda ref: jnp.sum(ref[...], axis=0, keepdims=True)
            dpg_ref[...] = red(a_pg)
            dgg_ref[...] = red(a_gg)
            dlng_ref[...] = red(a_lng)
            dlnb_ref[...] = red(a_lnb)
            dccb_ref[...] = red(a_ccb)
            dclg_ref[...] = red(a_clg)
            dclb_ref[...] = red(a_clb)
            dcaw_ref[...] = jnp.sum(a_caw[...], axis=1)
            dccw_ref[...] = jnp.sum(a_ccw[...], axis=1)

    rev = lambda c: pl.BlockSpec((tt, c), lambda i: (steps - 1 - i, 0))
    full = lambda shape: pl.BlockSpec(shape, lambda i: (0,) * len(shape))
    sds = jax.ShapeDtypeStruct
    outs = pl.pallas_call(
        body, name="mixer_bwd", grid=(steps,),
        in_specs=[rev(D_MODEL), rev(D_MODEL), rev(D_IN), rev(D_A + D_C)] + _mixer_param_specs(tt)
        + [_const_spec((N_HEADS_B, tt, tt)), _const_spec((1, D_MODEL)), _layer_spec(wog, layer),
           _const_spec((1, D_MODEL))],
        out_specs=[rev(D_IN), rev(D_MODEL), rev(D_MODEL), full((1, D_MODEL)), full((1, D_MODEL)), full((8, D_A)),
                   full((1, D_B)), full((1, D_B)), full((N_HEADS_B, tt, tt)), full((tt, D_B)), full((32, D_C)),
                   full((1, D_C)), full((1, D_C)), full((1, D_C))],
        out_shape=[sds((t, D_IN), MM_DTYPE), sds((t, D_MODEL), MM_DTYPE), sds((t, D_MODEL), MM_DTYPE),
                   sds((1, D_MODEL), F32), sds((1, D_MODEL), F32), sds((8, D_A), F32), sds((1, D_B), F32), sds((1, D_B), F32),
                   sds((N_HEADS_B, tt, tt), F32), sds((tt, D_B), F32), sds((32, D_C), F32), sds((1, D_C), F32),
                   sds((1, D_C), F32), sds((1, D_C), F32)],
        scratch_shapes=[pltpu.VMEM((tt + HR_A, D_A), F32), pltpu.VMEM((tt + HR_C, D_C), F32),
                        pltpu.VMEM((HR_A, D_A), F32), pltpu.VMEM((HR_C, D_C), F32),
                        pltpu.VMEM((8, D_MODEL), F32), pltpu.VMEM((8, D_MODEL), F32), pltpu.VMEM((8, 8, D_A), F32),
                        pltpu.VMEM((8, D_B), F32), pltpu.VMEM((8, D_B), F32), pltpu.VMEM((32, 8, D_C), F32),
                        pltpu.VMEM((8, D_C), F32), pltpu.VMEM((8, D_C), F32), pltpu.VMEM((8, D_C), F32)],
        compiler_params=_params(("arbitrary",)),
    )(dx1, o, z, cv, *mp, wmt, grp_g, wog, post_g)
    return outs


def _fetch_row_blocks(wg_ref, layer, w_scr, sems):
    r = wg_ref.shape[2]
    copies = [pltpu.make_async_copy(wg_ref.at[j, layer], w_scr.at[pl.ds(r * j, r), :], sems.at[j]) for j in range(N_CHIPS)]
    for cp in copies:
        cp.start()
    for cp in copies:
        cp.wait()


def _ffn_conv(ext, cw, c0, cn, tt):
    acc = cw[0:1, c0:c0 + cn] * ext[0:tt, c0:c0 + cn]
    for k in range(1, K_F):
        acc = acc + cw[k:k + 1, c0:c0 + cn] * ext[8 * k:8 * k + tt, c0:c0 + cn]
    return acc


def ffn_fwd(up0, x1, cw, wdg, layer, post_g, tt):
    t = up0.shape[0]
    assert t % tt == 0, (t, tt)
    cn = _col_chunk(D_FF)

    def body(up0_ref, x1_ref, cw_ref, wdg_ref, pg_ref, d_ref, x2_ref, ext, last, wd_ref, sems):
        i = pl.program_id(0)

        @pl.when(i == 0)
        def _():
            _fetch_row_blocks(wdg_ref, layer, wd_ref, sems)
            last[...] = jnp.zeros_like(last)

        ext[HR_F:HR_F + tt, :] = up0_ref[...]
        ext[0:HR_F, :] = _halo_before(up0_ref[tt - HR_F:, :], last[...])
        last[...] = up0_ref[tt - HR_F:, :]
        cwv = cw_ref[...]
        d = jnp.zeros((tt, D_MODEL), F32)
        for c0 in range(0, D_FF, cn):
            gate = _ffn_conv(ext, cwv, c0, cn, tt)
            val = _ffn_conv(ext, cwv, D_FF + c0, cn, tt)
            act = (gate * jax.nn.sigmoid(gate) * val).astype(MM_DTYPE)
            d = d + _dot(act, wd_ref[c0:c0 + cn, :])
        d_ref[...] = d
        x2_ref[...] = x1_ref[...] + d * _rstd(d) * pg_ref[...]

    row = lambda c: pl.BlockSpec((tt, c), lambda i: (i, 0))
    return pl.pallas_call(
        body, name="ffn_fwd", grid=(t // tt,),
        in_specs=[row(2 * D_FF), row(D_MODEL), _const_spec((8, 2 * D_FF)), _ANY, _const_spec((1, D_MODEL))],
        out_specs=[row(D_MODEL), row(D_MODEL)],
        out_shape=[jax.ShapeDtypeStruct((t, D_MODEL), F32), jax.ShapeDtypeStruct((t, D_MODEL), F32)],
        scratch_shapes=[pltpu.VMEM((HR_F + tt, 2 * D_FF), F32), pltpu.VMEM((HR_F, 2 * D_FF), F32),
                        pltpu.VMEM((D_FF, D_MODEL), MM_DTYPE), pltpu.SemaphoreType.DMA((N_CHIPS,))],
        compiler_params=_params(("arbitrary",)),
    )(up0, x1, cw, wdg, post_g)


def ffn_bwd(dx2, d, up0, cw, wdg, layer, post_g, tt):
    t = up0.shape[0]
    assert t % tt == 0, (t, tt)
    steps = t // tt
    hb = tt // HR_F
    cn = _col_chunk(D_FF)

    def body(dx2_ref, d_ref, up0_ref, uh_ref, cw_ref, wdg_ref, pg_ref,
             dd_ref, act_ref, dup0_ref, dpg_ref, dcw_ref, ext, dup_ext, first, a_pg, a_cw, wd_ref, sems):
        i = pl.program_id(0)
        tile = steps - 1 - i

        @pl.when(i == 0)
        def _():
            _fetch_row_blocks(wdg_ref, layer, wd_ref, sems)
            a_pg[...] = jnp.zeros_like(a_pg)
            a_cw[...] = jnp.zeros_like(a_cw)
            first[...] = jnp.zeros_like(first)

        ext[HR_F:HR_F + tt, :] = up0_ref[...]
        ext[0:HR_F, :] = _halo_before(up0_ref[tt - HR_F:, :], jnp.where(tile > 0, uh_ref[...], 0.0))
        cwv = cw_ref[...]
        dv = d_ref[...]
        dx2v = dx2_ref[...]
        r = _rstd(dv)
        a_pg[...] += _rowsum8(dx2v * dv * r)
        dd = _rms_bwd(dv, r, pg_ref[...], dx2v).astype(MM_DTYPE)
        dd_ref[...] = dd
        for c0 in range(0, D_FF, cn):
            gate = _ffn_conv(ext, cwv, c0, cn, tt)
            val = _ffn_conv(ext, cwv, D_FF + c0, cn, tt)
            sg = jax.nn.sigmoid(gate)
            sl = gate * sg
            act_ref[:, c0:c0 + cn] = (sl * val).astype(MM_DTYPE)
            da = _dot_nt(dd, wd_ref[c0:c0 + cn, :])
            dup_ext[0:tt, c0:c0 + cn] = da * val * (sg * (1.0 + gate * (1.0 - sg)))
            dup_ext[0:tt, D_FF + c0:D_FF + c0 + cn] = da * sl
        dup_ext[tt:tt + HR_F, :] = _halo_after(dup_ext[0:HR_F, :], first[...])
        first[...] = dup_ext[0:HR_F, :]
        for c0 in range(0, 2 * D_FF, cn):
            x = up0_ref[:, c0:c0 + cn]
            acc = None
            for k in range(K_F):
                off = 8 * (K_F - 1 - k)
                ld = dup_ext[off:off + tt, c0:c0 + cn]
                term = cwv[k:k + 1, c0:c0 + cn] * ld
                acc = term if acc is None else acc + term
                a_cw[k, :, c0:c0 + cn] += _rowsum8(ld * x)
            dup0_ref[:, c0:c0 + cn] = acc.astype(MM_DTYPE)

        @pl.when(i == steps - 1)
        def _():
            dpg_ref[...] = jnp.sum(a_pg[...], axis=0, keepdims=True)
            dcw_ref[...] = jnp.sum(a_cw[...], axis=1)

    rev = lambda c: pl.BlockSpec((tt, c), lambda i: (steps - 1 - i, 0))
    halo = pl.BlockSpec((HR_F, 2 * D_FF), lambda i: (jnp.maximum((steps - 1 - i) * hb - 1, 0), 0))
    full = lambda shape: pl.BlockSpec(shape, lambda i: (0,) * len(shape))
    sds = jax.ShapeDtypeStruct
    return pl.pallas_call(
        body, name="ffn_bwd", grid=(steps,),
        in_specs=[rev(D_MODEL), rev(D_MODEL), rev(2 * D_FF), halo, _const_spec((8, 2 * D_FF)), _ANY,
                  _const_spec((1, D_MODEL))],
        out_specs=[rev(D_MODEL), rev(D_FF), rev(2 * D_FF), full((1, D_MODEL)), full((8, 2 * D_FF))],
        out_shape=[sds((t, D_MODEL), MM_DTYPE), sds((t, D_FF), MM_DTYPE), sds((t, 2 * D_FF), MM_DTYPE),
                   sds((1, D_MODEL), F32), sds((8, 2 * D_FF), F32)],
        scratch_shapes=[pltpu.VMEM((HR_F + tt, 2 * D_FF), F32), pltpu.VMEM((tt + HR_F, 2 * D_FF), F32),
                        pltpu.VMEM((HR_F, 2 * D_FF), F32), pltpu.VMEM((8, D_MODEL), F32),
                        pltpu.VMEM((8, 8, 2 * D_FF), F32), pltpu.VMEM((D_FF, D_MODEL), MM_DTYPE),
                        pltpu.SemaphoreType.DMA((N_CHIPS,))],
        compiler_params=_params(("arbitrary",)),
    )(dx2, d, up0, up0, cw, wdg, post_g)


def loss_head(y, target, tm):
    t, d = y.shape
    assert t % tm == 0, (t, tm)
    steps = t // tm

    def body(y_ref, t_ref, dy_ref, loss_ref, acc):
        i = pl.program_id(0)

        @pl.when(i == 0)
        def _():
            acc[...] = jnp.zeros_like(acc)

        diff = y_ref[...] - t_ref[...]
        dy_ref[...] = diff * (1.0 / d)
        acc[...] += _rowsum8(diff * diff)

        @pl.when(i == steps - 1)
        def _():
            loss_ref[...] = (0.5 / d) * jnp.sum(jnp.sum(acc[...], axis=0, keepdims=True), axis=1, keepdims=True)

    row = pl.BlockSpec((tm, d), lambda i: (i, 0))
    return pl.pallas_call(
        body, name="loss_head", grid=(steps,), in_specs=[row, row],
        out_specs=[row, pl.BlockSpec((1, 1), lambda i: (0, 0))],
        out_shape=[jax.ShapeDtypeStruct((t, d), F32), jax.ShapeDtypeStruct((1, 1), F32)],
        scratch_shapes=[pltpu.VMEM((8, d), F32)],
        compiler_params=_params(("arbitrary",)),
    )(y, target)


def adamw(w, g, m, v):
    shape = w.shape
    cols = shape[-1]
    rows = w.size // cols
    tr = next((r for r in (512, 256, 128) if rows % r == 0 and rows > r), rows)
    c1 = 1.0 - ADAM_B1 ** ADAM_STEP
    c2 = 1.0 - ADAM_B2 ** ADAM_STEP

    def body(w_ref, g_ref, m_ref, v_ref, d_ref, nm_ref, nv_ref):
        gv = g_ref[...]
        nm = ADAM_B1 * m_ref[...] + (1.0 - ADAM_B1) * gv
        nv = ADAM_B2 * v_ref[...] + (1.0 - ADAM_B2) * (gv * gv)
        nm_ref[...] = nm
        nv_ref[...] = nv
        d_ref[...] = -ADAM_LR * ((nm / c1) / (jnp.sqrt(nv / c2) + ADAM_EPS) + ADAM_WD * w_ref[...])

    spec = pl.BlockSpec((tr, cols), lambda i: (i, 0))
    out = jax.ShapeDtypeStruct((rows, cols), F32)
    res = pl.pallas_call(
        body, name="adamw", grid=(rows // tr,), in_specs=[spec] * 4, out_specs=[spec] * 3, out_shape=[out] * 3,
        compiler_params=_params(("arbitrary",)),
    )(*[a.reshape(rows, cols) for a in (w, g, m, v)])
    return tuple(r.reshape(shape) for r in res)


HALF = DEPTH // 2


def _place():
    return lax.axis_index("x"), lax.axis_index("y"), lax.axis_index("c")


def _other_chips(x, y):
    return [(1 - x, y, 2 * (1 - x) + y), (x, 1 - y, 2 * x + 1 - y), (1 - x, 1 - y, 2 * (1 - x) + 1 - y)]


def _sem_specs(*counts):
    return [pltpu.SemaphoreType.DMA((n,)) for n in counts]


def cast_shard(w, chip):
    l, r, c = w.shape

    def body(chip_ref, w_ref, o_ref):
        del chip_ref
        o_ref[...] = w_ref[...].astype(MM_DTYPE)

    grid_spec = pltpu.PrefetchScalarGridSpec(
        num_scalar_prefetch=1, grid=(l,), in_specs=[pl.BlockSpec((None, r, c), lambda i, chip_ref: (i, 0, 0))],
        out_specs=pl.BlockSpec((None, None, r, c), lambda i, chip_ref: (chip_ref[0], i, 0, 0)))
    return pl.pallas_call(
        body, name="cast_shard", grid_spec=grid_spec, out_shape=jax.ShapeDtypeStruct((N_CHIPS, l, r, c), MM_DTYPE),
        compiler_params=_params(("arbitrary",)),
    )(jnp.reshape(chip, (1,)).astype(jnp.int32), w)


def gather_weights(shards, small):
    n = len(shards)

    def body(*refs):
        small_ref = refs[n]
        outs = refs[n + 1:2 * n + 1]
        smalls_ref = refs[2 * n + 1]
        send, recv, fsend, frecv, ssend, srecv, local = refs[2 * n + 2:]
        x, y, c = _place()
        me = 2 * x + y
        chips = _other_chips(x, y)
        mine = pl.ds(c * HALF, HALF)
        theirs = pl.ds((1 - c) * HALF, HALF)

        own_small = pltpu.make_async_copy(small_ref, smalls_ref.at[me], local.at[0])
        own_small.start()

        def block(ref, chip, layers, sems, k, to):
            part = ref.at[chip, layers]
            return pltpu.make_async_remote_copy(src_ref=part, dst_ref=part, send_sem=sems[0].at[k], recv_sem=sems[1].at[k],
                                                device_id=to, device_id_type=MESH_ID)

        def small_copy(k, chip, to):
            return pltpu.make_async_remote_copy(src_ref=small_ref, dst_ref=smalls_ref.at[chip], send_sem=ssend.at[k],
                                                recv_sem=srecv.at[k], device_id=to, device_id_type=MESH_ID)

        sends = []
        for k, (px, py, _) in enumerate(chips):
            for i, ref in enumerate(outs):
                sends.append(block(ref, me, mine, (send, recv), n * k + i, (px, py, c)))
            sends.append(small_copy(k, me, (px, py, c)))
        for cp in sends:
            cp.start()
        forwards = []
        for k, (_, _, pj) in enumerate(chips):
            for i, ref in enumerate(outs):
                block(ref, pj, mine, (send, recv), n * k + i, (x, y, c)).wait_recv()
                fw = block(ref, pj, mine, (fsend, frecv), n * k + i, (x, y, 1 - c))
                fw.start()
                forwards.append(fw)
        for k, (_, _, pj) in enumerate(chips):
            small_copy(k, pj, (x, y, c)).wait_recv()
            for i, ref in enumerate(outs):
                block(ref, pj, theirs, (fsend, frecv), n * k + i, (x, y, 1 - c)).wait_recv()
        for cp in sends + forwards:
            cp.wait_send()
        own_small.wait()

    res = pl.pallas_call(
        body, name="gather_weights", in_specs=[_ANY] * (n + 1), out_specs=[_ANY] * (n + 1),
        out_shape=[jax.ShapeDtypeStruct(s.shape, s.dtype) for s in shards]
        + [jax.ShapeDtypeStruct((N_CHIPS,) + small.shape, small.dtype)],
        input_output_aliases={i: i for i in range(n)},
        scratch_shapes=_sem_specs(3 * n, 3 * n, 3 * n, 3 * n, 3, 3, 1),
        compiler_params=pltpu.CompilerParams(has_side_effects=True),
    )(*shards, small)
    return res[:n], res[n]


def swap_halves(gs):
    n = len(gs)

    def body(*refs):
        g_refs, got_refs, (send, recv) = refs[:n], refs[n:2 * n], refs[2 * n:]
        x, y, c = _place()
        theirs = pl.ds((1 - c) * HALF, HALF)
        copies = [pltpu.make_async_remote_copy(
            src_ref=g.at[:, theirs], dst_ref=got, send_sem=send.at[i], recv_sem=recv.at[i],
            device_id=(x, y, 1 - c), device_id_type=MESH_ID) for i, (g, got) in enumerate(zip(g_refs, got_refs))]
        for cp in copies:
            cp.start()
        for cp in copies:
            cp.wait()

    return pl.pallas_call(
        body, name="swap_halves", in_specs=[_ANY] * n, out_specs=[_ANY] * n,
        out_shape=[jax.ShapeDtypeStruct((g.shape[0], HALF) + g.shape[2:], g.dtype) for g in gs],
        scratch_shapes=_sem_specs(n, n), compiler_params=pltpu.CompilerParams(has_side_effects=True),
    )(*gs)


def _row_tile(r):
    return next(t for t in (256, 352, 128) if r % t == 0)


def add_halves(g, got):
    n, _, r, cols = g.shape
    tr = _row_tile(r)
    c = lax.axis_index("c")

    def body(c_ref, g_ref, got_ref, s_ref, sb_ref):
        del c_ref
        s = g_ref[...] + got_ref[...]
        s_ref[...] = s
        sb_ref[...] = s.astype(BF16)

    blk = (None, None, tr, cols)
    grid_spec = pltpu.PrefetchScalarGridSpec(
        num_scalar_prefetch=1, grid=(n, HALF, r // tr),
        in_specs=[pl.BlockSpec(blk, lambda j, l, i, c_ref: (j, c_ref[0] * HALF + l, i, 0)),
                  pl.BlockSpec(blk, lambda j, l, i, c_ref: (j, l, i, 0))],
        out_specs=[pl.BlockSpec(blk, lambda j, l, i, c_ref: (j, l, i, 0))] * 2)
    return pl.pallas_call(
        body, name="add_halves", grid_spec=grid_spec,
        out_shape=[jax.ShapeDtypeStruct(got.shape, F32), jax.ShapeDtypeStruct(got.shape, BF16)],
        compiler_params=_params(("arbitrary",) * 3),
    )(jnp.reshape(c, (1,)).astype(jnp.int32), g, got)


def scatter_to_chips(sbs):
    n = len(sbs)

    def body(*refs):
        sb_refs, got_refs, (send, recv) = refs[:n], refs[n:2 * n], refs[2 * n:]
        x, y, c = _place()
        me = 2 * x + y
        chips = _other_chips(x, y)
        sends = []
        for k, (px, py, pj) in enumerate(chips):
            for i, (sb, got) in enumerate(zip(sb_refs, got_refs)):
                sends.append(pltpu.make_async_remote_copy(
                    src_ref=sb.at[pj], dst_ref=got.at[me], send_sem=send.at[n * k + i], recv_sem=recv.at[n * k + i],
                    device_id=(px, py, c), device_id_type=MESH_ID))
        for cp in sends:
            cp.start()
        for k, (_, _, pj) in enumerate(chips):
            for i, (sb, got) in enumerate(zip(sb_refs, got_refs)):
                pltpu.make_async_remote_copy(
                    src_ref=sb.at[pj], dst_ref=got.at[pj], send_sem=send.at[n * k + i], recv_sem=recv.at[n * k + i],
                    device_id=(x, y, c), device_id_type=MESH_ID).wait_recv()
        for cp in sends:
            cp.wait_send()

    return pl.pallas_call(
        body, name="scatter_to_chips", in_specs=[_ANY] * n, out_specs=[_ANY] * n,
        out_shape=[jax.ShapeDtypeStruct(sb.shape, sb.dtype) for sb in sbs],
        scratch_shapes=_sem_specs(3 * n, 3 * n), compiler_params=pltpu.CompilerParams(has_side_effects=True),
    )(*sbs)


def add_chips(s, got):
    n, _, r, cols = s.shape
    tr = _row_tile(r)
    x, y, c = _place()
    me = 2 * x + y

    def body(p_ref, s_ref, g1_ref, g2_ref, g3_ref, o_ref):
        del p_ref
        o_ref[...] = s_ref[...] + g1_ref[...].astype(F32) + g2_ref[...].astype(F32) + g3_ref[...].astype(F32)

    blk = (None, None, tr, cols)

    def other(k):
        return pl.BlockSpec(blk, lambda l, i, p_ref: ((p_ref[0] + k) % n, l, i, 0))

    grid_spec = pltpu.PrefetchScalarGridSpec(
        num_scalar_prefetch=1, grid=(HALF, r // tr),
        in_specs=[pl.BlockSpec(blk, lambda l, i, p_ref: (p_ref[0], l, i, 0)), other(1), other(2), other(3)],
        out_specs=pl.BlockSpec((None, tr, cols), lambda l, i, p_ref: (p_ref[1] * HALF + l, i, 0)))
    return pl.pallas_call(
        body, name="add_chips", grid_spec=grid_spec, out_shape=jax.ShapeDtypeStruct((DEPTH, r, cols), F32),
        compiler_params=_params(("arbitrary",) * 2),
    )(jnp.stack([me, c]).astype(jnp.int32), s, got, got, got)


def join_halves(fs):
    n = len(fs)

    def body(*refs):
        f_refs, (send, recv) = refs[n:2 * n], refs[2 * n:]
        x, y, c = _place()
        mine = pl.ds(c * HALF, HALF)
        theirs = pl.ds((1 - c) * HALF, HALF)
        sends = [pltpu.make_async_remote_copy(
            src_ref=f.at[mine], dst_ref=f.at[mine], send_sem=send.at[i], recv_sem=recv.at[i],
            device_id=(x, y, 1 - c), device_id_type=MESH_ID) for i, f in enumerate(f_refs)]
        for cp in sends:
            cp.start()
        for i, f in enumerate(f_refs):
            pltpu.make_async_remote_copy(
                src_ref=f.at[theirs], dst_ref=f.at[theirs], send_sem=send.at[i], recv_sem=recv.at[i],
                device_id=(x, y, 1 - c), device_id_type=MESH_ID).wait_recv()
        for cp in sends:
            cp.wait_send()

    return pl.pallas_call(
        body, name="join_halves", in_specs=[_ANY] * n, out_specs=[_ANY] * n,
        out_shape=[jax.ShapeDtypeStruct(f.shape, f.dtype) for f in fs], input_output_aliases={i: i for i in range(n)},
        scratch_shapes=_sem_specs(n, n), compiler_params=pltpu.CompilerParams(has_side_effects=True),
    )(*fs)


def gather_all(v):
    def body(v_ref, o_ref, send, recv, local):
        x, y, c = _place()
        me = 4 * x + 2 * y + c
        own = pltpu.make_async_copy(v_ref, o_ref.at[me], local.at[0])
        own.start()
        sends = []
        k = 0
        for fx in (0, 1):
            for fy in (0, 1):
                for fc in (0, 1):
                    if fx or fy or fc:
                        cp = pltpu.make_async_remote_copy(
                            src_ref=v_ref, dst_ref=o_ref.at[me], send_sem=send.at[k], recv_sem=recv.at[k],
                            device_id=(x ^ fx, y ^ fy, c ^ fc), device_id_type=MESH_ID)
                        cp.start()
                        sends.append((cp, k, 4 * (x ^ fx) + 2 * (y ^ fy) + (c ^ fc)))
                        k += 1
        for cp, k, peer in sends:
            pltpu.make_async_remote_copy(
                src_ref=v_ref, dst_ref=o_ref.at[peer], send_sem=send.at[k], recv_sem=recv.at[k],
                device_id=(x, y, c), device_id_type=MESH_ID).wait_recv()
        for cp, _, _ in sends:
            cp.wait_send()
        own.wait()

    return pl.pallas_call(
        body, name="gather_all", in_specs=[_ANY], out_specs=_ANY,
        out_shape=jax.ShapeDtypeStruct((8,) + v.shape, v.dtype), scratch_shapes=_sem_specs(7, 7, 1),
        compiler_params=pltpu.CompilerParams(has_side_effects=True),
    )(v)


def sum_devices(v8, tr):
    n, rows, lanes = v8.shape
    assert rows % tr == 0, (rows, tr)

    def body(v_ref, o_ref):
        acc = v_ref[0]
        for j in range(1, n):
            acc = acc + v_ref[j]
        o_ref[...] = acc

    return pl.pallas_call(
        body, name="sum_devices", grid=(rows // tr,),
        in_specs=[pl.BlockSpec((n, tr, lanes), lambda i: (0, i, 0))], out_specs=pl.BlockSpec((tr, lanes), lambda i: (i, 0)),
        out_shape=jax.ShapeDtypeStruct((rows, lanes), F32), compiler_params=_params(("arbitrary",)),
    )(v8)


def _pack(arrays, rows):
    flat = jnp.concatenate([a.reshape(-1) for a in arrays])
    return jnp.pad(flat, (0, rows * LANES - flat.size)).reshape(rows, LANES)


def _unpack(buf, shapes):
    flat = buf.reshape(-1)
    out, at = [], 0
    for s in shapes:
        n = math.prod(s)
        out.append(flat[at:at + n].reshape(s))
        at += n
    return out


CONV_SHARDS = [(DEPTH, K_A, D_A // N_CHIPS), (DEPTH, K_C, D_C // N_CHIPS), (DEPTH, K_F, 2 * D_FF // N_CHIPS)]
CONV_ROWS = 32
SMALL_ROWS = 576


def _join_cols(g):
    n, l, r, c = g.shape
    return jnp.transpose(g, (1, 2, 0, 3)).reshape(l, r, n * c)


TILE_MM = 512
TILE_EW = 256


def _pad_rows(a, rows):
    return jnp.pad(a, ((0, rows - a.shape[0]), (0, 0)))


def _row(a):
    return a.reshape(1, -1)


def _rows_and_cols_to_tiles(w, tt):
    h = w.shape[0]
    return w.reshape(h, 8, tt // 8, 8, tt // 8).transpose(0, 2, 1, 4, 3).reshape(h, tt, tt)


def _rows_and_cols_from_tiles(w, tt):
    h = w.shape[0]
    return w.reshape(h, tt // 8, 8, tt // 8, 8).transpose(0, 2, 1, 4, 3).reshape(h, tt, tt)


def _layer_params(wl, tt):
    n = tt // CHUNK
    tril = jnp.tril(jnp.ones((CHUNK, CHUNK), bool))
    wm = jnp.where(tril[None], wl["sgu_w"], 0.0)
    eye = jnp.eye(n, dtype=F32)
    wt = (eye[None, :, None, :, None] * wm[:, None, :, None, :]).reshape(N_HEADS_B, tt, tt)
    wp = _rows_and_cols_to_tiles(wt, tt)
    bias_e = jnp.repeat(wl["sgu_b"].T, HEAD, axis=1)
    bias_p = to_tiles(jnp.tile(bias_e, (n, 1)), tt)
    mp = (_pad_rows(wl["conv_a_w"], 8), _row(wl["sgu_ln_g"]), _row(wl["sgu_ln_b"]), wp.astype(MM_DTYPE), bias_p,
          _pad_rows(wl["conv_c_w"], 32), _row(wl["conv_c_b"]), _row(wl["conv_ln_g"]), _row(wl["conv_ln_b"]))
    wpt = jnp.transpose(wp, (0, 2, 1)).astype(MM_DTYPE)
    return mp, wpt, tril


def _sgu_grads(dwp, dbias_p, tril, tt):
    n = tt // CHUNK
    dw = _rows_and_cols_from_tiles(dwp, tt).reshape(N_HEADS_B, n, CHUNK, n, CHUNK)
    dw = sum(dw[:, c, :, c, :] for c in range(n))
    db = jnp.sum(from_tiles(dbias_p, tt).reshape(n, CHUNK, N_HEADS_B, HEAD), axis=(0, 3)).T
    return jnp.where(tril[None], dw, 0.0), db


def layer_fwd(x, wl, gw, layer, tm=TILE_MM, tt=TILE_EW):
    mp, _, _ = _layer_params(wl, tt)
    z, h = norm_matmul(x, _row(wl["pre_mix_g"]), gw["w_in"], layer, tm)
    o, x1, cv = mixer_fwd(z, x, mp, _row(wl["grp_norm_g"]), gw["w_out"], layer, _row(wl["post_mix_g"]), tt)
    up0, h2 = norm_matmul(x1, _row(wl["pre_ffn_g"]), gw["w_up"], layer, tt)
    d, x2 = ffn_fwd(up0, x1, _pad_rows(wl["ffn_conv_w"], 8), gw["w_down"], layer, _row(wl["post_ffn_g"]), tt)
    return x2, dict(x=x, z=z, h=h, o=o, x1=x1, up0=up0, h2=h2, d=d, cv=cv)


def layer_bwd(dx2, wl, gw, layer, sv, gbuf, tm=TILE_MM, tt=TILE_EW):
    mp, wmt, tril = _layer_params(wl, tt)
    tk = min(512, dx2.shape[0])
    g = {}
    gbuf = dict(gbuf)
    dd, act, dup0, dpg, dcw = ffn_bwd(dx2, sv["d"], sv["up0"], _pad_rows(wl["ffn_conv_w"], 8), gw["w_down"], layer,
                                      _row(wl["post_ffn_g"]), tt)
    g["post_ffn_g"] = dpg[0]
    g["ffn_conv_w"] = dcw[:K_F]
    gbuf["w_down"] = matmul_tn_down(act, dd, gbuf["w_down"], layer, tk)
    gbuf["w_up"] = matmul_tn_blocks(sv["h2"], dup0, gbuf["w_up"], layer, tk, by_rows=False)
    dx1, dg = matmul_nt_norm_bwd(dup0, gw["w_up"], layer, sv["x1"], _row(wl["pre_ffn_g"]), dx2, tm)
    g["pre_ffn_g"] = dg[0]
    (dz, do, yn, dpg, dgg, dcaw, dlng, dlnb, dwm, dbias, dccw, dccb, dclg, dclb) = mixer_bwd(
        dx1, sv["o"], sv["z"], sv["cv"], mp, wmt, _row(wl["grp_norm_g"]), gw["w_out"], layer, _row(wl["post_mix_g"]), tt)
    g["post_mix_g"] = dpg[0]
    g["grp_norm_g"] = dgg[0]
    g["conv_a_w"] = dcaw[:K_A]
    g["sgu_ln_g"] = dlng[0]
    g["sgu_ln_b"] = dlnb[0]
    g["sgu_w"], g["sgu_b"] = _sgu_grads(dwm, dbias, tril, tt)
    g["conv_c_w"] = dccw[:K_C]
    g["conv_c_b"] = dccb[0]
    g["conv_ln_g"] = dclg[0]
    g["conv_ln_b"] = dclb[0]
    gbuf["w_out"] = matmul_tn_blocks(yn, do, gbuf["w_out"], layer, tk, by_rows=True)
    gbuf["w_in"] = matmul_tn_in(sv["h"], dz, gbuf["w_in"], layer, tk)
    dx, dg = matmul_nt_norm_bwd(dz, gw["w_in"], layer, sv["x"], _row(wl["pre_mix_g"]), dx1, tm)
    g["pre_mix_g"] = dg[0]
    return dx, g, gbuf


def grad_buffers():
    sds = lambda *s: lax.empty(s, F32)
    return dict(w_in=sds(N_CHIPS, DEPTH, D_MODEL, D_IN // N_CHIPS), w_out=sds(N_CHIPS, DEPTH, D_MODEL // N_CHIPS, D_MODEL),
                w_up=sds(N_CHIPS, DEPTH, D_MODEL, 2 * D_FF // N_CHIPS), w_down=sds(2, 2, DEPTH, D_FF // N_CHIPS, D_MODEL))


BIG = ["w_in", "w_out", "w_up", "w_down"]
CONV = ["conv_a_w", "conv_c_w", "ffn_conv_w"]
REPL = ["pre_mix_g", "sgu_ln_g", "sgu_ln_b", "sgu_w", "sgu_b", "conv_c_b", "conv_ln_g", "conv_ln_b", "grp_norm_g",
        "post_mix_g", "pre_ffn_g", "post_ffn_g"]
WEIGHTS = ["pre_mix_g", "w_in", "conv_a_w", "sgu_ln_g", "sgu_ln_b", "sgu_w", "sgu_b", "conv_c_w", "conv_c_b", "conv_ln_g",
           "conv_ln_b", "grp_norm_g", "w_out", "post_mix_g", "pre_ffn_g", "w_up", "ffn_conv_w", "w_down", "post_ffn_g"]


def kernel(x, pre_mix_g, w_in, conv_a_w, sgu_ln_g, sgu_ln_b, sgu_w, sgu_b, conv_c_w, conv_c_b, conv_ln_g, conv_ln_b, grp_norm_g, w_out, post_mix_g, pre_ffn_g, w_up, ffn_conv_w, w_down, post_ffn_g, loss_target, m_pre_mix_g, m_w_in, m_conv_a_w, m_sgu_ln_g, m_sgu_ln_b, m_sgu_w, m_sgu_b, m_conv_c_w, m_conv_c_b, m_conv_ln_g, m_conv_ln_b, m_grp_norm_g, m_w_out, m_post_mix_g, m_pre_ffn_g, m_w_up, m_ffn_conv_w, m_w_down, m_post_ffn_g, v_pre_mix_g, v_w_in, v_conv_a_w, v_sgu_ln_g, v_sgu_ln_b, v_sgu_w, v_sgu_b, v_conv_c_w, v_conv_c_b, v_conv_ln_g, v_conv_ln_b, v_grp_norm_g, v_w_out, v_post_mix_g, v_pre_ffn_g, v_w_up, v_ffn_conv_w, v_w_down, v_post_ffn_g):
    w = dict(pre_mix_g=pre_mix_g, w_in=w_in, conv_a_w=conv_a_w, sgu_ln_g=sgu_ln_g, sgu_ln_b=sgu_ln_b, sgu_w=sgu_w, sgu_b=sgu_b,
             conv_c_w=conv_c_w, conv_c_b=conv_c_b, conv_ln_g=conv_ln_g, conv_ln_b=conv_ln_b, grp_norm_g=grp_norm_g,
             w_out=w_out, post_mix_g=post_mix_g, pre_ffn_g=pre_ffn_g, w_up=w_up, ffn_conv_w=ffn_conv_w, w_down=w_down,
             post_ffn_g=post_ffn_g)
    m = dict(pre_mix_g=m_pre_mix_g, w_in=m_w_in, conv_a_w=m_conv_a_w, sgu_ln_g=m_sgu_ln_g, sgu_ln_b=m_sgu_ln_b,
             sgu_w=m_sgu_w, sgu_b=m_sgu_b, conv_c_w=m_conv_c_w, conv_c_b=m_conv_c_b, conv_ln_g=m_conv_ln_g,
             conv_ln_b=m_conv_ln_b, grp_norm_g=m_grp_norm_g, w_out=m_w_out, post_mix_g=m_post_mix_g,
             pre_ffn_g=m_pre_ffn_g, w_up=m_w_up, ffn_conv_w=m_ffn_conv_w, w_down=m_w_down, post_ffn_g=m_post_ffn_g)
    v = dict(pre_mix_g=v_pre_mix_g, w_in=v_w_in, conv_a_w=v_conv_a_w, sgu_ln_g=v_sgu_ln_g, sgu_ln_b=v_sgu_ln_b,
             sgu_w=v_sgu_w, sgu_b=v_sgu_b, conv_c_w=v_conv_c_w, conv_c_b=v_conv_c_b, conv_ln_g=v_conv_ln_g,
             conv_ln_b=v_conv_ln_b, grp_norm_g=v_grp_norm_g, w_out=v_w_out, post_mix_g=v_post_mix_g,
             pre_ffn_g=v_pre_ffn_g, w_up=v_w_up, ffn_conv_w=v_ffn_conv_w, w_down=v_w_down, post_ffn_g=v_post_ffn_g)
    chip = 2 * lax.axis_index("x") + lax.axis_index("y")

    shards, convs = gather_weights([cast_shard(w[n], chip) for n in BIG], _pack([w[n] for n in CONV], CONV_ROWS))
    gw = dict(zip(BIG, shards))
    cparts = [_unpack(convs[j], CONV_SHARDS) for j in range(N_CHIPS)]
    full = dict(w)
    for i, n in enumerate(CONV):
        full[n] = _join_cols(jnp.stack([p[i] for p in cparts]))

    xc = to_tiles(x[0], TILE_EW)
    saved = []
    for layer in range(DEPTH):
        xc, sv = layer_fwd(xc, {n: full[n][layer] for n in REPL + CONV}, gw, layer)
        saved.append(sv)
    dxc, loss_part = loss_head(xc, to_tiles(loss_target[0], TILE_EW), TILE_MM)
    loss = lax.psum(loss_part[0, 0], ("x", "y", "c"))
    gbuf = grad_buffers()
    small = [None] * DEPTH
    for layer in reversed(range(DEPTH)):
        dxc, small[layer], gbuf = layer_bwd(dxc, {n: full[n][layer] for n in REPL + CONV}, gw, layer, saved[layer], gbuf)
    grads = {n: jnp.stack([small[layer][n] for layer in range(DEPTH)]) for n in REPL + CONV}

    gbuf["w_down"] = gbuf["w_down"].reshape(N_CHIPS, DEPTH, D_FF // N_CHIPS, D_MODEL)
    gs = [gbuf[n] for n in BIG]
    sums = [add_halves(g, got) for g, got in zip(gs, swap_halves(gs))]
    arrived = scatter_to_chips([s16 for _, s16 in sums])
    out_g = dict(zip(BIG, join_halves([add_chips(s32, got) for (s32, _), got in zip(sums, arrived)])))

    tot = sum_devices(gather_all(_pack([grads[n] for n in REPL + CONV], SMALL_ROWS)), 192)
    shapes = [grads[n].shape for n in REPL + CONV]
    for n, gfull in zip(REPL + CONV, _unpack(tot, shapes)):
        if n in CONV:
            width = gfull.shape[-1] // N_CHIPS
            gfull = lax.dynamic_slice_in_dim(gfull, chip * width, width, axis=2)
        out_g[n] = gfull

    deltas, new_m, new_v = {}, {}, {}
    for n in WEIGHTS:
        deltas[n], new_m[n], new_v[n] = adamw(w[n], out_g[n], m[n], v[n])
    return (loss, from_tiles(dxc, TILE_EW)[None], *[out_g[n] for n in WEIGHTS], *[deltas[n] for n in WEIGHTS], *[new_m[n] for n in WEIGHTS],
            *[new_v[n] for n in WEIGHTS])
```

```python
import functools
import math

import jax
import jax.numpy as jnp
from jax import lax
from jax.experimental import pallas as pl
from jax.experimental.pallas import tpu as pltpu

F32 = jnp.float32
BF16 = jnp.bfloat16
MM_DTYPE = BF16

D_MODEL = 1024
SEQ = 4096
DEPTH = 4
D_A = 256
D_B = 384
D_C = 384
D_IN = 3 * D_A + 2 * D_B + 2 * D_C
D_FF = 2816
K_A = 3
K_C = 31
K_F = 3
CHUNK = 128
HEAD = 64
N_HEADS_B = D_B // HEAD
EPS = 1e-6
N_CHIPS = 4

ADAM_LR = 0.001
ADAM_B1 = 0.9
ADAM_B2 = 0.999
ADAM_EPS = 1e-08
ADAM_WD = 0.01
ADAM_STEP = 10

LANES = 1024
VMEM_LIMIT = 56 * 1024 * 1024

MESH_ID = pl.DeviceIdType.MESH


def _params(sem=None):
    return pltpu.CompilerParams(dimension_semantics=sem, vmem_limit_bytes=VMEM_LIMIT)


def _const_spec(shape):
    nd = len(shape)
    return pl.BlockSpec(shape, lambda *_: (0,) * nd, pipeline_mode=pl.Buffered(1))


def _rowsum8(a):
    r, c = a.shape
    return jnp.sum(a.reshape(r // 8, 8, c), axis=0)


def _rstd(x):
    return lax.rsqrt(jnp.mean(x * x, axis=-1, keepdims=True) + EPS)


def _rms_bwd(x, r, g, dy):
    gdy = g * dy
    return r * gdy - x * (r * r * r) * jnp.mean(gdy * x, axis=-1, keepdims=True)


def _ln_fwd(x):
    mu = jnp.mean(x, axis=-1, keepdims=True)
    xc = x - mu
    r = lax.rsqrt(jnp.mean(xc * xc, axis=-1, keepdims=True) + EPS)
    return xc * r, r


def _ln_bwd(xh, r, dxh):
    return r * (dxh - jnp.mean(dxh, axis=-1, keepdims=True) - xh * jnp.mean(dxh * xh, axis=-1, keepdims=True))


def _gelu(x):
    return 0.5 * x * (1.0 + lax.erf(x * (1.0 / math.sqrt(2.0))))


def _gelu_grad(x):
    cdf = 0.5 * (1.0 + lax.erf(x * (1.0 / math.sqrt(2.0))))
    pdf = jnp.exp(-0.5 * x * x) * (1.0 / math.sqrt(2.0 * math.pi))
    return cdf + x * pdf


def _dot(a, b):
    return jnp.dot(a, b, preferred_element_type=F32)


def _dot_nt(a, b):
    return lax.dot_general(a, b, (((1,), (1,)), ((), ())), preferred_element_type=F32)


def _dot_tn(a, b):
    return lax.dot_general(a, b, (((0,), (0,)), ((), ())), preferred_element_type=F32)


def _col_chunk(n):
    for c in (1408, 1024, 768, 512, 256, 128):
        if n % c == 0:
            return c
    raise ValueError(n)


def _layer_spec(wg, layer):
    _, _, r, c = wg.shape
    return pl.BlockSpec((N_CHIPS, None, r, c), lambda *_: (0, layer, 0, 0), pipeline_mode=pl.Buffered(1))


def _join_col_blocks(w_ref, w_scr):
    c = w_ref.shape[2]
    for j in range(N_CHIPS):
        w_scr[:, c * j:c * (j + 1)] = w_ref[j]


def norm_matmul(x, g, wg, layer, tm):
    t, d = x.shape
    assert t % tm == 0, (t, tm)
    cw = wg.shape[3]
    n = N_CHIPS * cw
    aligned = cw % 128 == 0
    cn = cw if aligned else _col_chunk(n)

    def body(x_ref, g_ref, w_ref, o_ref, h_ref, *scr):
        if not aligned:
            @pl.when(pl.program_id(0) == 0)
            def _():
                _join_col_blocks(w_ref, scr[0])

        xv = x_ref[...]
        h = (xv * _rstd(xv) * g_ref[...]).astype(MM_DTYPE)
        h_ref[...] = h
        for j, c0 in enumerate(range(0, n, cn)):
            wv = w_ref[j] if aligned else scr[0][:, c0:c0 + cn]
            o_ref[:, c0:c0 + cn] = _dot(h, wv)

    return pl.pallas_call(
        body, name="norm_matmul", grid=(t // tm,),
        in_specs=[pl.BlockSpec((tm, d), lambda i: (i, 0)), _const_spec((1, d)), _layer_spec(wg, layer)],
        out_specs=[pl.BlockSpec((tm, n), lambda i: (i, 0)), pl.BlockSpec((tm, d), lambda i: (i, 0))],
        out_shape=[jax.ShapeDtypeStruct((t, n), F32), jax.ShapeDtypeStruct((t, d), MM_DTYPE)],
        scratch_shapes=[] if aligned else [pltpu.VMEM((d, n), MM_DTYPE)],
        compiler_params=_params(("arbitrary",)),
    )(x, g, wg)


def matmul_nt_norm_bwd(gy, wg, layer, x, g, dres, tm):
    t, n = gy.shape
    assert t % tm == 0, (t, tm)
    d, cw = wg.shape[2], wg.shape[3]
    aligned = cw % 128 == 0
    cn = cw if aligned else _col_chunk(n)
    steps = t // tm

    def body(gy_ref, w_ref, x_ref, g_ref, dres_ref, dx_ref, dg_ref, acc_ref, *scr):
        i = pl.program_id(0)

        @pl.when(i == 0)
        def _():
            acc_ref[...] = jnp.zeros_like(acc_ref)
            if not aligned:
                _join_col_blocks(w_ref, scr[0])

        dh = jnp.zeros((tm, d), F32)
        for j, c0 in enumerate(range(0, n, cn)):
            wv = w_ref[j] if aligned else scr[0][:, c0:c0 + cn]
            dh = dh + _dot_nt(gy_ref[:, c0:c0 + cn], wv)
        xv = x_ref[...]
        r = _rstd(xv)
        gv = g_ref[...]
        dx_ref[...] = dres_ref[...] + _rms_bwd(xv, r, gv, dh)
        acc_ref[...] += _rowsum8(dh * xv * r)

        @pl.when(i == steps - 1)
        def _():
            dg_ref[...] = jnp.sum(acc_ref[...], axis=0, keepdims=True)

    return pl.pallas_call(
        body, name="matmul_nt_norm_bwd", grid=(steps,),
        in_specs=[pl.BlockSpec((tm, n), lambda i: (i, 0)), _layer_spec(wg, layer), pl.BlockSpec((tm, d), lambda i: (i, 0)),
                  _const_spec((1, d)), pl.BlockSpec((tm, d), lambda i: (i, 0))],
        out_specs=[pl.BlockSpec((tm, d), lambda i: (i, 0)), pl.BlockSpec((1, d), lambda i: (0, 0))],
        out_shape=[jax.ShapeDtypeStruct((t, d), F32), jax.ShapeDtypeStruct((1, d), F32)],
        scratch_shapes=[pltpu.VMEM((8, d), F32)] + ([] if aligned else [pltpu.VMEM((d, n), MM_DTYPE)]),
        compiler_params=_params(("arbitrary",)),
    )(gy, wg, x, g, dres)


_ANY = pl.BlockSpec(memory_space=pl.ANY)


def matmul_tn_blocks(a, b, buf, layer, tk, by_rows):
    t = a.shape[0]
    _, _, r, c = buf.shape
    assert t % tk == 0 and r % 8 == 0 and c % 128 == 0, (a.shape, b.shape, buf.shape, tk)

    def body(a_ref, b_ref, buf_ref, o_ref):
        del buf_ref

        @pl.when(pl.program_id(1) == 0)
        def _():
            o_ref[...] = jnp.zeros_like(o_ref)

        o_ref[...] += _dot_tn(a_ref[...], b_ref[...])

    a_spec = pl.BlockSpec((tk, r), (lambda j, k: (k, j)) if by_rows else (lambda j, k: (k, 0)))
    b_spec = pl.BlockSpec((tk, c), (lambda j, k: (k, 0)) if by_rows else (lambda j, k: (k, j)))
    return pl.pallas_call(
        body, name="matmul_tn_blocks", grid=(N_CHIPS, t // tk), in_specs=[a_spec, b_spec, _ANY],
        out_specs=pl.BlockSpec((None, None, r, c), lambda j, k: (j, layer, 0, 0)),
        out_shape=jax.ShapeDtypeStruct(buf.shape, F32), input_output_aliases={2: 0},
        compiler_params=_params(("arbitrary", "arbitrary")),
    )(a, b, buf)


def matmul_tn_down(act, dd, buf, layer, tk):
    t = act.shape[0]
    _, _, _, r, c = buf.shape
    assert t % tk == 0, (t, tk)
    steps = t // tk

    def body(a_ref, b_ref, buf_ref, o_ref, acc):
        del buf_ref
        k = pl.program_id(1)

        @pl.when(k == 0)
        def _():
            acc[...] = jnp.zeros_like(acc)

        acc[...] += _dot_tn(a_ref[...], b_ref[...])

        @pl.when(k == steps - 1)
        def _():
            o_ref[0] = acc[0:r, :]
            o_ref[1] = acc[r:2 * r, :]

    return pl.pallas_call(
        body, name="matmul_tn_down", grid=(2, steps),
        in_specs=[pl.BlockSpec((tk, 2 * r), lambda p, k: (k, p)), pl.BlockSpec((tk, c), lambda p, k: (k, 0)), _ANY],
        out_specs=pl.BlockSpec((None, 2, None, r, c), lambda p, k: (p, 0, layer, 0, 0)),
        out_shape=jax.ShapeDtypeStruct(buf.shape, F32), input_output_aliases={2: 0},
        scratch_shapes=[pltpu.VMEM((2 * r, c), F32)],
        compiler_params=_params(("arbitrary", "arbitrary")),
    )(act, dd, buf)


def matmul_tn_in(h, dz, buf, layer, tk):
    t, d = h.shape
    n = dz.shape[1]
    _, _, r, c = buf.shape
    assert t % tk == 0 and r == d and N_CHIPS * c == n, (h.shape, dz.shape, buf.shape)
    steps = t // tk

    def body(a_ref, b_ref, buf_ref, o_ref, acc):
        del buf_ref
        k = pl.program_id(0)

        @pl.when(k == 0)
        def _():
            acc[...] = jnp.zeros_like(acc)

        acc[...] += _dot_tn(a_ref[...], b_ref[...])

        @pl.when(k == steps - 1)
        def _():
            for j in range(N_CHIPS):
                o_ref[j] = acc[:, c * j:c * (j + 1)]

    return pl.pallas_call(
        body, name="matmul_tn_in", grid=(steps,),
        in_specs=[pl.BlockSpec((tk, d), lambda k: (k, 0)), pl.BlockSpec((tk, n), lambda k: (k, 0)), _ANY],
        out_specs=pl.BlockSpec((N_CHIPS, None, r, c), lambda k: (0, layer, 0, 0)),
        out_shape=jax.ShapeDtypeStruct(buf.shape, F32), input_output_aliases={2: 0},
        scratch_shapes=[pltpu.VMEM((d, n), F32)],
        compiler_params=_params(("arbitrary",)),
    )(h, dz, buf)


def to_tiles(a, tt):
    t = a.shape[0]
    return a.reshape((t // tt, 8, tt // 8) + a.shape[1:]).swapaxes(1, 2).reshape(a.shape)


def from_tiles(a, tt):
    t = a.shape[0]
    return a.reshape((t // tt, tt // 8, 8) + a.shape[1:]).swapaxes(1, 2).reshape(a.shape)


def _roll_sublanes(a, shift):
    n = a.shape[0] // 8
    return pltpu.roll(a.reshape(n, 8, a.shape[1]), shift, 1).reshape(a.shape)


def _halo_before(cur_last, prev_last):
    sub = lax.broadcasted_iota(jnp.int32, cur_last.shape, 0) % 8
    return jnp.where(sub == 0, _roll_sublanes(prev_last, 1), _roll_sublanes(cur_last, 1))


def _halo_after(cur_first, next_first):
    sub = lax.broadcasted_iota(jnp.int32, cur_first.shape, 0) % 8
    return jnp.where(sub == 7, _roll_sublanes(next_first, 7), _roll_sublanes(cur_first, 7))


def _conv_causal(ext, cur, prev_last, w, taps, tt, cols=None):
    hr = 8 * (taps - 1)
    cs = slice(None) if cols is None else cols
    ext[hr:hr + tt, cs] = cur
    ext[0:hr, cs] = _halo_before(cur[tt - hr:, :], prev_last)
    acc = w[0:1, :] * ext[0:tt, cs]
    for k in range(1, taps):
        acc = acc + w[k:k + 1, :] * ext[8 * k:8 * k + tt, cs]
    return acc


def _conv_anticausal(ext, cur, next_first, w, taps, tt, x=None, acc_w=None, cols=None):
    hr = 8 * (taps - 1)
    cs = slice(None) if cols is None else cols
    ext[0:tt, cs] = cur
    ext[tt:tt + hr, cs] = _halo_after(cur[0:hr, :], next_first)
    acc = None
    for k in range(taps):
        off = 8 * (taps - 1 - k)
        ld = ext[off:off + tt, cs]
        term = w[k:k + 1, :] * ld
        acc = term if acc is None else acc + term
        if x is not None:
            acc_w[k, :, cs] += _rowsum8(ld * x)
    return acc


def _dot_exact(a, b, dims):
    return lax.dot_general(a, b, (dims, ((), ())), precision=lax.Precision.HIGHEST, preferred_element_type=F32)


def _to_tile_order(perm, wt_ref, w_scr, transpose):
    pb = perm.astype(MM_DTYPE)
    for h in range(N_HEADS_B):
        half = (_dot_nt(pb, wt_ref[h]) if transpose else _dot(pb, wt_ref[h])).astype(MM_DTYPE)
        w_scr[h] = _dot_nt(half, pb).astype(MM_DTYPE)


def _project_rows(y, w_ref):
    r = w_ref.shape[1]
    acc = _dot(y[:, 0:r], w_ref[0])
    for j in range(1, N_CHIPS):
        acc = acc + _dot(y[:, r * j:r * (j + 1)], w_ref[j])
    return acc


def _head_select(parts):
    head = lax.broadcasted_iota(jnp.int32, parts[0].shape, 1) // HEAD
    acc = parts[0]
    for h in range(1, N_HEADS_B):
        acc = jnp.where(head == h, parts[h], acc)
    return acc


def _mixer_forward(z, prm, q, yc):
    _, lng, lnb, wm, bias_p, _, _, clg, clb = prm
    bg = z[:, 0:D_A]
    ya = bg * q
    o_b = 3 * D_A
    zu = z[:, o_b:o_b + D_B]
    zv = z[:, o_b + D_B:o_b + 2 * D_B]
    u = _gelu(zu)
    vh, rv = _ln_fwd(_gelu(zv))
    vnb = (vh * lng + lnb).astype(MM_DTYPE)
    s = _head_select([_dot(wm[h], vnb) for h in range(N_HEADS_B)]) + bias_p
    yb = u * s
    yh, rc = _ln_fwd(yc)
    l = yh * clg + clb
    sl = jax.nn.sigmoid(l)
    return dict(bg=bg, q=q, ya=ya, zu=zu, zv=zv, u=u, vh=vh, rv=rv, vnb=vnb, s=s, yb=yb, yh=yh, rc=rc, l=l, sl=sl,
                yo=l * sl)


def _conv_inputs(z):
    o_c = 3 * D_A + 2 * D_B
    a = z[:, o_c:o_c + D_C]
    sg = jax.nn.sigmoid(z[:, o_c + D_C:o_c + 2 * D_C])
    return z[:, D_A:2 * D_A] * z[:, 2 * D_A:3 * D_A], a * sg, a, sg


def _group_norm(f, gg):
    ya, yb, yo = f["ya"], f["yb"], f["yo"]
    ra, rb, ro = _rstd(ya), _rstd(yb), _rstd(yo)
    yn = jnp.concatenate([ya * ra * gg[:, 0:D_A], yb * rb * gg[:, D_A:D_A + D_B], yo * ro * gg[:, D_A + D_B:]], axis=1)
    return yn, (ra, rb, ro)


def _mixer_prm(refs, wp_scr, bias_scr):
    caw_ref, lng_ref, lnb_ref, _, _, ccw_ref, ccb_ref, clg_ref, clb_ref = refs
    wm = [wp_scr[h] for h in range(N_HEADS_B)]
    return (caw_ref[...], lng_ref[...], lnb_ref[...], wm, bias_scr[...], ccw_ref[...], ccb_ref[...], clg_ref[...],
            clb_ref[...])


def _mixer_param_specs(tt):
    return [_const_spec((8, D_A)), _const_spec((1, D_B)), _const_spec((1, D_B)), _const_spec((N_HEADS_B, tt, tt)),
            _const_spec((tt, D_B)), _const_spec((32, D_C)), _const_spec((1, D_C)), _const_spec((1, D_C)),
            _const_spec((1, D_C))]


HR_A = 8 * (K_A - 1)
HR_C = 8 * (K_C - 1)
HR_F = 8 * (K_F - 1)


def mixer_fwd(z, x, mp, perm, grp_g, wog, layer, post_g, tt):
    t = z.shape[0]
    assert t % tt == 0 and tt % CHUNK == 0 and tt >= HR_C, (t, tt)

    def body(z_ref, x_ref, *rest):
        prm_refs = rest[:9]
        (perm_ref, gg_ref, wo_ref, pg_ref, o_ref, x1_ref, cv_ref, pa_ext, yg_ext, pa_last, yg_last, wp_scr, bias_scr) = rest[9:]
        i = pl.program_id(0)

        @pl.when(i == 0)
        def _():
            pa_last[...] = jnp.zeros_like(pa_last)
            yg_last[...] = jnp.zeros_like(yg_last)
            _to_tile_order(perm_ref[...], prm_refs[3], wp_scr, False)
            bias_scr[...] = _dot_exact(perm_ref[...], prm_refs[4][...], ((1,), (0,)))

        zv = z_ref[...]
        prm = _mixer_prm(prm_refs, wp_scr, bias_scr)
        pa, yg, _, _ = _conv_inputs(zv)
        q = _conv_causal(pa_ext, pa, pa_last[...], prm[0], K_A, tt)
        yc = _conv_causal(yg_ext, yg, yg_last[...], prm[5], K_C, tt) + prm[6]
        pa_last[...] = pa[tt - HR_A:, :]
        yg_last[...] = yg[tt - HR_C:, :]
        cv_ref[:, 0:D_A] = q
        cv_ref[:, D_A:] = yc
        f = _mixer_forward(zv, prm, q, yc)
        yn, _ = _group_norm(f, gg_ref[...])
        o = _project_rows(yn.astype(MM_DTYPE), wo_ref)
        o_ref[...] = o
        x1_ref[...] = x_ref[...] + o * _rstd(o) * pg_ref[...]

    row = lambda c: pl.BlockSpec((tt, c), lambda i: (i, 0))
    return pl.pallas_call(
        body, name="mixer_fwd", grid=(t // tt,),
        in_specs=[row(D_IN), row(D_MODEL)] + _mixer_param_specs(tt)
        + [_const_spec((tt, tt)), _const_spec((1, D_MODEL)), _layer_spec(wog, layer), _const_spec((1, D_MODEL))],
        out_specs=[row(D_MODEL), row(D_MODEL), row(D_A + D_C)],
        out_shape=[jax.ShapeDtypeStruct((t, D_MODEL), F32), jax.ShapeDtypeStruct((t, D_MODEL), F32),
                   jax.ShapeDtypeStruct((t, D_A + D_C), F32)],
        scratch_shapes=[pltpu.VMEM((HR_A + tt, D_A), F32), pltpu.VMEM((HR_C + tt, D_C), F32),
                        pltpu.VMEM((HR_A, D_A), F32), pltpu.VMEM((HR_C, D_C), F32),
                        pltpu.VMEM((N_HEADS_B, tt, tt), MM_DTYPE), pltpu.VMEM((tt, D_B), F32)],
        compiler_params=_params(("arbitrary",)),
    )(z, x, *mp, perm, grp_g, wog, post_g)


def mixer_bwd(dx1, o, z, cv, mp, perm, grp_g, wog, layer, post_g, tt):
    t = z.shape[0]
    assert t % tt == 0 and tt % CHUNK == 0 and tt >= HR_C, (t, tt)
    steps = t // tt

    def body(dx1_ref, o_ref, z_ref, cv_ref, *rest):
        prm_refs = rest[:9]
        (perm_ref, gg_ref, wo_ref, pg_ref,
         dz_ref, do_ref, yn_ref, dpg_ref, dgg_ref, dcaw_ref, dlng_ref, dlnb_ref, dwm_ref, dbias_ref, dccw_ref, dccb_ref,
         dclg_ref, dclb_ref,
         dq_ext, dyc_ext, dq_first, dyc_first, a_pg, a_gg, a_caw, a_lng, a_lnb, a_ccw, a_ccb, a_clg, a_clb,
         wp_scr, wpt_scr, bias_scr, a_wm, a_bias) = rest[9:]
        i = pl.program_id(0)
        small = (a_pg, a_gg, a_caw, a_lng, a_lnb, a_ccw, a_ccb, a_clg, a_clb)

        @pl.when(i == 0)
        def _():
            for ref in small + (a_wm, a_bias, dq_first, dyc_first):
                ref[...] = jnp.zeros_like(ref)
            _to_tile_order(perm_ref[...], prm_refs[3], wp_scr, False)
            _to_tile_order(perm_ref[...], prm_refs[3], wpt_scr, True)
            bias_scr[...] = _dot_exact(perm_ref[...], prm_refs[4][...], ((1,), (0,)))

        prm = _mixer_prm(prm_refs, wp_scr, bias_scr)
        caw, lng, lnb, wm, bias_p, ccw, ccb, clg, clb = prm

        zv = z_ref[...]
        pa, yg, a, sg = _conv_inputs(zv)
        f = _mixer_forward(zv, prm, cv_ref[:, 0:D_A], cv_ref[:, D_A:])
        gg = gg_ref[...]
        yn, (ra, rb, ro) = _group_norm(f, gg)
        yn_ref[...] = yn.astype(MM_DTYPE)

        ov = o_ref[...]
        dx1v = dx1_ref[...]
        r_o = _rstd(ov)
        pg = pg_ref[...]
        a_pg[...] += _rowsum8(dx1v * ov * r_o)
        do = _rms_bwd(ov, r_o, pg, dx1v).astype(MM_DTYPE)
        do_ref[...] = do
        dyn = jnp.concatenate([_dot_nt(do, wo_ref[j]) for j in range(N_CHIPS)], axis=1)

        dyn_a, dyn_b, dyn_c = dyn[:, 0:D_A], dyn[:, D_A:D_A + D_B], dyn[:, D_A + D_B:]
        ga, gb, gc = gg[:, 0:D_A], gg[:, D_A:D_A + D_B], gg[:, D_A + D_B:]
        a_gg[...] += _rowsum8(jnp.concatenate([dyn_a * f["ya"] * ra, dyn_b * f["yb"] * rb, dyn_c * f["yo"] * ro], axis=1))
        dya = _rms_bwd(f["ya"], ra, ga, dyn_a)
        dyb = _rms_bwd(f["yb"], rb, gb, dyn_b)
        dyo = _rms_bwd(f["yo"], ro, gc, dyn_c)

        dbg = dya * f["q"]
        dq = dya * f["bg"]
        dp = _conv_anticausal(dq_ext, dq, dq_first[...], caw, K_A, tt, x=pa, acc_w=a_caw)
        dq_first[...] = dq[0:HR_A, :]
        dcg = dp * zv[:, 2 * D_A:3 * D_A]
        dxa = dp * zv[:, D_A:2 * D_A]

        du = dyb * f["s"]
        ds = dyb * f["u"]
        dsb = ds.astype(MM_DTYPE)
        head = lax.broadcasted_iota(jnp.int32, (tt, D_B), 1) // HEAD
        a_bias[...] += ds
        parts = []
        for h in range(N_HEADS_B):
            a_wm[h] += _dot_nt(jnp.where(head == h, dsb, jnp.zeros_like(dsb)), f["vnb"])
            parts.append(_dot(wpt_scr[h], dsb))
        dvn = _head_select(parts)
        a_lng[...] += _rowsum8(dvn * f["vh"])
        a_lnb[...] += _rowsum8(dvn)
        dv = _ln_bwd(f["vh"], f["rv"], dvn * lng)
        dzu = du * _gelu_grad(f["zu"])
        dzv = dv * _gelu_grad(f["zv"])

        l, sl = f["l"], f["sl"]
        dl = dyo * (sl * (1.0 + l * (1.0 - sl)))
        a_clg[...] += _rowsum8(dl * f["yh"])
        a_clb[...] += _rowsum8(dl)
        dyc = _ln_bwd(f["yh"], f["rc"], dl * clg)
        a_ccb[...] += _rowsum8(dyc)
        dy = _conv_anticausal(dyc_ext, dyc, dyc_first[...], ccw, K_C, tt, x=yg, acc_w=a_ccw)
        dyc_first[...] = dyc[0:HR_C, :]
        da = dy * sg
        dg = dy * a * sg * (1.0 - sg)

        dz_ref[...] = jnp.concatenate([dbg, dcg, dxa, dzu, dzv, da, dg], axis=1).astype(MM_DTYPE)

        @pl.when(i == steps - 1)
        def _():
            red = lambda ref: jnp.sum(ref[...], axis=0, keepdims=True)
            dpg_ref[...] = red(a_pg)
            dgg_ref[...] = red(a_gg)
            dlng_ref[...] = red(a_lng)
            dlnb_ref[...] = red(a_lnb)
            dccb_ref[...] = red(a_ccb)
            dclg_ref[...] = red(a_clg)
            dclb_ref[...] = red(a_clb)
            dcaw_ref[...] = jnp.sum(a_caw[...], axis=1)
            dccw_ref[...] = jnp.sum(a_ccw[...], axis=1)
            pm = perm_ref[...]
            tril = lax.broadcasted_iota(jnp.int32, (CHUNK, CHUNK), 0) >= lax.broadcasted_iota(jnp.int32, (CHUNK, CHUNK), 1)
            for h in range(N_HEADS_B):
                dwt = _dot_exact(pm, _dot_exact(a_wm[h], pm, ((1,), (0,))), ((0,), (0,)))
                dw = dwt[0:CHUNK, 0:CHUNK]
                for c in range(1, tt // CHUNK):
                    dw = dw + dwt[c * CHUNK:(c + 1) * CHUNK, c * CHUNK:(c + 1) * CHUNK]
                dwm_ref[h] = jnp.where(tril, dw, 0.0)
            dbt = _dot_exact(pm, a_bias[...], ((0,), (0,)))
            db = dbt[0:CHUNK, :]
            for c in range(1, tt // CHUNK):
                db = db + dbt[c * CHUNK:(c + 1) * CHUNK, :]
            dbias_ref[...] = db

    rev = lambda c: pl.BlockSpec((tt, c), lambda i: (steps - 1 - i, 0))
    full = lambda shape: pl.BlockSpec(shape, lambda i: (0,) * len(shape))
    sds = jax.ShapeDtypeStruct
    outs = pl.pallas_call(
        body, name="mixer_bwd", grid=(steps,),
        in_specs=[rev(D_MODEL), rev(D_MODEL), rev(D_IN), rev(D_A + D_C)] + _mixer_param_specs(tt)
        + [_const_spec((tt, tt)), _const_spec((1, D_MODEL)), _layer_spec(wog, layer), _const_spec((1, D_MODEL))],
        out_specs=[rev(D_IN), rev(D_MODEL), rev(D_MODEL), full((1, D_MODEL)), full((1, D_MODEL)), full((8, D_A)),
                   full((1, D_B)), full((1, D_B)), full((N_HEADS_B, CHUNK, CHUNK)), full((CHUNK, D_B)), full((32, D_C)),
                   full((1, D_C)), full((1, D_C)), full((1, D_C))],
        out_shape=[sds((t, D_IN), MM_DTYPE), sds((t, D_MODEL), MM_DTYPE), sds((t, D_MODEL), MM_DTYPE),
                   sds((1, D_MODEL), F32), sds((1, D_MODEL), F32), sds((8, D_A), F32), sds((1, D_B), F32), sds((1, D_B), F32),
                   sds((N_HEADS_B, CHUNK, CHUNK), F32), sds((CHUNK, D_B), F32), sds((32, D_C), F32), sds((1, D_C), F32),
                   sds((1, D_C), F32), sds((1, D_C), F32)],
        scratch_shapes=[pltpu.VMEM((tt + HR_A, D_A), F32), pltpu.VMEM((tt + HR_C, D_C), F32),
                        pltpu.VMEM((HR_A, D_A), F32), pltpu.VMEM((HR_C, D_C), F32),
                        pltpu.VMEM((8, D_MODEL), F32), pltpu.VMEM((8, D_MODEL), F32), pltpu.VMEM((8, 8, D_A), F32),
                        pltpu.VMEM((8, D_B), F32), pltpu.VMEM((8, D_B), F32), pltpu.VMEM((32, 8, D_C), F32),
                        pltpu.VMEM((8, D_C), F32), pltpu.VMEM((8, D_C), F32), pltpu.VMEM((8, D_C), F32),
                        pltpu.VMEM((N_HEADS_B, tt, tt), MM_DTYPE), pltpu.VMEM((N_HEADS_B, tt, tt), MM_DTYPE),
                        pltpu.VMEM((tt, D_B), F32), pltpu.VMEM((N_HEADS_B, tt, tt), F32), pltpu.VMEM((tt, D_B), F32)],
        compiler_params=_params(("arbitrary",)),
    )(dx1, o, z, cv, *mp, perm, grp_g, wog, post_g)
    return outs


def _fetch_row_blocks(wg_ref, layer, w_scr, sems):
    r = wg_ref.shape[2]
    copies = [pltpu.make_async_copy(wg_ref.at[j, layer], w_scr.at[pl.ds(r * j, r), :], sems.at[j]) for j in range(N_CHIPS)]
    for cp in copies:
        cp.start()
    for cp in copies:
        cp.wait()


def _ffn_conv(ext, cw, c0, cn, tt):
    acc = cw[0:1, c0:c0 + cn] * ext[0:tt, c0:c0 + cn]
    for k in range(1, K_F):
        acc = acc + cw[k:k + 1, c0:c0 + cn] * ext[8 * k:8 * k + tt, c0:c0 + cn]
    return acc


def ffn_fwd(up0, x1, cw, wdg, layer, post_g, tt):
    t = up0.shape[0]
    assert t % tt == 0, (t, tt)
    cn = _col_chunk(D_FF)

    def body(up0_ref, x1_ref, cw_ref, wdg_ref, pg_ref, d_ref, x2_ref, ext, last, wd_ref, sems):
        i = pl.program_id(0)

        @pl.when(i == 0)
        def _():
            _fetch_row_blocks(wdg_ref, layer, wd_ref, sems)
            last[...] = jnp.zeros_like(last)

        ext[HR_F:HR_F + tt, :] = up0_ref[...]
        ext[0:HR_F, :] = _halo_before(up0_ref[tt - HR_F:, :], last[...])
        last[...] = up0_ref[tt - HR_F:, :]
        cwv = cw_ref[...]
        d = jnp.zeros((tt, D_MODEL), F32)
        for c0 in range(0, D_FF, cn):
            gate = _ffn_conv(ext, cwv, c0, cn, tt)
            val = _ffn_conv(ext, cwv, D_FF + c0, cn, tt)
            act = (gate * jax.nn.sigmoid(gate) * val).astype(MM_DTYPE)
            d = d + _dot(act, wd_ref[c0:c0 + cn, :])
        d_ref[...] = d
        x2_ref[...] = x1_ref[...] + d * _rstd(d) * pg_ref[...]

    row = lambda c: pl.BlockSpec((tt, c), lambda i: (i, 0))
    return pl.pallas_call(
        body, name="ffn_fwd", grid=(t // tt,),
        in_specs=[row(2 * D_FF), row(D_MODEL), _const_spec((8, 2 * D_FF)), _ANY, _const_spec((1, D_MODEL))],
        out_specs=[row(D_MODEL), row(D_MODEL)],
        out_shape=[jax.ShapeDtypeStruct((t, D_MODEL), F32), jax.ShapeDtypeStruct((t, D_MODEL), F32)],
        scratch_shapes=[pltpu.VMEM((HR_F + tt, 2 * D_FF), F32), pltpu.VMEM((HR_F, 2 * D_FF), F32),
                        pltpu.VMEM((D_FF, D_MODEL), MM_DTYPE), pltpu.SemaphoreType.DMA((N_CHIPS,))],
        compiler_params=_params(("arbitrary",)),
    )(up0, x1, cw, wdg, post_g)


def ffn_bwd(dx2, d, up0, cw, wdg, layer, post_g, tt):
    t = up0.shape[0]
    assert t % tt == 0, (t, tt)
    steps = t // tt
    hb = tt // HR_F
    cn = _col_chunk(D_FF)

    def body(dx2_ref, d_ref, up0_ref, uh_ref, cw_ref, wdg_ref, pg_ref,
             dd_ref, act_ref, dup0_ref, dpg_ref, dcw_ref, ext, dup_ext, first, a_pg, a_cw, wd_ref, sems):
        i = pl.program_id(0)
        tile = steps - 1 - i

        @pl.when(i == 0)
        def _():
            _fetch_row_blocks(wdg_ref, layer, wd_ref, sems)
            a_pg[...] = jnp.zeros_like(a_pg)
            a_cw[...] = jnp.zeros_like(a_cw)
            first[...] = jnp.zeros_like(first)

        ext[HR_F:HR_F + tt, :] = up0_ref[...]
        ext[0:HR_F, :] = _halo_before(up0_ref[tt - HR_F:, :], jnp.where(tile > 0, uh_ref[...], 0.0))
        cwv = cw_ref[...]
        dv = d_ref[...]
        dx2v = dx2_ref[...]
        r = _rstd(dv)
        a_pg[...] += _rowsum8(dx2v * dv * r)
        dd = _rms_bwd(dv, r, pg_ref[...], dx2v).astype(MM_DTYPE)
        dd_ref[...] = dd
        for c0 in range(0, D_FF, cn):
            gate = _ffn_conv(ext, cwv, c0, cn, tt)
            val = _ffn_conv(ext, cwv, D_FF + c0, cn, tt)
            sg = jax.nn.sigmoid(gate)
            sl = gate * sg
            act_ref[:, c0:c0 + cn] = (sl * val).astype(MM_DTYPE)
            da = _dot_nt(dd, wd_ref[c0:c0 + cn, :])
            dup_ext[0:tt, c0:c0 + cn] = da * val * (sg * (1.0 + gate * (1.0 - sg)))
            dup_ext[0:tt, D_FF + c0:D_FF + c0 + cn] = da * sl
        dup_ext[tt:tt + HR_F, :] = _halo_after(dup_ext[0:HR_F, :], first[...])
        first[...] = dup_ext[0:HR_F, :]
        for c0 in range(0, 2 * D_FF, cn):
            x = up0_ref[:, c0:c0 + cn]
            acc = None
            for k in range(K_F):
                off = 8 * (K_F - 1 - k)
                ld = dup_ext[off:off + tt, c0:c0 + cn]
                term = cwv[k:k + 1, c0:c0 + cn] * ld
                acc = term if acc is None else acc + term
                a_cw[k, :, c0:c0 + cn] += _rowsum8(ld * x)
            dup0_ref[:, c0:c0 + cn] = acc.astype(MM_DTYPE)

        @pl.when(i == steps - 1)
        def _():
            dpg_ref[...] = jnp.sum(a_pg[...], axis=0, keepdims=True)
            dcw_ref[...] = jnp.sum(a_cw[...], axis=1)

    rev = lambda c: pl.BlockSpec((tt, c), lambda i: (steps - 1 - i, 0))
    halo = pl.BlockSpec((HR_F, 2 * D_FF), lambda i: (jnp.maximum((steps - 1 - i) * hb - 1, 0), 0))
    full = lambda shape: pl.BlockSpec(shape, lambda i: (0,) * len(shape))
    sds = jax.ShapeDtypeStruct
    return pl.pallas_call(
        body, name="ffn_bwd", grid=(steps,),
        in_specs=[rev(D_MODEL), rev(D_MODEL), rev(2 * D_FF), halo, _const_spec((8, 2 * D_FF)), _ANY,
                  _const_spec((1, D_MODEL))],
        out_specs=[rev(D_MODEL), rev(D_FF), rev(2 * D_FF), full((1, D_MODEL)), full((8, 2 * D_FF))],
        out_shape=[sds((t, D_MODEL), MM_DTYPE), sds((t, D_FF), MM_DTYPE), sds((t, 2 * D_FF), MM_DTYPE),
                   sds((1, D_MODEL), F32), sds((8, 2 * D_FF), F32)],
        scratch_shapes=[pltpu.VMEM((HR_F + tt, 2 * D_FF), F32), pltpu.VMEM((tt + HR_F, 2 * D_FF), F32),
                        pltpu.VMEM((HR_F, 2 * D_FF), F32), pltpu.VMEM((8, D_MODEL), F32),
                        pltpu.VMEM((8, 8, 2 * D_FF), F32), pltpu.VMEM((D_FF, D_MODEL), MM_DTYPE),
                        pltpu.SemaphoreType.DMA((N_CHIPS,))],
        compiler_params=_params(("arbitrary",)),
    )(dx2, d, up0, up0, cw, wdg, post_g)


def loss_head(y, target, tm):
    t, d = y.shape
    assert t % tm == 0, (t, tm)
    steps = t // tm

    def body(y_ref, t_ref, dy_ref, loss_ref, acc):
        i = pl.program_id(0)

        @pl.when(i == 0)
        def _():
            acc[...] = jnp.zeros_like(acc)

        diff = y_ref[...] - t_ref[...]
        dy_ref[...] = diff * (1.0 / d)
        acc[...] += _rowsum8(diff * diff)

        @pl.when(i == steps - 1)
        def _():
            loss_ref[...] = (0.5 / d) * jnp.sum(jnp.sum(acc[...], axis=0, keepdims=True), axis=1, keepdims=True)

    row = pl.BlockSpec((tm, d), lambda i: (i, 0))
    return pl.pallas_call(
        body, name="loss_head", grid=(steps,), in_specs=[row, row],
        out_specs=[row, pl.BlockSpec((1, 1), lambda i: (0, 0))],
        out_shape=[jax.ShapeDtypeStruct((t, d), F32), jax.ShapeDtypeStruct((1, 1), F32)],
        scratch_shapes=[pltpu.VMEM((8, d), F32)],
        compiler_params=_params(("arbitrary",)),
    )(y, target)


def adamw(w, g, m, v):
    shape = w.shape
    cols = shape[-1]
    rows = w.size // cols
    tr = next((r for r in (512, 256, 128) if rows % r == 0 and rows > r), rows)
    c1 = 1.0 - ADAM_B1 ** ADAM_STEP
    c2 = 1.0 - ADAM_B2 ** ADAM_STEP

    def body(w_ref, g_ref, m_ref, v_ref, d_ref, nm_ref, nv_ref):
        gv = g_ref[...]
        nm = ADAM_B1 * m_ref[...] + (1.0 - ADAM_B1) * gv
        nv = ADAM_B2 * v_ref[...] + (1.0 - ADAM_B2) * (gv * gv)
        nm_ref[...] = nm
        nv_ref[...] = nv
        d_ref[...] = -ADAM_LR * ((nm / c1) / (jnp.sqrt(nv / c2) + ADAM_EPS) + ADAM_WD * w_ref[...])

    spec = pl.BlockSpec((tr, cols), lambda i: (i, 0))
    out = jax.ShapeDtypeStruct((rows, cols), F32)
    res = pl.pallas_call(
        body, name="adamw", grid=(rows // tr,), in_specs=[spec] * 4, out_specs=[spec] * 3, out_shape=[out] * 3,
        compiler_params=_params(("arbitrary",)),
    )(*[a.reshape(rows, cols) for a in (w, g, m, v)])
    return tuple(r.reshape(shape) for r in res)


HALF = DEPTH // 2


def _place():
    return lax.axis_index("x"), lax.axis_index("y"), lax.axis_index("c")


def _other_chips(x, y):
    return [(1 - x, y, 2 * (1 - x) + y), (x, 1 - y, 2 * x + 1 - y), (1 - x, 1 - y, 2 * (1 - x) + 1 - y)]


def _sem_specs(*counts):
    return [pltpu.SemaphoreType.DMA((n,)) for n in counts]


def cast_shard(w, chip):
    l, r, c = w.shape

    def body(chip_ref, w_ref, o_ref):
        del chip_ref
        o_ref[...] = w_ref[...].astype(MM_DTYPE)

    grid_spec = pltpu.PrefetchScalarGridSpec(
        num_scalar_prefetch=1, grid=(l,), in_specs=[pl.BlockSpec((None, r, c), lambda i, chip_ref: (i, 0, 0))],
        out_specs=pl.BlockSpec((None, None, r, c), lambda i, chip_ref: (chip_ref[0], i, 0, 0)))
    return pl.pallas_call(
        body, name="cast_shard", grid_spec=grid_spec, out_shape=jax.ShapeDtypeStruct((N_CHIPS, l, r, c), MM_DTYPE),
        compiler_params=_params(("arbitrary",)),
    )(jnp.reshape(chip, (1,)).astype(jnp.int32), w)


def gather_weights(shards, small):
    n = len(shards)

    def body(*refs):
        small_ref = refs[n]
        outs = refs[n + 1:2 * n + 1]
        smalls_ref = refs[2 * n + 1]
        send, recv, fsend, frecv, ssend, srecv, local = refs[2 * n + 2:]
        x, y, c = _place()
        me = 2 * x + y
        chips = _other_chips(x, y)
        mine = pl.ds(c * HALF, HALF)
        theirs = pl.ds((1 - c) * HALF, HALF)

        own_small = pltpu.make_async_copy(small_ref, smalls_ref.at[me], local.at[0])
        own_small.start()

        def block(ref, chip, layers, sems, k, to):
            part = ref.at[chip, layers]
            return pltpu.make_async_remote_copy(src_ref=part, dst_ref=part, send_sem=sems[0].at[k], recv_sem=sems[1].at[k],
                                                device_id=to, device_id_type=MESH_ID)

        def small_copy(k, chip, to):
            return pltpu.make_async_remote_copy(src_ref=small_ref, dst_ref=smalls_ref.at[chip], send_sem=ssend.at[k],
                                                recv_sem=srecv.at[k], device_id=to, device_id_type=MESH_ID)

        sends = []
        for k, (px, py, _) in enumerate(chips):
            for i, ref in enumerate(outs):
                sends.append(block(ref, me, mine, (send, recv), n * k + i, (px, py, c)))
            sends.append(small_copy(k, me, (px, py, c)))
        for cp in sends:
            cp.start()
        forwards = []
        for k, (_, _, pj) in enumerate(chips):
            for i, ref in enumerate(outs):
                block(ref, pj, mine, (send, recv), n * k + i, (x, y, c)).wait_recv()
                fw = block(ref, pj, mine, (fsend, frecv), n * k + i, (x, y, 1 - c))
                fw.start()
                forwards.append(fw)
        for k, (_, _, pj) in enumerate(chips):
            small_copy(k, pj, (x, y, c)).wait_recv()
            for i, ref in enumerate(outs):
                block(ref, pj, theirs, (fsend, frecv), n * k + i, (x, y, 1 - c)).wait_recv()
        for cp in sends + forwards:
            cp.wait_send()
        own_small.wait()

    res = pl.pallas_call(
        body, name="gather_weights", in_specs=[_ANY] * (n + 1), out_specs=[_ANY] * (n + 1),
        out_shape=[jax.ShapeDtypeStruct(s.shape, s.dtype) for s in shards]
        + [jax.ShapeDtypeStruct((N_CHIPS,) + small.shape, small.dtype)],
        input_output_aliases={i: i for i in range(n)},
        scratch_shapes=_sem_specs(3 * n, 3 * n, 3 * n, 3 * n, 3, 3, 1),
        compiler_params=pltpu.CompilerParams(has_side_effects=True),
    )(*shards, small)
    return res[:n], res[n]


def swap_halves(gs):
    n = len(gs)

    def body(*refs):
        g_refs, got_refs, (send, recv) = refs[:n], refs[n:2 * n], refs[2 * n:]
        x, y, c = _place()
        theirs = pl.ds((1 - c) * HALF, HALF)
        copies = [pltpu.make_async_remote_copy(
            src_ref=g.at[:, theirs], dst_ref=got, send_sem=send.at[i], recv_sem=recv.at[i],
            device_id=(x, y, 1 - c), device_id_type=MESH_ID) for i, (g, got) in enumerate(zip(g_refs, got_refs))]
        for cp in copies:
            cp.start()
        for cp in copies:
            cp.wait()

    return pl.pallas_call(
        body, name="swap_halves", in_specs=[_ANY] * n, out_specs=[_ANY] * n,
        out_shape=[jax.ShapeDtypeStruct((g.shape[0], HALF) + g.shape[2:], g.dtype) for g in gs],
        scratch_shapes=_sem_specs(n, n), compiler_params=pltpu.CompilerParams(has_side_effects=True),
    )(*gs)


def _row_tile(r):
    return next(t for t in (256, 352, 128) if r % t == 0)


def add_halves(g, got):
    n, _, r, cols = g.shape
    tr = _row_tile(r)
    c = lax.axis_index("c")

    def body(c_ref, g_ref, got_ref, s_ref, sb_ref):
        del c_ref
        s = g_ref[...] + got_ref[...]
        s_ref[...] = s
        sb_ref[...] = s.astype(BF16)

    blk = (None, None, tr, cols)
    grid_spec = pltpu.PrefetchScalarGridSpec(
        num_scalar_prefetch=1, grid=(n, HALF, r // tr),
        in_specs=[pl.BlockSpec(blk, lambda j, l, i, c_ref: (j, c_ref[0] * HALF + l, i, 0)),
                  pl.BlockSpec(blk, lambda j, l, i, c_ref: (j, l, i, 0))],
        out_specs=[pl.BlockSpec(blk, lambda j, l, i, c_ref: (j, l, i, 0))] * 2)
    return pl.pallas_call(
        body, name="add_halves", grid_spec=grid_spec,
        out_shape=[jax.ShapeDtypeStruct(got.shape, F32), jax.ShapeDtypeStruct(got.shape, BF16)],
        compiler_params=_params(("arbitrary",) * 3),
    )(jnp.reshape(c, (1,)).astype(jnp.int32), g, got)


def scatter_to_chips(sbs):
    n = len(sbs)

    def body(*refs):
        sb_refs, got_refs, (send, recv) = refs[:n], refs[n:2 * n], refs[2 * n:]
        x, y, c = _place()
        me = 2 * x + y
        chips = _other_chips(x, y)
        sends = []
        for k, (px, py, pj) in enumerate(chips):
            for i, (sb, got) in enumerate(zip(sb_refs, got_refs)):
                sends.append(pltpu.make_async_remote_copy(
                    src_ref=sb.at[pj], dst_ref=got.at[me], send_sem=send.at[n * k + i], recv_sem=recv.at[n * k + i],
                    device_id=(px, py, c), device_id_type=MESH_ID))
        for cp in sends:
            cp.start()
        for k, (_, _, pj) in enumerate(chips):
            for i, (sb, got) in enumerate(zip(sb_refs, got_refs)):
                pltpu.make_async_remote_copy(
                    src_ref=sb.at[pj], dst_ref=got.at[pj], send_sem=send.at[n * k + i], recv_sem=recv.at[n * k + i],
                    device_id=(x, y, c), device_id_type=MESH_ID).wait_recv()
        for cp in sends:
            cp.wait_send()

    return pl.pallas_call(
        body, name="scatter_to_chips", in_specs=[_ANY] * n, out_specs=[_ANY] * n,
        out_shape=[jax.ShapeDtypeStruct(sb.shape, sb.dtype) for sb in sbs],
        scratch_shapes=_sem_specs(3 * n, 3 * n), compiler_params=pltpu.CompilerParams(has_side_effects=True),
    )(*sbs)


def add_chips(s, got):
    n, _, r, cols = s.shape
    tr = _row_tile(r)
    x, y, c = _place()
    me = 2 * x + y

    def body(p_ref, s_ref, g1_ref, g2_ref, g3_ref, o_ref):
        del p_ref
        o_ref[...] = s_ref[...] + g1_ref[...].astype(F32) + g2_ref[...].astype(F32) + g3_ref[...].astype(F32)

    blk = (None, None, tr, cols)

    def other(k):
        return pl.BlockSpec(blk, lambda l, i, p_ref: ((p_ref[0] + k) % n, l, i, 0))

    grid_spec = pltpu.PrefetchScalarGridSpec(
        num_scalar_prefetch=1, grid=(HALF, r // tr),
        in_specs=[pl.BlockSpec(blk, lambda l, i, p_ref: (p_ref[0], l, i, 0)), other(1), other(2), other(3)],
        out_specs=pl.BlockSpec((None, tr, cols), lambda l, i, p_ref: (p_ref[1] * HALF + l, i, 0)))
    return pl.pallas_call(
        body, name="add_chips", grid_spec=grid_spec, out_shape=jax.ShapeDtypeStruct((DEPTH, r, cols), F32),
        compiler_params=_params(("arbitrary",) * 2),
    )(jnp.stack([me, c]).astype(jnp.int32), s, got, got, got)


def join_halves(fs):
    n = len(fs)

    def body(*refs):
        f_refs, (send, recv) = refs[n:2 * n], refs[2 * n:]
        x, y, c = _place()
        mine = pl.ds(c * HALF, HALF)
        theirs = pl.ds((1 - c) * HALF, HALF)
        sends = [pltpu.make_async_remote_copy(
            src_ref=f.at[mine], dst_ref=f.at[mine], send_sem=send.at[i], recv_sem=recv.at[i],
            device_id=(x, y, 1 - c), device_id_type=MESH_ID) for i, f in enumerate(f_refs)]
        for cp in sends:
            cp.start()
        for i, f in enumerate(f_refs):
            pltpu.make_async_remote_copy(
                src_ref=f.at[theirs], dst_ref=f.at[theirs], send_sem=send.at[i], recv_sem=recv.at[i],
                device_id=(x, y, 1 - c), device_id_type=MESH_ID).wait_recv()
        for cp in sends:
            cp.wait_send()

    return pl.pallas_call(
        body, name="join_halves", in_specs=[_ANY] * n, out_specs=[_ANY] * n,
        out_shape=[jax.ShapeDtypeStruct(f.shape, f.dtype) for f in fs], input_output_aliases={i: i for i in range(n)},
        scratch_shapes=_sem_specs(n, n), compiler_params=pltpu.CompilerParams(has_side_effects=True),
    )(*fs)


def gather_all(v):
    def body(v_ref, o_ref, send, recv, local):
        x, y, c = _place()
        me = 4 * x + 2 * y + c
        own = pltpu.make_async_copy(v_ref, o_ref.at[me], local.at[0])
        own.start()
        sends = []
        k = 0
        for fx in (0, 1):
            for fy in (0, 1):
                for fc in (0, 1):
                    if fx or fy or fc:
                        cp = pltpu.make_async_remote_copy(
                            src_ref=v_ref, dst_ref=o_ref.at[me], send_sem=send.at[k], recv_sem=recv.at[k],
                            device_id=(x ^ fx, y ^ fy, c ^ fc), device_id_type=MESH_ID)
                        cp.start()
                        sends.append((cp, k, 4 * (x ^ fx) + 2 * (y ^ fy) + (c ^ fc)))
                        k += 1
        for cp, k, peer in sends:
            pltpu.make_async_remote_copy(
                src_ref=v_ref, dst_ref=o_ref.at[peer], send_sem=send.at[k], recv_sem=recv.at[k],
                device_id=(x, y, c), device_id_type=MESH_ID).wait_recv()
        for cp, _, _ in sends:
            cp.wait_send()
        own.wait()

    return pl.pallas_call(
        body, name="gather_all", in_specs=[_ANY], out_specs=_ANY,
        out_shape=jax.ShapeDtypeStruct((8,) + v.shape, v.dtype), scratch_shapes=_sem_specs(7, 7, 1),
        compiler_params=pltpu.CompilerParams(has_side_effects=True),
    )(v)


def sum_devices(v8, tr):
    n, rows, lanes = v8.shape
    assert rows % tr == 0, (rows, tr)

    def body(v_ref, o_ref):
        acc = v_ref[0]
        for j in range(1, n):
            acc = acc + v_ref[j]
        o_ref[...] = acc

    return pl.pallas_call(
        body, name="sum_devices", grid=(rows // tr,),
        in_specs=[pl.BlockSpec((n, tr, lanes), lambda i: (0, i, 0))], out_specs=pl.BlockSpec((tr, lanes), lambda i: (i, 0)),
        out_shape=jax.ShapeDtypeStruct((rows, lanes), F32), compiler_params=_params(("arbitrary",)),
    )(v8)


def _pack(arrays, rows):
    flat = jnp.concatenate([a.reshape(-1) for a in arrays])
    return jnp.pad(flat, (0, rows * LANES - flat.size)).reshape(rows, LANES)


def _unpack(buf, shapes):
    flat = buf.reshape(-1)
    out, at = [], 0
    for s in shapes:
        n = math.prod(s)
        out.append(flat[at:at + n].reshape(s))
        at += n
    return out


CONV_SHARDS = [(DEPTH, K_A, D_A // N_CHIPS), (DEPTH, K_C, D_C // N_CHIPS), (DEPTH, K_F, 2 * D_FF // N_CHIPS)]
CONV_ROWS = 32
SMALL_ROWS = 576


def _join_cols(g):
    n, l, r, c = g.shape
    return jnp.transpose(g, (1, 2, 0, 3)).reshape(l, r, n * c)


TILE_MM = 512
TILE_EW = 256


def _pad_rows(a, rows):
    return jnp.pad(a, ((0, rows - a.shape[0]), (0, 0)))


def _row(a):
    return a.reshape(1, -1)


def _tile_perm(tt):
    p = lax.broadcasted_iota(jnp.int32, (tt, tt), 0)
    tok = lax.broadcasted_iota(jnp.int32, (tt, tt), 1)
    return ((tt // 8) * (p % 8) + p // 8 == tok).astype(F32)


def _layer_params(wl, tt):
    n = tt // CHUNK
    tril = jnp.tril(jnp.ones((CHUNK, CHUNK), bool))
    wm = jnp.where(tril[None], wl["sgu_w"], 0.0)
    eye = jnp.eye(n, dtype=F32)
    wt = (eye[None, :, None, :, None] * wm[:, None, :, None, :]).reshape(N_HEADS_B, tt, tt)
    bias_e = jnp.repeat(wl["sgu_b"].T, HEAD, axis=1)
    return (_pad_rows(wl["conv_a_w"], 8), _row(wl["sgu_ln_g"]), _row(wl["sgu_ln_b"]), wt.astype(MM_DTYPE),
            jnp.tile(bias_e, (n, 1)), _pad_rows(wl["conv_c_w"], 32), _row(wl["conv_c_b"]), _row(wl["conv_ln_g"]),
            _row(wl["conv_ln_b"]))


def layer_fwd(x, wl, gw, layer, tm=TILE_MM, tt=TILE_EW):
    mp = _layer_params(wl, tt)
    z, h = norm_matmul(x, _row(wl["pre_mix_g"]), gw["w_in"], layer, tm)
    o, x1, cv = mixer_fwd(z, x, mp, _tile_perm(tt), _row(wl["grp_norm_g"]), gw["w_out"], layer, _row(wl["post_mix_g"]), tt)
    up0, h2 = norm_matmul(x1, _row(wl["pre_ffn_g"]), gw["w_up"], layer, tt)
    d, x2 = ffn_fwd(up0, x1, _pad_rows(wl["ffn_conv_w"], 8), gw["w_down"], layer, _row(wl["post_ffn_g"]), tt)
    return x2, dict(x=x, z=z, h=h, o=o, x1=x1, up0=up0, h2=h2, d=d, cv=cv)


def layer_bwd(dx2, wl, gw, layer, sv, gbuf, tm=TILE_MM, tt=TILE_EW):
    mp = _layer_params(wl, tt)
    tk = min(512, dx2.shape[0])
    g = {}
    gbuf = dict(gbuf)
    dd, act, dup0, dpg, dcw = ffn_bwd(dx2, sv["d"], sv["up0"], _pad_rows(wl["ffn_conv_w"], 8), gw["w_down"], layer,
                                      _row(wl["post_ffn_g"]), tt)
    g["post_ffn_g"] = dpg[0]
    g["ffn_conv_w"] = dcw[:K_F]
    gbuf["w_down"] = matmul_tn_down(act, dd, gbuf["w_down"], layer, tk)
    gbuf["w_up"] = matmul_tn_blocks(sv["h2"], dup0, gbuf["w_up"], layer, tk, by_rows=False)
    dx1, dg = matmul_nt_norm_bwd(dup0, gw["w_up"], layer, sv["x1"], _row(wl["pre_ffn_g"]), dx2, tm)
    g["pre_ffn_g"] = dg[0]
    (dz, do, yn, dpg, dgg, dcaw, dlng, dlnb, dwm, dbias, dccw, dccb, dclg, dclb) = mixer_bwd(
        dx1, sv["o"], sv["z"], sv["cv"], mp, _tile_perm(tt), _row(wl["grp_norm_g"]), gw["w_out"], layer, _row(wl["post_mix_g"]), tt)
    g["post_mix_g"] = dpg[0]
    g["grp_norm_g"] = dgg[0]
    g["conv_a_w"] = dcaw[:K_A]
    g["sgu_ln_g"] = dlng[0]
    g["sgu_ln_b"] = dlnb[0]
    g["sgu_w"] = dwm
    g["sgu_b"] = jnp.sum(dbias.reshape(CHUNK, N_HEADS_B, HEAD), axis=2).T
    g["conv_c_w"] = dccw[:K_C]
    g["conv_c_b"] = dccb[0]
    g["conv_ln_g"] = dclg[0]
    g["conv_ln_b"] = dclb[0]
    gbuf["w_out"] = matmul_tn_blocks(yn, do, gbuf["w_out"], layer, tk, by_rows=True)
    gbuf["w_in"] = matmul_tn_in(sv["h"], dz, gbuf["w_in"], layer, tk)
    dx, dg = matmul_nt_norm_bwd(dz, gw["w_in"], layer, sv["x"], _row(wl["pre_mix_g"]), dx1, tm)
    g["pre_mix_g"] = dg[0]
    return dx, g, gbuf


def grad_buffers():
    sds = lambda *s: lax.empty(s, F32)
    return dict(w_in=sds(N_CHIPS, DEPTH, D_MODEL, D_IN // N_CHIPS), w_out=sds(N_CHIPS, DEPTH, D_MODEL // N_CHIPS, D_MODEL),
                w_up=sds(N_CHIPS, DEPTH, D_MODEL, 2 * D_FF // N_CHIPS), w_down=sds(2, 2, DEPTH, D_FF // N_CHIPS, D_MODEL))


BIG = ["w_in", "w_out", "w_up", "w_down"]
CONV = ["conv_a_w", "conv_c_w", "ffn_conv_w"]
REPL = ["pre_mix_g", "sgu_ln_g", "sgu_ln_b", "sgu_w", "sgu_b", "conv_c_b", "conv_ln_g", "conv_ln_b", "grp_norm_g",
        "post_mix_g", "pre_ffn_g", "post_ffn_g"]
WEIGHTS = ["pre_mix_g", "w_in", "conv_a_w", "sgu_ln_g", "sgu_ln_b", "sgu_w", "sgu_b", "conv_c_w", "conv_c_b", "conv_ln_g",
           "conv_ln_b", "grp_norm_g", "w_out", "post_mix_g", "pre_ffn_g", "w_up", "ffn_conv_w", "w_down", "post_ffn_g"]


def kernel(x, pre_mix_g, w_in, conv_a_w, sgu_ln_g, sgu_ln_b, sgu_w, sgu_b, conv_c_w, conv_c_b, conv_ln_g, conv_ln_b, grp_norm_g, w_out, post_mix_g, pre_ffn_g, w_up, ffn_conv_w, w_down, post_ffn_g, loss_target, m_pre_mix_g, m_w_in, m_conv_a_w, m_sgu_ln_g, m_sgu_ln_b, m_sgu_w, m_sgu_b, m_conv_c_w, m_conv_c_b, m_conv_ln_g, m_conv_ln_b, m_grp_norm_g, m_w_out, m_post_mix_g, m_pre_ffn_g, m_w_up, m_ffn_conv_w, m_w_down, m_post_ffn_g, v_pre_mix_g, v_w_in, v_conv_a_w, v_sgu_ln_g, v_sgu_ln_b, v_sgu_w, v_sgu_b, v_conv_c_w, v_conv_c_b, v_conv_ln_g, v_conv_ln_b, v_grp_norm_g, v_w_out, v_post_mix_g, v_pre_ffn_g, v_w_up, v_ffn_conv_w, v_w_down, v_post_ffn_g):
    w = dict(pre_mix_g=pre_mix_g, w_in=w_in, conv_a_w=conv_a_w, sgu_ln_g=sgu_ln_g, sgu_ln_b=sgu_ln_b, sgu_w=sgu_w, sgu_b=sgu_b,
             conv_c_w=conv_c_w, conv_c_b=conv_c_b, conv_ln_g=conv_ln_g, conv_ln_b=conv_ln_b, grp_norm_g=grp_norm_g,
             w_out=w_out, post_mix_g=post_mix_g, pre_ffn_g=pre_ffn_g, w_up=w_up, ffn_conv_w=ffn_conv_w, w_down=w_down,
             post_ffn_g=post_ffn_g)
    m = dict(pre_mix_g=m_pre_mix_g, w_in=m_w_in, conv_a_w=m_conv_a_w, sgu_ln_g=m_sgu_ln_g, sgu_ln_b=m_sgu_ln_b,
             sgu_w=m_sgu_w, sgu_b=m_sgu_b, conv_c_w=m_conv_c_w, conv_c_b=m_conv_c_b, conv_ln_g=m_conv_ln_g,
             conv_ln_b=m_conv_ln_b, grp_norm_g=m_grp_norm_g, w_out=m_w_out, post_mix_g=m_post_mix_g,
             pre_ffn_g=m_pre_ffn_g, w_up=m_w_up, ffn_conv_w=m_ffn_conv_w, w_down=m_w_down, post_ffn_g=m_post_ffn_g)
    v = dict(pre_mix_g=v_pre_mix_g, w_in=v_w_in, conv_a_w=v_conv_a_w, sgu_ln_g=v_sgu_ln_g, sgu_ln_b=v_sgu_ln_b,
             sgu_w=v_sgu_w, sgu_b=v_sgu_b, conv_c_w=v_conv_c_w, conv_c_b=v_conv_c_b, conv_ln_g=v_conv_ln_g,
             conv_ln_b=v_conv_ln_b, grp_norm_g=v_grp_norm_g, w_out=v_w_out, post_mix_g=v_post_mix_g,
             pre_ffn_g=v_pre_ffn_g, w_up=v_w_up, ffn_conv_w=v_ffn_conv_w, w_down=v_w_down, post_ffn_g=v_post_ffn_g)
    chip = 2 * lax.axis_index("x") + lax.axis_index("y")

    shards, convs = gather_weights([cast_shard(w[n], chip) for n in BIG], _pack([w[n] for n in CONV], CONV_ROWS))
    gw = dict(zip(BIG, shards))
    cparts = [_unpack(convs[j], CONV_SHARDS) for j in range(N_CHIPS)]
    full = dict(w)
    for i, n in enumerate(CONV):
        full[n] = _join_cols(jnp.stack([p[i] for p in cparts]))

    xc = to_tiles(x[0], TILE_EW)
    saved = []
    for layer in range(DEPTH):
        xc, sv = layer_fwd(xc, {n: full[n][layer] for n in REPL + CONV}, gw, layer)
        saved.append(sv)
    dxc, loss_part = loss_head(xc, to_tiles(loss_target[0], TILE_EW), TILE_MM)
    loss = lax.psum(loss_part[0, 0], ("x", "y", "c"))
    gbuf = grad_buffers()
    small = [None] * DEPTH
    for layer in reversed(range(DEPTH)):
        dxc, small[layer], gbuf = layer_bwd(dxc, {n: full[n][layer] for n in REPL + CONV}, gw, layer, saved[layer], gbuf)
    grads = {n: jnp.stack([small[layer][n] for layer in range(DEPTH)]) for n in REPL + CONV}

    gbuf["w_down"] = gbuf["w_down"].reshape(N_CHIPS, DEPTH, D_FF // N_CHIPS, D_MODEL)
    gs = [gbuf[n] for n in BIG]
    sums = [add_halves(g, got) for g, got in zip(gs, swap_halves(gs))]
    arrived = scatter_to_chips([s16 for _, s16 in sums])
    out_g = dict(zip(BIG, join_halves([add_chips(s32, got) for (s32, _), got in zip(sums, arrived)])))

    tot = sum_devices(gather_all(_pack([grads[n] for n in REPL + CONV], SMALL_ROWS)), 192)
    shapes = [grads[n].shape for n in REPL + CONV]
    for n, gfull in zip(REPL + CONV, _unpack(tot, shapes)):
        if n in CONV:
            width = gfull.shape[-1] // N_CHIPS
            gfull = lax.dynamic_slice_in_dim(gfull, chip * width, width, axis=2)
        out_g[n] = gfull

    deltas, new_m, new_v = {}, {}, {}
    for n in WEIGHTS:
        deltas[n], new_m[n], new_v[n] = adamw(w[n], out_g[n], m[n], v[n])
    return (loss, from_tiles(dxc, TILE_EW)[None], *[out_g[n] for n in WEIGHTS], *[deltas[n] for n in WEIGHTS], *[new_m[n] for n in WEIGHTS],
            *[new_v[n] for n in WEIGHTS])
```

```python
import functools
import math
from typing import Callable, NamedTuple

import jax
import jax.numpy as jnp
from jax import lax
from jax.experimental import pallas as pl
from jax.experimental.pallas import tpu as pltpu

F32 = jnp.float32
BF16 = jnp.bfloat16
MM_DTYPE = BF16

D_MODEL = 1024
SEQ = 4096
DEPTH = 4
D_A = 256
D_B = 384
D_C = 384
D_IN = 3 * D_A + 2 * D_B + 2 * D_C
D_FF = 2816
K_A = 3
K_C = 31
K_F = 3
CHUNK = 128
HEAD = 64
N_HEADS_B = D_B // HEAD
EPS = 1e-6
N_CHIPS = 4

ADAM_LR = 0.001
ADAM_B1 = 0.9
ADAM_B2 = 0.999
ADAM_EPS = 1e-08
ADAM_WD = 0.01
ADAM_STEP = 10

LANES = 1024
VMEM_LIMIT = 56 * 1024 * 1024

MESH_ID = pl.DeviceIdType.MESH
_ANY = pl.BlockSpec(memory_space=pl.ANY)


def _params(sem=None):
    return pltpu.CompilerParams(dimension_semantics=sem, vmem_limit_bytes=VMEM_LIMIT)


def _const_spec(shape):
    nd = len(shape)
    return pl.BlockSpec(shape, lambda *_: (0,) * nd, pipeline_mode=pl.Buffered(1))


def _rowsum8(a):
    r, c = a.shape
    return jnp.sum(a.reshape(r // 8, 8, c), axis=0)


def _rstd(x):
    return lax.rsqrt(jnp.mean(x * x, axis=-1, keepdims=True) + EPS)


def _rms_bwd(x, r, g, dy):
    gdy = g * dy
    return r * gdy - x * (r * r * r) * jnp.mean(gdy * x, axis=-1, keepdims=True)


def _ln_fwd(x):
    mu = jnp.mean(x, axis=-1, keepdims=True)
    xc = x - mu
    r = lax.rsqrt(jnp.mean(xc * xc, axis=-1, keepdims=True) + EPS)
    return xc * r, r


def _ln_bwd(xh, r, dxh):
    return r * (dxh - jnp.mean(dxh, axis=-1, keepdims=True) - xh * jnp.mean(dxh * xh, axis=-1, keepdims=True))


def _gelu(x):
    return 0.5 * x * (1.0 + lax.erf(x * (1.0 / math.sqrt(2.0))))


def _gelu_grad(x):
    cdf = 0.5 * (1.0 + lax.erf(x * (1.0 / math.sqrt(2.0))))
    pdf = jnp.exp(-0.5 * x * x) * (1.0 / math.sqrt(2.0 * math.pi))
    return cdf + x * pdf


def _dot(a, b):
    return jnp.dot(a, b, preferred_element_type=F32)


def _dot_nt(a, b):
    return lax.dot_general(a, b, (((1,), (1,)), ((), ())), preferred_element_type=F32)


def _dot_tn(a, b):
    return lax.dot_general(a, b, (((0,), (0,)), ((), ())), preferred_element_type=F32)


def _col_chunk(n):
    for c in (1408, 1024, 768, 512, 256, 128):
        if n % c == 0:
            return c
    raise ValueError(n)


class Rider(NamedTuple):
    name: str
    inputs: list
    out_shapes: list
    aliases: dict
    sems: tuple
    start: Callable
    wait: Callable


def _pallas(body, rider, *, name, steps, in_specs, out_specs, out_shape, scratch_shapes, args):
    if rider is None:
        res = pl.pallas_call(body, name=name, grid=(steps,), in_specs=in_specs, out_specs=out_specs, out_shape=out_shape,
                             scratch_shapes=scratch_shapes, compiler_params=_params(("arbitrary",)))(*args)
        return res, []
    n_in, n_out, n_scr = len(in_specs), len(out_specs), len(scratch_shapes)
    r_in, r_out = len(rider.inputs), len(rider.out_shapes)

    def wrapped(*refs):
        ins, rin = refs[:n_in], refs[n_in:n_in + r_in]
        at = n_in + r_in
        outs, rout = refs[at:at + n_out], refs[at + n_out:at + n_out + r_out]
        at += n_out + r_out
        scr, rsem = refs[at:at + n_scr], refs[at + n_scr:]

        @pl.when(pl.program_id(0) == 0)
        def _():
            rider.start(rin, rout, rsem)

        body(*ins, *outs, *scr)

        @pl.when(pl.program_id(0) == steps - 1)
        def _():
            rider.wait(rin, rout, rsem)

    res = pl.pallas_call(
        wrapped, name=name + "_" + rider.name, grid=(steps,), in_specs=list(in_specs) + [_ANY] * r_in,
        out_specs=list(out_specs) + [_ANY] * r_out, out_shape=list(out_shape) + list(rider.out_shapes),
        scratch_shapes=list(scratch_shapes) + [pltpu.SemaphoreType.DMA((n,)) for n in rider.sems],
        input_output_aliases={n_in + i: n_out + o for i, o in rider.aliases.items()},
        compiler_params=pltpu.CompilerParams(dimension_semantics=("arbitrary",), vmem_limit_bytes=VMEM_LIMIT,
                                             has_side_effects=True),
    )(*args, *rider.inputs)
    return res[:n_out], res[n_out:]


def run_rider(rider):
    def body(*refs):
        r_in, r_out = len(rider.inputs), len(rider.out_shapes)
        rin, rout, rsem = refs[:r_in], refs[r_in:r_in + r_out], refs[r_in + r_out:]
        rider.start(rin, rout, rsem)
        rider.wait(rin, rout, rsem)

    return pl.pallas_call(
        body, name=rider.name, in_specs=[_ANY] * len(rider.inputs), out_specs=[_ANY] * len(rider.out_shapes),
        out_shape=list(rider.out_shapes), scratch_shapes=[pltpu.SemaphoreType.DMA((n,)) for n in rider.sems],
        input_output_aliases=dict(rider.aliases), compiler_params=pltpu.CompilerParams(has_side_effects=True),
    )(*rider.inputs)


def _layer_spec(wg, layer):
    _, _, r, c = wg.shape
    return pl.BlockSpec((N_CHIPS, None, r, c), lambda *_: (0, layer, 0, 0), pipeline_mode=pl.Buffered(1))


def _join_col_blocks(w_ref, w_scr):
    c = w_ref.shape[2]
    for j in range(N_CHIPS):
        w_scr[:, c * j:c * (j + 1)] = w_ref[j]


def norm_matmul(x, g, wg, layer, tm):
    t, d = x.shape
    assert t % tm == 0, (t, tm)
    cw = wg.shape[3]
    n = N_CHIPS * cw
    aligned = cw % 128 == 0
    cn = cw if aligned else _col_chunk(n)

    def body(x_ref, g_ref, w_ref, o_ref, h_ref, *scr):
        if not aligned:
            @pl.when(pl.program_id(0) == 0)
            def _():
                _join_col_blocks(w_ref, scr[0])

        xv = x_ref[...]
        h = (xv * _rstd(xv) * g_ref[...]).astype(MM_DTYPE)
        h_ref[...] = h
        for j, c0 in enumerate(range(0, n, cn)):
            wv = w_ref[j] if aligned else scr[0][:, c0:c0 + cn]
            o_ref[:, c0:c0 + cn] = _dot(h, wv)

    return pl.pallas_call(
        body, name="norm_matmul", grid=(t // tm,),
        in_specs=[pl.BlockSpec((tm, d), lambda i: (i, 0)), _const_spec((1, d)), _layer_spec(wg, layer)],
        out_specs=[pl.BlockSpec((tm, n), lambda i: (i, 0)), pl.BlockSpec((tm, d), lambda i: (i, 0))],
        out_shape=[jax.ShapeDtypeStruct((t, n), F32), jax.ShapeDtypeStruct((t, d), MM_DTYPE)],
        scratch_shapes=[] if aligned else [pltpu.VMEM((d, n), MM_DTYPE)],
        compiler_params=_params(("arbitrary",)),
    )(x, g, wg)


def matmul_nt_norm_bwd(gy, wg, layer, x, g, dres, tm, rider=None):
    t, n = gy.shape
    assert t % tm == 0, (t, tm)
    d, cw = wg.shape[2], wg.shape[3]
    aligned = cw % 128 == 0
    cn = cw if aligned else _col_chunk(n)
    steps = t // tm

    def body(gy_ref, w_ref, x_ref, g_ref, dres_ref, dx_ref, dg_ref, acc_ref, *scr):
        i = pl.program_id(0)

        @pl.when(i == 0)
        def _():
            acc_ref[...] = jnp.zeros_like(acc_ref)
            if not aligned:
                _join_col_blocks(w_ref, scr[0])

        dh = jnp.zeros((tm, d), F32)
        for j, c0 in enumerate(range(0, n, cn)):
            wv = w_ref[j] if aligned else scr[0][:, c0:c0 + cn]
            dh = dh + _dot_nt(gy_ref[:, c0:c0 + cn], wv)
        xv = x_ref[...]
        r = _rstd(xv)
        gv = g_ref[...]
        dx_ref[...] = dres_ref[...] + _rms_bwd(xv, r, gv, dh)
        acc_ref[...] += _rowsum8(dh * xv * r)

        @pl.when(i == steps - 1)
        def _():
            dg_ref[...] = jnp.sum(acc_ref[...], axis=0, keepdims=True)

    return _pallas(
        body, rider, name="matmul_nt_norm_bwd", steps=steps,
        in_specs=[pl.BlockSpec((tm, n), lambda i: (i, 0)), _layer_spec(wg, layer), pl.BlockSpec((tm, d), lambda i: (i, 0)),
                  _const_spec((1, d)), pl.BlockSpec((tm, d), lambda i: (i, 0))],
        out_specs=[pl.BlockSpec((tm, d), lambda i: (i, 0)), pl.BlockSpec((1, d), lambda i: (0, 0))],
        out_shape=[jax.ShapeDtypeStruct((t, d), F32), jax.ShapeDtypeStruct((1, d), F32)],
        scratch_shapes=[pltpu.VMEM((8, d), F32)] + ([] if aligned else [pltpu.VMEM((d, n), MM_DTYPE)]),
        args=(gy, wg, x, g, dres))


def matmul_tn_blocks(a, b, r, c, tk, by_rows):
    t = a.shape[0]
    assert t % tk == 0 and r % 8 == 0 and c % 128 == 0, (a.shape, b.shape, r, c, tk)

    def body(a_ref, b_ref, o_ref):
        @pl.when(pl.program_id(1) == 0)
        def _():
            o_ref[...] = jnp.zeros_like(o_ref)

        o_ref[...] += _dot_tn(a_ref[...], b_ref[...])

    a_spec = pl.BlockSpec((tk, r), (lambda j, k: (k, j)) if by_rows else (lambda j, k: (k, 0)))
    b_spec = pl.BlockSpec((tk, c), (lambda j, k: (k, 0)) if by_rows else (lambda j, k: (k, j)))
    return pl.pallas_call(
        body, name="matmul_tn_blocks", grid=(N_CHIPS, t // tk), in_specs=[a_spec, b_spec],
        out_specs=pl.BlockSpec((None, r, c), lambda j, k: (j, 0, 0)),
        out_shape=jax.ShapeDtypeStruct((N_CHIPS, r, c), F32),
        compiler_params=_params(("arbitrary", "arbitrary")),
    )(a, b)


def matmul_tn_down(act, dd, tk):
    t, m = act.shape
    c = dd.shape[1]
    r = m // N_CHIPS
    assert t % tk == 0, (t, tk)
    steps = t // tk

    def body(a_ref, b_ref, o_ref, acc):
        k = pl.program_id(1)

        @pl.when(k == 0)
        def _():
            acc[...] = jnp.zeros_like(acc)

        acc[...] += _dot_tn(a_ref[...], b_ref[...])

        @pl.when(k == steps - 1)
        def _():
            o_ref[0] = acc[0:r, :]
            o_ref[1] = acc[r:2 * r, :]

    return pl.pallas_call(
        body, name="matmul_tn_down", grid=(2, steps),
        in_specs=[pl.BlockSpec((tk, 2 * r), lambda p, k: (k, p)), pl.BlockSpec((tk, c), lambda p, k: (k, 0))],
        out_specs=pl.BlockSpec((2, r, c), lambda p, k: (p, 0, 0)),
        out_shape=jax.ShapeDtypeStruct((N_CHIPS, r, c), F32),
        scratch_shapes=[pltpu.VMEM((2 * r, c), F32)],
        compiler_params=_params(("arbitrary", "arbitrary")),
    )(act, dd)


def matmul_tn_in(h, dz, tk):
    t, d = h.shape
    n = dz.shape[1]
    c = n // N_CHIPS
    assert t % tk == 0, (t, tk)
    steps = t // tk

    def body(a_ref, b_ref, o_ref, acc):
        k = pl.program_id(0)

        @pl.when(k == 0)
        def _():
            acc[...] = jnp.zeros_like(acc)

        acc[...] += _dot_tn(a_ref[...], b_ref[...])

        @pl.when(k == steps - 1)
        def _():
            for j in range(N_CHIPS):
                o_ref[j] = acc[:, c * j:c * (j + 1)]

    return pl.pallas_call(
        body, name="matmul_tn_in", grid=(steps,),
        in_specs=[pl.BlockSpec((tk, d), lambda k: (k, 0)), pl.BlockSpec((tk, n), lambda k: (k, 0))],
        out_specs=pl.BlockSpec((N_CHIPS, d, c), lambda k: (0, 0, 0)),
        out_shape=jax.ShapeDtypeStruct((N_CHIPS, d, c), F32),
        scratch_shapes=[pltpu.VMEM((d, n), F32)],
        compiler_params=_params(("arbitrary",)),
    )(h, dz)


def to_tiles(a, tt):
    t = a.shape[0]
    return a.reshape((t // tt, 8, tt // 8) + a.shape[1:]).swapaxes(1, 2).reshape(a.shape)


def from_tiles(a, tt):
    t = a.shape[0]
    return a.reshape((t // tt, tt // 8, 8) + a.shape[1:]).swapaxes(1, 2).reshape(a.shape)


def _roll_sublanes(a, shift):
    n = a.shape[0] // 8
    return pltpu.roll(a.reshape(n, 8, a.shape[1]), shift, 1).reshape(a.shape)


def _halo_before(cur_last, prev_last):
    sub = lax.broadcasted_iota(jnp.int32, cur_last.shape, 0) % 8
    return jnp.where(sub == 0, _roll_sublanes(prev_last, 1), _roll_sublanes(cur_last, 1))


def _halo_after(cur_first, next_first):
    sub = lax.broadcasted_iota(jnp.int32, cur_first.shape, 0) % 8
    return jnp.where(sub == 7, _roll_sublanes(next_first, 7), _roll_sublanes(cur_first, 7))


def _conv_causal(ext, cur, prev_last, w, taps, tt, cols=None):
    hr = 8 * (taps - 1)
    cs = slice(None) if cols is None else cols
    ext[hr:hr + tt, cs] = cur
    ext[0:hr, cs] = _halo_before(cur[tt - hr:, :], prev_last)
    acc = w[0:1, :] * ext[0:tt, cs]
    for k in range(1, taps):
        acc = acc + w[k:k + 1, :] * ext[8 * k:8 * k + tt, cs]
    return acc


def _conv_anticausal(ext, cur, next_first, w, taps, tt, x=None, acc_w=None, cols=None):
    hr = 8 * (taps - 1)
    cs = slice(None) if cols is None else cols
    ext[0:tt, cs] = cur
    ext[tt:tt + hr, cs] = _halo_after(cur[0:hr, :], next_first)
    acc = None
    for k in range(taps):
        off = 8 * (taps - 1 - k)
        ld = ext[off:off + tt, cs]
        term = w[k:k + 1, :] * ld
        acc = term if acc is None else acc + term
        if x is not None:
            acc_w[k, :, cs] += _rowsum8(ld * x)
    return acc


def _dot_exact(a, b, dims):
    return lax.dot_general(a, b, (dims, ((), ())), precision=lax.Precision.HIGHEST, preferred_element_type=F32)


def _to_tile_order(perm, wt_ref, w_scr, transpose):
    pb = perm.astype(MM_DTYPE)
    for h in range(N_HEADS_B):
        half = (_dot_nt(pb, wt_ref[h]) if transpose else _dot(pb, wt_ref[h])).astype(MM_DTYPE)
        w_scr[h] = _dot_nt(half, pb).astype(MM_DTYPE)


def _project_rows(y, w_ref):
    r = w_ref.shape[1]
    acc = _dot(y[:, 0:r], w_ref[0])
    for j in range(1, N_CHIPS):
        acc = acc + _dot(y[:, r * j:r * (j + 1)], w_ref[j])
    return acc


def _head_select(parts):
    head = lax.broadcasted_iota(jnp.int32, parts[0].shape, 1) // HEAD
    acc = parts[0]
    for h in range(1, N_HEADS_B):
        acc = jnp.where(head == h, parts[h], acc)
    return acc


def _mixer_forward(z, prm, q, yc):
    _, lng, lnb, wm, bias_p, _, _, clg, clb = prm
    bg = z[:, 0:D_A]
    ya = bg * q
    o_b = 3 * D_A
    zu = z[:, o_b:o_b + D_B]
    zv = z[:, o_b + D_B:o_b + 2 * D_B]
    u = _gelu(zu)
    vh, rv = _ln_fwd(_gelu(zv))
    vnb = (vh * lng + lnb).astype(MM_DTYPE)
    s = _head_select([_dot(wm[h], vnb) for h in range(N_HEADS_B)]) + bias_p
    yb = u * s
    yh, rc = _ln_fwd(yc)
    l = yh * clg + clb
    sl = jax.nn.sigmoid(l)
    return dict(bg=bg, q=q, ya=ya, zu=zu, zv=zv, u=u, vh=vh, rv=rv, vnb=vnb, s=s, yb=yb, yh=yh, rc=rc, l=l, sl=sl,
                yo=l * sl)


def _conv_inputs(z):
    o_c = 3 * D_A + 2 * D_B
    a = z[:, o_c:o_c + D_C]
    sg = jax.nn.sigmoid(z[:, o_c + D_C:o_c + 2 * D_C])
    return z[:, D_A:2 * D_A] * z[:, 2 * D_A:3 * D_A], a * sg, a, sg


def _group_norm(f, gg):
    ya, yb, yo = f["ya"], f["yb"], f["yo"]
    ra, rb, ro = _rstd(ya), _rstd(yb), _rstd(yo)
    yn = jnp.concatenate([ya * ra * gg[:, 0:D_A], yb * rb * gg[:, D_A:D_A + D_B], yo * ro * gg[:, D_A + D_B:]], axis=1)
    return yn, (ra, rb, ro)


def _mixer_prm(refs, wp_scr, bias_scr):
    caw_ref, lng_ref, lnb_ref, _, _, ccw_ref, ccb_ref, clg_ref, clb_ref = refs
    wm = [wp_scr[h] for h in range(N_HEADS_B)]
    return (caw_ref[...], lng_ref[...], lnb_ref[...], wm, bias_scr[...], ccw_ref[...], ccb_ref[...], clg_ref[...],
            clb_ref[...])


def _mixer_param_specs(tt):
    return [_const_spec((8, D_A)), _const_spec((1, D_B)), _const_spec((1, D_B)), _const_spec((N_HEADS_B, tt, tt)),
            _const_spec((tt, D_B)), _const_spec((32, D_C)), _const_spec((1, D_C)), _const_spec((1, D_C)),
            _const_spec((1, D_C))]


HR_A = 8 * (K_A - 1)
HR_C = 8 * (K_C - 1)
HR_F = 8 * (K_F - 1)


def mixer_fwd(z, x, mp, perm, grp_g, wog, layer, post_g, tt):
    t = z.shape[0]
    assert t % tt == 0 and tt % CHUNK == 0 and tt >= HR_C, (t, tt)

    def body(z_ref, x_ref, *rest):
        prm_refs = rest[:9]
        (perm_ref, gg_ref, wo_ref, pg_ref, o_ref, x1_ref, cv_ref, pa_ext, yg_ext, pa_last, yg_last, wp_scr, bias_scr) = rest[9:]
        i = pl.program_id(0)

        @pl.when(i == 0)
        def _():
            pa_last[...] = jnp.zeros_like(pa_last)
            yg_last[...] = jnp.zeros_like(yg_last)
            _to_tile_order(perm_ref[...], prm_refs[3], wp_scr, False)
            bias_scr[...] = _dot_exact(perm_ref[...], prm_refs[4][...], ((1,), (0,)))

        zv = z_ref[...]
        prm = _mixer_prm(prm_refs, wp_scr, bias_scr)
        pa, yg, _, _ = _conv_inputs(zv)
        q = _conv_causal(pa_ext, pa, pa_last[...], prm[0], K_A, tt)
        yc = _conv_causal(yg_ext, yg, yg_last[...], prm[5], K_C, tt) + prm[6]
        pa_last[...] = pa[tt - HR_A:, :]
        yg_last[...] = yg[tt - HR_C:, :]
        cv_ref[:, 0:D_A] = q
        cv_ref[:, D_A:] = yc
        f = _mixer_forward(zv, prm, q, yc)
        yn, _ = _group_norm(f, gg_ref[...])
        o = _project_rows(yn.astype(MM_DTYPE), wo_ref)
        o_ref[...] = o
        x1_ref[...] = x_ref[...] + o * _rstd(o) * pg_ref[...]

    row = lambda c: pl.BlockSpec((tt, c), lambda i: (i, 0))
    return pl.pallas_call(
        body, name="mixer_fwd", grid=(t // tt,),
        in_specs=[row(D_IN), row(D_MODEL)] + _mixer_param_specs(tt)
        + [_const_spec((tt, tt)), _const_spec((1, D_MODEL)), _layer_spec(wog, layer), _const_spec((1, D_MODEL))],
        out_specs=[row(D_MODEL), row(D_MODEL), row(D_A + D_C)],
        out_shape=[jax.ShapeDtypeStruct((t, D_MODEL), F32), jax.ShapeDtypeStruct((t, D_MODEL), F32),
                   jax.ShapeDtypeStruct((t, D_A + D_C), F32)],
        scratch_shapes=[pltpu.VMEM((HR_A + tt, D_A), F32), pltpu.VMEM((HR_C + tt, D_C), F32),
                        pltpu.VMEM((HR_A, D_A), F32), pltpu.VMEM((HR_C, D_C), F32),
                        pltpu.VMEM((N_HEADS_B, tt, tt), MM_DTYPE), pltpu.VMEM((tt, D_B), F32)],
        compiler_params=_params(("arbitrary",)),
    )(z, x, *mp, perm, grp_g, wog, post_g)


def mixer_bwd(dx1, o, z, cv, mp, perm, grp_g, wog, layer, post_g, tt, rider=None):
    t = z.shape[0]
    assert t % tt == 0 and tt % CHUNK == 0 and tt >= HR_C, (t, tt)
    steps = t // tt

    def body(dx1_ref, o_ref, z_ref, cv_ref, *rest):
        prm_refs = rest[:9]
        (perm_ref, gg_ref, wo_ref, pg_ref,
         dz_ref, do_ref, yn_ref, dpg_ref, dgg_ref, dcaw_ref, dlng_ref, dlnb_ref, dwm_ref, dbias_ref, dccw_ref, dccb_ref,
         dclg_ref, dclb_ref,
         dq_ext, dyc_ext, dq_first, dyc_first, a_pg, a_gg, a_caw, a_lng, a_lnb, a_ccw, a_ccb, a_clg, a_clb,
         wp_scr, wpt_scr, bias_scr, a_wm, a_bias) = rest[9:]
        i = pl.program_id(0)
        small = (a_pg, a_gg, a_caw, a_lng, a_lnb, a_ccw, a_ccb, a_clg, a_clb)

        @pl.when(i == 0)
        def _():
            for ref in small + (a_wm, a_bias, dq_first, dyc_first):
                ref[...] = jnp.zeros_like(ref)
            _to_tile_order(perm_ref[...], prm_refs[3], wp_scr, False)
            _to_tile_order(perm_ref[...], prm_refs[3], wpt_scr, True)
            bias_scr[...] = _dot_exact(perm_ref[...], prm_refs[4][...], ((1,), (0,)))

        prm = _mixer_prm(prm_refs, wp_scr, bias_scr)
        caw, lng, lnb, wm, bias_p, ccw, ccb, clg, clb = prm

        zv = z_ref[...]
        pa, yg, a, sg = _conv_inputs(zv)
        f = _mixer_forward(zv, prm, cv_ref[:, 0:D_A], cv_ref[:, D_A:])
        gg = gg_ref[...]
        yn, (ra, rb, ro) = _group_norm(f, gg)
        yn_ref[...] = yn.astype(MM_DTYPE)

        ov = o_ref[...]
        dx1v = dx1_ref[...]
        r_o = _rstd(ov)
        pg = pg_ref[...]
        a_pg[...] += _rowsum8(dx1v * ov * r_o)
        do = _rms_bwd(ov, r_o, pg, dx1v).astype(MM_DTYPE)
        do_ref[...] = do
        dyn = jnp.concatenate([_dot_nt(do, wo_ref[j]) for j in range(N_CHIPS)], axis=1)

        dyn_a, dyn_b, dyn_c = dyn[:, 0:D_A], dyn[:, D_A:D_A + D_B], dyn[:, D_A + D_B:]
        ga, gb, gc = gg[:, 0:D_A], gg[:, D_A:D_A + D_B], gg[:, D_A + D_B:]
        a_gg[...] += _rowsum8(jnp.concatenate([dyn_a * f["ya"] * ra, dyn_b * f["yb"] * rb, dyn_c * f["yo"] * ro], axis=1))
        dya = _rms_bwd(f["ya"], ra, ga, dyn_a)
        dyb = _rms_bwd(f["yb"], rb, gb, dyn_b)
        dyo = _rms_bwd(f["yo"], ro, gc, dyn_c)

        dbg = dya * f["q"]
        dq = dya * f["bg"]
        dp = _conv_anticausal(dq_ext, dq, dq_first[...], caw, K_A, tt, x=pa, acc_w=a_caw)
        dq_first[...] = dq[0:HR_A, :]
        dcg = dp * zv[:, 2 * D_A:3 * D_A]
        dxa = dp * zv[:, D_A:2 * D_A]

        du = dyb * f["s"]
        ds = dyb * f["u"]
        dsb = ds.astype(MM_DTYPE)
        head = lax.broadcasted_iota(jnp.int32, (tt, D_B), 1) // HEAD
        a_bias[...] += ds
        parts = []
        for h in range(N_HEADS_B):
            a_wm[h] += _dot_nt(jnp.where(head == h, dsb, jnp.zeros_like(dsb)), f["vnb"])
            parts.append(_dot(wpt_scr[h], dsb))
        dvn = _head_select(parts)
        a_lng[...] += _rowsum8(dvn * f["vh"])
        a_lnb[...] += _rowsum8(dvn)
        dv = _ln_bwd(f["vh"], f["rv"], dvn * lng)
        dzu = du * _gelu_grad(f["zu"])
        dzv = dv * _gelu_grad(f["zv"])

        l, sl = f["l"], f["sl"]
        dl = dyo * (sl * (1.0 + l * (1.0 - sl)))
        a_clg[...] += _rowsum8(dl * f["yh"])
        a_clb[...] += _rowsum8(dl)
        dyc = _ln_bwd(f["yh"], f["rc"], dl * clg)
        a_ccb[...] += _rowsum8(dyc)
        dy = _conv_anticausal(dyc_ext, dyc, dyc_first[...], ccw, K_C, tt, x=yg, acc_w=a_ccw)
        dyc_first[...] = dyc[0:HR_C, :]
        da = dy * sg
        dg = dy * a * sg * (1.0 - sg)

        dz_ref[...] = jnp.concatenate([dbg, dcg, dxa, dzu, dzv, da, dg], axis=1).astype(MM_DTYPE)

        @pl.when(i == steps - 1)
        def _():
            red = lambda ref: jnp.sum(ref[...], axis=0, keepdims=True)
            dpg_ref[...] = red(a_pg)
            dgg_ref[...] = red(a_gg)
            dlng_ref[...] = red(a_lng)
            dlnb_ref[...] = red(a_lnb)
            dccb_ref[...] = red(a_ccb)
            dclg_ref[...] = red(a_clg)
            dclb_ref[...] = red(a_clb)
            dcaw_ref[...] = jnp.sum(a_caw[...], axis=1)
            dccw_ref[...] = jnp.sum(a_ccw[...], axis=1)
            pm = perm_ref[...]
            tril = lax.broadcasted_iota(jnp.int32, (CHUNK, CHUNK), 0) >= lax.broadcasted_iota(jnp.int32, (CHUNK, CHUNK), 1)
            for h in range(N_HEADS_B):
                dwt = _dot_exact(pm, _dot_exact(a_wm[h], pm, ((1,), (0,))), ((0,), (0,)))
                dw = dwt[0:CHUNK, 0:CHUNK]
                for c in range(1, tt // CHUNK):
                    dw = dw + dwt[c * CHUNK:(c + 1) * CHUNK, c * CHUNK:(c + 1) * CHUNK]
                dwm_ref[h] = jnp.where(tril, dw, 0.0)
            dbt = _dot_exact(pm, a_bias[...], ((0,), (0,)))
            db = dbt[0:CHUNK, :]
            for c in range(1, tt // CHUNK):
                db = db + dbt[c * CHUNK:(c + 1) * CHUNK, :]
            dbias_ref[...] = db

    rev = lambda c: pl.BlockSpec((tt, c), lambda i: (steps - 1 - i, 0))
    full = lambda shape: pl.BlockSpec(shape, lambda i: (0,) * len(shape))
    sds = jax.ShapeDtypeStruct
    return _pallas(
        body, rider, name="mixer_bwd", steps=steps,
        in_specs=[rev(D_MODEL), rev(D_MODEL), rev(D_IN), rev(D_A + D_C)] + _mixer_param_specs(tt)
        + [_const_spec((tt, tt)), _const_spec((1, D_MODEL)), _layer_spec(wog, layer), _const_spec((1, D_MODEL))],
        out_specs=[rev(D_IN), rev(D_MODEL), rev(D_MODEL), full((1, D_MODEL)), full((1, D_MODEL)), full((8, D_A)),
                   full((1, D_B)), full((1, D_B)), full((N_HEADS_B, CHUNK, CHUNK)), full((CHUNK, D_B)), full((32, D_C)),
                   full((1, D_C)), full((1, D_C)), full((1, D_C))],
        out_shape=[sds((t, D_IN), MM_DTYPE), sds((t, D_MODEL), MM_DTYPE), sds((t, D_MODEL), MM_DTYPE),
                   sds((1, D_MODEL), F32), sds((1, D_MODEL), F32), sds((8, D_A), F32), sds((1, D_B), F32), sds((1, D_B), F32),
                   sds((N_HEADS_B, CHUNK, CHUNK), F32), sds((CHUNK, D_B), F32), sds((32, D_C), F32), sds((1, D_C), F32),
                   sds((1, D_C), F32), sds((1, D_C), F32)],
        scratch_shapes=[pltpu.VMEM((tt + HR_A, D_A), F32), pltpu.VMEM((tt + HR_C, D_C), F32),
                        pltpu.VMEM((HR_A, D_A), F32), pltpu.VMEM((HR_C, D_C), F32),
                        pltpu.VMEM((8, D_MODEL), F32), pltpu.VMEM((8, D_MODEL), F32), pltpu.VMEM((8, 8, D_A), F32),
                        pltpu.VMEM((8, D_B), F32), pltpu.VMEM((8, D_B), F32), pltpu.VMEM((32, 8, D_C), F32),
                        pltpu.VMEM((8, D_C), F32), pltpu.VMEM((8, D_C), F32), pltpu.VMEM((8, D_C), F32),
                        pltpu.VMEM((N_HEADS_B, tt, tt), MM_DTYPE), pltpu.VMEM((N_HEADS_B, tt, tt), MM_DTYPE),
                        pltpu.VMEM((tt, D_B), F32), pltpu.VMEM((N_HEADS_B, tt, tt), F32), pltpu.VMEM((tt, D_B), F32)],
        args=(dx1, o, z, cv, *mp, perm, grp_g, wog, post_g))


def _fetch_row_blocks(wg_ref, layer, w_scr, sems):
    r = wg_ref.shape[2]
    copies = [pltpu.make_async_copy(wg_ref.at[j, layer], w_scr.at[pl.ds(r * j, r), :], sems.at[j]) for j in range(N_CHIPS)]
    for cp in copies:
        cp.start()
    for cp in copies:
        cp.wait()


def _ffn_conv(ext, cw, c0, cn, tt):
    acc = cw[0:1, c0:c0 + cn] * ext[0:tt, c0:c0 + cn]
    for k in range(1, K_F):
        acc = acc + cw[k:k + 1, c0:c0 + cn] * ext[8 * k:8 * k + tt, c0:c0 + cn]
    return acc


def ffn_fwd(up0, x1, cw, wdg, layer, post_g, tt):
    t = up0.shape[0]
    assert t % tt == 0, (t, tt)
    cn = _col_chunk(D_FF)

    def body(up0_ref, x1_ref, cw_ref, wdg_ref, pg_ref, d_ref, x2_ref, ext, last, wd_ref, sems):
        i = pl.program_id(0)

        @pl.when(i == 0)
        def _():
            _fetch_row_blocks(wdg_ref, layer, wd_ref, sems)
            last[...] = jnp.zeros_like(last)

        ext[HR_F:HR_F + tt, :] = up0_ref[...]
        ext[0:HR_F, :] = _halo_before(up0_ref[tt - HR_F:, :], last[...])
        last[...] = up0_ref[tt - HR_F:, :]
        cwv = cw_ref[...]
        d = jnp.zeros((tt, D_MODEL), F32)
        for c0 in range(0, D_FF, cn):
            gate = _ffn_conv(ext, cwv, c0, cn, tt)
            val = _ffn_conv(ext, cwv, D_FF + c0, cn, tt)
            act = (gate * jax.nn.sigmoid(gate) * val).astype(MM_DTYPE)
            d = d + _dot(act, wd_ref[c0:c0 + cn, :])
        d_ref[...] = d
        x2_ref[...] = x1_ref[...] + d * _rstd(d) * pg_ref[...]

    row = lambda c: pl.BlockSpec((tt, c), lambda i: (i, 0))
    return pl.pallas_call(
        body, name="ffn_fwd", grid=(t // tt,),
        in_specs=[row(2 * D_FF), row(D_MODEL), _const_spec((8, 2 * D_FF)), _ANY, _const_spec((1, D_MODEL))],
        out_specs=[row(D_MODEL), row(D_MODEL)],
        out_shape=[jax.ShapeDtypeStruct((t, D_MODEL), F32), jax.ShapeDtypeStruct((t, D_MODEL), F32)],
        scratch_shapes=[pltpu.VMEM((HR_F + tt, 2 * D_FF), F32), pltpu.VMEM((HR_F, 2 * D_FF), F32),
                        pltpu.VMEM((D_FF, D_MODEL), MM_DTYPE), pltpu.SemaphoreType.DMA((N_CHIPS,))],
        compiler_params=_params(("arbitrary",)),
    )(up0, x1, cw, wdg, post_g)


def ffn_bwd(dx2, d, up0, cw, wdg, layer, post_g, tt, rider=None):
    t = up0.shape[0]
    assert t % tt == 0, (t, tt)
    steps = t // tt
    hb = tt // HR_F
    cn = _col_chunk(D_FF)

    def body(dx2_ref, d_ref, up0_ref, uh_ref, cw_ref, wdg_ref, pg_ref,
             dd_ref, act_ref, dup0_ref, dpg_ref, dcw_ref, ext, dup_ext, first, a_pg, a_cw, wd_ref, sems):
        i = pl.program_id(0)
        tile = steps - 1 - i

        @pl.when(i == 0)
        def _():
            _fetch_row_blocks(wdg_ref, layer, wd_ref, sems)
            a_pg[...] = jnp.zeros_like(a_pg)
            a_cw[...] = jnp.zeros_like(a_cw)
            first[...] = jnp.zeros_like(first)

        ext[HR_F:HR_F + tt, :] = up0_ref[...]
        ext[0:HR_F, :] = _halo_before(up0_ref[tt - HR_F:, :], jnp.where(tile > 0, uh_ref[...], 0.0))
        cwv = cw_ref[...]
        dv = d_ref[...]
        dx2v = dx2_ref[...]
        r = _rstd(dv)
        a_pg[...] += _rowsum8(dx2v * dv * r)
        dd = _rms_bwd(dv, r, pg_ref[...], dx2v).astype(MM_DTYPE)
        dd_ref[...] = dd
        for c0 in range(0, D_FF, cn):
            gate = _ffn_conv(ext, cwv, c0, cn, tt)
            val = _ffn_conv(ext, cwv, D_FF + c0, cn, tt)
            sg = jax.nn.sigmoid(gate)
            sl = gate * sg
            act_ref[:, c0:c0 + cn] = (sl * val).astype(MM_DTYPE)
            da = _dot_nt(dd, wd_ref[c0:c0 + cn, :])
            dup_ext[0:tt, c0:c0 + cn] = da * val * (sg * (1.0 + gate * (1.0 - sg)))
            dup_ext[0:tt, D_FF + c0:D_FF + c0 + cn] = da * sl
        dup_ext[tt:tt + HR_F, :] = _halo_after(dup_ext[0:HR_F, :], first[...])
        first[...] = dup_ext[0:HR_F, :]
        for c0 in range(0, 2 * D_FF, cn):
            x = up0_ref[:, c0:c0 + cn]
            acc = None
            for k in range(K_F):
                off = 8 * (K_F - 1 - k)
                ld = dup_ext[off:off + tt, c0:c0 + cn]
                term = cwv[k:k + 1, c0:c0 + cn] * ld
                acc = term if acc is None else acc + term
                a_cw[k, :, c0:c0 + cn] += _rowsum8(ld * x)
            dup0_ref[:, c0:c0 + cn] = acc.astype(MM_DTYPE)

        @pl.when(i == steps - 1)
        def _():
            dpg_ref[...] = jnp.sum(a_pg[...], axis=0, keepdims=True)
            dcw_ref[...] = jnp.sum(a_cw[...], axis=1)

    rev = lambda c: pl.BlockSpec((tt, c), lambda i: (steps - 1 - i, 0))
    halo = pl.BlockSpec((HR_F, 2 * D_FF), lambda i: (jnp.maximum((steps - 1 - i) * hb - 1, 0), 0))
    full = lambda shape: pl.BlockSpec(shape, lambda i: (0,) * len(shape))
    sds = jax.ShapeDtypeStruct
    return _pallas(
        body, rider, name="ffn_bwd", steps=steps,
        in_specs=[rev(D_MODEL), rev(D_MODEL), rev(2 * D_FF), halo, _const_spec((8, 2 * D_FF)), _ANY,
                  _const_spec((1, D_MODEL))],
        out_specs=[rev(D_MODEL), rev(D_FF), rev(2 * D_FF), full((1, D_MODEL)), full((8, 2 * D_FF))],
        out_shape=[sds((t, D_MODEL), MM_DTYPE), sds((t, D_FF), MM_DTYPE), sds((t, 2 * D_FF), MM_DTYPE),
                   sds((1, D_MODEL), F32), sds((8, 2 * D_FF), F32)],
        scratch_shapes=[pltpu.VMEM((HR_F + tt, 2 * D_FF), F32), pltpu.VMEM((tt + HR_F, 2 * D_FF), F32),
                        pltpu.VMEM((HR_F, 2 * D_FF), F32), pltpu.VMEM((8, D_MODEL), F32),
                        pltpu.VMEM((8, 8, 2 * D_FF), F32), pltpu.VMEM((D_FF, D_MODEL), MM_DTYPE),
                        pltpu.SemaphoreType.DMA((N_CHIPS,))],
        args=(dx2, d, up0, up0, cw, wdg, post_g))


def loss_head(y, target, tm):
    t, d = y.shape
    assert t % tm == 0, (t, tm)
    steps = t // tm

    def body(y_ref, t_ref, dy_ref, loss_ref, acc):
        i = pl.program_id(0)

        @pl.when(i == 0)
        def _():
            acc[...] = jnp.zeros_like(acc)

        diff = y_ref[...] - t_ref[...]
        dy_ref[...] = diff * (1.0 / d)
        acc[...] += _rowsum8(diff * diff)

        @pl.when(i == steps - 1)
        def _():
            loss_ref[...] = (0.5 / d) * jnp.sum(jnp.sum(acc[...], axis=0, keepdims=True), axis=1, keepdims=True)

    row = pl.BlockSpec((tm, d), lambda i: (i, 0))
    return pl.pallas_call(
        body, name="loss_head", grid=(steps,), in_specs=[row, row],
        out_specs=[row, pl.BlockSpec((1, 1), lambda i: (0, 0))],
        out_shape=[jax.ShapeDtypeStruct((t, d), F32), jax.ShapeDtypeStruct((1, 1), F32)],
        scratch_shapes=[pltpu.VMEM((8, d), F32)],
        compiler_params=_params(("arbitrary",)),
    )(y, target)


def adamw(w, g, m, v):
    shape = w.shape
    cols = shape[-1]
    rows = w.size // cols
    tr = next((r for r in (512, 256, 128) if rows % r == 0 and rows > r), rows)
    c1 = 1.0 - ADAM_B1 ** ADAM_STEP
    c2 = 1.0 - ADAM_B2 ** ADAM_STEP

    def body(w_ref, g_ref, m_ref, v_ref, d_ref, nm_ref, nv_ref):
        gv = g_ref[...]
        nm = ADAM_B1 * m_ref[...] + (1.0 - ADAM_B1) * gv
        nv = ADAM_B2 * v_ref[...] + (1.0 - ADAM_B2) * (gv * gv)
        nm_ref[...] = nm
        nv_ref[...] = nv
        d_ref[...] = -ADAM_LR * ((nm / c1) / (jnp.sqrt(nv / c2) + ADAM_EPS) + ADAM_WD * w_ref[...])

    spec = pl.BlockSpec((tr, cols), lambda i: (i, 0))
    out = jax.ShapeDtypeStruct((rows, cols), F32)
    res = pl.pallas_call(
        body, name="adamw", grid=(rows // tr,), in_specs=[spec] * 4, out_specs=[spec] * 3, out_shape=[out] * 3,
        compiler_params=_params(("arbitrary",)),
    )(*[a.reshape(rows, cols) for a in (w, g, m, v)])
    return tuple(r.reshape(shape) for r in res)


HALF = DEPTH // 2


def _place():
    return lax.axis_index("x"), lax.axis_index("y"), lax.axis_index("c")


def _other_chips(x, y):
    return [(1 - x, y, 2 * (1 - x) + y), (x, 1 - y, 2 * x + 1 - y), (1 - x, 1 - y, 2 * (1 - x) + 1 - y)]


def _sem_specs(*counts):
    return [pltpu.SemaphoreType.DMA((n,)) for n in counts]


def cast_shard(w, chip):
    l, r, c = w.shape

    def body(chip_ref, w_ref, o_ref):
        del chip_ref
        o_ref[...] = w_ref[...].astype(MM_DTYPE)

    grid_spec = pltpu.PrefetchScalarGridSpec(
        num_scalar_prefetch=1, grid=(l,), in_specs=[pl.BlockSpec((None, r, c), lambda i, chip_ref: (i, 0, 0))],
        out_specs=pl.BlockSpec((None, None, r, c), lambda i, chip_ref: (chip_ref[0], i, 0, 0)))
    return pl.pallas_call(
        body, name="cast_shard", grid_spec=grid_spec, out_shape=jax.ShapeDtypeStruct((N_CHIPS, l, r, c), MM_DTYPE),
        compiler_params=_params(("arbitrary",)),
    )(jnp.reshape(chip, (1,)).astype(jnp.int32), w)


def gather_weights(shards, small):
    n = len(shards)

    def body(*refs):
        small_ref = refs[n]
        outs = refs[n + 1:2 * n + 1]
        smalls_ref = refs[2 * n + 1]
        send, recv, fsend, frecv, ssend, srecv, local = refs[2 * n + 2:]
        x, y, c = _place()
        me = 2 * x + y
        chips = _other_chips(x, y)
        mine = pl.ds(c * HALF, HALF)
        theirs = pl.ds((1 - c) * HALF, HALF)

        own_small = pltpu.make_async_copy(small_ref, smalls_ref.at[me], local.at[0])
        own_small.start()

        def block(ref, chip, layers, sems, k, to):
            part = ref.at[chip, layers]
            return pltpu.make_async_remote_copy(src_ref=part, dst_ref=part, send_sem=sems[0].at[k], recv_sem=sems[1].at[k],
                                                device_id=to, device_id_type=MESH_ID)

        def small_copy(k, chip, to):
            return pltpu.make_async_remote_copy(src_ref=small_ref, dst_ref=smalls_ref.at[chip], send_sem=ssend.at[k],
                                                recv_sem=srecv.at[k], device_id=to, device_id_type=MESH_ID)

        sends = []
        for k, (px, py, _) in enumerate(chips):
            for i, ref in enumerate(outs):
                sends.append(block(ref, me, mine, (send, recv), n * k + i, (px, py, c)))
            sends.append(small_copy(k, me, (px, py, c)))
        for cp in sends:
            cp.start()
        forwards = []
        for k, (_, _, pj) in enumerate(chips):
            for i, ref in enumerate(outs):
                block(ref, pj, mine, (send, recv), n * k + i, (x, y, c)).wait_recv()
                fw = block(ref, pj, mine, (fsend, frecv), n * k + i, (x, y, 1 - c))
                fw.start()
                forwards.append(fw)
        for k, (_, _, pj) in enumerate(chips):
            small_copy(k, pj, (x, y, c)).wait_recv()
            for i, ref in enumerate(outs):
                block(ref, pj, theirs, (fsend, frecv), n * k + i, (x, y, 1 - c)).wait_recv()
        for cp in sends + forwards:
            cp.wait_send()
        own_small.wait()

    res = pl.pallas_call(
        body, name="gather_weights", in_specs=[_ANY] * (n + 1), out_specs=[_ANY] * (n + 1),
        out_shape=[jax.ShapeDtypeStruct(s.shape, s.dtype) for s in shards]
        + [jax.ShapeDtypeStruct((N_CHIPS,) + small.shape, small.dtype)],
        input_output_aliases={i: i for i in range(n)},
        scratch_shapes=_sem_specs(3 * n, 3 * n, 3 * n, 3 * n, 3, 3, 1),
        compiler_params=pltpu.CompilerParams(has_side_effects=True),
    )(*shards, small)
    return res[:n], res[n]


def swap_rider(gs):
    n = len(gs)

    def copies(rin, rout, sems):
        x, y, c = _place()
        out = []
        for i, (g, got) in enumerate(zip(rin, rout)):
            rh = g.shape[1] // 2
            theirs = pl.ds(pl.multiple_of((1 - c) * rh, 8), rh)
            out.append(pltpu.make_async_remote_copy(
                src_ref=g.at[:, theirs, :], dst_ref=got, send_sem=sems[0].at[i], recv_sem=sems[1].at[i],
                device_id=(x, y, 1 - c), device_id_type=MESH_ID))
        return out

    def start(rin, rout, sems):
        for cp in copies(rin, rout, sems):
            cp.start()

    def wait(rin, rout, sems):
        for cp in copies(rin, rout, sems):
            cp.wait()

    shapes = [jax.ShapeDtypeStruct((g.shape[0], g.shape[1] // 2, g.shape[2]), g.dtype) for g in gs]
    return Rider("swap", list(gs), shapes, {}, (n, n), start, wait)


def scatter_rider(sbs):
    n = len(sbs)

    def start(rin, rout, sems):
        x, y, c = _place()
        me = 2 * x + y
        for k, (px, py, pj) in enumerate(_other_chips(x, y)):
            for i, (sb, got) in enumerate(zip(rin, rout)):
                pltpu.make_async_remote_copy(
                    src_ref=sb.at[pj], dst_ref=got.at[me], send_sem=sems[0].at[n * k + i], recv_sem=sems[1].at[n * k + i],
                    device_id=(px, py, c), device_id_type=MESH_ID).start()

    def wait(rin, rout, sems):
        x, y, c = _place()
        for k, (_, _, pj) in enumerate(_other_chips(x, y)):
            for i, (sb, got) in enumerate(zip(rin, rout)):
                cp = pltpu.make_async_remote_copy(
                    src_ref=sb.at[pj], dst_ref=got.at[pj], send_sem=sems[0].at[n * k + i], recv_sem=sems[1].at[n * k + i],
                    device_id=(x, y, c), device_id_type=MESH_ID)
                cp.wait_recv()
                cp.wait_send()

    shapes = [jax.ShapeDtypeStruct(sb.shape, sb.dtype) for sb in sbs]
    return Rider("scatter", list(sbs), shapes, {}, (3 * n, 3 * n), start, wait)


def join_rider(fs, layer):
    n = len(fs)

    def half(f, mine, c):
        rh = f.shape[1] // 2
        return f.at[layer, pl.ds(pl.multiple_of((c if mine else 1 - c) * rh, 8), rh), :]

    def start(rin, rout, sems):
        x, y, c = _place()
        for i, f in enumerate(rout):
            pltpu.make_async_remote_copy(
                src_ref=half(f, True, c), dst_ref=half(f, True, c), send_sem=sems[0].at[i], recv_sem=sems[1].at[i],
                device_id=(x, y, 1 - c), device_id_type=MESH_ID).start()

    def wait(rin, rout, sems):
        x, y, c = _place()
        for i, f in enumerate(rout):
            pltpu.make_async_remote_copy(
                src_ref=half(f, False, c), dst_ref=half(f, False, c), send_sem=sems[0].at[i], recv_sem=sems[1].at[i],
                device_id=(x, y, 1 - c), device_id_type=MESH_ID).wait()

    shapes = [jax.ShapeDtypeStruct(f.shape, f.dtype) for f in fs]
    return Rider("join", list(fs), shapes, {i: i for i in range(n)}, (n, n), start, wait)


def _row_tile(r):
    return next(t for t in (256, 352, 128) if r % t == 0)


def add_halves(g, got):
    n, rh, cols = got.shape
    tr = _row_tile(rh)
    c = lax.axis_index("c")

    def body(c_ref, g_ref, got_ref, s_ref, sb_ref):
        del c_ref
        s = g_ref[...] + got_ref[...]
        s_ref[...] = s
        sb_ref[...] = s.astype(BF16)

    blk = (None, tr, cols)
    grid_spec = pltpu.PrefetchScalarGridSpec(
        num_scalar_prefetch=1, grid=(n, rh // tr),
        in_specs=[pl.BlockSpec(blk, lambda j, i, c_ref: (j, c_ref[0] * (rh // tr) + i, 0)),
                  pl.BlockSpec(blk, lambda j, i, c_ref: (j, i, 0))],
        out_specs=[pl.BlockSpec(blk, lambda j, i, c_ref: (j, i, 0))] * 2)
    return pl.pallas_call(
        body, name="add_halves", grid_spec=grid_spec,
        out_shape=[jax.ShapeDtypeStruct(got.shape, F32), jax.ShapeDtypeStruct(got.shape, BF16)],
        compiler_params=_params(("arbitrary",) * 2),
    )(jnp.reshape(c, (1,)).astype(jnp.int32), g, got)


def add_chips(s, got, fbuf, layer):
    n, rh, cols = s.shape
    tr = _row_tile(rh)
    x, y, c = _place()
    me = 2 * x + y

    def body(p_ref, s_ref, g1_ref, g2_ref, g3_ref, f_ref, o_ref):
        del p_ref, f_ref
        o_ref[...] = s_ref[...] + g1_ref[...].astype(F32) + g2_ref[...].astype(F32) + g3_ref[...].astype(F32)

    blk = (None, tr, cols)

    def other(k):
        return pl.BlockSpec(blk, lambda i, p_ref: ((p_ref[0] + k) % n, i, 0))

    grid_spec = pltpu.PrefetchScalarGridSpec(
        num_scalar_prefetch=1, grid=(rh // tr,),
        in_specs=[pl.BlockSpec(blk, lambda i, p_ref: (p_ref[0], i, 0)), other(1), other(2), other(3), _ANY],
        out_specs=pl.BlockSpec(blk, lambda i, p_ref: (layer, p_ref[1] * (rh // tr) + i, 0)))
    return pl.pallas_call(
        body, name="add_chips", grid_spec=grid_spec, out_shape=jax.ShapeDtypeStruct(fbuf.shape, F32),
        input_output_aliases={5: 0}, compiler_params=_params(("arbitrary",)),
    )(jnp.stack([me, c]).astype(jnp.int32), s, got, got, got, fbuf)


def gather_all(v):
    def body(v_ref, o_ref, send, recv, local):
        x, y, c = _place()
        me = 4 * x + 2 * y + c
        own = pltpu.make_async_copy(v_ref, o_ref.at[me], local.at[0])
        own.start()
        sends = []
        k = 0
        for fx in (0, 1):
            for fy in (0, 1):
                for fc in (0, 1):
                    if fx or fy or fc:
                        cp = pltpu.make_async_remote_copy(
                            src_ref=v_ref, dst_ref=o_ref.at[me], send_sem=send.at[k], recv_sem=recv.at[k],
                            device_id=(x ^ fx, y ^ fy, c ^ fc), device_id_type=MESH_ID)
                        cp.start()
                        sends.append((cp, k, 4 * (x ^ fx) + 2 * (y ^ fy) + (c ^ fc)))
                        k += 1
        for cp, k, peer in sends:
            pltpu.make_async_remote_copy(
                src_ref=v_ref, dst_ref=o_ref.at[peer], send_sem=send.at[k], recv_sem=recv.at[k],
                device_id=(x, y, c), device_id_type=MESH_ID).wait_recv()
        for cp, _, _ in sends:
            cp.wait_send()
        own.wait()

    return pl.pallas_call(
        body, name="gather_all", in_specs=[_ANY], out_specs=_ANY,
        out_shape=jax.ShapeDtypeStruct((8,) + v.shape, v.dtype), scratch_shapes=_sem_specs(7, 7, 1),
        compiler_params=pltpu.CompilerParams(has_side_effects=True),
    )(v)


def sum_devices(v8, tr):
    n, rows, lanes = v8.shape
    assert rows % tr == 0, (rows, tr)

    def body(v_ref, o_ref):
        acc = v_ref[0]
        for j in range(1, n):
            acc = acc + v_ref[j]
        o_ref[...] = acc

    return pl.pallas_call(
        body, name="sum_devices", grid=(rows // tr,),
        in_specs=[pl.BlockSpec((n, tr, lanes), lambda i: (0, i, 0))], out_specs=pl.BlockSpec((tr, lanes), lambda i: (i, 0)),
        out_shape=jax.ShapeDtypeStruct((rows, lanes), F32), compiler_params=_params(("arbitrary",)),
    )(v8)


def _pack(arrays, rows):
    flat = jnp.concatenate([a.reshape(-1) for a in arrays])
    return jnp.pad(flat, (0, rows * LANES - flat.size)).reshape(rows, LANES)


def _unpack(buf, shapes):
    flat = buf.reshape(-1)
    out, at = [], 0
    for s in shapes:
        n = math.prod(s)
        out.append(flat[at:at + n].reshape(s))
        at += n
    return out


CONV_SHARDS = [(DEPTH, K_A, D_A // N_CHIPS), (DEPTH, K_C, D_C // N_CHIPS), (DEPTH, K_F, 2 * D_FF // N_CHIPS)]
CONV_ROWS = 32
SMALL_ROWS = 576


def _join_cols(g):
    n, l, r, c = g.shape
    return jnp.transpose(g, (1, 2, 0, 3)).reshape(l, r, n * c)


BIG = ["w_in", "w_out", "w_up", "w_down"]
TILE_MM = 512
TILE_EW = 256


def _pad_rows(a, rows):
    return jnp.pad(a, ((0, rows - a.shape[0]), (0, 0)))


def _row(a):
    return a.reshape(1, -1)


def _tile_perm(tt):
    p = lax.broadcasted_iota(jnp.int32, (tt, tt), 0)
    tok = lax.broadcasted_iota(jnp.int32, (tt, tt), 1)
    return ((tt // 8) * (p % 8) + p // 8 == tok).astype(F32)


def _layer_params(wl, tt):
    n = tt // CHUNK
    tril = jnp.tril(jnp.ones((CHUNK, CHUNK), bool))
    wm = jnp.where(tril[None], wl["sgu_w"], 0.0)
    eye = jnp.eye(n, dtype=F32)
    wt = (eye[None, :, None, :, None] * wm[:, None, :, None, :]).reshape(N_HEADS_B, tt, tt)
    bias_e = jnp.repeat(wl["sgu_b"].T, HEAD, axis=1)
    return (_pad_rows(wl["conv_a_w"], 8), _row(wl["sgu_ln_g"]), _row(wl["sgu_ln_b"]), wt.astype(MM_DTYPE),
            jnp.tile(bias_e, (n, 1)), _pad_rows(wl["conv_c_w"], 32), _row(wl["conv_c_b"]), _row(wl["conv_ln_g"]),
            _row(wl["conv_ln_b"]))


def layer_fwd(x, wl, gw, layer, tm=TILE_MM, tt=TILE_EW):
    mp = _layer_params(wl, tt)
    z, h = norm_matmul(x, _row(wl["pre_mix_g"]), gw["w_in"], layer, tm)
    o, x1, cv = mixer_fwd(z, x, mp, _tile_perm(tt), _row(wl["grp_norm_g"]), gw["w_out"], layer, _row(wl["post_mix_g"]), tt)
    up0, h2 = norm_matmul(x1, _row(wl["pre_ffn_g"]), gw["w_up"], layer, tt)
    d, x2 = ffn_fwd(up0, x1, _pad_rows(wl["ffn_conv_w"], 8), gw["w_down"], layer, _row(wl["post_ffn_g"]), tt)
    return x2, dict(x=x, z=z, h=h, o=o, x1=x1, up0=up0, h2=h2, d=d, cv=cv)


def layer_bwd(dx2, wl, gw, layer, sv, pend=None, exchange=True, tm=TILE_MM, tt=TILE_EW):
    mp = _layer_params(wl, tt)
    tk = min(512, dx2.shape[0])
    g = {}
    ride = scatter_rider([s16 for _, s16 in pend["sums"]]) if pend else None
    (dd, act, dup0, dpg, dcw), arrived = ffn_bwd(dx2, sv["d"], sv["up0"], _pad_rows(wl["ffn_conv_w"], 8), gw["w_down"],
                                                 layer, _row(wl["post_ffn_g"]), tt, rider=ride)
    if pend:
        fbuf = [add_chips(s32, got, fb, pend["layer"]) for (s32, _), got, fb in zip(pend["sums"], arrived, pend["fbuf"])]
    g["post_ffn_g"] = dpg[0]
    g["ffn_conv_w"] = dcw[:K_F]
    gl = {}
    gl["w_down"] = matmul_tn_down(act, dd, tk)
    gl["w_up"] = matmul_tn_blocks(sv["h2"], dup0, D_MODEL, 2 * D_FF // N_CHIPS, tk, by_rows=False)
    (dx1, dg), _ = matmul_nt_norm_bwd(dup0, gw["w_up"], layer, sv["x1"], _row(wl["pre_ffn_g"]), dx2, tm)
    g["pre_ffn_g"] = dg[0]
    ride = join_rider(fbuf, pend["layer"]) if pend else None
    (dz, do, yn, dpg, dgg, dcaw, dlng, dlnb, dwm, dbias, dccw, dccb, dclg, dclb), joined = mixer_bwd(
        dx1, sv["o"], sv["z"], sv["cv"], mp, _tile_perm(tt), _row(wl["grp_norm_g"]), gw["w_out"], layer,
        _row(wl["post_mix_g"]), tt, rider=ride)
    g["post_mix_g"] = dpg[0]
    g["grp_norm_g"] = dgg[0]
    g["conv_a_w"] = dcaw[:K_A]
    g["sgu_ln_g"] = dlng[0]
    g["sgu_ln_b"] = dlnb[0]
    g["sgu_w"] = dwm
    g["sgu_b"] = jnp.sum(dbias.reshape(CHUNK, N_HEADS_B, HEAD), axis=2).T
    g["conv_c_w"] = dccw[:K_C]
    g["conv_c_b"] = dccb[0]
    g["conv_ln_g"] = dclg[0]
    g["conv_ln_b"] = dclb[0]
    gl["w_out"] = matmul_tn_blocks(yn, do, D_MODEL // N_CHIPS, D_MODEL, tk, by_rows=True)
    gl["w_in"] = matmul_tn_in(sv["h"], dz, tk)
    gs = [gl[n] for n in BIG]
    ride = swap_rider(gs) if exchange else None
    (dx, dg), got = matmul_nt_norm_bwd(dz, gw["w_in"], layer, sv["x"], _row(wl["pre_mix_g"]), dx1, tm, rider=ride)
    g["pre_mix_g"] = dg[0]
    if not exchange:
        return dx, g, gl
    sums = [add_halves(gi, gt) for gi, gt in zip(gs, got)]
    return dx, g, dict(sums=sums, fbuf=list(joined) if pend else grad_buffers(), layer=layer)


def grad_buffers():
    return [lax.empty(s, F32) for s in ((DEPTH, D_MODEL, D_IN // N_CHIPS), (DEPTH, D_MODEL // N_CHIPS, D_MODEL),
                                        (DEPTH, D_MODEL, 2 * D_FF // N_CHIPS), (DEPTH, D_FF // N_CHIPS, D_MODEL))]


CONV = ["conv_a_w", "conv_c_w", "ffn_conv_w"]
REPL = ["pre_mix_g", "sgu_ln_g", "sgu_ln_b", "sgu_w", "sgu_b", "conv_c_b", "conv_ln_g", "conv_ln_b", "grp_norm_g",
        "post_mix_g", "pre_ffn_g", "post_ffn_g"]
WEIGHTS = ["pre_mix_g", "w_in", "conv_a_w", "sgu_ln_g", "sgu_ln_b", "sgu_w", "sgu_b", "conv_c_w", "conv_c_b", "conv_ln_g",
           "conv_ln_b", "grp_norm_g", "w_out", "post_mix_g", "pre_ffn_g", "w_up", "ffn_conv_w", "w_down", "post_ffn_g"]


def kernel(x, pre_mix_g, w_in, conv_a_w, sgu_ln_g, sgu_ln_b, sgu_w, sgu_b, conv_c_w, conv_c_b, conv_ln_g, conv_ln_b, grp_norm_g, w_out, post_mix_g, pre_ffn_g, w_up, ffn_conv_w, w_down, post_ffn_g, loss_target, m_pre_mix_g, m_w_in, m_conv_a_w, m_sgu_ln_g, m_sgu_ln_b, m_sgu_w, m_sgu_b, m_conv_c_w, m_conv_c_b, m_conv_ln_g, m_conv_ln_b, m_grp_norm_g, m_w_out, m_post_mix_g, m_pre_ffn_g, m_w_up, m_ffn_conv_w, m_w_down, m_post_ffn_g, v_pre_mix_g, v_w_in, v_conv_a_w, v_sgu_ln_g, v_sgu_ln_b, v_sgu_w, v_sgu_b, v_conv_c_w, v_conv_c_b, v_conv_ln_g, v_conv_ln_b, v_grp_norm_g, v_w_out, v_post_mix_g, v_pre_ffn_g, v_w_up, v_ffn_conv_w, v_w_down, v_post_ffn_g):
    w = dict(pre_mix_g=pre_mix_g, w_in=w_in, conv_a_w=conv_a_w, sgu_ln_g=sgu_ln_g, sgu_ln_b=sgu_ln_b, sgu_w=sgu_w, sgu_b=sgu_b,
             conv_c_w=conv_c_w, conv_c_b=conv_c_b, conv_ln_g=conv_ln_g, conv_ln_b=conv_ln_b, grp_norm_g=grp_norm_g,
             w_out=w_out, post_mix_g=post_mix_g, pre_ffn_g=pre_ffn_g, w_up=w_up, ffn_conv_w=ffn_conv_w, w_down=w_down,
             post_ffn_g=post_ffn_g)
    m = dict(pre_mix_g=m_pre_mix_g, w_in=m_w_in, conv_a_w=m_conv_a_w, sgu_ln_g=m_sgu_ln_g, sgu_ln_b=m_sgu_ln_b,
             sgu_w=m_sgu_w, sgu_b=m_sgu_b, conv_c_w=m_conv_c_w, conv_c_b=m_conv_c_b, conv_ln_g=m_conv_ln_g,
             conv_ln_b=m_conv_ln_b, grp_norm_g=m_grp_norm_g, w_out=m_w_out, post_mix_g=m_post_mix_g,
             pre_ffn_g=m_pre_ffn_g, w_up=m_w_up, ffn_conv_w=m_ffn_conv_w, w_down=m_w_down, post_ffn_g=m_post_ffn_g)
    v = dict(pre_mix_g=v_pre_mix_g, w_in=v_w_in, conv_a_w=v_conv_a_w, sgu_ln_g=v_sgu_ln_g, sgu_ln_b=v_sgu_ln_b,
             sgu_w=v_sgu_w, sgu_b=v_sgu_b, conv_c_w=v_conv_c_w, conv_c_b=v_conv_c_b, conv_ln_g=v_conv_ln_g,
             conv_ln_b=v_conv_ln_b, grp_norm_g=v_grp_norm_g, w_out=v_w_out, post_mix_g=v_post_mix_g,
             pre_ffn_g=v_pre_ffn_g, w_up=v_w_up, ffn_conv_w=v_ffn_conv_w, w_down=v_w_down, post_ffn_g=v_post_ffn_g)
    chip = 2 * lax.axis_index("x") + lax.axis_index("y")

    shards, convs = gather_weights([cast_shard(w[n], chip) for n in BIG], _pack([w[n] for n in CONV], CONV_ROWS))
    gw = dict(zip(BIG, shards))
    cparts = [_unpack(convs[j], CONV_SHARDS) for j in range(N_CHIPS)]
    full = dict(w)
    for i, n in enumerate(CONV):
        full[n] = _join_cols(jnp.stack([p[i] for p in cparts]))

    xc = to_tiles(x[0], TILE_EW)
    saved = []
    for layer in range(DEPTH):
        xc, sv = layer_fwd(xc, {n: full[n][layer] for n in REPL + CONV}, gw, layer)
        saved.append(sv)
    dxc, loss_part = loss_head(xc, to_tiles(loss_target[0], TILE_EW), TILE_MM)
    loss = lax.psum(loss_part[0, 0], ("x", "y", "c"))
    small = [None] * DEPTH
    pend = None
    for layer in reversed(range(DEPTH)):
        dxc, small[layer], pend = layer_bwd(dxc, {n: full[n][layer] for n in REPL + CONV}, gw, layer, saved[layer], pend)
    grads = {n: jnp.stack([small[layer][n] for layer in range(DEPTH)]) for n in REPL + CONV}

    arrived = run_rider(scatter_rider([s16 for _, s16 in pend["sums"]]))
    fbuf = [add_chips(s32, got, fb, 0) for (s32, _), got, fb in zip(pend["sums"], arrived, pend["fbuf"])]
    out_g = dict(zip(BIG, run_rider(join_rider(fbuf, 0))))

    tot = sum_devices(gather_all(_pack([grads[n] for n in REPL + CONV], SMALL_ROWS)), 192)
    shapes = [grads[n].shape for n in REPL + CONV]
    for n, gfull in zip(REPL + CONV, _unpack(tot, shapes)):
        if n in CONV:
            width = gfull.shape[-1] // N_CHIPS
            gfull = lax.dynamic_slice_in_dim(gfull, chip * width, width, axis=2)
        out_g[n] = gfull

    deltas, new_m, new_v = {}, {}, {}
    for n in WEIGHTS:
        deltas[n], new_m[n], new_v[n] = adamw(w[n], out_g[n], m[n], v[n])
    return (loss, from_tiles(dxc, TILE_EW)[None], *[out_g[n] for n in WEIGHTS], *[deltas[n] for n in WEIGHTS], *[new_m[n] for n in WEIGHTS],
            *[new_v[n] for n in WEIGHTS])
```

```python
import functools
import math
from typing import Callable, NamedTuple

import jax
import jax.numpy as jnp
from jax import lax
from jax.experimental import pallas as pl
from jax.experimental.pallas import tpu as pltpu

F32 = jnp.float32
BF16 = jnp.bfloat16
MM_DTYPE = BF16

D_MODEL = 1024
SEQ = 4096
DEPTH = 4
D_A = 256
D_B = 384
D_C = 384
D_IN = 3 * D_A + 2 * D_B + 2 * D_C
D_FF = 2816
K_A = 3
K_C = 31
K_F = 3
CHUNK = 128
HEAD = 64
N_HEADS_B = D_B // HEAD
EPS = 1e-6
N_CHIPS = 4

ADAM_LR = 0.001
ADAM_B1 = 0.9
ADAM_B2 = 0.999
ADAM_EPS = 1e-08
ADAM_WD = 0.01
ADAM_STEP = 10

LANES = 1024
VMEM_LIMIT = 56 * 1024 * 1024

MESH_ID = pl.DeviceIdType.MESH
_ANY = pl.BlockSpec(memory_space=pl.ANY)


def _params(sem=None):
    return pltpu.CompilerParams(dimension_semantics=sem, vmem_limit_bytes=VMEM_LIMIT)


def _const_spec(shape):
    nd = len(shape)
    return pl.BlockSpec(shape, lambda *_: (0,) * nd, pipeline_mode=pl.Buffered(1))


def _rowsum8(a):
    r, c = a.shape
    return jnp.sum(a.reshape(r // 8, 8, c), axis=0)


def _rstd(x):
    return lax.rsqrt(jnp.mean(x * x, axis=-1, keepdims=True) + EPS)


def _rms_bwd(x, r, g, dy):
    gdy = g * dy
    return r * gdy - x * (r * r * r) * jnp.mean(gdy * x, axis=-1, keepdims=True)


def _ln_fwd(x):
    mu = jnp.mean(x, axis=-1, keepdims=True)
    xc = x - mu
    r = lax.rsqrt(jnp.mean(xc * xc, axis=-1, keepdims=True) + EPS)
    return xc * r, r


def _ln_bwd(xh, r, dxh):
    return r * (dxh - jnp.mean(dxh, axis=-1, keepdims=True) - xh * jnp.mean(dxh * xh, axis=-1, keepdims=True))


def _gelu(x):
    return 0.5 * x * (1.0 + lax.erf(x * (1.0 / math.sqrt(2.0))))


def _gelu_grad(x):
    cdf = 0.5 * (1.0 + lax.erf(x * (1.0 / math.sqrt(2.0))))
    pdf = jnp.exp(-0.5 * x * x) * (1.0 / math.sqrt(2.0 * math.pi))
    return cdf + x * pdf


def _dot(a, b):
    return jnp.dot(a, b, preferred_element_type=F32)


def _dot_nt(a, b):
    return lax.dot_general(a, b, (((1,), (1,)), ((), ())), preferred_element_type=F32)


def _dot_tn(a, b):
    return lax.dot_general(a, b, (((0,), (0,)), ((), ())), preferred_element_type=F32)


def _col_chunk(n):
    for c in (1408, 1024, 768, 512, 256, 128):
        if n % c == 0:
            return c
    raise ValueError(n)


class Rider(NamedTuple):
    name: str
    inputs: list
    out_shapes: list
    aliases: dict
    sems: tuple
    start: Callable
    wait: Callable


def _pallas(body, rider, *, name, steps, in_specs, out_specs, out_shape, scratch_shapes, args):
    if rider is None:
        res = pl.pallas_call(body, name=name, grid=(steps,), in_specs=in_specs, out_specs=out_specs, out_shape=out_shape,
                             scratch_shapes=scratch_shapes, compiler_params=_params(("arbitrary",)))(*args)
        return res, []
    n_in, n_out, n_scr = len(in_specs), len(out_specs), len(scratch_shapes)
    r_in, r_out = len(rider.inputs), len(rider.out_shapes)

    def wrapped(*refs):
        ins, rin = refs[:n_in], refs[n_in:n_in + r_in]
        at = n_in + r_in
        outs, rout = refs[at:at + n_out], refs[at + n_out:at + n_out + r_out]
        at += n_out + r_out
        scr, rsem = refs[at:at + n_scr], refs[at + n_scr:]

        @pl.when(pl.program_id(0) == 0)
        def _():
            rider.start(rin, rout, rsem)

        body(*ins, *outs, *scr)

        @pl.when(pl.program_id(0) == steps - 1)
        def _():
            rider.wait(rin, rout, rsem)

    res = pl.pallas_call(
        wrapped, name=name + "_" + rider.name, grid=(steps,), in_specs=list(in_specs) + [_ANY] * r_in,
        out_specs=list(out_specs) + [_ANY] * r_out, out_shape=list(out_shape) + list(rider.out_shapes),
        scratch_shapes=list(scratch_shapes) + [pltpu.SemaphoreType.DMA((n,)) for n in rider.sems],
        input_output_aliases={n_in + i: n_out + o for i, o in rider.aliases.items()},
        compiler_params=pltpu.CompilerParams(dimension_semantics=("arbitrary",), vmem_limit_bytes=VMEM_LIMIT,
                                             has_side_effects=True),
    )(*args, *rider.inputs)
    return res[:n_out], res[n_out:]


def run_rider(rider):
    def body(*refs):
        r_in, r_out = len(rider.inputs), len(rider.out_shapes)
        rin, rout, rsem = refs[:r_in], refs[r_in:r_in + r_out], refs[r_in + r_out:]
        rider.start(rin, rout, rsem)
        rider.wait(rin, rout, rsem)

    return pl.pallas_call(
        body, name=rider.name, in_specs=[_ANY] * len(rider.inputs), out_specs=[_ANY] * len(rider.out_shapes),
        out_shape=list(rider.out_shapes), scratch_shapes=[pltpu.SemaphoreType.DMA((n,)) for n in rider.sems],
        input_output_aliases=dict(rider.aliases), compiler_params=pltpu.CompilerParams(has_side_effects=True),
    )(*rider.inputs)


def _weight_spec(wg):
    return _const_spec(wg.shape)


def _join_col_blocks(w_ref, w_scr):
    c = w_ref.shape[2]
    for j in range(N_CHIPS):
        w_scr[:, c * j:c * (j + 1)] = w_ref[j]


def norm_matmul(x, g, wg, tm, rider=None):
    t, d = x.shape
    assert t % tm == 0, (t, tm)
    cw = wg.shape[2]
    n = N_CHIPS * cw
    aligned = cw % 128 == 0
    cn = cw if aligned else _col_chunk(n)

    def body(x_ref, g_ref, w_ref, o_ref, h_ref, *scr):
        if not aligned:
            @pl.when(pl.program_id(0) == 0)
            def _():
                _join_col_blocks(w_ref, scr[0])

        xv = x_ref[...]
        h = (xv * _rstd(xv) * g_ref[...]).astype(MM_DTYPE)
        h_ref[...] = h
        for j, c0 in enumerate(range(0, n, cn)):
            wv = w_ref[j] if aligned else scr[0][:, c0:c0 + cn]
            o_ref[:, c0:c0 + cn] = _dot(h, wv)

    return _pallas(
        body, rider, name="norm_matmul", steps=t // tm,
        in_specs=[pl.BlockSpec((tm, d), lambda i: (i, 0)), _const_spec((1, d)), _weight_spec(wg)],
        out_specs=[pl.BlockSpec((tm, n), lambda i: (i, 0)), pl.BlockSpec((tm, d), lambda i: (i, 0))],
        out_shape=[jax.ShapeDtypeStruct((t, n), F32), jax.ShapeDtypeStruct((t, d), MM_DTYPE)],
        scratch_shapes=[] if aligned else [pltpu.VMEM((d, n), MM_DTYPE)],
        args=(x, g, wg))


def matmul_nt_norm_bwd(gy, wg, x, g, dres, tm, rider=None):
    t, n = gy.shape
    assert t % tm == 0, (t, tm)
    d, cw = wg.shape[1], wg.shape[2]
    aligned = cw % 128 == 0
    cn = cw if aligned else _col_chunk(n)
    steps = t // tm

    def body(gy_ref, w_ref, x_ref, g_ref, dres_ref, dx_ref, dg_ref, acc_ref, *scr):
        i = pl.program_id(0)

        @pl.when(i == 0)
        def _():
            acc_ref[...] = jnp.zeros_like(acc_ref)
            if not aligned:
                _join_col_blocks(w_ref, scr[0])

        dh = jnp.zeros((tm, d), F32)
        for j, c0 in enumerate(range(0, n, cn)):
            wv = w_ref[j] if aligned else scr[0][:, c0:c0 + cn]
            dh = dh + _dot_nt(gy_ref[:, c0:c0 + cn], wv)
        xv = x_ref[...]
        r = _rstd(xv)
        gv = g_ref[...]
        dx_ref[...] = dres_ref[...] + _rms_bwd(xv, r, gv, dh)
        acc_ref[...] += _rowsum8(dh * xv * r)

        @pl.when(i == steps - 1)
        def _():
            dg_ref[...] = jnp.sum(acc_ref[...], axis=0, keepdims=True)

    return _pallas(
        body, rider, name="matmul_nt_norm_bwd", steps=steps,
        in_specs=[pl.BlockSpec((tm, n), lambda i: (i, 0)), _weight_spec(wg), pl.BlockSpec((tm, d), lambda i: (i, 0)),
                  _const_spec((1, d)), pl.BlockSpec((tm, d), lambda i: (i, 0))],
        out_specs=[pl.BlockSpec((tm, d), lambda i: (i, 0)), pl.BlockSpec((1, d), lambda i: (0, 0))],
        out_shape=[jax.ShapeDtypeStruct((t, d), F32), jax.ShapeDtypeStruct((1, d), F32)],
        scratch_shapes=[pltpu.VMEM((8, d), F32)] + ([] if aligned else [pltpu.VMEM((d, n), MM_DTYPE)]),
        args=(gy, wg, x, g, dres))


def matmul_tn_blocks(a, b, r, c, tk, by_rows):
    t = a.shape[0]
    assert t % tk == 0 and r % 8 == 0 and c % 128 == 0, (a.shape, b.shape, r, c, tk)

    def body(a_ref, b_ref, o_ref):
        @pl.when(pl.program_id(1) == 0)
        def _():
            o_ref[...] = jnp.zeros_like(o_ref)

        o_ref[...] += _dot_tn(a_ref[...], b_ref[...])

    a_spec = pl.BlockSpec((tk, r), (lambda j, k: (k, j)) if by_rows else (lambda j, k: (k, 0)))
    b_spec = pl.BlockSpec((tk, c), (lambda j, k: (k, 0)) if by_rows else (lambda j, k: (k, j)))
    return pl.pallas_call(
        body, name="matmul_tn_blocks", grid=(N_CHIPS, t // tk), in_specs=[a_spec, b_spec],
        out_specs=pl.BlockSpec((None, r, c), lambda j, k: (j, 0, 0)),
        out_shape=jax.ShapeDtypeStruct((N_CHIPS, r, c), F32),
        compiler_params=_params(("arbitrary", "arbitrary")),
    )(a, b)


def matmul_tn_down(act, dd, tk):
    t, m = act.shape
    c = dd.shape[1]
    r = m // N_CHIPS
    assert t % tk == 0, (t, tk)
    steps = t // tk

    def body(a_ref, b_ref, o_ref, acc):
        k = pl.program_id(1)

        @pl.when(k == 0)
        def _():
            acc[...] = jnp.zeros_like(acc)

        acc[...] += _dot_tn(a_ref[...], b_ref[...])

        @pl.when(k == steps - 1)
        def _():
            o_ref[0] = acc[0:r, :]
            o_ref[1] = acc[r:2 * r, :]

    return pl.pallas_call(
        body, name="matmul_tn_down", grid=(2, steps),
        in_specs=[pl.BlockSpec((tk, 2 * r), lambda p, k: (k, p)), pl.BlockSpec((tk, c), lambda p, k: (k, 0))],
        out_specs=pl.BlockSpec((2, r, c), lambda p, k: (p, 0, 0)),
        out_shape=jax.ShapeDtypeStruct((N_CHIPS, r, c), F32),
        scratch_shapes=[pltpu.VMEM((2 * r, c), F32)],
        compiler_params=_params(("arbitrary", "arbitrary")),
    )(act, dd)


def matmul_tn_in(h, dz, tk):
    t, d = h.shape
    n = dz.shape[1]
    c = n // N_CHIPS
    assert t % tk == 0, (t, tk)
    steps = t // tk

    def body(a_ref, b_ref, o_ref, acc):
        k = pl.program_id(0)

        @pl.when(k == 0)
        def _():
            acc[...] = jnp.zeros_like(acc)

        acc[...] += _dot_tn(a_ref[...], b_ref[...])

        @pl.when(k == steps - 1)
        def _():
            for j in range(N_CHIPS):
                o_ref[j] = acc[:, c * j:c * (j + 1)]

    return pl.pallas_call(
        body, name="matmul_tn_in", grid=(steps,),
        in_specs=[pl.BlockSpec((tk, d), lambda k: (k, 0)), pl.BlockSpec((tk, n), lambda k: (k, 0))],
        out_specs=pl.BlockSpec((N_CHIPS, d, c), lambda k: (0, 0, 0)),
        out_shape=jax.ShapeDtypeStruct((N_CHIPS, d, c), F32),
        scratch_shapes=[pltpu.VMEM((d, n), F32)],
        compiler_params=_params(("arbitrary",)),
    )(h, dz)


def to_tiles(a, tt):
    t = a.shape[0]
    return a.reshape((t // tt, 8, tt // 8) + a.shape[1:]).swapaxes(1, 2).reshape(a.shape)


def from_tiles(a, tt):
    t = a.shape[0]
    return a.reshape((t // tt, tt // 8, 8) + a.shape[1:]).swapaxes(1, 2).reshape(a.shape)


def _roll_sublanes(a, shift):
    n = a.shape[0] // 8
    return pltpu.roll(a.reshape(n, 8, a.shape[1]), shift, 1).reshape(a.shape)


def _halo_before(cur_last, prev_last):
    sub = lax.broadcasted_iota(jnp.int32, cur_last.shape, 0) % 8
    return jnp.where(sub == 0, _roll_sublanes(prev_last, 1), _roll_sublanes(cur_last, 1))


def _halo_after(cur_first, next_first):
    sub = lax.broadcasted_iota(jnp.int32, cur_first.shape, 0) % 8
    return jnp.where(sub == 7, _roll_sublanes(next_first, 7), _roll_sublanes(cur_first, 7))


def _conv_causal(ext, cur, prev_last, w, taps, tt, cols=None):
    hr = 8 * (taps - 1)
    cs = slice(None) if cols is None else cols
    ext[hr:hr + tt, cs] = cur
    ext[0:hr, cs] = _halo_before(cur[tt - hr:, :], prev_last)
    acc = w[0:1, :] * ext[0:tt, cs]
    for k in range(1, taps):
        acc = acc + w[k:k + 1, :] * ext[8 * k:8 * k + tt, cs]
    return acc


def _conv_anticausal(ext, cur, next_first, w, taps, tt, x=None, acc_w=None, cols=None):
    hr = 8 * (taps - 1)
    cs = slice(None) if cols is None else cols
    ext[0:tt, cs] = cur
    ext[tt:tt + hr, cs] = _halo_after(cur[0:hr, :], next_first)
    acc = None
    for k in range(taps):
        off = 8 * (taps - 1 - k)
        ld = ext[off:off + tt, cs]
        term = w[k:k + 1, :] * ld
        acc = term if acc is None else acc + term
        if x is not None:
            acc_w[k, :, cs] += _rowsum8(ld * x)
    return acc


def _dot_exact(a, b, dims):
    return lax.dot_general(a, b, (dims, ((), ())), precision=lax.Precision.HIGHEST, preferred_element_type=F32)


def _to_tile_order(perm, wt_ref, w_scr, transpose):
    pb = perm.astype(MM_DTYPE)
    for h in range(N_HEADS_B):
        half = (_dot_nt(pb, wt_ref[h]) if transpose else _dot(pb, wt_ref[h])).astype(MM_DTYPE)
        w_scr[h] = _dot_nt(half, pb).astype(MM_DTYPE)


def _project_rows(y, w_ref):
    r = w_ref.shape[1]
    acc = _dot(y[:, 0:r], w_ref[0])
    for j in range(1, N_CHIPS):
        acc = acc + _dot(y[:, r * j:r * (j + 1)], w_ref[j])
    return acc


def _head_select(parts):
    head = lax.broadcasted_iota(jnp.int32, parts[0].shape, 1) // HEAD
    acc = parts[0]
    for h in range(1, N_HEADS_B):
        acc = jnp.where(head == h, parts[h], acc)
    return acc


def _mixer_forward(z, prm, q, yc):
    _, lng, lnb, wm, bias_p, _, _, clg, clb = prm
    bg = z[:, 0:D_A]
    ya = bg * q
    o_b = 3 * D_A
    zu = z[:, o_b:o_b + D_B]
    zv = z[:, o_b + D_B:o_b + 2 * D_B]
    u = _gelu(zu)
    vh, rv = _ln_fwd(_gelu(zv))
    vnb = (vh * lng + lnb).astype(MM_DTYPE)
    s = _head_select([_dot(wm[h], vnb) for h in range(N_HEADS_B)]) + bias_p
    yb = u * s
    yh, rc = _ln_fwd(yc)
    l = yh * clg + clb
    sl = jax.nn.sigmoid(l)
    return dict(bg=bg, q=q, ya=ya, zu=zu, zv=zv, u=u, vh=vh, rv=rv, vnb=vnb, s=s, yb=yb, yh=yh, rc=rc, l=l, sl=sl,
                yo=l * sl)


def _conv_inputs(z):
    o_c = 3 * D_A + 2 * D_B
    a = z[:, o_c:o_c + D_C]
    sg = jax.nn.sigmoid(z[:, o_c + D_C:o_c + 2 * D_C])
    return z[:, D_A:2 * D_A] * z[:, 2 * D_A:3 * D_A], a * sg, a, sg


def _group_norm(f, gg):
    ya, yb, yo = f["ya"], f["yb"], f["yo"]
    ra, rb, ro = _rstd(ya), _rstd(yb), _rstd(yo)
    yn = jnp.concatenate([ya * ra * gg[:, 0:D_A], yb * rb * gg[:, D_A:D_A + D_B], yo * ro * gg[:, D_A + D_B:]], axis=1)
    return yn, (ra, rb, ro)


def _mixer_prm(refs, wp_scr, bias_scr):
    caw_ref, lng_ref, lnb_ref, _, _, ccw_ref, ccb_ref, clg_ref, clb_ref = refs
    wm = [wp_scr[h] for h in range(N_HEADS_B)]
    return (caw_ref[...], lng_ref[...], lnb_ref[...], wm, bias_scr[...], ccw_ref[...], ccb_ref[...], clg_ref[...],
            clb_ref[...])


def _mixer_param_specs(tt):
    return [_const_spec((8, D_A)), _const_spec((1, D_B)), _const_spec((1, D_B)), _const_spec((N_HEADS_B, tt, tt)),
            _const_spec((tt, D_B)), _const_spec((32, D_C)), _const_spec((1, D_C)), _const_spec((1, D_C)),
            _const_spec((1, D_C))]


HR_A = 8 * (K_A - 1)
HR_C = 8 * (K_C - 1)
HR_F = 8 * (K_F - 1)


def mixer_fwd(z, x, mp, perm, grp_g, wog, post_g, tt, rider=None):
    t = z.shape[0]
    assert t % tt == 0 and tt % CHUNK == 0 and tt >= HR_C, (t, tt)

    def body(z_ref, x_ref, *rest):
        prm_refs = rest[:9]
        (perm_ref, gg_ref, wo_ref, pg_ref, o_ref, x1_ref, cv_ref, pa_ext, yg_ext, pa_last, yg_last, wp_scr, bias_scr) = rest[9:]
        i = pl.program_id(0)

        @pl.when(i == 0)
        def _():
            pa_last[...] = jnp.zeros_like(pa_last)
            yg_last[...] = jnp.zeros_like(yg_last)
            _to_tile_order(perm_ref[...], prm_refs[3], wp_scr, False)
            bias_scr[...] = _dot_exact(perm_ref[...], prm_refs[4][...], ((1,), (0,)))

        zv = z_ref[...]
        prm = _mixer_prm(prm_refs, wp_scr, bias_scr)
        pa, yg, _, _ = _conv_inputs(zv)
        q = _conv_causal(pa_ext, pa, pa_last[...], prm[0], K_A, tt)
        yc = _conv_causal(yg_ext, yg, yg_last[...], prm[5], K_C, tt) + prm[6]
        pa_last[...] = pa[tt - HR_A:, :]
        yg_last[...] = yg[tt - HR_C:, :]
        cv_ref[:, 0:D_A] = q
        cv_ref[:, D_A:] = yc
        f = _mixer_forward(zv, prm, q, yc)
        yn, _ = _group_norm(f, gg_ref[...])
        o = _project_rows(yn.astype(MM_DTYPE), wo_ref)
        o_ref[...] = o
        x1_ref[...] = x_ref[...] + o * _rstd(o) * pg_ref[...]

    row = lambda c: pl.BlockSpec((tt, c), lambda i: (i, 0))
    return _pallas(
        body, rider, name="mixer_fwd", steps=t // tt,
        in_specs=[row(D_IN), row(D_MODEL)] + _mixer_param_specs(tt)
        + [_const_spec((tt, tt)), _const_spec((1, D_MODEL)), _weight_spec(wog), _const_spec((1, D_MODEL))],
        out_specs=[row(D_MODEL), row(D_MODEL), row(D_A + D_C)],
        out_shape=[jax.ShapeDtypeStruct((t, D_MODEL), F32), jax.ShapeDtypeStruct((t, D_MODEL), F32),
                   jax.ShapeDtypeStruct((t, D_A + D_C), F32)],
        scratch_shapes=[pltpu.VMEM((HR_A + tt, D_A), F32), pltpu.VMEM((HR_C + tt, D_C), F32),
                        pltpu.VMEM((HR_A, D_A), F32), pltpu.VMEM((HR_C, D_C), F32),
                        pltpu.VMEM((N_HEADS_B, tt, tt), MM_DTYPE), pltpu.VMEM((tt, D_B), F32)],
        args=(z, x, *mp, perm, grp_g, wog, post_g))


def mixer_bwd(dx1, o, z, cv, mp, perm, grp_g, wog, post_g, tt, rider=None):
    t = z.shape[0]
    assert t % tt == 0 and tt % CHUNK == 0 and tt >= HR_C, (t, tt)
    steps = t // tt

    def body(dx1_ref, o_ref, z_ref, cv_ref, *rest):
        prm_refs = rest[:9]
        (perm_ref, gg_ref, wo_ref, pg_ref,
         dz_ref, do_ref, yn_ref, dpg_ref, dgg_ref, dcaw_ref, dlng_ref, dlnb_ref, dwm_ref, dbias_ref, dccw_ref, dccb_ref,
         dclg_ref, dclb_ref,
         dq_ext, dyc_ext, dq_first, dyc_first, a_pg, a_gg, a_caw, a_lng, a_lnb, a_ccw, a_ccb, a_clg, a_clb,
         wp_scr, wpt_scr, bias_scr, a_wm, a_bias) = rest[9:]
        i = pl.program_id(0)
        small = (a_pg, a_gg, a_caw, a_lng, a_lnb, a_ccw, a_ccb, a_clg, a_clb)

        @pl.when(i == 0)
        def _():
            for ref in small + (a_wm, a_bias, dq_first, dyc_first):
                ref[...] = jnp.zeros_like(ref)
            _to_tile_order(perm_ref[...], prm_refs[3], wp_scr, False)
            _to_tile_order(perm_ref[...], prm_refs[3], wpt_scr, True)
            bias_scr[...] = _dot_exact(perm_ref[...], prm_refs[4][...], ((1,), (0,)))

        prm = _mixer_prm(prm_refs, wp_scr, bias_scr)
        caw, lng, lnb, wm, bias_p, ccw, ccb, clg, clb = prm

        zv = z_ref[...]
        pa, yg, a, sg = _conv_inputs(zv)
        f = _mixer_forward(zv, prm, cv_ref[:, 0:D_A], cv_ref[:, D_A:])
        gg = gg_ref[...]
        yn, (ra, rb, ro) = _group_norm(f, gg)
        yn_ref[...] = yn.astype(MM_DTYPE)

        ov = o_ref[...]
        dx1v = dx1_ref[...]
        r_o = _rstd(ov)
        pg = pg_ref[...]
        a_pg[...] += _rowsum8(dx1v * ov * r_o)
        do = _rms_bwd(ov, r_o, pg, dx1v).astype(MM_DTYPE)
        do_ref[...] = do
        dyn = jnp.concatenate([_dot_nt(do, wo_ref[j]) for j in range(N_CHIPS)], axis=1)

        dyn_a, dyn_b, dyn_c = dyn[:, 0:D_A], dyn[:, D_A:D_A + D_B], dyn[:, D_A + D_B:]
        ga, gb, gc = gg[:, 0:D_A], gg[:, D_A:D_A + D_B], gg[:, D_A + D_B:]
        a_gg[...] += _rowsum8(jnp.concatenate([dyn_a * f["ya"] * ra, dyn_b * f["yb"] * rb, dyn_c * f["yo"] * ro], axis=1))
        dya = _rms_bwd(f["ya"], ra, ga, dyn_a)
        dyb = _rms_bwd(f["yb"], rb, gb, dyn_b)
        dyo = _rms_bwd(f["yo"], ro, gc, dyn_c)

        dbg = dya * f["q"]
        dq = dya * f["bg"]
        dp = _conv_anticausal(dq_ext, dq, dq_first[...], caw, K_A, tt, x=pa, acc_w=a_caw)
        dq_first[...] = dq[0:HR_A, :]
        dcg = dp * zv[:, 2 * D_A:3 * D_A]
        dxa = dp * zv[:, D_A:2 * D_A]

        du = dyb * f["s"]
        ds = dyb * f["u"]
        dsb = ds.astype(MM_DTYPE)
        head = lax.broadcasted_iota(jnp.int32, (tt, D_B), 1) // HEAD
        a_bias[...] += ds
        parts = []
        for h in range(N_HEADS_B):
            a_wm[h] += _dot_nt(jnp.where(head == h, dsb, jnp.zeros_like(dsb)), f["vnb"])
            parts.append(_dot(wpt_scr[h], dsb))
        dvn = _head_select(parts)
        a_lng[...] += _rowsum8(dvn * f["vh"])
        a_lnb[...] += _rowsum8(dvn)
        dv = _ln_bwd(f["vh"], f["rv"], dvn * lng)
        dzu = du * _gelu_grad(f["zu"])
        dzv = dv * _gelu_grad(f["zv"])

        l, sl = f["l"], f["sl"]
        dl = dyo * (sl * (1.0 + l * (1.0 - sl)))
        a_clg[...] += _rowsum8(dl * f["yh"])
        a_clb[...] += _rowsum8(dl)
        dyc = _ln_bwd(f["yh"], f["rc"], dl * clg)
        a_ccb[...] += _rowsum8(dyc)
        dy = _conv_anticausal(dyc_ext, dyc, dyc_first[...], ccw, K_C, tt, x=yg, acc_w=a_ccw)
        dyc_first[...] = dyc[0:HR_C, :]
        da = dy * sg
        dg = dy * a * sg * (1.0 - sg)

        dz_ref[...] = jnp.concatenate([dbg, dcg, dxa, dzu, dzv, da, dg], axis=1).astype(MM_DTYPE)

        @pl.when(i == steps - 1)
        def _():
            red = lambda ref: jnp.sum(ref[...], axis=0, keepdims=True)
            dpg_ref[...] = red(a_pg)
            dgg_ref[...] = red(a_gg)
            dlng_ref[...] = red(a_lng)
            dlnb_ref[...] = red(a_lnb)
            dccb_ref[...] = red(a_ccb)
            dclg_ref[...] = red(a_clg)
            dclb_ref[...] = red(a_clb)
            dcaw_ref[...] = jnp.sum(a_caw[...], axis=1)
            dccw_ref[...] = jnp.sum(a_ccw[...], axis=1)
            pm = perm_ref[...]
            tril = lax.broadcasted_iota(jnp.int32, (CHUNK, CHUNK), 0) >= lax.broadcasted_iota(jnp.int32, (CHUNK, CHUNK), 1)
            for h in range(N_HEADS_B):
                dwt = _dot_exact(pm, _dot_exact(a_wm[h], pm, ((1,), (0,))), ((0,), (0,)))
                dw = dwt[0:CHUNK, 0:CHUNK]
                for c in range(1, tt // CHUNK):
                    dw = dw + dwt[c * CHUNK:(c + 1) * CHUNK, c * CHUNK:(c + 1) * CHUNK]
                dwm_ref[h] = jnp.where(tril, dw, 0.0)
            dbt = _dot_exact(pm, a_bias[...], ((0,), (0,)))
            db = dbt[0:CHUNK, :]
            for c in range(1, tt // CHUNK):
                db = db + dbt[c * CHUNK:(c + 1) * CHUNK, :]
            dbias_ref[...] = db

    rev = lambda c: pl.BlockSpec((tt, c), lambda i: (steps - 1 - i, 0))
    full = lambda shape: pl.BlockSpec(shape, lambda i: (0,) * len(shape))
    sds = jax.ShapeDtypeStruct
    return _pallas(
        body, rider, name="mixer_bwd", steps=steps,
        in_specs=[rev(D_MODEL), rev(D_MODEL), rev(D_IN), rev(D_A + D_C)] + _mixer_param_specs(tt)
        + [_const_spec((tt, tt)), _const_spec((1, D_MODEL)), _weight_spec(wog), _const_spec((1, D_MODEL))],
        out_specs=[rev(D_IN), rev(D_MODEL), rev(D_MODEL), full((1, D_MODEL)), full((1, D_MODEL)), full((8, D_A)),
                   full((1, D_B)), full((1, D_B)), full((N_HEADS_B, CHUNK, CHUNK)), full((CHUNK, D_B)), full((32, D_C)),
                   full((1, D_C)), full((1, D_C)), full((1, D_C))],
        out_shape=[sds((t, D_IN), MM_DTYPE), sds((t, D_MODEL), MM_DTYPE), sds((t, D_MODEL), MM_DTYPE),
                   sds((1, D_MODEL), F32), sds((1, D_MODEL), F32), sds((8, D_A), F32), sds((1, D_B), F32), sds((1, D_B), F32),
                   sds((N_HEADS_B, CHUNK, CHUNK), F32), sds((CHUNK, D_B), F32), sds((32, D_C), F32), sds((1, D_C), F32),
                   sds((1, D_C), F32), sds((1, D_C), F32)],
        scratch_shapes=[pltpu.VMEM((tt + HR_A, D_A), F32), pltpu.VMEM((tt + HR_C, D_C), F32),
                        pltpu.VMEM((HR_A, D_A), F32), pltpu.VMEM((HR_C, D_C), F32),
                        pltpu.VMEM((8, D_MODEL), F32), pltpu.VMEM((8, D_MODEL), F32), pltpu.VMEM((8, 8, D_A), F32),
                        pltpu.VMEM((8, D_B), F32), pltpu.VMEM((8, D_B), F32), pltpu.VMEM((32, 8, D_C), F32),
                        pltpu.VMEM((8, D_C), F32), pltpu.VMEM((8, D_C), F32), pltpu.VMEM((8, D_C), F32),
                        pltpu.VMEM((N_HEADS_B, tt, tt), MM_DTYPE), pltpu.VMEM((N_HEADS_B, tt, tt), MM_DTYPE),
                        pltpu.VMEM((tt, D_B), F32), pltpu.VMEM((N_HEADS_B, tt, tt), F32), pltpu.VMEM((tt, D_B), F32)],
        args=(dx1, o, z, cv, *mp, perm, grp_g, wog, post_g))


def _fetch_row_blocks(wg_ref, w_scr, sems):
    r = wg_ref.shape[1]
    copies = [pltpu.make_async_copy(wg_ref.at[j], w_scr.at[pl.ds(r * j, r), :], sems.at[j]) for j in range(N_CHIPS)]
    for cp in copies:
        cp.start()
    for cp in copies:
        cp.wait()


def _ffn_conv(ext, cw, c0, cn, tt):
    acc = cw[0:1, c0:c0 + cn] * ext[0:tt, c0:c0 + cn]
    for k in range(1, K_F):
        acc = acc + cw[k:k + 1, c0:c0 + cn] * ext[8 * k:8 * k + tt, c0:c0 + cn]
    return acc


def ffn_fwd(up0, x1, cw, wdg, post_g, tt, rider=None):
    t = up0.shape[0]
    assert t % tt == 0, (t, tt)
    cn = _col_chunk(D_FF)

    def body(up0_ref, x1_ref, cw_ref, wdg_ref, pg_ref, d_ref, x2_ref, ext, last, wd_ref, sems):
        i = pl.program_id(0)

        @pl.when(i == 0)
        def _():
            _fetch_row_blocks(wdg_ref, wd_ref, sems)
            last[...] = jnp.zeros_like(last)

        ext[HR_F:HR_F + tt, :] = up0_ref[...]
        ext[0:HR_F, :] = _halo_before(up0_ref[tt - HR_F:, :], last[...])
        last[...] = up0_ref[tt - HR_F:, :]
        cwv = cw_ref[...]
        d = jnp.zeros((tt, D_MODEL), F32)
        for c0 in range(0, D_FF, cn):
            gate = _ffn_conv(ext, cwv, c0, cn, tt)
            val = _ffn_conv(ext, cwv, D_FF + c0, cn, tt)
            act = (gate * jax.nn.sigmoid(gate) * val).astype(MM_DTYPE)
            d = d + _dot(act, wd_ref[c0:c0 + cn, :])
        d_ref[...] = d
        x2_ref[...] = x1_ref[...] + d * _rstd(d) * pg_ref[...]

    row = lambda c: pl.BlockSpec((tt, c), lambda i: (i, 0))
    return _pallas(
        body, rider, name="ffn_fwd", steps=t // tt,
        in_specs=[row(2 * D_FF), row(D_MODEL), _const_spec((8, 2 * D_FF)), _ANY, _const_spec((1, D_MODEL))],
        out_specs=[row(D_MODEL), row(D_MODEL)],
        out_shape=[jax.ShapeDtypeStruct((t, D_MODEL), F32), jax.ShapeDtypeStruct((t, D_MODEL), F32)],
        scratch_shapes=[pltpu.VMEM((HR_F + tt, 2 * D_FF), F32), pltpu.VMEM((HR_F, 2 * D_FF), F32),
                        pltpu.VMEM((D_FF, D_MODEL), MM_DTYPE), pltpu.SemaphoreType.DMA((N_CHIPS,))],
        args=(up0, x1, cw, wdg, post_g))


def ffn_bwd(dx2, d, up0, cw, wdg, post_g, tt, rider=None):
    t = up0.shape[0]
    assert t % tt == 0, (t, tt)
    steps = t // tt
    hb = tt // HR_F
    cn = _col_chunk(D_FF)

    def body(dx2_ref, d_ref, up0_ref, uh_ref, cw_ref, wdg_ref, pg_ref,
             dd_ref, act_ref, dup0_ref, dpg_ref, dcw_ref, ext, dup_ext, first, a_pg, a_cw, wd_ref, sems):
        i = pl.program_id(0)
        tile = steps - 1 - i

        @pl.when(i == 0)
        def _():
            _fetch_row_blocks(wdg_ref, wd_ref, sems)
            a_pg[...] = jnp.zeros_like(a_pg)
            a_cw[...] = jnp.zeros_like(a_cw)
            first[...] = jnp.zeros_like(first)

        ext[HR_F:HR_F + tt, :] = up0_ref[...]
        ext[0:HR_F, :] = _halo_before(up0_ref[tt - HR_F:, :], jnp.where(tile > 0, uh_ref[...], 0.0))
        cwv = cw_ref[...]
        dv = d_ref[...]
        dx2v = dx2_ref[...]
        r = _rstd(dv)
        a_pg[...] += _rowsum8(dx2v * dv * r)
        dd = _rms_bwd(dv, r, pg_ref[...], dx2v).astype(MM_DTYPE)
        dd_ref[...] = dd
        for c0 in range(0, D_FF, cn):
            gate = _ffn_conv(ext, cwv, c0, cn, tt)
            val = _ffn_conv(ext, cwv, D_FF + c0, cn, tt)
            sg = jax.nn.sigmoid(gate)
            sl = gate * sg
            act_ref[:, c0:c0 + cn] = (sl * val).astype(MM_DTYPE)
            da = _dot_nt(dd, wd_ref[c0:c0 + cn, :])
            dup_ext[0:tt, c0:c0 + cn] = da * val * (sg * (1.0 + gate * (1.0 - sg)))
            dup_ext[0:tt, D_FF + c0:D_FF + c0 + cn] = da * sl
        dup_ext[tt:tt + HR_F, :] = _halo_after(dup_ext[0:HR_F, :], first[...])
        first[...] = dup_ext[0:HR_F, :]
        for c0 in range(0, 2 * D_FF, cn):
            x = up0_ref[:, c0:c0 + cn]
            acc = None
            for k in range(K_F):
                off = 8 * (K_F - 1 - k)
                ld = dup_ext[off:off + tt, c0:c0 + cn]
                term = cwv[k:k + 1, c0:c0 + cn] * ld
                acc = term if acc is None else acc + term
                a_cw[k, :, c0:c0 + cn] += _rowsum8(ld * x)
            dup0_ref[:, c0:c0 + cn] = acc.astype(MM_DTYPE)

        @pl.when(i == steps - 1)
        def _():
            dpg_ref[...] = jnp.sum(a_pg[...], axis=0, keepdims=True)
            dcw_ref[...] = jnp.sum(a_cw[...], axis=1)

    rev = lambda c: pl.BlockSpec((tt, c), lambda i: (steps - 1 - i, 0))
    halo = pl.BlockSpec((HR_F, 2 * D_FF), lambda i: (jnp.maximum((steps - 1 - i) * hb - 1, 0), 0))
    full = lambda shape: pl.BlockSpec(shape, lambda i: (0,) * len(shape))
    sds = jax.ShapeDtypeStruct
    return _pallas(
        body, rider, name="ffn_bwd", steps=steps,
        in_specs=[rev(D_MODEL), rev(D_MODEL), rev(2 * D_FF), halo, _const_spec((8, 2 * D_FF)), _ANY,
                  _const_spec((1, D_MODEL))],
        out_specs=[rev(D_MODEL), rev(D_FF), rev(2 * D_FF), full((1, D_MODEL)), full((8, 2 * D_FF))],
        out_shape=[sds((t, D_MODEL), MM_DTYPE), sds((t, D_FF), MM_DTYPE), sds((t, 2 * D_FF), MM_DTYPE),
                   sds((1, D_MODEL), F32), sds((8, 2 * D_FF), F32)],
        scratch_shapes=[pltpu.VMEM((HR_F + tt, 2 * D_FF), F32), pltpu.VMEM((tt + HR_F, 2 * D_FF), F32),
                        pltpu.VMEM((HR_F, 2 * D_FF), F32), pltpu.VMEM((8, D_MODEL), F32),
                        pltpu.VMEM((8, 8, 2 * D_FF), F32), pltpu.VMEM((D_FF, D_MODEL), MM_DTYPE),
                        pltpu.SemaphoreType.DMA((N_CHIPS,))],
        args=(dx2, d, up0, up0, cw, wdg, post_g))


def loss_head(y, target, tm):
    t, d = y.shape
    assert t % tm == 0, (t, tm)
    steps = t // tm

    def body(y_ref, t_ref, dy_ref, loss_ref, acc):
        i = pl.program_id(0)

        @pl.when(i == 0)
        def _():
            acc[...] = jnp.zeros_like(acc)

        diff = y_ref[...] - t_ref[...]
        dy_ref[...] = diff * (1.0 / d)
        acc[...] += _rowsum8(diff * diff)

        @pl.when(i == steps - 1)
        def _():
            loss_ref[...] = (0.5 / d) * jnp.sum(jnp.sum(acc[...], axis=0, keepdims=True), axis=1, keepdims=True)

    row = pl.BlockSpec((tm, d), lambda i: (i, 0))
    return pl.pallas_call(
        body, name="loss_head", grid=(steps,), in_specs=[row, row],
        out_specs=[row, pl.BlockSpec((1, 1), lambda i: (0, 0))],
        out_shape=[jax.ShapeDtypeStruct((t, d), F32), jax.ShapeDtypeStruct((1, 1), F32)],
        scratch_shapes=[pltpu.VMEM((8, d), F32)],
        compiler_params=_params(("arbitrary",)),
    )(y, target)


def adamw(w, g, m, v):
    shape = w.shape
    cols = shape[-1]
    rows = w.size // cols
    tr = next((r for r in (512, 256, 128) if rows % r == 0 and rows > r), rows)
    c1 = 1.0 - ADAM_B1 ** ADAM_STEP
    c2 = 1.0 - ADAM_B2 ** ADAM_STEP

    def body(w_ref, g_ref, m_ref, v_ref, d_ref, nm_ref, nv_ref):
        gv = g_ref[...]
        nm = ADAM_B1 * m_ref[...] + (1.0 - ADAM_B1) * gv
        nv = ADAM_B2 * v_ref[...] + (1.0 - ADAM_B2) * (gv * gv)
        nm_ref[...] = nm
        nv_ref[...] = nv
        d_ref[...] = -ADAM_LR * ((nm / c1) / (jnp.sqrt(nv / c2) + ADAM_EPS) + ADAM_WD * w_ref[...])

    spec = pl.BlockSpec((tr, cols), lambda i: (i, 0))
    out = jax.ShapeDtypeStruct((rows, cols), F32)
    res = pl.pallas_call(
        body, name="adamw", grid=(rows // tr,), in_specs=[spec] * 4, out_specs=[spec] * 3, out_shape=[out] * 3,
        compiler_params=_params(("arbitrary",)),
    )(*[a.reshape(rows, cols) for a in (w, g, m, v)])
    return tuple(r.reshape(shape) for r in res)


def _place():
    return lax.axis_index("x"), lax.axis_index("y"), lax.axis_index("c")


def _other_chips(x, y):
    return [(1 - x, y, 2 * (1 - x) + y), (x, 1 - y, 2 * x + 1 - y), (1 - x, 1 - y, 2 * (1 - x) + 1 - y)]


def _sem_specs(*counts):
    return [pltpu.SemaphoreType.DMA((n,)) for n in counts]


def cast_shard(w, layer, chip):
    _, r, c = w.shape

    def body(chip_ref, w_ref, o_ref):
        del chip_ref
        o_ref[...] = w_ref[...].astype(MM_DTYPE)

    grid_spec = pltpu.PrefetchScalarGridSpec(
        num_scalar_prefetch=1, grid=(1,), in_specs=[pl.BlockSpec((None, r, c), lambda i, chip_ref: (layer, 0, 0))],
        out_specs=pl.BlockSpec((None, r, c), lambda i, chip_ref: (chip_ref[0], 0, 0)))
    return pl.pallas_call(
        body, name="cast_shard", grid_spec=grid_spec, out_shape=jax.ShapeDtypeStruct((N_CHIPS, r, c), MM_DTYPE),
        compiler_params=_params(("arbitrary",)),
    )(jnp.reshape(chip, (1,)).astype(jnp.int32), w)


def _row_half(buf, chip, mine, c):
    rh = buf.shape[1] // 2
    return buf.at[chip, pl.ds(pl.multiple_of((c if mine else 1 - c) * rh, 16), rh), :]


def spread_rider(bufs):
    n = len(bufs)

    def start(rin, rout, sems):
        x, y, c = _place()
        me = 2 * x + y
        for k, (px, py, _) in enumerate(_other_chips(x, y)):
            for i, buf in enumerate(rout):
                part = _row_half(buf, me, True, c)
                pltpu.make_async_remote_copy(
                    src_ref=part, dst_ref=part, send_sem=sems[0].at[n * k + i], recv_sem=sems[1].at[n * k + i],
                    device_id=(px, py, c), device_id_type=MESH_ID).start()

    def wait(rin, rout, sems):
        x, y, c = _place()
        for k, (_, _, pj) in enumerate(_other_chips(x, y)):
            for i, buf in enumerate(rout):
                part = _row_half(buf, pj, True, c)
                pltpu.make_async_remote_copy(
                    src_ref=part, dst_ref=part, send_sem=sems[0].at[n * k + i], recv_sem=sems[1].at[n * k + i],
                    device_id=(x, y, c), device_id_type=MESH_ID).wait()

    shapes = [jax.ShapeDtypeStruct(b.shape, b.dtype) for b in bufs]
    return Rider("spread", list(bufs), shapes, {i: i for i in range(n)}, (3 * n, 3 * n), start, wait)


def pass_rider(bufs):
    n = len(bufs)

    def start(rin, rout, sems):
        x, y, c = _place()
        for k, (_, _, pj) in enumerate(_other_chips(x, y)):
            for i, buf in enumerate(rout):
                part = _row_half(buf, pj, True, c)
                pltpu.make_async_remote_copy(
                    src_ref=part, dst_ref=part, send_sem=sems[0].at[n * k + i], recv_sem=sems[1].at[n * k + i],
                    device_id=(x, y, 1 - c), device_id_type=MESH_ID).start()

    def wait(rin, rout, sems):
        x, y, c = _place()
        for k, (_, _, pj) in enumerate(_other_chips(x, y)):
            for i, buf in enumerate(rout):
                part = _row_half(buf, pj, False, c)
                pltpu.make_async_remote_copy(
                    src_ref=part, dst_ref=part, send_sem=sems[0].at[n * k + i], recv_sem=sems[1].at[n * k + i],
                    device_id=(x, y, 1 - c), device_id_type=MESH_ID).wait()

    shapes = [jax.ShapeDtypeStruct(b.shape, b.dtype) for b in bufs]
    return Rider("pass", list(bufs), shapes, {i: i for i in range(n)}, (3 * n, 3 * n), start, wait)


def both_riders(a, b):
    na, oa, sa = len(a.inputs), len(a.out_shapes), len(a.sems)

    def start(rin, rout, sems):
        a.start(rin[:na], rout[:oa], sems[:sa])
        b.start(rin[na:], rout[oa:], sems[sa:])

    def wait(rin, rout, sems):
        a.wait(rin[:na], rout[:oa], sems[:sa])
        b.wait(rin[na:], rout[oa:], sems[sa:])

    aliases = dict(a.aliases)
    aliases.update({na + i: oa + o for i, o in b.aliases.items()})
    return Rider(a.name + "_" + b.name, a.inputs + b.inputs, a.out_shapes + b.out_shapes, aliases, a.sems + b.sems,
                 start, wait)


def gather_small(small):
    def body(small_ref, out_ref, send, recv, local):
        x, y, c = _place()
        me = 2 * x + y
        chips = _other_chips(x, y)
        own = pltpu.make_async_copy(small_ref, out_ref.at[me], local.at[0])
        own.start()
        sends = [pltpu.make_async_remote_copy(src_ref=small_ref, dst_ref=out_ref.at[me], send_sem=send.at[k],
                                              recv_sem=recv.at[k], device_id=(px, py, c), device_id_type=MESH_ID)
                 for k, (px, py, _) in enumerate(chips)]
        for cp in sends:
            cp.start()
        for k, (_, _, pj) in enumerate(chips):
            pltpu.make_async_remote_copy(src_ref=small_ref, dst_ref=out_ref.at[pj], send_sem=send.at[k], recv_sem=recv.at[k],
                                         device_id=(x, y, c), device_id_type=MESH_ID).wait_recv()
        for cp in sends:
            cp.wait_send()
        own.wait()

    return pl.pallas_call(
        body, name="gather_small", in_specs=[_ANY], out_specs=_ANY,
        out_shape=jax.ShapeDtypeStruct((N_CHIPS,) + small.shape, small.dtype), scratch_shapes=_sem_specs(3, 3, 1),
        compiler_params=pltpu.CompilerParams(has_side_effects=True),
    )(small)


def swap_rider(gs):
    n = len(gs)

    def copies(rin, rout, sems):
        x, y, c = _place()
        out = []
        for i, (g, got) in enumerate(zip(rin, rout)):
            rh = g.shape[1] // 2
            theirs = pl.ds(pl.multiple_of((1 - c) * rh, 8), rh)
            out.append(pltpu.make_async_remote_copy(
                src_ref=g.at[:, theirs, :], dst_ref=got, send_sem=sems[0].at[i], recv_sem=sems[1].at[i],
                device_id=(x, y, 1 - c), device_id_type=MESH_ID))
        return out

    def start(rin, rout, sems):
        for cp in copies(rin, rout, sems):
            cp.start()

    def wait(rin, rout, sems):
        for cp in copies(rin, rout, sems):
            cp.wait()

    shapes = [jax.ShapeDtypeStruct((g.shape[0], g.shape[1] // 2, g.shape[2]), g.dtype) for g in gs]
    return Rider("swap", list(gs), shapes, {}, (n, n), start, wait)


def scatter_rider(sbs):
    n = len(sbs)

    def start(rin, rout, sems):
        x, y, c = _place()
        me = 2 * x + y
        for k, (px, py, pj) in enumerate(_other_chips(x, y)):
            for i, (sb, got) in enumerate(zip(rin, rout)):
                pltpu.make_async_remote_copy(
                    src_ref=sb.at[pj], dst_ref=got.at[me], send_sem=sems[0].at[n * k + i], recv_sem=sems[1].at[n * k + i],
                    device_id=(px, py, c), device_id_type=MESH_ID).start()

    def wait(rin, rout, sems):
        x, y, c = _place()
        for k, (_, _, pj) in enumerate(_other_chips(x, y)):
            for i, (sb, got) in enumerate(zip(rin, rout)):
                cp = pltpu.make_async_remote_copy(
                    src_ref=sb.at[pj], dst_ref=got.at[pj], send_sem=sems[0].at[n * k + i], recv_sem=sems[1].at[n * k + i],
                    device_id=(x, y, c), device_id_type=MESH_ID)
                cp.wait_recv()
                cp.wait_send()

    shapes = [jax.ShapeDtypeStruct(sb.shape, sb.dtype) for sb in sbs]
    return Rider("scatter", list(sbs), shapes, {}, (3 * n, 3 * n), start, wait)


def join_rider(fs, layer):
    n = len(fs)

    def half(f, mine, c):
        rh = f.shape[1] // 2
        return f.at[layer, pl.ds(pl.multiple_of((c if mine else 1 - c) * rh, 8), rh), :]

    def start(rin, rout, sems):
        x, y, c = _place()
        for i, f in enumerate(rout):
            pltpu.make_async_remote_copy(
                src_ref=half(f, True, c), dst_ref=half(f, True, c), send_sem=sems[0].at[i], recv_sem=sems[1].at[i],
                device_id=(x, y, 1 - c), device_id_type=MESH_ID).start()

    def wait(rin, rout, sems):
        x, y, c = _place()
        for i, f in enumerate(rout):
            pltpu.make_async_remote_copy(
                src_ref=half(f, False, c), dst_ref=half(f, False, c), send_sem=sems[0].at[i], recv_sem=sems[1].at[i],
                device_id=(x, y, 1 - c), device_id_type=MESH_ID).wait()

    shapes = [jax.ShapeDtypeStruct(f.shape, f.dtype) for f in fs]
    return Rider("join", list(fs), shapes, {i: i for i in range(n)}, (n, n), start, wait)


def _row_tile(r):
    return next(t for t in (256, 352, 128) if r % t == 0)


def add_halves(g, got):
    n, rh, cols = got.shape
    tr = _row_tile(rh)
    c = lax.axis_index("c")

    def body(c_ref, g_ref, got_ref, s_ref, sb_ref):
        del c_ref
        s = g_ref[...] + got_ref[...]
        s_ref[...] = s
        sb_ref[...] = s.astype(BF16)

    blk = (None, tr, cols)
    grid_spec = pltpu.PrefetchScalarGridSpec(
        num_scalar_prefetch=1, grid=(n, rh // tr),
        in_specs=[pl.BlockSpec(blk, lambda j, i, c_ref: (j, c_ref[0] * (rh // tr) + i, 0)),
                  pl.BlockSpec(blk, lambda j, i, c_ref: (j, i, 0))],
        out_specs=[pl.BlockSpec(blk, lambda j, i, c_ref: (j, i, 0))] * 2)
    return pl.pallas_call(
        body, name="add_halves", grid_spec=grid_spec,
        out_shape=[jax.ShapeDtypeStruct(got.shape, F32), jax.ShapeDtypeStruct(got.shape, BF16)],
        compiler_params=_params(("arbitrary",) * 2),
    )(jnp.reshape(c, (1,)).astype(jnp.int32), g, got)


def add_chips(s, got, fbuf, layer):
    n, rh, cols = s.shape
    tr = _row_tile(rh)
    x, y, c = _place()
    me = 2 * x + y

    def body(p_ref, s_ref, g1_ref, g2_ref, g3_ref, f_ref, o_ref):
        del p_ref, f_ref
        o_ref[...] = s_ref[...] + g1_ref[...].astype(F32) + g2_ref[...].astype(F32) + g3_ref[...].astype(F32)

    blk = (None, tr, cols)

    def other(k):
        return pl.BlockSpec(blk, lambda i, p_ref: ((p_ref[0] + k) % n, i, 0))

    grid_spec = pltpu.PrefetchScalarGridSpec(
        num_scalar_prefetch=1, grid=(rh // tr,),
        in_specs=[pl.BlockSpec(blk, lambda i, p_ref: (p_ref[0], i, 0)), other(1), other(2), other(3), _ANY],
        out_specs=pl.BlockSpec(blk, lambda i, p_ref: (layer, p_ref[1] * (rh // tr) + i, 0)))
    return pl.pallas_call(
        body, name="add_chips", grid_spec=grid_spec, out_shape=jax.ShapeDtypeStruct(fbuf.shape, F32),
        input_output_aliases={5: 0}, compiler_params=_params(("arbitrary",)),
    )(jnp.stack([me, c]).astype(jnp.int32), s, got, got, got, fbuf)


def gather_all(v):
    def body(v_ref, o_ref, send, recv, local):
        x, y, c = _place()
        me = 4 * x + 2 * y + c
        own = pltpu.make_async_copy(v_ref, o_ref.at[me], local.at[0])
        own.start()
        sends = []
        k = 0
        for fx in (0, 1):
            for fy in (0, 1):
                for fc in (0, 1):
                    if fx or fy or fc:
                        cp = pltpu.make_async_remote_copy(
                            src_ref=v_ref, dst_ref=o_ref.at[me], send_sem=send.at[k], recv_sem=recv.at[k],
                            device_id=(x ^ fx, y ^ fy, c ^ fc), device_id_type=MESH_ID)
                        cp.start()
                        sends.append((cp, k, 4 * (x ^ fx) + 2 * (y ^ fy) + (c ^ fc)))
                        k += 1
        for cp, k, peer in sends:
            pltpu.make_async_remote_copy(
                src_ref=v_ref, dst_ref=o_ref.at[peer], send_sem=send.at[k], recv_sem=recv.at[k],
                device_id=(x, y, c), device_id_type=MESH_ID).wait_recv()
        for cp, _, _ in sends:
            cp.wait_send()
        own.wait()

    return pl.pallas_call(
        body, name="gather_all", in_specs=[_ANY], out_specs=_ANY,
        out_shape=jax.ShapeDtypeStruct((8,) + v.shape, v.dtype), scratch_shapes=_sem_specs(7, 7, 1),
        compiler_params=pltpu.CompilerParams(has_side_effects=True),
    )(v)


def sum_devices(v8, tr):
    n, rows, lanes = v8.shape
    assert rows % tr == 0, (rows, tr)

    def body(v_ref, o_ref):
        acc = v_ref[0]
        for j in range(1, n):
            acc = acc + v_ref[j]
        o_ref[...] = acc

    return pl.pallas_call(
        body, name="sum_devices", grid=(rows // tr,),
        in_specs=[pl.BlockSpec((n, tr, lanes), lambda i: (0, i, 0))], out_specs=pl.BlockSpec((tr, lanes), lambda i: (i, 0)),
        out_shape=jax.ShapeDtypeStruct((rows, lanes), F32), compiler_params=_params(("arbitrary",)),
    )(v8)


def _pack(arrays, rows):
    flat = jnp.concatenate([a.reshape(-1) for a in arrays])
    return jnp.pad(flat, (0, rows * LANES - flat.size)).reshape(rows, LANES)


def _unpack(buf, shapes):
    flat = buf.reshape(-1)
    out, at = [], 0
    for s in shapes:
        n = math.prod(s)
        out.append(flat[at:at + n].reshape(s))
        at += n
    return out


CONV_SHARDS = [(DEPTH, K_A, D_A // N_CHIPS), (DEPTH, K_C, D_C // N_CHIPS), (DEPTH, K_F, 2 * D_FF // N_CHIPS)]
CONV_ROWS = 32
SMALL_ROWS = 576


def _join_cols(g):
    n, l, r, c = g.shape
    return jnp.transpose(g, (1, 2, 0, 3)).reshape(l, r, n * c)


BIG = ["w_in", "w_out", "w_up", "w_down"]
TILE_MM = 512
TILE_EW = 256


def _pad_rows(a, rows):
    return jnp.pad(a, ((0, rows - a.shape[0]), (0, 0)))


def _row(a):
    return a.reshape(1, -1)


def _tile_perm(tt):
    p = lax.broadcasted_iota(jnp.int32, (tt, tt), 0)
    tok = lax.broadcasted_iota(jnp.int32, (tt, tt), 1)
    return ((tt // 8) * (p % 8) + p // 8 == tok).astype(F32)


def _layer_params(wl, tt):
    n = tt // CHUNK
    tril = jnp.tril(jnp.ones((CHUNK, CHUNK), bool))
    wm = jnp.where(tril[None], wl["sgu_w"], 0.0)
    eye = jnp.eye(n, dtype=F32)
    wt = (eye[None, :, None, :, None] * wm[:, None, :, None, :]).reshape(N_HEADS_B, tt, tt)
    bias_e = jnp.repeat(wl["sgu_b"].T, HEAD, axis=1)
    return (_pad_rows(wl["conv_a_w"], 8), _row(wl["sgu_ln_g"]), _row(wl["sgu_ln_b"]), wt.astype(MM_DTYPE),
            jnp.tile(bias_e, (n, 1)), _pad_rows(wl["conv_c_w"], 32), _row(wl["conv_c_b"]), _row(wl["conv_ln_g"]),
            _row(wl["conv_ln_b"]))


def layer_fwd(x, wl, gw, nxt=None, tm=TILE_MM, tt=TILE_EW):
    mp = _layer_params(wl, tt)
    ride = pass_rider([gw["w_down"]]) if gw.get("pass_down") else None
    (z, h), done = norm_matmul(x, _row(wl["pre_mix_g"]), gw["w_in"], tm, rider=ride)
    gw = {n: (done[0] if ride and n == "w_down" else gw[n]) for n in BIG}
    ride = spread_rider([nxt["w_in"], nxt["w_out"]]) if nxt else None
    (o, x1, cv), done = mixer_fwd(z, x, mp, _tile_perm(tt), _row(wl["grp_norm_g"]), gw["w_out"], _row(wl["post_mix_g"]), tt,
                                  rider=ride)
    ride = both_riders(spread_rider([nxt["w_up"]]), pass_rider(list(done))) if nxt else None
    (up0, h2), done = norm_matmul(x1, _row(wl["pre_ffn_g"]), gw["w_up"], tt, rider=ride)
    if nxt:
        nxt = dict(nxt, w_up=done[0], w_in=done[1], w_out=done[2])
        ride = both_riders(spread_rider([nxt["w_down"]]), pass_rider([nxt["w_up"]]))
    (d, x2), done = ffn_fwd(up0, x1, _pad_rows(wl["ffn_conv_w"], 8), gw["w_down"], _row(wl["post_ffn_g"]), tt, rider=ride)
    if nxt:
        nxt = dict(nxt, w_down=done[0], w_up=done[1], pass_down=True)
    return x2, dict(x=x, z=z, h=h, o=o, x1=x1, up0=up0, h2=h2, d=d, cv=cv, gw=gw), nxt


def layer_bwd(dx2, wl, layer, sv, pend=None, exchange=True, tm=TILE_MM, tt=TILE_EW):
    mp = _layer_params(wl, tt)
    gw = sv["gw"]
    tk = min(512, dx2.shape[0])
    g = {}
    ride = scatter_rider([s16 for _, s16 in pend["sums"]]) if pend else None
    (dd, act, dup0, dpg, dcw), arrived = ffn_bwd(dx2, sv["d"], sv["up0"], _pad_rows(wl["ffn_conv_w"], 8), gw["w_down"],
                                                 _row(wl["post_ffn_g"]), tt, rider=ride)
    if pend:
        fbuf = [add_chips(s32, got, fb, pend["layer"]) for (s32, _), got, fb in zip(pend["sums"], arrived, pend["fbuf"])]
    g["post_ffn_g"] = dpg[0]
    g["ffn_conv_w"] = dcw[:K_F]
    gl = {}
    gl["w_down"] = matmul_tn_down(act, dd, tk)
    gl["w_up"] = matmul_tn_blocks(sv["h2"], dup0, D_MODEL, 2 * D_FF // N_CHIPS, tk, by_rows=False)
    (dx1, dg), _ = matmul_nt_norm_bwd(dup0, gw["w_up"], sv["x1"], _row(wl["pre_ffn_g"]), dx2, tm)
    g["pre_ffn_g"] = dg[0]
    ride = join_rider(fbuf, pend["layer"]) if pend else None
    (dz, do, yn, dpg, dgg, dcaw, dlng, dlnb, dwm, dbias, dccw, dccb, dclg, dclb), joined = mixer_bwd(
        dx1, sv["o"], sv["z"], sv["cv"], mp, _tile_perm(tt), _row(wl["grp_norm_g"]), gw["w_out"],
        _row(wl["post_mix_g"]), tt, rider=ride)
    g["post_mix_g"] = dpg[0]
    g["grp_norm_g"] = dgg[0]
    g["conv_a_w"] = dcaw[:K_A]
    g["sgu_ln_g"] = dlng[0]
    g["sgu_ln_b"] = dlnb[0]
    g["sgu_w"] = dwm
    g["sgu_b"] = jnp.sum(dbias.reshape(CHUNK, N_HEADS_B, HEAD), axis=2).T
    g["conv_c_w"] = dccw[:K_C]
    g["conv_c_b"] = dccb[0]
    g["conv_ln_g"] = dclg[0]
    g["conv_ln_b"] = dclb[0]
    gl["w_out"] = matmul_tn_blocks(yn, do, D_MODEL // N_CHIPS, D_MODEL, tk, by_rows=True)
    gl["w_in"] = matmul_tn_in(sv["h"], dz, tk)
    gs = [gl[n] for n in BIG]
    ride = swap_rider(gs) if exchange else None
    (dx, dg), got = matmul_nt_norm_bwd(dz, gw["w_in"], sv["x"], _row(wl["pre_mix_g"]), dx1, tm, rider=ride)
    g["pre_mix_g"] = dg[0]
    if not exchange:
        return dx, g, gl
    sums = [add_halves(gi, gt) for gi, gt in zip(gs, got)]
    return dx, g, dict(sums=sums, fbuf=list(joined) if pend else grad_buffers(), layer=layer)


def grad_buffers():
    return [lax.empty(s, F32) for s in ((DEPTH, D_MODEL, D_IN // N_CHIPS), (DEPTH, D_MODEL // N_CHIPS, D_MODEL),
                                        (DEPTH, D_MODEL, 2 * D_FF // N_CHIPS), (DEPTH, D_FF // N_CHIPS, D_MODEL))]


CONV = ["conv_a_w", "conv_c_w", "ffn_conv_w"]
REPL = ["pre_mix_g", "sgu_ln_g", "sgu_ln_b", "sgu_w", "sgu_b", "conv_c_b", "conv_ln_g", "conv_ln_b", "grp_norm_g",
        "post_mix_g", "pre_ffn_g", "post_ffn_g"]
WEIGHTS = ["pre_mix_g", "w_in", "conv_a_w", "sgu_ln_g", "sgu_ln_b", "sgu_w", "sgu_b", "conv_c_w", "conv_c_b", "conv_ln_g",
           "conv_ln_b", "grp_norm_g", "w_out", "post_mix_g", "pre_ffn_g", "w_up", "ffn_conv_w", "w_down", "post_ffn_g"]


def kernel(x, pre_mix_g, w_in, conv_a_w, sgu_ln_g, sgu_ln_b, sgu_w, sgu_b, conv_c_w, conv_c_b, conv_ln_g, conv_ln_b, grp_norm_g, w_out, post_mix_g, pre_ffn_g, w_up, ffn_conv_w, w_down, post_ffn_g, loss_target, m_pre_mix_g, m_w_in, m_conv_a_w, m_sgu_ln_g, m_sgu_ln_b, m_sgu_w, m_sgu_b, m_conv_c_w, m_conv_c_b, m_conv_ln_g, m_conv_ln_b, m_grp_norm_g, m_w_out, m_post_mix_g, m_pre_ffn_g, m_w_up, m_ffn_conv_w, m_w_down, m_post_ffn_g, v_pre_mix_g, v_w_in, v_conv_a_w, v_sgu_ln_g, v_sgu_ln_b, v_sgu_w, v_sgu_b, v_conv_c_w, v_conv_c_b, v_conv_ln_g, v_conv_ln_b, v_grp_norm_g, v_w_out, v_post_mix_g, v_pre_ffn_g, v_w_up, v_ffn_conv_w, v_w_down, v_post_ffn_g):
    w = dict(pre_mix_g=pre_mix_g, w_in=w_in, conv_a_w=conv_a_w, sgu_ln_g=sgu_ln_g, sgu_ln_b=sgu_ln_b, sgu_w=sgu_w, sgu_b=sgu_b,
             conv_c_w=conv_c_w, conv_c_b=conv_c_b, conv_ln_g=conv_ln_g, conv_ln_b=conv_ln_b, grp_norm_g=grp_norm_g,
             w_out=w_out, post_mix_g=post_mix_g, pre_ffn_g=pre_ffn_g, w_up=w_up, ffn_conv_w=ffn_conv_w, w_down=w_down,
             post_ffn_g=post_ffn_g)
    m = dict(pre_mix_g=m_pre_mix_g, w_in=m_w_in, conv_a_w=m_conv_a_w, sgu_ln_g=m_sgu_ln_g, sgu_ln_b=m_sgu_ln_b,
             sgu_w=m_sgu_w, sgu_b=m_sgu_b, conv_c_w=m_conv_c_w, conv_c_b=m_conv_c_b, conv_ln_g=m_conv_ln_g,
             conv_ln_b=m_conv_ln_b, grp_norm_g=m_grp_norm_g, w_out=m_w_out, post_mix_g=m_post_mix_g,
             pre_ffn_g=m_pre_ffn_g, w_up=m_w_up, ffn_conv_w=m_ffn_conv_w, w_down=m_w_down, post_ffn_g=m_post_ffn_g)
    v = dict(pre_mix_g=v_pre_mix_g, w_in=v_w_in, conv_a_w=v_conv_a_w, sgu_ln_g=v_sgu_ln_g, sgu_ln_b=v_sgu_ln_b,
             sgu_w=v_sgu_w, sgu_b=v_sgu_b, conv_c_w=v_conv_c_w, conv_c_b=v_conv_c_b, conv_ln_g=v_conv_ln_g,
             conv_ln_b=v_conv_ln_b, grp_norm_g=v_grp_norm_g, w_out=v_w_out, post_mix_g=v_post_mix_g,
             pre_ffn_g=v_pre_ffn_g, w_up=v_w_up, ffn_conv_w=v_ffn_conv_w, w_down=v_w_down, post_ffn_g=v_post_ffn_g)
    chip = 2 * lax.axis_index("x") + lax.axis_index("y")

    convs = gather_small(_pack([w[n] for n in CONV], CONV_ROWS))
    gws = [{n: cast_shard(w[n], layer, chip) for n in BIG} for layer in range(DEPTH)]
    first = run_rider(pass_rider(run_rider(spread_rider([gws[0][n] for n in BIG]))))
    gws[0] = dict(zip(BIG, first))
    cparts = [_unpack(convs[j], CONV_SHARDS) for j in range(N_CHIPS)]
    full = dict(w)
    for i, n in enumerate(CONV):
        full[n] = _join_cols(jnp.stack([p[i] for p in cparts]))

    xc = to_tiles(x[0], TILE_EW)
    saved = []
    for layer in range(DEPTH):
        nxt = gws[layer + 1] if layer + 1 < DEPTH else None
        xc, sv, nxt = layer_fwd(xc, {n: full[n][layer] for n in REPL + CONV}, gws[layer], nxt)
        if nxt:
            gws[layer + 1] = nxt
        saved.append(sv)
    dxc, loss_part = loss_head(xc, to_tiles(loss_target[0], TILE_EW), TILE_MM)
    loss = lax.psum(loss_part[0, 0], ("x", "y", "c"))
    small = [None] * DEPTH
    pend = None
    for layer in reversed(range(DEPTH)):
        dxc, small[layer], pend = layer_bwd(dxc, {n: full[n][layer] for n in REPL + CONV}, layer, saved[layer], pend)
    grads = {n: jnp.stack([small[layer][n] for layer in range(DEPTH)]) for n in REPL + CONV}

    arrived = run_rider(scatter_rider([s16 for _, s16 in pend["sums"]]))
    fbuf = [add_chips(s32, got, fb, 0) for (s32, _), got, fb in zip(pend["sums"], arrived, pend["fbuf"])]
    out_g = dict(zip(BIG, run_rider(join_rider(fbuf, 0))))

    tot = sum_devices(gather_all(_pack([grads[n] for n in REPL + CONV], SMALL_ROWS)), 192)
    shapes = [grads[n].shape for n in REPL + CONV]
    for n, gfull in zip(REPL + CONV, _unpack(tot, shapes)):
        if n in CONV:
            width = gfull.shape[-1] // N_CHIPS
            gfull = lax.dynamic_slice_in_dim(gfull, chip * width, width, axis=2)
        out_g[n] = gfull

    deltas, new_m, new_v = {}, {}, {}
    for n in WEIGHTS:
        deltas[n], new_m[n], new_v[n] = adamw(w[n], out_g[n], m[n], v[n])
    return (loss, from_tiles(dxc, TILE_EW)[None], *[out_g[n] for n in WEIGHTS], *[deltas[n] for n in WEIGHTS], *[new_m[n] for n in WEIGHTS],
            *[new_v[n] for n in WEIGHTS])
```

```python
import functools
import math
from typing import Callable, NamedTuple

import jax
import jax.numpy as jnp
from jax import lax
from jax.experimental import pallas as pl
from jax.experimental.pallas import tpu as pltpu

F32 = jnp.float32
BF16 = jnp.bfloat16
MM_DTYPE = BF16

D_MODEL = 1024
SEQ = 4096
DEPTH = 4
D_A = 256
D_B = 384
D_C = 384
D_IN = 3 * D_A + 2 * D_B + 2 * D_C
D_FF = 2816
K_A = 3
K_C = 31
K_F = 3
CHUNK = 128
HEAD = 64
N_HEADS_B = D_B // HEAD
EPS = 1e-6
N_CHIPS = 4

ADAM_LR = 0.001
ADAM_B1 = 0.9
ADAM_B2 = 0.999
ADAM_EPS = 1e-08
ADAM_WD = 0.01
ADAM_STEP = 10

LANES = 1024
VMEM_LIMIT = 56 * 1024 * 1024

MESH_ID = pl.DeviceIdType.MESH
_ANY = pl.BlockSpec(memory_space=pl.ANY)


def _params(sem=None):
    return pltpu.CompilerParams(dimension_semantics=sem, vmem_limit_bytes=VMEM_LIMIT)


def _const_spec(shape):
    nd = len(shape)
    return pl.BlockSpec(shape, lambda *_: (0,) * nd, pipeline_mode=pl.Buffered(1))


def _rowsum8(a):
    r, c = a.shape
    return jnp.sum(a.reshape(r // 8, 8, c), axis=0)


def _rstd(x):
    return lax.rsqrt(jnp.mean(x * x, axis=-1, keepdims=True) + EPS)


def _rms_bwd(x, r, g, dy):
    gdy = g * dy
    return r * gdy - x * (r * r * r) * jnp.mean(gdy * x, axis=-1, keepdims=True)


def _ln_fwd(x):
    mu = jnp.mean(x, axis=-1, keepdims=True)
    xc = x - mu
    r = lax.rsqrt(jnp.mean(xc * xc, axis=-1, keepdims=True) + EPS)
    return xc * r, r


def _ln_bwd(xh, r, dxh):
    return r * (dxh - jnp.mean(dxh, axis=-1, keepdims=True) - xh * jnp.mean(dxh * xh, axis=-1, keepdims=True))


def _gelu(x):
    return 0.5 * x * (1.0 + lax.erf(x * (1.0 / math.sqrt(2.0))))


def _gelu_grad(x):
    cdf = 0.5 * (1.0 + lax.erf(x * (1.0 / math.sqrt(2.0))))
    pdf = jnp.exp(-0.5 * x * x) * (1.0 / math.sqrt(2.0 * math.pi))
    return cdf + x * pdf


def _dot(a, b):
    return jnp.dot(a, b, preferred_element_type=F32)


def _dot_nt(a, b):
    return lax.dot_general(a, b, (((1,), (1,)), ((), ())), preferred_element_type=F32)


def _dot_tn(a, b):
    return lax.dot_general(a, b, (((0,), (0,)), ((), ())), preferred_element_type=F32)


def _col_chunk(n):
    for c in (1408, 1024, 768, 512, 256, 128):
        if n % c == 0:
            return c
    raise ValueError(n)


class Rider(NamedTuple):
    name: str
    inputs: list
    out_shapes: list
    aliases: dict
    sems: tuple
    start: Callable
    wait: Callable


def _pallas(body, rider, *, name, steps, in_specs, out_specs, out_shape, scratch_shapes, args):
    if rider is None:
        res = pl.pallas_call(body, name=name, grid=(steps,), in_specs=in_specs, out_specs=out_specs, out_shape=out_shape,
                             scratch_shapes=scratch_shapes, compiler_params=_params(("arbitrary",)))(*args)
        return res, []
    n_in, n_out, n_scr = len(in_specs), len(out_specs), len(scratch_shapes)
    r_in, r_out = len(rider.inputs), len(rider.out_shapes)

    def wrapped(*refs):
        ins, rin = refs[:n_in], refs[n_in:n_in + r_in]
        at = n_in + r_in
        outs, rout = refs[at:at + n_out], refs[at + n_out:at + n_out + r_out]
        at += n_out + r_out
        scr, rsem = refs[at:at + n_scr], refs[at + n_scr:]

        @pl.when(pl.program_id(0) == 0)
        def _():
            rider.start(rin, rout, rsem)

        body(*ins, *outs, *scr)

        @pl.when(pl.program_id(0) == steps - 1)
        def _():
            rider.wait(rin, rout, rsem)

    res = pl.pallas_call(
        wrapped, name=name + "_" + rider.name, grid=(steps,), in_specs=list(in_specs) + [_ANY] * r_in,
        out_specs=list(out_specs) + [_ANY] * r_out, out_shape=list(out_shape) + list(rider.out_shapes),
        scratch_shapes=list(scratch_shapes) + [pltpu.SemaphoreType.DMA((n,)) for n in rider.sems],
        input_output_aliases={n_in + i: n_out + o for i, o in rider.aliases.items()},
        compiler_params=pltpu.CompilerParams(dimension_semantics=("arbitrary",), vmem_limit_bytes=VMEM_LIMIT,
                                             has_side_effects=True),
    )(*args, *rider.inputs)
    return res[:n_out], res[n_out:]


def run_rider(rider):
    def body(*refs):
        r_in, r_out = len(rider.inputs), len(rider.out_shapes)
        rin, rout, rsem = refs[:r_in], refs[r_in:r_in + r_out], refs[r_in + r_out:]
        rider.start(rin, rout, rsem)
        rider.wait(rin, rout, rsem)

    return pl.pallas_call(
        body, name=rider.name, in_specs=[_ANY] * len(rider.inputs), out_specs=[_ANY] * len(rider.out_shapes),
        out_shape=list(rider.out_shapes), scratch_shapes=[pltpu.SemaphoreType.DMA((n,)) for n in rider.sems],
        input_output_aliases=dict(rider.aliases), compiler_params=pltpu.CompilerParams(has_side_effects=True),
    )(*rider.inputs)


def _weight_spec(wg):
    return _const_spec(wg.shape)


def _join_col_blocks(w_ref, w_scr):
    c = w_ref.shape[2]
    for j in range(N_CHIPS):
        w_scr[:, c * j:c * (j + 1)] = w_ref[j]


def norm_matmul(x, g, wg, tm, rider=None):
    t, d = x.shape
    assert t % tm == 0, (t, tm)
    cw = wg.shape[2]
    n = N_CHIPS * cw
    aligned = cw % 128 == 0
    cn = cw if aligned else _col_chunk(n)

    def body(x_ref, g_ref, w_ref, o_ref, h_ref, *scr):
        if not aligned:
            @pl.when(pl.program_id(0) == 0)
            def _():
                _join_col_blocks(w_ref, scr[0])

        xv = x_ref[...]
        h = (xv * _rstd(xv) * g_ref[...]).astype(MM_DTYPE)
        h_ref[...] = h
        for j, c0 in enumerate(range(0, n, cn)):
            wv = w_ref[j] if aligned else scr[0][:, c0:c0 + cn]
            o_ref[:, c0:c0 + cn] = _dot(h, wv)

    return _pallas(
        body, rider, name="norm_matmul", steps=t // tm,
        in_specs=[pl.BlockSpec((tm, d), lambda i: (i, 0)), _const_spec((1, d)), _weight_spec(wg)],
        out_specs=[pl.BlockSpec((tm, n), lambda i: (i, 0)), pl.BlockSpec((tm, d), lambda i: (i, 0))],
        out_shape=[jax.ShapeDtypeStruct((t, n), F32), jax.ShapeDtypeStruct((t, d), MM_DTYPE)],
        scratch_shapes=[] if aligned else [pltpu.VMEM((d, n), MM_DTYPE)],
        args=(x, g, wg))


def matmul_nt_norm_bwd(gy, wg, x, g, dres, tm, rider=None):
    t, n = gy.shape
    assert t % tm == 0, (t, tm)
    d, cw = wg.shape[1], wg.shape[2]
    aligned = cw % 128 == 0
    cn = cw if aligned else _col_chunk(n)
    steps = t // tm

    def body(gy_ref, w_ref, x_ref, g_ref, dres_ref, dx_ref, dg_ref, acc_ref, *scr):
        i = pl.program_id(0)

        @pl.when(i == 0)
        def _():
            acc_ref[...] = jnp.zeros_like(acc_ref)
            if not aligned:
                _join_col_blocks(w_ref, scr[0])

        dh = jnp.zeros((tm, d), F32)
        for j, c0 in enumerate(range(0, n, cn)):
            wv = w_ref[j] if aligned else scr[0][:, c0:c0 + cn]
            dh = dh + _dot_nt(gy_ref[:, c0:c0 + cn], wv)
        xv = x_ref[...]
        r = _rstd(xv)
        gv = g_ref[...]
        dx_ref[...] = dres_ref[...] + _rms_bwd(xv, r, gv, dh)
        acc_ref[...] += _rowsum8(dh * xv * r)

        @pl.when(i == steps - 1)
        def _():
            dg_ref[...] = jnp.sum(acc_ref[...], axis=0, keepdims=True)

    return _pallas(
        body, rider, name="matmul_nt_norm_bwd", steps=steps,
        in_specs=[pl.BlockSpec((tm, n), lambda i: (i, 0)), _weight_spec(wg), pl.BlockSpec((tm, d), lambda i: (i, 0)),
                  _const_spec((1, d)), pl.BlockSpec((tm, d), lambda i: (i, 0))],
        out_specs=[pl.BlockSpec((tm, d), lambda i: (i, 0)), pl.BlockSpec((1, d), lambda i: (0, 0))],
        out_shape=[jax.ShapeDtypeStruct((t, d), F32), jax.ShapeDtypeStruct((1, d), F32)],
        scratch_shapes=[pltpu.VMEM((8, d), F32)] + ([] if aligned else [pltpu.VMEM((d, n), MM_DTYPE)]),
        args=(gy, wg, x, g, dres))


def matmul_tn_blocks(a, b, r, c, tk, by_rows):
    t = a.shape[0]
    assert t % tk == 0 and r % 8 == 0 and c % 128 == 0, (a.shape, b.shape, r, c, tk)

    def body(a_ref, b_ref, o_ref):
        @pl.when(pl.program_id(1) == 0)
        def _():
            o_ref[...] = jnp.zeros_like(o_ref)

        o_ref[...] += _dot_tn(a_ref[...], b_ref[...])

    a_spec = pl.BlockSpec((tk, r), (lambda j, k: (k, j)) if by_rows else (lambda j, k: (k, 0)))
    b_spec = pl.BlockSpec((tk, c), (lambda j, k: (k, 0)) if by_rows else (lambda j, k: (k, j)))
    return pl.pallas_call(
        body, name="matmul_tn_blocks", grid=(N_CHIPS, t // tk), in_specs=[a_spec, b_spec],
        out_specs=pl.BlockSpec((None, r, c), lambda j, k: (j, 0, 0)),
        out_shape=jax.ShapeDtypeStruct((N_CHIPS, r, c), F32),
        compiler_params=_params(("arbitrary", "arbitrary")),
    )(a, b)


def matmul_tn_down(act, dd, tk):
    t, m = act.shape
    c = dd.shape[1]
    r = m // N_CHIPS
    assert t % tk == 0, (t, tk)
    steps = t // tk

    def body(a_ref, b_ref, o_ref, acc):
        k = pl.program_id(1)

        @pl.when(k == 0)
        def _():
            acc[...] = jnp.zeros_like(acc)

        acc[...] += _dot_tn(a_ref[...], b_ref[...])

        @pl.when(k == steps - 1)
        def _():
            o_ref[0] = acc[0:r, :]
            o_ref[1] = acc[r:2 * r, :]

    return pl.pallas_call(
        body, name="matmul_tn_down", grid=(2, steps),
        in_specs=[pl.BlockSpec((tk, 2 * r), lambda p, k: (k, p)), pl.BlockSpec((tk, c), lambda p, k: (k, 0))],
        out_specs=pl.BlockSpec((2, r, c), lambda p, k: (p, 0, 0)),
        out_shape=jax.ShapeDtypeStruct((N_CHIPS, r, c), F32),
        scratch_shapes=[pltpu.VMEM((2 * r, c), F32)],
        compiler_params=_params(("arbitrary", "arbitrary")),
    )(act, dd)


def matmul_tn_in(h, dz, tk):
    t, d = h.shape
    n = dz.shape[1]
    c = n // N_CHIPS
    assert t % tk == 0, (t, tk)
    steps = t // tk

    def body(a_ref, b_ref, o_ref, acc):
        k = pl.program_id(0)

        @pl.when(k == 0)
        def _():
            acc[...] = jnp.zeros_like(acc)

        acc[...] += _dot_tn(a_ref[...], b_ref[...])

        @pl.when(k == steps - 1)
        def _():
            for j in range(N_CHIPS):
                o_ref[j] = acc[:, c * j:c * (j + 1)]

    return pl.pallas_call(
        body, name="matmul_tn_in", grid=(steps,),
        in_specs=[pl.BlockSpec((tk, d), lambda k: (k, 0)), pl.BlockSpec((tk, n), lambda k: (k, 0))],
        out_specs=pl.BlockSpec((N_CHIPS, d, c), lambda k: (0, 0, 0)),
        out_shape=jax.ShapeDtypeStruct((N_CHIPS, d, c), F32),
        scratch_shapes=[pltpu.VMEM((d, n), F32)],
        compiler_params=_params(("arbitrary",)),
    )(h, dz)


def to_tiles(a, tt):
    t = a.shape[0]
    return a.reshape((t // tt, 8, tt // 8) + a.shape[1:]).swapaxes(1, 2).reshape(a.shape)


def from_tiles(a, tt):
    t = a.shape[0]
    return a.reshape((t // tt, tt // 8, 8) + a.shape[1:]).swapaxes(1, 2).reshape(a.shape)


def _roll_sublanes(a, shift):
    n = a.shape[0] // 8
    return pltpu.roll(a.reshape(n, 8, a.shape[1]), shift, 1).reshape(a.shape)


def _halo_before(cur_last, prev_last):
    sub = lax.broadcasted_iota(jnp.int32, cur_last.shape, 0) % 8
    return jnp.where(sub == 0, _roll_sublanes(prev_last, 1), _roll_sublanes(cur_last, 1))


def _halo_after(cur_first, next_first):
    sub = lax.broadcasted_iota(jnp.int32, cur_first.shape, 0) % 8
    return jnp.where(sub == 7, _roll_sublanes(next_first, 7), _roll_sublanes(cur_first, 7))


def _conv_causal(ext, cur, prev_last, w, taps, tt, cols=None):
    hr = 8 * (taps - 1)
    cs = slice(None) if cols is None else cols
    ext[hr:hr + tt, cs] = cur
    ext[0:hr, cs] = _halo_before(cur[tt - hr:, :], prev_last)
    acc = w[0:1, :] * ext[0:tt, cs]
    for k in range(1, taps):
        acc = acc + w[k:k + 1, :] * ext[8 * k:8 * k + tt, cs]
    return acc


def _conv_anticausal(ext, cur, next_first, w, taps, tt, x=None, acc_w=None, cols=None):
    hr = 8 * (taps - 1)
    cs = slice(None) if cols is None else cols
    ext[0:tt, cs] = cur
    ext[tt:tt + hr, cs] = _halo_after(cur[0:hr, :], next_first)
    acc = None
    for k in range(taps):
        off = 8 * (taps - 1 - k)
        ld = ext[off:off + tt, cs]
        term = w[k:k + 1, :] * ld
        acc = term if acc is None else acc + term
        if x is not None:
            acc_w[k, :, cs] += _rowsum8(ld * x)
    return acc


def _dot_exact(a, b, dims):
    return lax.dot_general(a, b, (dims, ((), ())), precision=lax.Precision.HIGHEST, preferred_element_type=F32)


def _to_tile_order(perm, wt_ref, w_scr, transpose):
    pb = perm.astype(MM_DTYPE)
    for h in range(N_HEADS_B):
        half = (_dot_nt(pb, wt_ref[h]) if transpose else _dot(pb, wt_ref[h])).astype(MM_DTYPE)
        w_scr[h] = _dot_nt(half, pb).astype(MM_DTYPE)


def _project_rows(y, w_ref):
    r = w_ref.shape[1]
    acc = _dot(y[:, 0:r], w_ref[0])
    for j in range(1, N_CHIPS):
        acc = acc + _dot(y[:, r * j:r * (j + 1)], w_ref[j])
    return acc


def _head_select(parts):
    head = lax.broadcasted_iota(jnp.int32, parts[0].shape, 1) // HEAD
    acc = parts[0]
    for h in range(1, N_HEADS_B):
        acc = jnp.where(head == h, parts[h], acc)
    return acc


def _mixer_forward(z, prm, q, yc):
    _, lng, lnb, wm, bias_p, _, _, clg, clb = prm
    bg = z[:, 0:D_A]
    ya = bg * q
    o_b = 3 * D_A
    zu = z[:, o_b:o_b + D_B]
    zv = z[:, o_b + D_B:o_b + 2 * D_B]
    u = _gelu(zu)
    vh, rv = _ln_fwd(_gelu(zv))
    vnb = (vh * lng + lnb).astype(MM_DTYPE)
    s = _head_select([_dot(wm[h], vnb) for h in range(N_HEADS_B)]) + bias_p
    yb = u * s
    yh, rc = _ln_fwd(yc)
    l = yh * clg + clb
    sl = jax.nn.sigmoid(l)
    return dict(bg=bg, q=q, ya=ya, zu=zu, zv=zv, u=u, vh=vh, rv=rv, vnb=vnb, s=s, yb=yb, yh=yh, rc=rc, l=l, sl=sl,
                yo=l * sl)


def _conv_inputs(z):
    o_c = 3 * D_A + 2 * D_B
    a = z[:, o_c:o_c + D_C]
    sg = jax.nn.sigmoid(z[:, o_c + D_C:o_c + 2 * D_C])
    return z[:, D_A:2 * D_A] * z[:, 2 * D_A:3 * D_A], a * sg, a, sg


def _group_norm(f, gg):
    ya, yb, yo = f["ya"], f["yb"], f["yo"]
    ra, rb, ro = _rstd(ya), _rstd(yb), _rstd(yo)
    yn = jnp.concatenate([ya * ra * gg[:, 0:D_A], yb * rb * gg[:, D_A:D_A + D_B], yo * ro * gg[:, D_A + D_B:]], axis=1)
    return yn, (ra, rb, ro)


def _mixer_prm(refs, wp_scr, bias_scr):
    caw_ref, lng_ref, lnb_ref, _, _, ccw_ref, ccb_ref, clg_ref, clb_ref = refs
    wm = [wp_scr[h] for h in range(N_HEADS_B)]
    return (caw_ref[...], lng_ref[...], lnb_ref[...], wm, bias_scr[...], ccw_ref[...], ccb_ref[...], clg_ref[...],
            clb_ref[...])


def _mixer_param_specs(tt):
    return [_const_spec((8, D_A)), _const_spec((1, D_B)), _const_spec((1, D_B)), _const_spec((N_HEADS_B, tt, tt)),
            _const_spec((tt, D_B)), _const_spec((32, D_C)), _const_spec((1, D_C)), _const_spec((1, D_C)),
            _const_spec((1, D_C))]


HR_A = 8 * (K_A - 1)
HR_C = 8 * (K_C - 1)
HR_F = 8 * (K_F - 1)


def mixer_fwd(z, x, mp, perm, grp_g, wog, post_g, tt, rider=None):
    t = z.shape[0]
    assert t % tt == 0 and tt % CHUNK == 0 and tt >= HR_C, (t, tt)

    def body(z_ref, x_ref, *rest):
        prm_refs = rest[:9]
        (perm_ref, gg_ref, wo_ref, pg_ref, o_ref, x1_ref, cv_ref, pa_ext, yg_ext, pa_last, yg_last, wp_scr, bias_scr) = rest[9:]
        i = pl.program_id(0)

        @pl.when(i == 0)
        def _():
            pa_last[...] = jnp.zeros_like(pa_last)
            yg_last[...] = jnp.zeros_like(yg_last)
            _to_tile_order(perm_ref[...], prm_refs[3], wp_scr, False)
            bias_scr[...] = _dot_exact(perm_ref[...], prm_refs[4][...], ((1,), (0,)))

        zv = z_ref[...]
        prm = _mixer_prm(prm_refs, wp_scr, bias_scr)
        pa, yg, _, _ = _conv_inputs(zv)
        q = _conv_causal(pa_ext, pa, pa_last[...], prm[0], K_A, tt)
        yc = _conv_causal(yg_ext, yg, yg_last[...], prm[5], K_C, tt) + prm[6]
        pa_last[...] = pa[tt - HR_A:, :]
        yg_last[...] = yg[tt - HR_C:, :]
        cv_ref[:, 0:D_A] = q
        cv_ref[:, D_A:] = yc
        f = _mixer_forward(zv, prm, q, yc)
        yn, _ = _group_norm(f, gg_ref[...])
        o = _project_rows(yn.astype(MM_DTYPE), wo_ref)
        o_ref[...] = o
        x1_ref[...] = x_ref[...] + o * _rstd(o) * pg_ref[...]

    row = lambda c: pl.BlockSpec((tt, c), lambda i: (i, 0))
    return _pallas(
        body, rider, name="mixer_fwd", steps=t // tt,
        in_specs=[row(D_IN), row(D_MODEL)] + _mixer_param_specs(tt)
        + [_const_spec((tt, tt)), _const_spec((1, D_MODEL)), _weight_spec(wog), _const_spec((1, D_MODEL))],
        out_specs=[row(D_MODEL), row(D_MODEL), row(D_A + D_C)],
        out_shape=[jax.ShapeDtypeStruct((t, D_MODEL), F32), jax.ShapeDtypeStruct((t, D_MODEL), F32),
                   jax.ShapeDtypeStruct((t, D_A + D_C), F32)],
        scratch_shapes=[pltpu.VMEM((HR_A + tt, D_A), F32), pltpu.VMEM((HR_C + tt, D_C), F32),
                        pltpu.VMEM((HR_A, D_A), F32), pltpu.VMEM((HR_C, D_C), F32),
                        pltpu.VMEM((N_HEADS_B, tt, tt), MM_DTYPE), pltpu.VMEM((tt, D_B), F32)],
        args=(z, x, *mp, perm, grp_g, wog, post_g))


def mixer_bwd(dx1, o, z, cv, mp, perm, grp_g, wog, post_g, tt, rider=None):
    t = z.shape[0]
    assert t % tt == 0 and tt % CHUNK == 0 and tt >= HR_C, (t, tt)
    steps = t // tt

    def body(dx1_ref, o_ref, z_ref, cv_ref, *rest):
        prm_refs = rest[:9]
        (perm_ref, gg_ref, wo_ref, pg_ref,
         dz_ref, do_ref, yn_ref, dpg_ref, dgg_ref, dcaw_ref, dlng_ref, dlnb_ref, dwm_ref, dbias_ref, dccw_ref, dccb_ref,
         dclg_ref, dclb_ref,
         dq_ext, dyc_ext, dq_first, dyc_first, a_pg, a_gg, a_caw, a_lng, a_lnb, a_ccw, a_ccb, a_clg, a_clb,
         wp_scr, wpt_scr, bias_scr, a_wm, a_bias) = rest[9:]
        i = pl.program_id(0)
        small = (a_pg, a_gg, a_caw, a_lng, a_lnb, a_ccw, a_ccb, a_clg, a_clb)

        @pl.when(i == 0)
        def _():
            for ref in small + (a_wm, a_bias, dq_first, dyc_first):
                ref[...] = jnp.zeros_like(ref)
            _to_tile_order(perm_ref[...], prm_refs[3], wp_scr, False)
            _to_tile_order(perm_ref[...], prm_refs[3], wpt_scr, True)
            bias_scr[...] = _dot_exact(perm_ref[...], prm_refs[4][...], ((1,), (0,)))

        prm = _mixer_prm(prm_refs, wp_scr, bias_scr)
        caw, lng, lnb, wm, bias_p, ccw, ccb, clg, clb = prm

        zv = z_ref[...]
        pa, yg, a, sg = _conv_inputs(zv)
        f = _mixer_forward(zv, prm, cv_ref[:, 0:D_A], cv_ref[:, D_A:])
        gg = gg_ref[...]
        yn, (ra, rb, ro) = _group_norm(f, gg)
        yn_ref[...] = yn.astype(MM_DTYPE)

        ov = o_ref[...]
        dx1v = dx1_ref[...]
        r_o = _rstd(ov)
        pg = pg_ref[...]
        a_pg[...] += _rowsum8(dx1v * ov * r_o)
        do = _rms_bwd(ov, r_o, pg, dx1v).astype(MM_DTYPE)
        do_ref[...] = do
        dyn = jnp.concatenate([_dot_nt(do, wo_ref[j]) for j in range(N_CHIPS)], axis=1)

        dyn_a, dyn_b, dyn_c = dyn[:, 0:D_A], dyn[:, D_A:D_A + D_B], dyn[:, D_A + D_B:]
        ga, gb, gc = gg[:, 0:D_A], gg[:, D_A:D_A + D_B], gg[:, D_A + D_B:]
        a_gg[...] += _rowsum8(jnp.concatenate([dyn_a * f["ya"] * ra, dyn_b * f["yb"] * rb, dyn_c * f["yo"] * ro], axis=1))
        dya = _rms_bwd(f["ya"], ra, ga, dyn_a)
        dyb = _rms_bwd(f["yb"], rb, gb, dyn_b)
        dyo = _rms_bwd(f["yo"], ro, gc, dyn_c)

        dbg = dya * f["q"]
        dq = dya * f["bg"]
        dp = _conv_anticausal(dq_ext, dq, dq_first[...], caw, K_A, tt, x=pa, acc_w=a_caw)
        dq_first[...] = dq[0:HR_A, :]
        dcg = dp * zv[:, 2 * D_A:3 * D_A]
        dxa = dp * zv[:, D_A:2 * D_A]

        du = dyb * f["s"]
        ds = dyb * f["u"]
        dsb = ds.astype(MM_DTYPE)
        head = lax.broadcasted_iota(jnp.int32, (tt, D_B), 1) // HEAD
        a_bias[...] += ds
        parts = []
        for h in range(N_HEADS_B):
            a_wm[h] += _dot_nt(jnp.where(head == h, dsb, jnp.zeros_like(dsb)), f["vnb"])
            parts.append(_dot(wpt_scr[h], dsb))
        dvn = _head_select(parts)
        a_lng[...] += _rowsum8(dvn * f["vh"])
        a_lnb[...] += _rowsum8(dvn)
        dv = _ln_bwd(f["vh"], f["rv"], dvn * lng)
        dzu = du * _gelu_grad(f["zu"])
        dzv = dv * _gelu_grad(f["zv"])

        l, sl = f["l"], f["sl"]
        dl = dyo * (sl * (1.0 + l * (1.0 - sl)))
        a_clg[...] += _rowsum8(dl * f["yh"])
        a_clb[...] += _rowsum8(dl)
        dyc = _ln_bwd(f["yh"], f["rc"], dl * clg)
        a_ccb[...] += _rowsum8(dyc)
        dy = _conv_anticausal(dyc_ext, dyc, dyc_first[...], ccw, K_C, tt, x=yg, acc_w=a_ccw)
        dyc_first[...] = dyc[0:HR_C, :]
        da = dy * sg
        dg = dy * a * sg * (1.0 - sg)

        dz_ref[...] = jnp.concatenate([dbg, dcg, dxa, dzu, dzv, da, dg], axis=1).astype(MM_DTYPE)

        @pl.when(i == steps - 1)
        def _():
            red = lambda ref: jnp.sum(ref[...], axis=0, keepdims=True)
            dpg_ref[...] = red(a_pg)
            dgg_ref[...] = red(a_gg)
            dlng_ref[...] = red(a_lng)
            dlnb_ref[...] = red(a_lnb)
            dccb_ref[...] = red(a_ccb)
            dclg_ref[...] = red(a_clg)
            dclb_ref[...] = red(a_clb)
            dcaw_ref[...] = jnp.sum(a_caw[...], axis=1)
            dccw_ref[...] = jnp.sum(a_ccw[...], axis=1)
            pm = perm_ref[...]
            tril = lax.broadcasted_iota(jnp.int32, (CHUNK, CHUNK), 0) >= lax.broadcasted_iota(jnp.int32, (CHUNK, CHUNK), 1)
            for h in range(N_HEADS_B):
                dwt = _dot_exact(pm, _dot_exact(a_wm[h], pm, ((1,), (0,))), ((0,), (0,)))
                dw = dwt[0:CHUNK, 0:CHUNK]
                for c in range(1, tt // CHUNK):
                    dw = dw + dwt[c * CHUNK:(c + 1) * CHUNK, c * CHUNK:(c + 1) * CHUNK]
                dwm_ref[h] = jnp.where(tril, dw, 0.0)
            dbt = _dot_exact(pm, a_bias[...], ((0,), (0,)))
            db = dbt[0:CHUNK, :]
            for c in range(1, tt // CHUNK):
                db = db + dbt[c * CHUNK:(c + 1) * CHUNK, :]
            dbias_ref[...] = db

    rev = lambda c: pl.BlockSpec((tt, c), lambda i: (steps - 1 - i, 0))
    full = lambda shape: pl.BlockSpec(shape, lambda i: (0,) * len(shape))
    sds = jax.ShapeDtypeStruct
    return _pallas(
        body, rider, name="mixer_bwd", steps=steps,
        in_specs=[rev(D_MODEL), rev(D_MODEL), rev(D_IN), rev(D_A + D_C)] + _mixer_param_specs(tt)
        + [_const_spec((tt, tt)), _const_spec((1, D_MODEL)), _weight_spec(wog), _const_spec((1, D_MODEL))],
        out_specs=[rev(D_IN), rev(D_MODEL), rev(D_MODEL), full((1, D_MODEL)), full((1, D_MODEL)), full((8, D_A)),
                   full((1, D_B)), full((1, D_B)), full((N_HEADS_B, CHUNK, CHUNK)), full((CHUNK, D_B)), full((32, D_C)),
                   full((1, D_C)), full((1, D_C)), full((1, D_C))],
        out_shape=[sds((t, D_IN), MM_DTYPE), sds((t, D_MODEL), MM_DTYPE), sds((t, D_MODEL), MM_DTYPE),
                   sds((1, D_MODEL), F32), sds((1, D_MODEL), F32), sds((8, D_A), F32), sds((1, D_B), F32), sds((1, D_B), F32),
                   sds((N_HEADS_B, CHUNK, CHUNK), F32), sds((CHUNK, D_B), F32), sds((32, D_C), F32), sds((1, D_C), F32),
                   sds((1, D_C), F32), sds((1, D_C), F32)],
        scratch_shapes=[pltpu.VMEM((tt + HR_A, D_A), F32), pltpu.VMEM((tt + HR_C, D_C), F32),
                        pltpu.VMEM((HR_A, D_A), F32), pltpu.VMEM((HR_C, D_C), F32),
                        pltpu.VMEM((8, D_MODEL), F32), pltpu.VMEM((8, D_MODEL), F32), pltpu.VMEM((8, 8, D_A), F32),
                        pltpu.VMEM((8, D_B), F32), pltpu.VMEM((8, D_B), F32), pltpu.VMEM((32, 8, D_C), F32),
                        pltpu.VMEM((8, D_C), F32), pltpu.VMEM((8, D_C), F32), pltpu.VMEM((8, D_C), F32),
                        pltpu.VMEM((N_HEADS_B, tt, tt), MM_DTYPE), pltpu.VMEM((N_HEADS_B, tt, tt), MM_DTYPE),
                        pltpu.VMEM((tt, D_B), F32), pltpu.VMEM((N_HEADS_B, tt, tt), F32), pltpu.VMEM((tt, D_B), F32)],
        args=(dx1, o, z, cv, *mp, perm, grp_g, wog, post_g))


def _fetch_row_blocks(wg_ref, w_scr, sems):
    r = wg_ref.shape[1]
    copies = [pltpu.make_async_copy(wg_ref.at[j], w_scr.at[pl.ds(r * j, r), :], sems.at[j]) for j in range(N_CHIPS)]
    for cp in copies:
        cp.start()
    for cp in copies:
        cp.wait()


def _ffn_conv(ext, cw, c0, cn, tt):
    acc = cw[0:1, c0:c0 + cn] * ext[0:tt, c0:c0 + cn]
    for k in range(1, K_F):
        acc = acc + cw[k:k + 1, c0:c0 + cn] * ext[8 * k:8 * k + tt, c0:c0 + cn]
    return acc


def ffn_fwd(up0, x1, cw, wdg, post_g, tt, rider=None):
    t = up0.shape[0]
    assert t % tt == 0, (t, tt)
    cn = _col_chunk(D_FF)

    def body(up0_ref, x1_ref, cw_ref, wdg_ref, pg_ref, d_ref, x2_ref, ext, last, wd_ref, sems):
        i = pl.program_id(0)

        @pl.when(i == 0)
        def _():
            _fetch_row_blocks(wdg_ref, wd_ref, sems)
            last[...] = jnp.zeros_like(last)

        ext[HR_F:HR_F + tt, :] = up0_ref[...]
        ext[0:HR_F, :] = _halo_before(up0_ref[tt - HR_F:, :], last[...])
        last[...] = up0_ref[tt - HR_F:, :]
        cwv = cw_ref[...]
        d = jnp.zeros((tt, D_MODEL), F32)
        for c0 in range(0, D_FF, cn):
            gate = _ffn_conv(ext, cwv, c0, cn, tt)
            val = _ffn_conv(ext, cwv, D_FF + c0, cn, tt)
            act = (gate * jax.nn.sigmoid(gate) * val).astype(MM_DTYPE)
            d = d + _dot(act, wd_ref[c0:c0 + cn, :])
        d_ref[...] = d
        x2_ref[...] = x1_ref[...] + d * _rstd(d) * pg_ref[...]

    row = lambda c: pl.BlockSpec((tt, c), lambda i: (i, 0))
    return _pallas(
        body, rider, name="ffn_fwd", steps=t // tt,
        in_specs=[row(2 * D_FF), row(D_MODEL), _const_spec((8, 2 * D_FF)), _ANY, _const_spec((1, D_MODEL))],
        out_specs=[row(D_MODEL), row(D_MODEL)],
        out_shape=[jax.ShapeDtypeStruct((t, D_MODEL), F32), jax.ShapeDtypeStruct((t, D_MODEL), F32)],
        scratch_shapes=[pltpu.VMEM((HR_F + tt, 2 * D_FF), F32), pltpu.VMEM((HR_F, 2 * D_FF), F32),
                        pltpu.VMEM((D_FF, D_MODEL), MM_DTYPE), pltpu.SemaphoreType.DMA((N_CHIPS,))],
        args=(up0, x1, cw, wdg, post_g))


def ffn_bwd(dx2, d, up0, cw, wdg, post_g, tt, rider=None):
    t = up0.shape[0]
    assert t % tt == 0, (t, tt)
    steps = t // tt
    hb = tt // HR_F
    cn = _col_chunk(D_FF)

    def body(dx2_ref, d_ref, up0_ref, uh_ref, cw_ref, wdg_ref, pg_ref,
             dd_ref, act_ref, dup0_ref, dpg_ref, dcw_ref, ext, dup_ext, first, a_pg, a_cw, wd_ref, sems):
        i = pl.program_id(0)
        tile = steps - 1 - i

        @pl.when(i == 0)
        def _():
            _fetch_row_blocks(wdg_ref, wd_ref, sems)
            a_pg[...] = jnp.zeros_like(a_pg)
            a_cw[...] = jnp.zeros_like(a_cw)
            first[...] = jnp.zeros_like(first)

        ext[HR_F:HR_F + tt, :] = up0_ref[...]
        ext[0:HR_F, :] = _halo_before(up0_ref[tt - HR_F:, :], jnp.where(tile > 0, uh_ref[...], 0.0))
        cwv = cw_ref[...]
        dv = d_ref[...]
        dx2v = dx2_ref[...]
        r = _rstd(dv)
        a_pg[...] += _rowsum8(dx2v * dv * r)
        dd = _rms_bwd(dv, r, pg_ref[...], dx2v).astype(MM_DTYPE)
        dd_ref[...] = dd
        for c0 in range(0, D_FF, cn):
            gate = _ffn_conv(ext, cwv, c0, cn, tt)
            val = _ffn_conv(ext, cwv, D_FF + c0, cn, tt)
            sg = jax.nn.sigmoid(gate)
            sl = gate * sg
            act_ref[:, c0:c0 + cn] = (sl * val).astype(MM_DTYPE)
            da = _dot_nt(dd, wd_ref[c0:c0 + cn, :])
            dup_ext[0:tt, c0:c0 + cn] = da * val * (sg * (1.0 + gate * (1.0 - sg)))
            dup_ext[0:tt, D_FF + c0:D_FF + c0 + cn] = da * sl
        dup_ext[tt:tt + HR_F, :] = _halo_after(dup_ext[0:HR_F, :], first[...])
        first[...] = dup_ext[0:HR_F, :]
        for c0 in range(0, 2 * D_FF, cn):
            x = up0_ref[:, c0:c0 + cn]
            acc = None
            for k in range(K_F):
                off = 8 * (K_F - 1 - k)
                ld = dup_ext[off:off + tt, c0:c0 + cn]
                term = cwv[k:k + 1, c0:c0 + cn] * ld
                acc = term if acc is None else acc + term
                a_cw[k, :, c0:c0 + cn] += _rowsum8(ld * x)
            dup0_ref[:, c0:c0 + cn] = acc.astype(MM_DTYPE)

        @pl.when(i == steps - 1)
        def _():
            dpg_ref[...] = jnp.sum(a_pg[...], axis=0, keepdims=True)
            dcw_ref[...] = jnp.sum(a_cw[...], axis=1)

    rev = lambda c: pl.BlockSpec((tt, c), lambda i: (steps - 1 - i, 0))
    halo = pl.BlockSpec((HR_F, 2 * D_FF), lambda i: (jnp.maximum((steps - 1 - i) * hb - 1, 0), 0))
    full = lambda shape: pl.BlockSpec(shape, lambda i: (0,) * len(shape))
    sds = jax.ShapeDtypeStruct
    return _pallas(
        body, rider, name="ffn_bwd", steps=steps,
        in_specs=[rev(D_MODEL), rev(D_MODEL), rev(2 * D_FF), halo, _const_spec((8, 2 * D_FF)), _ANY,
                  _const_spec((1, D_MODEL))],
        out_specs=[rev(D_MODEL), rev(D_FF), rev(2 * D_FF), full((1, D_MODEL)), full((8, 2 * D_FF))],
        out_shape=[sds((t, D_MODEL), MM_DTYPE), sds((t, D_FF), MM_DTYPE), sds((t, 2 * D_FF), MM_DTYPE),
                   sds((1, D_MODEL), F32), sds((8, 2 * D_FF), F32)],
        scratch_shapes=[pltpu.VMEM((HR_F + tt, 2 * D_FF), F32), pltpu.VMEM((tt + HR_F, 2 * D_FF), F32),
                        pltpu.VMEM((HR_F, 2 * D_FF), F32), pltpu.VMEM((8, D_MODEL), F32),
                        pltpu.VMEM((8, 8, 2 * D_FF), F32), pltpu.VMEM((D_FF, D_MODEL), MM_DTYPE),
                        pltpu.SemaphoreType.DMA((N_CHIPS,))],
        args=(dx2, d, up0, up0, cw, wdg, post_g))


def loss_head(y, target, tm):
    t, d = y.shape
    assert t % tm == 0, (t, tm)
    steps = t // tm

    def body(y_ref, t_ref, dy_ref, loss_ref, acc):
        i = pl.program_id(0)

        @pl.when(i == 0)
        def _():
            acc[...] = jnp.zeros_like(acc)

        diff = y_ref[...] - t_ref[...]
        dy_ref[...] = diff * (1.0 / d)
        acc[...] += _rowsum8(diff * diff)

        @pl.when(i == steps - 1)
        def _():
            loss_ref[...] = (0.5 / d) * jnp.sum(jnp.sum(acc[...], axis=0, keepdims=True), axis=1, keepdims=True)

    row = pl.BlockSpec((tm, d), lambda i: (i, 0))
    return pl.pallas_call(
        body, name="loss_head", grid=(steps,), in_specs=[row, row],
        out_specs=[row, pl.BlockSpec((1, 1), lambda i: (0, 0))],
        out_shape=[jax.ShapeDtypeStruct((t, d), F32), jax.ShapeDtypeStruct((1, 1), F32)],
        scratch_shapes=[pltpu.VMEM((8, d), F32)],
        compiler_params=_params(("arbitrary",)),
    )(y, target)


def adamw(w, g, m, v):
    shape = w.shape
    cols = shape[-1]
    rows = w.size // cols
    tr = next((r for r in (512, 256, 128) if rows % r == 0 and rows > r), rows)
    c1 = 1.0 - ADAM_B1 ** ADAM_STEP
    c2 = 1.0 - ADAM_B2 ** ADAM_STEP

    def body(w_ref, g_ref, m_ref, v_ref, d_ref, nm_ref, nv_ref):
        gv = g_ref[...]
        nm = ADAM_B1 * m_ref[...] + (1.0 - ADAM_B1) * gv
        nv = ADAM_B2 * v_ref[...] + (1.0 - ADAM_B2) * (gv * gv)
        nm_ref[...] = nm
        nv_ref[...] = nv
        d_ref[...] = -ADAM_LR * ((nm / c1) / (jnp.sqrt(nv / c2) + ADAM_EPS) + ADAM_WD * w_ref[...])

    spec = pl.BlockSpec((tr, cols), lambda i: (i, 0))
    out = jax.ShapeDtypeStruct((rows, cols), F32)
    res = pl.pallas_call(
        body, name="adamw", grid=(rows // tr,), in_specs=[spec] * 4, out_specs=[spec] * 3, out_shape=[out] * 3,
        compiler_params=_params(("arbitrary",)),
    )(*[a.reshape(rows, cols) for a in (w, g, m, v)])
    return tuple(r.reshape(shape) for r in res)


def _place():
    return lax.axis_index("x"), lax.axis_index("y"), lax.axis_index("c")


def _other_chips(x, y):
    return [(1 - x, y, 2 * (1 - x) + y), (x, 1 - y, 2 * x + 1 - y), (1 - x, 1 - y, 2 * (1 - x) + 1 - y)]


def _sem_specs(*counts):
    return [pltpu.SemaphoreType.DMA((n,)) for n in counts]


def cast_shard(w, layer, chip):
    _, r, c = w.shape

    def body(chip_ref, w_ref, o_ref):
        del chip_ref
        o_ref[...] = w_ref[...].astype(MM_DTYPE)

    grid_spec = pltpu.PrefetchScalarGridSpec(
        num_scalar_prefetch=1, grid=(1,), in_specs=[pl.BlockSpec((None, r, c), lambda i, chip_ref: (layer, 0, 0))],
        out_specs=pl.BlockSpec((None, r, c), lambda i, chip_ref: (chip_ref[0], 0, 0)))
    return pl.pallas_call(
        body, name="cast_shard", grid_spec=grid_spec, out_shape=jax.ShapeDtypeStruct((N_CHIPS, r, c), MM_DTYPE),
        compiler_params=_params(("arbitrary",)),
    )(jnp.reshape(chip, (1,)).astype(jnp.int32), w)


def _row_half(buf, chip, mine, c):
    rh = buf.shape[1] // 2
    return buf.at[chip, pl.ds(pl.multiple_of((c if mine else 1 - c) * rh, 16), rh), :]


def spread_rider(bufs):
    n = len(bufs)

    def start(rin, rout, sems):
        x, y, c = _place()
        me = 2 * x + y
        for k, (px, py, _) in enumerate(_other_chips(x, y)):
            for i, buf in enumerate(rout):
                part = _row_half(buf, me, True, c)
                pltpu.make_async_remote_copy(
                    src_ref=part, dst_ref=part, send_sem=sems[0].at[n * k + i], recv_sem=sems[1].at[n * k + i],
                    device_id=(px, py, c), device_id_type=MESH_ID).start()

    def wait(rin, rout, sems):
        x, y, c = _place()
        for k, (_, _, pj) in enumerate(_other_chips(x, y)):
            for i, buf in enumerate(rout):
                part = _row_half(buf, pj, True, c)
                pltpu.make_async_remote_copy(
                    src_ref=part, dst_ref=part, send_sem=sems[0].at[n * k + i], recv_sem=sems[1].at[n * k + i],
                    device_id=(x, y, c), device_id_type=MESH_ID).wait()

    shapes = [jax.ShapeDtypeStruct(b.shape, b.dtype) for b in bufs]
    return Rider("spread", list(bufs), shapes, {i: i for i in range(n)}, (3 * n, 3 * n), start, wait)


def pass_rider(bufs):
    n = len(bufs)

    def start(rin, rout, sems):
        x, y, c = _place()
        for k, (_, _, pj) in enumerate(_other_chips(x, y)):
            for i, buf in enumerate(rout):
                part = _row_half(buf, pj, True, c)
                pltpu.make_async_remote_copy(
                    src_ref=part, dst_ref=part, send_sem=sems[0].at[n * k + i], recv_sem=sems[1].at[n * k + i],
                    device_id=(x, y, 1 - c), device_id_type=MESH_ID).start()

    def wait(rin, rout, sems):
        x, y, c = _place()
        for k, (_, _, pj) in enumerate(_other_chips(x, y)):
            for i, buf in enumerate(rout):
                part = _row_half(buf, pj, False, c)
                pltpu.make_async_remote_copy(
                    src_ref=part, dst_ref=part, send_sem=sems[0].at[n * k + i], recv_sem=sems[1].at[n * k + i],
                    device_id=(x, y, 1 - c), device_id_type=MESH_ID).wait()

    shapes = [jax.ShapeDtypeStruct(b.shape, b.dtype) for b in bufs]
    return Rider("pass", list(bufs), shapes, {i: i for i in range(n)}, (3 * n, 3 * n), start, wait)


def both_riders(a, b):
    na, oa, sa = len(a.inputs), len(a.out_shapes), len(a.sems)

    def start(rin, rout, sems):
        a.start(rin[:na], rout[:oa], sems[:sa])
        b.start(rin[na:], rout[oa:], sems[sa:])

    def wait(rin, rout, sems):
        a.wait(rin[:na], rout[:oa], sems[:sa])
        b.wait(rin[na:], rout[oa:], sems[sa:])

    aliases = dict(a.aliases)
    aliases.update({na + i: oa + o for i, o in b.aliases.items()})
    return Rider(a.name + "_" + b.name, a.inputs + b.inputs, a.out_shapes + b.out_shapes, aliases, a.sems + b.sems,
                 start, wait)


def gather_small(small):
    def body(small_ref, out_ref, send, recv, local):
        x, y, c = _place()
        me = 2 * x + y
        chips = _other_chips(x, y)
        own = pltpu.make_async_copy(small_ref, out_ref.at[me], local.at[0])
        own.start()
        sends = [pltpu.make_async_remote_copy(src_ref=small_ref, dst_ref=out_ref.at[me], send_sem=send.at[k],
                                              recv_sem=recv.at[k], device_id=(px, py, c), device_id_type=MESH_ID)
                 for k, (px, py, _) in enumerate(chips)]
        for cp in sends:
            cp.start()
        for k, (_, _, pj) in enumerate(chips):
            pltpu.make_async_remote_copy(src_ref=small_ref, dst_ref=out_ref.at[pj], send_sem=send.at[k], recv_sem=recv.at[k],
                                         device_id=(x, y, c), device_id_type=MESH_ID).wait_recv()
        for cp in sends:
            cp.wait_send()
        own.wait()

    return pl.pallas_call(
        body, name="gather_small", in_specs=[_ANY], out_specs=_ANY,
        out_shape=jax.ShapeDtypeStruct((N_CHIPS,) + small.shape, small.dtype), scratch_shapes=_sem_specs(3, 3, 1),
        compiler_params=pltpu.CompilerParams(has_side_effects=True),
    )(small)


def swap_rider(gs):
    n = len(gs)

    def copies(rin, rout, sems):
        x, y, c = _place()
        out = []
        for i, (g, got) in enumerate(zip(rin, rout)):
            rh = g.shape[1] // 2
            theirs = pl.ds(pl.multiple_of((1 - c) * rh, 8), rh)
            out.append(pltpu.make_async_remote_copy(
                src_ref=g.at[:, theirs, :], dst_ref=got, send_sem=sems[0].at[i], recv_sem=sems[1].at[i],
                device_id=(x, y, 1 - c), device_id_type=MESH_ID))
        return out

    def start(rin, rout, sems):
        for cp in copies(rin, rout, sems):
            cp.start()

    def wait(rin, rout, sems):
        for cp in copies(rin, rout, sems):
            cp.wait()

    shapes = [jax.ShapeDtypeStruct((g.shape[0], g.shape[1] // 2, g.shape[2]), g.dtype) for g in gs]
    return Rider("swap", list(gs), shapes, {}, (n, n), start, wait)


def scatter_rider(sbs):
    n = len(sbs)

    def start(rin, rout, sems):
        x, y, c = _place()
        me = 2 * x + y
        for k, (px, py, pj) in enumerate(_other_chips(x, y)):
            for i, (sb, got) in enumerate(zip(rin, rout)):
                pltpu.make_async_remote_copy(
                    src_ref=sb.at[pj], dst_ref=got.at[me], send_sem=sems[0].at[n * k + i], recv_sem=sems[1].at[n * k + i],
                    device_id=(px, py, c), device_id_type=MESH_ID).start()

    def wait(rin, rout, sems):
        x, y, c = _place()
        for k, (_, _, pj) in enumerate(_other_chips(x, y)):
            for i, (sb, got) in enumerate(zip(rin, rout)):
                cp = pltpu.make_async_remote_copy(
                    src_ref=sb.at[pj], dst_ref=got.at[pj], send_sem=sems[0].at[n * k + i], recv_sem=sems[1].at[n * k + i],
                    device_id=(x, y, c), device_id_type=MESH_ID)
                cp.wait_recv()
                cp.wait_send()

    shapes = [jax.ShapeDtypeStruct(sb.shape, sb.dtype) for sb in sbs]
    return Rider("scatter", list(sbs), shapes, {}, (3 * n, 3 * n), start, wait)


def join_rider(fs, layers):
    n = len(fs)

    def half(i, f, mine, place):
        x, y, c = place
        rh = f.shape[1] // 2
        block = 2 * x + y if layers[i] is None else layers[i]
        return f.at[block, pl.ds(pl.multiple_of((c if mine else 1 - c) * rh, 8), rh), :]

    def start(rin, rout, sems):
        x, y, c = _place()
        for i, f in enumerate(rout):
            part = half(i, f, True, (x, y, c))
            pltpu.make_async_remote_copy(
                src_ref=part, dst_ref=part, send_sem=sems[0].at[i], recv_sem=sems[1].at[i],
                device_id=(x, y, 1 - c), device_id_type=MESH_ID).start()

    def wait(rin, rout, sems):
        x, y, c = _place()
        for i, f in enumerate(rout):
            part = half(i, f, False, (x, y, c))
            pltpu.make_async_remote_copy(
                src_ref=part, dst_ref=part, send_sem=sems[0].at[i], recv_sem=sems[1].at[i],
                device_id=(x, y, 1 - c), device_id_type=MESH_ID).wait()

    shapes = [jax.ShapeDtypeStruct(f.shape, f.dtype) for f in fs]
    return Rider("join", list(fs), shapes, {i: i for i in range(n)}, (n, n), start, wait)


def add_halves(g, got, wire=BF16):
    n, rh, cols = got.shape
    x, y, c = _place()

    def body(p_ref, g_ref, got_ref, sw_ref, sme_ref):
        s = g_ref[...] + got_ref[...]
        sw_ref[...] = s.astype(wire)

        @pl.when(pl.program_id(0) == p_ref[0])
        def _():
            sme_ref[...] = s

    blk = (None, rh, cols)
    grid_spec = pltpu.PrefetchScalarGridSpec(
        num_scalar_prefetch=1, grid=(n,),
        in_specs=[pl.BlockSpec(blk, lambda j, p_ref: (j, p_ref[1], 0)), pl.BlockSpec(blk, lambda j, p_ref: (j, 0, 0))],
        out_specs=[pl.BlockSpec(blk, lambda j, p_ref: (j, 0, 0)), pl.BlockSpec((rh, cols), lambda j, p_ref: (0, 0))])
    wired, own = pl.pallas_call(
        body, name="add_halves", grid_spec=grid_spec,
        out_shape=[jax.ShapeDtypeStruct(got.shape, wire), jax.ShapeDtypeStruct((rh, cols), F32)],
        compiler_params=_params(("arbitrary",)),
    )(jnp.stack([2 * x + y, c]).astype(jnp.int32), g, got)
    return own, wired


def add_chips(own, got, fbuf, block=None):
    n, rh, cols = got.shape
    x, y, c = _place()
    me = 2 * x + y

    def body(p_ref, s_ref, g1_ref, g2_ref, g3_ref, f_ref, o_ref):
        del p_ref, f_ref
        o_ref[...] = s_ref[...] + g1_ref[...].astype(F32) + g2_ref[...].astype(F32) + g3_ref[...].astype(F32)

    blk = (None, rh, cols)

    def other(k):
        return pl.BlockSpec(blk, lambda i, p_ref: ((p_ref[0] + k) % n, 0, 0))

    grid_spec = pltpu.PrefetchScalarGridSpec(
        num_scalar_prefetch=1, grid=(1,),
        in_specs=[pl.BlockSpec((rh, cols), lambda i, p_ref: (0, 0)), other(1), other(2), other(3), _ANY],
        out_specs=pl.BlockSpec(blk, lambda i, p_ref: (p_ref[2], p_ref[1], 0)))
    return pl.pallas_call(
        body, name="add_chips", grid_spec=grid_spec, out_shape=jax.ShapeDtypeStruct(fbuf.shape, F32),
        input_output_aliases={5: 0}, compiler_params=_params(("arbitrary",)),
    )(jnp.stack([me, c, me if block is None else block]).astype(jnp.int32), own, got, got, got, fbuf)


def _pack(arrays, rows):
    flat = jnp.concatenate([a.reshape(-1) for a in arrays])
    return jnp.pad(flat, (0, rows * LANES - flat.size)).reshape(rows, LANES)


def _unpack(buf, shapes):
    flat = buf.reshape(-1)
    out, at = [], 0
    for s in shapes:
        n = math.prod(s)
        out.append(flat[at:at + n].reshape(s))
        at += n
    return out


CONV_SHARDS = [(DEPTH, K_A, D_A // N_CHIPS), (DEPTH, K_C, D_C // N_CHIPS), (DEPTH, K_F, 2 * D_FF // N_CHIPS)]
CONV_ROWS = 32
SMALL_ROWS = 640


def _join_cols(g):
    n, l, r, c = g.shape
    return jnp.transpose(g, (1, 2, 0, 3)).reshape(l, r, n * c)


BIG = ["w_in", "w_out", "w_up", "w_down"]
TILE_MM = 512
TILE_EW = 256


def _pad_rows(a, rows):
    return jnp.pad(a, ((0, rows - a.shape[0]), (0, 0)))


def _row(a):
    return a.reshape(1, -1)


def _tile_perm(tt):
    p = lax.broadcasted_iota(jnp.int32, (tt, tt), 0)
    tok = lax.broadcasted_iota(jnp.int32, (tt, tt), 1)
    return ((tt // 8) * (p % 8) + p // 8 == tok).astype(F32)


def _layer_params(wl, tt):
    n = tt // CHUNK
    tril = jnp.tril(jnp.ones((CHUNK, CHUNK), bool))
    wm = jnp.where(tril[None], wl["sgu_w"], 0.0)
    eye = jnp.eye(n, dtype=F32)
    wt = (eye[None, :, None, :, None] * wm[:, None, :, None, :]).reshape(N_HEADS_B, tt, tt)
    bias_e = jnp.repeat(wl["sgu_b"].T, HEAD, axis=1)
    return (_pad_rows(wl["conv_a_w"], 8), _row(wl["sgu_ln_g"]), _row(wl["sgu_ln_b"]), wt.astype(MM_DTYPE),
            jnp.tile(bias_e, (n, 1)), _pad_rows(wl["conv_c_w"], 32), _row(wl["conv_c_b"]), _row(wl["conv_ln_g"]),
            _row(wl["conv_ln_b"]))


def layer_fwd(x, wl, gw, nxt=None, tm=TILE_MM, tt=TILE_EW):
    mp = _layer_params(wl, tt)
    ride = pass_rider([gw["w_down"]]) if gw.get("pass_down") else None
    (z, h), done = norm_matmul(x, _row(wl["pre_mix_g"]), gw["w_in"], tm, rider=ride)
    gw = {n: (done[0] if ride and n == "w_down" else gw[n]) for n in BIG}
    ride = spread_rider([nxt["w_in"], nxt["w_out"]]) if nxt else None
    (o, x1, cv), done = mixer_fwd(z, x, mp, _tile_perm(tt), _row(wl["grp_norm_g"]), gw["w_out"], _row(wl["post_mix_g"]), tt,
                                  rider=ride)
    ride = both_riders(spread_rider([nxt["w_up"]]), pass_rider(list(done))) if nxt else None
    (up0, h2), done = norm_matmul(x1, _row(wl["pre_ffn_g"]), gw["w_up"], tt, rider=ride)
    if nxt:
        nxt = dict(nxt, w_up=done[0], w_in=done[1], w_out=done[2])
        ride = both_riders(spread_rider([nxt["w_down"]]), pass_rider([nxt["w_up"]]))
    (d, x2), done = ffn_fwd(up0, x1, _pad_rows(wl["ffn_conv_w"], 8), gw["w_down"], _row(wl["post_ffn_g"]), tt, rider=ride)
    if nxt:
        nxt = dict(nxt, w_down=done[0], w_up=done[1], pass_down=True)
    return x2, dict(x=x, z=z, h=h, o=o, x1=x1, up0=up0, h2=h2, d=d, cv=cv, gw=gw), nxt


def layer_bwd(dx2, wl, layer, sv, pend=None, exchange=True, tm=TILE_MM, tt=TILE_EW):
    mp = _layer_params(wl, tt)
    gw = sv["gw"]
    tk = min(512, dx2.shape[0])
    g = {}
    wide, narrow = [BIG.index("w_up"), BIG.index("w_down")], [BIG.index("w_in"), BIG.index("w_out")]
    ride = scatter_rider([pend["sums"][i][1] for i in wide]) if pend else None
    (dd, act, dup0, dpg, dcw), arrived = ffn_bwd(dx2, sv["d"], sv["up0"], _pad_rows(wl["ffn_conv_w"], 8), gw["w_down"],
                                                 _row(wl["post_ffn_g"]), tt, rider=ride)
    fbuf = list(pend["fbuf"]) if pend else None
    if pend:
        for i, got in zip(wide, arrived):
            fbuf[i] = add_chips(pend["sums"][i][0], got, fbuf[i], pend["layer"])
    g["post_ffn_g"] = dpg[0]
    g["ffn_conv_w"] = dcw[:K_F]
    gl = {}
    gl["w_down"] = matmul_tn_down(act, dd, tk)
    gl["w_up"] = matmul_tn_blocks(sv["h2"], dup0, D_MODEL, 2 * D_FF // N_CHIPS, tk, by_rows=False)
    ride = scatter_rider([pend["sums"][i][1] for i in narrow]) if pend else None
    (dx1, dg), arrived = matmul_nt_norm_bwd(dup0, gw["w_up"], sv["x1"], _row(wl["pre_ffn_g"]), dx2, tm, rider=ride)
    if pend:
        for i, got in zip(narrow, arrived):
            fbuf[i] = add_chips(pend["sums"][i][0], got, fbuf[i], pend["layer"])
    g["pre_ffn_g"] = dg[0]
    ride = swap_rider([gl["w_up"], gl["w_down"]]) if exchange else None
    if pend:
        ride = both_riders(join_rider(fbuf, [pend["layer"]] * len(fbuf)), ride)
    (dz, do, yn, dpg, dgg, dcaw, dlng, dlnb, dwm, dbias, dccw, dccb, dclg, dclb), rode = mixer_bwd(
        dx1, sv["o"], sv["z"], sv["cv"], mp, _tile_perm(tt), _row(wl["grp_norm_g"]), gw["w_out"],
        _row(wl["post_mix_g"]), tt, rider=ride)
    joined, got_wide = (list(rode[:len(BIG)]), list(rode[len(BIG):])) if pend else (None, list(rode))
    g["post_mix_g"] = dpg[0]
    g["grp_norm_g"] = dgg[0]
    g["conv_a_w"] = dcaw[:K_A]
    g["sgu_ln_g"] = dlng[0]
    g["sgu_ln_b"] = dlnb[0]
    g["sgu_w"] = dwm
    g["sgu_b"] = jnp.sum(dbias.reshape(CHUNK, N_HEADS_B, HEAD), axis=2).T
    g["conv_c_w"] = dccw[:K_C]
    g["conv_c_b"] = dccb[0]
    g["conv_ln_g"] = dclg[0]
    g["conv_ln_b"] = dclb[0]
    gl["w_out"] = matmul_tn_blocks(yn, do, D_MODEL // N_CHIPS, D_MODEL, tk, by_rows=True)
    gl["w_in"] = matmul_tn_in(sv["h"], dz, tk)
    ride = swap_rider([gl["w_in"], gl["w_out"]]) if exchange else None
    (dx, dg), got_narrow = matmul_nt_norm_bwd(dz, gw["w_in"], sv["x"], _row(wl["pre_mix_g"]), dx1, tm, rider=ride)
    g["pre_mix_g"] = dg[0]
    if not exchange:
        return dx, g, gl
    got = dict(zip(["w_in", "w_out", "w_up", "w_down"], list(got_narrow) + got_wide))
    sums = [add_halves(gl[n], got[n]) for n in BIG]
    return dx, g, dict(sums=sums, fbuf=joined if pend else grad_buffers(), layer=layer)


def grad_buffers():
    return [lax.empty(s, F32) for s in ((DEPTH, D_MODEL, D_IN // N_CHIPS), (DEPTH, D_MODEL // N_CHIPS, D_MODEL),
                                        (DEPTH, D_MODEL, 2 * D_FF // N_CHIPS), (DEPTH, D_FF // N_CHIPS, D_MODEL))]


CONV = ["conv_a_w", "conv_c_w", "ffn_conv_w"]
REPL = ["pre_mix_g", "sgu_ln_g", "sgu_ln_b", "sgu_w", "sgu_b", "conv_c_b", "conv_ln_g", "conv_ln_b", "grp_norm_g",
        "post_mix_g", "pre_ffn_g", "post_ffn_g"]
WEIGHTS = ["pre_mix_g", "w_in", "conv_a_w", "sgu_ln_g", "sgu_ln_b", "sgu_w", "sgu_b", "conv_c_w", "conv_c_b", "conv_ln_g",
           "conv_ln_b", "grp_norm_g", "w_out", "post_mix_g", "pre_ffn_g", "w_up", "ffn_conv_w", "w_down", "post_ffn_g"]


def kernel(x, pre_mix_g, w_in, conv_a_w, sgu_ln_g, sgu_ln_b, sgu_w, sgu_b, conv_c_w, conv_c_b, conv_ln_g, conv_ln_b, grp_norm_g, w_out, post_mix_g, pre_ffn_g, w_up, ffn_conv_w, w_down, post_ffn_g, loss_target, m_pre_mix_g, m_w_in, m_conv_a_w, m_sgu_ln_g, m_sgu_ln_b, m_sgu_w, m_sgu_b, m_conv_c_w, m_conv_c_b, m_conv_ln_g, m_conv_ln_b, m_grp_norm_g, m_w_out, m_post_mix_g, m_pre_ffn_g, m_w_up, m_ffn_conv_w, m_w_down, m_post_ffn_g, v_pre_mix_g, v_w_in, v_conv_a_w, v_sgu_ln_g, v_sgu_ln_b, v_sgu_w, v_sgu_b, v_conv_c_w, v_conv_c_b, v_conv_ln_g, v_conv_ln_b, v_grp_norm_g, v_w_out, v_post_mix_g, v_pre_ffn_g, v_w_up, v_ffn_conv_w, v_w_down, v_post_ffn_g):
    w = dict(pre_mix_g=pre_mix_g, w_in=w_in, conv_a_w=conv_a_w, sgu_ln_g=sgu_ln_g, sgu_ln_b=sgu_ln_b, sgu_w=sgu_w, sgu_b=sgu_b,
             conv_c_w=conv_c_w, conv_c_b=conv_c_b, conv_ln_g=conv_ln_g, conv_ln_b=conv_ln_b, grp_norm_g=grp_norm_g,
             w_out=w_out, post_mix_g=post_mix_g, pre_ffn_g=pre_ffn_g, w_up=w_up, ffn_conv_w=ffn_conv_w, w_down=w_down,
             post_ffn_g=post_ffn_g)
    m = dict(pre_mix_g=m_pre_mix_g, w_in=m_w_in, conv_a_w=m_conv_a_w, sgu_ln_g=m_sgu_ln_g, sgu_ln_b=m_sgu_ln_b,
             sgu_w=m_sgu_w, sgu_b=m_sgu_b, conv_c_w=m_conv_c_w, conv_c_b=m_conv_c_b, conv_ln_g=m_conv_ln_g,
             conv_ln_b=m_conv_ln_b, grp_norm_g=m_grp_norm_g, w_out=m_w_out, post_mix_g=m_post_mix_g,
             pre_ffn_g=m_pre_ffn_g, w_up=m_w_up, ffn_conv_w=m_ffn_conv_w, w_down=m_w_down, post_ffn_g=m_post_ffn_g)
    v = dict(pre_mix_g=v_pre_mix_g, w_in=v_w_in, conv_a_w=v_conv_a_w, sgu_ln_g=v_sgu_ln_g, sgu_ln_b=v_sgu_ln_b,
             sgu_w=v_sgu_w, sgu_b=v_sgu_b, conv_c_w=v_conv_c_w, conv_c_b=v_conv_c_b, conv_ln_g=v_conv_ln_g,
             conv_ln_b=v_conv_ln_b, grp_norm_g=v_grp_norm_g, w_out=v_w_out, post_mix_g=v_post_mix_g,
             pre_ffn_g=v_pre_ffn_g, w_up=v_w_up, ffn_conv_w=v_ffn_conv_w, w_down=v_w_down, post_ffn_g=v_post_ffn_g)
    chip = 2 * lax.axis_index("x") + lax.axis_index("y")

    convs = gather_small(_pack([w[n] for n in CONV], CONV_ROWS))
    gws = [{n: cast_shard(w[n], layer, chip) for n in BIG} for layer in range(DEPTH)]
    first = run_rider(pass_rider(run_rider(spread_rider([gws[0][n] for n in BIG]))))
    gws[0] = dict(zip(BIG, first))
    cparts = [_unpack(convs[j], CONV_SHARDS) for j in range(N_CHIPS)]
    full = dict(w)
    for i, n in enumerate(CONV):
        full[n] = _join_cols(jnp.stack([p[i] for p in cparts]))

    xc = to_tiles(x[0], TILE_EW)
    saved = []
    for layer in range(DEPTH):
        nxt = gws[layer + 1] if layer + 1 < DEPTH else None
        xc, sv, nxt = layer_fwd(xc, {n: full[n][layer] for n in REPL + CONV}, gws[layer], nxt)
        if nxt:
            gws[layer + 1] = nxt
        saved.append(sv)
    dxc, loss_part = loss_head(xc, to_tiles(loss_target[0], TILE_EW), TILE_MM)
    loss = lax.psum(loss_part[0, 0], ("x", "y", "c"))
    small = [None] * DEPTH
    pend = None
    for layer in reversed(range(DEPTH)):
        dxc, small[layer], pend = layer_bwd(dxc, {n: full[n][layer] for n in REPL + CONV}, layer, saved[layer], pend)
    grads = {n: jnp.stack([small[layer][n] for layer in range(DEPTH)]) for n in REPL + CONV}

    gsmall = _pack([grads[n] for n in REPL + CONV], SMALL_ROWS).reshape(N_CHIPS, SMALL_ROWS // N_CHIPS, LANES)
    sums = pend["sums"] + [add_halves(gsmall, run_rider(swap_rider([gsmall]))[0], wire=F32)]
    arrived = run_rider(scatter_rider([sw for _, sw in sums]))
    fbuf = pend["fbuf"] + [lax.empty(gsmall.shape, F32)]
    fbuf = [add_chips(own, got, fb, blk) for (own, _), got, fb, blk in zip(sums, arrived, fbuf, [0] * len(BIG) + [None])]
    joined = run_rider(join_rider(fbuf, [0] * len(BIG) + [None]))
    out_g = dict(zip(BIG, joined))
    tot = run_rider(pass_rider(run_rider(spread_rider([joined[len(BIG)]]))))[0].reshape(SMALL_ROWS, LANES)
    shapes = [grads[n].shape for n in REPL + CONV]
    for n, gfull in zip(REPL + CONV, _unpack(tot, shapes)):
        if n in CONV:
            width = gfull.shape[-1] // N_CHIPS
            gfull = lax.dynamic_slice_in_dim(gfull, chip * width, width, axis=2)
        out_g[n] = gfull

    deltas, new_m, new_v = {}, {}, {}
    for n in WEIGHTS:
        deltas[n], new_m[n], new_v[n] = adamw(w[n], out_g[n], m[n], v[n])
    return (loss, from_tiles(dxc, TILE_EW)[None], *[out_g[n] for n in WEIGHTS], *[deltas[n] for n in WEIGHTS], *[new_m[n] for n in WEIGHTS],
            *[new_v[n] for n in WEIGHTS])
```

```python
import math
from typing import Callable, NamedTuple

import jax
import jax.numpy as jnp
from jax import lax
from jax.experimental import pallas as pl
from jax.experimental.pallas import tpu as pltpu

F32 = jnp.float32
BF16 = jnp.bfloat16
MM_DTYPE = BF16

D_MODEL = 1024
DEPTH = 4
D_A = 256
D_B = 384
D_C = 384
D_IN = 3 * D_A + 2 * D_B + 2 * D_C
D_FF = 2816
K_A = 3
K_C = 31
K_F = 3
CHUNK = 128
HEAD = 64
N_HEADS_B = D_B // HEAD
EPS = 1e-6
N_CHIPS = 4

ADAM_LR = 0.001
ADAM_B1 = 0.9
ADAM_B2 = 0.999
ADAM_EPS = 1e-08
ADAM_WD = 0.01
ADAM_STEP = 10

LANES = 1024
VMEM_LIMIT = 56 * 1024 * 1024

MESH_ID = pl.DeviceIdType.MESH
_ANY = pl.BlockSpec(memory_space=pl.ANY)


def _params(sem=None):
    return pltpu.CompilerParams(dimension_semantics=sem, vmem_limit_bytes=VMEM_LIMIT)


def _const_spec(shape):
    nd = len(shape)
    return pl.BlockSpec(shape, lambda *_: (0,) * nd, pipeline_mode=pl.Buffered(1))


def _rowsum8(a):
    r, c = a.shape
    return jnp.sum(a.reshape(r // 8, 8, c), axis=0)


def _rstd(x):
    return lax.rsqrt(jnp.mean(x * x, axis=-1, keepdims=True) + EPS)


def _rms_bwd(x, r, g, dy):
    gdy = g * dy
    return r * gdy - x * (r * r * r) * jnp.mean(gdy * x, axis=-1, keepdims=True)


def _ln_fwd(x):
    mu = jnp.mean(x, axis=-1, keepdims=True)
    xc = x - mu
    r = lax.rsqrt(jnp.mean(xc * xc, axis=-1, keepdims=True) + EPS)
    return xc * r, r


def _ln_bwd(xh, r, dxh):
    return r * (dxh - jnp.mean(dxh, axis=-1, keepdims=True) - xh * jnp.mean(dxh * xh, axis=-1, keepdims=True))


def _gelu(x):
    return 0.5 * x * (1.0 + lax.erf(x * (1.0 / math.sqrt(2.0))))


def _gelu_grad(x):
    cdf = 0.5 * (1.0 + lax.erf(x * (1.0 / math.sqrt(2.0))))
    pdf = jnp.exp(-0.5 * x * x) * (1.0 / math.sqrt(2.0 * math.pi))
    return cdf + x * pdf


def _dot(a, b):
    return jnp.dot(a, b, preferred_element_type=F32)


def _dot_nt(a, b):
    return lax.dot_general(a, b, (((1,), (1,)), ((), ())), preferred_element_type=F32)


def _dot_tn(a, b):
    return lax.dot_general(a, b, (((0,), (0,)), ((), ())), preferred_element_type=F32)


def _col_chunk(n):
    for c in (1408, 1024, 768, 512, 256, 128):
        if n % c == 0:
            return c
    raise ValueError(n)


class Rider(NamedTuple):
    name: str
    inputs: list
    out_shapes: list
    aliases: dict
    sems: tuple
    start: Callable
    wait: Callable


def _pallas(body, rider, *, name, steps, in_specs, out_specs, out_shape, scratch_shapes, args):
    if rider is None:
        res = pl.pallas_call(body, name=name, grid=(steps,), in_specs=in_specs, out_specs=out_specs, out_shape=out_shape,
                             scratch_shapes=scratch_shapes, compiler_params=_params(("arbitrary",)))(*args)
        return res, []
    n_in, n_out, n_scr = len(in_specs), len(out_specs), len(scratch_shapes)
    r_in, r_out = len(rider.inputs), len(rider.out_shapes)

    def wrapped(*refs):
        ins, rin = refs[:n_in], refs[n_in:n_in + r_in]
        at = n_in + r_in
        outs, rout = refs[at:at + n_out], refs[at + n_out:at + n_out + r_out]
        at += n_out + r_out
        scr, rsem = refs[at:at + n_scr], refs[at + n_scr:]

        @pl.when(pl.program_id(0) == 0)
        def _():
            rider.start(rin, rout, rsem)

        body(*ins, *outs, *scr)

        @pl.when(pl.program_id(0) == steps - 1)
        def _():
            rider.wait(rin, rout, rsem)

    res = pl.pallas_call(
        wrapped, name=name + "_" + rider.name, grid=(steps,), in_specs=list(in_specs) + [_ANY] * r_in,
        out_specs=list(out_specs) + [_ANY] * r_out, out_shape=list(out_shape) + list(rider.out_shapes),
        scratch_shapes=list(scratch_shapes) + [pltpu.SemaphoreType.DMA((n,)) for n in rider.sems],
        input_output_aliases={n_in + i: n_out + o for i, o in rider.aliases.items()},
        compiler_params=pltpu.CompilerParams(dimension_semantics=("arbitrary",), vmem_limit_bytes=VMEM_LIMIT,
                                             has_side_effects=True),
    )(*args, *rider.inputs)
    return res[:n_out], res[n_out:]


def run_rider(rider):
    def body(*refs):
        r_in, r_out = len(rider.inputs), len(rider.out_shapes)
        rin, rout, rsem = refs[:r_in], refs[r_in:r_in + r_out], refs[r_in + r_out:]
        rider.start(rin, rout, rsem)
        rider.wait(rin, rout, rsem)

    return pl.pallas_call(
        body, name=rider.name, in_specs=[_ANY] * len(rider.inputs), out_specs=[_ANY] * len(rider.out_shapes),
        out_shape=list(rider.out_shapes), scratch_shapes=[pltpu.SemaphoreType.DMA((n,)) for n in rider.sems],
        input_output_aliases=dict(rider.aliases), compiler_params=pltpu.CompilerParams(has_side_effects=True),
    )(*rider.inputs)


def _weight_spec(wg):
    return _const_spec(wg.shape)


def _join_col_blocks(w_ref, w_scr):
    c = w_ref.shape[2]
    for j in range(N_CHIPS):
        w_scr[:, c * j:c * (j + 1)] = w_ref[j]


def norm_matmul(x, g, wg, tm, rider=None):
    t, d = x.shape
    assert t % tm == 0, (t, tm)
    cw = wg.shape[2]
    n = N_CHIPS * cw
    aligned = cw % 128 == 0
    cn = cw if aligned else _col_chunk(n)

    def body(x_ref, g_ref, w_ref, o_ref, h_ref, *scr):
        if not aligned:
            @pl.when(pl.program_id(0) == 0)
            def _():
                _join_col_blocks(w_ref, scr[0])

        xv = x_ref[...]
        h = (xv * _rstd(xv) * g_ref[...]).astype(MM_DTYPE)
        h_ref[...] = h
        for j, c0 in enumerate(range(0, n, cn)):
            wv = w_ref[j] if aligned else scr[0][:, c0:c0 + cn]
            o_ref[:, c0:c0 + cn] = _dot(h, wv)

    return _pallas(
        body, rider, name="norm_matmul", steps=t // tm,
        in_specs=[pl.BlockSpec((tm, d), lambda i: (i, 0)), _const_spec((1, d)), _weight_spec(wg)],
        out_specs=[pl.BlockSpec((tm, n), lambda i: (i, 0)), pl.BlockSpec((tm, d), lambda i: (i, 0))],
        out_shape=[jax.ShapeDtypeStruct((t, n), F32), jax.ShapeDtypeStruct((t, d), MM_DTYPE)],
        scratch_shapes=[] if aligned else [pltpu.VMEM((d, n), MM_DTYPE)],
        args=(x, g, wg))


def matmul_nt_norm_bwd(gy, wg, x, g, dres, tm, rider=None):
    t, n = gy.shape
    assert t % tm == 0, (t, tm)
    d, cw = wg.shape[1], wg.shape[2]
    aligned = cw % 128 == 0
    cn = cw if aligned else _col_chunk(n)
    steps = t // tm

    def body(gy_ref, w_ref, x_ref, g_ref, dres_ref, dx_ref, dg_ref, acc_ref, *scr):
        i = pl.program_id(0)

        @pl.when(i == 0)
        def _():
            acc_ref[...] = jnp.zeros_like(acc_ref)
            if not aligned:
                _join_col_blocks(w_ref, scr[0])

        dh = jnp.zeros((tm, d), F32)
        for j, c0 in enumerate(range(0, n, cn)):
            wv = w_ref[j] if aligned else scr[0][:, c0:c0 + cn]
            dh = dh + _dot_nt(gy_ref[:, c0:c0 + cn], wv)
        xv = x_ref[...]
        r = _rstd(xv)
        gv = g_ref[...]
        dx_ref[...] = dres_ref[...] + _rms_bwd(xv, r, gv, dh)
        acc_ref[...] += _rowsum8(dh * xv * r)

        @pl.when(i == steps - 1)
        def _():
            dg_ref[...] = jnp.sum(acc_ref[...], axis=0, keepdims=True)

    return _pallas(
        body, rider, name="matmul_nt_norm_bwd", steps=steps,
        in_specs=[pl.BlockSpec((tm, n), lambda i: (i, 0)), _weight_spec(wg), pl.BlockSpec((tm, d), lambda i: (i, 0)),
                  _const_spec((1, d)), pl.BlockSpec((tm, d), lambda i: (i, 0))],
        out_specs=[pl.BlockSpec((tm, d), lambda i: (i, 0)), pl.BlockSpec((1, d), lambda i: (0, 0))],
        out_shape=[jax.ShapeDtypeStruct((t, d), F32), jax.ShapeDtypeStruct((1, d), F32)],
        scratch_shapes=[pltpu.VMEM((8, d), F32)] + ([] if aligned else [pltpu.VMEM((d, n), MM_DTYPE)]),
        args=(gy, wg, x, g, dres))


def matmul_tn_blocks(a, b, r, c, tk, by_rows):
    t = a.shape[0]
    assert t % tk == 0 and r % 8 == 0 and c % 128 == 0, (a.shape, b.shape, r, c, tk)

    def body(a_ref, b_ref, o_ref):
        @pl.when(pl.program_id(1) == 0)
        def _():
            o_ref[...] = jnp.zeros_like(o_ref)

        o_ref[...] += _dot_tn(a_ref[...], b_ref[...])

    a_spec = pl.BlockSpec((tk, r), (lambda j, k: (k, j)) if by_rows else (lambda j, k: (k, 0)))
    b_spec = pl.BlockSpec((tk, c), (lambda j, k: (k, 0)) if by_rows else (lambda j, k: (k, j)))
    return pl.pallas_call(
        body, name="matmul_tn_blocks", grid=(N_CHIPS, t // tk), in_specs=[a_spec, b_spec],
        out_specs=pl.BlockSpec((None, r, c), lambda j, k: (j, 0, 0)),
        out_shape=jax.ShapeDtypeStruct((N_CHIPS, r, c), F32),
        compiler_params=_params(("arbitrary", "arbitrary")),
    )(a, b)


def matmul_tn_down(act, dd, tk):
    t, m = act.shape
    c = dd.shape[1]
    r = m // N_CHIPS
    assert t % tk == 0, (t, tk)
    steps = t // tk

    def body(a_ref, b_ref, o_ref, acc):
        k = pl.program_id(1)

        @pl.when(k == 0)
        def _():
            acc[...] = jnp.zeros_like(acc)

        acc[...] += _dot_tn(a_ref[...], b_ref[...])

        @pl.when(k == steps - 1)
        def _():
            o_ref[0] = acc[0:r, :]
            o_ref[1] = acc[r:2 * r, :]

    return pl.pallas_call(
        body, name="matmul_tn_down", grid=(2, steps),
        in_specs=[pl.BlockSpec((tk, 2 * r), lambda p, k: (k, p)), pl.BlockSpec((tk, c), lambda p, k: (k, 0))],
        out_specs=pl.BlockSpec((2, r, c), lambda p, k: (p, 0, 0)),
        out_shape=jax.ShapeDtypeStruct((N_CHIPS, r, c), F32),
        scratch_shapes=[pltpu.VMEM((2 * r, c), F32)],
        compiler_params=_params(("arbitrary", "arbitrary")),
    )(act, dd)


def matmul_tn_in(h, dz, tk):
    t, d = h.shape
    n = dz.shape[1]
    c = n // N_CHIPS
    assert t % tk == 0, (t, tk)
    steps = t // tk

    def body(a_ref, b_ref, o_ref, acc):
        k = pl.program_id(0)

        @pl.when(k == 0)
        def _():
            acc[...] = jnp.zeros_like(acc)

        acc[...] += _dot_tn(a_ref[...], b_ref[...])

        @pl.when(k == steps - 1)
        def _():
            for j in range(N_CHIPS):
                o_ref[j] = acc[:, c * j:c * (j + 1)]

    return pl.pallas_call(
        body, name="matmul_tn_in", grid=(steps,),
        in_specs=[pl.BlockSpec((tk, d), lambda k: (k, 0)), pl.BlockSpec((tk, n), lambda k: (k, 0))],
        out_specs=pl.BlockSpec((N_CHIPS, d, c), lambda k: (0, 0, 0)),
        out_shape=jax.ShapeDtypeStruct((N_CHIPS, d, c), F32),
        scratch_shapes=[pltpu.VMEM((d, n), F32)],
        compiler_params=_params(("arbitrary",)),
    )(h, dz)


def to_tiles(a, tt):
    t = a.shape[0]
    return a.reshape((t // tt, 8, tt // 8) + a.shape[1:]).swapaxes(1, 2).reshape(a.shape)


def from_tiles(a, tt):
    t = a.shape[0]
    return a.reshape((t // tt, tt // 8, 8) + a.shape[1:]).swapaxes(1, 2).reshape(a.shape)


def _roll_sublanes(a, shift):
    n = a.shape[0] // 8
    return pltpu.roll(a.reshape(n, 8, a.shape[1]), shift, 1).reshape(a.shape)


def _halo_before(cur_last, prev_last):
    sub = lax.broadcasted_iota(jnp.int32, cur_last.shape, 0) % 8
    return jnp.where(sub == 0, _roll_sublanes(prev_last, 1), _roll_sublanes(cur_last, 1))


def _halo_after(cur_first, next_first):
    sub = lax.broadcasted_iota(jnp.int32, cur_first.shape, 0) % 8
    return jnp.where(sub == 7, _roll_sublanes(next_first, 7), _roll_sublanes(cur_first, 7))


def _conv_causal(ext, cur, prev_last, w, taps, tt, cols=None):
    hr = 8 * (taps - 1)
    cs = slice(None) if cols is None else cols
    ext[hr:hr + tt, cs] = cur
    ext[0:hr, cs] = _halo_before(cur[tt - hr:, :], prev_last)
    acc = w[0:1, :] * ext[0:tt, cs]
    for k in range(1, taps):
        acc = acc + w[k:k + 1, :] * ext[8 * k:8 * k + tt, cs]
    return acc


def _conv_anticausal(ext, cur, next_first, w, taps, tt, x=None, acc_w=None, cols=None):
    hr = 8 * (taps - 1)
    cs = slice(None) if cols is None else cols
    ext[0:tt, cs] = cur
    ext[tt:tt + hr, cs] = _halo_after(cur[0:hr, :], next_first)
    acc = None
    for k in range(taps):
        off = 8 * (taps - 1 - k)
        ld = ext[off:off + tt, cs]
        term = w[k:k + 1, :] * ld
        acc = term if acc is None else acc + term
        if x is not None:
            acc_w[k, :, cs] += _rowsum8(ld * x)
    return acc


def _dot_exact(a, b, dims):
    return lax.dot_general(a, b, (dims, ((), ())), precision=lax.Precision.HIGHEST, preferred_element_type=F32)


def _to_tile_order(perm, wt_ref, w_scr, transpose):
    pb = perm.astype(MM_DTYPE)
    for h in range(N_HEADS_B):
        half = (_dot_nt(pb, wt_ref[h]) if transpose else _dot(pb, wt_ref[h])).astype(MM_DTYPE)
        w_scr[h] = _dot_nt(half, pb).astype(MM_DTYPE)


def _project_rows(y, w_ref):
    r = w_ref.shape[1]
    acc = _dot(y[:, 0:r], w_ref[0])
    for j in range(1, N_CHIPS):
        acc = acc + _dot(y[:, r * j:r * (j + 1)], w_ref[j])
    return acc


def _head_select(parts):
    head = lax.broadcasted_iota(jnp.int32, parts[0].shape, 1) // HEAD
    acc = parts[0]
    for h in range(1, N_HEADS_B):
        acc = jnp.where(head == h, parts[h], acc)
    return acc


def _mixer_forward(z, prm, q, yc):
    _, lng, lnb, wm, bias_p, _, _, clg, clb = prm
    bg = z[:, 0:D_A]
    ya = bg * q
    o_b = 3 * D_A
    zu = z[:, o_b:o_b + D_B]
    zv = z[:, o_b + D_B:o_b + 2 * D_B]
    u = _gelu(zu)
    vh, rv = _ln_fwd(_gelu(zv))
    vnb = (vh * lng + lnb).astype(MM_DTYPE)
    s = _head_select([_dot(wm[h], vnb) for h in range(N_HEADS_B)]) + bias_p
    yb = u * s
    yh, rc = _ln_fwd(yc)
    l = yh * clg + clb
    sl = jax.nn.sigmoid(l)
    return dict(bg=bg, q=q, ya=ya, zu=zu, zv=zv, u=u, vh=vh, rv=rv, vnb=vnb, s=s, yb=yb, yh=yh, rc=rc, l=l, sl=sl,
                yo=l * sl)


def _conv_inputs(z):
    o_c = 3 * D_A + 2 * D_B
    a = z[:, o_c:o_c + D_C]
    sg = jax.nn.sigmoid(z[:, o_c + D_C:o_c + 2 * D_C])
    return z[:, D_A:2 * D_A] * z[:, 2 * D_A:3 * D_A], a * sg, a, sg


def _group_norm(f, gg):
    ya, yb, yo = f["ya"], f["yb"], f["yo"]
    ra, rb, ro = _rstd(ya), _rstd(yb), _rstd(yo)
    yn = jnp.concatenate([ya * ra * gg[:, 0:D_A], yb * rb * gg[:, D_A:D_A + D_B], yo * ro * gg[:, D_A + D_B:]], axis=1)
    return yn, (ra, rb, ro)


def _mixer_prm(refs, wp_scr, bias_scr):
    caw_ref, lng_ref, lnb_ref, _, _, ccw_ref, ccb_ref, clg_ref, clb_ref = refs
    wm = [wp_scr[h] for h in range(N_HEADS_B)]
    return (caw_ref[...], lng_ref[...], lnb_ref[...], wm, bias_scr[...], ccw_ref[...], ccb_ref[...], clg_ref[...],
            clb_ref[...])


def _mixer_param_specs(tt):
    return [_const_spec((8, D_A)), _const_spec((1, D_B)), _const_spec((1, D_B)), _const_spec((N_HEADS_B, tt, tt)),
            _const_spec((tt, D_B)), _const_spec((32, D_C)), _const_spec((1, D_C)), _const_spec((1, D_C)),
            _const_spec((1, D_C))]


HR_A = 8 * (K_A - 1)
HR_C = 8 * (K_C - 1)
HR_F = 8 * (K_F - 1)


def mixer_fwd(z, x, mp, perm, grp_g, wog, post_g, tt, rider=None):
    t = z.shape[0]
    assert t % tt == 0 and tt % CHUNK == 0 and tt >= HR_C, (t, tt)

    def body(z_ref, x_ref, *rest):
        prm_refs = rest[:9]
        (perm_ref, gg_ref, wo_ref, pg_ref, o_ref, x1_ref, cv_ref, pa_ext, yg_ext, pa_last, yg_last, wp_scr, bias_scr) = rest[9:]
        i = pl.program_id(0)

        @pl.when(i == 0)
        def _():
            pa_last[...] = jnp.zeros_like(pa_last)
            yg_last[...] = jnp.zeros_like(yg_last)
            _to_tile_order(perm_ref[...], prm_refs[3], wp_scr, False)
            bias_scr[...] = _dot_exact(perm_ref[...], prm_refs[4][...], ((1,), (0,)))

        zv = z_ref[...]
        prm = _mixer_prm(prm_refs, wp_scr, bias_scr)
        pa, yg, _, _ = _conv_inputs(zv)
        q = _conv_causal(pa_ext, pa, pa_last[...], prm[0], K_A, tt)
        yc = _conv_causal(yg_ext, yg, yg_last[...], prm[5], K_C, tt) + prm[6]
        pa_last[...] = pa[tt - HR_A:, :]
        yg_last[...] = yg[tt - HR_C:, :]
        cv_ref[:, 0:D_A] = q
        cv_ref[:, D_A:] = yc
        f = _mixer_forward(zv, prm, q, yc)
        yn, _ = _group_norm(f, gg_ref[...])
        o = _project_rows(yn.astype(MM_DTYPE), wo_ref)
        o_ref[...] = o
        x1_ref[...] = x_ref[...] + o * _rstd(o) * pg_ref[...]

    row = lambda c: pl.BlockSpec((tt, c), lambda i: (i, 0))
    return _pallas(
        body, rider, name="mixer_fwd", steps=t // tt,
        in_specs=[row(D_IN), row(D_MODEL)] + _mixer_param_specs(tt)
        + [_const_spec((tt, tt)), _const_spec((1, D_MODEL)), _weight_spec(wog), _const_spec((1, D_MODEL))],
        out_specs=[row(D_MODEL), row(D_MODEL), row(D_A + D_C)],
        out_shape=[jax.ShapeDtypeStruct((t, D_MODEL), F32), jax.ShapeDtypeStruct((t, D_MODEL), F32),
                   jax.ShapeDtypeStruct((t, D_A + D_C), F32)],
        scratch_shapes=[pltpu.VMEM((HR_A + tt, D_A), F32), pltpu.VMEM((HR_C + tt, D_C), F32),
                        pltpu.VMEM((HR_A, D_A), F32), pltpu.VMEM((HR_C, D_C), F32),
                        pltpu.VMEM((N_HEADS_B, tt, tt), MM_DTYPE), pltpu.VMEM((tt, D_B), F32)],
        args=(z, x, *mp, perm, grp_g, wog, post_g))


def mixer_bwd(dx1, o, z, cv, mp, perm, grp_g, wog, post_g, tt, rider=None):
    t = z.shape[0]
    assert t % tt == 0 and tt % CHUNK == 0 and tt >= HR_C, (t, tt)
    steps = t // tt

    def body(dx1_ref, o_ref, z_ref, cv_ref, *rest):
        prm_refs = rest[:9]
        (perm_ref, gg_ref, wo_ref, pg_ref,
         dz_ref, do_ref, yn_ref, dpg_ref, dgg_ref, dcaw_ref, dlng_ref, dlnb_ref, dwm_ref, dbias_ref, dccw_ref, dccb_ref,
         dclg_ref, dclb_ref,
         dq_ext, dyc_ext, dq_first, dyc_first, a_pg, a_gg, a_caw, a_lng, a_lnb, a_ccw, a_ccb, a_clg, a_clb,
         wp_scr, wpt_scr, bias_scr, a_wm, a_bias) = rest[9:]
        i = pl.program_id(0)
        small = (a_pg, a_gg, a_caw, a_lng, a_lnb, a_ccw, a_ccb, a_clg, a_clb)

        @pl.when(i == 0)
        def _():
            for ref in small + (a_wm, a_bias, dq_first, dyc_first):
                ref[...] = jnp.zeros_like(ref)
            _to_tile_order(perm_ref[...], prm_refs[3], wp_scr, False)
            _to_tile_order(perm_ref[...], prm_refs[3], wpt_scr, True)
            bias_scr[...] = _dot_exact(perm_ref[...], prm_refs[4][...], ((1,), (0,)))

        prm = _mixer_prm(prm_refs, wp_scr, bias_scr)
        caw, lng, lnb, wm, bias_p, ccw, ccb, clg, clb = prm

        zv = z_ref[...]
        pa, yg, a, sg = _conv_inputs(zv)
        f = _mixer_forward(zv, prm, cv_ref[:, 0:D_A], cv_ref[:, D_A:])
        gg = gg_ref[...]
        yn, (ra, rb, ro) = _group_norm(f, gg)
        yn_ref[...] = yn.astype(MM_DTYPE)

        ov = o_ref[...]
        dx1v = dx1_ref[...]
        r_o = _rstd(ov)
        pg = pg_ref[...]
        a_pg[...] += _rowsum8(dx1v * ov * r_o)
        do = _rms_bwd(ov, r_o, pg, dx1v).astype(MM_DTYPE)
        do_ref[...] = do
        dyn = jnp.concatenate([_dot_nt(do, wo_ref[j]) for j in range(N_CHIPS)], axis=1)

        dyn_a, dyn_b, dyn_c = dyn[:, 0:D_A], dyn[:, D_A:D_A + D_B], dyn[:, D_A + D_B:]
        ga, gb, gc = gg[:, 0:D_A], gg[:, D_A:D_A + D_B], gg[:, D_A + D_B:]
        a_gg[...] += _rowsum8(jnp.concatenate([dyn_a * f["ya"] * ra, dyn_b * f["yb"] * rb, dyn_c * f["yo"] * ro], axis=1))
        dya = _rms_bwd(f["ya"], ra, ga, dyn_a)
        dyb = _rms_bwd(f["yb"], rb, gb, dyn_b)
        dyo = _rms_bwd(f["yo"], ro, gc, dyn_c)

        dbg = dya * f["q"]
        dq = dya * f["bg"]
        dp = _conv_anticausal(dq_ext, dq, dq_first[...], caw, K_A, tt, x=pa, acc_w=a_caw)
        dq_first[...] = dq[0:HR_A, :]
        dcg = dp * zv[:, 2 * D_A:3 * D_A]
        dxa = dp * zv[:, D_A:2 * D_A]

        du = dyb * f["s"]
        ds = dyb * f["u"]
        dsb = ds.astype(MM_DTYPE)
        head = lax.broadcasted_iota(jnp.int32, (tt, D_B), 1) // HEAD
        a_bias[...] += ds
        parts = []
        for h in range(N_HEADS_B):
            a_wm[h] += _dot_nt(jnp.where(head == h, dsb, jnp.zeros_like(dsb)), f["vnb"])
            parts.append(_dot(wpt_scr[h], dsb))
        dvn = _head_select(parts)
        a_lng[...] += _rowsum8(dvn * f["vh"])
        a_lnb[...] += _rowsum8(dvn)
        dv = _ln_bwd(f["vh"], f["rv"], dvn * lng)
        dzu = du * _gelu_grad(f["zu"])
        dzv = dv * _gelu_grad(f["zv"])

        l, sl = f["l"], f["sl"]
        dl = dyo * (sl * (1.0 + l * (1.0 - sl)))
        a_clg[...] += _rowsum8(dl * f["yh"])
        a_clb[...] += _rowsum8(dl)
        dyc = _ln_bwd(f["yh"], f["rc"], dl * clg)
        a_ccb[...] += _rowsum8(dyc)
        dy = _conv_anticausal(dyc_ext, dyc, dyc_first[...], ccw, K_C, tt, x=yg, acc_w=a_ccw)
        dyc_first[...] = dyc[0:HR_C, :]
        da = dy * sg
        dg = dy * a * sg * (1.0 - sg)

        dz_ref[...] = jnp.concatenate([dbg, dcg, dxa, dzu, dzv, da, dg], axis=1).astype(MM_DTYPE)

        @pl.when(i == steps - 1)
        def _():
            red = lambda ref: jnp.sum(ref[...], axis=0, keepdims=True)
            dpg_ref[...] = red(a_pg)
            dgg_ref[...] = red(a_gg)
            dlng_ref[...] = red(a_lng)
            dlnb_ref[...] = red(a_lnb)
            dccb_ref[...] = red(a_ccb)
            dclg_ref[...] = red(a_clg)
            dclb_ref[...] = red(a_clb)
            dcaw_ref[...] = jnp.sum(a_caw[...], axis=1)
            dccw_ref[...] = jnp.sum(a_ccw[...], axis=1)
            pm = perm_ref[...]
            tril = lax.broadcasted_iota(jnp.int32, (CHUNK, CHUNK), 0) >= lax.broadcasted_iota(jnp.int32, (CHUNK, CHUNK), 1)
            for h in range(N_HEADS_B):
                dwt = _dot_exact(pm, _dot_exact(a_wm[h], pm, ((1,), (0,))), ((0,), (0,)))
                dw = dwt[0:CHUNK, 0:CHUNK]
                for c in range(1, tt // CHUNK):
                    dw = dw + dwt[c * CHUNK:(c + 1) * CHUNK, c * CHUNK:(c + 1) * CHUNK]
                dwm_ref[h] = jnp.where(tril, dw, 0.0)
            dbt = _dot_exact(pm, a_bias[...], ((0,), (0,)))
            db = dbt[0:CHUNK, :]
            for c in range(1, tt // CHUNK):
                db = db + dbt[c * CHUNK:(c + 1) * CHUNK, :]
            dbias_ref[...] = db

    rev = lambda c: pl.BlockSpec((tt, c), lambda i: (steps - 1 - i, 0))
    full = lambda shape: pl.BlockSpec(shape, lambda i: (0,) * len(shape))
    sds = jax.ShapeDtypeStruct
    return _pallas(
        body, rider, name="mixer_bwd", steps=steps,
        in_specs=[rev(D_MODEL), rev(D_MODEL), rev(D_IN), rev(D_A + D_C)] + _mixer_param_specs(tt)
        + [_const_spec((tt, tt)), _const_spec((1, D_MODEL)), _weight_spec(wog), _const_spec((1, D_MODEL))],
        out_specs=[rev(D_IN), rev(D_MODEL), rev(D_MODEL), full((1, D_MODEL)), full((1, D_MODEL)), full((8, D_A)),
                   full((1, D_B)), full((1, D_B)), full((N_HEADS_B, CHUNK, CHUNK)), full((CHUNK, D_B)), full((32, D_C)),
                   full((1, D_C)), full((1, D_C)), full((1, D_C))],
        out_shape=[sds((t, D_IN), MM_DTYPE), sds((t, D_MODEL), MM_DTYPE), sds((t, D_MODEL), MM_DTYPE),
                   sds((1, D_MODEL), F32), sds((1, D_MODEL), F32), sds((8, D_A), F32), sds((1, D_B), F32), sds((1, D_B), F32),
                   sds((N_HEADS_B, CHUNK, CHUNK), F32), sds((CHUNK, D_B), F32), sds((32, D_C), F32), sds((1, D_C), F32),
                   sds((1, D_C), F32), sds((1, D_C), F32)],
        scratch_shapes=[pltpu.VMEM((tt + HR_A, D_A), F32), pltpu.VMEM((tt + HR_C, D_C), F32),
                        pltpu.VMEM((HR_A, D_A), F32), pltpu.VMEM((HR_C, D_C), F32),
                        pltpu.VMEM((8, D_MODEL), F32), pltpu.VMEM((8, D_MODEL), F32), pltpu.VMEM((8, 8, D_A), F32),
                        pltpu.VMEM((8, D_B), F32), pltpu.VMEM((8, D_B), F32), pltpu.VMEM((32, 8, D_C), F32),
                        pltpu.VMEM((8, D_C), F32), pltpu.VMEM((8, D_C), F32), pltpu.VMEM((8, D_C), F32),
                        pltpu.VMEM((N_HEADS_B, tt, tt), MM_DTYPE), pltpu.VMEM((N_HEADS_B, tt, tt), MM_DTYPE),
                        pltpu.VMEM((tt, D_B), F32), pltpu.VMEM((N_HEADS_B, tt, tt), F32), pltpu.VMEM((tt, D_B), F32)],
        args=(dx1, o, z, cv, *mp, perm, grp_g, wog, post_g))


def _fetch_row_blocks(wg_ref, w_scr, sems):
    r = wg_ref.shape[1]
    copies = [pltpu.make_async_copy(wg_ref.at[j], w_scr.at[pl.ds(r * j, r), :], sems.at[j]) for j in range(N_CHIPS)]
    for cp in copies:
        cp.start()
    for cp in copies:
        cp.wait()


def _ffn_conv(ext, cw, c0, cn, tt):
    acc = cw[0:1, c0:c0 + cn] * ext[0:tt, c0:c0 + cn]
    for k in range(1, K_F):
        acc = acc + cw[k:k + 1, c0:c0 + cn] * ext[8 * k:8 * k + tt, c0:c0 + cn]
    return acc


def ffn_fwd(up0, x1, cw, wdg, post_g, tt, rider=None):
    t = up0.shape[0]
    assert t % tt == 0, (t, tt)
    cn = _col_chunk(D_FF)

    def body(up0_ref, x1_ref, cw_ref, wdg_ref, pg_ref, d_ref, x2_ref, ext, last, wd_ref, sems):
        i = pl.program_id(0)

        @pl.when(i == 0)
        def _():
            _fetch_row_blocks(wdg_ref, wd_ref, sems)
            last[...] = jnp.zeros_like(last)

        ext[HR_F:HR_F + tt, :] = up0_ref[...]
        ext[0:HR_F, :] = _halo_before(up0_ref[tt - HR_F:, :], last[...])
        last[...] = up0_ref[tt - HR_F:, :]
        cwv = cw_ref[...]
        d = jnp.zeros((tt, D_MODEL), F32)
        for c0 in range(0, D_FF, cn):
            gate = _ffn_conv(ext, cwv, c0, cn, tt)
            val = _ffn_conv(ext, cwv, D_FF + c0, cn, tt)
            act = (gate * jax.nn.sigmoid(gate) * val).astype(MM_DTYPE)
            d = d + _dot(act, wd_ref[c0:c0 + cn, :])
        d_ref[...] = d
        x2_ref[...] = x1_ref[...] + d * _rstd(d) * pg_ref[...]

    row = lambda c: pl.BlockSpec((tt, c), lambda i: (i, 0))
    return _pallas(
        body, rider, name="ffn_fwd", steps=t // tt,
        in_specs=[row(2 * D_FF), row(D_MODEL), _const_spec((8, 2 * D_FF)), _ANY, _const_spec((1, D_MODEL))],
        out_specs=[row(D_MODEL), row(D_MODEL)],
        out_shape=[jax.ShapeDtypeStruct((t, D_MODEL), F32), jax.ShapeDtypeStruct((t, D_MODEL), F32)],
        scratch_shapes=[pltpu.VMEM((HR_F + tt, 2 * D_FF), F32), pltpu.VMEM((HR_F, 2 * D_FF), F32),
                        pltpu.VMEM((D_FF, D_MODEL), MM_DTYPE), pltpu.SemaphoreType.DMA((N_CHIPS,))],
        args=(up0, x1, cw, wdg, post_g))


def ffn_bwd(dx2, d, up0, cw, wdg, post_g, tt, rider=None):
    t = up0.shape[0]
    assert t % tt == 0, (t, tt)
    steps = t // tt
    hb = tt // HR_F
    cn = _col_chunk(D_FF)

    def body(dx2_ref, d_ref, up0_ref, uh_ref, cw_ref, wdg_ref, pg_ref,
             dd_ref, act_ref, dup0_ref, dpg_ref, dcw_ref, ext, dup_ext, first, a_pg, a_cw, wd_ref, sems):
        i = pl.program_id(0)
        tile = steps - 1 - i

        @pl.when(i == 0)
        def _():
            _fetch_row_blocks(wdg_ref, wd_ref, sems)
            a_pg[...] = jnp.zeros_like(a_pg)
            a_cw[...] = jnp.zeros_like(a_cw)
            first[...] = jnp.zeros_like(first)

        ext[HR_F:HR_F + tt, :] = up0_ref[...]
        ext[0:HR_F, :] = _halo_before(up0_ref[tt - HR_F:, :], jnp.where(tile > 0, uh_ref[...], 0.0))
        cwv = cw_ref[...]
        dv = d_ref[...]
        dx2v = dx2_ref[...]
        r = _rstd(dv)
        a_pg[...] += _rowsum8(dx2v * dv * r)
        dd = _rms_bwd(dv, r, pg_ref[...], dx2v).astype(MM_DTYPE)
        dd_ref[...] = dd
        for c0 in range(0, D_FF, cn):
            gate = _ffn_conv(ext, cwv, c0, cn, tt)
            val = _ffn_conv(ext, cwv, D_FF + c0, cn, tt)
            sg = jax.nn.sigmoid(gate)
            sl = gate * sg
            act_ref[:, c0:c0 + cn] = (sl * val).astype(MM_DTYPE)
            da = _dot_nt(dd, wd_ref[c0:c0 + cn, :])
            dup_ext[0:tt, c0:c0 + cn] = da * val * (sg * (1.0 + gate * (1.0 - sg)))
            dup_ext[0:tt, D_FF + c0:D_FF + c0 + cn] = da * sl
        dup_ext[tt:tt + HR_F, :] = _halo_after(dup_ext[0:HR_F, :], first[...])
        first[...] = dup_ext[0:HR_F, :]
        for c0 in range(0, 2 * D_FF, cn):
            x = up0_ref[:, c0:c0 + cn]
            acc = None
            for k in range(K_F):
                off = 8 * (K_F - 1 - k)
                ld = dup_ext[off:off + tt, c0:c0 + cn]
                term = cwv[k:k + 1, c0:c0 + cn] * ld
                acc = term if acc is None else acc + term
                a_cw[k, :, c0:c0 + cn] += _rowsum8(ld * x)
            dup0_ref[:, c0:c0 + cn] = acc.astype(MM_DTYPE)

        @pl.when(i == steps - 1)
        def _():
            dpg_ref[...] = jnp.sum(a_pg[...], axis=0, keepdims=True)
            dcw_ref[...] = jnp.sum(a_cw[...], axis=1)

    rev = lambda c: pl.BlockSpec((tt, c), lambda i: (steps - 1 - i, 0))
    halo = pl.BlockSpec((HR_F, 2 * D_FF), lambda i: (jnp.maximum((steps - 1 - i) * hb - 1, 0), 0))
    full = lambda shape: pl.BlockSpec(shape, lambda i: (0,) * len(shape))
    sds = jax.ShapeDtypeStruct
    return _pallas(
        body, rider, name="ffn_bwd", steps=steps,
        in_specs=[rev(D_MODEL), rev(D_MODEL), rev(2 * D_FF), halo, _const_spec((8, 2 * D_FF)), _ANY,
                  _const_spec((1, D_MODEL))],
        out_specs=[rev(D_MODEL), rev(D_FF), rev(2 * D_FF), full((1, D_MODEL)), full((8, 2 * D_FF))],
        out_shape=[sds((t, D_MODEL), MM_DTYPE), sds((t, D_FF), MM_DTYPE), sds((t, 2 * D_FF), MM_DTYPE),
                   sds((1, D_MODEL), F32), sds((8, 2 * D_FF), F32)],
        scratch_shapes=[pltpu.VMEM((HR_F + tt, 2 * D_FF), F32), pltpu.VMEM((tt + HR_F, 2 * D_FF), F32),
                        pltpu.VMEM((HR_F, 2 * D_FF), F32), pltpu.VMEM((8, D_MODEL), F32),
                        pltpu.VMEM((8, 8, 2 * D_FF), F32), pltpu.VMEM((D_FF, D_MODEL), MM_DTYPE),
                        pltpu.SemaphoreType.DMA((N_CHIPS,))],
        args=(dx2, d, up0, up0, cw, wdg, post_g))


def loss_head(y, target, tm):
    t, d = y.shape
    assert t % tm == 0, (t, tm)
    steps = t // tm

    def body(y_ref, t_ref, dy_ref, loss_ref, acc):
        i = pl.program_id(0)

        @pl.when(i == 0)
        def _():
            acc[...] = jnp.zeros_like(acc)

        diff = y_ref[...] - t_ref[...]
        dy_ref[...] = diff * (1.0 / d)
        acc[...] += _rowsum8(diff * diff)

        @pl.when(i == steps - 1)
        def _():
            loss_ref[...] = (0.5 / d) * jnp.sum(jnp.sum(acc[...], axis=0, keepdims=True), axis=1, keepdims=True)

    row = pl.BlockSpec((tm, d), lambda i: (i, 0))
    return pl.pallas_call(
        body, name="loss_head", grid=(steps,), in_specs=[row, row],
        out_specs=[row, pl.BlockSpec((1, 1), lambda i: (0, 0))],
        out_shape=[jax.ShapeDtypeStruct((t, d), F32), jax.ShapeDtypeStruct((1, 1), F32)],
        scratch_shapes=[pltpu.VMEM((8, d), F32)],
        compiler_params=_params(("arbitrary",)),
    )(y, target)


def adamw(w, g, m, v):
    shape = w.shape
    cols = shape[-1]
    rows = w.size // cols
    tr = next((r for r in (512, 256, 128) if rows % r == 0 and rows > r), rows)
    c1 = 1.0 - ADAM_B1 ** ADAM_STEP
    c2 = 1.0 - ADAM_B2 ** ADAM_STEP

    def body(w_ref, g_ref, m_ref, v_ref, d_ref, nm_ref, nv_ref):
        gv = g_ref[...]
        nm = ADAM_B1 * m_ref[...] + (1.0 - ADAM_B1) * gv
        nv = ADAM_B2 * v_ref[...] + (1.0 - ADAM_B2) * (gv * gv)
        nm_ref[...] = nm
        nv_ref[...] = nv
        d_ref[...] = -ADAM_LR * ((nm / c1) / (jnp.sqrt(nv / c2) + ADAM_EPS) + ADAM_WD * w_ref[...])

    spec = pl.BlockSpec((tr, cols), lambda i: (i, 0))
    out = jax.ShapeDtypeStruct((rows, cols), F32)
    res = pl.pallas_call(
        body, name="adamw", grid=(rows // tr,), in_specs=[spec] * 4, out_specs=[spec] * 3, out_shape=[out] * 3,
        compiler_params=_params(("arbitrary",)),
    )(*[a.reshape(rows, cols) for a in (w, g, m, v)])
    return tuple(r.reshape(shape) for r in res)


def _place():
    return lax.axis_index("x"), lax.axis_index("y"), lax.axis_index("c")


def _other_chips(x, y):
    return [(1 - x, y, 2 * (1 - x) + y), (x, 1 - y, 2 * x + 1 - y), (1 - x, 1 - y, 2 * (1 - x) + 1 - y)]


def _sem_specs(*counts):
    return [pltpu.SemaphoreType.DMA((n,)) for n in counts]


def cast_shard(w, layer, chip):
    _, r, c = w.shape

    def body(chip_ref, w_ref, o_ref):
        del chip_ref
        o_ref[...] = w_ref[...].astype(MM_DTYPE)

    grid_spec = pltpu.PrefetchScalarGridSpec(
        num_scalar_prefetch=1, grid=(1,), in_specs=[pl.BlockSpec((None, r, c), lambda i, chip_ref: (layer, 0, 0))],
        out_specs=pl.BlockSpec((None, r, c), lambda i, chip_ref: (chip_ref[0], 0, 0)))
    return pl.pallas_call(
        body, name="cast_shard", grid_spec=grid_spec, out_shape=jax.ShapeDtypeStruct((N_CHIPS, r, c), MM_DTYPE),
        compiler_params=_params(("arbitrary",)),
    )(jnp.reshape(chip, (1,)).astype(jnp.int32), w)


def _row_half(buf, chip, mine, c):
    rh = buf.shape[1] // 2
    return buf.at[chip, pl.ds(pl.multiple_of((c if mine else 1 - c) * rh, 16), rh), :]


def spread_rider(bufs):
    n = len(bufs)

    def start(rin, rout, sems):
        x, y, c = _place()
        me = 2 * x + y
        for k, (px, py, _) in enumerate(_other_chips(x, y)):
            for i, buf in enumerate(rout):
                part = _row_half(buf, me, True, c)
                pltpu.make_async_remote_copy(
                    src_ref=part, dst_ref=part, send_sem=sems[0].at[n * k + i], recv_sem=sems[1].at[n * k + i],
                    device_id=(px, py, c), device_id_type=MESH_ID).start()

    def wait(rin, rout, sems):
        x, y, c = _place()
        for k, (_, _, pj) in enumerate(_other_chips(x, y)):
            for i, buf in enumerate(rout):
                part = _row_half(buf, pj, True, c)
                pltpu.make_async_remote_copy(
                    src_ref=part, dst_ref=part, send_sem=sems[0].at[n * k + i], recv_sem=sems[1].at[n * k + i],
                    device_id=(x, y, c), device_id_type=MESH_ID).wait()

    shapes = [jax.ShapeDtypeStruct(b.shape, b.dtype) for b in bufs]
    return Rider("spread", list(bufs), shapes, {i: i for i in range(n)}, (3 * n, 3 * n), start, wait)


def pass_rider(bufs):
    n = len(bufs)

    def start(rin, rout, sems):
        x, y, c = _place()
        for k, (_, _, pj) in enumerate(_other_chips(x, y)):
            for i, buf in enumerate(rout):
                part = _row_half(buf, pj, True, c)
                pltpu.make_async_remote_copy(
                    src_ref=part, dst_ref=part, send_sem=sems[0].at[n * k + i], recv_sem=sems[1].at[n * k + i],
                    device_id=(x, y, 1 - c), device_id_type=MESH_ID).start()

    def wait(rin, rout, sems):
        x, y, c = _place()
        for k, (_, _, pj) in enumerate(_other_chips(x, y)):
            for i, buf in enumerate(rout):
                part = _row_half(buf, pj, False, c)
                pltpu.make_async_remote_copy(
                    src_ref=part, dst_ref=part, send_sem=sems[0].at[n * k + i], recv_sem=sems[1].at[n * k + i],
                    device_id=(x, y, 1 - c), device_id_type=MESH_ID).wait()

    shapes = [jax.ShapeDtypeStruct(b.shape, b.dtype) for b in bufs]
    return Rider("pass", list(bufs), shapes, {i: i for i in range(n)}, (3 * n, 3 * n), start, wait)


def both_riders(a, b):
    na, oa, sa = len(a.inputs), len(a.out_shapes), len(a.sems)

    def start(rin, rout, sems):
        a.start(rin[:na], rout[:oa], sems[:sa])
        b.start(rin[na:], rout[oa:], sems[sa:])

    def wait(rin, rout, sems):
        a.wait(rin[:na], rout[:oa], sems[:sa])
        b.wait(rin[na:], rout[oa:], sems[sa:])

    aliases = dict(a.aliases)
    aliases.update({na + i: oa + o for i, o in b.aliases.items()})
    return Rider(a.name + "_" + b.name, a.inputs + b.inputs, a.out_shapes + b.out_shapes, aliases, a.sems + b.sems,
                 start, wait)


def gather_small(small):
    def body(small_ref, out_ref, send, recv, local):
        x, y, c = _place()
        me = 2 * x + y
        chips = _other_chips(x, y)
        own = pltpu.make_async_copy(small_ref, out_ref.at[me], local.at[0])
        own.start()
        sends = [pltpu.make_async_remote_copy(src_ref=small_ref, dst_ref=out_ref.at[me], send_sem=send.at[k],
                                              recv_sem=recv.at[k], device_id=(px, py, c), device_id_type=MESH_ID)
                 for k, (px, py, _) in enumerate(chips)]
        for cp in sends:
            cp.start()
        for k, (_, _, pj) in enumerate(chips):
            pltpu.make_async_remote_copy(src_ref=small_ref, dst_ref=out_ref.at[pj], send_sem=send.at[k], recv_sem=recv.at[k],
                                         device_id=(x, y, c), device_id_type=MESH_ID).wait_recv()
        for cp in sends:
            cp.wait_send()
        own.wait()

    return pl.pallas_call(
        body, name="gather_small", in_specs=[_ANY], out_specs=_ANY,
        out_shape=jax.ShapeDtypeStruct((N_CHIPS,) + small.shape, small.dtype), scratch_shapes=_sem_specs(3, 3, 1),
        compiler_params=pltpu.CompilerParams(has_side_effects=True),
    )(small)


def swap_rider(gs):
    n = len(gs)

    def copies(rin, rout, sems):
        x, y, c = _place()
        out = []
        for i, (g, got) in enumerate(zip(rin, rout)):
            rh = g.shape[1] // 2
            theirs = pl.ds(pl.multiple_of((1 - c) * rh, 8), rh)
            out.append(pltpu.make_async_remote_copy(
                src_ref=g.at[:, theirs, :], dst_ref=got, send_sem=sems[0].at[i], recv_sem=sems[1].at[i],
                device_id=(x, y, 1 - c), device_id_type=MESH_ID))
        return out

    def start(rin, rout, sems):
        for cp in copies(rin, rout, sems):
            cp.start()

    def wait(rin, rout, sems):
        for cp in copies(rin, rout, sems):
            cp.wait()

    shapes = [jax.ShapeDtypeStruct((g.shape[0], g.shape[1] // 2, g.shape[2]), g.dtype) for g in gs]
    return Rider("swap", list(gs), shapes, {}, (n, n), start, wait)


def scatter_rider(sbs):
    n = len(sbs)

    def start(rin, rout, sems):
        x, y, c = _place()
        me = 2 * x + y
        for k, (px, py, pj) in enumerate(_other_chips(x, y)):
            for i, (sb, got) in enumerate(zip(rin, rout)):
                pltpu.make_async_remote_copy(
                    src_ref=sb.at[pj], dst_ref=got.at[me], send_sem=sems[0].at[n * k + i], recv_sem=sems[1].at[n * k + i],
                    device_id=(px, py, c), device_id_type=MESH_ID).start()

    def wait(rin, rout, sems):
        x, y, c = _place()
        for k, (_, _, pj) in enumerate(_other_chips(x, y)):
            for i, (sb, got) in enumerate(zip(rin, rout)):
                cp = pltpu.make_async_remote_copy(
                    src_ref=sb.at[pj], dst_ref=got.at[pj], send_sem=sems[0].at[n * k + i], recv_sem=sems[1].at[n * k + i],
                    device_id=(x, y, c), device_id_type=MESH_ID)
                cp.wait_recv()
                cp.wait_send()

    shapes = [jax.ShapeDtypeStruct(sb.shape, sb.dtype) for sb in sbs]
    return Rider("scatter", list(sbs), shapes, {}, (3 * n, 3 * n), start, wait)


def join_rider(fs, layers):
    n = len(fs)

    def half(i, f, mine, place):
        x, y, c = place
        rh = f.shape[1] // 2
        block = 2 * x + y if layers[i] is None else layers[i]
        return f.at[block, pl.ds(pl.multiple_of((c if mine else 1 - c) * rh, 8), rh), :]

    def start(rin, rout, sems):
        x, y, c = _place()
        for i, f in enumerate(rout):
            part = half(i, f, True, (x, y, c))
            pltpu.make_async_remote_copy(
                src_ref=part, dst_ref=part, send_sem=sems[0].at[i], recv_sem=sems[1].at[i],
                device_id=(x, y, 1 - c), device_id_type=MESH_ID).start()

    def wait(rin, rout, sems):
        x, y, c = _place()
        for i, f in enumerate(rout):
            part = half(i, f, False, (x, y, c))
            pltpu.make_async_remote_copy(
                src_ref=part, dst_ref=part, send_sem=sems[0].at[i], recv_sem=sems[1].at[i],
                device_id=(x, y, 1 - c), device_id_type=MESH_ID).wait()

    shapes = [jax.ShapeDtypeStruct(f.shape, f.dtype) for f in fs]
    return Rider("join", list(fs), shapes, {i: i for i in range(n)}, (n, n), start, wait)


def add_halves(g, got, wire=BF16):
    n, rh, cols = got.shape
    x, y, c = _place()
    nr = next(k for k in (4, 2, 1) if rh % k == 0 and (rh // k) % 16 == 0)
    tr = rh // nr

    def body(p_ref, g_ref, got_ref, sw_ref, sme_ref):
        s = g_ref[...] + got_ref[...]
        sw_ref[...] = s.astype(wire)

        @pl.when(pl.program_id(1) == p_ref[0])
        def _():
            sme_ref[...] = s

    blk = (None, tr, cols)
    grid_spec = pltpu.PrefetchScalarGridSpec(
        num_scalar_prefetch=1, grid=(nr, n),
        in_specs=[pl.BlockSpec(blk, lambda i, j, p_ref: (j, p_ref[1] * nr + i, 0)),
                  pl.BlockSpec(blk, lambda i, j, p_ref: (j, i, 0))],
        out_specs=[pl.BlockSpec(blk, lambda i, j, p_ref: (j, i, 0)), pl.BlockSpec((tr, cols), lambda i, j, p_ref: (i, 0))])
    wired, own = pl.pallas_call(
        body, name="add_halves", grid_spec=grid_spec,
        out_shape=[jax.ShapeDtypeStruct(got.shape, wire), jax.ShapeDtypeStruct((rh, cols), F32)],
        compiler_params=_params(("arbitrary", "arbitrary")),
    )(jnp.stack([2 * x + y, c]).astype(jnp.int32), g, got)
    return own, wired


def add_chips(own, got, fbuf, block=None):
    n, rh, cols = got.shape
    x, y, c = _place()
    me = 2 * x + y

    def body(p_ref, s_ref, g1_ref, g2_ref, g3_ref, f_ref, o_ref):
        del p_ref, f_ref
        o_ref[...] = s_ref[...] + g1_ref[...].astype(F32) + g2_ref[...].astype(F32) + g3_ref[...].astype(F32)

    blk = (None, rh, cols)

    def other(k):
        return pl.BlockSpec(blk, lambda i, p_ref: ((p_ref[0] + k) % n, 0, 0))

    grid_spec = pltpu.PrefetchScalarGridSpec(
        num_scalar_prefetch=1, grid=(1,),
        in_specs=[pl.BlockSpec((rh, cols), lambda i, p_ref: (0, 0)), other(1), other(2), other(3), _ANY],
        out_specs=pl.BlockSpec(blk, lambda i, p_ref: (p_ref[2], p_ref[1], 0)))
    return pl.pallas_call(
        body, name="add_chips", grid_spec=grid_spec, out_shape=jax.ShapeDtypeStruct(fbuf.shape, F32),
        input_output_aliases={5: 0}, compiler_params=_params(("arbitrary",)),
    )(jnp.stack([me, c, me if block is None else block]).astype(jnp.int32), own, got, got, got, fbuf)


def _pack(arrays, rows):
    flat = jnp.concatenate([a.reshape(-1) for a in arrays])
    return jnp.pad(flat, (0, rows * LANES - flat.size)).reshape(rows, LANES)


def _unpack(buf, shapes):
    flat = buf.reshape(-1)
    out, at = [], 0
    for s in shapes:
        n = math.prod(s)
        out.append(flat[at:at + n].reshape(s))
        at += n
    return out


CONV_SHARDS = [(DEPTH, K_A, D_A // N_CHIPS), (DEPTH, K_C, D_C // N_CHIPS), (DEPTH, K_F, 2 * D_FF // N_CHIPS)]
CONV_ROWS = 32
SMALL_ROWS = 640


def _join_cols(g):
    n, l, r, c = g.shape
    return jnp.transpose(g, (1, 2, 0, 3)).reshape(l, r, n * c)


BIG = ["w_in", "w_out", "w_up", "w_down"]
TILE_MM = 512
TILE_EW = 256


def _pad_rows(a, rows):
    return jnp.pad(a, ((0, rows - a.shape[0]), (0, 0)))


def _row(a):
    return a.reshape(1, -1)


def _tile_perm(tt):
    p = lax.broadcasted_iota(jnp.int32, (tt, tt), 0)
    tok = lax.broadcasted_iota(jnp.int32, (tt, tt), 1)
    return ((tt // 8) * (p % 8) + p // 8 == tok).astype(F32)


def _layer_params(wl, tt):
    n = tt // CHUNK
    tril = jnp.tril(jnp.ones((CHUNK, CHUNK), bool))
    wm = jnp.where(tril[None], wl["sgu_w"], 0.0)
    eye = jnp.eye(n, dtype=F32)
    wt = (eye[None, :, None, :, None] * wm[:, None, :, None, :]).reshape(N_HEADS_B, tt, tt)
    bias_e = jnp.repeat(wl["sgu_b"].T, HEAD, axis=1)
    return (_pad_rows(wl["conv_a_w"], 8), _row(wl["sgu_ln_g"]), _row(wl["sgu_ln_b"]), wt.astype(MM_DTYPE),
            jnp.tile(bias_e, (n, 1)), _pad_rows(wl["conv_c_w"], 32), _row(wl["conv_c_b"]), _row(wl["conv_ln_g"]),
            _row(wl["conv_ln_b"]))


def layer_fwd(x, wl, gw, nxt=None, tm=TILE_MM, tt=TILE_EW):
    mp = _layer_params(wl, tt)
    ride = pass_rider([gw["w_down"]]) if gw.get("pass_down") else None
    (z, h), done = norm_matmul(x, _row(wl["pre_mix_g"]), gw["w_in"], tm, rider=ride)
    gw = {n: (done[0] if ride and n == "w_down" else gw[n]) for n in BIG}
    ride = spread_rider([nxt["w_in"], nxt["w_out"]]) if nxt else None
    (o, x1, cv), done = mixer_fwd(z, x, mp, _tile_perm(tt), _row(wl["grp_norm_g"]), gw["w_out"], _row(wl["post_mix_g"]), tt,
                                  rider=ride)
    ride = both_riders(spread_rider([nxt["w_up"]]), pass_rider(list(done))) if nxt else None
    (up0, h2), done = norm_matmul(x1, _row(wl["pre_ffn_g"]), gw["w_up"], tt, rider=ride)
    if nxt:
        nxt = dict(nxt, w_up=done[0], w_in=done[1], w_out=done[2])
        ride = both_riders(spread_rider([nxt["w_down"]]), pass_rider([nxt["w_up"]]))
    (d, x2), done = ffn_fwd(up0, x1, _pad_rows(wl["ffn_conv_w"], 8), gw["w_down"], _row(wl["post_ffn_g"]), tt, rider=ride)
    if nxt:
        nxt = dict(nxt, w_down=done[0], w_up=done[1], pass_down=True)
    return x2, dict(x=x, z=z, h=h, o=o, x1=x1, up0=up0, h2=h2, d=d, cv=cv, gw=gw), nxt


WIDE = ["w_up", "w_down"]
NARROW = ["w_in", "w_out"]


def layer_bwd(dx2, wl, layer, sv, pend=None, exchange=True, tm=TILE_MM, tt=TILE_EW):
    mp = _layer_params(wl, tt)
    gw = sv["gw"]
    tk = min(512, dx2.shape[0])
    at = {n: BIG.index(n) for n in BIG}
    g = {}
    ride = scatter_rider([sw for _, sw in pend["narrow"]]) if pend else None
    (dd, act, dup0, dpg, dcw), arrived = ffn_bwd(dx2, sv["d"], sv["up0"], _pad_rows(wl["ffn_conv_w"], 8), gw["w_down"],
                                                 _row(wl["post_ffn_g"]), tt, rider=ride)
    fbuf = list(pend["fbuf"]) if pend else grad_buffers()
    if pend:
        for n, (own, _), got in zip(NARROW, pend["narrow"], arrived):
            fbuf[at[n]] = add_chips(own, got, fbuf[at[n]], pend["layer"])
    g["post_ffn_g"] = dpg[0]
    g["ffn_conv_w"] = dcw[:K_F]
    gl = {}
    gl["w_down"] = matmul_tn_down(act, dd, tk)
    gl["w_up"] = matmul_tn_blocks(sv["h2"], dup0, D_MODEL, 2 * D_FF // N_CHIPS, tk, by_rows=False)
    ride = swap_rider([gl[n] for n in WIDE]) if exchange else None
    (dx1, dg), got = matmul_nt_norm_bwd(dup0, gw["w_up"], sv["x1"], _row(wl["pre_ffn_g"]), dx2, tm, rider=ride)
    g["pre_ffn_g"] = dg[0]
    wide = [add_halves(gl[n], gt) for n, gt in zip(WIDE, got)] if exchange else None
    ride = scatter_rider([sw for _, sw in wide]) if exchange else None
    if pend:
        ride = both_riders(join_rider(fbuf, [pend["layer"]] * len(fbuf)), ride)
    (dz, do, yn, dpg, dgg, dcaw, dlng, dlnb, dwm, dbias, dccw, dccb, dclg, dclb), rode = mixer_bwd(
        dx1, sv["o"], sv["z"], sv["cv"], mp, _tile_perm(tt), _row(wl["grp_norm_g"]), gw["w_out"],
        _row(wl["post_mix_g"]), tt, rider=ride)
    if exchange:
        fbuf, arrived = (list(rode[:len(BIG)]), rode[len(BIG):]) if pend else (fbuf, rode)
        for n, (own, _), got in zip(WIDE, wide, arrived):
            fbuf[at[n]] = add_chips(own, got, fbuf[at[n]], layer)
    g["post_mix_g"] = dpg[0]
    g["grp_norm_g"] = dgg[0]
    g["conv_a_w"] = dcaw[:K_A]
    g["sgu_ln_g"] = dlng[0]
    g["sgu_ln_b"] = dlnb[0]
    g["sgu_w"] = dwm
    g["sgu_b"] = jnp.sum(dbias.reshape(CHUNK, N_HEADS_B, HEAD), axis=2).T
    g["conv_c_w"] = dccw[:K_C]
    g["conv_c_b"] = dccb[0]
    g["conv_ln_g"] = dclg[0]
    g["conv_ln_b"] = dclb[0]
    gl["w_out"] = matmul_tn_blocks(yn, do, D_MODEL // N_CHIPS, D_MODEL, tk, by_rows=True)
    gl["w_in"] = matmul_tn_in(sv["h"], dz, tk)
    ride = swap_rider([gl[n] for n in NARROW]) if exchange else None
    (dx, dg), got = matmul_nt_norm_bwd(dz, gw["w_in"], sv["x"], _row(wl["pre_mix_g"]), dx1, tm, rider=ride)
    g["pre_mix_g"] = dg[0]
    if not exchange:
        return dx, g, gl
    narrow = [add_halves(gl[n], gt) for n, gt in zip(NARROW, got)]
    return dx, g, dict(narrow=narrow, fbuf=fbuf, layer=layer)


def grad_buffers():
    return [lax.empty(s, F32) for s in ((DEPTH, D_MODEL, D_IN // N_CHIPS), (DEPTH, D_MODEL // N_CHIPS, D_MODEL),
                                        (DEPTH, D_MODEL, 2 * D_FF // N_CHIPS), (DEPTH, D_FF // N_CHIPS, D_MODEL))]


CONV = ["conv_a_w", "conv_c_w", "ffn_conv_w"]
REPL = ["pre_mix_g", "sgu_ln_g", "sgu_ln_b", "sgu_w", "sgu_b", "conv_c_b", "conv_ln_g", "conv_ln_b", "grp_norm_g",
        "post_mix_g", "pre_ffn_g", "post_ffn_g"]
WEIGHTS = ["pre_mix_g", "w_in", "conv_a_w", "sgu_ln_g", "sgu_ln_b", "sgu_w", "sgu_b", "conv_c_w", "conv_c_b", "conv_ln_g",
           "conv_ln_b", "grp_norm_g", "w_out", "post_mix_g", "pre_ffn_g", "w_up", "ffn_conv_w", "w_down", "post_ffn_g"]


def kernel(x, pre_mix_g, w_in, conv_a_w, sgu_ln_g, sgu_ln_b, sgu_w, sgu_b, conv_c_w, conv_c_b, conv_ln_g, conv_ln_b, grp_norm_g, w_out, post_mix_g, pre_ffn_g, w_up, ffn_conv_w, w_down, post_ffn_g, loss_target, m_pre_mix_g, m_w_in, m_conv_a_w, m_sgu_ln_g, m_sgu_ln_b, m_sgu_w, m_sgu_b, m_conv_c_w, m_conv_c_b, m_conv_ln_g, m_conv_ln_b, m_grp_norm_g, m_w_out, m_post_mix_g, m_pre_ffn_g, m_w_up, m_ffn_conv_w, m_w_down, m_post_ffn_g, v_pre_mix_g, v_w_in, v_conv_a_w, v_sgu_ln_g, v_sgu_ln_b, v_sgu_w, v_sgu_b, v_conv_c_w, v_conv_c_b, v_conv_ln_g, v_conv_ln_b, v_grp_norm_g, v_w_out, v_post_mix_g, v_pre_ffn_g, v_w_up, v_ffn_conv_w, v_w_down, v_post_ffn_g):
    w = dict(pre_mix_g=pre_mix_g, w_in=w_in, conv_a_w=conv_a_w, sgu_ln_g=sgu_ln_g, sgu_ln_b=sgu_ln_b, sgu_w=sgu_w, sgu_b=sgu_b,
             conv_c_w=conv_c_w, conv_c_b=conv_c_b, conv_ln_g=conv_ln_g, conv_ln_b=conv_ln_b, grp_norm_g=grp_norm_g,
             w_out=w_out, post_mix_g=post_mix_g, pre_ffn_g=pre_ffn_g, w_up=w_up, ffn_conv_w=ffn_conv_w, w_down=w_down,
             post_ffn_g=post_ffn_g)
    m = dict(pre_mix_g=m_pre_mix_g, w_in=m_w_in, conv_a_w=m_conv_a_w, sgu_ln_g=m_sgu_ln_g, sgu_ln_b=m_sgu_ln_b,
             sgu_w=m_sgu_w, sgu_b=m_sgu_b, conv_c_w=m_conv_c_w, conv_c_b=m_conv_c_b, conv_ln_g=m_conv_ln_g,
             conv_ln_b=m_conv_ln_b, grp_norm_g=m_grp_norm_g, w_out=m_w_out, post_mix_g=m_post_mix_g,
             pre_ffn_g=m_pre_ffn_g, w_up=m_w_up, ffn_conv_w=m_ffn_conv_w, w_down=m_w_down, post_ffn_g=m_post_ffn_g)
    v = dict(pre_mix_g=v_pre_mix_g, w_in=v_w_in, conv_a_w=v_conv_a_w, sgu_ln_g=v_sgu_ln_g, sgu_ln_b=v_sgu_ln_b,
             sgu_w=v_sgu_w, sgu_b=v_sgu_b, conv_c_w=v_conv_c_w, conv_c_b=v_conv_c_b, conv_ln_g=v_conv_ln_g,
             conv_ln_b=v_conv_ln_b, grp_norm_g=v_grp_norm_g, w_out=v_w_out, post_mix_g=v_post_mix_g,
             pre_ffn_g=v_pre_ffn_g, w_up=v_w_up, ffn_conv_w=v_ffn_conv_w, w_down=v_w_down, post_ffn_g=v_post_ffn_g)
    chip = 2 * lax.axis_index("x") + lax.axis_index("y")

    convs = gather_small(_pack([w[n] for n in CONV], CONV_ROWS))
    gws = [{n: cast_shard(w[n], layer, chip) for n in BIG} for layer in range(DEPTH)]
    first = run_rider(pass_rider(run_rider(spread_rider([gws[0][n] for n in BIG]))))
    gws[0] = dict(zip(BIG, first))
    cparts = [_unpack(convs[j], CONV_SHARDS) for j in range(N_CHIPS)]
    full = dict(w)
    for i, n in enumerate(CONV):
        full[n] = _join_cols(jnp.stack([p[i] for p in cparts]))

    xc = to_tiles(x[0], TILE_EW)
    saved = []
    for layer in range(DEPTH):
        nxt = gws[layer + 1] if layer + 1 < DEPTH else None
        xc, sv, nxt = layer_fwd(xc, {n: full[n][layer] for n in REPL + CONV}, gws[layer], nxt)
        if nxt:
            gws[layer + 1] = nxt
        saved.append(sv)
    dxc, loss_part = loss_head(xc, to_tiles(loss_target[0], TILE_EW), TILE_MM)
    loss = lax.psum(loss_part[0, 0], ("x", "y", "c"))
    small = [None] * DEPTH
    pend = None
    for layer in reversed(range(DEPTH)):
        dxc, small[layer], pend = layer_bwd(dxc, {n: full[n][layer] for n in REPL + CONV}, layer, saved[layer], pend)
    grads = {n: jnp.stack([small[layer][n] for layer in range(DEPTH)]) for n in REPL + CONV}

    gsmall = _pack([grads[n] for n in REPL + CONV], SMALL_ROWS).reshape(N_CHIPS, SMALL_ROWS // N_CHIPS, LANES)
    sums = pend["narrow"] + [add_halves(gsmall, run_rider(swap_rider([gsmall]))[0], wire=F32)]
    arrived = run_rider(scatter_rider([sw for _, sw in sums]))
    fbuf = pend["fbuf"] + [lax.empty(gsmall.shape, F32)]
    for i, blk, (own, _), got in zip([BIG.index(n) for n in NARROW] + [len(BIG)], [0, 0, None], sums, arrived):
        fbuf[i] = add_chips(own, got, fbuf[i], blk)
    joined = run_rider(join_rider(fbuf, [0] * len(BIG) + [None]))
    out_g = dict(zip(BIG, joined))
    tot = run_rider(pass_rider(run_rider(spread_rider([joined[len(BIG)]]))))[0].reshape(SMALL_ROWS, LANES)
    shapes = [grads[n].shape for n in REPL + CONV]
    for n, gfull in zip(REPL + CONV, _unpack(tot, shapes)):
        if n in CONV:
            width = gfull.shape[-1] // N_CHIPS
            gfull = lax.dynamic_slice_in_dim(gfull, chip * width, width, axis=2)
        out_g[n] = gfull

    deltas, new_m, new_v = {}, {}, {}
    for n in WEIGHTS:
        deltas[n], new_m[n], new_v[n] = adamw(w[n], out_g[n], m[n], v[n])
    return (loss, from_tiles(dxc, TILE_EW)[None], *[out_g[n] for n in WEIGHTS], *[deltas[n] for n in WEIGHTS], *[new_m[n] for n in WEIGHTS],
            *[new_v[n] for n in WEIGHTS])
```

```python
import math
from typing import Callable, NamedTuple

import jax
import jax.numpy as jnp
from jax import lax
from jax.experimental import pallas as pl
from jax.experimental.pallas import tpu as pltpu

F32 = jnp.float32
BF16 = jnp.bfloat16
MM_DTYPE = BF16

D_MODEL = 1024
DEPTH = 4
D_A = 256
D_B = 384
D_C = 384
D_IN = 3 * D_A + 2 * D_B + 2 * D_C
D_FF = 2816
K_A = 3
K_C = 31
K_F = 3
CHUNK = 128
HEAD = 64
N_HEADS_B = D_B // HEAD
EPS = 1e-6
N_CHIPS = 4

ADAM_LR = 0.001
ADAM_B1 = 0.9
ADAM_B2 = 0.999
ADAM_EPS = 1e-08
ADAM_WD = 0.01
ADAM_STEP = 10

LANES = 1024
VMEM_LIMIT = 56 * 1024 * 1024

MESH_ID = pl.DeviceIdType.MESH
_ANY = pl.BlockSpec(memory_space=pl.ANY)


def _params(sem=None):
    return pltpu.CompilerParams(dimension_semantics=sem, vmem_limit_bytes=VMEM_LIMIT)


def _const_spec(shape):
    nd = len(shape)
    return pl.BlockSpec(shape, lambda *_: (0,) * nd, pipeline_mode=pl.Buffered(1))


def _rowsum8(a):
    r, c = a.shape
    return jnp.sum(a.reshape(r // 8, 8, c), axis=0)


def _rstd(x):
    return lax.rsqrt(jnp.mean(x * x, axis=-1, keepdims=True) + EPS)


def _rms_bwd(x, r, g, dy):
    gdy = g * dy
    return r * gdy - x * (r * r * r) * jnp.mean(gdy * x, axis=-1, keepdims=True)


def _ln_fwd(x):
    mu = jnp.mean(x, axis=-1, keepdims=True)
    xc = x - mu
    r = lax.rsqrt(jnp.mean(xc * xc, axis=-1, keepdims=True) + EPS)
    return xc * r, r


def _ln_bwd(xh, r, dxh):
    return r * (dxh - jnp.mean(dxh, axis=-1, keepdims=True) - xh * jnp.mean(dxh * xh, axis=-1, keepdims=True))


def _gelu(x):
    return 0.5 * x * (1.0 + lax.erf(x * (1.0 / math.sqrt(2.0))))


def _gelu_grad(x):
    cdf = 0.5 * (1.0 + lax.erf(x * (1.0 / math.sqrt(2.0))))
    pdf = jnp.exp(-0.5 * x * x) * (1.0 / math.sqrt(2.0 * math.pi))
    return cdf + x * pdf


def _dot(a, b):
    return jnp.dot(a, b, preferred_element_type=F32)


def _dot_nt(a, b):
    return lax.dot_general(a, b, (((1,), (1,)), ((), ())), preferred_element_type=F32)


def _dot_tn(a, b):
    return lax.dot_general(a, b, (((0,), (0,)), ((), ())), preferred_element_type=F32)


def _col_chunk(n):
    for c in (1408, 1024, 768, 512, 256, 128):
        if n % c == 0:
            return c
    raise ValueError(n)


class Rider(NamedTuple):
    name: str
    inputs: list
    out_shapes: list
    aliases: dict
    sems: tuple
    start: Callable
    wait: Callable


def _pallas(body, rider, *, name, steps, in_specs, out_specs, out_shape, scratch_shapes, args):
    if rider is None:
        res = pl.pallas_call(body, name=name, grid=(steps,), in_specs=in_specs, out_specs=out_specs, out_shape=out_shape,
                             scratch_shapes=scratch_shapes, compiler_params=_params(("arbitrary",)))(*args)
        return res, []
    n_in, n_out, n_scr = len(in_specs), len(out_specs), len(scratch_shapes)
    r_in, r_out = len(rider.inputs), len(rider.out_shapes)

    def wrapped(*refs):
        ins, rin = refs[:n_in], refs[n_in:n_in + r_in]
        at = n_in + r_in
        outs, rout = refs[at:at + n_out], refs[at + n_out:at + n_out + r_out]
        at += n_out + r_out
        scr, rsem = refs[at:at + n_scr], refs[at + n_scr:]

        @pl.when(pl.program_id(0) == 0)
        def _():
            rider.start(rin, rout, rsem)

        body(*ins, *outs, *scr)

        @pl.when(pl.program_id(0) == steps - 1)
        def _():
            rider.wait(rin, rout, rsem)

    res = pl.pallas_call(
        wrapped, name=name + "_" + rider.name, grid=(steps,), in_specs=list(in_specs) + [_ANY] * r_in,
        out_specs=list(out_specs) + [_ANY] * r_out, out_shape=list(out_shape) + list(rider.out_shapes),
        scratch_shapes=list(scratch_shapes) + [pltpu.SemaphoreType.DMA((n,)) for n in rider.sems],
        input_output_aliases={n_in + i: n_out + o for i, o in rider.aliases.items()},
        compiler_params=pltpu.CompilerParams(dimension_semantics=("arbitrary",), vmem_limit_bytes=VMEM_LIMIT,
                                             has_side_effects=True),
    )(*args, *rider.inputs)
    return res[:n_out], res[n_out:]


def run_rider(rider):
    def body(*refs):
        r_in, r_out = len(rider.inputs), len(rider.out_shapes)
        rin, rout, rsem = refs[:r_in], refs[r_in:r_in + r_out], refs[r_in + r_out:]
        rider.start(rin, rout, rsem)
        rider.wait(rin, rout, rsem)

    return pl.pallas_call(
        body, name=rider.name, in_specs=[_ANY] * len(rider.inputs), out_specs=[_ANY] * len(rider.out_shapes),
        out_shape=list(rider.out_shapes), scratch_shapes=[pltpu.SemaphoreType.DMA((n,)) for n in rider.sems],
        input_output_aliases=dict(rider.aliases), compiler_params=pltpu.CompilerParams(has_side_effects=True),
    )(*rider.inputs)


def _weight_spec(wg):
    return _const_spec(wg.shape)


def _join_col_blocks(w_ref, w_scr):
    c = w_ref.shape[2]
    for j in range(N_CHIPS):
        w_scr[:, c * j:c * (j + 1)] = w_ref[j]


def norm_matmul(x, g, wg, tm, rider=None):
    t, d = x.shape
    assert t % tm == 0, (t, tm)
    cw = wg.shape[2]
    n = N_CHIPS * cw
    aligned = cw % 128 == 0
    cn = cw if aligned else _col_chunk(n)

    def body(x_ref, g_ref, w_ref, o_ref, h_ref, *scr):
        if not aligned:
            @pl.when(pl.program_id(0) == 0)
            def _():
                _join_col_blocks(w_ref, scr[0])

        xv = x_ref[...]
        h = (xv * _rstd(xv) * g_ref[...]).astype(MM_DTYPE)
        h_ref[...] = h
        for j, c0 in enumerate(range(0, n, cn)):
            wv = w_ref[j] if aligned else scr[0][:, c0:c0 + cn]
            o_ref[:, c0:c0 + cn] = _dot(h, wv)

    return _pallas(
        body, rider, name="norm_matmul", steps=t // tm,
        in_specs=[pl.BlockSpec((tm, d), lambda i: (i, 0)), _const_spec((1, d)), _weight_spec(wg)],
        out_specs=[pl.BlockSpec((tm, n), lambda i: (i, 0)), pl.BlockSpec((tm, d), lambda i: (i, 0))],
        out_shape=[jax.ShapeDtypeStruct((t, n), F32), jax.ShapeDtypeStruct((t, d), MM_DTYPE)],
        scratch_shapes=[] if aligned else [pltpu.VMEM((d, n), MM_DTYPE)],
        args=(x, g, wg))


def matmul_nt_norm_bwd(gy, wg, x, g, dres, tm, rider=None):
    t, n = gy.shape
    assert t % tm == 0, (t, tm)
    d, cw = wg.shape[1], wg.shape[2]
    aligned = cw % 128 == 0
    cn = cw if aligned else _col_chunk(n)
    steps = t // tm

    def body(gy_ref, w_ref, x_ref, g_ref, dres_ref, dx_ref, dg_ref, acc_ref, *scr):
        i = pl.program_id(0)

        @pl.when(i == 0)
        def _():
            acc_ref[...] = jnp.zeros_like(acc_ref)
            if not aligned:
                _join_col_blocks(w_ref, scr[0])

        dh = jnp.zeros((tm, d), F32)
        for j, c0 in enumerate(range(0, n, cn)):
            wv = w_ref[j] if aligned else scr[0][:, c0:c0 + cn]
            dh = dh + _dot_nt(gy_ref[:, c0:c0 + cn], wv)
        xv = x_ref[...]
        r = _rstd(xv)
        gv = g_ref[...]
        dx_ref[...] = dres_ref[...] + _rms_bwd(xv, r, gv, dh)
        acc_ref[...] += _rowsum8(dh * xv * r)

        @pl.when(i == steps - 1)
        def _():
            dg_ref[...] = jnp.sum(acc_ref[...], axis=0, keepdims=True)

    return _pallas(
        body, rider, name="matmul_nt_norm_bwd", steps=steps,
        in_specs=[pl.BlockSpec((tm, n), lambda i: (i, 0)), _weight_spec(wg), pl.BlockSpec((tm, d), lambda i: (i, 0)),
                  _const_spec((1, d)), pl.BlockSpec((tm, d), lambda i: (i, 0))],
        out_specs=[pl.BlockSpec((tm, d), lambda i: (i, 0)), pl.BlockSpec((1, d), lambda i: (0, 0))],
        out_shape=[jax.ShapeDtypeStruct((t, d), F32), jax.ShapeDtypeStruct((1, d), F32)],
        scratch_shapes=[pltpu.VMEM((8, d), F32)] + ([] if aligned else [pltpu.VMEM((d, n), MM_DTYPE)]),
        args=(gy, wg, x, g, dres))


def matmul_tn_blocks(a, b, r, c, tk, by_rows):
    t = a.shape[0]
    assert t % tk == 0 and r % 8 == 0 and c % 128 == 0, (a.shape, b.shape, r, c, tk)

    def body(a_ref, b_ref, o_ref):
        @pl.when(pl.program_id(1) == 0)
        def _():
            o_ref[...] = jnp.zeros_like(o_ref)

        o_ref[...] += _dot_tn(a_ref[...], b_ref[...])

    a_spec = pl.BlockSpec((tk, r), (lambda j, k: (k, j)) if by_rows else (lambda j, k: (k, 0)))
    b_spec = pl.BlockSpec((tk, c), (lambda j, k: (k, 0)) if by_rows else (lambda j, k: (k, j)))
    return pl.pallas_call(
        body, name="matmul_tn_blocks", grid=(N_CHIPS, t // tk), in_specs=[a_spec, b_spec],
        out_specs=pl.BlockSpec((None, r, c), lambda j, k: (j, 0, 0)),
        out_shape=jax.ShapeDtypeStruct((N_CHIPS, r, c), F32),
        compiler_params=_params(("arbitrary", "arbitrary")),
    )(a, b)


def matmul_tn_down(act, dd, tk):
    t, m = act.shape
    c = dd.shape[1]
    r = m // N_CHIPS
    assert t % tk == 0, (t, tk)
    steps = t // tk

    def body(a_ref, b_ref, o_ref, acc):
        k = pl.program_id(1)

        @pl.when(k == 0)
        def _():
            acc[...] = jnp.zeros_like(acc)

        acc[...] += _dot_tn(a_ref[...], b_ref[...])

        @pl.when(k == steps - 1)
        def _():
            o_ref[0] = acc[0:r, :]
            o_ref[1] = acc[r:2 * r, :]

    return pl.pallas_call(
        body, name="matmul_tn_down", grid=(2, steps),
        in_specs=[pl.BlockSpec((tk, 2 * r), lambda p, k: (k, p)), pl.BlockSpec((tk, c), lambda p, k: (k, 0))],
        out_specs=pl.BlockSpec((2, r, c), lambda p, k: (p, 0, 0)),
        out_shape=jax.ShapeDtypeStruct((N_CHIPS, r, c), F32),
        scratch_shapes=[pltpu.VMEM((2 * r, c), F32)],
        compiler_params=_params(("arbitrary", "arbitrary")),
    )(act, dd)


def matmul_tn_in(h, dz, tk):
    t, d = h.shape
    n = dz.shape[1]
    c = n // N_CHIPS
    assert t % tk == 0, (t, tk)
    steps = t // tk

    def body(a_ref, b_ref, o_ref, acc):
        k = pl.program_id(0)

        @pl.when(k == 0)
        def _():
            acc[...] = jnp.zeros_like(acc)

        acc[...] += _dot_tn(a_ref[...], b_ref[...])

        @pl.when(k == steps - 1)
        def _():
            for j in range(N_CHIPS):
                o_ref[j] = acc[:, c * j:c * (j + 1)]

    return pl.pallas_call(
        body, name="matmul_tn_in", grid=(steps,),
        in_specs=[pl.BlockSpec((tk, d), lambda k: (k, 0)), pl.BlockSpec((tk, n), lambda k: (k, 0))],
        out_specs=pl.BlockSpec((N_CHIPS, d, c), lambda k: (0, 0, 0)),
        out_shape=jax.ShapeDtypeStruct((N_CHIPS, d, c), F32),
        scratch_shapes=[pltpu.VMEM((d, n), F32)],
        compiler_params=_params(("arbitrary",)),
    )(h, dz)


def to_tiles(a, tt):
    t = a.shape[0]
    return a.reshape((t // tt, 8, tt // 8) + a.shape[1:]).swapaxes(1, 2).reshape(a.shape)


def from_tiles(a, tt):
    t = a.shape[0]
    return a.reshape((t // tt, tt // 8, 8) + a.shape[1:]).swapaxes(1, 2).reshape(a.shape)


def _roll_sublanes(a, shift):
    n = a.shape[0] // 8
    return pltpu.roll(a.reshape(n, 8, a.shape[1]), shift, 1).reshape(a.shape)


def _halo_before(cur_last, prev_last):
    sub = lax.broadcasted_iota(jnp.int32, cur_last.shape, 0) % 8
    return jnp.where(sub == 0, _roll_sublanes(prev_last, 1), _roll_sublanes(cur_last, 1))


def _halo_after(cur_first, next_first):
    sub = lax.broadcasted_iota(jnp.int32, cur_first.shape, 0) % 8
    return jnp.where(sub == 7, _roll_sublanes(next_first, 7), _roll_sublanes(cur_first, 7))


def _conv_causal(ext, cur, prev_last, w, taps, tt, cols=None):
    hr = 8 * (taps - 1)
    cs = slice(None) if cols is None else cols
    ext[hr:hr + tt, cs] = cur
    ext[0:hr, cs] = _halo_before(cur[tt - hr:, :], prev_last)
    acc = w[0:1, :] * ext[0:tt, cs]
    for k in range(1, taps):
        acc = acc + w[k:k + 1, :] * ext[8 * k:8 * k + tt, cs]
    return acc


def _conv_anticausal(ext, cur, next_first, w, taps, tt, x=None, acc_w=None, cols=None):
    hr = 8 * (taps - 1)
    cs = slice(None) if cols is None else cols
    ext[0:tt, cs] = cur
    ext[tt:tt + hr, cs] = _halo_after(cur[0:hr, :], next_first)
    acc = None
    for k in range(taps):
        off = 8 * (taps - 1 - k)
        ld = ext[off:off + tt, cs]
        term = w[k:k + 1, :] * ld
        acc = term if acc is None else acc + term
        if x is not None:
            acc_w[k, :, cs] += _rowsum8(ld * x)
    return acc


def _dot_exact(a, b, dims):
    return lax.dot_general(a, b, (dims, ((), ())), precision=lax.Precision.HIGHEST, preferred_element_type=F32)


def _to_tile_order(perm, wt_ref, w_scr, transpose):
    pb = perm.astype(MM_DTYPE)
    for h in range(N_HEADS_B):
        half = (_dot_nt(pb, wt_ref[h]) if transpose else _dot(pb, wt_ref[h])).astype(MM_DTYPE)
        w_scr[h] = _dot_nt(half, pb).astype(MM_DTYPE)


def _project_rows(y, w_ref):
    r = w_ref.shape[1]
    acc = _dot(y[:, 0:r], w_ref[0])
    for j in range(1, N_CHIPS):
        acc = acc + _dot(y[:, r * j:r * (j + 1)], w_ref[j])
    return acc


def _head_select(parts):
    head = lax.broadcasted_iota(jnp.int32, parts[0].shape, 1) // HEAD
    acc = parts[0]
    for h in range(1, N_HEADS_B):
        acc = jnp.where(head == h, parts[h], acc)
    return acc


def _mixer_forward(z, prm, q, yc):
    _, lng, lnb, wm, bias_p, _, _, clg, clb = prm
    bg = z[:, 0:D_A]
    ya = bg * q
    o_b = 3 * D_A
    zu = z[:, o_b:o_b + D_B]
    zv = z[:, o_b + D_B:o_b + 2 * D_B]
    u = _gelu(zu)
    vh, rv = _ln_fwd(_gelu(zv))
    vnb = (vh * lng + lnb).astype(MM_DTYPE)
    s = _head_select([_dot(wm[h], vnb) for h in range(N_HEADS_B)]) + bias_p
    yb = u * s
    yh, rc = _ln_fwd(yc)
    l = yh * clg + clb
    sl = jax.nn.sigmoid(l)
    return dict(bg=bg, q=q, ya=ya, zu=zu, zv=zv, u=u, vh=vh, rv=rv, vnb=vnb, s=s, yb=yb, yh=yh, rc=rc, l=l, sl=sl,
                yo=l * sl)


def _conv_inputs(z):
    o_c = 3 * D_A + 2 * D_B
    a = z[:, o_c:o_c + D_C]
    sg = jax.nn.sigmoid(z[:, o_c + D_C:o_c + 2 * D_C])
    return z[:, D_A:2 * D_A] * z[:, 2 * D_A:3 * D_A], a * sg, a, sg


def _group_norm(f, gg):
    ya, yb, yo = f["ya"], f["yb"], f["yo"]
    ra, rb, ro = _rstd(ya), _rstd(yb), _rstd(yo)
    yn = jnp.concatenate([ya * ra * gg[:, 0:D_A], yb * rb * gg[:, D_A:D_A + D_B], yo * ro * gg[:, D_A + D_B:]], axis=1)
    return yn, (ra, rb, ro)


def _mixer_prm(refs, wp_scr, bias_scr):
    caw_ref, lng_ref, lnb_ref, _, _, ccw_ref, ccb_ref, clg_ref, clb_ref = refs
    wm = [wp_scr[h] for h in range(N_HEADS_B)]
    return (caw_ref[...], lng_ref[...], lnb_ref[...], wm, bias_scr[...], ccw_ref[...], ccb_ref[...], clg_ref[...],
            clb_ref[...])


def _mixer_param_specs(tt):
    return [_const_spec((8, D_A)), _const_spec((1, D_B)), _const_spec((1, D_B)), _const_spec((N_HEADS_B, tt, tt)),
            _const_spec((tt, D_B)), _const_spec((32, D_C)), _const_spec((1, D_C)), _const_spec((1, D_C)),
            _const_spec((1, D_C))]


HR_A = 8 * (K_A - 1)
HR_C = 8 * (K_C - 1)
HR_F = 8 * (K_F - 1)


def mixer_fwd(z, x, mp, perm, grp_g, wog, post_g, tt, rider=None):
    t = z.shape[0]
    assert t % tt == 0 and tt % CHUNK == 0 and tt >= HR_C, (t, tt)

    def body(z_ref, x_ref, *rest):
        prm_refs = rest[:9]
        (perm_ref, gg_ref, wo_ref, pg_ref, o_ref, x1_ref, cv_ref, pa_ext, yg_ext, pa_last, yg_last, wp_scr, bias_scr) = rest[9:]
        i = pl.program_id(0)

        @pl.when(i == 0)
        def _():
            pa_last[...] = jnp.zeros_like(pa_last)
            yg_last[...] = jnp.zeros_like(yg_last)
            _to_tile_order(perm_ref[...], prm_refs[3], wp_scr, False)
            bias_scr[...] = _dot_exact(perm_ref[...], prm_refs[4][...], ((1,), (0,)))

        zv = z_ref[...]
        prm = _mixer_prm(prm_refs, wp_scr, bias_scr)
        pa, yg, _, _ = _conv_inputs(zv)
        q = _conv_causal(pa_ext, pa, pa_last[...], prm[0], K_A, tt)
        yc = _conv_causal(yg_ext, yg, yg_last[...], prm[5], K_C, tt) + prm[6]
        pa_last[...] = pa[tt - HR_A:, :]
        yg_last[...] = yg[tt - HR_C:, :]
        cv_ref[:, 0:D_A] = q
        cv_ref[:, D_A:] = yc
        f = _mixer_forward(zv, prm, q, yc)
        yn, _ = _group_norm(f, gg_ref[...])
        o = _project_rows(yn.astype(MM_DTYPE), wo_ref)
        o_ref[...] = o
        x1_ref[...] = x_ref[...] + o * _rstd(o) * pg_ref[...]

    row = lambda c: pl.BlockSpec((tt, c), lambda i: (i, 0))
    return _pallas(
        body, rider, name="mixer_fwd", steps=t // tt,
        in_specs=[row(D_IN), row(D_MODEL)] + _mixer_param_specs(tt)
        + [_const_spec((tt, tt)), _const_spec((1, D_MODEL)), _weight_spec(wog), _const_spec((1, D_MODEL))],
        out_specs=[row(D_MODEL), row(D_MODEL), row(D_A + D_C)],
        out_shape=[jax.ShapeDtypeStruct((t, D_MODEL), F32), jax.ShapeDtypeStruct((t, D_MODEL), F32),
                   jax.ShapeDtypeStruct((t, D_A + D_C), F32)],
        scratch_shapes=[pltpu.VMEM((HR_A + tt, D_A), F32), pltpu.VMEM((HR_C + tt, D_C), F32),
                        pltpu.VMEM((HR_A, D_A), F32), pltpu.VMEM((HR_C, D_C), F32),
                        pltpu.VMEM((N_HEADS_B, tt, tt), MM_DTYPE), pltpu.VMEM((tt, D_B), F32)],
        args=(z, x, *mp, perm, grp_g, wog, post_g))


def mixer_bwd(dx1, o, z, cv, mp, perm, grp_g, wog, post_g, tt, rider=None):
    t = z.shape[0]
    assert t % tt == 0 and tt % CHUNK == 0 and tt >= HR_C, (t, tt)
    steps = t // tt

    def body(dx1_ref, o_ref, z_ref, cv_ref, *rest):
        prm_refs = rest[:9]
        (perm_ref, gg_ref, wo_ref, pg_ref,
         dz_ref, do_ref, yn_ref, dpg_ref, dgg_ref, dcaw_ref, dlng_ref, dlnb_ref, dwm_ref, dbias_ref, dccw_ref, dccb_ref,
         dclg_ref, dclb_ref,
         dq_ext, dyc_ext, dq_first, dyc_first, a_pg, a_gg, a_caw, a_lng, a_lnb, a_ccw, a_ccb, a_clg, a_clb,
         wp_scr, wpt_scr, bias_scr, a_wm, a_bias) = rest[9:]
        i = pl.program_id(0)
        small = (a_pg, a_gg, a_caw, a_lng, a_lnb, a_ccw, a_ccb, a_clg, a_clb)

        @pl.when(i == 0)
        def _():
            for ref in small + (a_wm, a_bias, dq_first, dyc_first):
                ref[...] = jnp.zeros_like(ref)
            _to_tile_order(perm_ref[...], prm_refs[3], wp_scr, False)
            _to_tile_order(perm_ref[...], prm_refs[3], wpt_scr, True)
            bias_scr[...] = _dot_exact(perm_ref[...], prm_refs[4][...], ((1,), (0,)))

        prm = _mixer_prm(prm_refs, wp_scr, bias_scr)
        caw, lng, lnb, wm, bias_p, ccw, ccb, clg, clb = prm

        zv = z_ref[...]
        pa, yg, a, sg = _conv_inputs(zv)
        f = _mixer_forward(zv, prm, cv_ref[:, 0:D_A], cv_ref[:, D_A:])
        gg = gg_ref[...]
        yn, (ra, rb, ro) = _group_norm(f, gg)
        yn_ref[...] = yn.astype(MM_DTYPE)

        ov = o_ref[...]
        dx1v = dx1_ref[...]
        r_o = _rstd(ov)
        pg = pg_ref[...]
        a_pg[...] += _rowsum8(dx1v * ov * r_o)
        do = _rms_bwd(ov, r_o, pg, dx1v).astype(MM_DTYPE)
        do_ref[...] = do
        dyn = jnp.concatenate([_dot_nt(do, wo_ref[j]) for j in range(N_CHIPS)], axis=1)

        dyn_a, dyn_b, dyn_c = dyn[:, 0:D_A], dyn[:, D_A:D_A + D_B], dyn[:, D_A + D_B:]
        ga, gb, gc = gg[:, 0:D_A], gg[:, D_A:D_A + D_B], gg[:, D_A + D_B:]
        a_gg[...] += _rowsum8(jnp.concatenate([dyn_a * f["ya"] * ra, dyn_b * f["yb"] * rb, dyn_c * f["yo"] * ro], axis=1))
        dya = _rms_bwd(f["ya"], ra, ga, dyn_a)
        dyb = _rms_bwd(f["yb"], rb, gb, dyn_b)
        dyo = _rms_bwd(f["yo"], ro, gc, dyn_c)

        dbg = dya * f["q"]
        dq = dya * f["bg"]
        dp = _conv_anticausal(dq_ext, dq, dq_first[...], caw, K_A, tt, x=pa, acc_w=a_caw)
        dq_first[...] = dq[0:HR_A, :]
        dcg = dp * zv[:, 2 * D_A:3 * D_A]
        dxa = dp * zv[:, D_A:2 * D_A]

        du = dyb * f["s"]
        ds = dyb * f["u"]
        dsb = ds.astype(MM_DTYPE)
        head = lax.broadcasted_iota(jnp.int32, (tt, D_B), 1) // HEAD
        a_bias[...] += ds
        parts = []
        for h in range(N_HEADS_B):
            a_wm[h] += _dot_nt(jnp.where(head == h, dsb, jnp.zeros_like(dsb)), f["vnb"])
            parts.append(_dot(wpt_scr[h], dsb))
        dvn = _head_select(parts)
        a_lng[...] += _rowsum8(dvn * f["vh"])
        a_lnb[...] += _rowsum8(dvn)
        dv = _ln_bwd(f["vh"], f["rv"], dvn * lng)
        dzu = du * _gelu_grad(f["zu"])
        dzv = dv * _gelu_grad(f["zv"])

        l, sl = f["l"], f["sl"]
        dl = dyo * (sl * (1.0 + l * (1.0 - sl)))
        a_clg[...] += _rowsum8(dl * f["yh"])
        a_clb[...] += _rowsum8(dl)
        dyc = _ln_bwd(f["yh"], f["rc"], dl * clg)
        a_ccb[...] += _rowsum8(dyc)
        dy = _conv_anticausal(dyc_ext, dyc, dyc_first[...], ccw, K_C, tt, x=yg, acc_w=a_ccw)
        dyc_first[...] = dyc[0:HR_C, :]
        da = dy * sg
        dg = dy * a * sg * (1.0 - sg)

        dz_ref[...] = jnp.concatenate([dbg, dcg, dxa, dzu, dzv, da, dg], axis=1).astype(MM_DTYPE)

        @pl.when(i == steps - 1)
        def _():
            red = lambda ref: jnp.sum(ref[...], axis=0, keepdims=True)
            dpg_ref[...] = red(a_pg)
            dgg_ref[...] = red(a_gg)
            dlng_ref[...] = red(a_lng)
            dlnb_ref[...] = red(a_lnb)
            dccb_ref[...] = red(a_ccb)
            dclg_ref[...] = red(a_clg)
            dclb_ref[...] = red(a_clb)
            dcaw_ref[...] = jnp.sum(a_caw[...], axis=1)
            dccw_ref[...] = jnp.sum(a_ccw[...], axis=1)
            pm = perm_ref[...]
            tril = lax.broadcasted_iota(jnp.int32, (CHUNK, CHUNK), 0) >= lax.broadcasted_iota(jnp.int32, (CHUNK, CHUNK), 1)
            for h in range(N_HEADS_B):
                dwt = _dot_exact(pm, _dot_exact(a_wm[h], pm, ((1,), (0,))), ((0,), (0,)))
                dw = dwt[0:CHUNK, 0:CHUNK]
                for c in range(1, tt // CHUNK):
                    dw = dw + dwt[c * CHUNK:(c + 1) * CHUNK, c * CHUNK:(c + 1) * CHUNK]
                dwm_ref[h] = jnp.where(tril, dw, 0.0)
            dbt = _dot_exact(pm, a_bias[...], ((0,), (0,)))
            db = dbt[0:CHUNK, :]
            for c in range(1, tt // CHUNK):
                db = db + dbt[c * CHUNK:(c + 1) * CHUNK, :]
            dbias_ref[...] = db

    rev = lambda c: pl.BlockSpec((tt, c), lambda i: (steps - 1 - i, 0))
    full = lambda shape: pl.BlockSpec(shape, lambda i: (0,) * len(shape))
    sds = jax.ShapeDtypeStruct
    return _pallas(
        body, rider, name="mixer_bwd", steps=steps,
        in_specs=[rev(D_MODEL), rev(D_MODEL), rev(D_IN), rev(D_A + D_C)] + _mixer_param_specs(tt)
        + [_const_spec((tt, tt)), _const_spec((1, D_MODEL)), _weight_spec(wog), _const_spec((1, D_MODEL))],
        out_specs=[rev(D_IN), rev(D_MODEL), rev(D_MODEL), full((1, D_MODEL)), full((1, D_MODEL)), full((8, D_A)),
                   full((1, D_B)), full((1, D_B)), full((N_HEADS_B, CHUNK, CHUNK)), full((CHUNK, D_B)), full((32, D_C)),
                   full((1, D_C)), full((1, D_C)), full((1, D_C))],
        out_shape=[sds((t, D_IN), MM_DTYPE), sds((t, D_MODEL), MM_DTYPE), sds((t, D_MODEL), MM_DTYPE),
                   sds((1, D_MODEL), F32), sds((1, D_MODEL), F32), sds((8, D_A), F32), sds((1, D_B), F32), sds((1, D_B), F32),
                   sds((N_HEADS_B, CHUNK, CHUNK), F32), sds((CHUNK, D_B), F32), sds((32, D_C), F32), sds((1, D_C), F32),
                   sds((1, D_C), F32), sds((1, D_C), F32)],
        scratch_shapes=[pltpu.VMEM((tt + HR_A, D_A), F32), pltpu.VMEM((tt + HR_C, D_C), F32),
                        pltpu.VMEM((HR_A, D_A), F32), pltpu.VMEM((HR_C, D_C), F32),
                        pltpu.VMEM((8, D_MODEL), F32), pltpu.VMEM((8, D_MODEL), F32), pltpu.VMEM((8, 8, D_A), F32),
                        pltpu.VMEM((8, D_B), F32), pltpu.VMEM((8, D_B), F32), pltpu.VMEM((32, 8, D_C), F32),
                        pltpu.VMEM((8, D_C), F32), pltpu.VMEM((8, D_C), F32), pltpu.VMEM((8, D_C), F32),
                        pltpu.VMEM((N_HEADS_B, tt, tt), MM_DTYPE), pltpu.VMEM((N_HEADS_B, tt, tt), MM_DTYPE),
                        pltpu.VMEM((tt, D_B), F32), pltpu.VMEM((N_HEADS_B, tt, tt), F32), pltpu.VMEM((tt, D_B), F32)],
        args=(dx1, o, z, cv, *mp, perm, grp_g, wog, post_g))


def _fetch_row_blocks(wg_ref, w_scr, sems):
    r = wg_ref.shape[1]
    copies = [pltpu.make_async_copy(wg_ref.at[j], w_scr.at[pl.ds(r * j, r), :], sems.at[j]) for j in range(N_CHIPS)]
    for cp in copies:
        cp.start()
    for cp in copies:
        cp.wait()


def _ffn_conv(ext, cw, c0, cn, tt):
    acc = cw[0:1, c0:c0 + cn] * ext[0:tt, c0:c0 + cn]
    for k in range(1, K_F):
        acc = acc + cw[k:k + 1, c0:c0 + cn] * ext[8 * k:8 * k + tt, c0:c0 + cn]
    return acc


def ffn_fwd(up0, x1, cw, wdg, post_g, tt, rider=None):
    t = up0.shape[0]
    assert t % tt == 0, (t, tt)
    cn = _col_chunk(D_FF)

    def body(up0_ref, x1_ref, cw_ref, wdg_ref, pg_ref, d_ref, x2_ref, ext, last, wd_ref, sems):
        i = pl.program_id(0)

        @pl.when(i == 0)
        def _():
            _fetch_row_blocks(wdg_ref, wd_ref, sems)
            last[...] = jnp.zeros_like(last)

        ext[HR_F:HR_F + tt, :] = up0_ref[...]
        ext[0:HR_F, :] = _halo_before(up0_ref[tt - HR_F:, :], last[...])
        last[...] = up0_ref[tt - HR_F:, :]
        cwv = cw_ref[...]
        d = jnp.zeros((tt, D_MODEL), F32)
        for c0 in range(0, D_FF, cn):
            gate = _ffn_conv(ext, cwv, c0, cn, tt)
            val = _ffn_conv(ext, cwv, D_FF + c0, cn, tt)
            act = (gate * jax.nn.sigmoid(gate) * val).astype(MM_DTYPE)
            d = d + _dot(act, wd_ref[c0:c0 + cn, :])
        d_ref[...] = d
        x2_ref[...] = x1_ref[...] + d * _rstd(d) * pg_ref[...]

    row = lambda c: pl.BlockSpec((tt, c), lambda i: (i, 0))
    return _pallas(
        body, rider, name="ffn_fwd", steps=t // tt,
        in_specs=[row(2 * D_FF), row(D_MODEL), _const_spec((8, 2 * D_FF)), _ANY, _const_spec((1, D_MODEL))],
        out_specs=[row(D_MODEL), row(D_MODEL)],
        out_shape=[jax.ShapeDtypeStruct((t, D_MODEL), F32), jax.ShapeDtypeStruct((t, D_MODEL), F32)],
        scratch_shapes=[pltpu.VMEM((HR_F + tt, 2 * D_FF), F32), pltpu.VMEM((HR_F, 2 * D_FF), F32),
                        pltpu.VMEM((D_FF, D_MODEL), MM_DTYPE), pltpu.SemaphoreType.DMA((N_CHIPS,))],
        args=(up0, x1, cw, wdg, post_g))


def ffn_bwd(dx2, d, up0, cw, wdg, post_g, tt, rider=None):
    t = up0.shape[0]
    assert t % tt == 0, (t, tt)
    steps = t // tt
    hb = tt // HR_F
    cn = _col_chunk(D_FF)

    def body(dx2_ref, d_ref, up0_ref, uh_ref, cw_ref, wdg_ref, pg_ref,
             dd_ref, act_ref, dup0_ref, dpg_ref, dcw_ref, ext, dup_ext, first, a_pg, a_cw, wd_ref, sems):
        i = pl.program_id(0)
        tile = steps - 1 - i

        @pl.when(i == 0)
        def _():
            _fetch_row_blocks(wdg_ref, wd_ref, sems)
            a_pg[...] = jnp.zeros_like(a_pg)
            a_cw[...] = jnp.zeros_like(a_cw)
            first[...] = jnp.zeros_like(first)

        ext[HR_F:HR_F + tt, :] = up0_ref[...]
        ext[0:HR_F, :] = _halo_before(up0_ref[tt - HR_F:, :], jnp.where(tile > 0, uh_ref[...], 0.0))
        cwv = cw_ref[...]
        dv = d_ref[...]
        dx2v = dx2_ref[...]
        r = _rstd(dv)
        a_pg[...] += _rowsum8(dx2v * dv * r)
        dd = _rms_bwd(dv, r, pg_ref[...], dx2v).astype(MM_DTYPE)
        dd_ref[...] = dd
        for c0 in range(0, D_FF, cn):
            gate = _ffn_conv(ext, cwv, c0, cn, tt)
            val = _ffn_conv(ext, cwv, D_FF + c0, cn, tt)
            sg = jax.nn.sigmoid(gate)
            sl = gate * sg
            act_ref[:, c0:c0 + cn] = (sl * val).astype(MM_DTYPE)
            da = _dot_nt(dd, wd_ref[c0:c0 + cn, :])
            dup_ext[0:tt, c0:c0 + cn] = da * val * (sg * (1.0 + gate * (1.0 - sg)))
            dup_ext[0:tt, D_FF + c0:D_FF + c0 + cn] = da * sl
        dup_ext[tt:tt + HR_F, :] = _halo_after(dup_ext[0:HR_F, :], first[...])
        first[...] = dup_ext[0:HR_F, :]
        for c0 in range(0, 2 * D_FF, cn):
            x = up0_ref[:, c0:c0 + cn]
            acc = None
            for k in range(K_F):
                off = 8 * (K_F - 1 - k)
                ld = dup_ext[off:off + tt, c0:c0 + cn]
                term = cwv[k:k + 1, c0:c0 + cn] * ld
                acc = term if acc is None else acc + term
                a_cw[k, :, c0:c0 + cn] += _rowsum8(ld * x)
            dup0_ref[:, c0:c0 + cn] = acc.astype(MM_DTYPE)

        @pl.when(i == steps - 1)
        def _():
            dpg_ref[...] = jnp.sum(a_pg[...], axis=0, keepdims=True)
            dcw_ref[...] = jnp.sum(a_cw[...], axis=1)

    rev = lambda c: pl.BlockSpec((tt, c), lambda i: (steps - 1 - i, 0))
    halo = pl.BlockSpec((HR_F, 2 * D_FF), lambda i: (jnp.maximum((steps - 1 - i) * hb - 1, 0), 0))
    full = lambda shape: pl.BlockSpec(shape, lambda i: (0,) * len(shape))
    sds = jax.ShapeDtypeStruct
    return _pallas(
        body, rider, name="ffn_bwd", steps=steps,
        in_specs=[rev(D_MODEL), rev(D_MODEL), rev(2 * D_FF), halo, _const_spec((8, 2 * D_FF)), _ANY,
                  _const_spec((1, D_MODEL))],
        out_specs=[rev(D_MODEL), rev(D_FF), rev(2 * D_FF), full((1, D_MODEL)), full((8, 2 * D_FF))],
        out_shape=[sds((t, D_MODEL), MM_DTYPE), sds((t, D_FF), MM_DTYPE), sds((t, 2 * D_FF), MM_DTYPE),
                   sds((1, D_MODEL), F32), sds((8, 2 * D_FF), F32)],
        scratch_shapes=[pltpu.VMEM((HR_F + tt, 2 * D_FF), F32), pltpu.VMEM((tt + HR_F, 2 * D_FF), F32),
                        pltpu.VMEM((HR_F, 2 * D_FF), F32), pltpu.VMEM((8, D_MODEL), F32),
                        pltpu.VMEM((8, 8, 2 * D_FF), F32), pltpu.VMEM((D_FF, D_MODEL), MM_DTYPE),
                        pltpu.SemaphoreType.DMA((N_CHIPS,))],
        args=(dx2, d, up0, up0, cw, wdg, post_g))


def loss_head(y, target, tm):
    t, d = y.shape
    assert t % tm == 0, (t, tm)
    steps = t // tm

    def body(y_ref, t_ref, dy_ref, loss_ref, acc):
        i = pl.program_id(0)

        @pl.when(i == 0)
        def _():
            acc[...] = jnp.zeros_like(acc)

        diff = y_ref[...] - t_ref[...]
        dy_ref[...] = diff * (1.0 / d)
        acc[...] += _rowsum8(diff * diff)

        @pl.when(i == steps - 1)
        def _():
            loss_ref[...] = (0.5 / d) * jnp.sum(jnp.sum(acc[...], axis=0, keepdims=True), axis=1, keepdims=True)

    row = pl.BlockSpec((tm, d), lambda i: (i, 0))
    return pl.pallas_call(
        body, name="loss_head", grid=(steps,), in_specs=[row, row],
        out_specs=[row, pl.BlockSpec((1, 1), lambda i: (0, 0))],
        out_shape=[jax.ShapeDtypeStruct((t, d), F32), jax.ShapeDtypeStruct((1, 1), F32)],
        scratch_shapes=[pltpu.VMEM((8, d), F32)],
        compiler_params=_params(("arbitrary",)),
    )(y, target)


def adamw(w, g, m, v):
    shape = w.shape
    cols = shape[-1]
    rows = w.size // cols
    tr = next((r for r in (512, 256, 128) if rows % r == 0 and rows > r), rows)
    c1 = 1.0 - ADAM_B1 ** ADAM_STEP
    c2 = 1.0 - ADAM_B2 ** ADAM_STEP

    def body(w_ref, g_ref, m_ref, v_ref, d_ref, nm_ref, nv_ref):
        gv = g_ref[...]
        nm = ADAM_B1 * m_ref[...] + (1.0 - ADAM_B1) * gv
        nv = ADAM_B2 * v_ref[...] + (1.0 - ADAM_B2) * (gv * gv)
        nm_ref[...] = nm
        nv_ref[...] = nv
        d_ref[...] = -ADAM_LR * ((nm / c1) / (jnp.sqrt(nv / c2) + ADAM_EPS) + ADAM_WD * w_ref[...])

    spec = pl.BlockSpec((tr, cols), lambda i: (i, 0))
    out = jax.ShapeDtypeStruct((rows, cols), F32)
    res = pl.pallas_call(
        body, name="adamw", grid=(rows // tr,), in_specs=[spec] * 4, out_specs=[spec] * 3, out_shape=[out] * 3,
        compiler_params=_params(("arbitrary",)),
    )(*[a.reshape(rows, cols) for a in (w, g, m, v)])
    return tuple(r.reshape(shape) for r in res)


def _place():
    return lax.axis_index("x"), lax.axis_index("y"), lax.axis_index("c")


def _other_chips(x, y):
    return [(1 - x, y, 2 * (1 - x) + y), (x, 1 - y, 2 * x + 1 - y), (1 - x, 1 - y, 2 * (1 - x) + 1 - y)]


def _sem_specs(*counts):
    return [pltpu.SemaphoreType.DMA((n,)) for n in counts]


def cast_shard(w, layer, chip):
    _, r, c = w.shape

    def body(chip_ref, w_ref, o_ref):
        del chip_ref
        o_ref[...] = w_ref[...].astype(MM_DTYPE)

    grid_spec = pltpu.PrefetchScalarGridSpec(
        num_scalar_prefetch=1, grid=(1,), in_specs=[pl.BlockSpec((None, r, c), lambda i, chip_ref: (layer, 0, 0))],
        out_specs=pl.BlockSpec((None, r, c), lambda i, chip_ref: (chip_ref[0], 0, 0)))
    return pl.pallas_call(
        body, name="cast_shard", grid_spec=grid_spec, out_shape=jax.ShapeDtypeStruct((N_CHIPS, r, c), MM_DTYPE),
        compiler_params=_params(("arbitrary",)),
    )(jnp.reshape(chip, (1,)).astype(jnp.int32), w)


def _row_half(buf, chip, mine, c):
    rh = buf.shape[1] // 2
    return buf.at[chip, pl.ds(pl.multiple_of((c if mine else 1 - c) * rh, 16), rh), :]


def spread_rider(bufs):
    n = len(bufs)

    def start(rin, rout, sems):
        x, y, c = _place()
        me = 2 * x + y
        for k, (px, py, _) in enumerate(_other_chips(x, y)):
            for i, buf in enumerate(rout):
                part = _row_half(buf, me, True, c)
                pltpu.make_async_remote_copy(
                    src_ref=part, dst_ref=part, send_sem=sems[0].at[n * k + i], recv_sem=sems[1].at[n * k + i],
                    device_id=(px, py, c), device_id_type=MESH_ID).start()

    def wait(rin, rout, sems):
        x, y, c = _place()
        for k, (_, _, pj) in enumerate(_other_chips(x, y)):
            for i, buf in enumerate(rout):
                part = _row_half(buf, pj, True, c)
                pltpu.make_async_remote_copy(
                    src_ref=part, dst_ref=part, send_sem=sems[0].at[n * k + i], recv_sem=sems[1].at[n * k + i],
                    device_id=(x, y, c), device_id_type=MESH_ID).wait()

    shapes = [jax.ShapeDtypeStruct(b.shape, b.dtype) for b in bufs]
    return Rider("spread", list(bufs), shapes, {i: i for i in range(n)}, (3 * n, 3 * n), start, wait)


def pass_rider(bufs):
    n = len(bufs)

    def start(rin, rout, sems):
        x, y, c = _place()
        for k, (_, _, pj) in enumerate(_other_chips(x, y)):
            for i, buf in enumerate(rout):
                part = _row_half(buf, pj, True, c)
                pltpu.make_async_remote_copy(
                    src_ref=part, dst_ref=part, send_sem=sems[0].at[n * k + i], recv_sem=sems[1].at[n * k + i],
                    device_id=(x, y, 1 - c), device_id_type=MESH_ID).start()

    def wait(rin, rout, sems):
        x, y, c = _place()
        for k, (_, _, pj) in enumerate(_other_chips(x, y)):
            for i, buf in enumerate(rout):
                part = _row_half(buf, pj, False, c)
                pltpu.make_async_remote_copy(
                    src_ref=part, dst_ref=part, send_sem=sems[0].at[n * k + i], recv_sem=sems[1].at[n * k + i],
                    device_id=(x, y, 1 - c), device_id_type=MESH_ID).wait()

    shapes = [jax.ShapeDtypeStruct(b.shape, b.dtype) for b in bufs]
    return Rider("pass", list(bufs), shapes, {i: i for i in range(n)}, (3 * n, 3 * n), start, wait)


def both_riders(a, b):
    na, oa, sa = len(a.inputs), len(a.out_shapes), len(a.sems)

    def start(rin, rout, sems):
        a.start(rin[:na], rout[:oa], sems[:sa])
        b.start(rin[na:], rout[oa:], sems[sa:])

    def wait(rin, rout, sems):
        a.wait(rin[:na], rout[:oa], sems[:sa])
        b.wait(rin[na:], rout[oa:], sems[sa:])

    aliases = dict(a.aliases)
    aliases.update({na + i: oa + o for i, o in b.aliases.items()})
    return Rider(a.name + "_" + b.name, a.inputs + b.inputs, a.out_shapes + b.out_shapes, aliases, a.sems + b.sems,
                 start, wait)


def gather_small(small):
    def body(small_ref, out_ref, send, recv, local):
        x, y, c = _place()
        me = 2 * x + y
        chips = _other_chips(x, y)
        own = pltpu.make_async_copy(small_ref, out_ref.at[me], local.at[0])
        own.start()
        sends = [pltpu.make_async_remote_copy(src_ref=small_ref, dst_ref=out_ref.at[me], send_sem=send.at[k],
                                              recv_sem=recv.at[k], device_id=(px, py, c), device_id_type=MESH_ID)
                 for k, (px, py, _) in enumerate(chips)]
        for cp in sends:
            cp.start()
        for k, (_, _, pj) in enumerate(chips):
            pltpu.make_async_remote_copy(src_ref=small_ref, dst_ref=out_ref.at[pj], send_sem=send.at[k], recv_sem=recv.at[k],
                                         device_id=(x, y, c), device_id_type=MESH_ID).wait_recv()
        for cp in sends:
            cp.wait_send()
        own.wait()

    return pl.pallas_call(
        body, name="gather_small", in_specs=[_ANY], out_specs=_ANY,
        out_shape=jax.ShapeDtypeStruct((N_CHIPS,) + small.shape, small.dtype), scratch_shapes=_sem_specs(3, 3, 1),
        compiler_params=pltpu.CompilerParams(has_side_effects=True),
    )(small)


def swap_rider(gs):
    n = len(gs)

    def copies(rin, rout, sems):
        x, y, c = _place()
        out = []
        for i, (g, got) in enumerate(zip(rin, rout)):
            rh = g.shape[1] // 2
            theirs = pl.ds(pl.multiple_of((1 - c) * rh, 8), rh)
            out.append(pltpu.make_async_remote_copy(
                src_ref=g.at[:, theirs, :], dst_ref=got, send_sem=sems[0].at[i], recv_sem=sems[1].at[i],
                device_id=(x, y, 1 - c), device_id_type=MESH_ID))
        return out

    def start(rin, rout, sems):
        for cp in copies(rin, rout, sems):
            cp.start()

    def wait(rin, rout, sems):
        for cp in copies(rin, rout, sems):
            cp.wait()

    shapes = [jax.ShapeDtypeStruct((g.shape[0], g.shape[1] // 2, g.shape[2]), g.dtype) for g in gs]
    return Rider("swap", list(gs), shapes, {}, (n, n), start, wait)


def scatter_rider(sbs):
    n = len(sbs)

    def start(rin, rout, sems):
        x, y, c = _place()
        me = 2 * x + y
        for k, (px, py, pj) in enumerate(_other_chips(x, y)):
            for i, (sb, got) in enumerate(zip(rin, rout)):
                pltpu.make_async_remote_copy(
                    src_ref=sb.at[pj], dst_ref=got.at[me], send_sem=sems[0].at[n * k + i], recv_sem=sems[1].at[n * k + i],
                    device_id=(px, py, c), device_id_type=MESH_ID).start()

    def wait(rin, rout, sems):
        x, y, c = _place()
        for k, (_, _, pj) in enumerate(_other_chips(x, y)):
            for i, (sb, got) in enumerate(zip(rin, rout)):
                cp = pltpu.make_async_remote_copy(
                    src_ref=sb.at[pj], dst_ref=got.at[pj], send_sem=sems[0].at[n * k + i], recv_sem=sems[1].at[n * k + i],
                    device_id=(x, y, c), device_id_type=MESH_ID)
                cp.wait_recv()
                cp.wait_send()

    shapes = [jax.ShapeDtypeStruct(sb.shape, sb.dtype) for sb in sbs]
    return Rider("scatter", list(sbs), shapes, {}, (3 * n, 3 * n), start, wait)


def join_rider(fs, layers):
    n = len(fs)

    def half(i, f, mine, place):
        x, y, c = place
        rh = f.shape[1] // 2
        block = 2 * x + y if layers[i] is None else layers[i]
        return f.at[block, pl.ds(pl.multiple_of((c if mine else 1 - c) * rh, 8), rh), :]

    def start(rin, rout, sems):
        x, y, c = _place()
        for i, f in enumerate(rout):
            part = half(i, f, True, (x, y, c))
            pltpu.make_async_remote_copy(
                src_ref=part, dst_ref=part, send_sem=sems[0].at[i], recv_sem=sems[1].at[i],
                device_id=(x, y, 1 - c), device_id_type=MESH_ID).start()

    def wait(rin, rout, sems):
        x, y, c = _place()
        for i, f in enumerate(rout):
            part = half(i, f, False, (x, y, c))
            pltpu.make_async_remote_copy(
                src_ref=part, dst_ref=part, send_sem=sems[0].at[i], recv_sem=sems[1].at[i],
                device_id=(x, y, 1 - c), device_id_type=MESH_ID).wait()

    shapes = [jax.ShapeDtypeStruct(f.shape, f.dtype) for f in fs]
    return Rider("join", list(fs), shapes, {i: i for i in range(n)}, (n, n), start, wait)


def add_halves(g, got, wire=BF16):
    n, rh, cols = got.shape
    x, y, c = _place()

    def body(p_ref, g_ref, got_ref, sw_ref, sme_ref):
        s = g_ref[...] + got_ref[...]
        sw_ref[...] = s.astype(wire)

        @pl.when(pl.program_id(0) == p_ref[0])
        def _():
            sme_ref[...] = s

    blk = (None, rh, cols)
    grid_spec = pltpu.PrefetchScalarGridSpec(
        num_scalar_prefetch=1, grid=(n,),
        in_specs=[pl.BlockSpec(blk, lambda j, p_ref: (j, p_ref[1], 0)), pl.BlockSpec(blk, lambda j, p_ref: (j, 0, 0))],
        out_specs=[pl.BlockSpec(blk, lambda j, p_ref: (j, 0, 0)), pl.BlockSpec((rh, cols), lambda j, p_ref: (0, 0))])
    wired, own = pl.pallas_call(
        body, name="add_halves", grid_spec=grid_spec,
        out_shape=[jax.ShapeDtypeStruct(got.shape, wire), jax.ShapeDtypeStruct((rh, cols), F32)],
        compiler_params=_params(("arbitrary",)),
    )(jnp.stack([2 * x + y, c]).astype(jnp.int32), g, got)
    return own, wired


def add_chips(own, got, fbuf, block=None):
    n, rh, cols = got.shape
    x, y, c = _place()
    me = 2 * x + y

    def body(p_ref, s_ref, g1_ref, g2_ref, g3_ref, f_ref, o_ref):
        del p_ref, f_ref
        o_ref[...] = s_ref[...] + g1_ref[...].astype(F32) + g2_ref[...].astype(F32) + g3_ref[...].astype(F32)

    blk = (None, rh, cols)

    def other(k):
        return pl.BlockSpec(blk, lambda i, p_ref: ((p_ref[0] + k) % n, 0, 0))

    grid_spec = pltpu.PrefetchScalarGridSpec(
        num_scalar_prefetch=1, grid=(1,),
        in_specs=[pl.BlockSpec((rh, cols), lambda i, p_ref: (0, 0)), other(1), other(2), other(3), _ANY],
        out_specs=pl.BlockSpec(blk, lambda i, p_ref: (p_ref[2], p_ref[1], 0)))
    return pl.pallas_call(
        body, name="add_chips", grid_spec=grid_spec, out_shape=jax.ShapeDtypeStruct(fbuf.shape, F32),
        input_output_aliases={5: 0}, compiler_params=_params(("arbitrary",)),
    )(jnp.stack([me, c, me if block is None else block]).astype(jnp.int32), own, got, got, got, fbuf)


def _pack(arrays, rows):
    flat = jnp.concatenate([a.reshape(-1) for a in arrays])
    return jnp.pad(flat, (0, rows * LANES - flat.size)).reshape(rows, LANES)


def _unpack(buf, shapes):
    flat = buf.reshape(-1)
    out, at = [], 0
    for s in shapes:
        n = math.prod(s)
        out.append(flat[at:at + n].reshape(s))
        at += n
    return out


CONV_SHARDS = [(DEPTH, K_A, D_A // N_CHIPS), (DEPTH, K_C, D_C // N_CHIPS), (DEPTH, K_F, 2 * D_FF // N_CHIPS)]
CONV_ROWS = 32
SMALL_ROWS = 640


def _join_cols(g):
    n, l, r, c = g.shape
    return jnp.transpose(g, (1, 2, 0, 3)).reshape(l, r, n * c)


BIG = ["w_in", "w_out", "w_up", "w_down"]
TILE_MM = 512
TILE_EW = 256


def _pad_rows(a, rows):
    return jnp.pad(a, ((0, rows - a.shape[0]), (0, 0)))


def _row(a):
    return a.reshape(1, -1)


def _tile_perm(tt):
    p = lax.broadcasted_iota(jnp.int32, (tt, tt), 0)
    tok = lax.broadcasted_iota(jnp.int32, (tt, tt), 1)
    return ((tt // 8) * (p % 8) + p // 8 == tok).astype(F32)


def _layer_params(wl, tt):
    n = tt // CHUNK
    tril = jnp.tril(jnp.ones((CHUNK, CHUNK), bool))
    wm = jnp.where(tril[None], wl["sgu_w"], 0.0)
    eye = jnp.eye(n, dtype=F32)
    wt = (eye[None, :, None, :, None] * wm[:, None, :, None, :]).reshape(N_HEADS_B, tt, tt)
    bias_e = jnp.repeat(wl["sgu_b"].T, HEAD, axis=1)
    return (_pad_rows(wl["conv_a_w"], 8), _row(wl["sgu_ln_g"]), _row(wl["sgu_ln_b"]), wt.astype(MM_DTYPE),
            jnp.tile(bias_e, (n, 1)), _pad_rows(wl["conv_c_w"], 32), _row(wl["conv_c_b"]), _row(wl["conv_ln_g"]),
            _row(wl["conv_ln_b"]))


def layer_fwd(x, wl, gw, nxt=None, tm=TILE_MM, tt=TILE_EW):
    mp = _layer_params(wl, tt)
    ride = pass_rider([gw["w_down"]]) if gw.get("pass_down") else None
    (z, h), done = norm_matmul(x, _row(wl["pre_mix_g"]), gw["w_in"], tm, rider=ride)
    gw = {n: (done[0] if ride and n == "w_down" else gw[n]) for n in BIG}
    ride = spread_rider([nxt["w_in"], nxt["w_out"]]) if nxt else None
    (o, x1, cv), done = mixer_fwd(z, x, mp, _tile_perm(tt), _row(wl["grp_norm_g"]), gw["w_out"], _row(wl["post_mix_g"]), tt,
                                  rider=ride)
    ride = both_riders(spread_rider([nxt["w_up"]]), pass_rider(list(done))) if nxt else None
    (up0, h2), done = norm_matmul(x1, _row(wl["pre_ffn_g"]), gw["w_up"], tt, rider=ride)
    if nxt:
        nxt = dict(nxt, w_up=done[0], w_in=done[1], w_out=done[2])
        ride = both_riders(spread_rider([nxt["w_down"]]), pass_rider([nxt["w_up"]]))
    (d, x2), done = ffn_fwd(up0, x1, _pad_rows(wl["ffn_conv_w"], 8), gw["w_down"], _row(wl["post_ffn_g"]), tt, rider=ride)
    if nxt:
        nxt = dict(nxt, w_down=done[0], w_up=done[1], pass_down=True)
    return x2, dict(x=x, z=z, h=h, o=o, x1=x1, up0=up0, h2=h2, d=d, cv=cv, gw=gw), nxt


WIDE = ["w_up", "w_down"]
NARROW = ["w_in", "w_out"]


def layer_bwd(dx2, wl, layer, sv, pend=None, exchange=True, tm=TILE_MM, tt=TILE_EW):
    mp = _layer_params(wl, tt)
    gw = sv["gw"]
    tk = min(512, dx2.shape[0])
    at = {n: BIG.index(n) for n in BIG}
    g = {}
    ride = scatter_rider([sw for _, sw in pend["narrow"]]) if pend else None
    (dd, act, dup0, dpg, dcw), arrived = ffn_bwd(dx2, sv["d"], sv["up0"], _pad_rows(wl["ffn_conv_w"], 8), gw["w_down"],
                                                 _row(wl["post_ffn_g"]), tt, rider=ride)
    fbuf = list(pend["fbuf"]) if pend else grad_buffers()
    if pend:
        for n, (own, _), got in zip(NARROW, pend["narrow"], arrived):
            fbuf[at[n]] = add_chips(own, got, fbuf[at[n]], pend["layer"])
    g["post_ffn_g"] = dpg[0]
    g["ffn_conv_w"] = dcw[:K_F]
    gl = {}
    gl["w_down"] = matmul_tn_down(act, dd, tk)
    gl["w_up"] = matmul_tn_blocks(sv["h2"], dup0, D_MODEL, 2 * D_FF // N_CHIPS, tk, by_rows=False)
    ride = swap_rider([gl[n] for n in WIDE]) if exchange else None
    (dx1, dg), got = matmul_nt_norm_bwd(dup0, gw["w_up"], sv["x1"], _row(wl["pre_ffn_g"]), dx2, tm, rider=ride)
    g["pre_ffn_g"] = dg[0]
    wide = [add_halves(gl[n], gt) for n, gt in zip(WIDE, got)] if exchange else None
    ride = scatter_rider([sw for _, sw in wide]) if exchange else None
    if pend:
        ride = both_riders(join_rider(fbuf, [pend["layer"]] * len(fbuf)), ride)
    (dz, do, yn, dpg, dgg, dcaw, dlng, dlnb, dwm, dbias, dccw, dccb, dclg, dclb), rode = mixer_bwd(
        dx1, sv["o"], sv["z"], sv["cv"], mp, _tile_perm(tt), _row(wl["grp_norm_g"]), gw["w_out"],
        _row(wl["post_mix_g"]), tt, rider=ride)
    if exchange:
        fbuf, arrived = (list(rode[:len(BIG)]), rode[len(BIG):]) if pend else (fbuf, rode)
        for n, (own, _), got in zip(WIDE, wide, arrived):
            fbuf[at[n]] = add_chips(own, got, fbuf[at[n]], layer)
    g["post_mix_g"] = dpg[0]
    g["grp_norm_g"] = dgg[0]
    g["conv_a_w"] = dcaw[:K_A]
    g["sgu_ln_g"] = dlng[0]
    g["sgu_ln_b"] = dlnb[0]
    g["sgu_w"] = dwm
    g["sgu_b"] = jnp.sum(dbias.reshape(CHUNK, N_HEADS_B, HEAD), axis=2).T
    g["conv_c_w"] = dccw[:K_C]
    g["conv_c_b"] = dccb[0]
    g["conv_ln_g"] = dclg[0]
    g["conv_ln_b"] = dclb[0]
    gl["w_out"] = matmul_tn_blocks(yn, do, D_MODEL // N_CHIPS, D_MODEL, tk, by_rows=True)
    gl["w_in"] = matmul_tn_in(sv["h"], dz, tk)
    ride = swap_rider([gl[n] for n in NARROW]) if exchange else None
    (dx, dg), got = matmul_nt_norm_bwd(dz, gw["w_in"], sv["x"], _row(wl["pre_mix_g"]), dx1, tm, rider=ride)
    g["pre_mix_g"] = dg[0]
    if not exchange:
        return dx, g, gl
    narrow = [add_halves(gl[n], gt) for n, gt in zip(NARROW, got)]
    return dx, g, dict(narrow=narrow, fbuf=fbuf, layer=layer)


def grad_buffers():
    return [lax.empty(s, F32) for s in ((DEPTH, D_MODEL, D_IN // N_CHIPS), (DEPTH, D_MODEL // N_CHIPS, D_MODEL),
                                        (DEPTH, D_MODEL, 2 * D_FF // N_CHIPS), (DEPTH, D_FF // N_CHIPS, D_MODEL))]


CONV = ["conv_a_w", "conv_c_w", "ffn_conv_w"]
REPL = ["pre_mix_g", "sgu_ln_g", "sgu_ln_b", "sgu_w", "sgu_b", "conv_c_b", "conv_ln_g", "conv_ln_b", "grp_norm_g",
        "post_mix_g", "pre_ffn_g", "post_ffn_g"]
WEIGHTS = ["pre_mix_g", "w_in", "conv_a_w", "sgu_ln_g", "sgu_ln_b", "sgu_w", "sgu_b", "conv_c_w", "conv_c_b", "conv_ln_g",
           "conv_ln_b", "grp_norm_g", "w_out", "post_mix_g", "pre_ffn_g", "w_up", "ffn_conv_w", "w_down", "post_ffn_g"]


def kernel(x, pre_mix_g, w_in, conv_a_w, sgu_ln_g, sgu_ln_b, sgu_w, sgu_b, conv_c_w, conv_c_b, conv_ln_g, conv_ln_b, grp_norm_g, w_out, post_mix_g, pre_ffn_g, w_up, ffn_conv_w, w_down, post_ffn_g, loss_target, m_pre_mix_g, m_w_in, m_conv_a_w, m_sgu_ln_g, m_sgu_ln_b, m_sgu_w, m_sgu_b, m_conv_c_w, m_conv_c_b, m_conv_ln_g, m_conv_ln_b, m_grp_norm_g, m_w_out, m_post_mix_g, m_pre_ffn_g, m_w_up, m_ffn_conv_w, m_w_down, m_post_ffn_g, v_pre_mix_g, v_w_in, v_conv_a_w, v_sgu_ln_g, v_sgu_ln_b, v_sgu_w, v_sgu_b, v_conv_c_w, v_conv_c_b, v_conv_ln_g, v_conv_ln_b, v_grp_norm_g, v_w_out, v_post_mix_g, v_pre_ffn_g, v_w_up, v_ffn_conv_w, v_w_down, v_post_ffn_g):
    w = dict(pre_mix_g=pre_mix_g, w_in=w_in, conv_a_w=conv_a_w, sgu_ln_g=sgu_ln_g, sgu_ln_b=sgu_ln_b, sgu_w=sgu_w, sgu_b=sgu_b,
             conv_c_w=conv_c_w, conv_c_b=conv_c_b, conv_ln_g=conv_ln_g, conv_ln_b=conv_ln_b, grp_norm_g=grp_norm_g,
             w_out=w_out, post_mix_g=post_mix_g, pre_ffn_g=pre_ffn_g, w_up=w_up, ffn_conv_w=ffn_conv_w, w_down=w_down,
             post_ffn_g=post_ffn_g)
    m = dict(pre_mix_g=m_pre_mix_g, w_in=m_w_in, conv_a_w=m_conv_a_w, sgu_ln_g=m_sgu_ln_g, sgu_ln_b=m_sgu_ln_b,
             sgu_w=m_sgu_w, sgu_b=m_sgu_b, conv_c_w=m_conv_c_w, conv_c_b=m_conv_c_b, conv_ln_g=m_conv_ln_g,
             conv_ln_b=m_conv_ln_b, grp_norm_g=m_grp_norm_g, w_out=m_w_out, post_mix_g=m_post_mix_g,
             pre_ffn_g=m_pre_ffn_g, w_up=m_w_up, ffn_conv_w=m_ffn_conv_w, w_down=m_w_down, post_ffn_g=m_post_ffn_g)
    v = dict(pre_mix_g=v_pre_mix_g, w_in=v_w_in, conv_a_w=v_conv_a_w, sgu_ln_g=v_sgu_ln_g, sgu_ln_b=v_sgu_ln_b,
             sgu_w=v_sgu_w, sgu_b=v_sgu_b, conv_c_w=v_conv_c_w, conv_c_b=v_conv_c_b, conv_ln_g=v_conv_ln_g,
             conv_ln_b=v_conv_ln_b, grp_norm_g=v_grp_norm_g, w_out=v_w_out, post_mix_g=v_post_mix_g,
             pre_ffn_g=v_pre_ffn_g, w_up=v_w_up, ffn_conv_w=v_ffn_conv_w, w_down=v_w_down, post_ffn_g=v_post_ffn_g)
    chip = 2 * lax.axis_index("x") + lax.axis_index("y")

    convs = gather_small(_pack([w[n] for n in CONV], CONV_ROWS))
    gws = [{n: cast_shard(w[n], layer, chip) for n in BIG} for layer in range(DEPTH)]
    first = run_rider(pass_rider(run_rider(spread_rider([gws[0][n] for n in BIG]))))
    gws[0] = dict(zip(BIG, first))
    cparts = [_unpack(convs[j], CONV_SHARDS) for j in range(N_CHIPS)]
    full = dict(w)
    for i, n in enumerate(CONV):
        full[n] = _join_cols(jnp.stack([p[i] for p in cparts]))

    xc = to_tiles(x[0], TILE_EW)
    saved = []
    for layer in range(DEPTH):
        nxt = gws[layer + 1] if layer + 1 < DEPTH else None
        xc, sv, nxt = layer_fwd(xc, {n: full[n][layer] for n in REPL + CONV}, gws[layer], nxt)
        if nxt:
            gws[layer + 1] = nxt
        saved.append(sv)
    dxc, loss_part = loss_head(xc, to_tiles(loss_target[0], TILE_EW), TILE_MM)
    loss = lax.psum(loss_part[0, 0], ("x", "y", "c"))
    small = [None] * DEPTH
    pend = None
    for layer in reversed(range(DEPTH)):
        dxc, small[layer], pend = layer_bwd(dxc, {n: full[n][layer] for n in REPL + CONV}, layer, saved[layer], pend)
    grads = {n: jnp.stack([small[layer][n] for layer in range(DEPTH)]) for n in REPL + CONV}

    gsmall = _pack([grads[n] for n in REPL + CONV], SMALL_ROWS).reshape(N_CHIPS, SMALL_ROWS // N_CHIPS, LANES)
    sums = pend["narrow"] + [add_halves(gsmall, run_rider(swap_rider([gsmall]))[0], wire=F32)]
    arrived = run_rider(scatter_rider([sw for _, sw in sums]))
    fbuf = pend["fbuf"] + [lax.empty(gsmall.shape, F32)]
    for i, blk, (own, _), got in zip([BIG.index(n) for n in NARROW] + [len(BIG)], [0, 0, None], sums, arrived):
        fbuf[i] = add_chips(own, got, fbuf[i], blk)
    joined = run_rider(join_rider(fbuf, [0] * len(BIG) + [None]))
    out_g = dict(zip(BIG, joined))
    tot = run_rider(pass_rider(run_rider(spread_rider([joined[len(BIG)]]))))[0].reshape(SMALL_ROWS, LANES)
    shapes = [grads[n].shape for n in REPL + CONV]
    for n, gfull in zip(REPL + CONV, _unpack(tot, shapes)):
        if n in CONV:
            width = gfull.shape[-1] // N_CHIPS
            gfull = lax.dynamic_slice_in_dim(gfull, chip * width, width, axis=2)
        out_g[n] = gfull

    deltas, new_m, new_v = {}, {}, {}
    for n in WEIGHTS:
        deltas[n], new_m[n], new_v[n] = adamw(w[n], out_g[n], m[n], v[n])
    return (loss, from_tiles(dxc, TILE_EW)[None], *[out_g[n] for n in WEIGHTS], *[deltas[n] for n in WEIGHTS], *[new_m[n] for n in WEIGHTS],
            *[new_v[n] for n in WEIGHTS])
```

```python
import math
from typing import Callable, NamedTuple

import jax
import jax.numpy as jnp
from jax import lax
from jax.experimental import pallas as pl
from jax.experimental.pallas import tpu as pltpu

F32 = jnp.float32
BF16 = jnp.bfloat16
MM_DTYPE = BF16

D_MODEL = 1024
DEPTH = 4
D_A = 256
D_B = 384
D_C = 384
D_IN = 3 * D_A + 2 * D_B + 2 * D_C
D_FF = 2816
K_A = 3
K_C = 31
K_F = 3
CHUNK = 128
HEAD = 64
N_HEADS_B = D_B // HEAD
EPS = 1e-6
N_CHIPS = 4

ADAM_LR = 0.001
ADAM_B1 = 0.9
ADAM_B2 = 0.999
ADAM_EPS = 1e-08
ADAM_WD = 0.01
ADAM_STEP = 10

LANES = 1024
VMEM_LIMIT = 56 * 1024 * 1024

MESH_ID = pl.DeviceIdType.MESH
_ANY = pl.BlockSpec(memory_space=pl.ANY)


def _params(sem=None):
    return pltpu.CompilerParams(dimension_semantics=sem, vmem_limit_bytes=VMEM_LIMIT)


def _const_spec(shape):
    nd = len(shape)
    return pl.BlockSpec(shape, lambda *_: (0,) * nd, pipeline_mode=pl.Buffered(1))


def _rowsum8(a):
    r, c = a.shape
    return jnp.sum(a.reshape(r // 8, 8, c), axis=0)


def _rstd(x):
    return lax.rsqrt(jnp.mean(x * x, axis=-1, keepdims=True) + EPS)


def _rms_bwd(x, r, g, dy):
    gdy = g * dy
    return r * gdy - x * (r * r * r) * jnp.mean(gdy * x, axis=-1, keepdims=True)


def _ln_fwd(x):
    mu = jnp.mean(x, axis=-1, keepdims=True)
    xc = x - mu
    r = lax.rsqrt(jnp.mean(xc * xc, axis=-1, keepdims=True) + EPS)
    return xc * r, r


def _ln_bwd(xh, r, dxh):
    return r * (dxh - jnp.mean(dxh, axis=-1, keepdims=True) - xh * jnp.mean(dxh * xh, axis=-1, keepdims=True))


def _gelu(x):
    return 0.5 * x * (1.0 + lax.erf(x * (1.0 / math.sqrt(2.0))))


def _gelu_grad(x):
    cdf = 0.5 * (1.0 + lax.erf(x * (1.0 / math.sqrt(2.0))))
    pdf = jnp.exp(-0.5 * x * x) * (1.0 / math.sqrt(2.0 * math.pi))
    return cdf + x * pdf


def _dot(a, b):
    return jnp.dot(a, b, preferred_element_type=F32)


def _dot_nt(a, b):
    return lax.dot_general(a, b, (((1,), (1,)), ((), ())), preferred_element_type=F32)


def _dot_tn(a, b):
    return lax.dot_general(a, b, (((0,), (0,)), ((), ())), preferred_element_type=F32)


def _col_chunk(n):
    for c in (1408, 1024, 768, 512, 256, 128):
        if n % c == 0:
            return c
    raise ValueError(n)


class Rider(NamedTuple):
    name: str
    inputs: list
    out_shapes: list
    aliases: dict
    sems: tuple
    start: Callable
    wait: Callable


def _pallas(body, rider, *, name, steps, in_specs, out_specs, out_shape, scratch_shapes, args):
    if rider is None:
        res = pl.pallas_call(body, name=name, grid=(steps,), in_specs=in_specs, out_specs=out_specs, out_shape=out_shape,
                             scratch_shapes=scratch_shapes, compiler_params=_params(("arbitrary",)))(*args)
        return res, []
    n_in, n_out, n_scr = len(in_specs), len(out_specs), len(scratch_shapes)
    r_in, r_out = len(rider.inputs), len(rider.out_shapes)

    def wrapped(*refs):
        ins, rin = refs[:n_in], refs[n_in:n_in + r_in]
        at = n_in + r_in
        outs, rout = refs[at:at + n_out], refs[at + n_out:at + n_out + r_out]
        at += n_out + r_out
        scr, rsem = refs[at:at + n_scr], refs[at + n_scr:]

        @pl.when(pl.program_id(0) == 0)
        def _():
            rider.start(rin, rout, rsem)

        body(*ins, *outs, *scr)

        @pl.when(pl.program_id(0) == steps - 1)
        def _():
            rider.wait(rin, rout, rsem)

    res = pl.pallas_call(
        wrapped, name=name + "_" + rider.name, grid=(steps,), in_specs=list(in_specs) + [_ANY] * r_in,
        out_specs=list(out_specs) + [_ANY] * r_out, out_shape=list(out_shape) + list(rider.out_shapes),
        scratch_shapes=list(scratch_shapes) + [pltpu.SemaphoreType.DMA((n,)) for n in rider.sems],
        input_output_aliases={n_in + i: n_out + o for i, o in rider.aliases.items()},
        compiler_params=pltpu.CompilerParams(dimension_semantics=("arbitrary",), vmem_limit_bytes=VMEM_LIMIT,
                                             has_side_effects=True),
    )(*args, *rider.inputs)
    return res[:n_out], res[n_out:]


def run_rider(rider):
    def body(*refs):
        r_in, r_out = len(rider.inputs), len(rider.out_shapes)
        rin, rout, rsem = refs[:r_in], refs[r_in:r_in + r_out], refs[r_in + r_out:]
        rider.start(rin, rout, rsem)
        rider.wait(rin, rout, rsem)

    return pl.pallas_call(
        body, name=rider.name, in_specs=[_ANY] * len(rider.inputs), out_specs=[_ANY] * len(rider.out_shapes),
        out_shape=list(rider.out_shapes), scratch_shapes=[pltpu.SemaphoreType.DMA((n,)) for n in rider.sems],
        input_output_aliases=dict(rider.aliases), compiler_params=pltpu.CompilerParams(has_side_effects=True),
    )(*rider.inputs)


def _weight_spec(wg):
    return _const_spec(wg.shape)


def _join_col_blocks(w_ref, w_scr):
    c = w_ref.shape[2]
    for j in range(N_CHIPS):
        w_scr[:, c * j:c * (j + 1)] = w_ref[j]


def norm_matmul(x, g, wg, tm, rider=None):
    t, d = x.shape
    assert t % tm == 0, (t, tm)
    cw = wg.shape[2]
    n = N_CHIPS * cw
    aligned = cw % 128 == 0
    cn = cw if aligned else _col_chunk(n)

    def body(x_ref, g_ref, w_ref, o_ref, h_ref, *scr):
        if not aligned:
            @pl.when(pl.program_id(0) == 0)
            def _():
                _join_col_blocks(w_ref, scr[0])

        xv = x_ref[...]
        h = (xv * _rstd(xv) * g_ref[...]).astype(MM_DTYPE)
        h_ref[...] = h
        for j, c0 in enumerate(range(0, n, cn)):
            wv = w_ref[j] if aligned else scr[0][:, c0:c0 + cn]
            o_ref[:, c0:c0 + cn] = _dot(h, wv)

    return _pallas(
        body, rider, name="norm_matmul", steps=t // tm,
        in_specs=[pl.BlockSpec((tm, d), lambda i: (i, 0)), _const_spec((1, d)), _weight_spec(wg)],
        out_specs=[pl.BlockSpec((tm, n), lambda i: (i, 0)), pl.BlockSpec((tm, d), lambda i: (i, 0))],
        out_shape=[jax.ShapeDtypeStruct((t, n), F32), jax.ShapeDtypeStruct((t, d), MM_DTYPE)],
        scratch_shapes=[] if aligned else [pltpu.VMEM((d, n), MM_DTYPE)],
        args=(x, g, wg))


def matmul_nt_norm_bwd(gy, wg, x, g, dres, tm, rider=None):
    t, n = gy.shape
    assert t % tm == 0, (t, tm)
    d, cw = wg.shape[1], wg.shape[2]
    aligned = cw % 128 == 0
    cn = cw if aligned else _col_chunk(n)
    steps = t // tm

    def body(gy_ref, w_ref, x_ref, g_ref, dres_ref, dx_ref, dg_ref, acc_ref, *scr):
        i = pl.program_id(0)

        @pl.when(i == 0)
        def _():
            acc_ref[...] = jnp.zeros_like(acc_ref)
            if not aligned:
                _join_col_blocks(w_ref, scr[0])

        dh = jnp.zeros((tm, d), F32)
        for j, c0 in enumerate(range(0, n, cn)):
            wv = w_ref[j] if aligned else scr[0][:, c0:c0 + cn]
            dh = dh + _dot_nt(gy_ref[:, c0:c0 + cn], wv)
        xv = x_ref[...]
        r = _rstd(xv)
        gv = g_ref[...]
        dx_ref[...] = dres_ref[...] + _rms_bwd(xv, r, gv, dh)
        acc_ref[...] += _rowsum8(dh * xv * r)

        @pl.when(i == steps - 1)
        def _():
            dg_ref[...] = jnp.sum(acc_ref[...], axis=0, keepdims=True)

    return _pallas(
        body, rider, name="matmul_nt_norm_bwd", steps=steps,
        in_specs=[pl.BlockSpec((tm, n), lambda i: (i, 0)), _weight_spec(wg), pl.BlockSpec((tm, d), lambda i: (i, 0)),
                  _const_spec((1, d)), pl.BlockSpec((tm, d), lambda i: (i, 0))],
        out_specs=[pl.BlockSpec((tm, d), lambda i: (i, 0)), pl.BlockSpec((1, d), lambda i: (0, 0))],
        out_shape=[jax.ShapeDtypeStruct((t, d), F32), jax.ShapeDtypeStruct((1, d), F32)],
        scratch_shapes=[pltpu.VMEM((8, d), F32)] + ([] if aligned else [pltpu.VMEM((d, n), MM_DTYPE)]),
        args=(gy, wg, x, g, dres))


def matmul_tn_cols(a, b, tk):
    t, r = a.shape
    c = b.shape[1] // N_CHIPS
    assert t % tk == 0 and r % 8 == 0 and c % 128 == 0, (a.shape, b.shape, tk)

    def body(a_ref, b_ref, o_ref):
        @pl.when(pl.program_id(1) == 0)
        def _():
            o_ref[...] = jnp.zeros_like(o_ref)

        o_ref[...] += _dot_tn(a_ref[...], b_ref[...])

    a_spec = pl.BlockSpec((tk, r), lambda j, k: (k, 0))
    b_spec = pl.BlockSpec((tk, c), lambda j, k: (k, j))
    return pl.pallas_call(
        body, name="matmul_tn_cols", grid=(N_CHIPS, t // tk), in_specs=[a_spec, b_spec],
        out_specs=pl.BlockSpec((None, r, c), lambda j, k: (j, 0, 0)),
        out_shape=jax.ShapeDtypeStruct((N_CHIPS, r, c), F32),
        compiler_params=_params(("arbitrary", "arbitrary")),
    )(a, b)


def matmul_tn_down(act, dd, tk):
    t, m = act.shape
    c = dd.shape[1]
    r = m // N_CHIPS
    assert t % tk == 0, (t, tk)
    steps = t // tk

    def body(a_ref, b_ref, o_ref, acc):
        k = pl.program_id(1)

        @pl.when(k == 0)
        def _():
            acc[...] = jnp.zeros_like(acc)

        acc[...] += _dot_tn(a_ref[...], b_ref[...])

        @pl.when(k == steps - 1)
        def _():
            o_ref[0] = acc[0:r, :]
            o_ref[1] = acc[r:2 * r, :]

    return pl.pallas_call(
        body, name="matmul_tn_down", grid=(2, steps),
        in_specs=[pl.BlockSpec((tk, 2 * r), lambda p, k: (k, p)), pl.BlockSpec((tk, c), lambda p, k: (k, 0))],
        out_specs=pl.BlockSpec((2, r, c), lambda p, k: (p, 0, 0)),
        out_shape=jax.ShapeDtypeStruct((N_CHIPS, r, c), F32),
        scratch_shapes=[pltpu.VMEM((2 * r, c), F32)],
        compiler_params=_params(("arbitrary", "arbitrary")),
    )(act, dd)


def matmul_tn_rows(a, b, tk):
    t, m = a.shape
    n = b.shape[1]
    r = m // N_CHIPS
    assert t % tk == 0 and r % 8 == 0, (a.shape, b.shape, tk)
    steps = t // tk

    def body(a_ref, b_ref, o_ref, acc):
        k = pl.program_id(0)

        @pl.when(k == 0)
        def _():
            acc[...] = jnp.zeros_like(acc)

        acc[...] += _dot_tn(a_ref[...], b_ref[...])

        @pl.when(k == steps - 1)
        def _():
            for j in range(N_CHIPS):
                o_ref[j] = acc[r * j:r * (j + 1), :]

    return pl.pallas_call(
        body, name="matmul_tn_rows", grid=(steps,),
        in_specs=[pl.BlockSpec((tk, m), lambda k: (k, 0)), pl.BlockSpec((tk, n), lambda k: (k, 0))],
        out_specs=pl.BlockSpec((N_CHIPS, r, n), lambda k: (0, 0, 0)),
        out_shape=jax.ShapeDtypeStruct((N_CHIPS, r, n), F32),
        scratch_shapes=[pltpu.VMEM((m, n), F32)],
        compiler_params=_params(("arbitrary",)),
    )(a, b)


def matmul_tn_in(h, dz, tk):
    t, d = h.shape
    n = dz.shape[1]
    c = n // N_CHIPS
    assert t % tk == 0, (t, tk)
    steps = t // tk

    def body(a_ref, b_ref, o_ref, acc):
        k = pl.program_id(0)

        @pl.when(k == 0)
        def _():
            acc[...] = jnp.zeros_like(acc)

        acc[...] += _dot_tn(a_ref[...], b_ref[...])

        @pl.when(k == steps - 1)
        def _():
            for j in range(N_CHIPS):
                o_ref[j] = acc[:, c * j:c * (j + 1)]

    return pl.pallas_call(
        body, name="matmul_tn_in", grid=(steps,),
        in_specs=[pl.BlockSpec((tk, d), lambda k: (k, 0)), pl.BlockSpec((tk, n), lambda k: (k, 0))],
        out_specs=pl.BlockSpec((N_CHIPS, d, c), lambda k: (0, 0, 0)),
        out_shape=jax.ShapeDtypeStruct((N_CHIPS, d, c), F32),
        scratch_shapes=[pltpu.VMEM((d, n), F32)],
        compiler_params=_params(("arbitrary",)),
    )(h, dz)


def to_tiles(a, tt):
    t = a.shape[0]
    return a.reshape((t // tt, 8, tt // 8) + a.shape[1:]).swapaxes(1, 2).reshape(a.shape)


def from_tiles(a, tt):
    t = a.shape[0]
    return a.reshape((t // tt, tt // 8, 8) + a.shape[1:]).swapaxes(1, 2).reshape(a.shape)


def _roll_sublanes(a, shift):
    n = a.shape[0] // 8
    return pltpu.roll(a.reshape(n, 8, a.shape[1]), shift, 1).reshape(a.shape)


def _halo_before(cur_last, prev_last):
    sub = lax.broadcasted_iota(jnp.int32, cur_last.shape, 0) % 8
    return jnp.where(sub == 0, _roll_sublanes(prev_last, 1), _roll_sublanes(cur_last, 1))


def _halo_after(cur_first, next_first):
    sub = lax.broadcasted_iota(jnp.int32, cur_first.shape, 0) % 8
    return jnp.where(sub == 7, _roll_sublanes(next_first, 7), _roll_sublanes(cur_first, 7))


def _conv_causal(ext, cur, prev_last, w, taps, tt, cols=None):
    hr = 8 * (taps - 1)
    cs = slice(None) if cols is None else cols
    ext[hr:hr + tt, cs] = cur
    ext[0:hr, cs] = _halo_before(cur[tt - hr:, :], prev_last)
    acc = w[0:1, :] * ext[0:tt, cs]
    for k in range(1, taps):
        acc = acc + w[k:k + 1, :] * ext[8 * k:8 * k + tt, cs]
    return acc


def _conv_anticausal(ext, cur, next_first, w, taps, tt, x=None, acc_w=None, cols=None):
    hr = 8 * (taps - 1)
    cs = slice(None) if cols is None else cols
    ext[0:tt, cs] = cur
    ext[tt:tt + hr, cs] = _halo_after(cur[0:hr, :], next_first)
    acc = None
    for k in range(taps):
        off = 8 * (taps - 1 - k)
        ld = ext[off:off + tt, cs]
        term = w[k:k + 1, :] * ld
        acc = term if acc is None else acc + term
        if x is not None:
            acc_w[k, :, cs] += _rowsum8(ld * x)
    return acc


def _dot_exact(a, b, dims):
    return lax.dot_general(a, b, (dims, ((), ())), precision=lax.Precision.HIGHEST, preferred_element_type=F32)


def _to_tile_order(perm, wt_ref, w_scr, transpose):
    pb = perm.astype(MM_DTYPE)
    for h in range(N_HEADS_B):
        half = (_dot_nt(pb, wt_ref[h]) if transpose else _dot(pb, wt_ref[h])).astype(MM_DTYPE)
        w_scr[h] = _dot_nt(half, pb).astype(MM_DTYPE)


def _project_rows(y, w_ref):
    r = w_ref.shape[1]
    acc = _dot(y[:, 0:r], w_ref[0])
    for j in range(1, N_CHIPS):
        acc = acc + _dot(y[:, r * j:r * (j + 1)], w_ref[j])
    return acc


def _head_select(parts):
    head = lax.broadcasted_iota(jnp.int32, parts[0].shape, 1) // HEAD
    acc = parts[0]
    for h in range(1, N_HEADS_B):
        acc = jnp.where(head == h, parts[h], acc)
    return acc


def _mixer_forward(z, prm, q, yc):
    _, lng, lnb, wm, bias_p, _, _, clg, clb = prm
    bg = z[:, 0:D_A]
    ya = bg * q
    o_b = 3 * D_A
    zu = z[:, o_b:o_b + D_B]
    zv = z[:, o_b + D_B:o_b + 2 * D_B]
    u = _gelu(zu)
    vh, rv = _ln_fwd(_gelu(zv))
    vnb = (vh * lng + lnb).astype(MM_DTYPE)
    s = _head_select([_dot(wm[h], vnb) for h in range(N_HEADS_B)]) + bias_p
    yb = u * s
    yh, rc = _ln_fwd(yc)
    l = yh * clg + clb
    sl = jax.nn.sigmoid(l)
    return dict(bg=bg, q=q, ya=ya, zu=zu, zv=zv, u=u, vh=vh, rv=rv, vnb=vnb, s=s, yb=yb, yh=yh, rc=rc, l=l, sl=sl,
                yo=l * sl)


def _conv_inputs(z):
    o_c = 3 * D_A + 2 * D_B
    a = z[:, o_c:o_c + D_C]
    sg = jax.nn.sigmoid(z[:, o_c + D_C:o_c + 2 * D_C])
    return z[:, D_A:2 * D_A] * z[:, 2 * D_A:3 * D_A], a * sg, a, sg


def _group_norm(f, gg):
    ya, yb, yo = f["ya"], f["yb"], f["yo"]
    ra, rb, ro = _rstd(ya), _rstd(yb), _rstd(yo)
    yn = jnp.concatenate([ya * ra * gg[:, 0:D_A], yb * rb * gg[:, D_A:D_A + D_B], yo * ro * gg[:, D_A + D_B:]], axis=1)
    return yn, (ra, rb, ro)


def _mixer_prm(refs, wp_scr, bias_scr):
    caw_ref, lng_ref, lnb_ref, _, _, ccw_ref, ccb_ref, clg_ref, clb_ref = refs
    wm = [wp_scr[h] for h in range(N_HEADS_B)]
    return (caw_ref[...], lng_ref[...], lnb_ref[...], wm, bias_scr[...], ccw_ref[...], ccb_ref[...], clg_ref[...],
            clb_ref[...])


def _mixer_param_specs(tt):
    return [_const_spec((8, D_A)), _const_spec((1, D_B)), _const_spec((1, D_B)), _const_spec((N_HEADS_B, tt, tt)),
            _const_spec((tt, D_B)), _const_spec((32, D_C)), _const_spec((1, D_C)), _const_spec((1, D_C)),
            _const_spec((1, D_C))]


HR_A = 8 * (K_A - 1)
HR_C = 8 * (K_C - 1)
HR_F = 8 * (K_F - 1)


def mixer_fwd(z, x, mp, perm, grp_g, wog, post_g, tt, rider=None):
    t = z.shape[0]
    assert t % tt == 0 and tt % CHUNK == 0 and tt >= HR_C, (t, tt)

    def body(z_ref, x_ref, *rest):
        prm_refs = rest[:9]
        (perm_ref, gg_ref, wo_ref, pg_ref, o_ref, x1_ref, cv_ref, pa_ext, yg_ext, pa_last, yg_last, wp_scr, bias_scr) = rest[9:]
        i = pl.program_id(0)

        @pl.when(i == 0)
        def _():
            pa_last[...] = jnp.zeros_like(pa_last)
            yg_last[...] = jnp.zeros_like(yg_last)
            _to_tile_order(perm_ref[...], prm_refs[3], wp_scr, False)
            bias_scr[...] = _dot_exact(perm_ref[...], prm_refs[4][...], ((1,), (0,)))

        zv = z_ref[...]
        prm = _mixer_prm(prm_refs, wp_scr, bias_scr)
        pa, yg, _, _ = _conv_inputs(zv)
        q = _conv_causal(pa_ext, pa, pa_last[...], prm[0], K_A, tt)
        yc = _conv_causal(yg_ext, yg, yg_last[...], prm[5], K_C, tt) + prm[6]
        pa_last[...] = pa[tt - HR_A:, :]
        yg_last[...] = yg[tt - HR_C:, :]
        cv_ref[:, 0:D_A] = q
        cv_ref[:, D_A:] = yc
        f = _mixer_forward(zv, prm, q, yc)
        yn, _ = _group_norm(f, gg_ref[...])
        o = _project_rows(yn.astype(MM_DTYPE), wo_ref)
        o_ref[...] = o
        x1_ref[...] = x_ref[...] + o * _rstd(o) * pg_ref[...]

    row = lambda c: pl.BlockSpec((tt, c), lambda i: (i, 0))
    return _pallas(
        body, rider, name="mixer_fwd", steps=t // tt,
        in_specs=[row(D_IN), row(D_MODEL)] + _mixer_param_specs(tt)
        + [_const_spec((tt, tt)), _const_spec((1, D_MODEL)), _weight_spec(wog), _const_spec((1, D_MODEL))],
        out_specs=[row(D_MODEL), row(D_MODEL), row(D_A + D_C)],
        out_shape=[jax.ShapeDtypeStruct((t, D_MODEL), F32), jax.ShapeDtypeStruct((t, D_MODEL), F32),
                   jax.ShapeDtypeStruct((t, D_A + D_C), F32)],
        scratch_shapes=[pltpu.VMEM((HR_A + tt, D_A), F32), pltpu.VMEM((HR_C + tt, D_C), F32),
                        pltpu.VMEM((HR_A, D_A), F32), pltpu.VMEM((HR_C, D_C), F32),
                        pltpu.VMEM((N_HEADS_B, tt, tt), MM_DTYPE), pltpu.VMEM((tt, D_B), F32)],
        args=(z, x, *mp, perm, grp_g, wog, post_g))


def mixer_bwd(dx1, o, z, cv, mp, perm, grp_g, wog, post_g, tt, rider=None):
    t = z.shape[0]
    assert t % tt == 0 and tt % CHUNK == 0 and tt >= HR_C, (t, tt)
    steps = t // tt

    def body(dx1_ref, o_ref, z_ref, cv_ref, *rest):
        prm_refs = rest[:9]
        (perm_ref, gg_ref, wo_ref, pg_ref,
         dz_ref, do_ref, yn_ref, dpg_ref, dgg_ref, dcaw_ref, dlng_ref, dlnb_ref, dwm_ref, dbias_ref, dccw_ref, dccb_ref,
         dclg_ref, dclb_ref,
         dq_ext, dyc_ext, dq_first, dyc_first, a_pg, a_gg, a_caw, a_lng, a_lnb, a_ccw, a_ccb, a_clg, a_clb,
         wp_scr, wpt_scr, bias_scr, a_wm, a_bias) = rest[9:]
        i = pl.program_id(0)
        small = (a_pg, a_gg, a_caw, a_lng, a_lnb, a_ccw, a_ccb, a_clg, a_clb)

        @pl.when(i == 0)
        def _():
            for ref in small + (a_wm, a_bias, dq_first, dyc_first):
                ref[...] = jnp.zeros_like(ref)
            _to_tile_order(perm_ref[...], prm_refs[3], wp_scr, False)
            _to_tile_order(perm_ref[...], prm_refs[3], wpt_scr, True)
            bias_scr[...] = _dot_exact(perm_ref[...], prm_refs[4][...], ((1,), (0,)))

        prm = _mixer_prm(prm_refs, wp_scr, bias_scr)
        caw, lng, lnb, wm, bias_p, ccw, ccb, clg, clb = prm

        zv = z_ref[...]
        pa, yg, a, sg = _conv_inputs(zv)
        f = _mixer_forward(zv, prm, cv_ref[:, 0:D_A], cv_ref[:, D_A:])
        gg = gg_ref[...]
        yn, (ra, rb, ro) = _group_norm(f, gg)
        yn_ref[...] = yn.astype(MM_DTYPE)

        ov = o_ref[...]
        dx1v = dx1_ref[...]
        r_o = _rstd(ov)
        pg = pg_ref[...]
        a_pg[...] += _rowsum8(dx1v * ov * r_o)
        do = _rms_bwd(ov, r_o, pg, dx1v).astype(MM_DTYPE)
        do_ref[...] = do
        dyn = jnp.concatenate([_dot_nt(do, wo_ref[j]) for j in range(N_CHIPS)], axis=1)

        dyn_a, dyn_b, dyn_c = dyn[:, 0:D_A], dyn[:, D_A:D_A + D_B], dyn[:, D_A + D_B:]
        ga, gb, gc = gg[:, 0:D_A], gg[:, D_A:D_A + D_B], gg[:, D_A + D_B:]
        a_gg[...] += _rowsum8(jnp.concatenate([dyn_a * f["ya"] * ra, dyn_b * f["yb"] * rb, dyn_c * f["yo"] * ro], axis=1))
        dya = _rms_bwd(f["ya"], ra, ga, dyn_a)
        dyb = _rms_bwd(f["yb"], rb, gb, dyn_b)
        dyo = _rms_bwd(f["yo"], ro, gc, dyn_c)

        dbg = dya * f["q"]
        dq = dya * f["bg"]
        dp = _conv_anticausal(dq_ext, dq, dq_first[...], caw, K_A, tt, x=pa, acc_w=a_caw)
        dq_first[...] = dq[0:HR_A, :]
        dcg = dp * zv[:, 2 * D_A:3 * D_A]
        dxa = dp * zv[:, D_A:2 * D_A]

        du = dyb * f["s"]
        ds = dyb * f["u"]
        dsb = ds.astype(MM_DTYPE)
        head = lax.broadcasted_iota(jnp.int32, (tt, D_B), 1) // HEAD
        a_bias[...] += ds
        parts = []
        for h in range(N_HEADS_B):
            a_wm[h] += _dot_nt(jnp.where(head == h, dsb, jnp.zeros_like(dsb)), f["vnb"])
            parts.append(_dot(wpt_scr[h], dsb))
        dvn = _head_select(parts)
        a_lng[...] += _rowsum8(dvn * f["vh"])
        a_lnb[...] += _rowsum8(dvn)
        dv = _ln_bwd(f["vh"], f["rv"], dvn * lng)
        dzu = du * _gelu_grad(f["zu"])
        dzv = dv * _gelu_grad(f["zv"])

        l, sl = f["l"], f["sl"]
        dl = dyo * (sl * (1.0 + l * (1.0 - sl)))
        a_clg[...] += _rowsum8(dl * f["yh"])
        a_clb[...] += _rowsum8(dl)
        dyc = _ln_bwd(f["yh"], f["rc"], dl * clg)
        a_ccb[...] += _rowsum8(dyc)
        dy = _conv_anticausal(dyc_ext, dyc, dyc_first[...], ccw, K_C, tt, x=yg, acc_w=a_ccw)
        dyc_first[...] = dyc[0:HR_C, :]
        da = dy * sg
        dg = dy * a * sg * (1.0 - sg)

        dz_ref[...] = jnp.concatenate([dbg, dcg, dxa, dzu, dzv, da, dg], axis=1).astype(MM_DTYPE)

        @pl.when(i == steps - 1)
        def _():
            red = lambda ref: jnp.sum(ref[...], axis=0, keepdims=True)
            dpg_ref[...] = red(a_pg)
            dgg_ref[...] = red(a_gg)
            dlng_ref[...] = red(a_lng)
            dlnb_ref[...] = red(a_lnb)
            dccb_ref[...] = red(a_ccb)
            dclg_ref[...] = red(a_clg)
            dclb_ref[...] = red(a_clb)
            dcaw_ref[...] = jnp.sum(a_caw[...], axis=1)
            dccw_ref[...] = jnp.sum(a_ccw[...], axis=1)
            pm = perm_ref[...]
            tril = lax.broadcasted_iota(jnp.int32, (CHUNK, CHUNK), 0) >= lax.broadcasted_iota(jnp.int32, (CHUNK, CHUNK), 1)
            for h in range(N_HEADS_B):
                dwt = _dot_exact(pm, _dot_exact(a_wm[h], pm, ((1,), (0,))), ((0,), (0,)))
                dw = dwt[0:CHUNK, 0:CHUNK]
                for c in range(1, tt // CHUNK):
                    dw = dw + dwt[c * CHUNK:(c + 1) * CHUNK, c * CHUNK:(c + 1) * CHUNK]
                dwm_ref[h] = jnp.where(tril, dw, 0.0)
            dbt = _dot_exact(pm, a_bias[...], ((0,), (0,)))
            db = dbt[0:CHUNK, :]
            for c in range(1, tt // CHUNK):
                db = db + dbt[c * CHUNK:(c + 1) * CHUNK, :]
            dbias_ref[...] = db

    rev = lambda c: pl.BlockSpec((tt, c), lambda i: (steps - 1 - i, 0))
    full = lambda shape: pl.BlockSpec(shape, lambda i: (0,) * len(shape))
    sds = jax.ShapeDtypeStruct
    return _pallas(
        body, rider, name="mixer_bwd", steps=steps,
        in_specs=[rev(D_MODEL), rev(D_MODEL), rev(D_IN), rev(D_A + D_C)] + _mixer_param_specs(tt)
        + [_const_spec((tt, tt)), _const_spec((1, D_MODEL)), _weight_spec(wog), _const_spec((1, D_MODEL))],
        out_specs=[rev(D_IN), rev(D_MODEL), rev(D_MODEL), full((1, D_MODEL)), full((1, D_MODEL)), full((8, D_A)),
                   full((1, D_B)), full((1, D_B)), full((N_HEADS_B, CHUNK, CHUNK)), full((CHUNK, D_B)), full((32, D_C)),
                   full((1, D_C)), full((1, D_C)), full((1, D_C))],
        out_shape=[sds((t, D_IN), MM_DTYPE), sds((t, D_MODEL), MM_DTYPE), sds((t, D_MODEL), MM_DTYPE),
                   sds((1, D_MODEL), F32), sds((1, D_MODEL), F32), sds((8, D_A), F32), sds((1, D_B), F32), sds((1, D_B), F32),
                   sds((N_HEADS_B, CHUNK, CHUNK), F32), sds((CHUNK, D_B), F32), sds((32, D_C), F32), sds((1, D_C), F32),
                   sds((1, D_C), F32), sds((1, D_C), F32)],
        scratch_shapes=[pltpu.VMEM((tt + HR_A, D_A), F32), pltpu.VMEM((tt + HR_C, D_C), F32),
                        pltpu.VMEM((HR_A, D_A), F32), pltpu.VMEM((HR_C, D_C), F32),
                        pltpu.VMEM((8, D_MODEL), F32), pltpu.VMEM((8, D_MODEL), F32), pltpu.VMEM((8, 8, D_A), F32),
                        pltpu.VMEM((8, D_B), F32), pltpu.VMEM((8, D_B), F32), pltpu.VMEM((32, 8, D_C), F32),
                        pltpu.VMEM((8, D_C), F32), pltpu.VMEM((8, D_C), F32), pltpu.VMEM((8, D_C), F32),
                        pltpu.VMEM((N_HEADS_B, tt, tt), MM_DTYPE), pltpu.VMEM((N_HEADS_B, tt, tt), MM_DTYPE),
                        pltpu.VMEM((tt, D_B), F32), pltpu.VMEM((N_HEADS_B, tt, tt), F32), pltpu.VMEM((tt, D_B), F32)],
        args=(dx1, o, z, cv, *mp, perm, grp_g, wog, post_g))


def _fetch_row_blocks(wg_ref, w_scr, sems):
    r = wg_ref.shape[1]
    copies = [pltpu.make_async_copy(wg_ref.at[j], w_scr.at[pl.ds(r * j, r), :], sems.at[j]) for j in range(N_CHIPS)]
    for cp in copies:
        cp.start()
    for cp in copies:
        cp.wait()


def _ffn_conv(ext, cw, c0, cn, tt):
    acc = cw[0:1, c0:c0 + cn] * ext[0:tt, c0:c0 + cn]
    for k in range(1, K_F):
        acc = acc + cw[k:k + 1, c0:c0 + cn] * ext[8 * k:8 * k + tt, c0:c0 + cn]
    return acc


def ffn_fwd(up0, x1, cw, wdg, post_g, tt, rider=None):
    t = up0.shape[0]
    assert t % tt == 0, (t, tt)
    cn = _col_chunk(D_FF)

    def body(up0_ref, x1_ref, cw_ref, wdg_ref, pg_ref, d_ref, x2_ref, ext, last, wd_ref, sems):
        i = pl.program_id(0)

        @pl.when(i == 0)
        def _():
            _fetch_row_blocks(wdg_ref, wd_ref, sems)
            last[...] = jnp.zeros_like(last)

        ext[HR_F:HR_F + tt, :] = up0_ref[...]
        ext[0:HR_F, :] = _halo_before(up0_ref[tt - HR_F:, :], last[...])
        last[...] = up0_ref[tt - HR_F:, :]
        cwv = cw_ref[...]
        d = jnp.zeros((tt, D_MODEL), F32)
        for c0 in range(0, D_FF, cn):
            gate = _ffn_conv(ext, cwv, c0, cn, tt)
            val = _ffn_conv(ext, cwv, D_FF + c0, cn, tt)
            act = (gate * jax.nn.sigmoid(gate) * val).astype(MM_DTYPE)
            d = d + _dot(act, wd_ref[c0:c0 + cn, :])
        d_ref[...] = d
        x2_ref[...] = x1_ref[...] + d * _rstd(d) * pg_ref[...]

    row = lambda c: pl.BlockSpec((tt, c), lambda i: (i, 0))
    return _pallas(
        body, rider, name="ffn_fwd", steps=t // tt,
        in_specs=[row(2 * D_FF), row(D_MODEL), _const_spec((8, 2 * D_FF)), _ANY, _const_spec((1, D_MODEL))],
        out_specs=[row(D_MODEL), row(D_MODEL)],
        out_shape=[jax.ShapeDtypeStruct((t, D_MODEL), F32), jax.ShapeDtypeStruct((t, D_MODEL), F32)],
        scratch_shapes=[pltpu.VMEM((HR_F + tt, 2 * D_FF), F32), pltpu.VMEM((HR_F, 2 * D_FF), F32),
                        pltpu.VMEM((D_FF, D_MODEL), MM_DTYPE), pltpu.SemaphoreType.DMA((N_CHIPS,))],
        args=(up0, x1, cw, wdg, post_g))


def ffn_bwd(dx2, d, up0, cw, wdg, post_g, tt, rider=None):
    t = up0.shape[0]
    assert t % tt == 0, (t, tt)
    steps = t // tt
    hb = tt // HR_F
    cn = _col_chunk(D_FF)

    def body(dx2_ref, d_ref, up0_ref, uh_ref, cw_ref, wdg_ref, pg_ref,
             dd_ref, act_ref, dup0_ref, dpg_ref, dcw_ref, ext, dup_ext, first, a_pg, a_cw, wd_ref, sems):
        i = pl.program_id(0)
        tile = steps - 1 - i

        @pl.when(i == 0)
        def _():
            _fetch_row_blocks(wdg_ref, wd_ref, sems)
            a_pg[...] = jnp.zeros_like(a_pg)
            a_cw[...] = jnp.zeros_like(a_cw)
            first[...] = jnp.zeros_like(first)

        ext[HR_F:HR_F + tt, :] = up0_ref[...]
        ext[0:HR_F, :] = _halo_before(up0_ref[tt - HR_F:, :], jnp.where(tile > 0, uh_ref[...], 0.0))
        cwv = cw_ref[...]
        dv = d_ref[...]
        dx2v = dx2_ref[...]
        r = _rstd(dv)
        a_pg[...] += _rowsum8(dx2v * dv * r)
        dd = _rms_bwd(dv, r, pg_ref[...], dx2v).astype(MM_DTYPE)
        dd_ref[...] = dd
        for c0 in range(0, D_FF, cn):
            gate = _ffn_conv(ext, cwv, c0, cn, tt)
            val = _ffn_conv(ext, cwv, D_FF + c0, cn, tt)
            sg = jax.nn.sigmoid(gate)
            sl = gate * sg
            act_ref[:, c0:c0 + cn] = (sl * val).astype(MM_DTYPE)
            da = _dot_nt(dd, wd_ref[c0:c0 + cn, :])
            dup_ext[0:tt, c0:c0 + cn] = da * val * (sg * (1.0 + gate * (1.0 - sg)))
            dup_ext[0:tt, D_FF + c0:D_FF + c0 + cn] = da * sl
        dup_ext[tt:tt + HR_F, :] = _halo_after(dup_ext[0:HR_F, :], first[...])
        first[...] = dup_ext[0:HR_F, :]
        for c0 in range(0, 2 * D_FF, cn):
            x = up0_ref[:, c0:c0 + cn]
            acc = None
            for k in range(K_F):
                off = 8 * (K_F - 1 - k)
                ld = dup_ext[off:off + tt, c0:c0 + cn]
                term = cwv[k:k + 1, c0:c0 + cn] * ld
                acc = term if acc is None else acc + term
                a_cw[k, :, c0:c0 + cn] += _rowsum8(ld * x)
            dup0_ref[:, c0:c0 + cn] = acc.astype(MM_DTYPE)

        @pl.when(i == steps - 1)
        def _():
            dpg_ref[...] = jnp.sum(a_pg[...], axis=0, keepdims=True)
            dcw_ref[...] = jnp.sum(a_cw[...], axis=1)

    rev = lambda c: pl.BlockSpec((tt, c), lambda i: (steps - 1 - i, 0))
    halo = pl.BlockSpec((HR_F, 2 * D_FF), lambda i: (jnp.maximum((steps - 1 - i) * hb - 1, 0), 0))
    full = lambda shape: pl.BlockSpec(shape, lambda i: (0,) * len(shape))
    sds = jax.ShapeDtypeStruct
    return _pallas(
        body, rider, name="ffn_bwd", steps=steps,
        in_specs=[rev(D_MODEL), rev(D_MODEL), rev(2 * D_FF), halo, _const_spec((8, 2 * D_FF)), _ANY,
                  _const_spec((1, D_MODEL))],
        out_specs=[rev(D_MODEL), rev(D_FF), rev(2 * D_FF), full((1, D_MODEL)), full((8, 2 * D_FF))],
        out_shape=[sds((t, D_MODEL), MM_DTYPE), sds((t, D_FF), MM_DTYPE), sds((t, 2 * D_FF), MM_DTYPE),
                   sds((1, D_MODEL), F32), sds((8, 2 * D_FF), F32)],
        scratch_shapes=[pltpu.VMEM((HR_F + tt, 2 * D_FF), F32), pltpu.VMEM((tt + HR_F, 2 * D_FF), F32),
                        pltpu.VMEM((HR_F, 2 * D_FF), F32), pltpu.VMEM((8, D_MODEL), F32),
                        pltpu.VMEM((8, 8, 2 * D_FF), F32), pltpu.VMEM((D_FF, D_MODEL), MM_DTYPE),
                        pltpu.SemaphoreType.DMA((N_CHIPS,))],
        args=(dx2, d, up0, up0, cw, wdg, post_g))


def loss_head(y, target, tm):
    t, d = y.shape
    assert t % tm == 0, (t, tm)
    steps = t // tm

    def body(y_ref, t_ref, dy_ref, loss_ref, acc):
        i = pl.program_id(0)

        @pl.when(i == 0)
        def _():
            acc[...] = jnp.zeros_like(acc)

        diff = y_ref[...] - t_ref[...]
        dy_ref[...] = diff * (1.0 / d)
        acc[...] += _rowsum8(diff * diff)

        @pl.when(i == steps - 1)
        def _():
            loss_ref[...] = (0.5 / d) * jnp.sum(jnp.sum(acc[...], axis=0, keepdims=True), axis=1, keepdims=True)

    row = pl.BlockSpec((tm, d), lambda i: (i, 0))
    return pl.pallas_call(
        body, name="loss_head", grid=(steps,), in_specs=[row, row],
        out_specs=[row, pl.BlockSpec((1, 1), lambda i: (0, 0))],
        out_shape=[jax.ShapeDtypeStruct((t, d), F32), jax.ShapeDtypeStruct((1, 1), F32)],
        scratch_shapes=[pltpu.VMEM((8, d), F32)],
        compiler_params=_params(("arbitrary",)),
    )(y, target)


def adamw(w, g, m, v):
    shape = w.shape
    cols = shape[-1]
    rows = w.size // cols
    tr = next((r for r in (512, 256, 128) if rows % r == 0 and rows > r), rows)
    c1 = 1.0 - ADAM_B1 ** ADAM_STEP
    c2 = 1.0 - ADAM_B2 ** ADAM_STEP

    def body(w_ref, g_ref, m_ref, v_ref, d_ref, nm_ref, nv_ref):
        gv = g_ref[...]
        nm = ADAM_B1 * m_ref[...] + (1.0 - ADAM_B1) * gv
        nv = ADAM_B2 * v_ref[...] + (1.0 - ADAM_B2) * (gv * gv)
        nm_ref[...] = nm
        nv_ref[...] = nv
        d_ref[...] = -ADAM_LR * ((nm / c1) / (jnp.sqrt(nv / c2) + ADAM_EPS) + ADAM_WD * w_ref[...])

    spec = pl.BlockSpec((tr, cols), lambda i: (i, 0))
    out = jax.ShapeDtypeStruct((rows, cols), F32)
    res = pl.pallas_call(
        body, name="adamw", grid=(rows // tr,), in_specs=[spec] * 4, out_specs=[spec] * 3, out_shape=[out] * 3,
        compiler_params=_params(("arbitrary",)),
    )(*[a.reshape(rows, cols) for a in (w, g, m, v)])
    return tuple(r.reshape(shape) for r in res)


def _place():
    return lax.axis_index("x"), lax.axis_index("y"), lax.axis_index("c")


def _other_chips(x, y):
    return [(1 - x, y, 2 * (1 - x) + y), (x, 1 - y, 2 * x + 1 - y), (1 - x, 1 - y, 2 * (1 - x) + 1 - y)]


def _sem_specs(*counts):
    return [pltpu.SemaphoreType.DMA((n,)) for n in counts]


def cast_shard(w, layer, chip):
    _, r, c = w.shape

    def body(chip_ref, w_ref, o_ref):
        del chip_ref
        o_ref[...] = w_ref[...].astype(MM_DTYPE)

    grid_spec = pltpu.PrefetchScalarGridSpec(
        num_scalar_prefetch=1, grid=(1,), in_specs=[pl.BlockSpec((None, r, c), lambda i, chip_ref: (layer, 0, 0))],
        out_specs=pl.BlockSpec((None, r, c), lambda i, chip_ref: (chip_ref[0], 0, 0)))
    return pl.pallas_call(
        body, name="cast_shard", grid_spec=grid_spec, out_shape=jax.ShapeDtypeStruct((N_CHIPS, r, c), MM_DTYPE),
        compiler_params=_params(("arbitrary",)),
    )(jnp.reshape(chip, (1,)).astype(jnp.int32), w)


def _row_half(buf, chip, mine, c):
    rh = buf.shape[1] // 2
    return buf.at[chip, pl.ds(pl.multiple_of((c if mine else 1 - c) * rh, 16), rh), :]


def spread_rider(bufs):
    n = len(bufs)

    def start(rin, rout, sems):
        x, y, c = _place()
        me = 2 * x + y
        for k, (px, py, _) in enumerate(_other_chips(x, y)):
            for i, buf in enumerate(rout):
                part = _row_half(buf, me, True, c)
                pltpu.make_async_remote_copy(
                    src_ref=part, dst_ref=part, send_sem=sems[0].at[n * k + i], recv_sem=sems[1].at[n * k + i],
                    device_id=(px, py, c), device_id_type=MESH_ID).start()

    def wait(rin, rout, sems):
        x, y, c = _place()
        for k, (_, _, pj) in enumerate(_other_chips(x, y)):
            for i, buf in enumerate(rout):
                part = _row_half(buf, pj, True, c)
                pltpu.make_async_remote_copy(
                    src_ref=part, dst_ref=part, send_sem=sems[0].at[n * k + i], recv_sem=sems[1].at[n * k + i],
                    device_id=(x, y, c), device_id_type=MESH_ID).wait()

    shapes = [jax.ShapeDtypeStruct(b.shape, b.dtype) for b in bufs]
    return Rider("spread", list(bufs), shapes, {i: i for i in range(n)}, (3 * n, 3 * n), start, wait)


def pass_rider(bufs):
    n = len(bufs)

    def start(rin, rout, sems):
        x, y, c = _place()
        for k, (_, _, pj) in enumerate(_other_chips(x, y)):
            for i, buf in enumerate(rout):
                part = _row_half(buf, pj, True, c)
                pltpu.make_async_remote_copy(
                    src_ref=part, dst_ref=part, send_sem=sems[0].at[n * k + i], recv_sem=sems[1].at[n * k + i],
                    device_id=(x, y, 1 - c), device_id_type=MESH_ID).start()

    def wait(rin, rout, sems):
        x, y, c = _place()
        for k, (_, _, pj) in enumerate(_other_chips(x, y)):
            for i, buf in enumerate(rout):
                part = _row_half(buf, pj, False, c)
                pltpu.make_async_remote_copy(
                    src_ref=part, dst_ref=part, send_sem=sems[0].at[n * k + i], recv_sem=sems[1].at[n * k + i],
                    device_id=(x, y, 1 - c), device_id_type=MESH_ID).wait()

    shapes = [jax.ShapeDtypeStruct(b.shape, b.dtype) for b in bufs]
    return Rider("pass", list(bufs), shapes, {i: i for i in range(n)}, (3 * n, 3 * n), start, wait)


def both_riders(a, b):
    na, oa, sa = len(a.inputs), len(a.out_shapes), len(a.sems)

    def start(rin, rout, sems):
        a.start(rin[:na], rout[:oa], sems[:sa])
        b.start(rin[na:], rout[oa:], sems[sa:])

    def wait(rin, rout, sems):
        a.wait(rin[:na], rout[:oa], sems[:sa])
        b.wait(rin[na:], rout[oa:], sems[sa:])

    aliases = dict(a.aliases)
    aliases.update({na + i: oa + o for i, o in b.aliases.items()})
    return Rider(a.name + "_" + b.name, a.inputs + b.inputs, a.out_shapes + b.out_shapes, aliases, a.sems + b.sems,
                 start, wait)


def gather_small(small):
    def body(small_ref, out_ref, send, recv, local):
        x, y, c = _place()
        me = 2 * x + y
        chips = _other_chips(x, y)
        own = pltpu.make_async_copy(small_ref, out_ref.at[me], local.at[0])
        own.start()
        sends = [pltpu.make_async_remote_copy(src_ref=small_ref, dst_ref=out_ref.at[me], send_sem=send.at[k],
                                              recv_sem=recv.at[k], device_id=(px, py, c), device_id_type=MESH_ID)
                 for k, (px, py, _) in enumerate(chips)]
        for cp in sends:
            cp.start()
        for k, (_, _, pj) in enumerate(chips):
            pltpu.make_async_remote_copy(src_ref=small_ref, dst_ref=out_ref.at[pj], send_sem=send.at[k], recv_sem=recv.at[k],
                                         device_id=(x, y, c), device_id_type=MESH_ID).wait_recv()
        for cp in sends:
            cp.wait_send()
        own.wait()

    return pl.pallas_call(
        body, name="gather_small", in_specs=[_ANY], out_specs=_ANY,
        out_shape=jax.ShapeDtypeStruct((N_CHIPS,) + small.shape, small.dtype), scratch_shapes=_sem_specs(3, 3, 1),
        compiler_params=pltpu.CompilerParams(has_side_effects=True),
    )(small)


def swap_rider(gs):
    n = len(gs)

    def copies(rin, rout, sems):
        x, y, c = _place()
        out = []
        for i, (g, got) in enumerate(zip(rin, rout)):
            rh = g.shape[1] // 2
            theirs = pl.ds(pl.multiple_of((1 - c) * rh, 8), rh)
            out.append(pltpu.make_async_remote_copy(
                src_ref=g.at[:, theirs, :], dst_ref=got, send_sem=sems[0].at[i], recv_sem=sems[1].at[i],
                device_id=(x, y, 1 - c), device_id_type=MESH_ID))
        return out

    def start(rin, rout, sems):
        for cp in copies(rin, rout, sems):
            cp.start()

    def wait(rin, rout, sems):
        for cp in copies(rin, rout, sems):
            cp.wait()

    shapes = [jax.ShapeDtypeStruct((g.shape[0], g.shape[1] // 2, g.shape[2]), g.dtype) for g in gs]
    return Rider("swap", list(gs), shapes, {}, (n, n), start, wait)


def scatter_rider(sbs):
    n = len(sbs)

    def start(rin, rout, sems):
        x, y, c = _place()
        me = 2 * x + y
        for k, (px, py, pj) in enumerate(_other_chips(x, y)):
            for i, (sb, got) in enumerate(zip(rin, rout)):
                pltpu.make_async_remote_copy(
                    src_ref=sb.at[pj], dst_ref=got.at[me], send_sem=sems[0].at[n * k + i], recv_sem=sems[1].at[n * k + i],
                    device_id=(px, py, c), device_id_type=MESH_ID).start()

    def wait(rin, rout, sems):
        x, y, c = _place()
        for k, (_, _, pj) in enumerate(_other_chips(x, y)):
            for i, (sb, got) in enumerate(zip(rin, rout)):
                cp = pltpu.make_async_remote_copy(
                    src_ref=sb.at[pj], dst_ref=got.at[pj], send_sem=sems[0].at[n * k + i], recv_sem=sems[1].at[n * k + i],
                    device_id=(x, y, c), device_id_type=MESH_ID)
                cp.wait_recv()
                cp.wait_send()

    shapes = [jax.ShapeDtypeStruct(sb.shape, sb.dtype) for sb in sbs]
    return Rider("scatter", list(sbs), shapes, {}, (3 * n, 3 * n), start, wait)


def join_rider(fs, layers):
    n = len(fs)

    def half(i, f, mine, place):
        x, y, c = place
        rh = f.shape[1] // 2
        block = 2 * x + y if layers[i] is None else layers[i]
        return f.at[block, pl.ds(pl.multiple_of((c if mine else 1 - c) * rh, 8), rh), :]

    def start(rin, rout, sems):
        x, y, c = _place()
        for i, f in enumerate(rout):
            part = half(i, f, True, (x, y, c))
            pltpu.make_async_remote_copy(
                src_ref=part, dst_ref=part, send_sem=sems[0].at[i], recv_sem=sems[1].at[i],
                device_id=(x, y, 1 - c), device_id_type=MESH_ID).start()

    def wait(rin, rout, sems):
        x, y, c = _place()
        for i, f in enumerate(rout):
            part = half(i, f, False, (x, y, c))
            pltpu.make_async_remote_copy(
                src_ref=part, dst_ref=part, send_sem=sems[0].at[i], recv_sem=sems[1].at[i],
                device_id=(x, y, 1 - c), device_id_type=MESH_ID).wait()

    shapes = [jax.ShapeDtypeStruct(f.shape, f.dtype) for f in fs]
    return Rider("join", list(fs), shapes, {i: i for i in range(n)}, (n, n), start, wait)


def add_halves(gs, gots, wire=BF16):
    m = len(gs)
    x, y, c = _place()

    def body(p_ref, *refs):
        ins, outs = refs[:2 * m], refs[2 * m:]
        for i in range(m):
            s = ins[2 * i][...] + ins[2 * i + 1][...]
            outs[2 * i][...] = s.astype(wire)

            @pl.when(pl.program_id(0) == p_ref[0])
            def _(s=s, own_ref=outs[2 * i + 1]):
                own_ref[...] = s

    in_specs, out_specs, out_shape = [], [], []
    for got in gots:
        n, rh, cols = got.shape
        blk = (None, rh, cols)
        in_specs += [pl.BlockSpec(blk, lambda j, p_ref: (j, p_ref[1], 0)), pl.BlockSpec(blk, lambda j, p_ref: (j, 0, 0))]
        out_specs += [pl.BlockSpec(blk, lambda j, p_ref: (j, 0, 0)), pl.BlockSpec((rh, cols), lambda j, p_ref: (0, 0))]
        out_shape += [jax.ShapeDtypeStruct(got.shape, wire), jax.ShapeDtypeStruct((rh, cols), F32)]
    grid_spec = pltpu.PrefetchScalarGridSpec(num_scalar_prefetch=1, grid=(N_CHIPS,), in_specs=in_specs, out_specs=out_specs)
    res = pl.pallas_call(
        body, name="add_halves", grid_spec=grid_spec, out_shape=out_shape, compiler_params=_params(("arbitrary",)),
    )(jnp.stack([2 * x + y, c]).astype(jnp.int32), *[a for pair in zip(gs, gots) for a in pair])
    return [(res[2 * i + 1], res[2 * i]) for i in range(m)]


def add_chips(owns, gots, fbufs, block=None):
    m = len(owns)
    x, y, c = _place()
    me = 2 * x + y

    def body(p_ref, *refs):
        ins, outs = refs[:5 * m], refs[5 * m:]
        for i in range(m):
            s_ref, g1_ref, g2_ref, g3_ref, _ = ins[5 * i:5 * i + 5]
            outs[i][...] = s_ref[...] + g1_ref[...].astype(F32) + g2_ref[...].astype(F32) + g3_ref[...].astype(F32)

    def other(blk, n, k):
        return pl.BlockSpec(blk, lambda i, p_ref: ((p_ref[0] + k) % n, 0, 0))

    in_specs, out_specs, args = [], [], []
    for own, got, fbuf in zip(owns, gots, fbufs):
        n, rh, cols = got.shape
        blk = (None, rh, cols)
        in_specs += [pl.BlockSpec((rh, cols), lambda i, p_ref: (0, 0)), other(blk, n, 1), other(blk, n, 2), other(blk, n, 3),
                     _ANY]
        out_specs.append(pl.BlockSpec(blk, lambda i, p_ref: (p_ref[2], p_ref[1], 0)))
        args += [own, got, got, got, fbuf]
    grid_spec = pltpu.PrefetchScalarGridSpec(num_scalar_prefetch=1, grid=(1,), in_specs=in_specs, out_specs=out_specs)
    return pl.pallas_call(
        body, name="add_chips", grid_spec=grid_spec, out_shape=[jax.ShapeDtypeStruct(f.shape, F32) for f in fbufs],
        input_output_aliases={5 * i + 5: i for i in range(m)}, compiler_params=_params(("arbitrary",)),
    )(jnp.stack([me, c, me if block is None else block]).astype(jnp.int32), *args)


def _pack(arrays, rows):
    flat = jnp.concatenate([a.reshape(-1) for a in arrays])
    return jnp.pad(flat, (0, rows * LANES - flat.size)).reshape(rows, LANES)


def _unpack(buf, shapes):
    flat = buf.reshape(-1)
    out, at = [], 0
    for s in shapes:
        n = math.prod(s)
        out.append(flat[at:at + n].reshape(s))
        at += n
    return out


CONV_SHARDS = [(DEPTH, K_A, D_A // N_CHIPS), (DEPTH, K_C, D_C // N_CHIPS), (DEPTH, K_F, 2 * D_FF // N_CHIPS)]
CONV_ROWS = 32
SMALL_ROWS = 640


def _join_cols(g):
    n, l, r, c = g.shape
    return jnp.transpose(g, (1, 2, 0, 3)).reshape(l, r, n * c)


BIG = ["w_in", "w_out", "w_up", "w_down"]
TILE_MM = 512
TILE_EW = 256


def _pad_rows(a, rows):
    return jnp.pad(a, ((0, rows - a.shape[0]), (0, 0)))


def _row(a):
    return a.reshape(1, -1)


def _tile_perm(tt):
    p = lax.broadcasted_iota(jnp.int32, (tt, tt), 0)
    tok = lax.broadcasted_iota(jnp.int32, (tt, tt), 1)
    return ((tt // 8) * (p % 8) + p // 8 == tok).astype(F32)


def _layer_params(wl, tt):
    n = tt // CHUNK
    tril = jnp.tril(jnp.ones((CHUNK, CHUNK), bool))
    wm = jnp.where(tril[None], wl["sgu_w"], 0.0)
    eye = jnp.eye(n, dtype=F32)
    wt = (eye[None, :, None, :, None] * wm[:, None, :, None, :]).reshape(N_HEADS_B, tt, tt)
    bias_e = jnp.repeat(wl["sgu_b"].T, HEAD, axis=1)
    return (_pad_rows(wl["conv_a_w"], 8), _row(wl["sgu_ln_g"]), _row(wl["sgu_ln_b"]), wt.astype(MM_DTYPE),
            jnp.tile(bias_e, (n, 1)), _pad_rows(wl["conv_c_w"], 32), _row(wl["conv_c_b"]), _row(wl["conv_ln_g"]),
            _row(wl["conv_ln_b"]))


def layer_fwd(x, wl, gw, nxt=None, tm=TILE_MM, tt=TILE_EW):
    mp = _layer_params(wl, tt)
    ride = pass_rider([gw["w_down"]]) if gw.get("pass_down") else None
    (z, h), done = norm_matmul(x, _row(wl["pre_mix_g"]), gw["w_in"], tm, rider=ride)
    gw = {n: (done[0] if ride and n == "w_down" else gw[n]) for n in BIG}
    ride = spread_rider([nxt["w_in"], nxt["w_out"]]) if nxt else None
    (o, x1, cv), done = mixer_fwd(z, x, mp, _tile_perm(tt), _row(wl["grp_norm_g"]), gw["w_out"], _row(wl["post_mix_g"]), tt,
                                  rider=ride)
    ride = both_riders(spread_rider([nxt["w_up"]]), pass_rider(list(done))) if nxt else None
    (up0, h2), done = norm_matmul(x1, _row(wl["pre_ffn_g"]), gw["w_up"], tt, rider=ride)
    if nxt:
        nxt = dict(nxt, w_up=done[0], w_in=done[1], w_out=done[2])
        ride = both_riders(spread_rider([nxt["w_down"]]), pass_rider([nxt["w_up"]]))
    (d, x2), done = ffn_fwd(up0, x1, _pad_rows(wl["ffn_conv_w"], 8), gw["w_down"], _row(wl["post_ffn_g"]), tt, rider=ride)
    if nxt:
        nxt = dict(nxt, w_down=done[0], w_up=done[1], pass_down=True)
    return x2, dict(x=x, z=z, h=h, o=o, x1=x1, up0=up0, h2=h2, d=d, cv=cv, gw=gw), nxt


WIDE = ["w_up", "w_down"]
NARROW = ["w_in", "w_out"]


def layer_bwd(dx2, wl, layer, sv, pend=None, exchange=True, tm=TILE_MM, tt=TILE_EW):
    mp = _layer_params(wl, tt)
    gw = sv["gw"]
    tk = min(512, dx2.shape[0])
    at = {n: BIG.index(n) for n in BIG}
    g = {}
    ride = scatter_rider([sw for _, sw in pend["narrow"]]) if pend else None
    (dd, act, dup0, dpg, dcw), arrived = ffn_bwd(dx2, sv["d"], sv["up0"], _pad_rows(wl["ffn_conv_w"], 8), gw["w_down"],
                                                 _row(wl["post_ffn_g"]), tt, rider=ride)
    fbuf = list(pend["fbuf"]) if pend else grad_buffers()
    if pend:
        done = add_chips([own for own, _ in pend["narrow"]], arrived, [fbuf[at[n]] for n in NARROW], pend["layer"])
        for n, f in zip(NARROW, done):
            fbuf[at[n]] = f
    g["post_ffn_g"] = dpg[0]
    g["ffn_conv_w"] = dcw[:K_F]
    gl = {}
    gl["w_down"] = matmul_tn_down(act, dd, tk)
    gl["w_up"] = matmul_tn_cols(sv["h2"], dup0, tk)
    ride = swap_rider([gl[n] for n in WIDE]) if exchange else None
    (dx1, dg), got = matmul_nt_norm_bwd(dup0, gw["w_up"], sv["x1"], _row(wl["pre_ffn_g"]), dx2, tm, rider=ride)
    g["pre_ffn_g"] = dg[0]
    wide = add_halves([gl[n] for n in WIDE], got) if exchange else None
    ride = scatter_rider([sw for _, sw in wide]) if exchange else None
    if pend:
        ride = both_riders(join_rider(fbuf, [pend["layer"]] * len(fbuf)), ride)
    (dz, do, yn, dpg, dgg, dcaw, dlng, dlnb, dwm, dbias, dccw, dccb, dclg, dclb), rode = mixer_bwd(
        dx1, sv["o"], sv["z"], sv["cv"], mp, _tile_perm(tt), _row(wl["grp_norm_g"]), gw["w_out"],
        _row(wl["post_mix_g"]), tt, rider=ride)
    if exchange:
        fbuf, arrived = (list(rode[:len(BIG)]), rode[len(BIG):]) if pend else (fbuf, rode)
        done = add_chips([own for own, _ in wide], arrived, [fbuf[at[n]] for n in WIDE], layer)
        for n, f in zip(WIDE, done):
            fbuf[at[n]] = f
    g["post_mix_g"] = dpg[0]
    g["grp_norm_g"] = dgg[0]
    g["conv_a_w"] = dcaw[:K_A]
    g["sgu_ln_g"] = dlng[0]
    g["sgu_ln_b"] = dlnb[0]
    g["sgu_w"] = dwm
    g["sgu_b"] = jnp.sum(dbias.reshape(CHUNK, N_HEADS_B, HEAD), axis=2).T
    g["conv_c_w"] = dccw[:K_C]
    g["conv_c_b"] = dccb[0]
    g["conv_ln_g"] = dclg[0]
    g["conv_ln_b"] = dclb[0]
    gl["w_out"] = matmul_tn_rows(yn, do, tk)
    gl["w_in"] = matmul_tn_in(sv["h"], dz, tk)
    ride = swap_rider([gl[n] for n in NARROW]) if exchange else None
    (dx, dg), got = matmul_nt_norm_bwd(dz, gw["w_in"], sv["x"], _row(wl["pre_mix_g"]), dx1, tm, rider=ride)
    g["pre_mix_g"] = dg[0]
    if not exchange:
        return dx, g, gl
    narrow = add_halves([gl[n] for n in NARROW], got)
    return dx, g, dict(narrow=narrow, fbuf=fbuf, layer=layer)


def grad_buffers():
    return [lax.empty(s, F32) for s in ((DEPTH, D_MODEL, D_IN // N_CHIPS), (DEPTH, D_MODEL // N_CHIPS, D_MODEL),
                                        (DEPTH, D_MODEL, 2 * D_FF // N_CHIPS), (DEPTH, D_FF // N_CHIPS, D_MODEL))]


CONV = ["conv_a_w", "conv_c_w", "ffn_conv_w"]
REPL = ["pre_mix_g", "sgu_ln_g", "sgu_ln_b", "sgu_w", "sgu_b", "conv_c_b", "conv_ln_g", "conv_ln_b", "grp_norm_g",
        "post_mix_g", "pre_ffn_g", "post_ffn_g"]
WEIGHTS = ["pre_mix_g", "w_in", "conv_a_w", "sgu_ln_g", "sgu_ln_b", "sgu_w", "sgu_b", "conv_c_w", "conv_c_b", "conv_ln_g",
           "conv_ln_b", "grp_norm_g", "w_out", "post_mix_g", "pre_ffn_g", "w_up", "ffn_conv_w", "w_down", "post_ffn_g"]


def kernel(x, pre_mix_g, w_in, conv_a_w, sgu_ln_g, sgu_ln_b, sgu_w, sgu_b, conv_c_w, conv_c_b, conv_ln_g, conv_ln_b, grp_norm_g, w_out, post_mix_g, pre_ffn_g, w_up, ffn_conv_w, w_down, post_ffn_g, loss_target, m_pre_mix_g, m_w_in, m_conv_a_w, m_sgu_ln_g, m_sgu_ln_b, m_sgu_w, m_sgu_b, m_conv_c_w, m_conv_c_b, m_conv_ln_g, m_conv_ln_b, m_grp_norm_g, m_w_out, m_post_mix_g, m_pre_ffn_g, m_w_up, m_ffn_conv_w, m_w_down, m_post_ffn_g, v_pre_mix_g, v_w_in, v_conv_a_w, v_sgu_ln_g, v_sgu_ln_b, v_sgu_w, v_sgu_b, v_conv_c_w, v_conv_c_b, v_conv_ln_g, v_conv_ln_b, v_grp_norm_g, v_w_out, v_post_mix_g, v_pre_ffn_g, v_w_up, v_ffn_conv_w, v_w_down, v_post_ffn_g):
    w = dict(pre_mix_g=pre_mix_g, w_in=w_in, conv_a_w=conv_a_w, sgu_ln_g=sgu_ln_g, sgu_ln_b=sgu_ln_b, sgu_w=sgu_w, sgu_b=sgu_b,
             conv_c_w=conv_c_w, conv_c_b=conv_c_b, conv_ln_g=conv_ln_g, conv_ln_b=conv_ln_b, grp_norm_g=grp_norm_g,
             w_out=w_out, post_mix_g=post_mix_g, pre_ffn_g=pre_ffn_g, w_up=w_up, ffn_conv_w=ffn_conv_w, w_down=w_down,
             post_ffn_g=post_ffn_g)
    m = dict(pre_mix_g=m_pre_mix_g, w_in=m_w_in, conv_a_w=m_conv_a_w, sgu_ln_g=m_sgu_ln_g, sgu_ln_b=m_sgu_ln_b,
             sgu_w=m_sgu_w, sgu_b=m_sgu_b, conv_c_w=m_conv_c_w, conv_c_b=m_conv_c_b, conv_ln_g=m_conv_ln_g,
             conv_ln_b=m_conv_ln_b, grp_norm_g=m_grp_norm_g, w_out=m_w_out, post_mix_g=m_post_mix_g,
             pre_ffn_g=m_pre_ffn_g, w_up=m_w_up, ffn_conv_w=m_ffn_conv_w, w_down=m_w_down, post_ffn_g=m_post_ffn_g)
    v = dict(pre_mix_g=v_pre_mix_g, w_in=v_w_in, conv_a_w=v_conv_a_w, sgu_ln_g=v_sgu_ln_g, sgu_ln_b=v_sgu_ln_b,
             sgu_w=v_sgu_w, sgu_b=v_sgu_b, conv_c_w=v_conv_c_w, conv_c_b=v_conv_c_b, conv_ln_g=v_conv_ln_g,
             conv_ln_b=v_conv_ln_b, grp_norm_g=v_grp_norm_g, w_out=v_w_out, post_mix_g=v_post_mix_g,
             pre_ffn_g=v_pre_ffn_g, w_up=v_w_up, ffn_conv_w=v_ffn_conv_w, w_down=v_w_down, post_ffn_g=v_post_ffn_g)
    chip = 2 * lax.axis_index("x") + lax.axis_index("y")

    convs = gather_small(_pack([w[n] for n in CONV], CONV_ROWS))
    gws = [{n: cast_shard(w[n], layer, chip) for n in BIG} for layer in range(DEPTH)]
    first = run_rider(pass_rider(run_rider(spread_rider([gws[0][n] for n in BIG]))))
    gws[0] = dict(zip(BIG, first))
    cparts = [_unpack(convs[j], CONV_SHARDS) for j in range(N_CHIPS)]
    full = dict(w)
    for i, n in enumerate(CONV):
        full[n] = _join_cols(jnp.stack([p[i] for p in cparts]))

    xc = to_tiles(x[0], TILE_EW)
    saved = []
    for layer in range(DEPTH):
        nxt = gws[layer + 1] if layer + 1 < DEPTH else None
        xc, sv, nxt = layer_fwd(xc, {n: full[n][layer] for n in REPL + CONV}, gws[layer], nxt)
        if nxt:
            gws[layer + 1] = nxt
        saved.append(sv)
    dxc, loss_part = loss_head(xc, to_tiles(loss_target[0], TILE_EW), TILE_MM)
    loss = lax.psum(loss_part[0, 0], ("x", "y", "c"))
    small = [None] * DEPTH
    pend = None
    for layer in reversed(range(DEPTH)):
        dxc, small[layer], pend = layer_bwd(dxc, {n: full[n][layer] for n in REPL + CONV}, layer, saved[layer], pend)
    grads = {n: jnp.stack([small[layer][n] for layer in range(DEPTH)]) for n in REPL + CONV}

    gsmall = _pack([grads[n] for n in REPL + CONV], SMALL_ROWS).reshape(N_CHIPS, SMALL_ROWS // N_CHIPS, LANES)
    sums = pend["narrow"] + add_halves([gsmall], run_rider(swap_rider([gsmall])), wire=F32)
    arrived = run_rider(scatter_rider([sw for _, sw in sums]))
    fbuf = list(pend["fbuf"])
    at = [BIG.index(n) for n in NARROW]
    for i, f in zip(at, add_chips([own for own, _ in sums[:2]], arrived[:2], [fbuf[i] for i in at], 0)):
        fbuf[i] = f
    fbuf += add_chips([sums[2][0]], arrived[2:], [lax.empty(gsmall.shape, F32)])
    joined = run_rider(join_rider(fbuf, [0] * len(BIG) + [None]))
    out_g = dict(zip(BIG, joined))
    tot = run_rider(pass_rider(run_rider(spread_rider([joined[len(BIG)]]))))[0].reshape(SMALL_ROWS, LANES)
    shapes = [grads[n].shape for n in REPL + CONV]
    for n, gfull in zip(REPL + CONV, _unpack(tot, shapes)):
        if n in CONV:
            width = gfull.shape[-1] // N_CHIPS
            gfull = lax.dynamic_slice_in_dim(gfull, chip * width, width, axis=2)
        out_g[n] = gfull

    deltas, new_m, new_v = {}, {}, {}
    for n in WEIGHTS:
        deltas[n], new_m[n], new_v[n] = adamw(w[n], out_g[n], m[n], v[n])
    return (loss, from_tiles(dxc, TILE_EW)[None], *[out_g[n] for n in WEIGHTS], *[deltas[n] for n in WEIGHTS], *[new_m[n] for n in WEIGHTS],
            *[new_v[n] for n in WEIGHTS])
```

```python
import math
from typing import Callable, NamedTuple

import jax
import jax.numpy as jnp
from jax import lax
from jax.experimental import pallas as pl
from jax.experimental.pallas import tpu as pltpu

F32 = jnp.float32
BF16 = jnp.bfloat16
MM_DTYPE = BF16

D_MODEL = 1024
DEPTH = 4
D_A = 256
D_B = 384
D_C = 384
D_IN = 3 * D_A + 2 * D_B + 2 * D_C
D_FF = 2816
K_A = 3
K_C = 31
K_F = 3
CHUNK = 128
HEAD = 64
N_HEADS_B = D_B // HEAD
EPS = 1e-6
N_CHIPS = 4

ADAM_LR = 0.001
ADAM_B1 = 0.9
ADAM_B2 = 0.999
ADAM_EPS = 1e-08
ADAM_WD = 0.01
ADAM_STEP = 10

LANES = 1024
VMEM_LIMIT = 56 * 1024 * 1024

MESH_ID = pl.DeviceIdType.MESH
_ANY = pl.BlockSpec(memory_space=pl.ANY)


def _params(sem=None):
    return pltpu.CompilerParams(dimension_semantics=sem, vmem_limit_bytes=VMEM_LIMIT)


def _const_spec(shape):
    nd = len(shape)
    return pl.BlockSpec(shape, lambda *_: (0,) * nd, pipeline_mode=pl.Buffered(1))


def _rowsum8(a):
    r, c = a.shape
    return jnp.sum(a.reshape(r // 8, 8, c), axis=0)


def _rstd(x):
    return lax.rsqrt(jnp.mean(x * x, axis=-1, keepdims=True) + EPS)


def _rms_bwd(x, r, g, dy):
    gdy = g * dy
    return r * gdy - x * (r * r * r) * jnp.mean(gdy * x, axis=-1, keepdims=True)


def _ln_fwd(x):
    mu = jnp.mean(x, axis=-1, keepdims=True)
    xc = x - mu
    r = lax.rsqrt(jnp.mean(xc * xc, axis=-1, keepdims=True) + EPS)
    return xc * r, r


def _ln_bwd(xh, r, dxh):
    return r * (dxh - jnp.mean(dxh, axis=-1, keepdims=True) - xh * jnp.mean(dxh * xh, axis=-1, keepdims=True))


def _gelu(x):
    return 0.5 * x * (1.0 + lax.erf(x * (1.0 / math.sqrt(2.0))))


def _gelu_grad(x):
    cdf = 0.5 * (1.0 + lax.erf(x * (1.0 / math.sqrt(2.0))))
    pdf = jnp.exp(-0.5 * x * x) * (1.0 / math.sqrt(2.0 * math.pi))
    return cdf + x * pdf


def _dot(a, b):
    return jnp.dot(a, b, preferred_element_type=F32)


def _dot_nt(a, b):
    return lax.dot_general(a, b, (((1,), (1,)), ((), ())), preferred_element_type=F32)


def _dot_tn(a, b):
    return lax.dot_general(a, b, (((0,), (0,)), ((), ())), preferred_element_type=F32)


def _col_chunk(n):
    for c in (1408, 1024, 768, 512, 256, 128):
        if n % c == 0:
            return c
    raise ValueError(n)


class Rider(NamedTuple):
    name: str
    inputs: list
    out_shapes: list
    aliases: dict
    sems: tuple
    start: Callable
    wait: Callable


def _pallas(body, rider, *, name, steps, in_specs, out_specs, out_shape, scratch_shapes, args):
    if rider is None:
        res = pl.pallas_call(body, name=name, grid=(steps,), in_specs=in_specs, out_specs=out_specs, out_shape=out_shape,
                             scratch_shapes=scratch_shapes, compiler_params=_params(("arbitrary",)))(*args)
        return res, []
    n_in, n_out, n_scr = len(in_specs), len(out_specs), len(scratch_shapes)
    r_in, r_out = len(rider.inputs), len(rider.out_shapes)

    def wrapped(*refs):
        ins, rin = refs[:n_in], refs[n_in:n_in + r_in]
        at = n_in + r_in
        outs, rout = refs[at:at + n_out], refs[at + n_out:at + n_out + r_out]
        at += n_out + r_out
        scr, rsem = refs[at:at + n_scr], refs[at + n_scr:]

        @pl.when(pl.program_id(0) == 0)
        def _():
            rider.start(rin, rout, rsem)

        body(*ins, *outs, *scr)

        @pl.when(pl.program_id(0) == steps - 1)
        def _():
            rider.wait(rin, rout, rsem)

    res = pl.pallas_call(
        wrapped, name=name + "_" + rider.name, grid=(steps,), in_specs=list(in_specs) + [_ANY] * r_in,
        out_specs=list(out_specs) + [_ANY] * r_out, out_shape=list(out_shape) + list(rider.out_shapes),
        scratch_shapes=list(scratch_shapes) + [pltpu.SemaphoreType.DMA((n,)) for n in rider.sems],
        input_output_aliases={n_in + i: n_out + o for i, o in rider.aliases.items()},
        compiler_params=pltpu.CompilerParams(dimension_semantics=("arbitrary",), vmem_limit_bytes=VMEM_LIMIT,
                                             has_side_effects=True),
    )(*args, *rider.inputs)
    return res[:n_out], res[n_out:]


def run_rider(rider):
    def body(*refs):
        r_in, r_out = len(rider.inputs), len(rider.out_shapes)
        rin, rout, rsem = refs[:r_in], refs[r_in:r_in + r_out], refs[r_in + r_out:]
        rider.start(rin, rout, rsem)
        rider.wait(rin, rout, rsem)

    return pl.pallas_call(
        body, name=rider.name, in_specs=[_ANY] * len(rider.inputs), out_specs=[_ANY] * len(rider.out_shapes),
        out_shape=list(rider.out_shapes), scratch_shapes=[pltpu.SemaphoreType.DMA((n,)) for n in rider.sems],
        input_output_aliases=dict(rider.aliases), compiler_params=pltpu.CompilerParams(has_side_effects=True),
    )(*rider.inputs)


def _weight_spec(wg):
    return _const_spec(wg.shape)


def _join_col_blocks(w_ref, w_scr):
    c = w_ref.shape[2]
    for j in range(N_CHIPS):
        w_scr[:, c * j:c * (j + 1)] = w_ref[j]


def norm_matmul(x, g, wg, tm, rider=None):
    t, d = x.shape
    assert t % tm == 0, (t, tm)
    cw = wg.shape[2]
    n = N_CHIPS * cw
    aligned = cw % 128 == 0
    cn = cw if aligned else _col_chunk(n)

    def body(x_ref, g_ref, w_ref, o_ref, h_ref, *scr):
        if not aligned:
            @pl.when(pl.program_id(0) == 0)
            def _():
                _join_col_blocks(w_ref, scr[0])

        xv = x_ref[...]
        h = (xv * _rstd(xv) * g_ref[...]).astype(MM_DTYPE)
        h_ref[...] = h
        for j, c0 in enumerate(range(0, n, cn)):
            wv = w_ref[j] if aligned else scr[0][:, c0:c0 + cn]
            o_ref[:, c0:c0 + cn] = _dot(h, wv)

    return _pallas(
        body, rider, name="norm_matmul", steps=t // tm,
        in_specs=[pl.BlockSpec((tm, d), lambda i: (i, 0)), _const_spec((1, d)), _weight_spec(wg)],
        out_specs=[pl.BlockSpec((tm, n), lambda i: (i, 0)), pl.BlockSpec((tm, d), lambda i: (i, 0))],
        out_shape=[jax.ShapeDtypeStruct((t, n), F32), jax.ShapeDtypeStruct((t, d), MM_DTYPE)],
        scratch_shapes=[] if aligned else [pltpu.VMEM((d, n), MM_DTYPE)],
        args=(x, g, wg))


def matmul_nt_norm_bwd(gy, wg, x, g, dres, tm, rider=None):
    t, n = gy.shape
    assert t % tm == 0, (t, tm)
    d, cw = wg.shape[1], wg.shape[2]
    aligned = cw % 128 == 0
    cn = cw if aligned else _col_chunk(n)
    steps = t // tm

    def body(gy_ref, w_ref, x_ref, g_ref, dres_ref, dx_ref, dg_ref, acc_ref, *scr):
        i = pl.program_id(0)

        @pl.when(i == 0)
        def _():
            acc_ref[...] = jnp.zeros_like(acc_ref)
            if not aligned:
                _join_col_blocks(w_ref, scr[0])

        dh = jnp.zeros((tm, d), F32)
        for j, c0 in enumerate(range(0, n, cn)):
            wv = w_ref[j] if aligned else scr[0][:, c0:c0 + cn]
            dh = dh + _dot_nt(gy_ref[:, c0:c0 + cn], wv)
        xv = x_ref[...]
        r = _rstd(xv)
        gv = g_ref[...]
        dx_ref[...] = dres_ref[...] + _rms_bwd(xv, r, gv, dh)
        acc_ref[...] += _rowsum8(dh * xv * r)

        @pl.when(i == steps - 1)
        def _():
            dg_ref[...] = jnp.sum(acc_ref[...], axis=0, keepdims=True)

    return _pallas(
        body, rider, name="matmul_nt_norm_bwd", steps=steps,
        in_specs=[pl.BlockSpec((tm, n), lambda i: (i, 0)), _weight_spec(wg), pl.BlockSpec((tm, d), lambda i: (i, 0)),
                  _const_spec((1, d)), pl.BlockSpec((tm, d), lambda i: (i, 0))],
        out_specs=[pl.BlockSpec((tm, d), lambda i: (i, 0)), pl.BlockSpec((1, d), lambda i: (0, 0))],
        out_shape=[jax.ShapeDtypeStruct((t, d), F32), jax.ShapeDtypeStruct((1, d), F32)],
        scratch_shapes=[pltpu.VMEM((8, d), F32)] + ([] if aligned else [pltpu.VMEM((d, n), MM_DTYPE)]),
        args=(gy, wg, x, g, dres))


def matmul_tn_cols(a, b, tk):
    t, r = a.shape
    c = b.shape[1] // N_CHIPS
    assert t % tk == 0 and r % 8 == 0 and c % 128 == 0, (a.shape, b.shape, tk)

    def body(a_ref, b_ref, o_ref):
        @pl.when(pl.program_id(1) == 0)
        def _():
            o_ref[...] = jnp.zeros_like(o_ref)

        o_ref[...] += _dot_tn(a_ref[...], b_ref[...])

    a_spec = pl.BlockSpec((tk, r), lambda j, k: (k, 0))
    b_spec = pl.BlockSpec((tk, c), lambda j, k: (k, j))
    return pl.pallas_call(
        body, name="matmul_tn_cols", grid=(N_CHIPS, t // tk), in_specs=[a_spec, b_spec],
        out_specs=pl.BlockSpec((None, r, c), lambda j, k: (j, 0, 0)),
        out_shape=jax.ShapeDtypeStruct((N_CHIPS, r, c), F32),
        compiler_params=_params(("arbitrary", "arbitrary")),
    )(a, b)


def matmul_tn_down(act, dd, tk):
    t, m = act.shape
    c = dd.shape[1]
    r = m // N_CHIPS
    assert t % tk == 0, (t, tk)
    steps = t // tk

    def body(a_ref, b_ref, o_ref, acc):
        k = pl.program_id(1)

        @pl.when(k == 0)
        def _():
            acc[...] = jnp.zeros_like(acc)

        acc[...] += _dot_tn(a_ref[...], b_ref[...])

        @pl.when(k == steps - 1)
        def _():
            o_ref[0] = acc[0:r, :]
            o_ref[1] = acc[r:2 * r, :]

    return pl.pallas_call(
        body, name="matmul_tn_down", grid=(2, steps),
        in_specs=[pl.BlockSpec((tk, 2 * r), lambda p, k: (k, p)), pl.BlockSpec((tk, c), lambda p, k: (k, 0))],
        out_specs=pl.BlockSpec((2, r, c), lambda p, k: (p, 0, 0)),
        out_shape=jax.ShapeDtypeStruct((N_CHIPS, r, c), F32),
        scratch_shapes=[pltpu.VMEM((2 * r, c), F32)],
        compiler_params=_params(("arbitrary", "arbitrary")),
    )(act, dd)


def matmul_tn_rows(a, b, tk):
    t, m = a.shape
    n = b.shape[1]
    r = m // N_CHIPS
    assert t % tk == 0 and r % 8 == 0, (a.shape, b.shape, tk)
    steps = t // tk

    def body(a_ref, b_ref, o_ref, acc):
        k = pl.program_id(0)

        @pl.when(k == 0)
        def _():
            acc[...] = jnp.zeros_like(acc)

        acc[...] += _dot_tn(a_ref[...], b_ref[...])

        @pl.when(k == steps - 1)
        def _():
            for j in range(N_CHIPS):
                o_ref[j] = acc[r * j:r * (j + 1), :]

    return pl.pallas_call(
        body, name="matmul_tn_rows", grid=(steps,),
        in_specs=[pl.BlockSpec((tk, m), lambda k: (k, 0)), pl.BlockSpec((tk, n), lambda k: (k, 0))],
        out_specs=pl.BlockSpec((N_CHIPS, r, n), lambda k: (0, 0, 0)),
        out_shape=jax.ShapeDtypeStruct((N_CHIPS, r, n), F32),
        scratch_shapes=[pltpu.VMEM((m, n), F32)],
        compiler_params=_params(("arbitrary",)),
    )(a, b)


def matmul_tn_in(h, dz, tk):
    t, d = h.shape
    n = dz.shape[1]
    c = n // N_CHIPS
    assert t % tk == 0, (t, tk)
    steps = t // tk

    def body(a_ref, b_ref, o_ref, acc):
        k = pl.program_id(0)

        @pl.when(k == 0)
        def _():
            acc[...] = jnp.zeros_like(acc)

        acc[...] += _dot_tn(a_ref[...], b_ref[...])

        @pl.when(k == steps - 1)
        def _():
            for j in range(N_CHIPS):
                o_ref[j] = acc[:, c * j:c * (j + 1)]

    return pl.pallas_call(
        body, name="matmul_tn_in", grid=(steps,),
        in_specs=[pl.BlockSpec((tk, d), lambda k: (k, 0)), pl.BlockSpec((tk, n), lambda k: (k, 0))],
        out_specs=pl.BlockSpec((N_CHIPS, d, c), lambda k: (0, 0, 0)),
        out_shape=jax.ShapeDtypeStruct((N_CHIPS, d, c), F32),
        scratch_shapes=[pltpu.VMEM((d, n), F32)],
        compiler_params=_params(("arbitrary",)),
    )(h, dz)


def to_tiles(a, tt):
    t = a.shape[0]
    return a.reshape((t // tt, 8, tt // 8) + a.shape[1:]).swapaxes(1, 2).reshape(a.shape)


def from_tiles(a, tt):
    t = a.shape[0]
    return a.reshape((t // tt, tt // 8, 8) + a.shape[1:]).swapaxes(1, 2).reshape(a.shape)


def _roll_sublanes(a, shift):
    n = a.shape[0] // 8
    return pltpu.roll(a.reshape(n, 8, a.shape[1]), shift, 1).reshape(a.shape)


def _halo_before(cur_last, prev_last):
    sub = lax.broadcasted_iota(jnp.int32, cur_last.shape, 0) % 8
    return jnp.where(sub == 0, _roll_sublanes(prev_last, 1), _roll_sublanes(cur_last, 1))


def _halo_after(cur_first, next_first):
    sub = lax.broadcasted_iota(jnp.int32, cur_first.shape, 0) % 8
    return jnp.where(sub == 7, _roll_sublanes(next_first, 7), _roll_sublanes(cur_first, 7))


def _conv_causal(ext, cur, prev_last, w, taps, tt, cols=None):
    hr = 8 * (taps - 1)
    cs = slice(None) if cols is None else cols
    ext[hr:hr + tt, cs] = cur
    ext[0:hr, cs] = _halo_before(cur[tt - hr:, :], prev_last)
    acc = w[0:1, :] * ext[0:tt, cs]
    for k in range(1, taps):
        acc = acc + w[k:k + 1, :] * ext[8 * k:8 * k + tt, cs]
    return acc


def _conv_anticausal(ext, cur, next_first, w, taps, tt, x=None, acc_w=None, cols=None):
    hr = 8 * (taps - 1)
    cs = slice(None) if cols is None else cols
    ext[0:tt, cs] = cur
    ext[tt:tt + hr, cs] = _halo_after(cur[0:hr, :], next_first)
    acc = None
    for k in range(taps):
        off = 8 * (taps - 1 - k)
        ld = ext[off:off + tt, cs]
        term = w[k:k + 1, :] * ld
        acc = term if acc is None else acc + term
        if x is not None:
            acc_w[k, :, cs] += _rowsum8(ld * x)
    return acc


def _dot_exact(a, b, dims):
    return lax.dot_general(a, b, (dims, ((), ())), precision=lax.Precision.HIGHEST, preferred_element_type=F32)


def _to_tile_order(perm, wt_ref, w_scr, transpose):
    pb = perm.astype(MM_DTYPE)
    for h in range(N_HEADS_B):
        half = (_dot_nt(pb, wt_ref[h]) if transpose else _dot(pb, wt_ref[h])).astype(MM_DTYPE)
        w_scr[h] = _dot_nt(half, pb).astype(MM_DTYPE)


def _project_rows(y, w_ref):
    r = w_ref.shape[1]
    acc = _dot(y[:, 0:r], w_ref[0])
    for j in range(1, N_CHIPS):
        acc = acc + _dot(y[:, r * j:r * (j + 1)], w_ref[j])
    return acc


def _head_select(parts):
    head = lax.broadcasted_iota(jnp.int32, parts[0].shape, 1) // HEAD
    acc = parts[0]
    for h in range(1, N_HEADS_B):
        acc = jnp.where(head == h, parts[h], acc)
    return acc


def _mixer_forward(z, prm, q, yc):
    _, lng, lnb, wm, bias_p, _, _, clg, clb = prm
    bg = z[:, 0:D_A]
    ya = bg * q
    o_b = 3 * D_A
    zu = z[:, o_b:o_b + D_B]
    zv = z[:, o_b + D_B:o_b + 2 * D_B]
    u = _gelu(zu)
    vh, rv = _ln_fwd(_gelu(zv))
    vnb = (vh * lng + lnb).astype(MM_DTYPE)
    s = _head_select([_dot(wm[h], vnb) for h in range(N_HEADS_B)]) + bias_p
    yb = u * s
    yh, rc = _ln_fwd(yc)
    l = yh * clg + clb
    sl = jax.nn.sigmoid(l)
    return dict(bg=bg, q=q, ya=ya, zu=zu, zv=zv, u=u, vh=vh, rv=rv, vnb=vnb, s=s, yb=yb, yh=yh, rc=rc, l=l, sl=sl,
                yo=l * sl)


def _conv_inputs(z):
    o_c = 3 * D_A + 2 * D_B
    a = z[:, o_c:o_c + D_C]
    sg = jax.nn.sigmoid(z[:, o_c + D_C:o_c + 2 * D_C])
    return z[:, D_A:2 * D_A] * z[:, 2 * D_A:3 * D_A], a * sg, a, sg


def _group_norm(f, gg):
    ya, yb, yo = f["ya"], f["yb"], f["yo"]
    ra, rb, ro = _rstd(ya), _rstd(yb), _rstd(yo)
    yn = jnp.concatenate([ya * ra * gg[:, 0:D_A], yb * rb * gg[:, D_A:D_A + D_B], yo * ro * gg[:, D_A + D_B:]], axis=1)
    return yn, (ra, rb, ro)


def _mixer_prm(refs, wp_scr, bias_scr):
    caw_ref, lng_ref, lnb_ref, _, _, ccw_ref, ccb_ref, clg_ref, clb_ref = refs
    wm = [wp_scr[h] for h in range(N_HEADS_B)]
    return (caw_ref[...], lng_ref[...], lnb_ref[...], wm, bias_scr[...], ccw_ref[...], ccb_ref[...], clg_ref[...],
            clb_ref[...])


def _mixer_param_specs(tt):
    return [_const_spec((8, D_A)), _const_spec((1, D_B)), _const_spec((1, D_B)), _const_spec((N_HEADS_B, tt, tt)),
            _const_spec((tt, D_B)), _const_spec((32, D_C)), _const_spec((1, D_C)), _const_spec((1, D_C)),
            _const_spec((1, D_C))]


HR_A = 8 * (K_A - 1)
HR_C = 8 * (K_C - 1)
HR_F = 8 * (K_F - 1)


def mixer_fwd(z, x, mp, perm, grp_g, wog, post_g, tt, rider=None):
    t = z.shape[0]
    assert t % tt == 0 and tt % CHUNK == 0 and tt >= HR_C, (t, tt)

    def body(z_ref, x_ref, *rest):
        prm_refs = rest[:9]
        (perm_ref, gg_ref, wo_ref, pg_ref, o_ref, x1_ref, cv_ref, pa_ext, yg_ext, pa_last, yg_last, wp_scr, bias_scr) = rest[9:]
        i = pl.program_id(0)

        @pl.when(i == 0)
        def _():
            pa_last[...] = jnp.zeros_like(pa_last)
            yg_last[...] = jnp.zeros_like(yg_last)
            _to_tile_order(perm_ref[...], prm_refs[3], wp_scr, False)
            bias_scr[...] = _dot_exact(perm_ref[...], prm_refs[4][...], ((1,), (0,)))

        zv = z_ref[...]
        prm = _mixer_prm(prm_refs, wp_scr, bias_scr)
        pa, yg, _, _ = _conv_inputs(zv)
        q = _conv_causal(pa_ext, pa, pa_last[...], prm[0], K_A, tt)
        yc = _conv_causal(yg_ext, yg, yg_last[...], prm[5], K_C, tt) + prm[6]
        pa_last[...] = pa[tt - HR_A:, :]
        yg_last[...] = yg[tt - HR_C:, :]
        cv_ref[:, 0:D_A] = q
        cv_ref[:, D_A:] = yc
        f = _mixer_forward(zv, prm, q, yc)
        yn, _ = _group_norm(f, gg_ref[...])
        o = _project_rows(yn.astype(MM_DTYPE), wo_ref)
        o_ref[...] = o
        x1_ref[...] = x_ref[...] + o * _rstd(o) * pg_ref[...]

    row = lambda c: pl.BlockSpec((tt, c), lambda i: (i, 0))
    return _pallas(
        body, rider, name="mixer_fwd", steps=t // tt,
        in_specs=[row(D_IN), row(D_MODEL)] + _mixer_param_specs(tt)
        + [_const_spec((tt, tt)), _const_spec((1, D_MODEL)), _weight_spec(wog), _const_spec((1, D_MODEL))],
        out_specs=[row(D_MODEL), row(D_MODEL), row(D_A + D_C)],
        out_shape=[jax.ShapeDtypeStruct((t, D_MODEL), F32), jax.ShapeDtypeStruct((t, D_MODEL), F32),
                   jax.ShapeDtypeStruct((t, D_A + D_C), F32)],
        scratch_shapes=[pltpu.VMEM((HR_A + tt, D_A), F32), pltpu.VMEM((HR_C + tt, D_C), F32),
                        pltpu.VMEM((HR_A, D_A), F32), pltpu.VMEM((HR_C, D_C), F32),
                        pltpu.VMEM((N_HEADS_B, tt, tt), MM_DTYPE), pltpu.VMEM((tt, D_B), F32)],
        args=(z, x, *mp, perm, grp_g, wog, post_g))


def mixer_bwd(dx1, o, z, cv, mp, perm, grp_g, wog, post_g, tt, rider=None):
    t = z.shape[0]
    assert t % tt == 0 and tt % CHUNK == 0 and tt >= HR_C, (t, tt)
    steps = t // tt

    def body(dx1_ref, o_ref, z_ref, cv_ref, *rest):
        prm_refs = rest[:9]
        (perm_ref, gg_ref, wo_ref, pg_ref,
         dz_ref, do_ref, yn_ref, dpg_ref, dgg_ref, dcaw_ref, dlng_ref, dlnb_ref, dwm_ref, dbias_ref, dccw_ref, dccb_ref,
         dclg_ref, dclb_ref,
         dq_ext, dyc_ext, dq_first, dyc_first, a_pg, a_gg, a_caw, a_lng, a_lnb, a_ccw, a_ccb, a_clg, a_clb,
         wp_scr, wpt_scr, bias_scr, a_wm, a_bias) = rest[9:]
        i = pl.program_id(0)
        small = (a_pg, a_gg, a_caw, a_lng, a_lnb, a_ccw, a_ccb, a_clg, a_clb)

        @pl.when(i == 0)
        def _():
            for ref in small + (a_wm, a_bias, dq_first, dyc_first):
                ref[...] = jnp.zeros_like(ref)
            _to_tile_order(perm_ref[...], prm_refs[3], wp_scr, False)
            _to_tile_order(perm_ref[...], prm_refs[3], wpt_scr, True)
            bias_scr[...] = _dot_exact(perm_ref[...], prm_refs[4][...], ((1,), (0,)))

        prm = _mixer_prm(prm_refs, wp_scr, bias_scr)
        caw, lng, lnb, wm, bias_p, ccw, ccb, clg, clb = prm

        zv = z_ref[...]
        pa, yg, a, sg = _conv_inputs(zv)
        f = _mixer_forward(zv, prm, cv_ref[:, 0:D_A], cv_ref[:, D_A:])
        gg = gg_ref[...]
        yn, (ra, rb, ro) = _group_norm(f, gg)
        yn_ref[...] = yn.astype(MM_DTYPE)

        ov = o_ref[...]
        dx1v = dx1_ref[...]
        r_o = _rstd(ov)
        pg = pg_ref[...]
        a_pg[...] += _rowsum8(dx1v * ov * r_o)
        do = _rms_bwd(ov, r_o, pg, dx1v).astype(MM_DTYPE)
        do_ref[...] = do
        dyn = jnp.concatenate([_dot_nt(do, wo_ref[j]) for j in range(N_CHIPS)], axis=1)

        dyn_a, dyn_b, dyn_c = dyn[:, 0:D_A], dyn[:, D_A:D_A + D_B], dyn[:, D_A + D_B:]
        ga, gb, gc = gg[:, 0:D_A], gg[:, D_A:D_A + D_B], gg[:, D_A + D_B:]
        a_gg[...] += _rowsum8(jnp.concatenate([dyn_a * f["ya"] * ra, dyn_b * f["yb"] * rb, dyn_c * f["yo"] * ro], axis=1))
        dya = _rms_bwd(f["ya"], ra, ga, dyn_a)
        dyb = _rms_bwd(f["yb"], rb, gb, dyn_b)
        dyo = _rms_bwd(f["yo"], ro, gc, dyn_c)

        dbg = dya * f["q"]
        dq = dya * f["bg"]
        dp = _conv_anticausal(dq_ext, dq, dq_first[...], caw, K_A, tt, x=pa, acc_w=a_caw)
        dq_first[...] = dq[0:HR_A, :]
        dcg = dp * zv[:, 2 * D_A:3 * D_A]
        dxa = dp * zv[:, D_A:2 * D_A]

        du = dyb * f["s"]
        ds = dyb * f["u"]
        dsb = ds.astype(MM_DTYPE)
        head = lax.broadcasted_iota(jnp.int32, (tt, D_B), 1) // HEAD
        a_bias[...] += ds
        parts = []
        for h in range(N_HEADS_B):
            a_wm[h] += _dot_nt(jnp.where(head == h, dsb, jnp.zeros_like(dsb)), f["vnb"])
            parts.append(_dot(wpt_scr[h], dsb))
        dvn = _head_select(parts)
        a_lng[...] += _rowsum8(dvn * f["vh"])
        a_lnb[...] += _rowsum8(dvn)
        dv = _ln_bwd(f["vh"], f["rv"], dvn * lng)
        dzu = du * _gelu_grad(f["zu"])
        dzv = dv * _gelu_grad(f["zv"])

        l, sl = f["l"], f["sl"]
        dl = dyo * (sl * (1.0 + l * (1.0 - sl)))
        a_clg[...] += _rowsum8(dl * f["yh"])
        a_clb[...] += _rowsum8(dl)
        dyc = _ln_bwd(f["yh"], f["rc"], dl * clg)
        a_ccb[...] += _rowsum8(dyc)
        dy = _conv_anticausal(dyc_ext, dyc, dyc_first[...], ccw, K_C, tt, x=yg, acc_w=a_ccw)
        dyc_first[...] = dyc[0:HR_C, :]
        da = dy * sg
        dg = dy * a * sg * (1.0 - sg)

        dz_ref[...] = jnp.concatenate([dbg, dcg, dxa, dzu, dzv, da, dg], axis=1).astype(MM_DTYPE)

        @pl.when(i == steps - 1)
        def _():
            red = lambda ref: jnp.sum(ref[...], axis=0, keepdims=True)
            dpg_ref[...] = red(a_pg)
            dgg_ref[...] = red(a_gg)
            dlng_ref[...] = red(a_lng)
            dlnb_ref[...] = red(a_lnb)
            dccb_ref[...] = red(a_ccb)
            dclg_ref[...] = red(a_clg)
            dclb_ref[...] = red(a_clb)
            dcaw_ref[...] = jnp.sum(a_caw[...], axis=1)
            dccw_ref[...] = jnp.sum(a_ccw[...], axis=1)
            pm = perm_ref[...]
            tril = lax.broadcasted_iota(jnp.int32, (CHUNK, CHUNK), 0) >= lax.broadcasted_iota(jnp.int32, (CHUNK, CHUNK), 1)
            for h in range(N_HEADS_B):
                dwt = _dot_exact(pm, _dot_exact(a_wm[h], pm, ((1,), (0,))), ((0,), (0,)))
                dw = dwt[0:CHUNK, 0:CHUNK]
                for c in range(1, tt // CHUNK):
                    dw = dw + dwt[c * CHUNK:(c + 1) * CHUNK, c * CHUNK:(c + 1) * CHUNK]
                dwm_ref[h] = jnp.where(tril, dw, 0.0)
            dbt = _dot_exact(pm, a_bias[...], ((0,), (0,)))
            db = dbt[0:CHUNK, :]
            for c in range(1, tt // CHUNK):
                db = db + dbt[c * CHUNK:(c + 1) * CHUNK, :]
            dbias_ref[...] = db

    rev = lambda c: pl.BlockSpec((tt, c), lambda i: (steps - 1 - i, 0))
    full = lambda shape: pl.BlockSpec(shape, lambda i: (0,) * len(shape))
    sds = jax.ShapeDtypeStruct
    return _pallas(
        body, rider, name="mixer_bwd", steps=steps,
        in_specs=[rev(D_MODEL), rev(D_MODEL), rev(D_IN), rev(D_A + D_C)] + _mixer_param_specs(tt)
        + [_const_spec((tt, tt)), _const_spec((1, D_MODEL)), _weight_spec(wog), _const_spec((1, D_MODEL))],
        out_specs=[rev(D_IN), rev(D_MODEL), rev(D_MODEL), full((1, D_MODEL)), full((1, D_MODEL)), full((8, D_A)),
                   full((1, D_B)), full((1, D_B)), full((N_HEADS_B, CHUNK, CHUNK)), full((CHUNK, D_B)), full((32, D_C)),
                   full((1, D_C)), full((1, D_C)), full((1, D_C))],
        out_shape=[sds((t, D_IN), MM_DTYPE), sds((t, D_MODEL), MM_DTYPE), sds((t, D_MODEL), MM_DTYPE),
                   sds((1, D_MODEL), F32), sds((1, D_MODEL), F32), sds((8, D_A), F32), sds((1, D_B), F32), sds((1, D_B), F32),
                   sds((N_HEADS_B, CHUNK, CHUNK), F32), sds((CHUNK, D_B), F32), sds((32, D_C), F32), sds((1, D_C), F32),
                   sds((1, D_C), F32), sds((1, D_C), F32)],
        scratch_shapes=[pltpu.VMEM((tt + HR_A, D_A), F32), pltpu.VMEM((tt + HR_C, D_C), F32),
                        pltpu.VMEM((HR_A, D_A), F32), pltpu.VMEM((HR_C, D_C), F32),
                        pltpu.VMEM((8, D_MODEL), F32), pltpu.VMEM((8, D_MODEL), F32), pltpu.VMEM((8, 8, D_A), F32),
                        pltpu.VMEM((8, D_B), F32), pltpu.VMEM((8, D_B), F32), pltpu.VMEM((32, 8, D_C), F32),
                        pltpu.VMEM((8, D_C), F32), pltpu.VMEM((8, D_C), F32), pltpu.VMEM((8, D_C), F32),
                        pltpu.VMEM((N_HEADS_B, tt, tt), MM_DTYPE), pltpu.VMEM((N_HEADS_B, tt, tt), MM_DTYPE),
                        pltpu.VMEM((tt, D_B), F32), pltpu.VMEM((N_HEADS_B, tt, tt), F32), pltpu.VMEM((tt, D_B), F32)],
        args=(dx1, o, z, cv, *mp, perm, grp_g, wog, post_g))


def _fetch_row_blocks(wg_ref, w_scr, sems):
    r = wg_ref.shape[1]
    copies = [pltpu.make_async_copy(wg_ref.at[j], w_scr.at[pl.ds(r * j, r), :], sems.at[j]) for j in range(N_CHIPS)]
    for cp in copies:
        cp.start()
    for cp in copies:
        cp.wait()


def _ffn_conv(ext, cw, c0, cn, tt):
    acc = cw[0:1, c0:c0 + cn] * ext[0:tt, c0:c0 + cn]
    for k in range(1, K_F):
        acc = acc + cw[k:k + 1, c0:c0 + cn] * ext[8 * k:8 * k + tt, c0:c0 + cn]
    return acc


def ffn_fwd(up0, x1, cw, wdg, post_g, tt, rider=None):
    t = up0.shape[0]
    assert t % tt == 0, (t, tt)
    cn = _col_chunk(D_FF)

    def body(up0_ref, x1_ref, cw_ref, wdg_ref, pg_ref, d_ref, x2_ref, ext, last, wd_ref, sems):
        i = pl.program_id(0)

        @pl.when(i == 0)
        def _():
            _fetch_row_blocks(wdg_ref, wd_ref, sems)
            last[...] = jnp.zeros_like(last)

        ext[HR_F:HR_F + tt, :] = up0_ref[...]
        ext[0:HR_F, :] = _halo_before(up0_ref[tt - HR_F:, :], last[...])
        last[...] = up0_ref[tt - HR_F:, :]
        cwv = cw_ref[...]
        d = jnp.zeros((tt, D_MODEL), F32)
        for c0 in range(0, D_FF, cn):
            gate = _ffn_conv(ext, cwv, c0, cn, tt)
            val = _ffn_conv(ext, cwv, D_FF + c0, cn, tt)
            act = (gate * jax.nn.sigmoid(gate) * val).astype(MM_DTYPE)
            d = d + _dot(act, wd_ref[c0:c0 + cn, :])
        d_ref[...] = d
        x2_ref[...] = x1_ref[...] + d * _rstd(d) * pg_ref[...]

    row = lambda c: pl.BlockSpec((tt, c), lambda i: (i, 0))
    return _pallas(
        body, rider, name="ffn_fwd", steps=t // tt,
        in_specs=[row(2 * D_FF), row(D_MODEL), _const_spec((8, 2 * D_FF)), _ANY, _const_spec((1, D_MODEL))],
        out_specs=[row(D_MODEL), row(D_MODEL)],
        out_shape=[jax.ShapeDtypeStruct((t, D_MODEL), F32), jax.ShapeDtypeStruct((t, D_MODEL), F32)],
        scratch_shapes=[pltpu.VMEM((HR_F + tt, 2 * D_FF), F32), pltpu.VMEM((HR_F, 2 * D_FF), F32),
                        pltpu.VMEM((D_FF, D_MODEL), MM_DTYPE), pltpu.SemaphoreType.DMA((N_CHIPS,))],
        args=(up0, x1, cw, wdg, post_g))


def ffn_bwd(dx2, d, up0, cw, wdg, post_g, tt, rider=None):
    t = up0.shape[0]
    assert t % tt == 0, (t, tt)
    steps = t // tt
    hb = tt // HR_F
    cn = _col_chunk(D_FF)

    def body(dx2_ref, d_ref, up0_ref, uh_ref, cw_ref, wdg_ref, pg_ref,
             dd_ref, act_ref, dup0_ref, dpg_ref, dcw_ref, ext, dup_ext, first, a_pg, a_cw, wd_ref, sems):
        i = pl.program_id(0)
        tile = steps - 1 - i

        @pl.when(i == 0)
        def _():
            _fetch_row_blocks(wdg_ref, wd_ref, sems)
            a_pg[...] = jnp.zeros_like(a_pg)
            a_cw[...] = jnp.zeros_like(a_cw)
            first[...] = jnp.zeros_like(first)

        ext[HR_F:HR_F + tt, :] = up0_ref[...]
        ext[0:HR_F, :] = _halo_before(up0_ref[tt - HR_F:, :], jnp.where(tile > 0, uh_ref[...], 0.0))
        cwv = cw_ref[...]
        dv = d_ref[...]
        dx2v = dx2_ref[...]
        r = _rstd(dv)
        a_pg[...] += _rowsum8(dx2v * dv * r)
        dd = _rms_bwd(dv, r, pg_ref[...], dx2v).astype(MM_DTYPE)
        dd_ref[...] = dd
        for c0 in range(0, D_FF, cn):
            gate = _ffn_conv(ext, cwv, c0, cn, tt)
            val = _ffn_conv(ext, cwv, D_FF + c0, cn, tt)
            sg = jax.nn.sigmoid(gate)
            sl = gate * sg
            act_ref[:, c0:c0 + cn] = (sl * val).astype(MM_DTYPE)
            da = _dot_nt(dd, wd_ref[c0:c0 + cn, :])
            dup_ext[0:tt, c0:c0 + cn] = da * val * (sg * (1.0 + gate * (1.0 - sg)))
            dup_ext[0:tt, D_FF + c0:D_FF + c0 + cn] = da * sl
        dup_ext[tt:tt + HR_F, :] = _halo_after(dup_ext[0:HR_F, :], first[...])
        first[...] = dup_ext[0:HR_F, :]
        for c0 in range(0, 2 * D_FF, cn):
            x = up0_ref[:, c0:c0 + cn]
            acc = None
            for k in range(K_F):
                off = 8 * (K_F - 1 - k)
                ld = dup_ext[off:off + tt, c0:c0 + cn]
                term = cwv[k:k + 1, c0:c0 + cn] * ld
                acc = term if acc is None else acc + term
                a_cw[k, :, c0:c0 + cn] += _rowsum8(ld * x)
            dup0_ref[:, c0:c0 + cn] = acc.astype(MM_DTYPE)

        @pl.when(i == steps - 1)
        def _():
            dpg_ref[...] = jnp.sum(a_pg[...], axis=0, keepdims=True)
            dcw_ref[...] = jnp.sum(a_cw[...], axis=1)

    rev = lambda c: pl.BlockSpec((tt, c), lambda i: (steps - 1 - i, 0))
    halo = pl.BlockSpec((HR_F, 2 * D_FF), lambda i: (jnp.maximum((steps - 1 - i) * hb - 1, 0), 0))
    full = lambda shape: pl.BlockSpec(shape, lambda i: (0,) * len(shape))
    sds = jax.ShapeDtypeStruct
    return _pallas(
        body, rider, name="ffn_bwd", steps=steps,
        in_specs=[rev(D_MODEL), rev(D_MODEL), rev(2 * D_FF), halo, _const_spec((8, 2 * D_FF)), _ANY,
                  _const_spec((1, D_MODEL))],
        out_specs=[rev(D_MODEL), rev(D_FF), rev(2 * D_FF), full((1, D_MODEL)), full((8, 2 * D_FF))],
        out_shape=[sds((t, D_MODEL), MM_DTYPE), sds((t, D_FF), MM_DTYPE), sds((t, 2 * D_FF), MM_DTYPE),
                   sds((1, D_MODEL), F32), sds((8, 2 * D_FF), F32)],
        scratch_shapes=[pltpu.VMEM((HR_F + tt, 2 * D_FF), F32), pltpu.VMEM((tt + HR_F, 2 * D_FF), F32),
                        pltpu.VMEM((HR_F, 2 * D_FF), F32), pltpu.VMEM((8, D_MODEL), F32),
                        pltpu.VMEM((8, 8, 2 * D_FF), F32), pltpu.VMEM((D_FF, D_MODEL), MM_DTYPE),
                        pltpu.SemaphoreType.DMA((N_CHIPS,))],
        args=(dx2, d, up0, up0, cw, wdg, post_g))


def loss_head(y, target, tm):
    t, d = y.shape
    assert t % tm == 0, (t, tm)
    steps = t // tm

    def body(y_ref, t_ref, dy_ref, loss_ref, acc):
        i = pl.program_id(0)

        @pl.when(i == 0)
        def _():
            acc[...] = jnp.zeros_like(acc)

        diff = y_ref[...] - t_ref[...]
        dy_ref[...] = diff * (1.0 / d)
        acc[...] += _rowsum8(diff * diff)

        @pl.when(i == steps - 1)
        def _():
            loss_ref[...] = (0.5 / d) * jnp.sum(jnp.sum(acc[...], axis=0, keepdims=True), axis=1, keepdims=True)

    row = pl.BlockSpec((tm, d), lambda i: (i, 0))
    return pl.pallas_call(
        body, name="loss_head", grid=(steps,), in_specs=[row, row],
        out_specs=[row, pl.BlockSpec((1, 1), lambda i: (0, 0))],
        out_shape=[jax.ShapeDtypeStruct((t, d), F32), jax.ShapeDtypeStruct((1, 1), F32)],
        scratch_shapes=[pltpu.VMEM((8, d), F32)],
        compiler_params=_params(("arbitrary",)),
    )(y, target)


def adamw(w, g, m, v):
    shape = w.shape
    cols = shape[-1]
    rows = w.size // cols
    tr = next((r for r in (512, 256, 128) if rows % r == 0 and rows > r), rows)
    c1 = 1.0 - ADAM_B1 ** ADAM_STEP
    c2 = 1.0 - ADAM_B2 ** ADAM_STEP

    def body(w_ref, g_ref, m_ref, v_ref, d_ref, nm_ref, nv_ref):
        gv = g_ref[...]
        nm = ADAM_B1 * m_ref[...] + (1.0 - ADAM_B1) * gv
        nv = ADAM_B2 * v_ref[...] + (1.0 - ADAM_B2) * (gv * gv)
        nm_ref[...] = nm
        nv_ref[...] = nv
        d_ref[...] = -ADAM_LR * ((nm / c1) / (jnp.sqrt(nv / c2) + ADAM_EPS) + ADAM_WD * w_ref[...])

    spec = pl.BlockSpec((tr, cols), lambda i: (i, 0))
    out = jax.ShapeDtypeStruct((rows, cols), F32)
    res = pl.pallas_call(
        body, name="adamw", grid=(rows // tr,), in_specs=[spec] * 4, out_specs=[spec] * 3, out_shape=[out] * 3,
        compiler_params=_params(("arbitrary",)),
    )(*[a.reshape(rows, cols) for a in (w, g, m, v)])
    return tuple(r.reshape(shape) for r in res)


def _place():
    return lax.axis_index("x"), lax.axis_index("y"), lax.axis_index("c")


def _other_chips(x, y):
    return [(1 - x, y, 2 * (1 - x) + y), (x, 1 - y, 2 * x + 1 - y), (1 - x, 1 - y, 2 * (1 - x) + 1 - y)]


def _sem_specs(*counts):
    return [pltpu.SemaphoreType.DMA((n,)) for n in counts]


def cast_shard(w, layer, chip):
    _, r, c = w.shape

    def body(chip_ref, w_ref, o_ref):
        del chip_ref
        o_ref[...] = w_ref[...].astype(MM_DTYPE)

    grid_spec = pltpu.PrefetchScalarGridSpec(
        num_scalar_prefetch=1, grid=(1,), in_specs=[pl.BlockSpec((None, r, c), lambda i, chip_ref: (layer, 0, 0))],
        out_specs=pl.BlockSpec((None, r, c), lambda i, chip_ref: (chip_ref[0], 0, 0)))
    return pl.pallas_call(
        body, name="cast_shard", grid_spec=grid_spec, out_shape=jax.ShapeDtypeStruct((N_CHIPS, r, c), MM_DTYPE),
        compiler_params=_params(("arbitrary",)),
    )(jnp.reshape(chip, (1,)).astype(jnp.int32), w)


def _row_half(buf, chip, mine, c):
    rh = buf.shape[1] // 2
    return buf.at[chip, pl.ds(pl.multiple_of((c if mine else 1 - c) * rh, 16), rh), :]


def spread_rider(bufs):
    n = len(bufs)

    def start(rin, rout, sems):
        x, y, c = _place()
        me = 2 * x + y
        for k, (px, py, _) in enumerate(_other_chips(x, y)):
            for i, buf in enumerate(rout):
                part = _row_half(buf, me, True, c)
                pltpu.make_async_remote_copy(
                    src_ref=part, dst_ref=part, send_sem=sems[0].at[n * k + i], recv_sem=sems[1].at[n * k + i],
                    device_id=(px, py, c), device_id_type=MESH_ID).start()

    def wait(rin, rout, sems):
        x, y, c = _place()
        for k, (_, _, pj) in enumerate(_other_chips(x, y)):
            for i, buf in enumerate(rout):
                part = _row_half(buf, pj, True, c)
                pltpu.make_async_remote_copy(
                    src_ref=part, dst_ref=part, send_sem=sems[0].at[n * k + i], recv_sem=sems[1].at[n * k + i],
                    device_id=(x, y, c), device_id_type=MESH_ID).wait()

    shapes = [jax.ShapeDtypeStruct(b.shape, b.dtype) for b in bufs]
    return Rider("spread", list(bufs), shapes, {i: i for i in range(n)}, (3 * n, 3 * n), start, wait)


def pass_rider(bufs):
    n = len(bufs)

    def start(rin, rout, sems):
        x, y, c = _place()
        for k, (_, _, pj) in enumerate(_other_chips(x, y)):
            for i, buf in enumerate(rout):
                part = _row_half(buf, pj, True, c)
                pltpu.make_async_remote_copy(
                    src_ref=part, dst_ref=part, send_sem=sems[0].at[n * k + i], recv_sem=sems[1].at[n * k + i],
                    device_id=(x, y, 1 - c), device_id_type=MESH_ID).start()

    def wait(rin, rout, sems):
        x, y, c = _place()
        for k, (_, _, pj) in enumerate(_other_chips(x, y)):
            for i, buf in enumerate(rout):
                part = _row_half(buf, pj, False, c)
                pltpu.make_async_remote_copy(
                    src_ref=part, dst_ref=part, send_sem=sems[0].at[n * k + i], recv_sem=sems[1].at[n * k + i],
                    device_id=(x, y, 1 - c), device_id_type=MESH_ID).wait()

    shapes = [jax.ShapeDtypeStruct(b.shape, b.dtype) for b in bufs]
    return Rider("pass", list(bufs), shapes, {i: i for i in range(n)}, (3 * n, 3 * n), start, wait)


def both_riders(a, b):
    na, oa, sa = len(a.inputs), len(a.out_shapes), len(a.sems)

    def start(rin, rout, sems):
        a.start(rin[:na], rout[:oa], sems[:sa])
        b.start(rin[na:], rout[oa:], sems[sa:])

    def wait(rin, rout, sems):
        a.wait(rin[:na], rout[:oa], sems[:sa])
        b.wait(rin[na:], rout[oa:], sems[sa:])

    aliases = dict(a.aliases)
    aliases.update({na + i: oa + o for i, o in b.aliases.items()})
    return Rider(a.name + "_" + b.name, a.inputs + b.inputs, a.out_shapes + b.out_shapes, aliases, a.sems + b.sems,
                 start, wait)


def gather_small(small):
    def body(small_ref, out_ref, send, recv, local):
        x, y, c = _place()
        me = 2 * x + y
        chips = _other_chips(x, y)
        own = pltpu.make_async_copy(small_ref, out_ref.at[me], local.at[0])
        own.start()
        sends = [pltpu.make_async_remote_copy(src_ref=small_ref, dst_ref=out_ref.at[me], send_sem=send.at[k],
                                              recv_sem=recv.at[k], device_id=(px, py, c), device_id_type=MESH_ID)
                 for k, (px, py, _) in enumerate(chips)]
        for cp in sends:
            cp.start()
        for k, (_, _, pj) in enumerate(chips):
            pltpu.make_async_remote_copy(src_ref=small_ref, dst_ref=out_ref.at[pj], send_sem=send.at[k], recv_sem=recv.at[k],
                                         device_id=(x, y, c), device_id_type=MESH_ID).wait_recv()
        for cp in sends:
            cp.wait_send()
        own.wait()

    return pl.pallas_call(
        body, name="gather_small", in_specs=[_ANY], out_specs=_ANY,
        out_shape=jax.ShapeDtypeStruct((N_CHIPS,) + small.shape, small.dtype), scratch_shapes=_sem_specs(3, 3, 1),
        compiler_params=pltpu.CompilerParams(has_side_effects=True),
    )(small)


def swap_rider(gs):
    n = len(gs)

    def copies(rin, rout, sems):
        x, y, c = _place()
        out = []
        for i, (g, got) in enumerate(zip(rin, rout)):
            rh = g.shape[1] // 2
            theirs = pl.ds(pl.multiple_of((1 - c) * rh, 8), rh)
            out.append(pltpu.make_async_remote_copy(
                src_ref=g.at[:, theirs, :], dst_ref=got, send_sem=sems[0].at[i], recv_sem=sems[1].at[i],
                device_id=(x, y, 1 - c), device_id_type=MESH_ID))
        return out

    def start(rin, rout, sems):
        for cp in copies(rin, rout, sems):
            cp.start()

    def wait(rin, rout, sems):
        for cp in copies(rin, rout, sems):
            cp.wait()

    shapes = [jax.ShapeDtypeStruct((g.shape[0], g.shape[1] // 2, g.shape[2]), g.dtype) for g in gs]
    return Rider("swap", list(gs), shapes, {}, (n, n), start, wait)


def scatter_rider(sbs):
    n = len(sbs)

    def start(rin, rout, sems):
        x, y, c = _place()
        me = 2 * x + y
        for k, (px, py, pj) in enumerate(_other_chips(x, y)):
            for i, (sb, got) in enumerate(zip(rin, rout)):
                pltpu.make_async_remote_copy(
                    src_ref=sb.at[pj], dst_ref=got.at[me], send_sem=sems[0].at[n * k + i], recv_sem=sems[1].at[n * k + i],
                    device_id=(px, py, c), device_id_type=MESH_ID).start()

    def wait(rin, rout, sems):
        x, y, c = _place()
        for k, (_, _, pj) in enumerate(_other_chips(x, y)):
            for i, (sb, got) in enumerate(zip(rin, rout)):
                cp = pltpu.make_async_remote_copy(
                    src_ref=sb.at[pj], dst_ref=got.at[pj], send_sem=sems[0].at[n * k + i], recv_sem=sems[1].at[n * k + i],
                    device_id=(x, y, c), device_id_type=MESH_ID)
                cp.wait_recv()
                cp.wait_send()

    shapes = [jax.ShapeDtypeStruct(sb.shape, sb.dtype) for sb in sbs]
    return Rider("scatter", list(sbs), shapes, {}, (3 * n, 3 * n), start, wait)


def join_rider(fs, layers):
    n = len(fs)

    def half(i, f, mine, place):
        x, y, c = place
        rh = f.shape[1] // 2
        block = 2 * x + y if layers[i] is None else layers[i]
        return f.at[block, pl.ds(pl.multiple_of((c if mine else 1 - c) * rh, 8), rh), :]

    def start(rin, rout, sems):
        x, y, c = _place()
        for i, f in enumerate(rout):
            part = half(i, f, True, (x, y, c))
            pltpu.make_async_remote_copy(
                src_ref=part, dst_ref=part, send_sem=sems[0].at[i], recv_sem=sems[1].at[i],
                device_id=(x, y, 1 - c), device_id_type=MESH_ID).start()

    def wait(rin, rout, sems):
        x, y, c = _place()
        for i, f in enumerate(rout):
            part = half(i, f, False, (x, y, c))
            pltpu.make_async_remote_copy(
                src_ref=part, dst_ref=part, send_sem=sems[0].at[i], recv_sem=sems[1].at[i],
                device_id=(x, y, 1 - c), device_id_type=MESH_ID).wait()

    shapes = [jax.ShapeDtypeStruct(f.shape, f.dtype) for f in fs]
    return Rider("join", list(fs), shapes, {i: i for i in range(n)}, (n, n), start, wait)


def add_halves(gs, gots, wire=BF16):
    m = len(gs)
    x, y, c = _place()

    def body(p_ref, *refs):
        ins, outs = refs[:2 * m], refs[2 * m:]
        for i in range(m):
            s = ins[2 * i][...] + ins[2 * i + 1][...]
            outs[2 * i][...] = s.astype(wire)

            @pl.when(pl.program_id(0) == p_ref[0])
            def _(s=s, own_ref=outs[2 * i + 1]):
                own_ref[...] = s

    in_specs, out_specs, out_shape = [], [], []
    for got in gots:
        n, rh, cols = got.shape
        blk = (None, rh, cols)
        in_specs += [pl.BlockSpec(blk, lambda j, p_ref: (j, p_ref[1], 0)), pl.BlockSpec(blk, lambda j, p_ref: (j, 0, 0))]
        out_specs += [pl.BlockSpec(blk, lambda j, p_ref: (j, 0, 0)), pl.BlockSpec((rh, cols), lambda j, p_ref: (0, 0))]
        out_shape += [jax.ShapeDtypeStruct(got.shape, wire), jax.ShapeDtypeStruct((rh, cols), F32)]
    grid_spec = pltpu.PrefetchScalarGridSpec(num_scalar_prefetch=1, grid=(N_CHIPS,), in_specs=in_specs, out_specs=out_specs)
    res = pl.pallas_call(
        body, name="add_halves", grid_spec=grid_spec, out_shape=out_shape, compiler_params=_params(("arbitrary",)),
    )(jnp.stack([2 * x + y, c]).astype(jnp.int32), *[a for pair in zip(gs, gots) for a in pair])
    return [(res[2 * i + 1], res[2 * i]) for i in range(m)]


def add_chips(owns, gots, fbufs, block=None):
    m = len(owns)
    x, y, c = _place()
    me = 2 * x + y

    def body(p_ref, *refs):
        ins, outs = refs[:5 * m], refs[5 * m:]
        for i in range(m):
            s_ref, g1_ref, g2_ref, g3_ref, _ = ins[5 * i:5 * i + 5]
            outs[i][...] = s_ref[...] + g1_ref[...].astype(F32) + g2_ref[...].astype(F32) + g3_ref[...].astype(F32)

    def other(blk, n, k):
        return pl.BlockSpec(blk, lambda i, p_ref: ((p_ref[0] + k) % n, 0, 0))

    in_specs, out_specs, args = [], [], []
    for own, got, fbuf in zip(owns, gots, fbufs):
        n, rh, cols = got.shape
        blk = (None, rh, cols)
        in_specs += [pl.BlockSpec((rh, cols), lambda i, p_ref: (0, 0)), other(blk, n, 1), other(blk, n, 2), other(blk, n, 3),
                     _ANY]
        out_specs.append(pl.BlockSpec(blk, lambda i, p_ref: (p_ref[2], p_ref[1], 0)))
        args += [own, got, got, got, fbuf]
    grid_spec = pltpu.PrefetchScalarGridSpec(num_scalar_prefetch=1, grid=(1,), in_specs=in_specs, out_specs=out_specs)
    return pl.pallas_call(
        body, name="add_chips", grid_spec=grid_spec, out_shape=[jax.ShapeDtypeStruct(f.shape, F32) for f in fbufs],
        input_output_aliases={5 * i + 5: i for i in range(m)}, compiler_params=_params(("arbitrary",)),
    )(jnp.stack([me, c, me if block is None else block]).astype(jnp.int32), *args)


def _pack(arrays, rows):
    flat = jnp.concatenate([a.reshape(-1) for a in arrays])
    return jnp.pad(flat, (0, rows * LANES - flat.size)).reshape(rows, LANES)


def _unpack(buf, shapes):
    flat = buf.reshape(-1)
    out, at = [], 0
    for s in shapes:
        n = math.prod(s)
        out.append(flat[at:at + n].reshape(s))
        at += n
    return out


CONV_SHARDS = [(DEPTH, K_A, D_A // N_CHIPS), (DEPTH, K_C, D_C // N_CHIPS), (DEPTH, K_F, 2 * D_FF // N_CHIPS)]
CONV_ROWS = 32
SMALL_ROWS = 640


def _join_cols(g):
    n, l, r, c = g.shape
    return jnp.transpose(g, (1, 2, 0, 3)).reshape(l, r, n * c)


BIG = ["w_in", "w_out", "w_up", "w_down"]
TILE_MM = 512
TILE_TN = 1024
TILE_EW = 256


def _pad_rows(a, rows):
    return jnp.pad(a, ((0, rows - a.shape[0]), (0, 0)))


def _row(a):
    return a.reshape(1, -1)


def _tile_perm(tt):
    p = lax.broadcasted_iota(jnp.int32, (tt, tt), 0)
    tok = lax.broadcasted_iota(jnp.int32, (tt, tt), 1)
    return ((tt // 8) * (p % 8) + p // 8 == tok).astype(F32)


def _layer_params(wl, tt):
    n = tt // CHUNK
    tril = jnp.tril(jnp.ones((CHUNK, CHUNK), bool))
    wm = jnp.where(tril[None], wl["sgu_w"], 0.0)
    eye = jnp.eye(n, dtype=F32)
    wt = (eye[None, :, None, :, None] * wm[:, None, :, None, :]).reshape(N_HEADS_B, tt, tt)
    bias_e = jnp.repeat(wl["sgu_b"].T, HEAD, axis=1)
    return (_pad_rows(wl["conv_a_w"], 8), _row(wl["sgu_ln_g"]), _row(wl["sgu_ln_b"]), wt.astype(MM_DTYPE),
            jnp.tile(bias_e, (n, 1)), _pad_rows(wl["conv_c_w"], 32), _row(wl["conv_c_b"]), _row(wl["conv_ln_g"]),
            _row(wl["conv_ln_b"]))


def layer_fwd(x, wl, gw, nxt=None, tm=TILE_MM, tt=TILE_EW):
    mp = _layer_params(wl, tt)
    ride = pass_rider([gw["w_down"]]) if gw.get("pass_down") else None
    (z, h), done = norm_matmul(x, _row(wl["pre_mix_g"]), gw["w_in"], tm, rider=ride)
    gw = {n: (done[0] if ride and n == "w_down" else gw[n]) for n in BIG}
    ride = spread_rider([nxt["w_in"], nxt["w_out"]]) if nxt else None
    (o, x1, cv), done = mixer_fwd(z, x, mp, _tile_perm(tt), _row(wl["grp_norm_g"]), gw["w_out"], _row(wl["post_mix_g"]), tt,
                                  rider=ride)
    ride = both_riders(spread_rider([nxt["w_up"]]), pass_rider(list(done))) if nxt else None
    (up0, h2), done = norm_matmul(x1, _row(wl["pre_ffn_g"]), gw["w_up"], tm, rider=ride)
    if nxt:
        nxt = dict(nxt, w_up=done[0], w_in=done[1], w_out=done[2])
        ride = both_riders(spread_rider([nxt["w_down"]]), pass_rider([nxt["w_up"]]))
    (d, x2), done = ffn_fwd(up0, x1, _pad_rows(wl["ffn_conv_w"], 8), gw["w_down"], _row(wl["post_ffn_g"]), tt, rider=ride)
    if nxt:
        nxt = dict(nxt, w_down=done[0], w_up=done[1], pass_down=True)
    return x2, dict(x=x, z=z, h=h, o=o, x1=x1, up0=up0, h2=h2, d=d, cv=cv, gw=gw), nxt


WIDE = ["w_up", "w_down"]
NARROW = ["w_in", "w_out"]


def layer_bwd(dx2, wl, layer, sv, pend=None, exchange=True, tm=TILE_MM, tt=TILE_EW):
    mp = _layer_params(wl, tt)
    gw = sv["gw"]
    tk = min(TILE_TN, dx2.shape[0])
    at = {n: BIG.index(n) for n in BIG}
    g = {}
    ride = scatter_rider([sw for _, sw in pend["narrow"]]) if pend else None
    (dd, act, dup0, dpg, dcw), arrived = ffn_bwd(dx2, sv["d"], sv["up0"], _pad_rows(wl["ffn_conv_w"], 8), gw["w_down"],
                                                 _row(wl["post_ffn_g"]), tt, rider=ride)
    fbuf = list(pend["fbuf"]) if pend else grad_buffers()
    if pend:
        done = add_chips([own for own, _ in pend["narrow"]], arrived, [fbuf[at[n]] for n in NARROW], pend["layer"])
        for n, f in zip(NARROW, done):
            fbuf[at[n]] = f
    g["post_ffn_g"] = dpg[0]
    g["ffn_conv_w"] = dcw[:K_F]
    gl = {}
    gl["w_down"] = matmul_tn_down(act, dd, tk)
    gl["w_up"] = matmul_tn_cols(sv["h2"], dup0, tk)
    ride = swap_rider([gl[n] for n in WIDE]) if exchange else None
    (dx1, dg), got = matmul_nt_norm_bwd(dup0, gw["w_up"], sv["x1"], _row(wl["pre_ffn_g"]), dx2, tm, rider=ride)
    g["pre_ffn_g"] = dg[0]
    wide = add_halves([gl[n] for n in WIDE], got) if exchange else None
    ride = scatter_rider([sw for _, sw in wide]) if exchange else None
    if pend:
        ride = both_riders(join_rider(fbuf, [pend["layer"]] * len(fbuf)), ride)
    (dz, do, yn, dpg, dgg, dcaw, dlng, dlnb, dwm, dbias, dccw, dccb, dclg, dclb), rode = mixer_bwd(
        dx1, sv["o"], sv["z"], sv["cv"], mp, _tile_perm(tt), _row(wl["grp_norm_g"]), gw["w_out"],
        _row(wl["post_mix_g"]), tt, rider=ride)
    if exchange:
        fbuf, arrived = (list(rode[:len(BIG)]), rode[len(BIG):]) if pend else (fbuf, rode)
        done = add_chips([own for own, _ in wide], arrived, [fbuf[at[n]] for n in WIDE], layer)
        for n, f in zip(WIDE, done):
            fbuf[at[n]] = f
    g["post_mix_g"] = dpg[0]
    g["grp_norm_g"] = dgg[0]
    g["conv_a_w"] = dcaw[:K_A]
    g["sgu_ln_g"] = dlng[0]
    g["sgu_ln_b"] = dlnb[0]
    g["sgu_w"] = dwm
    g["sgu_b"] = jnp.sum(dbias.reshape(CHUNK, N_HEADS_B, HEAD), axis=2).T
    g["conv_c_w"] = dccw[:K_C]
    g["conv_c_b"] = dccb[0]
    g["conv_ln_g"] = dclg[0]
    g["conv_ln_b"] = dclb[0]
    gl["w_out"] = matmul_tn_rows(yn, do, tk)
    gl["w_in"] = matmul_tn_in(sv["h"], dz, tk)
    ride = swap_rider([gl[n] for n in NARROW]) if exchange else None
    (dx, dg), got = matmul_nt_norm_bwd(dz, gw["w_in"], sv["x"], _row(wl["pre_mix_g"]), dx1, tm, rider=ride)
    g["pre_mix_g"] = dg[0]
    if not exchange:
        return dx, g, gl
    narrow = add_halves([gl[n] for n in NARROW], got)
    return dx, g, dict(narrow=narrow, fbuf=fbuf, layer=layer)


def grad_buffers():
    return [lax.empty(s, F32) for s in ((DEPTH, D_MODEL, D_IN // N_CHIPS), (DEPTH, D_MODEL // N_CHIPS, D_MODEL),
                                        (DEPTH, D_MODEL, 2 * D_FF // N_CHIPS), (DEPTH, D_FF // N_CHIPS, D_MODEL))]


CONV = ["conv_a_w", "conv_c_w", "ffn_conv_w"]
REPL = ["pre_mix_g", "sgu_ln_g", "sgu_ln_b", "sgu_w", "sgu_b", "conv_c_b", "conv_ln_g", "conv_ln_b", "grp_norm_g",
        "post_mix_g", "pre_ffn_g", "post_ffn_g"]
WEIGHTS = ["pre_mix_g", "w_in", "conv_a_w", "sgu_ln_g", "sgu_ln_b", "sgu_w", "sgu_b", "conv_c_w", "conv_c_b", "conv_ln_g",
           "conv_ln_b", "grp_norm_g", "w_out", "post_mix_g", "pre_ffn_g", "w_up", "ffn_conv_w", "w_down", "post_ffn_g"]


def kernel(x, pre_mix_g, w_in, conv_a_w, sgu_ln_g, sgu_ln_b, sgu_w, sgu_b, conv_c_w, conv_c_b, conv_ln_g, conv_ln_b, grp_norm_g, w_out, post_mix_g, pre_ffn_g, w_up, ffn_conv_w, w_down, post_ffn_g, loss_target, m_pre_mix_g, m_w_in, m_conv_a_w, m_sgu_ln_g, m_sgu_ln_b, m_sgu_w, m_sgu_b, m_conv_c_w, m_conv_c_b, m_conv_ln_g, m_conv_ln_b, m_grp_norm_g, m_w_out, m_post_mix_g, m_pre_ffn_g, m_w_up, m_ffn_conv_w, m_w_down, m_post_ffn_g, v_pre_mix_g, v_w_in, v_conv_a_w, v_sgu_ln_g, v_sgu_ln_b, v_sgu_w, v_sgu_b, v_conv_c_w, v_conv_c_b, v_conv_ln_g, v_conv_ln_b, v_grp_norm_g, v_w_out, v_post_mix_g, v_pre_ffn_g, v_w_up, v_ffn_conv_w, v_w_down, v_post_ffn_g):
    w = dict(pre_mix_g=pre_mix_g, w_in=w_in, conv_a_w=conv_a_w, sgu_ln_g=sgu_ln_g, sgu_ln_b=sgu_ln_b, sgu_w=sgu_w, sgu_b=sgu_b,
             conv_c_w=conv_c_w, conv_c_b=conv_c_b, conv_ln_g=conv_ln_g, conv_ln_b=conv_ln_b, grp_norm_g=grp_norm_g,
             w_out=w_out, post_mix_g=post_mix_g, pre_ffn_g=pre_ffn_g, w_up=w_up, ffn_conv_w=ffn_conv_w, w_down=w_down,
             post_ffn_g=post_ffn_g)
    m = dict(pre_mix_g=m_pre_mix_g, w_in=m_w_in, conv_a_w=m_conv_a_w, sgu_ln_g=m_sgu_ln_g, sgu_ln_b=m_sgu_ln_b,
             sgu_w=m_sgu_w, sgu_b=m_sgu_b, conv_c_w=m_conv_c_w, conv_c_b=m_conv_c_b, conv_ln_g=m_conv_ln_g,
             conv_ln_b=m_conv_ln_b, grp_norm_g=m_grp_norm_g, w_out=m_w_out, post_mix_g=m_post_mix_g,
             pre_ffn_g=m_pre_ffn_g, w_up=m_w_up, ffn_conv_w=m_ffn_conv_w, w_down=m_w_down, post_ffn_g=m_post_ffn_g)
    v = dict(pre_mix_g=v_pre_mix_g, w_in=v_w_in, conv_a_w=v_conv_a_w, sgu_ln_g=v_sgu_ln_g, sgu_ln_b=v_sgu_ln_b,
             sgu_w=v_sgu_w, sgu_b=v_sgu_b, conv_c_w=v_conv_c_w, conv_c_b=v_conv_c_b, conv_ln_g=v_conv_ln_g,
             conv_ln_b=v_conv_ln_b, grp_norm_g=v_grp_norm_g, w_out=v_w_out, post_mix_g=v_post_mix_g,
             pre_ffn_g=v_pre_ffn_g, w_up=v_w_up, ffn_conv_w=v_ffn_conv_w, w_down=v_w_down, post_ffn_g=v_post_ffn_g)
    chip = 2 * lax.axis_index("x") + lax.axis_index("y")

    convs = gather_small(_pack([w[n] for n in CONV], CONV_ROWS))
    gws = [{n: cast_shard(w[n], layer, chip) for n in BIG} for layer in range(DEPTH)]
    first = run_rider(pass_rider(run_rider(spread_rider([gws[0][n] for n in BIG]))))
    gws[0] = dict(zip(BIG, first))
    cparts = [_unpack(convs[j], CONV_SHARDS) for j in range(N_CHIPS)]
    full = dict(w)
    for i, n in enumerate(CONV):
        full[n] = _join_cols(jnp.stack([p[i] for p in cparts]))

    xc = to_tiles(x[0], TILE_EW)
    saved = []
    for layer in range(DEPTH):
        nxt = gws[layer + 1] if layer + 1 < DEPTH else None
        xc, sv, nxt = layer_fwd(xc, {n: full[n][layer] for n in REPL + CONV}, gws[layer], nxt)
        if nxt:
            gws[layer + 1] = nxt
        saved.append(sv)
    dxc, loss_part = loss_head(xc, to_tiles(loss_target[0], TILE_EW), TILE_MM)
    loss = lax.psum(loss_part[0, 0], ("x", "y", "c"))
    small = [None] * DEPTH
    pend = None
    for layer in reversed(range(DEPTH)):
        dxc, small[layer], pend = layer_bwd(dxc, {n: full[n][layer] for n in REPL + CONV}, layer, saved[layer], pend)
    grads = {n: jnp.stack([small[layer][n] for layer in range(DEPTH)]) for n in REPL + CONV}

    gsmall = _pack([grads[n] for n in REPL + CONV], SMALL_ROWS).reshape(N_CHIPS, SMALL_ROWS // N_CHIPS, LANES)
    sums = pend["narrow"] + add_halves([gsmall], run_rider(swap_rider([gsmall])), wire=F32)
    arrived = run_rider(scatter_rider([sw for _, sw in sums]))
    fbuf = list(pend["fbuf"])
    at = [BIG.index(n) for n in NARROW]
    for i, f in zip(at, add_chips([own for own, _ in sums[:2]], arrived[:2], [fbuf[i] for i in at], 0)):
        fbuf[i] = f
    fbuf += add_chips([sums[2][0]], arrived[2:], [lax.empty(gsmall.shape, F32)])
    joined = run_rider(join_rider(fbuf, [0] * len(BIG) + [None]))
    out_g = dict(zip(BIG, joined))
    tot = run_rider(pass_rider(run_rider(spread_rider([joined[len(BIG)]]))))[0].reshape(SMALL_ROWS, LANES)
    shapes = [grads[n].shape for n in REPL + CONV]
    for n, gfull in zip(REPL + CONV, _unpack(tot, shapes)):
        if n in CONV:
            width = gfull.shape[-1] // N_CHIPS
            gfull = lax.dynamic_slice_in_dim(gfull, chip * width, width, axis=2)
        out_g[n] = gfull

    deltas, new_m, new_v = {}, {}, {}
    for n in WEIGHTS:
        deltas[n], new_m[n], new_v[n] = adamw(w[n], out_g[n], m[n], v[n])
    return (loss, from_tiles(dxc, TILE_EW)[None], *[out_g[n] for n in WEIGHTS], *[deltas[n] for n in WEIGHTS], *[new_m[n] for n in WEIGHTS],
            *[new_v[n] for n in WEIGHTS])
```

```python
import math
from typing import Callable, NamedTuple

import jax
import jax.numpy as jnp
from jax import lax
from jax.experimental import pallas as pl
from jax.experimental.pallas import tpu as pltpu

F32 = jnp.float32
BF16 = jnp.bfloat16
MM_DTYPE = BF16

D_MODEL = 1024
DEPTH = 4
D_A = 256
D_B = 384
D_C = 384
D_IN = 3 * D_A + 2 * D_B + 2 * D_C
D_FF = 2816
K_A = 3
K_C = 31
K_F = 3
CHUNK = 128
HEAD = 64
N_HEADS_B = D_B // HEAD
EPS = 1e-6
N_CHIPS = 4

ADAM_LR = 0.001
ADAM_B1 = 0.9
ADAM_B2 = 0.999
ADAM_EPS = 1e-08
ADAM_WD = 0.01
ADAM_STEP = 10

LANES = 1024
VMEM_LIMIT = 56 * 1024 * 1024

MESH_ID = pl.DeviceIdType.MESH
_ANY = pl.BlockSpec(memory_space=pl.ANY)


def _params(sem=None):
    return pltpu.CompilerParams(dimension_semantics=sem, vmem_limit_bytes=VMEM_LIMIT)


def _const_spec(shape):
    nd = len(shape)
    return pl.BlockSpec(shape, lambda *_: (0,) * nd, pipeline_mode=pl.Buffered(1))


def _rowsum8(a):
    r, c = a.shape
    return jnp.sum(a.reshape(r // 8, 8, c), axis=0)


def _rstd(x):
    return lax.rsqrt(jnp.mean(x * x, axis=-1, keepdims=True) + EPS)


def _rms_bwd(x, r, g, dy):
    gdy = g * dy
    return r * gdy - x * (r * r * r) * jnp.mean(gdy * x, axis=-1, keepdims=True)


def _ln_fwd(x):
    mu = jnp.mean(x, axis=-1, keepdims=True)
    xc = x - mu
    r = lax.rsqrt(jnp.mean(xc * xc, axis=-1, keepdims=True) + EPS)
    return xc * r, r


def _ln_bwd(xh, r, dxh):
    return r * (dxh - jnp.mean(dxh, axis=-1, keepdims=True) - xh * jnp.mean(dxh * xh, axis=-1, keepdims=True))


def _gelu(x):
    return 0.5 * x * (1.0 + lax.erf(x * (1.0 / math.sqrt(2.0))))


def _gelu_grad(x):
    cdf = 0.5 * (1.0 + lax.erf(x * (1.0 / math.sqrt(2.0))))
    pdf = jnp.exp(-0.5 * x * x) * (1.0 / math.sqrt(2.0 * math.pi))
    return cdf + x * pdf


def _dot(a, b):
    return jnp.dot(a, b, preferred_element_type=F32)


def _dot_nt(a, b):
    return lax.dot_general(a, b, (((1,), (1,)), ((), ())), preferred_element_type=F32)


def _dot_tn(a, b):
    return lax.dot_general(a, b, (((0,), (0,)), ((), ())), preferred_element_type=F32)


def _col_chunk(n):
    for c in (1408, 1024, 768, 512, 256, 128):
        if n % c == 0:
            return c
    raise ValueError(n)


class Rider(NamedTuple):
    name: str
    inputs: list
    out_shapes: list
    aliases: dict
    sems: tuple
    start: Callable
    wait: Callable


def _pallas(body, rider, *, name, steps, in_specs, out_specs, out_shape, scratch_shapes, args):
    if rider is None:
        res = pl.pallas_call(body, name=name, grid=(steps,), in_specs=in_specs, out_specs=out_specs, out_shape=out_shape,
                             scratch_shapes=scratch_shapes, compiler_params=_params(("arbitrary",)))(*args)
        return res, []
    n_in, n_out, n_scr = len(in_specs), len(out_specs), len(scratch_shapes)
    r_in, r_out = len(rider.inputs), len(rider.out_shapes)

    def wrapped(*refs):
        ins, rin = refs[:n_in], refs[n_in:n_in + r_in]
        at = n_in + r_in
        outs, rout = refs[at:at + n_out], refs[at + n_out:at + n_out + r_out]
        at += n_out + r_out
        scr, rsem = refs[at:at + n_scr], refs[at + n_scr:]

        @pl.when(pl.program_id(0) == 0)
        def _():
            rider.start(rin, rout, rsem)

        body(*ins, *outs, *scr)

        @pl.when(pl.program_id(0) == steps - 1)
        def _():
            rider.wait(rin, rout, rsem)

    res = pl.pallas_call(
        wrapped, name=name + "_" + rider.name, grid=(steps,), in_specs=list(in_specs) + [_ANY] * r_in,
        out_specs=list(out_specs) + [_ANY] * r_out, out_shape=list(out_shape) + list(rider.out_shapes),
        scratch_shapes=list(scratch_shapes) + [pltpu.SemaphoreType.DMA((n,)) for n in rider.sems],
        input_output_aliases={n_in + i: n_out + o for i, o in rider.aliases.items()},
        compiler_params=pltpu.CompilerParams(dimension_semantics=("arbitrary",), vmem_limit_bytes=VMEM_LIMIT,
                                             has_side_effects=True),
    )(*args, *rider.inputs)
    return res[:n_out], res[n_out:]


def run_rider(rider):
    def body(*refs):
        r_in, r_out = len(rider.inputs), len(rider.out_shapes)
        rin, rout, rsem = refs[:r_in], refs[r_in:r_in + r_out], refs[r_in + r_out:]
        rider.start(rin, rout, rsem)
        rider.wait(rin, rout, rsem)

    return pl.pallas_call(
        body, name=rider.name, in_specs=[_ANY] * len(rider.inputs), out_specs=[_ANY] * len(rider.out_shapes),
        out_shape=list(rider.out_shapes), scratch_shapes=[pltpu.SemaphoreType.DMA((n,)) for n in rider.sems],
        input_output_aliases=dict(rider.aliases), compiler_params=pltpu.CompilerParams(has_side_effects=True),
    )(*rider.inputs)


def _weight_spec(wg):
    return _const_spec(wg.shape)


def _join_col_blocks(w_ref, w_scr):
    c = w_ref.shape[2]
    for j in range(N_CHIPS):
        w_scr[:, c * j:c * (j + 1)] = w_ref[j]


def norm_matmul(x, g, wg, tm, rider=None):
    t, d = x.shape
    assert t % tm == 0, (t, tm)
    cw = wg.shape[2]
    n = N_CHIPS * cw
    aligned = cw % 128 == 0
    cn = cw if aligned else _col_chunk(n)

    def body(x_ref, g_ref, w_ref, o_ref, h_ref, *scr):
        if not aligned:
            @pl.when(pl.program_id(0) == 0)
            def _():
                _join_col_blocks(w_ref, scr[0])

        xv = x_ref[...]
        h = (xv * _rstd(xv) * g_ref[...]).astype(MM_DTYPE)
        h_ref[...] = h
        for j, c0 in enumerate(range(0, n, cn)):
            wv = w_ref[j] if aligned else scr[0][:, c0:c0 + cn]
            o_ref[:, c0:c0 + cn] = _dot(h, wv)

    return _pallas(
        body, rider, name="norm_matmul", steps=t // tm,
        in_specs=[pl.BlockSpec((tm, d), lambda i: (i, 0)), _const_spec((1, d)), _weight_spec(wg)],
        out_specs=[pl.BlockSpec((tm, n), lambda i: (i, 0)), pl.BlockSpec((tm, d), lambda i: (i, 0))],
        out_shape=[jax.ShapeDtypeStruct((t, n), F32), jax.ShapeDtypeStruct((t, d), MM_DTYPE)],
        scratch_shapes=[] if aligned else [pltpu.VMEM((d, n), MM_DTYPE)],
        args=(x, g, wg))


def matmul_nt_norm_bwd(gy, wg, x, g, dres, tm, rider=None):
    t, n = gy.shape
    assert t % tm == 0, (t, tm)
    d, cw = wg.shape[1], wg.shape[2]
    aligned = cw % 128 == 0
    cn = cw if aligned else _col_chunk(n)
    steps = t // tm

    def body(gy_ref, w_ref, x_ref, g_ref, dres_ref, dx_ref, dg_ref, acc_ref, *scr):
        i = pl.program_id(0)

        @pl.when(i == 0)
        def _():
            acc_ref[...] = jnp.zeros_like(acc_ref)
            if not aligned:
                _join_col_blocks(w_ref, scr[0])

        dh = jnp.zeros((tm, d), F32)
        for j, c0 in enumerate(range(0, n, cn)):
            wv = w_ref[j] if aligned else scr[0][:, c0:c0 + cn]
            dh = dh + _dot_nt(gy_ref[:, c0:c0 + cn], wv)
        xv = x_ref[...]
        r = _rstd(xv)
        gv = g_ref[...]
        dx_ref[...] = dres_ref[...] + _rms_bwd(xv, r, gv, dh)
        acc_ref[...] += _rowsum8(dh * xv * r)

        @pl.when(i == steps - 1)
        def _():
            dg_ref[...] = jnp.sum(acc_ref[...], axis=0, keepdims=True)

    return _pallas(
        body, rider, name="matmul_nt_norm_bwd", steps=steps,
        in_specs=[pl.BlockSpec((tm, n), lambda i: (i, 0)), _weight_spec(wg), pl.BlockSpec((tm, d), lambda i: (i, 0)),
                  _const_spec((1, d)), pl.BlockSpec((tm, d), lambda i: (i, 0))],
        out_specs=[pl.BlockSpec((tm, d), lambda i: (i, 0)), pl.BlockSpec((1, d), lambda i: (0, 0))],
        out_shape=[jax.ShapeDtypeStruct((t, d), F32), jax.ShapeDtypeStruct((1, d), F32)],
        scratch_shapes=[pltpu.VMEM((8, d), F32)] + ([] if aligned else [pltpu.VMEM((d, n), MM_DTYPE)]),
        args=(gy, wg, x, g, dres))


def matmul_tn_cols(a, b, tk):
    t, r = a.shape
    c = b.shape[1] // N_CHIPS
    assert t % tk == 0 and r % 8 == 0 and c % 128 == 0, (a.shape, b.shape, tk)

    def body(a_ref, b_ref, o_ref):
        @pl.when(pl.program_id(1) == 0)
        def _():
            o_ref[...] = jnp.zeros_like(o_ref)

        o_ref[...] += _dot_tn(a_ref[...], b_ref[...])

    a_spec = pl.BlockSpec((tk, r), lambda j, k: (k, 0))
    b_spec = pl.BlockSpec((tk, c), lambda j, k: (k, j))
    return pl.pallas_call(
        body, name="matmul_tn_cols", grid=(N_CHIPS, t // tk), in_specs=[a_spec, b_spec],
        out_specs=pl.BlockSpec((None, r, c), lambda j, k: (j, 0, 0)),
        out_shape=jax.ShapeDtypeStruct((N_CHIPS, r, c), F32),
        compiler_params=_params(("arbitrary", "arbitrary")),
    )(a, b)


def matmul_tn_down(act, dd, tk):
    t, m = act.shape
    c = dd.shape[1]
    r = m // N_CHIPS
    assert t % tk == 0, (t, tk)
    steps = t // tk

    def body(a_ref, b_ref, o_ref, acc):
        k = pl.program_id(1)

        @pl.when(k == 0)
        def _():
            acc[...] = jnp.zeros_like(acc)

        acc[...] += _dot_tn(a_ref[...], b_ref[...])

        @pl.when(k == steps - 1)
        def _():
            o_ref[0] = acc[0:r, :]
            o_ref[1] = acc[r:2 * r, :]

    return pl.pallas_call(
        body, name="matmul_tn_down", grid=(2, steps),
        in_specs=[pl.BlockSpec((tk, 2 * r), lambda p, k: (k, p)), pl.BlockSpec((tk, c), lambda p, k: (k, 0))],
        out_specs=pl.BlockSpec((2, r, c), lambda p, k: (p, 0, 0)),
        out_shape=jax.ShapeDtypeStruct((N_CHIPS, r, c), F32),
        scratch_shapes=[pltpu.VMEM((2 * r, c), F32)],
        compiler_params=_params(("arbitrary", "arbitrary")),
    )(act, dd)


def matmul_tn_rows(a, b, tk):
    t, m = a.shape
    n = b.shape[1]
    r = m // N_CHIPS
    assert t % tk == 0 and r % 8 == 0, (a.shape, b.shape, tk)
    steps = t // tk

    def body(a_ref, b_ref, o_ref, acc):
        k = pl.program_id(0)

        @pl.when(k == 0)
        def _():
            acc[...] = jnp.zeros_like(acc)

        acc[...] += _dot_tn(a_ref[...], b_ref[...])

        @pl.when(k == steps - 1)
        def _():
            for j in range(N_CHIPS):
                o_ref[j] = acc[r * j:r * (j + 1), :]

    return pl.pallas_call(
        body, name="matmul_tn_rows", grid=(steps,),
        in_specs=[pl.BlockSpec((tk, m), lambda k: (k, 0)), pl.BlockSpec((tk, n), lambda k: (k, 0))],
        out_specs=pl.BlockSpec((N_CHIPS, r, n), lambda k: (0, 0, 0)),
        out_shape=jax.ShapeDtypeStruct((N_CHIPS, r, n), F32),
        scratch_shapes=[pltpu.VMEM((m, n), F32)],
        compiler_params=_params(("arbitrary",)),
    )(a, b)


def matmul_tn_in(h, dz, tk):
    t, d = h.shape
    n = dz.shape[1]
    c = n // N_CHIPS
    assert t % tk == 0, (t, tk)
    steps = t // tk

    def body(a_ref, b_ref, o_ref, acc):
        k = pl.program_id(0)

        @pl.when(k == 0)
        def _():
            acc[...] = jnp.zeros_like(acc)

        acc[...] += _dot_tn(a_ref[...], b_ref[...])

        @pl.when(k == steps - 1)
        def _():
            for j in range(N_CHIPS):
                o_ref[j] = acc[:, c * j:c * (j + 1)]

    return pl.pallas_call(
        body, name="matmul_tn_in", grid=(steps,),
        in_specs=[pl.BlockSpec((tk, d), lambda k: (k, 0)), pl.BlockSpec((tk, n), lambda k: (k, 0))],
        out_specs=pl.BlockSpec((N_CHIPS, d, c), lambda k: (0, 0, 0)),
        out_shape=jax.ShapeDtypeStruct((N_CHIPS, d, c), F32),
        scratch_shapes=[pltpu.VMEM((d, n), F32)],
        compiler_params=_params(("arbitrary",)),
    )(h, dz)


def to_tiles(a, tt):
    t = a.shape[0]
    return a.reshape((t // tt, 8, tt // 8) + a.shape[1:]).swapaxes(1, 2).reshape(a.shape)


def from_tiles(a, tt):
    t = a.shape[0]
    return a.reshape((t // tt, tt // 8, 8) + a.shape[1:]).swapaxes(1, 2).reshape(a.shape)


def _roll_sublanes(a, shift):
    n = a.shape[0] // 8
    return pltpu.roll(a.reshape(n, 8, a.shape[1]), shift, 1).reshape(a.shape)


def _halo_before(cur_last, prev_last):
    sub = lax.broadcasted_iota(jnp.int32, cur_last.shape, 0) % 8
    return jnp.where(sub == 0, _roll_sublanes(prev_last, 1), _roll_sublanes(cur_last, 1))


def _halo_after(cur_first, next_first):
    sub = lax.broadcasted_iota(jnp.int32, cur_first.shape, 0) % 8
    return jnp.where(sub == 7, _roll_sublanes(next_first, 7), _roll_sublanes(cur_first, 7))


def _conv_causal(ext, cur, prev_last, w, taps, tt, cols=None):
    hr = 8 * (taps - 1)
    cs = slice(None) if cols is None else cols
    ext[hr:hr + tt, cs] = cur
    ext[0:hr, cs] = _halo_before(cur[tt - hr:, :], prev_last)
    acc = w[0:1, :] * ext[0:tt, cs]
    for k in range(1, taps):
        acc = acc + w[k:k + 1, :] * ext[8 * k:8 * k + tt, cs]
    return acc


def _conv_anticausal(ext, cur, next_first, w, taps, tt, x=None, acc_w=None, cols=None):
    hr = 8 * (taps - 1)
    cs = slice(None) if cols is None else cols
    ext[0:tt, cs] = cur
    ext[tt:tt + hr, cs] = _halo_after(cur[0:hr, :], next_first)
    acc = None
    for k in range(taps):
        off = 8 * (taps - 1 - k)
        ld = ext[off:off + tt, cs]
        term = w[k:k + 1, :] * ld
        acc = term if acc is None else acc + term
        if x is not None:
            acc_w[k, :, cs] += _rowsum8(ld * x)
    return acc


def _dot_exact(a, b, dims):
    return lax.dot_general(a, b, (dims, ((), ())), precision=lax.Precision.HIGHEST, preferred_element_type=F32)


def _to_tile_order(perm, wt_ref, w_scr, transpose):
    pb = perm.astype(MM_DTYPE)
    for h in range(N_HEADS_B):
        half = (_dot_nt(pb, wt_ref[h]) if transpose else _dot(pb, wt_ref[h])).astype(MM_DTYPE)
        w_scr[h] = _dot_nt(half, pb).astype(MM_DTYPE)


def _project_rows(y, w_ref):
    r = w_ref.shape[1]
    acc = _dot(y[:, 0:r], w_ref[0])
    for j in range(1, N_CHIPS):
        acc = acc + _dot(y[:, r * j:r * (j + 1)], w_ref[j])
    return acc


def _head_select(parts):
    head = lax.broadcasted_iota(jnp.int32, parts[0].shape, 1) // HEAD
    acc = parts[0]
    for h in range(1, N_HEADS_B):
        acc = jnp.where(head == h, parts[h], acc)
    return acc


def _mixer_forward(z, prm, q, yc):
    _, lng, lnb, wm, bias_p, _, _, clg, clb = prm
    bg = z[:, 0:D_A]
    ya = bg * q
    o_b = 3 * D_A
    zu = z[:, o_b:o_b + D_B]
    zv = z[:, o_b + D_B:o_b + 2 * D_B]
    u = _gelu(zu)
    vh, rv = _ln_fwd(_gelu(zv))
    vnb = (vh * lng + lnb).astype(MM_DTYPE)
    s = _head_select([_dot(wm[h], vnb) for h in range(N_HEADS_B)]) + bias_p
    yb = u * s
    yh, rc = _ln_fwd(yc)
    l = yh * clg + clb
    sl = jax.nn.sigmoid(l)
    return dict(bg=bg, q=q, ya=ya, zu=zu, zv=zv, u=u, vh=vh, rv=rv, vnb=vnb, s=s, yb=yb, yh=yh, rc=rc, l=l, sl=sl,
                yo=l * sl)


def _conv_inputs(z):
    o_c = 3 * D_A + 2 * D_B
    a = z[:, o_c:o_c + D_C]
    sg = jax.nn.sigmoid(z[:, o_c + D_C:o_c + 2 * D_C])
    return z[:, D_A:2 * D_A] * z[:, 2 * D_A:3 * D_A], a * sg, a, sg


def _group_norm(f, gg):
    ya, yb, yo = f["ya"], f["yb"], f["yo"]
    ra, rb, ro = _rstd(ya), _rstd(yb), _rstd(yo)
    yn = jnp.concatenate([ya * ra * gg[:, 0:D_A], yb * rb * gg[:, D_A:D_A + D_B], yo * ro * gg[:, D_A + D_B:]], axis=1)
    return yn, (ra, rb, ro)


def _mixer_prm(refs, wp_scr, bias_scr):
    caw_ref, lng_ref, lnb_ref, _, _, ccw_ref, ccb_ref, clg_ref, clb_ref = refs
    wm = [wp_scr[h] for h in range(N_HEADS_B)]
    return (caw_ref[...], lng_ref[...], lnb_ref[...], wm, bias_scr[...], ccw_ref[...], ccb_ref[...], clg_ref[...],
            clb_ref[...])


def _mixer_param_specs(tt):
    return [_const_spec((8, D_A)), _const_spec((1, D_B)), _const_spec((1, D_B)), _const_spec((N_HEADS_B, tt, tt)),
            _const_spec((tt, D_B)), _const_spec((32, D_C)), _const_spec((1, D_C)), _const_spec((1, D_C)),
            _const_spec((1, D_C))]


HR_A = 8 * (K_A - 1)
HR_C = 8 * (K_C - 1)
HR_F = 8 * (K_F - 1)


def mixer_fwd(z, x, mp, perm, grp_g, wog, post_g, tt, rider=None):
    t = z.shape[0]
    assert t % tt == 0 and tt % CHUNK == 0 and tt >= HR_C, (t, tt)

    def body(z_ref, x_ref, *rest):
        prm_refs = rest[:9]
        (perm_ref, gg_ref, wo_ref, pg_ref, o_ref, x1_ref, cv_ref, pa_ext, yg_ext, pa_last, yg_last, wp_scr, bias_scr) = rest[9:]
        i = pl.program_id(0)

        @pl.when(i == 0)
        def _():
            pa_last[...] = jnp.zeros_like(pa_last)
            yg_last[...] = jnp.zeros_like(yg_last)
            _to_tile_order(perm_ref[...], prm_refs[3], wp_scr, False)
            bias_scr[...] = _dot_exact(perm_ref[...], prm_refs[4][...], ((1,), (0,)))

        zv = z_ref[...]
        prm = _mixer_prm(prm_refs, wp_scr, bias_scr)
        pa, yg, _, _ = _conv_inputs(zv)
        q = _conv_causal(pa_ext, pa, pa_last[...], prm[0], K_A, tt)
        yc = _conv_causal(yg_ext, yg, yg_last[...], prm[5], K_C, tt) + prm[6]
        pa_last[...] = pa[tt - HR_A:, :]
        yg_last[...] = yg[tt - HR_C:, :]
        cv_ref[:, 0:D_A] = q
        cv_ref[:, D_A:] = yc
        f = _mixer_forward(zv, prm, q, yc)
        yn, _ = _group_norm(f, gg_ref[...])
        o = _project_rows(yn.astype(MM_DTYPE), wo_ref)
        o_ref[...] = o
        x1_ref[...] = x_ref[...] + o * _rstd(o) * pg_ref[...]

    row = lambda c: pl.BlockSpec((tt, c), lambda i: (i, 0))
    return _pallas(
        body, rider, name="mixer_fwd", steps=t // tt,
        in_specs=[row(D_IN), row(D_MODEL)] + _mixer_param_specs(tt)
        + [_const_spec((tt, tt)), _const_spec((1, D_MODEL)), _weight_spec(wog), _const_spec((1, D_MODEL))],
        out_specs=[row(D_MODEL), row(D_MODEL), row(D_A + D_C)],
        out_shape=[jax.ShapeDtypeStruct((t, D_MODEL), F32), jax.ShapeDtypeStruct((t, D_MODEL), F32),
                   jax.ShapeDtypeStruct((t, D_A + D_C), F32)],
        scratch_shapes=[pltpu.VMEM((HR_A + tt, D_A), F32), pltpu.VMEM((HR_C + tt, D_C), F32),
                        pltpu.VMEM((HR_A, D_A), F32), pltpu.VMEM((HR_C, D_C), F32),
                        pltpu.VMEM((N_HEADS_B, tt, tt), MM_DTYPE), pltpu.VMEM((tt, D_B), F32)],
        args=(z, x, *mp, perm, grp_g, wog, post_g))


def mixer_bwd(dx1, o, z, cv, mp, perm, grp_g, wog, post_g, tt, rider=None):
    t = z.shape[0]
    assert t % tt == 0 and tt % CHUNK == 0 and tt >= HR_C, (t, tt)
    steps = t // tt

    def body(dx1_ref, o_ref, z_ref, cv_ref, *rest):
        prm_refs = rest[:9]
        (perm_ref, gg_ref, wo_ref, pg_ref,
         dz_ref, do_ref, yn_ref, dpg_ref, dgg_ref, dcaw_ref, dlng_ref, dlnb_ref, dwm_ref, dbias_ref, dccw_ref, dccb_ref,
         dclg_ref, dclb_ref,
         dq_ext, dyc_ext, dq_first, dyc_first, a_pg, a_gg, a_caw, a_lng, a_lnb, a_ccw, a_ccb, a_clg, a_clb,
         wp_scr, wpt_scr, bias_scr, a_wm, a_bias) = rest[9:]
        i = pl.program_id(0)
        small = (a_pg, a_gg, a_caw, a_lng, a_lnb, a_ccw, a_ccb, a_clg, a_clb)

        @pl.when(i == 0)
        def _():
            for ref in small + (a_wm, a_bias, dq_first, dyc_first):
                ref[...] = jnp.zeros_like(ref)
            _to_tile_order(perm_ref[...], prm_refs[3], wp_scr, False)
            _to_tile_order(perm_ref[...], prm_refs[3], wpt_scr, True)
            bias_scr[...] = _dot_exact(perm_ref[...], prm_refs[4][...], ((1,), (0,)))

        prm = _mixer_prm(prm_refs, wp_scr, bias_scr)
        caw, lng, lnb, wm, bias_p, ccw, ccb, clg, clb = prm

        zv = z_ref[...]
        pa, yg, a, sg = _conv_inputs(zv)
        f = _mixer_forward(zv, prm, cv_ref[:, 0:D_A], cv_ref[:, D_A:])
        gg = gg_ref[...]
        yn, (ra, rb, ro) = _group_norm(f, gg)
        yn_ref[...] = yn.astype(MM_DTYPE)

        ov = o_ref[...]
        dx1v = dx1_ref[...]
        r_o = _rstd(ov)
        pg = pg_ref[...]
        a_pg[...] += _rowsum8(dx1v * ov * r_o)
        do = _rms_bwd(ov, r_o, pg, dx1v).astype(MM_DTYPE)
        do_ref[...] = do
        dyn = jnp.concatenate([_dot_nt(do, wo_ref[j]) for j in range(N_CHIPS)], axis=1)

        dyn_a, dyn_b, dyn_c = dyn[:, 0:D_A], dyn[:, D_A:D_A + D_B], dyn[:, D_A + D_B:]
        ga, gb, gc = gg[:, 0:D_A], gg[:, D_A:D_A + D_B], gg[:, D_A + D_B:]
        a_gg[...] += _rowsum8(jnp.concatenate([dyn_a * f["ya"] * ra, dyn_b * f["yb"] * rb, dyn_c * f["yo"] * ro], axis=1))
        dya = _rms_bwd(f["ya"], ra, ga, dyn_a)
        dyb = _rms_bwd(f["yb"], rb, gb, dyn_b)
        dyo = _rms_bwd(f["yo"], ro, gc, dyn_c)

        dbg = dya * f["q"]
        dq = dya * f["bg"]
        dp = _conv_anticausal(dq_ext, dq, dq_first[...], caw, K_A, tt, x=pa, acc_w=a_caw)
        dq_first[...] = dq[0:HR_A, :]
        dcg = dp * zv[:, 2 * D_A:3 * D_A]
        dxa = dp * zv[:, D_A:2 * D_A]

        du = dyb * f["s"]
        ds = dyb * f["u"]
        dsb = ds.astype(MM_DTYPE)
        head = lax.broadcasted_iota(jnp.int32, (tt, D_B), 1) // HEAD
        a_bias[...] += ds
        parts = []
        for h in range(N_HEADS_B):
            a_wm[h] += _dot_nt(jnp.where(head == h, dsb, jnp.zeros_like(dsb)), f["vnb"])
            parts.append(_dot(wpt_scr[h], dsb))
        dvn = _head_select(parts)
        a_lng[...] += _rowsum8(dvn * f["vh"])
        a_lnb[...] += _rowsum8(dvn)
        dv = _ln_bwd(f["vh"], f["rv"], dvn * lng)
        dzu = du * _gelu_grad(f["zu"])
        dzv = dv * _gelu_grad(f["zv"])

        l, sl = f["l"], f["sl"]
        dl = dyo * (sl * (1.0 + l * (1.0 - sl)))
        a_clg[...] += _rowsum8(dl * f["yh"])
        a_clb[...] += _rowsum8(dl)
        dyc = _ln_bwd(f["yh"], f["rc"], dl * clg)
        a_ccb[...] += _rowsum8(dyc)
        dy = _conv_anticausal(dyc_ext, dyc, dyc_first[...], ccw, K_C, tt, x=yg, acc_w=a_ccw)
        dyc_first[...] = dyc[0:HR_C, :]
        da = dy * sg
        dg = dy * a * sg * (1.0 - sg)

        dz_ref[...] = jnp.concatenate([dbg, dcg, dxa, dzu, dzv, da, dg], axis=1).astype(MM_DTYPE)

        @pl.when(i == steps - 1)
        def _():
            red = lambda ref: jnp.sum(ref[...], axis=0, keepdims=True)
            dpg_ref[...] = red(a_pg)
            dgg_ref[...] = red(a_gg)
            dlng_ref[...] = red(a_lng)
            dlnb_ref[...] = red(a_lnb)
            dccb_ref[...] = red(a_ccb)
            dclg_ref[...] = red(a_clg)
            dclb_ref[...] = red(a_clb)
            dcaw_ref[...] = jnp.sum(a_caw[...], axis=1)
            dccw_ref[...] = jnp.sum(a_ccw[...], axis=1)
            pm = perm_ref[...]
            tril = lax.broadcasted_iota(jnp.int32, (CHUNK, CHUNK), 0) >= lax.broadcasted_iota(jnp.int32, (CHUNK, CHUNK), 1)
            for h in range(N_HEADS_B):
                dwt = _dot_exact(pm, _dot_exact(a_wm[h], pm, ((1,), (0,))), ((0,), (0,)))
                dw = dwt[0:CHUNK, 0:CHUNK]
                for c in range(1, tt // CHUNK):
                    dw = dw + dwt[c * CHUNK:(c + 1) * CHUNK, c * CHUNK:(c + 1) * CHUNK]
                dwm_ref[h] = jnp.where(tril, dw, 0.0)
            dbt = _dot_exact(pm, a_bias[...], ((0,), (0,)))
            db = dbt[0:CHUNK, :]
            for c in range(1, tt // CHUNK):
                db = db + dbt[c * CHUNK:(c + 1) * CHUNK, :]
            lane_head = lax.broadcasted_iota(jnp.int32, (D_B, CHUNK), 0) // HEAD
            fold = (lane_head == lax.broadcasted_iota(jnp.int32, (D_B, CHUNK), 1)).astype(F32)
            dbias_ref[...] = _dot_exact(db, fold, ((1,), (0,)))

    rev = lambda c: pl.BlockSpec((tt, c), lambda i: (steps - 1 - i, 0))
    full = lambda shape: pl.BlockSpec(shape, lambda i: (0,) * len(shape))
    sds = jax.ShapeDtypeStruct
    return _pallas(
        body, rider, name="mixer_bwd", steps=steps,
        in_specs=[rev(D_MODEL), rev(D_MODEL), rev(D_IN), rev(D_A + D_C)] + _mixer_param_specs(tt)
        + [_const_spec((tt, tt)), _const_spec((1, D_MODEL)), _weight_spec(wog), _const_spec((1, D_MODEL))],
        out_specs=[rev(D_IN), rev(D_MODEL), rev(D_MODEL), full((1, D_MODEL)), full((1, D_MODEL)), full((8, D_A)),
                   full((1, D_B)), full((1, D_B)), full((N_HEADS_B, CHUNK, CHUNK)), full((CHUNK, CHUNK)), full((32, D_C)),
                   full((1, D_C)), full((1, D_C)), full((1, D_C))],
        out_shape=[sds((t, D_IN), MM_DTYPE), sds((t, D_MODEL), MM_DTYPE), sds((t, D_MODEL), MM_DTYPE),
                   sds((1, D_MODEL), F32), sds((1, D_MODEL), F32), sds((8, D_A), F32), sds((1, D_B), F32), sds((1, D_B), F32),
                   sds((N_HEADS_B, CHUNK, CHUNK), F32), sds((CHUNK, CHUNK), F32), sds((32, D_C), F32), sds((1, D_C), F32),
                   sds((1, D_C), F32), sds((1, D_C), F32)],
        scratch_shapes=[pltpu.VMEM((tt + HR_A, D_A), F32), pltpu.VMEM((tt + HR_C, D_C), F32),
                        pltpu.VMEM((HR_A, D_A), F32), pltpu.VMEM((HR_C, D_C), F32),
                        pltpu.VMEM((8, D_MODEL), F32), pltpu.VMEM((8, D_MODEL), F32), pltpu.VMEM((8, 8, D_A), F32),
                        pltpu.VMEM((8, D_B), F32), pltpu.VMEM((8, D_B), F32), pltpu.VMEM((32, 8, D_C), F32),
                        pltpu.VMEM((8, D_C), F32), pltpu.VMEM((8, D_C), F32), pltpu.VMEM((8, D_C), F32),
                        pltpu.VMEM((N_HEADS_B, tt, tt), MM_DTYPE), pltpu.VMEM((N_HEADS_B, tt, tt), MM_DTYPE),
                        pltpu.VMEM((tt, D_B), F32), pltpu.VMEM((N_HEADS_B, tt, tt), F32), pltpu.VMEM((tt, D_B), F32)],
        args=(dx1, o, z, cv, *mp, perm, grp_g, wog, post_g))


def _fetch_row_blocks(wg_ref, w_scr, sems):
    r = wg_ref.shape[1]
    copies = [pltpu.make_async_copy(wg_ref.at[j], w_scr.at[pl.ds(r * j, r), :], sems.at[j]) for j in range(N_CHIPS)]
    for cp in copies:
        cp.start()
    for cp in copies:
        cp.wait()


def _ffn_conv(ext, cw, c0, cn, tt):
    acc = cw[0:1, c0:c0 + cn] * ext[0:tt, c0:c0 + cn]
    for k in range(1, K_F):
        acc = acc + cw[k:k + 1, c0:c0 + cn] * ext[8 * k:8 * k + tt, c0:c0 + cn]
    return acc


def ffn_fwd(up0, x1, cw, wdg, post_g, tt, rider=None):
    t = up0.shape[0]
    assert t % tt == 0, (t, tt)
    cn = _col_chunk(D_FF)

    def body(up0_ref, x1_ref, cw_ref, wdg_ref, pg_ref, d_ref, x2_ref, ext, last, wd_ref, sems):
        i = pl.program_id(0)

        @pl.when(i == 0)
        def _():
            _fetch_row_blocks(wdg_ref, wd_ref, sems)
            last[...] = jnp.zeros_like(last)

        ext[HR_F:HR_F + tt, :] = up0_ref[...]
        ext[0:HR_F, :] = _halo_before(up0_ref[tt - HR_F:, :], last[...])
        last[...] = up0_ref[tt - HR_F:, :]
        cwv = cw_ref[...]
        d = jnp.zeros((tt, D_MODEL), F32)
        for c0 in range(0, D_FF, cn):
            gate = _ffn_conv(ext, cwv, c0, cn, tt)
            val = _ffn_conv(ext, cwv, D_FF + c0, cn, tt)
            act = (gate * jax.nn.sigmoid(gate) * val).astype(MM_DTYPE)
            d = d + _dot(act, wd_ref[c0:c0 + cn, :])
        d_ref[...] = d
        x2_ref[...] = x1_ref[...] + d * _rstd(d) * pg_ref[...]

    row = lambda c: pl.BlockSpec((tt, c), lambda i: (i, 0))
    return _pallas(
        body, rider, name="ffn_fwd", steps=t // tt,
        in_specs=[row(2 * D_FF), row(D_MODEL), _const_spec((8, 2 * D_FF)), _ANY, _const_spec((1, D_MODEL))],
        out_specs=[row(D_MODEL), row(D_MODEL)],
        out_shape=[jax.ShapeDtypeStruct((t, D_MODEL), F32), jax.ShapeDtypeStruct((t, D_MODEL), F32)],
        scratch_shapes=[pltpu.VMEM((HR_F + tt, 2 * D_FF), F32), pltpu.VMEM((HR_F, 2 * D_FF), F32),
                        pltpu.VMEM((D_FF, D_MODEL), MM_DTYPE), pltpu.SemaphoreType.DMA((N_CHIPS,))],
        args=(up0, x1, cw, wdg, post_g))


def ffn_bwd(dx2, d, up0, cw, wdg, post_g, tt, rider=None):
    t = up0.shape[0]
    assert t % tt == 0, (t, tt)
    steps = t // tt
    hb = tt // HR_F
    cn = _col_chunk(D_FF)

    def body(dx2_ref, d_ref, up0_ref, uh_ref, cw_ref, wdg_ref, pg_ref,
             dd_ref, act_ref, dup0_ref, dpg_ref, dcw_ref, ext, dup_ext, first, a_pg, a_cw, wd_ref, sems):
        i = pl.program_id(0)
        tile = steps - 1 - i

        @pl.when(i == 0)
        def _():
            _fetch_row_blocks(wdg_ref, wd_ref, sems)
            a_pg[...] = jnp.zeros_like(a_pg)
            a_cw[...] = jnp.zeros_like(a_cw)
            first[...] = jnp.zeros_like(first)

        ext[HR_F:HR_F + tt, :] = up0_ref[...]
        ext[0:HR_F, :] = _halo_before(up0_ref[tt - HR_F:, :], jnp.where(tile > 0, uh_ref[...], 0.0))
        cwv = cw_ref[...]
        dv = d_ref[...]
        dx2v = dx2_ref[...]
        r = _rstd(dv)
        a_pg[...] += _rowsum8(dx2v * dv * r)
        dd = _rms_bwd(dv, r, pg_ref[...], dx2v).astype(MM_DTYPE)
        dd_ref[...] = dd
        for c0 in range(0, D_FF, cn):
            gate = _ffn_conv(ext, cwv, c0, cn, tt)
            val = _ffn_conv(ext, cwv, D_FF + c0, cn, tt)
            sg = jax.nn.sigmoid(gate)
            sl = gate * sg
            act_ref[:, c0:c0 + cn] = (sl * val).astype(MM_DTYPE)
            da = _dot_nt(dd, wd_ref[c0:c0 + cn, :])
            dup_ext[0:tt, c0:c0 + cn] = da * val * (sg * (1.0 + gate * (1.0 - sg)))
            dup_ext[0:tt, D_FF + c0:D_FF + c0 + cn] = da * sl
        dup_ext[tt:tt + HR_F, :] = _halo_after(dup_ext[0:HR_F, :], first[...])
        first[...] = dup_ext[0:HR_F, :]
        for c0 in range(0, 2 * D_FF, cn):
            x = up0_ref[:, c0:c0 + cn]
            acc = None
            for k in range(K_F):
                off = 8 * (K_F - 1 - k)
                ld = dup_ext[off:off + tt, c0:c0 + cn]
                term = cwv[k:k + 1, c0:c0 + cn] * ld
                acc = term if acc is None else acc + term
                a_cw[k, :, c0:c0 + cn] += _rowsum8(ld * x)
            dup0_ref[:, c0:c0 + cn] = acc.astype(MM_DTYPE)

        @pl.when(i == steps - 1)
        def _():
            dpg_ref[...] = jnp.sum(a_pg[...], axis=0, keepdims=True)
            dcw_ref[...] = jnp.sum(a_cw[...], axis=1)

    rev = lambda c: pl.BlockSpec((tt, c), lambda i: (steps - 1 - i, 0))
    halo = pl.BlockSpec((HR_F, 2 * D_FF), lambda i: (jnp.maximum((steps - 1 - i) * hb - 1, 0), 0))
    full = lambda shape: pl.BlockSpec(shape, lambda i: (0,) * len(shape))
    sds = jax.ShapeDtypeStruct
    return _pallas(
        body, rider, name="ffn_bwd", steps=steps,
        in_specs=[rev(D_MODEL), rev(D_MODEL), rev(2 * D_FF), halo, _const_spec((8, 2 * D_FF)), _ANY,
                  _const_spec((1, D_MODEL))],
        out_specs=[rev(D_MODEL), rev(D_FF), rev(2 * D_FF), full((1, D_MODEL)), full((8, 2 * D_FF))],
        out_shape=[sds((t, D_MODEL), MM_DTYPE), sds((t, D_FF), MM_DTYPE), sds((t, 2 * D_FF), MM_DTYPE),
                   sds((1, D_MODEL), F32), sds((8, 2 * D_FF), F32)],
        scratch_shapes=[pltpu.VMEM((HR_F + tt, 2 * D_FF), F32), pltpu.VMEM((tt + HR_F, 2 * D_FF), F32),
                        pltpu.VMEM((HR_F, 2 * D_FF), F32), pltpu.VMEM((8, D_MODEL), F32),
                        pltpu.VMEM((8, 8, 2 * D_FF), F32), pltpu.VMEM((D_FF, D_MODEL), MM_DTYPE),
                        pltpu.SemaphoreType.DMA((N_CHIPS,))],
        args=(dx2, d, up0, up0, cw, wdg, post_g))


def loss_head(y, target, tm):
    t, d = y.shape
    assert t % tm == 0, (t, tm)
    steps = t // tm

    def body(y_ref, t_ref, dy_ref, loss_ref, acc):
        i = pl.program_id(0)

        @pl.when(i == 0)
        def _():
            acc[...] = jnp.zeros_like(acc)

        diff = y_ref[...] - t_ref[...]
        dy_ref[...] = diff * (1.0 / d)
        acc[...] += _rowsum8(diff * diff)

        @pl.when(i == steps - 1)
        def _():
            loss_ref[...] = (0.5 / d) * jnp.sum(jnp.sum(acc[...], axis=0, keepdims=True), axis=1, keepdims=True)

    row = pl.BlockSpec((tm, d), lambda i: (i, 0))
    return pl.pallas_call(
        body, name="loss_head", grid=(steps,), in_specs=[row, row],
        out_specs=[row, pl.BlockSpec((1, 1), lambda i: (0, 0))],
        out_shape=[jax.ShapeDtypeStruct((t, d), F32), jax.ShapeDtypeStruct((1, 1), F32)],
        scratch_shapes=[pltpu.VMEM((8, d), F32)],
        compiler_params=_params(("arbitrary",)),
    )(y, target)


def adamw(w, g, m, v):
    shape = w.shape
    cols = shape[-1]
    rows = w.size // cols
    tr = next((r for r in (512, 256, 128) if rows % r == 0 and rows > r), rows)
    c1 = 1.0 - ADAM_B1 ** ADAM_STEP
    c2 = 1.0 - ADAM_B2 ** ADAM_STEP

    def body(w_ref, g_ref, m_ref, v_ref, d_ref, nm_ref, nv_ref):
        gv = g_ref[...]
        nm = ADAM_B1 * m_ref[...] + (1.0 - ADAM_B1) * gv
        nv = ADAM_B2 * v_ref[...] + (1.0 - ADAM_B2) * (gv * gv)
        nm_ref[...] = nm
        nv_ref[...] = nv
        d_ref[...] = -ADAM_LR * ((nm / c1) / (jnp.sqrt(nv / c2) + ADAM_EPS) + ADAM_WD * w_ref[...])

    spec = pl.BlockSpec((tr, cols), lambda i: (i, 0))
    out = jax.ShapeDtypeStruct((rows, cols), F32)
    res = pl.pallas_call(
        body, name="adamw", grid=(rows // tr,), in_specs=[spec] * 4, out_specs=[spec] * 3, out_shape=[out] * 3,
        compiler_params=_params(("arbitrary",)),
    )(*[a.reshape(rows, cols) for a in (w, g, m, v)])
    return tuple(r.reshape(shape) for r in res)


def _place():
    return lax.axis_index("x"), lax.axis_index("y"), lax.axis_index("c")


def _other_chips(x, y):
    return [(1 - x, y, 2 * (1 - x) + y), (x, 1 - y, 2 * x + 1 - y), (1 - x, 1 - y, 2 * (1 - x) + 1 - y)]


def _sem_specs(*counts):
    return [pltpu.SemaphoreType.DMA((n,)) for n in counts]


def cast_shard(w, layer, chip):
    _, r, c = w.shape

    def body(chip_ref, w_ref, o_ref):
        del chip_ref
        o_ref[...] = w_ref[...].astype(MM_DTYPE)

    grid_spec = pltpu.PrefetchScalarGridSpec(
        num_scalar_prefetch=1, grid=(1,), in_specs=[pl.BlockSpec((None, r, c), lambda i, chip_ref: (layer, 0, 0))],
        out_specs=pl.BlockSpec((None, r, c), lambda i, chip_ref: (chip_ref[0], 0, 0)))
    return pl.pallas_call(
        body, name="cast_shard", grid_spec=grid_spec, out_shape=jax.ShapeDtypeStruct((N_CHIPS, r, c), MM_DTYPE),
        compiler_params=_params(("arbitrary",)),
    )(jnp.reshape(chip, (1,)).astype(jnp.int32), w)


def _row_half(buf, chip, mine, c):
    rh = buf.shape[1] // 2
    return buf.at[chip, pl.ds(pl.multiple_of((c if mine else 1 - c) * rh, 16), rh), :]


def spread_rider(bufs):
    n = len(bufs)

    def start(rin, rout, sems):
        x, y, c = _place()
        me = 2 * x + y
        for k, (px, py, _) in enumerate(_other_chips(x, y)):
            for i, buf in enumerate(rout):
                part = _row_half(buf, me, True, c)
                pltpu.make_async_remote_copy(
                    src_ref=part, dst_ref=part, send_sem=sems[0].at[n * k + i], recv_sem=sems[1].at[n * k + i],
                    device_id=(px, py, c), device_id_type=MESH_ID).start()

    def wait(rin, rout, sems):
        x, y, c = _place()
        for k, (_, _, pj) in enumerate(_other_chips(x, y)):
            for i, buf in enumerate(rout):
                part = _row_half(buf, pj, True, c)
                pltpu.make_async_remote_copy(
                    src_ref=part, dst_ref=part, send_sem=sems[0].at[n * k + i], recv_sem=sems[1].at[n * k + i],
                    device_id=(x, y, c), device_id_type=MESH_ID).wait()

    shapes = [jax.ShapeDtypeStruct(b.shape, b.dtype) for b in bufs]
    return Rider("spread", list(bufs), shapes, {i: i for i in range(n)}, (3 * n, 3 * n), start, wait)


def pass_rider(bufs):
    n = len(bufs)

    def start(rin, rout, sems):
        x, y, c = _place()
        for k, (_, _, pj) in enumerate(_other_chips(x, y)):
            for i, buf in enumerate(rout):
                part = _row_half(buf, pj, True, c)
                pltpu.make_async_remote_copy(
                    src_ref=part, dst_ref=part, send_sem=sems[0].at[n * k + i], recv_sem=sems[1].at[n * k + i],
                    device_id=(x, y, 1 - c), device_id_type=MESH_ID).start()

    def wait(rin, rout, sems):
        x, y, c = _place()
        for k, (_, _, pj) in enumerate(_other_chips(x, y)):
            for i, buf in enumerate(rout):
                part = _row_half(buf, pj, False, c)
                pltpu.make_async_remote_copy(
                    src_ref=part, dst_ref=part, send_sem=sems[0].at[n * k + i], recv_sem=sems[1].at[n * k + i],
                    device_id=(x, y, 1 - c), device_id_type=MESH_ID).wait()

    shapes = [jax.ShapeDtypeStruct(b.shape, b.dtype) for b in bufs]
    return Rider("pass", list(bufs), shapes, {i: i for i in range(n)}, (3 * n, 3 * n), start, wait)


def both_riders(a, b):
    na, oa, sa = len(a.inputs), len(a.out_shapes), len(a.sems)

    def start(rin, rout, sems):
        a.start(rin[:na], rout[:oa], sems[:sa])
        b.start(rin[na:], rout[oa:], sems[sa:])

    def wait(rin, rout, sems):
        a.wait(rin[:na], rout[:oa], sems[:sa])
        b.wait(rin[na:], rout[oa:], sems[sa:])

    aliases = dict(a.aliases)
    aliases.update({na + i: oa + o for i, o in b.aliases.items()})
    return Rider(a.name + "_" + b.name, a.inputs + b.inputs, a.out_shapes + b.out_shapes, aliases, a.sems + b.sems,
                 start, wait)


def gather_small(small):
    def body(small_ref, out_ref, send, recv, local):
        x, y, c = _place()
        me = 2 * x + y
        chips = _other_chips(x, y)
        own = pltpu.make_async_copy(small_ref, out_ref.at[me], local.at[0])
        own.start()
        sends = [pltpu.make_async_remote_copy(src_ref=small_ref, dst_ref=out_ref.at[me], send_sem=send.at[k],
                                              recv_sem=recv.at[k], device_id=(px, py, c), device_id_type=MESH_ID)
                 for k, (px, py, _) in enumerate(chips)]
        for cp in sends:
            cp.start()
        for k, (_, _, pj) in enumerate(chips):
            pltpu.make_async_remote_copy(src_ref=small_ref, dst_ref=out_ref.at[pj], send_sem=send.at[k], recv_sem=recv.at[k],
                                         device_id=(x, y, c), device_id_type=MESH_ID).wait_recv()
        for cp in sends:
            cp.wait_send()
        own.wait()

    return pl.pallas_call(
        body, name="gather_small", in_specs=[_ANY], out_specs=_ANY,
        out_shape=jax.ShapeDtypeStruct((N_CHIPS,) + small.shape, small.dtype), scratch_shapes=_sem_specs(3, 3, 1),
        compiler_params=pltpu.CompilerParams(has_side_effects=True),
    )(small)


def swap_rider(gs):
    n = len(gs)

    def copies(rin, rout, sems):
        x, y, c = _place()
        out = []
        for i, (g, got) in enumerate(zip(rin, rout)):
            rh = g.shape[1] // 2
            theirs = pl.ds(pl.multiple_of((1 - c) * rh, 8), rh)
            out.append(pltpu.make_async_remote_copy(
                src_ref=g.at[:, theirs, :], dst_ref=got, send_sem=sems[0].at[i], recv_sem=sems[1].at[i],
                device_id=(x, y, 1 - c), device_id_type=MESH_ID))
        return out

    def start(rin, rout, sems):
        for cp in copies(rin, rout, sems):
            cp.start()

    def wait(rin, rout, sems):
        for cp in copies(rin, rout, sems):
            cp.wait()

    shapes = [jax.ShapeDtypeStruct((g.shape[0], g.shape[1] // 2, g.shape[2]), g.dtype) for g in gs]
    return Rider("swap", list(gs), shapes, {}, (n, n), start, wait)


def scatter_rider(sbs):
    n = len(sbs)

    def start(rin, rout, sems):
        x, y, c = _place()
        me = 2 * x + y
        for k, (px, py, pj) in enumerate(_other_chips(x, y)):
            for i, (sb, got) in enumerate(zip(rin, rout)):
                pltpu.make_async_remote_copy(
                    src_ref=sb.at[pj], dst_ref=got.at[me], send_sem=sems[0].at[n * k + i], recv_sem=sems[1].at[n * k + i],
                    device_id=(px, py, c), device_id_type=MESH_ID).start()

    def wait(rin, rout, sems):
        x, y, c = _place()
        for k, (_, _, pj) in enumerate(_other_chips(x, y)):
            for i, (sb, got) in enumerate(zip(rin, rout)):
                cp = pltpu.make_async_remote_copy(
                    src_ref=sb.at[pj], dst_ref=got.at[pj], send_sem=sems[0].at[n * k + i], recv_sem=sems[1].at[n * k + i],
                    device_id=(x, y, c), device_id_type=MESH_ID)
                cp.wait_recv()
                cp.wait_send()

    shapes = [jax.ShapeDtypeStruct(sb.shape, sb.dtype) for sb in sbs]
    return Rider("scatter", list(sbs), shapes, {}, (3 * n, 3 * n), start, wait)


def join_rider(fs, layers):
    n = len(fs)

    def half(i, f, mine, place):
        x, y, c = place
        rh = f.shape[1] // 2
        block = 2 * x + y if layers[i] is None else layers[i]
        return f.at[block, pl.ds(pl.multiple_of((c if mine else 1 - c) * rh, 8), rh), :]

    def start(rin, rout, sems):
        x, y, c = _place()
        for i, f in enumerate(rout):
            part = half(i, f, True, (x, y, c))
            pltpu.make_async_remote_copy(
                src_ref=part, dst_ref=part, send_sem=sems[0].at[i], recv_sem=sems[1].at[i],
                device_id=(x, y, 1 - c), device_id_type=MESH_ID).start()

    def wait(rin, rout, sems):
        x, y, c = _place()
        for i, f in enumerate(rout):
            part = half(i, f, False, (x, y, c))
            pltpu.make_async_remote_copy(
                src_ref=part, dst_ref=part, send_sem=sems[0].at[i], recv_sem=sems[1].at[i],
                device_id=(x, y, 1 - c), device_id_type=MESH_ID).wait()

    shapes = [jax.ShapeDtypeStruct(f.shape, f.dtype) for f in fs]
    return Rider("join", list(fs), shapes, {i: i for i in range(n)}, (n, n), start, wait)


def add_halves(gs, gots, wire=BF16):
    m = len(gs)
    x, y, c = _place()

    def body(p_ref, *refs):
        ins, outs = refs[:2 * m], refs[2 * m:]
        for i in range(m):
            s = ins[2 * i][...] + ins[2 * i + 1][...]
            outs[2 * i][...] = s.astype(wire)

            @pl.when(pl.program_id(0) == p_ref[0])
            def _(s=s, own_ref=outs[2 * i + 1]):
                own_ref[...] = s

    in_specs, out_specs, out_shape = [], [], []
    for got in gots:
        n, rh, cols = got.shape
        blk = (None, rh, cols)
        in_specs += [pl.BlockSpec(blk, lambda j, p_ref: (j, p_ref[1], 0)), pl.BlockSpec(blk, lambda j, p_ref: (j, 0, 0))]
        out_specs += [pl.BlockSpec(blk, lambda j, p_ref: (j, 0, 0)), pl.BlockSpec((rh, cols), lambda j, p_ref: (0, 0))]
        out_shape += [jax.ShapeDtypeStruct(got.shape, wire), jax.ShapeDtypeStruct((rh, cols), F32)]
    grid_spec = pltpu.PrefetchScalarGridSpec(num_scalar_prefetch=1, grid=(N_CHIPS,), in_specs=in_specs, out_specs=out_specs)
    res = pl.pallas_call(
        body, name="add_halves", grid_spec=grid_spec, out_shape=out_shape, compiler_params=_params(("arbitrary",)),
    )(jnp.stack([2 * x + y, c]).astype(jnp.int32), *[a for pair in zip(gs, gots) for a in pair])
    return [(res[2 * i + 1], res[2 * i]) for i in range(m)]


def add_chips(owns, gots, fbufs, block=None):
    m = len(owns)
    x, y, c = _place()
    me = 2 * x + y

    def body(p_ref, *refs):
        ins, outs = refs[:5 * m], refs[5 * m:]
        for i in range(m):
            s_ref, g1_ref, g2_ref, g3_ref, _ = ins[5 * i:5 * i + 5]
            outs[i][...] = s_ref[...] + g1_ref[...].astype(F32) + g2_ref[...].astype(F32) + g3_ref[...].astype(F32)

    def other(blk, n, k):
        return pl.BlockSpec(blk, lambda i, p_ref: ((p_ref[0] + k) % n, 0, 0))

    in_specs, out_specs, args = [], [], []
    for own, got, fbuf in zip(owns, gots, fbufs):
        n, rh, cols = got.shape
        blk = (None, rh, cols)
        in_specs += [pl.BlockSpec((rh, cols), lambda i, p_ref: (0, 0)), other(blk, n, 1), other(blk, n, 2), other(blk, n, 3),
                     _ANY]
        out_specs.append(pl.BlockSpec(blk, lambda i, p_ref: (p_ref[2], p_ref[1], 0)))
        args += [own, got, got, got, fbuf]
    grid_spec = pltpu.PrefetchScalarGridSpec(num_scalar_prefetch=1, grid=(1,), in_specs=in_specs, out_specs=out_specs)
    return pl.pallas_call(
        body, name="add_chips", grid_spec=grid_spec, out_shape=[jax.ShapeDtypeStruct(f.shape, F32) for f in fbufs],
        input_output_aliases={5 * i + 5: i for i in range(m)}, compiler_params=_params(("arbitrary",)),
    )(jnp.stack([me, c, me if block is None else block]).astype(jnp.int32), *args)


def _pack(arrays, rows):
    flat = jnp.concatenate([a.reshape(-1) for a in arrays])
    return jnp.pad(flat, (0, rows * LANES - flat.size)).reshape(rows, LANES)


def _unpack(buf, shapes):
    flat = buf.reshape(-1)
    out, at = [], 0
    for s in shapes:
        n = math.prod(s)
        out.append(flat[at:at + n].reshape(s))
        at += n
    return out


CONV_SHARDS = [(DEPTH, K_A, D_A // N_CHIPS), (DEPTH, K_C, D_C // N_CHIPS), (DEPTH, K_F, 2 * D_FF // N_CHIPS)]
CONV_ROWS = 32
SMALL_ROWS = 640


def _join_cols(g):
    n, l, r, c = g.shape
    return jnp.transpose(g, (1, 2, 0, 3)).reshape(l, r, n * c)


BIG = ["w_in", "w_out", "w_up", "w_down"]
TILE_MM = 512
TILE_TN = 1024
TILE_EW = 256


def _pad_rows(a, rows):
    return jnp.pad(a, ((0, rows - a.shape[0]), (0, 0)))


def _row(a):
    return a.reshape(1, -1)


def _tile_perm(tt):
    p = lax.broadcasted_iota(jnp.int32, (tt, tt), 0)
    tok = lax.broadcasted_iota(jnp.int32, (tt, tt), 1)
    return ((tt // 8) * (p % 8) + p // 8 == tok).astype(F32)


def _layer_params(wl, tt):
    n = tt // CHUNK
    tril = jnp.tril(jnp.ones((CHUNK, CHUNK), bool))
    wm = jnp.where(tril[None], wl["sgu_w"], 0.0)
    eye = jnp.eye(n, dtype=F32)
    wt = (eye[None, :, None, :, None] * wm[:, None, :, None, :]).reshape(N_HEADS_B, tt, tt)
    bias_e = jnp.repeat(wl["sgu_b"].T, HEAD, axis=1)
    return (_pad_rows(wl["conv_a_w"], 8), _row(wl["sgu_ln_g"]), _row(wl["sgu_ln_b"]), wt.astype(MM_DTYPE),
            jnp.tile(bias_e, (n, 1)), _pad_rows(wl["conv_c_w"], 32), _row(wl["conv_c_b"]), _row(wl["conv_ln_g"]),
            _row(wl["conv_ln_b"]))


def layer_fwd(x, wl, gw, nxt=None, tm=TILE_MM, tt=TILE_EW):
    mp = _layer_params(wl, tt)
    ride = pass_rider([gw["w_down"]]) if gw.get("pass_down") else None
    (z, h), done = norm_matmul(x, _row(wl["pre_mix_g"]), gw["w_in"], tm, rider=ride)
    gw = {n: (done[0] if ride and n == "w_down" else gw[n]) for n in BIG}
    ride = spread_rider([nxt["w_in"], nxt["w_out"]]) if nxt else None
    (o, x1, cv), done = mixer_fwd(z, x, mp, _tile_perm(tt), _row(wl["grp_norm_g"]), gw["w_out"], _row(wl["post_mix_g"]), tt,
                                  rider=ride)
    ride = both_riders(spread_rider([nxt["w_up"]]), pass_rider(list(done))) if nxt else None
    (up0, h2), done = norm_matmul(x1, _row(wl["pre_ffn_g"]), gw["w_up"], tm, rider=ride)
    if nxt:
        nxt = dict(nxt, w_up=done[0], w_in=done[1], w_out=done[2])
        ride = both_riders(spread_rider([nxt["w_down"]]), pass_rider([nxt["w_up"]]))
    (d, x2), done = ffn_fwd(up0, x1, _pad_rows(wl["ffn_conv_w"], 8), gw["w_down"], _row(wl["post_ffn_g"]), tt, rider=ride)
    if nxt:
        nxt = dict(nxt, w_down=done[0], w_up=done[1], pass_down=True)
    return x2, dict(x=x, z=z, h=h, o=o, x1=x1, up0=up0, h2=h2, d=d, cv=cv, gw=gw), nxt


WIDE = ["w_up", "w_down"]
NARROW = ["w_in", "w_out"]


def layer_bwd(dx2, wl, layer, sv, pend=None, exchange=True, tm=TILE_MM, tt=TILE_EW):
    mp = _layer_params(wl, tt)
    gw = sv["gw"]
    tk = min(TILE_TN, dx2.shape[0])
    at = {n: BIG.index(n) for n in BIG}
    g = {}
    ride = scatter_rider([sw for _, sw in pend["narrow"]]) if pend else None
    (dd, act, dup0, dpg, dcw), arrived = ffn_bwd(dx2, sv["d"], sv["up0"], _pad_rows(wl["ffn_conv_w"], 8), gw["w_down"],
                                                 _row(wl["post_ffn_g"]), tt, rider=ride)
    fbuf = list(pend["fbuf"]) if pend else grad_buffers()
    if pend:
        done = add_chips([own for own, _ in pend["narrow"]], arrived, [fbuf[at[n]] for n in NARROW], pend["layer"])
        for n, f in zip(NARROW, done):
            fbuf[at[n]] = f
    g["post_ffn_g"] = dpg[0]
    g["ffn_conv_w"] = dcw[:K_F]
    gl = {}
    gl["w_down"] = matmul_tn_down(act, dd, tk)
    gl["w_up"] = matmul_tn_cols(sv["h2"], dup0, tk)
    ride = swap_rider([gl[n] for n in WIDE]) if exchange else None
    (dx1, dg), got = matmul_nt_norm_bwd(dup0, gw["w_up"], sv["x1"], _row(wl["pre_ffn_g"]), dx2, tm, rider=ride)
    g["pre_ffn_g"] = dg[0]
    wide = add_halves([gl[n] for n in WIDE], got) if exchange else None
    ride = scatter_rider([sw for _, sw in wide]) if exchange else None
    if pend:
        ride = both_riders(join_rider(fbuf, [pend["layer"]] * len(fbuf)), ride)
    (dz, do, yn, dpg, dgg, dcaw, dlng, dlnb, dwm, dbias, dccw, dccb, dclg, dclb), rode = mixer_bwd(
        dx1, sv["o"], sv["z"], sv["cv"], mp, _tile_perm(tt), _row(wl["grp_norm_g"]), gw["w_out"],
        _row(wl["post_mix_g"]), tt, rider=ride)
    if exchange:
        fbuf, arrived = (list(rode[:len(BIG)]), rode[len(BIG):]) if pend else (fbuf, rode)
        done = add_chips([own for own, _ in wide], arrived, [fbuf[at[n]] for n in WIDE], layer)
        for n, f in zip(WIDE, done):
            fbuf[at[n]] = f
    g["post_mix_g"] = dpg[0]
    g["grp_norm_g"] = dgg[0]
    g["conv_a_w"] = dcaw[:K_A]
    g["sgu_ln_g"] = dlng[0]
    g["sgu_ln_b"] = dlnb[0]
    g["sgu_w"] = dwm
    g["sgu_b"] = dbias[:, :N_HEADS_B].T
    g["conv_c_w"] = dccw[:K_C]
    g["conv_c_b"] = dccb[0]
    g["conv_ln_g"] = dclg[0]
    g["conv_ln_b"] = dclb[0]
    gl["w_out"] = matmul_tn_rows(yn, do, tk)
    gl["w_in"] = matmul_tn_in(sv["h"], dz, tk)
    ride = swap_rider([gl[n] for n in NARROW]) if exchange else None
    (dx, dg), got = matmul_nt_norm_bwd(dz, gw["w_in"], sv["x"], _row(wl["pre_mix_g"]), dx1, tm, rider=ride)
    g["pre_mix_g"] = dg[0]
    if not exchange:
        return dx, g, gl
    narrow = add_halves([gl[n] for n in NARROW], got)
    return dx, g, dict(narrow=narrow, fbuf=fbuf, layer=layer)


def grad_buffers():
    return [lax.empty(s, F32) for s in ((DEPTH, D_MODEL, D_IN // N_CHIPS), (DEPTH, D_MODEL // N_CHIPS, D_MODEL),
                                        (DEPTH, D_MODEL, 2 * D_FF // N_CHIPS), (DEPTH, D_FF // N_CHIPS, D_MODEL))]


CONV = ["conv_a_w", "conv_c_w", "ffn_conv_w"]
REPL = ["pre_mix_g", "sgu_ln_g", "sgu_ln_b", "sgu_w", "sgu_b", "conv_c_b", "conv_ln_g", "conv_ln_b", "grp_norm_g",
        "post_mix_g", "pre_ffn_g", "post_ffn_g"]
WEIGHTS = ["pre_mix_g", "w_in", "conv_a_w", "sgu_ln_g", "sgu_ln_b", "sgu_w", "sgu_b", "conv_c_w", "conv_c_b", "conv_ln_g",
           "conv_ln_b", "grp_norm_g", "w_out", "post_mix_g", "pre_ffn_g", "w_up", "ffn_conv_w", "w_down", "post_ffn_g"]


def kernel(x, pre_mix_g, w_in, conv_a_w, sgu_ln_g, sgu_ln_b, sgu_w, sgu_b, conv_c_w, conv_c_b, conv_ln_g, conv_ln_b, grp_norm_g, w_out, post_mix_g, pre_ffn_g, w_up, ffn_conv_w, w_down, post_ffn_g, loss_target, m_pre_mix_g, m_w_in, m_conv_a_w, m_sgu_ln_g, m_sgu_ln_b, m_sgu_w, m_sgu_b, m_conv_c_w, m_conv_c_b, m_conv_ln_g, m_conv_ln_b, m_grp_norm_g, m_w_out, m_post_mix_g, m_pre_ffn_g, m_w_up, m_ffn_conv_w, m_w_down, m_post_ffn_g, v_pre_mix_g, v_w_in, v_conv_a_w, v_sgu_ln_g, v_sgu_ln_b, v_sgu_w, v_sgu_b, v_conv_c_w, v_conv_c_b, v_conv_ln_g, v_conv_ln_b, v_grp_norm_g, v_w_out, v_post_mix_g, v_pre_ffn_g, v_w_up, v_ffn_conv_w, v_w_down, v_post_ffn_g):
    w = dict(pre_mix_g=pre_mix_g, w_in=w_in, conv_a_w=conv_a_w, sgu_ln_g=sgu_ln_g, sgu_ln_b=sgu_ln_b, sgu_w=sgu_w, sgu_b=sgu_b,
             conv_c_w=conv_c_w, conv_c_b=conv_c_b, conv_ln_g=conv_ln_g, conv_ln_b=conv_ln_b, grp_norm_g=grp_norm_g,
             w_out=w_out, post_mix_g=post_mix_g, pre_ffn_g=pre_ffn_g, w_up=w_up, ffn_conv_w=ffn_conv_w, w_down=w_down,
             post_ffn_g=post_ffn_g)
    m = dict(pre_mix_g=m_pre_mix_g, w_in=m_w_in, conv_a_w=m_conv_a_w, sgu_ln_g=m_sgu_ln_g, sgu_ln_b=m_sgu_ln_b,
             sgu_w=m_sgu_w, sgu_b=m_sgu_b, conv_c_w=m_conv_c_w, conv_c_b=m_conv_c_b, conv_ln_g=m_conv_ln_g,
             conv_ln_b=m_conv_ln_b, grp_norm_g=m_grp_norm_g, w_out=m_w_out, post_mix_g=m_post_mix_g,
             pre_ffn_g=m_pre_ffn_g, w_up=m_w_up, ffn_conv_w=m_ffn_conv_w, w_down=m_w_down, post_ffn_g=m_post_ffn_g)
    v = dict(pre_mix_g=v_pre_mix_g, w_in=v_w_in, conv_a_w=v_conv_a_w, sgu_ln_g=v_sgu_ln_g, sgu_ln_b=v_sgu_ln_b,
             sgu_w=v_sgu_w, sgu_b=v_sgu_b, conv_c_w=v_conv_c_w, conv_c_b=v_conv_c_b, conv_ln_g=v_conv_ln_g,
             conv_ln_b=v_conv_ln_b, grp_norm_g=v_grp_norm_g, w_out=v_w_out, post_mix_g=v_post_mix_g,
             pre_ffn_g=v_pre_ffn_g, w_up=v_w_up, ffn_conv_w=v_ffn_conv_w, w_down=v_w_down, post_ffn_g=v_post_ffn_g)
    chip = 2 * lax.axis_index("x") + lax.axis_index("y")

    convs = gather_small(_pack([w[n] for n in CONV], CONV_ROWS))
    gws = [{n: cast_shard(w[n], layer, chip) for n in BIG} for layer in range(DEPTH)]
    first = run_rider(pass_rider(run_rider(spread_rider([gws[0][n] for n in BIG]))))
    gws[0] = dict(zip(BIG, first))
    cparts = [_unpack(convs[j], CONV_SHARDS) for j in range(N_CHIPS)]
    full = dict(w)
    for i, n in enumerate(CONV):
        full[n] = _join_cols(jnp.stack([p[i] for p in cparts]))

    xc = to_tiles(x[0], TILE_EW)
    saved = []
    for layer in range(DEPTH):
        nxt = gws[layer + 1] if layer + 1 < DEPTH else None
        xc, sv, nxt = layer_fwd(xc, {n: full[n][layer] for n in REPL + CONV}, gws[layer], nxt)
        if nxt:
            gws[layer + 1] = nxt
        saved.append(sv)
    dxc, loss_part = loss_head(xc, to_tiles(loss_target[0], TILE_EW), TILE_MM)
    loss = lax.psum(loss_part[0, 0], ("x", "y", "c"))
    small = [None] * DEPTH
    pend = None
    for layer in reversed(range(DEPTH)):
        dxc, small[layer], pend = layer_bwd(dxc, {n: full[n][layer] for n in REPL + CONV}, layer, saved[layer], pend)
    grads = {n: jnp.stack([small[layer][n] for layer in range(DEPTH)]) for n in REPL + CONV}

    gsmall = _pack([grads[n] for n in REPL + CONV], SMALL_ROWS).reshape(N_CHIPS, SMALL_ROWS // N_CHIPS, LANES)
    sums = pend["narrow"] + add_halves([gsmall], run_rider(swap_rider([gsmall])), wire=F32)
    arrived = run_rider(scatter_rider([sw for _, sw in sums]))
    fbuf = list(pend["fbuf"])
    at = [BIG.index(n) for n in NARROW]
    for i, f in zip(at, add_chips([own for own, _ in sums[:2]], arrived[:2], [fbuf[i] for i in at], 0)):
        fbuf[i] = f
    fbuf += add_chips([sums[2][0]], arrived[2:], [lax.empty(gsmall.shape, F32)])
    joined = run_rider(join_rider(fbuf, [0] * len(BIG) + [None]))
    out_g = dict(zip(BIG, joined))
    tot = run_rider(pass_rider(run_rider(spread_rider([joined[len(BIG)]]))))[0].reshape(SMALL_ROWS, LANES)
    shapes = [grads[n].shape for n in REPL + CONV]
    for n, gfull in zip(REPL + CONV, _unpack(tot, shapes)):
        if n in CONV:
            width = gfull.shape[-1] // N_CHIPS
            gfull = lax.dynamic_slice_in_dim(gfull, chip * width, width, axis=2)
        out_g[n] = gfull

    deltas, new_m, new_v = {}, {}, {}
    for n in WEIGHTS:
        deltas[n], new_m[n], new_v[n] = adamw(w[n], out_g[n], m[n], v[n])
    return (loss, from_tiles(dxc, TILE_EW)[None], *[out_g[n] for n in WEIGHTS], *[deltas[n] for n in WEIGHTS], *[new_m[n] for n in WEIGHTS],
            *[new_v[n] for n in WEIGHTS])
```

```python
import math
from typing import Callable, NamedTuple

import jax
import jax.numpy as jnp
from jax import lax
from jax.experimental import pallas as pl
from jax.experimental.pallas import tpu as pltpu

F32 = jnp.float32
BF16 = jnp.bfloat16
MM_DTYPE = BF16

D_MODEL = 1024
DEPTH = 4
D_A = 256
D_B = 384
D_C = 384
D_IN = 3 * D_A + 2 * D_B + 2 * D_C
D_FF = 2816
K_A = 3
K_C = 31
K_F = 3
CHUNK = 128
HEAD = 64
N_HEADS_B = D_B // HEAD
EPS = 1e-6
N_CHIPS = 4

ADAM_LR = 0.001
ADAM_B1 = 0.9
ADAM_B2 = 0.999
ADAM_EPS = 1e-08
ADAM_WD = 0.01
ADAM_STEP = 10

LANES = 1024
VMEM_LIMIT = 56 * 1024 * 1024

MESH_ID = pl.DeviceIdType.MESH
_ANY = pl.BlockSpec(memory_space=pl.ANY)


def _params(sem=None):
    return pltpu.CompilerParams(dimension_semantics=sem, vmem_limit_bytes=VMEM_LIMIT)


def _const_spec(shape):
    nd = len(shape)
    return pl.BlockSpec(shape, lambda *_: (0,) * nd, pipeline_mode=pl.Buffered(1))


def _rowsum8(a):
    r, c = a.shape
    return jnp.sum(a.reshape(r // 8, 8, c), axis=0)


def _rstd(x):
    return lax.rsqrt(jnp.mean(x * x, axis=-1, keepdims=True) + EPS)


def _rms_bwd(x, r, g, dy):
    gdy = g * dy
    return r * gdy - x * (r * r * r) * jnp.mean(gdy * x, axis=-1, keepdims=True)


def _ln_fwd(x):
    mu = jnp.mean(x, axis=-1, keepdims=True)
    xc = x - mu
    r = lax.rsqrt(jnp.mean(xc * xc, axis=-1, keepdims=True) + EPS)
    return xc * r, r


def _ln_bwd(xh, r, dxh):
    return r * (dxh - jnp.mean(dxh, axis=-1, keepdims=True) - xh * jnp.mean(dxh * xh, axis=-1, keepdims=True))


def _gelu(x):
    return 0.5 * x * (1.0 + lax.erf(x * (1.0 / math.sqrt(2.0))))


def _gelu_grad(x):
    cdf = 0.5 * (1.0 + lax.erf(x * (1.0 / math.sqrt(2.0))))
    pdf = jnp.exp(-0.5 * x * x) * (1.0 / math.sqrt(2.0 * math.pi))
    return cdf + x * pdf


def _dot(a, b):
    return jnp.dot(a, b, preferred_element_type=F32)


def _dot_nt(a, b):
    return lax.dot_general(a, b, (((1,), (1,)), ((), ())), preferred_element_type=F32)


def _dot_tn(a, b):
    return lax.dot_general(a, b, (((0,), (0,)), ((), ())), preferred_element_type=F32)


def _col_chunk(n):
    for c in (1408, 1024, 768, 512, 256, 128):
        if n % c == 0:
            return c
    raise ValueError(n)


class Rider(NamedTuple):
    name: str
    inputs: list
    out_shapes: list
    aliases: dict
    sems: tuple
    start: Callable
    wait: Callable


def _pallas(body, rider, *, name, steps, in_specs, out_specs, out_shape, scratch_shapes, args):
    if rider is None:
        res = pl.pallas_call(body, name=name, grid=(steps,), in_specs=in_specs, out_specs=out_specs, out_shape=out_shape,
                             scratch_shapes=scratch_shapes, compiler_params=_params(("arbitrary",)))(*args)
        return res, []
    n_in, n_out, n_scr = len(in_specs), len(out_specs), len(scratch_shapes)
    r_in, r_out = len(rider.inputs), len(rider.out_shapes)

    def wrapped(*refs):
        ins, rin = refs[:n_in], refs[n_in:n_in + r_in]
        at = n_in + r_in
        outs, rout = refs[at:at + n_out], refs[at + n_out:at + n_out + r_out]
        at += n_out + r_out
        scr, rsem = refs[at:at + n_scr], refs[at + n_scr:]

        @pl.when(pl.program_id(0) == 0)
        def _():
            rider.start(rin, rout, rsem)

        body(*ins, *outs, *scr)

        @pl.when(pl.program_id(0) == steps - 1)
        def _():
            rider.wait(rin, rout, rsem)

    res = pl.pallas_call(
        wrapped, name=name + "_" + rider.name, grid=(steps,), in_specs=list(in_specs) + [_ANY] * r_in,
        out_specs=list(out_specs) + [_ANY] * r_out, out_shape=list(out_shape) + list(rider.out_shapes),
        scratch_shapes=list(scratch_shapes) + [pltpu.SemaphoreType.DMA((n,)) for n in rider.sems],
        input_output_aliases={n_in + i: n_out + o for i, o in rider.aliases.items()},
        compiler_params=pltpu.CompilerParams(dimension_semantics=("arbitrary",), vmem_limit_bytes=VMEM_LIMIT,
                                             has_side_effects=True),
    )(*args, *rider.inputs)
    return res[:n_out], res[n_out:]


def run_rider(rider):
    def body(*refs):
        r_in, r_out = len(rider.inputs), len(rider.out_shapes)
        rin, rout, rsem = refs[:r_in], refs[r_in:r_in + r_out], refs[r_in + r_out:]
        rider.start(rin, rout, rsem)
        rider.wait(rin, rout, rsem)

    return pl.pallas_call(
        body, name=rider.name, in_specs=[_ANY] * len(rider.inputs), out_specs=[_ANY] * len(rider.out_shapes),
        out_shape=list(rider.out_shapes), scratch_shapes=[pltpu.SemaphoreType.DMA((n,)) for n in rider.sems],
        input_output_aliases=dict(rider.aliases), compiler_params=pltpu.CompilerParams(has_side_effects=True),
    )(*rider.inputs)


def _weight_spec(wg):
    return _const_spec(wg.shape)


def _join_col_blocks(w_ref, w_scr):
    c = w_ref.shape[2]
    for j in range(N_CHIPS):
        w_scr[:, c * j:c * (j + 1)] = w_ref[j]


def norm_matmul(x, g, wg, tm, rider=None):
    t, d = x.shape
    assert t % tm == 0, (t, tm)
    cw = wg.shape[2]
    n = N_CHIPS * cw
    aligned = cw % 128 == 0
    cn = cw if aligned else _col_chunk(n)

    def body(x_ref, g_ref, w_ref, o_ref, h_ref, *scr):
        if not aligned:
            @pl.when(pl.program_id(0) == 0)
            def _():
                _join_col_blocks(w_ref, scr[0])

        xv = x_ref[...]
        h = (xv * _rstd(xv) * g_ref[...]).astype(MM_DTYPE)
        h_ref[...] = h
        for j, c0 in enumerate(range(0, n, cn)):
            wv = w_ref[j] if aligned else scr[0][:, c0:c0 + cn]
            o_ref[:, c0:c0 + cn] = _dot(h, wv)

    return _pallas(
        body, rider, name="norm_matmul", steps=t // tm,
        in_specs=[pl.BlockSpec((tm, d), lambda i: (i, 0)), _const_spec((1, d)), _weight_spec(wg)],
        out_specs=[pl.BlockSpec((tm, n), lambda i: (i, 0)), pl.BlockSpec((tm, d), lambda i: (i, 0))],
        out_shape=[jax.ShapeDtypeStruct((t, n), F32), jax.ShapeDtypeStruct((t, d), MM_DTYPE)],
        scratch_shapes=[] if aligned else [pltpu.VMEM((d, n), MM_DTYPE)],
        args=(x, g, wg))


def matmul_nt_norm_bwd(gy, wg, x, g, dres, tm, rider=None):
    t, n = gy.shape
    assert t % tm == 0, (t, tm)
    d, cw = wg.shape[1], wg.shape[2]
    aligned = cw % 128 == 0
    cn = cw if aligned else _col_chunk(n)
    steps = t // tm

    def body(gy_ref, w_ref, x_ref, g_ref, dres_ref, dx_ref, dg_ref, acc_ref, *scr):
        i = pl.program_id(0)

        @pl.when(i == 0)
        def _():
            acc_ref[...] = jnp.zeros_like(acc_ref)
            if not aligned:
                _join_col_blocks(w_ref, scr[0])

        dh = jnp.zeros((tm, d), F32)
        for j, c0 in enumerate(range(0, n, cn)):
            wv = w_ref[j] if aligned else scr[0][:, c0:c0 + cn]
            dh = dh + _dot_nt(gy_ref[:, c0:c0 + cn], wv)
        xv = x_ref[...]
        r = _rstd(xv)
        gv = g_ref[...]
        dx_ref[...] = dres_ref[...] + _rms_bwd(xv, r, gv, dh)
        acc_ref[...] += _rowsum8(dh * xv * r)

        @pl.when(i == steps - 1)
        def _():
            dg_ref[...] = jnp.sum(acc_ref[...], axis=0, keepdims=True)

    return _pallas(
        body, rider, name="matmul_nt_norm_bwd", steps=steps,
        in_specs=[pl.BlockSpec((tm, n), lambda i: (i, 0)), _weight_spec(wg), pl.BlockSpec((tm, d), lambda i: (i, 0)),
                  _const_spec((1, d)), pl.BlockSpec((tm, d), lambda i: (i, 0))],
        out_specs=[pl.BlockSpec((tm, d), lambda i: (i, 0)), pl.BlockSpec((1, d), lambda i: (0, 0))],
        out_shape=[jax.ShapeDtypeStruct((t, d), F32), jax.ShapeDtypeStruct((1, d), F32)],
        scratch_shapes=[pltpu.VMEM((8, d), F32)] + ([] if aligned else [pltpu.VMEM((d, n), MM_DTYPE)]),
        args=(gy, wg, x, g, dres))


def matmul_tn_cols(a, b, tk):
    t, r = a.shape
    c = b.shape[1] // N_CHIPS
    assert t % tk == 0 and r % 8 == 0 and c % 128 == 0, (a.shape, b.shape, tk)

    def body(a_ref, b_ref, o_ref):
        @pl.when(pl.program_id(1) == 0)
        def _():
            o_ref[...] = jnp.zeros_like(o_ref)

        o_ref[...] += _dot_tn(a_ref[...], b_ref[...])

    a_spec = pl.BlockSpec((tk, r), lambda j, k: (k, 0))
    b_spec = pl.BlockSpec((tk, c), lambda j, k: (k, j))
    return pl.pallas_call(
        body, name="matmul_tn_cols", grid=(N_CHIPS, t // tk), in_specs=[a_spec, b_spec],
        out_specs=pl.BlockSpec((None, r, c), lambda j, k: (j, 0, 0)),
        out_shape=jax.ShapeDtypeStruct((N_CHIPS, r, c), F32),
        compiler_params=_params(("arbitrary", "arbitrary")),
    )(a, b)


def matmul_tn_down(act, dd, tk):
    t, m = act.shape
    c = dd.shape[1]
    r = m // N_CHIPS
    assert t % tk == 0, (t, tk)
    steps = t // tk

    def body(a_ref, b_ref, o_ref, acc):
        k = pl.program_id(1)

        @pl.when(k == 0)
        def _():
            acc[...] = jnp.zeros_like(acc)

        acc[...] += _dot_tn(a_ref[...], b_ref[...])

        @pl.when(k == steps - 1)
        def _():
            o_ref[0] = acc[0:r, :]
            o_ref[1] = acc[r:2 * r, :]

    return pl.pallas_call(
        body, name="matmul_tn_down", grid=(2, steps),
        in_specs=[pl.BlockSpec((tk, 2 * r), lambda p, k: (k, p)), pl.BlockSpec((tk, c), lambda p, k: (k, 0))],
        out_specs=pl.BlockSpec((2, r, c), lambda p, k: (p, 0, 0)),
        out_shape=jax.ShapeDtypeStruct((N_CHIPS, r, c), F32),
        scratch_shapes=[pltpu.VMEM((2 * r, c), F32)],
        compiler_params=_params(("arbitrary", "arbitrary")),
    )(act, dd)


def matmul_tn_rows(a, b, tk):
    t, m = a.shape
    n = b.shape[1]
    r = m // N_CHIPS
    assert t % tk == 0 and r % 8 == 0, (a.shape, b.shape, tk)
    steps = t // tk

    def body(a_ref, b_ref, o_ref, acc):
        k = pl.program_id(0)

        @pl.when(k == 0)
        def _():
            acc[...] = jnp.zeros_like(acc)

        acc[...] += _dot_tn(a_ref[...], b_ref[...])

        @pl.when(k == steps - 1)
        def _():
            for j in range(N_CHIPS):
                o_ref[j] = acc[r * j:r * (j + 1), :]

    return pl.pallas_call(
        body, name="matmul_tn_rows", grid=(steps,),
        in_specs=[pl.BlockSpec((tk, m), lambda k: (k, 0)), pl.BlockSpec((tk, n), lambda k: (k, 0))],
        out_specs=pl.BlockSpec((N_CHIPS, r, n), lambda k: (0, 0, 0)),
        out_shape=jax.ShapeDtypeStruct((N_CHIPS, r, n), F32),
        scratch_shapes=[pltpu.VMEM((m, n), F32)],
        compiler_params=_params(("arbitrary",)),
    )(a, b)


def matmul_tn_in(h, dz, tk):
    t, d = h.shape
    n = dz.shape[1]
    c = n // N_CHIPS
    assert t % tk == 0, (t, tk)
    steps = t // tk

    def body(a_ref, b_ref, o_ref, acc):
        k = pl.program_id(0)

        @pl.when(k == 0)
        def _():
            acc[...] = jnp.zeros_like(acc)

        acc[...] += _dot_tn(a_ref[...], b_ref[...])

        @pl.when(k == steps - 1)
        def _():
            for j in range(N_CHIPS):
                o_ref[j] = acc[:, c * j:c * (j + 1)]

    return pl.pallas_call(
        body, name="matmul_tn_in", grid=(steps,),
        in_specs=[pl.BlockSpec((tk, d), lambda k: (k, 0)), pl.BlockSpec((tk, n), lambda k: (k, 0))],
        out_specs=pl.BlockSpec((N_CHIPS, d, c), lambda k: (0, 0, 0)),
        out_shape=jax.ShapeDtypeStruct((N_CHIPS, d, c), F32),
        scratch_shapes=[pltpu.VMEM((d, n), F32)],
        compiler_params=_params(("arbitrary",)),
    )(h, dz)


def to_tiles(a, tt):
    t = a.shape[0]
    return a.reshape((t // tt, 8, tt // 8) + a.shape[1:]).swapaxes(1, 2).reshape(a.shape)


def from_tiles(a, tt):
    t = a.shape[0]
    return a.reshape((t // tt, tt // 8, 8) + a.shape[1:]).swapaxes(1, 2).reshape(a.shape)


def _roll_sublanes(a, shift):
    n = a.shape[0] // 8
    return pltpu.roll(a.reshape(n, 8, a.shape[1]), shift, 1).reshape(a.shape)


def _halo_before(cur_last, prev_last):
    sub = lax.broadcasted_iota(jnp.int32, cur_last.shape, 0) % 8
    return jnp.where(sub == 0, _roll_sublanes(prev_last, 1), _roll_sublanes(cur_last, 1))


def _halo_after(cur_first, next_first):
    sub = lax.broadcasted_iota(jnp.int32, cur_first.shape, 0) % 8
    return jnp.where(sub == 7, _roll_sublanes(next_first, 7), _roll_sublanes(cur_first, 7))


def _conv_causal(ext, cur, prev_last, w, taps, tt, cols=None):
    hr = 8 * (taps - 1)
    cs = slice(None) if cols is None else cols
    ext[hr:hr + tt, cs] = cur
    ext[0:hr, cs] = _halo_before(cur[tt - hr:, :], prev_last)
    acc = w[0:1, :] * ext[0:tt, cs]
    for k in range(1, taps):
        acc = acc + w[k:k + 1, :] * ext[8 * k:8 * k + tt, cs]
    return acc


def _conv_anticausal(ext, cur, next_first, w, taps, tt, x=None, acc_w=None, cols=None):
    hr = 8 * (taps - 1)
    cs = slice(None) if cols is None else cols
    ext[0:tt, cs] = cur
    ext[tt:tt + hr, cs] = _halo_after(cur[0:hr, :], next_first)
    acc = None
    for k in range(taps):
        off = 8 * (taps - 1 - k)
        ld = ext[off:off + tt, cs]
        term = w[k:k + 1, :] * ld
        acc = term if acc is None else acc + term
        if x is not None:
            acc_w[k, :, cs] += _rowsum8(ld * x)
    return acc


def _dot_exact(a, b, dims):
    return lax.dot_general(a, b, (dims, ((), ())), precision=lax.Precision.HIGHEST, preferred_element_type=F32)


def _to_tile_order(perm, wt_ref, w_scr, transpose):
    pb = perm.astype(MM_DTYPE)
    for h in range(N_HEADS_B):
        half = (_dot_nt(pb, wt_ref[h]) if transpose else _dot(pb, wt_ref[h])).astype(MM_DTYPE)
        w_scr[h] = _dot_nt(half, pb).astype(MM_DTYPE)


def _project_rows(y, w_ref):
    r = w_ref.shape[1]
    acc = _dot(y[:, 0:r], w_ref[0])
    for j in range(1, N_CHIPS):
        acc = acc + _dot(y[:, r * j:r * (j + 1)], w_ref[j])
    return acc


PAIR = 2 * HEAD


def _pair_lanes(h):
    return slice(PAIR * (h // 2), PAIR * (h // 2 + 1))


def _per_head(fn):
    cols = []
    for p in range(N_HEADS_B // 2):
        lo, hi = fn(2 * p), fn(2 * p + 1)
        cols.append(jnp.where(lax.broadcasted_iota(jnp.int32, lo.shape, 1) < HEAD, lo, hi))
    return jnp.concatenate(cols, axis=1)


def _mixer_forward(z, prm, q, yc):
    _, lng, lnb, wm, bias_p, _, _, clg, clb = prm
    bg = z[:, 0:D_A]
    ya = bg * q
    o_b = 3 * D_A
    zu = z[:, o_b:o_b + D_B]
    zv = z[:, o_b + D_B:o_b + 2 * D_B]
    u = _gelu(zu)
    vh, rv = _ln_fwd(_gelu(zv))
    vnb = (vh * lng + lnb).astype(MM_DTYPE)
    s = _per_head(lambda h: _dot(wm[h], vnb[:, _pair_lanes(h)])) + bias_p
    yb = u * s
    yh, rc = _ln_fwd(yc)
    l = yh * clg + clb
    sl = jax.nn.sigmoid(l)
    return dict(bg=bg, q=q, ya=ya, zu=zu, zv=zv, u=u, vh=vh, rv=rv, vnb=vnb, s=s, yb=yb, yh=yh, rc=rc, l=l, sl=sl,
                yo=l * sl)


def _conv_inputs(z):
    o_c = 3 * D_A + 2 * D_B
    a = z[:, o_c:o_c + D_C]
    sg = jax.nn.sigmoid(z[:, o_c + D_C:o_c + 2 * D_C])
    return z[:, D_A:2 * D_A] * z[:, 2 * D_A:3 * D_A], a * sg, a, sg


def _group_norm(f, gg):
    ya, yb, yo = f["ya"], f["yb"], f["yo"]
    ra, rb, ro = _rstd(ya), _rstd(yb), _rstd(yo)
    yn = jnp.concatenate([ya * ra * gg[:, 0:D_A], yb * rb * gg[:, D_A:D_A + D_B], yo * ro * gg[:, D_A + D_B:]], axis=1)
    return yn, (ra, rb, ro)


def _mixer_prm(refs, wp_scr, bias_scr):
    caw_ref, lng_ref, lnb_ref, _, _, ccw_ref, ccb_ref, clg_ref, clb_ref = refs
    wm = [wp_scr[h] for h in range(N_HEADS_B)]
    return (caw_ref[...], lng_ref[...], lnb_ref[...], wm, bias_scr[...], ccw_ref[...], ccb_ref[...], clg_ref[...],
            clb_ref[...])


def _mixer_param_specs(tt):
    return [_const_spec((8, D_A)), _const_spec((1, D_B)), _const_spec((1, D_B)), _const_spec((N_HEADS_B, tt, tt)),
            _const_spec((tt, D_B)), _const_spec((32, D_C)), _const_spec((1, D_C)), _const_spec((1, D_C)),
            _const_spec((1, D_C))]


HR_A = 8 * (K_A - 1)
HR_C = 8 * (K_C - 1)
HR_F = 8 * (K_F - 1)


def mixer_fwd(z, x, mp, perm, grp_g, wog, post_g, tt, rider=None):
    t = z.shape[0]
    assert t % tt == 0 and tt % CHUNK == 0 and tt >= HR_C, (t, tt)

    def body(z_ref, x_ref, *rest):
        prm_refs = rest[:9]
        (perm_ref, gg_ref, wo_ref, pg_ref, o_ref, x1_ref, cv_ref, pa_ext, yg_ext, pa_last, yg_last, wp_scr, bias_scr) = rest[9:]
        i = pl.program_id(0)

        @pl.when(i == 0)
        def _():
            pa_last[...] = jnp.zeros_like(pa_last)
            yg_last[...] = jnp.zeros_like(yg_last)
            _to_tile_order(perm_ref[...], prm_refs[3], wp_scr, False)
            bias_scr[...] = _dot_exact(perm_ref[...], prm_refs[4][...], ((1,), (0,)))

        zv = z_ref[...]
        prm = _mixer_prm(prm_refs, wp_scr, bias_scr)
        pa, yg, _, _ = _conv_inputs(zv)
        q = _conv_causal(pa_ext, pa, pa_last[...], prm[0], K_A, tt)
        yc = _conv_causal(yg_ext, yg, yg_last[...], prm[5], K_C, tt) + prm[6]
        pa_last[...] = pa[tt - HR_A:, :]
        yg_last[...] = yg[tt - HR_C:, :]
        cv_ref[:, 0:D_A] = q
        cv_ref[:, D_A:] = yc
        f = _mixer_forward(zv, prm, q, yc)
        yn, _ = _group_norm(f, gg_ref[...])
        o = _project_rows(yn.astype(MM_DTYPE), wo_ref)
        o_ref[...] = o
        x1_ref[...] = x_ref[...] + o * _rstd(o) * pg_ref[...]

    row = lambda c: pl.BlockSpec((tt, c), lambda i: (i, 0))
    return _pallas(
        body, rider, name="mixer_fwd", steps=t // tt,
        in_specs=[row(D_IN), row(D_MODEL)] + _mixer_param_specs(tt)
        + [_const_spec((tt, tt)), _const_spec((1, D_MODEL)), _weight_spec(wog), _const_spec((1, D_MODEL))],
        out_specs=[row(D_MODEL), row(D_MODEL), row(D_A + D_C)],
        out_shape=[jax.ShapeDtypeStruct((t, D_MODEL), F32), jax.ShapeDtypeStruct((t, D_MODEL), F32),
                   jax.ShapeDtypeStruct((t, D_A + D_C), F32)],
        scratch_shapes=[pltpu.VMEM((HR_A + tt, D_A), F32), pltpu.VMEM((HR_C + tt, D_C), F32),
                        pltpu.VMEM((HR_A, D_A), F32), pltpu.VMEM((HR_C, D_C), F32),
                        pltpu.VMEM((N_HEADS_B, tt, tt), MM_DTYPE), pltpu.VMEM((tt, D_B), F32)],
        args=(z, x, *mp, perm, grp_g, wog, post_g))


def mixer_bwd(dx1, o, z, cv, mp, perm, grp_g, wog, post_g, tt, rider=None):
    t = z.shape[0]
    assert t % tt == 0 and tt % CHUNK == 0 and tt >= HR_C, (t, tt)
    steps = t // tt

    def body(dx1_ref, o_ref, z_ref, cv_ref, *rest):
        prm_refs = rest[:9]
        (perm_ref, gg_ref, wo_ref, pg_ref,
         dz_ref, do_ref, yn_ref, dpg_ref, dgg_ref, dcaw_ref, dlng_ref, dlnb_ref, dwm_ref, dbias_ref, dccw_ref, dccb_ref,
         dclg_ref, dclb_ref,
         dq_ext, dyc_ext, dq_first, dyc_first, a_pg, a_gg, a_caw, a_lng, a_lnb, a_ccw, a_ccb, a_clg, a_clb,
         wp_scr, wpt_scr, bias_scr, a_wm, a_bias) = rest[9:]
        i = pl.program_id(0)
        small = (a_pg, a_gg, a_caw, a_lng, a_lnb, a_ccw, a_ccb, a_clg, a_clb)

        @pl.when(i == 0)
        def _():
            for ref in small + (a_wm, a_bias, dq_first, dyc_first):
                ref[...] = jnp.zeros_like(ref)
            _to_tile_order(perm_ref[...], prm_refs[3], wp_scr, False)
            _to_tile_order(perm_ref[...], prm_refs[3], wpt_scr, True)
            bias_scr[...] = _dot_exact(perm_ref[...], prm_refs[4][...], ((1,), (0,)))

        prm = _mixer_prm(prm_refs, wp_scr, bias_scr)
        caw, lng, lnb, wm, bias_p, ccw, ccb, clg, clb = prm

        zv = z_ref[...]
        pa, yg, a, sg = _conv_inputs(zv)
        f = _mixer_forward(zv, prm, cv_ref[:, 0:D_A], cv_ref[:, D_A:])
        gg = gg_ref[...]
        yn, (ra, rb, ro) = _group_norm(f, gg)
        yn_ref[...] = yn.astype(MM_DTYPE)

        ov = o_ref[...]
        dx1v = dx1_ref[...]
        r_o = _rstd(ov)
        pg = pg_ref[...]
        a_pg[...] += _rowsum8(dx1v * ov * r_o)
        do = _rms_bwd(ov, r_o, pg, dx1v).astype(MM_DTYPE)
        do_ref[...] = do
        dyn = jnp.concatenate([_dot_nt(do, wo_ref[j]) for j in range(N_CHIPS)], axis=1)

        dyn_a, dyn_b, dyn_c = dyn[:, 0:D_A], dyn[:, D_A:D_A + D_B], dyn[:, D_A + D_B:]
        ga, gb, gc = gg[:, 0:D_A], gg[:, D_A:D_A + D_B], gg[:, D_A + D_B:]
        a_gg[...] += _rowsum8(jnp.concatenate([dyn_a * f["ya"] * ra, dyn_b * f["yb"] * rb, dyn_c * f["yo"] * ro], axis=1))
        dya = _rms_bwd(f["ya"], ra, ga, dyn_a)
        dyb = _rms_bwd(f["yb"], rb, gb, dyn_b)
        dyo = _rms_bwd(f["yo"], ro, gc, dyn_c)

        dbg = dya * f["q"]
        dq = dya * f["bg"]
        dp = _conv_anticausal(dq_ext, dq, dq_first[...], caw, K_A, tt, x=pa, acc_w=a_caw)
        dq_first[...] = dq[0:HR_A, :]
        dcg = dp * zv[:, 2 * D_A:3 * D_A]
        dxa = dp * zv[:, D_A:2 * D_A]

        du = dyb * f["s"]
        ds = dyb * f["u"]
        dsb = ds.astype(MM_DTYPE)
        a_bias[...] += ds
        odd = lax.broadcasted_iota(jnp.int32, (tt, PAIR), 1) // HEAD
        for h in range(N_HEADS_B):
            dsp = dsb[:, _pair_lanes(h)]
            a_wm[h] += _dot_nt(jnp.where(odd == h % 2, dsp, jnp.zeros_like(dsp)), f["vnb"][:, _pair_lanes(h)])
        dvn = _per_head(lambda h: _dot(wpt_scr[h], dsb[:, _pair_lanes(h)]))
        a_lng[...] += _rowsum8(dvn * f["vh"])
        a_lnb[...] += _rowsum8(dvn)
        dv = _ln_bwd(f["vh"], f["rv"], dvn * lng)
        dzu = du * _gelu_grad(f["zu"])
        dzv = dv * _gelu_grad(f["zv"])

        l, sl = f["l"], f["sl"]
        dl = dyo * (sl * (1.0 + l * (1.0 - sl)))
        a_clg[...] += _rowsum8(dl * f["yh"])
        a_clb[...] += _rowsum8(dl)
        dyc = _ln_bwd(f["yh"], f["rc"], dl * clg)
        a_ccb[...] += _rowsum8(dyc)
        dy = _conv_anticausal(dyc_ext, dyc, dyc_first[...], ccw, K_C, tt, x=yg, acc_w=a_ccw)
        dyc_first[...] = dyc[0:HR_C, :]
        da = dy * sg
        dg = dy * a * sg * (1.0 - sg)

        dz_ref[...] = jnp.concatenate([dbg, dcg, dxa, dzu, dzv, da, dg], axis=1).astype(MM_DTYPE)

        @pl.when(i == steps - 1)
        def _():
            red = lambda ref: jnp.sum(ref[...], axis=0, keepdims=True)
            dpg_ref[...] = red(a_pg)
            dgg_ref[...] = red(a_gg)
            dlng_ref[...] = red(a_lng)
            dlnb_ref[...] = red(a_lnb)
            dccb_ref[...] = red(a_ccb)
            dclg_ref[...] = red(a_clg)
            dclb_ref[...] = red(a_clb)
            dcaw_ref[...] = jnp.sum(a_caw[...], axis=1)
            dccw_ref[...] = jnp.sum(a_ccw[...], axis=1)
            pm = perm_ref[...]
            tril = lax.broadcasted_iota(jnp.int32, (CHUNK, CHUNK), 0) >= lax.broadcasted_iota(jnp.int32, (CHUNK, CHUNK), 1)
            for h in range(N_HEADS_B):
                dwt = _dot_exact(pm, _dot_exact(a_wm[h], pm, ((1,), (0,))), ((0,), (0,)))
                dw = dwt[0:CHUNK, 0:CHUNK]
                for c in range(1, tt // CHUNK):
                    dw = dw + dwt[c * CHUNK:(c + 1) * CHUNK, c * CHUNK:(c + 1) * CHUNK]
                dwm_ref[h] = jnp.where(tril, dw, 0.0)
            dbt = _dot_exact(pm, a_bias[...], ((0,), (0,)))
            db = dbt[0:CHUNK, :]
            for c in range(1, tt // CHUNK):
                db = db + dbt[c * CHUNK:(c + 1) * CHUNK, :]
            lane_head = lax.broadcasted_iota(jnp.int32, (D_B, CHUNK), 0) // HEAD
            fold = (lane_head == lax.broadcasted_iota(jnp.int32, (D_B, CHUNK), 1)).astype(F32)
            dbias_ref[...] = _dot_exact(db, fold, ((1,), (0,)))

    rev = lambda c: pl.BlockSpec((tt, c), lambda i: (steps - 1 - i, 0))
    full = lambda shape: pl.BlockSpec(shape, lambda i: (0,) * len(shape))
    sds = jax.ShapeDtypeStruct
    return _pallas(
        body, rider, name="mixer_bwd", steps=steps,
        in_specs=[rev(D_MODEL), rev(D_MODEL), rev(D_IN), rev(D_A + D_C)] + _mixer_param_specs(tt)
        + [_const_spec((tt, tt)), _const_spec((1, D_MODEL)), _weight_spec(wog), _const_spec((1, D_MODEL))],
        out_specs=[rev(D_IN), rev(D_MODEL), rev(D_MODEL), full((1, D_MODEL)), full((1, D_MODEL)), full((8, D_A)),
                   full((1, D_B)), full((1, D_B)), full((N_HEADS_B, CHUNK, CHUNK)), full((CHUNK, CHUNK)), full((32, D_C)),
                   full((1, D_C)), full((1, D_C)), full((1, D_C))],
        out_shape=[sds((t, D_IN), MM_DTYPE), sds((t, D_MODEL), MM_DTYPE), sds((t, D_MODEL), MM_DTYPE),
                   sds((1, D_MODEL), F32), sds((1, D_MODEL), F32), sds((8, D_A), F32), sds((1, D_B), F32), sds((1, D_B), F32),
                   sds((N_HEADS_B, CHUNK, CHUNK), F32), sds((CHUNK, CHUNK), F32), sds((32, D_C), F32), sds((1, D_C), F32),
                   sds((1, D_C), F32), sds((1, D_C), F32)],
        scratch_shapes=[pltpu.VMEM((tt + HR_A, D_A), F32), pltpu.VMEM((tt + HR_C, D_C), F32),
                        pltpu.VMEM((HR_A, D_A), F32), pltpu.VMEM((HR_C, D_C), F32),
                        pltpu.VMEM((8, D_MODEL), F32), pltpu.VMEM((8, D_MODEL), F32), pltpu.VMEM((8, 8, D_A), F32),
                        pltpu.VMEM((8, D_B), F32), pltpu.VMEM((8, D_B), F32), pltpu.VMEM((32, 8, D_C), F32),
                        pltpu.VMEM((8, D_C), F32), pltpu.VMEM((8, D_C), F32), pltpu.VMEM((8, D_C), F32),
                        pltpu.VMEM((N_HEADS_B, tt, tt), MM_DTYPE), pltpu.VMEM((N_HEADS_B, tt, tt), MM_DTYPE),
                        pltpu.VMEM((tt, D_B), F32), pltpu.VMEM((N_HEADS_B, tt, tt), F32), pltpu.VMEM((tt, D_B), F32)],
        args=(dx1, o, z, cv, *mp, perm, grp_g, wog, post_g))


def _fetch_row_blocks(wg_ref, w_scr, sems):
    r = wg_ref.shape[1]
    copies = [pltpu.make_async_copy(wg_ref.at[j], w_scr.at[pl.ds(r * j, r), :], sems.at[j]) for j in range(N_CHIPS)]
    for cp in copies:
        cp.start()
    for cp in copies:
        cp.wait()


def _ffn_conv(ext, cw, c0, cn, tt):
    acc = cw[0:1, c0:c0 + cn] * ext[0:tt, c0:c0 + cn]
    for k in range(1, K_F):
        acc = acc + cw[k:k + 1, c0:c0 + cn] * ext[8 * k:8 * k + tt, c0:c0 + cn]
    return acc


def ffn_fwd(up0, x1, cw, wdg, post_g, tt, rider=None):
    t = up0.shape[0]
    assert t % tt == 0, (t, tt)
    cn = _col_chunk(D_FF)

    def body(up0_ref, x1_ref, cw_ref, wdg_ref, pg_ref, d_ref, x2_ref, ext, last, wd_ref, sems):
        i = pl.program_id(0)

        @pl.when(i == 0)
        def _():
            _fetch_row_blocks(wdg_ref, wd_ref, sems)
            last[...] = jnp.zeros_like(last)

        ext[HR_F:HR_F + tt, :] = up0_ref[...]
        ext[0:HR_F, :] = _halo_before(up0_ref[tt - HR_F:, :], last[...])
        last[...] = up0_ref[tt - HR_F:, :]
        cwv = cw_ref[...]
        d = jnp.zeros((tt, D_MODEL), F32)
        for c0 in range(0, D_FF, cn):
            gate = _ffn_conv(ext, cwv, c0, cn, tt)
            val = _ffn_conv(ext, cwv, D_FF + c0, cn, tt)
            act = (gate * jax.nn.sigmoid(gate) * val).astype(MM_DTYPE)
            d = d + _dot(act, wd_ref[c0:c0 + cn, :])
        d_ref[...] = d
        x2_ref[...] = x1_ref[...] + d * _rstd(d) * pg_ref[...]

    row = lambda c: pl.BlockSpec((tt, c), lambda i: (i, 0))
    return _pallas(
        body, rider, name="ffn_fwd", steps=t // tt,
        in_specs=[row(2 * D_FF), row(D_MODEL), _const_spec((8, 2 * D_FF)), _ANY, _const_spec((1, D_MODEL))],
        out_specs=[row(D_MODEL), row(D_MODEL)],
        out_shape=[jax.ShapeDtypeStruct((t, D_MODEL), F32), jax.ShapeDtypeStruct((t, D_MODEL), F32)],
        scratch_shapes=[pltpu.VMEM((HR_F + tt, 2 * D_FF), F32), pltpu.VMEM((HR_F, 2 * D_FF), F32),
                        pltpu.VMEM((D_FF, D_MODEL), MM_DTYPE), pltpu.SemaphoreType.DMA((N_CHIPS,))],
        args=(up0, x1, cw, wdg, post_g))


def ffn_bwd(dx2, d, up0, cw, wdg, post_g, tt, rider=None):
    t = up0.shape[0]
    assert t % tt == 0, (t, tt)
    steps = t // tt
    hb = tt // HR_F
    cn = _col_chunk(D_FF)

    def body(dx2_ref, d_ref, up0_ref, uh_ref, cw_ref, wdg_ref, pg_ref,
             dd_ref, act_ref, dup0_ref, dpg_ref, dcw_ref, ext, dup_ext, first, a_pg, a_cw, wd_ref, sems):
        i = pl.program_id(0)
        tile = steps - 1 - i

        @pl.when(i == 0)
        def _():
            _fetch_row_blocks(wdg_ref, wd_ref, sems)
            a_pg[...] = jnp.zeros_like(a_pg)
            a_cw[...] = jnp.zeros_like(a_cw)
            first[...] = jnp.zeros_like(first)

        ext[HR_F:HR_F + tt, :] = up0_ref[...]
        ext[0:HR_F, :] = _halo_before(up0_ref[tt - HR_F:, :], jnp.where(tile > 0, uh_ref[...], 0.0))
        cwv = cw_ref[...]
        dv = d_ref[...]
        dx2v = dx2_ref[...]
        r = _rstd(dv)
        a_pg[...] += _rowsum8(dx2v * dv * r)
        dd = _rms_bwd(dv, r, pg_ref[...], dx2v).astype(MM_DTYPE)
        dd_ref[...] = dd
        for c0 in range(0, D_FF, cn):
            gate = _ffn_conv(ext, cwv, c0, cn, tt)
            val = _ffn_conv(ext, cwv, D_FF + c0, cn, tt)
            sg = jax.nn.sigmoid(gate)
            sl = gate * sg
            act_ref[:, c0:c0 + cn] = (sl * val).astype(MM_DTYPE)
            da = _dot_nt(dd, wd_ref[c0:c0 + cn, :])
            dup_ext[0:tt, c0:c0 + cn] = da * val * (sg * (1.0 + gate * (1.0 - sg)))
            dup_ext[0:tt, D_FF + c0:D_FF + c0 + cn] = da * sl
        dup_ext[tt:tt + HR_F, :] = _halo_after(dup_ext[0:HR_F, :], first[...])
        first[...] = dup_ext[0:HR_F, :]
        for c0 in range(0, 2 * D_FF, cn):
            x = up0_ref[:, c0:c0 + cn]
            acc = None
            for k in range(K_F):
                off = 8 * (K_F - 1 - k)
                ld = dup_ext[off:off + tt, c0:c0 + cn]
                term = cwv[k:k + 1, c0:c0 + cn] * ld
                acc = term if acc is None else acc + term
                a_cw[k, :, c0:c0 + cn] += _rowsum8(ld * x)
            dup0_ref[:, c0:c0 + cn] = acc.astype(MM_DTYPE)

        @pl.when(i == steps - 1)
        def _():
            dpg_ref[...] = jnp.sum(a_pg[...], axis=0, keepdims=True)
            dcw_ref[...] = jnp.sum(a_cw[...], axis=1)

    rev = lambda c: pl.BlockSpec((tt, c), lambda i: (steps - 1 - i, 0))
    halo = pl.BlockSpec((HR_F, 2 * D_FF), lambda i: (jnp.maximum((steps - 1 - i) * hb - 1, 0), 0))
    full = lambda shape: pl.BlockSpec(shape, lambda i: (0,) * len(shape))
    sds = jax.ShapeDtypeStruct
    return _pallas(
        body, rider, name="ffn_bwd", steps=steps,
        in_specs=[rev(D_MODEL), rev(D_MODEL), rev(2 * D_FF), halo, _const_spec((8, 2 * D_FF)), _ANY,
                  _const_spec((1, D_MODEL))],
        out_specs=[rev(D_MODEL), rev(D_FF), rev(2 * D_FF), full((1, D_MODEL)), full((8, 2 * D_FF))],
        out_shape=[sds((t, D_MODEL), MM_DTYPE), sds((t, D_FF), MM_DTYPE), sds((t, 2 * D_FF), MM_DTYPE),
                   sds((1, D_MODEL), F32), sds((8, 2 * D_FF), F32)],
        scratch_shapes=[pltpu.VMEM((HR_F + tt, 2 * D_FF), F32), pltpu.VMEM((tt + HR_F, 2 * D_FF), F32),
                        pltpu.VMEM((HR_F, 2 * D_FF), F32), pltpu.VMEM((8, D_MODEL), F32),
                        pltpu.VMEM((8, 8, 2 * D_FF), F32), pltpu.VMEM((D_FF, D_MODEL), MM_DTYPE),
                        pltpu.SemaphoreType.DMA((N_CHIPS,))],
        args=(dx2, d, up0, up0, cw, wdg, post_g))


def loss_head(y, target, tm):
    t, d = y.shape
    assert t % tm == 0, (t, tm)
    steps = t // tm

    def body(y_ref, t_ref, dy_ref, loss_ref, acc):
        i = pl.program_id(0)

        @pl.when(i == 0)
        def _():
            acc[...] = jnp.zeros_like(acc)

        diff = y_ref[...] - t_ref[...]
        dy_ref[...] = diff * (1.0 / d)
        acc[...] += _rowsum8(diff * diff)

        @pl.when(i == steps - 1)
        def _():
            loss_ref[...] = (0.5 / d) * jnp.sum(jnp.sum(acc[...], axis=0, keepdims=True), axis=1, keepdims=True)

    row = pl.BlockSpec((tm, d), lambda i: (i, 0))
    return pl.pallas_call(
        body, name="loss_head", grid=(steps,), in_specs=[row, row],
        out_specs=[row, pl.BlockSpec((1, 1), lambda i: (0, 0))],
        out_shape=[jax.ShapeDtypeStruct((t, d), F32), jax.ShapeDtypeStruct((1, 1), F32)],
        scratch_shapes=[pltpu.VMEM((8, d), F32)],
        compiler_params=_params(("arbitrary",)),
    )(y, target)


def adamw(w, g, m, v):
    shape = w.shape
    cols = shape[-1]
    rows = w.size // cols
    tr = next((r for r in (512, 256, 128) if rows % r == 0 and rows > r), rows)
    c1 = 1.0 - ADAM_B1 ** ADAM_STEP
    c2 = 1.0 - ADAM_B2 ** ADAM_STEP

    def body(w_ref, g_ref, m_ref, v_ref, d_ref, nm_ref, nv_ref):
        gv = g_ref[...]
        nm = ADAM_B1 * m_ref[...] + (1.0 - ADAM_B1) * gv
        nv = ADAM_B2 * v_ref[...] + (1.0 - ADAM_B2) * (gv * gv)
        nm_ref[...] = nm
        nv_ref[...] = nv
        d_ref[...] = -ADAM_LR * ((nm / c1) / (jnp.sqrt(nv / c2) + ADAM_EPS) + ADAM_WD * w_ref[...])

    spec = pl.BlockSpec((tr, cols), lambda i: (i, 0))
    out = jax.ShapeDtypeStruct((rows, cols), F32)
    res = pl.pallas_call(
        body, name="adamw", grid=(rows // tr,), in_specs=[spec] * 4, out_specs=[spec] * 3, out_shape=[out] * 3,
        compiler_params=_params(("arbitrary",)),
    )(*[a.reshape(rows, cols) for a in (w, g, m, v)])
    return tuple(r.reshape(shape) for r in res)


def _place():
    return lax.axis_index("x"), lax.axis_index("y"), lax.axis_index("c")


def _other_chips(x, y):
    return [(1 - x, y, 2 * (1 - x) + y), (x, 1 - y, 2 * x + 1 - y), (1 - x, 1 - y, 2 * (1 - x) + 1 - y)]


def _sem_specs(*counts):
    return [pltpu.SemaphoreType.DMA((n,)) for n in counts]


def cast_shard(w, layer, chip):
    _, r, c = w.shape

    def body(chip_ref, w_ref, o_ref):
        del chip_ref
        o_ref[...] = w_ref[...].astype(MM_DTYPE)

    grid_spec = pltpu.PrefetchScalarGridSpec(
        num_scalar_prefetch=1, grid=(1,), in_specs=[pl.BlockSpec((None, r, c), lambda i, chip_ref: (layer, 0, 0))],
        out_specs=pl.BlockSpec((None, r, c), lambda i, chip_ref: (chip_ref[0], 0, 0)))
    return pl.pallas_call(
        body, name="cast_shard", grid_spec=grid_spec, out_shape=jax.ShapeDtypeStruct((N_CHIPS, r, c), MM_DTYPE),
        compiler_params=_params(("arbitrary",)),
    )(jnp.reshape(chip, (1,)).astype(jnp.int32), w)


def _row_half(buf, chip, mine, c):
    rh = buf.shape[1] // 2
    return buf.at[chip, pl.ds(pl.multiple_of((c if mine else 1 - c) * rh, 16), rh), :]


def spread_rider(bufs):
    n = len(bufs)

    def start(rin, rout, sems):
        x, y, c = _place()
        me = 2 * x + y
        for k, (px, py, _) in enumerate(_other_chips(x, y)):
            for i, buf in enumerate(rout):
                part = _row_half(buf, me, True, c)
                pltpu.make_async_remote_copy(
                    src_ref=part, dst_ref=part, send_sem=sems[0].at[n * k + i], recv_sem=sems[1].at[n * k + i],
                    device_id=(px, py, c), device_id_type=MESH_ID).start()

    def wait(rin, rout, sems):
        x, y, c = _place()
        for k, (_, _, pj) in enumerate(_other_chips(x, y)):
            for i, buf in enumerate(rout):
                part = _row_half(buf, pj, True, c)
                pltpu.make_async_remote_copy(
                    src_ref=part, dst_ref=part, send_sem=sems[0].at[n * k + i], recv_sem=sems[1].at[n * k + i],
                    device_id=(x, y, c), device_id_type=MESH_ID).wait()

    shapes = [jax.ShapeDtypeStruct(b.shape, b.dtype) for b in bufs]
    return Rider("spread", list(bufs), shapes, {i: i for i in range(n)}, (3 * n, 3 * n), start, wait)


def pass_rider(bufs):
    n = len(bufs)

    def start(rin, rout, sems):
        x, y, c = _place()
        for k, (_, _, pj) in enumerate(_other_chips(x, y)):
            for i, buf in enumerate(rout):
                part = _row_half(buf, pj, True, c)
                pltpu.make_async_remote_copy(
                    src_ref=part, dst_ref=part, send_sem=sems[0].at[n * k + i], recv_sem=sems[1].at[n * k + i],
                    device_id=(x, y, 1 - c), device_id_type=MESH_ID).start()

    def wait(rin, rout, sems):
        x, y, c = _place()
        for k, (_, _, pj) in enumerate(_other_chips(x, y)):
            for i, buf in enumerate(rout):
                part = _row_half(buf, pj, False, c)
                pltpu.make_async_remote_copy(
                    src_ref=part, dst_ref=part, send_sem=sems[0].at[n * k + i], recv_sem=sems[1].at[n * k + i],
                    device_id=(x, y, 1 - c), device_id_type=MESH_ID).wait()

    shapes = [jax.ShapeDtypeStruct(b.shape, b.dtype) for b in bufs]
    return Rider("pass", list(bufs), shapes, {i: i for i in range(n)}, (3 * n, 3 * n), start, wait)


def both_riders(a, b):
    na, oa, sa = len(a.inputs), len(a.out_shapes), len(a.sems)

    def start(rin, rout, sems):
        a.start(rin[:na], rout[:oa], sems[:sa])
        b.start(rin[na:], rout[oa:], sems[sa:])

    def wait(rin, rout, sems):
        a.wait(rin[:na], rout[:oa], sems[:sa])
        b.wait(rin[na:], rout[oa:], sems[sa:])

    aliases = dict(a.aliases)
    aliases.update({na + i: oa + o for i, o in b.aliases.items()})
    return Rider(a.name + "_" + b.name, a.inputs + b.inputs, a.out_shapes + b.out_shapes, aliases, a.sems + b.sems,
                 start, wait)


def gather_small(small):
    def body(small_ref, out_ref, send, recv, local):
        x, y, c = _place()
        me = 2 * x + y
        chips = _other_chips(x, y)
        own = pltpu.make_async_copy(small_ref, out_ref.at[me], local.at[0])
        own.start()
        sends = [pltpu.make_async_remote_copy(src_ref=small_ref, dst_ref=out_ref.at[me], send_sem=send.at[k],
                                              recv_sem=recv.at[k], device_id=(px, py, c), device_id_type=MESH_ID)
                 for k, (px, py, _) in enumerate(chips)]
        for cp in sends:
            cp.start()
        for k, (_, _, pj) in enumerate(chips):
            pltpu.make_async_remote_copy(src_ref=small_ref, dst_ref=out_ref.at[pj], send_sem=send.at[k], recv_sem=recv.at[k],
                                         device_id=(x, y, c), device_id_type=MESH_ID).wait_recv()
        for cp in sends:
            cp.wait_send()
        own.wait()

    return pl.pallas_call(
        body, name="gather_small", in_specs=[_ANY], out_specs=_ANY,
        out_shape=jax.ShapeDtypeStruct((N_CHIPS,) + small.shape, small.dtype), scratch_shapes=_sem_specs(3, 3, 1),
        compiler_params=pltpu.CompilerParams(has_side_effects=True),
    )(small)


def swap_rider(gs):
    n = len(gs)

    def copies(rin, rout, sems):
        x, y, c = _place()
        out = []
        for i, (g, got) in enumerate(zip(rin, rout)):
            rh = g.shape[1] // 2
            theirs = pl.ds(pl.multiple_of((1 - c) * rh, 8), rh)
            out.append(pltpu.make_async_remote_copy(
                src_ref=g.at[:, theirs, :], dst_ref=got, send_sem=sems[0].at[i], recv_sem=sems[1].at[i],
                device_id=(x, y, 1 - c), device_id_type=MESH_ID))
        return out

    def start(rin, rout, sems):
        for cp in copies(rin, rout, sems):
            cp.start()

    def wait(rin, rout, sems):
        for cp in copies(rin, rout, sems):
            cp.wait()

    shapes = [jax.ShapeDtypeStruct((g.shape[0], g.shape[1] // 2, g.shape[2]), g.dtype) for g in gs]
    return Rider("swap", list(gs), shapes, {}, (n, n), start, wait)


def scatter_rider(sbs):
    n = len(sbs)

    def start(rin, rout, sems):
        x, y, c = _place()
        me = 2 * x + y
        for k, (px, py, pj) in enumerate(_other_chips(x, y)):
            for i, (sb, got) in enumerate(zip(rin, rout)):
                pltpu.make_async_remote_copy(
                    src_ref=sb.at[pj], dst_ref=got.at[me], send_sem=sems[0].at[n * k + i], recv_sem=sems[1].at[n * k + i],
                    device_id=(px, py, c), device_id_type=MESH_ID).start()

    def wait(rin, rout, sems):
        x, y, c = _place()
        for k, (_, _, pj) in enumerate(_other_chips(x, y)):
            for i, (sb, got) in enumerate(zip(rin, rout)):
                cp = pltpu.make_async_remote_copy(
                    src_ref=sb.at[pj], dst_ref=got.at[pj], send_sem=sems[0].at[n * k + i], recv_sem=sems[1].at[n * k + i],
                    device_id=(x, y, c), device_id_type=MESH_ID)
                cp.wait_recv()
                cp.wait_send()

    shapes = [jax.ShapeDtypeStruct(sb.shape, sb.dtype) for sb in sbs]
    return Rider("scatter", list(sbs), shapes, {}, (3 * n, 3 * n), start, wait)


def join_rider(fs, layers):
    n = len(fs)

    def half(i, f, mine, place):
        x, y, c = place
        rh = f.shape[1] // 2
        block = 2 * x + y if layers[i] is None else layers[i]
        return f.at[block, pl.ds(pl.multiple_of((c if mine else 1 - c) * rh, 8), rh), :]

    def start(rin, rout, sems):
        x, y, c = _place()
        for i, f in enumerate(rout):
            part = half(i, f, True, (x, y, c))
            pltpu.make_async_remote_copy(
                src_ref=part, dst_ref=part, send_sem=sems[0].at[i], recv_sem=sems[1].at[i],
                device_id=(x, y, 1 - c), device_id_type=MESH_ID).start()

    def wait(rin, rout, sems):
        x, y, c = _place()
        for i, f in enumerate(rout):
            part = half(i, f, False, (x, y, c))
            pltpu.make_async_remote_copy(
                src_ref=part, dst_ref=part, send_sem=sems[0].at[i], recv_sem=sems[1].at[i],
                device_id=(x, y, 1 - c), device_id_type=MESH_ID).wait()

    shapes = [jax.ShapeDtypeStruct(f.shape, f.dtype) for f in fs]
    return Rider("join", list(fs), shapes, {i: i for i in range(n)}, (n, n), start, wait)


def add_halves(gs, gots, wire=BF16):
    m = len(gs)
    x, y, c = _place()

    def body(p_ref, *refs):
        ins, outs = refs[:2 * m], refs[2 * m:]
        for i in range(m):
            s = ins[2 * i][...] + ins[2 * i + 1][...]
            outs[2 * i][...] = s.astype(wire)

            @pl.when(pl.program_id(0) == p_ref[0])
            def _(s=s, own_ref=outs[2 * i + 1]):
                own_ref[...] = s

    in_specs, out_specs, out_shape = [], [], []
    for got in gots:
        n, rh, cols = got.shape
        blk = (None, rh, cols)
        in_specs += [pl.BlockSpec(blk, lambda j, p_ref: (j, p_ref[1], 0)), pl.BlockSpec(blk, lambda j, p_ref: (j, 0, 0))]
        out_specs += [pl.BlockSpec(blk, lambda j, p_ref: (j, 0, 0)), pl.BlockSpec((rh, cols), lambda j, p_ref: (0, 0))]
        out_shape += [jax.ShapeDtypeStruct(got.shape, wire), jax.ShapeDtypeStruct((rh, cols), F32)]
    grid_spec = pltpu.PrefetchScalarGridSpec(num_scalar_prefetch=1, grid=(N_CHIPS,), in_specs=in_specs, out_specs=out_specs)
    res = pl.pallas_call(
        body, name="add_halves", grid_spec=grid_spec, out_shape=out_shape, compiler_params=_params(("arbitrary",)),
    )(jnp.stack([2 * x + y, c]).astype(jnp.int32), *[a for pair in zip(gs, gots) for a in pair])
    return [(res[2 * i + 1], res[2 * i]) for i in range(m)]


def add_chips(owns, gots, fbufs, block=None):
    m = len(owns)
    x, y, c = _place()
    me = 2 * x + y

    def body(p_ref, *refs):
        ins, outs = refs[:5 * m], refs[5 * m:]
        for i in range(m):
            s_ref, g1_ref, g2_ref, g3_ref, _ = ins[5 * i:5 * i + 5]
            outs[i][...] = s_ref[...] + g1_ref[...].astype(F32) + g2_ref[...].astype(F32) + g3_ref[...].astype(F32)

    def other(blk, n, k):
        return pl.BlockSpec(blk, lambda i, p_ref: ((p_ref[0] + k) % n, 0, 0))

    in_specs, out_specs, args = [], [], []
    for own, got, fbuf in zip(owns, gots, fbufs):
        n, rh, cols = got.shape
        blk = (None, rh, cols)
        in_specs += [pl.BlockSpec((rh, cols), lambda i, p_ref: (0, 0)), other(blk, n, 1), other(blk, n, 2), other(blk, n, 3),
                     _ANY]
        out_specs.append(pl.BlockSpec(blk, lambda i, p_ref: (p_ref[2], p_ref[1], 0)))
        args += [own, got, got, got, fbuf]
    grid_spec = pltpu.PrefetchScalarGridSpec(num_scalar_prefetch=1, grid=(1,), in_specs=in_specs, out_specs=out_specs)
    return pl.pallas_call(
        body, name="add_chips", grid_spec=grid_spec, out_shape=[jax.ShapeDtypeStruct(f.shape, F32) for f in fbufs],
        input_output_aliases={5 * i + 5: i for i in range(m)}, compiler_params=_params(("arbitrary",)),
    )(jnp.stack([me, c, me if block is None else block]).astype(jnp.int32), *args)


def _pack(arrays, rows):
    flat = jnp.concatenate([a.reshape(-1) for a in arrays])
    return jnp.pad(flat, (0, rows * LANES - flat.size)).reshape(rows, LANES)


def _unpack(buf, shapes):
    flat = buf.reshape(-1)
    out, at = [], 0
    for s in shapes:
        n = math.prod(s)
        out.append(flat[at:at + n].reshape(s))
        at += n
    return out


CONV_SHARDS = [(DEPTH, K_A, D_A // N_CHIPS), (DEPTH, K_C, D_C // N_CHIPS), (DEPTH, K_F, 2 * D_FF // N_CHIPS)]
CONV_ROWS = 32
SMALL_ROWS = 640


def _join_cols(g):
    n, l, r, c = g.shape
    return jnp.transpose(g, (1, 2, 0, 3)).reshape(l, r, n * c)


BIG = ["w_in", "w_out", "w_up", "w_down"]
TILE_MM = 512
TILE_TN = 1024
TILE_EW = 256


def _pad_rows(a, rows):
    return jnp.pad(a, ((0, rows - a.shape[0]), (0, 0)))


def _row(a):
    return a.reshape(1, -1)


def _tile_perm(tt):
    p = lax.broadcasted_iota(jnp.int32, (tt, tt), 0)
    tok = lax.broadcasted_iota(jnp.int32, (tt, tt), 1)
    return ((tt // 8) * (p % 8) + p // 8 == tok).astype(F32)


def _layer_params(wl, tt):
    n = tt // CHUNK
    tril = jnp.tril(jnp.ones((CHUNK, CHUNK), bool))
    wm = jnp.where(tril[None], wl["sgu_w"], 0.0)
    eye = jnp.eye(n, dtype=F32)
    wt = (eye[None, :, None, :, None] * wm[:, None, :, None, :]).reshape(N_HEADS_B, tt, tt)
    bias_e = jnp.repeat(wl["sgu_b"].T, HEAD, axis=1)
    return (_pad_rows(wl["conv_a_w"], 8), _row(wl["sgu_ln_g"]), _row(wl["sgu_ln_b"]), wt.astype(MM_DTYPE),
            jnp.tile(bias_e, (n, 1)), _pad_rows(wl["conv_c_w"], 32), _row(wl["conv_c_b"]), _row(wl["conv_ln_g"]),
            _row(wl["conv_ln_b"]))


def layer_fwd(x, wl, gw, nxt=None, tm=TILE_MM, tt=TILE_EW):
    mp = _layer_params(wl, tt)
    ride = pass_rider([gw["w_down"]]) if gw.get("pass_down") else None
    (z, h), done = norm_matmul(x, _row(wl["pre_mix_g"]), gw["w_in"], tm, rider=ride)
    gw = {n: (done[0] if ride and n == "w_down" else gw[n]) for n in BIG}
    ride = spread_rider([nxt["w_in"], nxt["w_out"]]) if nxt else None
    (o, x1, cv), done = mixer_fwd(z, x, mp, _tile_perm(tt), _row(wl["grp_norm_g"]), gw["w_out"], _row(wl["post_mix_g"]), tt,
                                  rider=ride)
    ride = both_riders(spread_rider([nxt["w_up"]]), pass_rider(list(done))) if nxt else None
    (up0, h2), done = norm_matmul(x1, _row(wl["pre_ffn_g"]), gw["w_up"], tm, rider=ride)
    if nxt:
        nxt = dict(nxt, w_up=done[0], w_in=done[1], w_out=done[2])
        ride = both_riders(spread_rider([nxt["w_down"]]), pass_rider([nxt["w_up"]]))
    (d, x2), done = ffn_fwd(up0, x1, _pad_rows(wl["ffn_conv_w"], 8), gw["w_down"], _row(wl["post_ffn_g"]), tt, rider=ride)
    if nxt:
        nxt = dict(nxt, w_down=done[0], w_up=done[1], pass_down=True)
    return x2, dict(x=x, z=z, h=h, o=o, x1=x1, up0=up0, h2=h2, d=d, cv=cv, gw=gw), nxt


WIDE = ["w_up", "w_down"]
NARROW = ["w_in", "w_out"]


def layer_bwd(dx2, wl, layer, sv, pend=None, exchange=True, tm=TILE_MM, tt=TILE_EW):
    mp = _layer_params(wl, tt)
    gw = sv["gw"]
    tk = min(TILE_TN, dx2.shape[0])
    at = {n: BIG.index(n) for n in BIG}
    g = {}
    ride = scatter_rider([sw for _, sw in pend["narrow"]]) if pend else None
    (dd, act, dup0, dpg, dcw), arrived = ffn_bwd(dx2, sv["d"], sv["up0"], _pad_rows(wl["ffn_conv_w"], 8), gw["w_down"],
                                                 _row(wl["post_ffn_g"]), tt, rider=ride)
    fbuf = list(pend["fbuf"]) if pend else grad_buffers()
    if pend:
        done = add_chips([own for own, _ in pend["narrow"]], arrived, [fbuf[at[n]] for n in NARROW], pend["layer"])
        for n, f in zip(NARROW, done):
            fbuf[at[n]] = f
    g["post_ffn_g"] = dpg[0]
    g["ffn_conv_w"] = dcw[:K_F]
    gl = {}
    gl["w_down"] = matmul_tn_down(act, dd, tk)
    gl["w_up"] = matmul_tn_cols(sv["h2"], dup0, tk)
    ride = swap_rider([gl[n] for n in WIDE]) if exchange else None
    (dx1, dg), got = matmul_nt_norm_bwd(dup0, gw["w_up"], sv["x1"], _row(wl["pre_ffn_g"]), dx2, tm, rider=ride)
    g["pre_ffn_g"] = dg[0]
    wide = add_halves([gl[n] for n in WIDE], got) if exchange else None
    ride = scatter_rider([sw for _, sw in wide]) if exchange else None
    if pend:
        ride = both_riders(join_rider(fbuf, [pend["layer"]] * len(fbuf)), ride)
    (dz, do, yn, dpg, dgg, dcaw, dlng, dlnb, dwm, dbias, dccw, dccb, dclg, dclb), rode = mixer_bwd(
        dx1, sv["o"], sv["z"], sv["cv"], mp, _tile_perm(tt), _row(wl["grp_norm_g"]), gw["w_out"],
        _row(wl["post_mix_g"]), tt, rider=ride)
    if exchange:
        fbuf, arrived = (list(rode[:len(BIG)]), rode[len(BIG):]) if pend else (fbuf, rode)
        done = add_chips([own for own, _ in wide], arrived, [fbuf[at[n]] for n in WIDE], layer)
        for n, f in zip(WIDE, done):
            fbuf[at[n]] = f
    g["post_mix_g"] = dpg[0]
    g["grp_norm_g"] = dgg[0]
    g["conv_a_w"] = dcaw[:K_A]
    g["sgu_ln_g"] = dlng[0]
    g["sgu_ln_b"] = dlnb[0]
    g["sgu_w"] = dwm
    g["sgu_b"] = dbias[:, :N_HEADS_B].T
    g["conv_c_w"] = dccw[:K_C]
    g["conv_c_b"] = dccb[0]
    g["conv_ln_g"] = dclg[0]
    g["conv_ln_b"] = dclb[0]
    gl["w_out"] = matmul_tn_rows(yn, do, tk)
    gl["w_in"] = matmul_tn_in(sv["h"], dz, tk)
    ride = swap_rider([gl[n] for n in NARROW]) if exchange else None
    (dx, dg), got = matmul_nt_norm_bwd(dz, gw["w_in"], sv["x"], _row(wl["pre_mix_g"]), dx1, tm, rider=ride)
    g["pre_mix_g"] = dg[0]
    if not exchange:
        return dx, g, gl
    narrow = add_halves([gl[n] for n in NARROW], got)
    return dx, g, dict(narrow=narrow, fbuf=fbuf, layer=layer)


def grad_buffers():
    return [lax.empty(s, F32) for s in ((DEPTH, D_MODEL, D_IN // N_CHIPS), (DEPTH, D_MODEL // N_CHIPS, D_MODEL),
                                        (DEPTH, D_MODEL, 2 * D_FF // N_CHIPS), (DEPTH, D_FF // N_CHIPS, D_MODEL))]


CONV = ["conv_a_w", "conv_c_w", "ffn_conv_w"]
REPL = ["pre_mix_g", "sgu_ln_g", "sgu_ln_b", "sgu_w", "sgu_b", "conv_c_b", "conv_ln_g", "conv_ln_b", "grp_norm_g",
        "post_mix_g", "pre_ffn_g", "post_ffn_g"]
WEIGHTS = ["pre_mix_g", "w_in", "conv_a_w", "sgu_ln_g", "sgu_ln_b", "sgu_w", "sgu_b", "conv_c_w", "conv_c_b", "conv_ln_g",
           "conv_ln_b", "grp_norm_g", "w_out", "post_mix_g", "pre_ffn_g", "w_up", "ffn_conv_w", "w_down", "post_ffn_g"]


def kernel(x, pre_mix_g, w_in, conv_a_w, sgu_ln_g, sgu_ln_b, sgu_w, sgu_b, conv_c_w, conv_c_b, conv_ln_g, conv_ln_b, grp_norm_g, w_out, post_mix_g, pre_ffn_g, w_up, ffn_conv_w, w_down, post_ffn_g, loss_target, m_pre_mix_g, m_w_in, m_conv_a_w, m_sgu_ln_g, m_sgu_ln_b, m_sgu_w, m_sgu_b, m_conv_c_w, m_conv_c_b, m_conv_ln_g, m_conv_ln_b, m_grp_norm_g, m_w_out, m_post_mix_g, m_pre_ffn_g, m_w_up, m_ffn_conv_w, m_w_down, m_post_ffn_g, v_pre_mix_g, v_w_in, v_conv_a_w, v_sgu_ln_g, v_sgu_ln_b, v_sgu_w, v_sgu_b, v_conv_c_w, v_conv_c_b, v_conv_ln_g, v_conv_ln_b, v_grp_norm_g, v_w_out, v_post_mix_g, v_pre_ffn_g, v_w_up, v_ffn_conv_w, v_w_down, v_post_ffn_g):
    w = dict(pre_mix_g=pre_mix_g, w_in=w_in, conv_a_w=conv_a_w, sgu_ln_g=sgu_ln_g, sgu_ln_b=sgu_ln_b, sgu_w=sgu_w, sgu_b=sgu_b,
             conv_c_w=conv_c_w, conv_c_b=conv_c_b, conv_ln_g=conv_ln_g, conv_ln_b=conv_ln_b, grp_norm_g=grp_norm_g,
             w_out=w_out, post_mix_g=post_mix_g, pre_ffn_g=pre_ffn_g, w_up=w_up, ffn_conv_w=ffn_conv_w, w_down=w_down,
             post_ffn_g=post_ffn_g)
    m = dict(pre_mix_g=m_pre_mix_g, w_in=m_w_in, conv_a_w=m_conv_a_w, sgu_ln_g=m_sgu_ln_g, sgu_ln_b=m_sgu_ln_b,
             sgu_w=m_sgu_w, sgu_b=m_sgu_b, conv_c_w=m_conv_c_w, conv_c_b=m_conv_c_b, conv_ln_g=m_conv_ln_g,
             conv_ln_b=m_conv_ln_b, grp_norm_g=m_grp_norm_g, w_out=m_w_out, post_mix_g=m_post_mix_g,
             pre_ffn_g=m_pre_ffn_g, w_up=m_w_up, ffn_conv_w=m_ffn_conv_w, w_down=m_w_down, post_ffn_g=m_post_ffn_g)
    v = dict(pre_mix_g=v_pre_mix_g, w_in=v_w_in, conv_a_w=v_conv_a_w, sgu_ln_g=v_sgu_ln_g, sgu_ln_b=v_sgu_ln_b,
             sgu_w=v_sgu_w, sgu_b=v_sgu_b, conv_c_w=v_conv_c_w, conv_c_b=v_conv_c_b, conv_ln_g=v_conv_ln_g,
             conv_ln_b=v_conv_ln_b, grp_norm_g=v_grp_norm_g, w_out=v_w_out, post_mix_g=v_post_mix_g,
             pre_ffn_g=v_pre_ffn_g, w_up=v_w_up, ffn_conv_w=v_ffn_conv_w, w_down=v_w_down, post_ffn_g=v_post_ffn_g)
    chip = 2 * lax.axis_index("x") + lax.axis_index("y")

    convs = gather_small(_pack([w[n] for n in CONV], CONV_ROWS))
    gws = [{n: cast_shard(w[n], layer, chip) for n in BIG} for layer in range(DEPTH)]
    first = run_rider(pass_rider(run_rider(spread_rider([gws[0][n] for n in BIG]))))
    gws[0] = dict(zip(BIG, first))
    cparts = [_unpack(convs[j], CONV_SHARDS) for j in range(N_CHIPS)]
    full = dict(w)
    for i, n in enumerate(CONV):
        full[n] = _join_cols(jnp.stack([p[i] for p in cparts]))

    xc = to_tiles(x[0], TILE_EW)
    saved = []
    for layer in range(DEPTH):
        nxt = gws[layer + 1] if layer + 1 < DEPTH else None
        xc, sv, nxt = layer_fwd(xc, {n: full[n][layer] for n in REPL + CONV}, gws[layer], nxt)
        if nxt:
            gws[layer + 1] = nxt
        saved.append(sv)
    dxc, loss_part = loss_head(xc, to_tiles(loss_target[0], TILE_EW), TILE_MM)
    loss = lax.psum(loss_part[0, 0], ("x", "y", "c"))
    small = [None] * DEPTH
    pend = None
    for layer in reversed(range(DEPTH)):
        dxc, small[layer], pend = layer_bwd(dxc, {n: full[n][layer] for n in REPL + CONV}, layer, saved[layer], pend)
    grads = {n: jnp.stack([small[layer][n] for layer in range(DEPTH)]) for n in REPL + CONV}

    gsmall = _pack([grads[n] for n in REPL + CONV], SMALL_ROWS).reshape(N_CHIPS, SMALL_ROWS // N_CHIPS, LANES)
    sums = pend["narrow"] + add_halves([gsmall], run_rider(swap_rider([gsmall])), wire=F32)
    arrived = run_rider(scatter_rider([sw for _, sw in sums]))
    fbuf = list(pend["fbuf"])
    at = [BIG.index(n) for n in NARROW]
    for i, f in zip(at, add_chips([own for own, _ in sums[:2]], arrived[:2], [fbuf[i] for i in at], 0)):
        fbuf[i] = f
    fbuf += add_chips([sums[2][0]], arrived[2:], [lax.empty(gsmall.shape, F32)])
    joined = run_rider(join_rider(fbuf, [0] * len(BIG) + [None]))
    out_g = dict(zip(BIG, joined))
    tot = run_rider(pass_rider(run_rider(spread_rider([joined[len(BIG)]]))))[0].reshape(SMALL_ROWS, LANES)
    shapes = [grads[n].shape for n in REPL + CONV]
    for n, gfull in zip(REPL + CONV, _unpack(tot, shapes)):
        if n in CONV:
            width = gfull.shape[-1] // N_CHIPS
            gfull = lax.dynamic_slice_in_dim(gfull, chip * width, width, axis=2)
        out_g[n] = gfull

    deltas, new_m, new_v = {}, {}, {}
    for n in WEIGHTS:
        deltas[n], new_m[n], new_v[n] = adamw(w[n], out_g[n], m[n], v[n])
    return (loss, from_tiles(dxc, TILE_EW)[None], *[out_g[n] for n in WEIGHTS], *[deltas[n] for n in WEIGHTS], *[new_m[n] for n in WEIGHTS],
            *[new_v[n] for n in WEIGHTS])
```

```python
import math
from typing import Callable, NamedTuple

import jax
import jax.numpy as jnp
from jax import lax
from jax.experimental import pallas as pl
from jax.experimental.pallas import tpu as pltpu

F32 = jnp.float32
BF16 = jnp.bfloat16
MM_DTYPE = BF16

D_MODEL = 1024
DEPTH = 4
D_A = 256
D_B = 384
D_C = 384
D_IN = 3 * D_A + 2 * D_B + 2 * D_C
D_FF = 2816
K_A = 3
K_C = 31
K_F = 3
CHUNK = 128
HEAD = 64
N_HEADS_B = D_B // HEAD
EPS = 1e-6
N_CHIPS = 4

ADAM_LR = 0.001
ADAM_B1 = 0.9
ADAM_B2 = 0.999
ADAM_EPS = 1e-08
ADAM_WD = 0.01
ADAM_STEP = 10

LANES = 1024
VMEM_LIMIT = 56 * 1024 * 1024

MESH_ID = pl.DeviceIdType.MESH
_ANY = pl.BlockSpec(memory_space=pl.ANY)


def _params(sem=None):
    return pltpu.CompilerParams(dimension_semantics=sem, vmem_limit_bytes=VMEM_LIMIT)


def _const_spec(shape):
    nd = len(shape)
    return pl.BlockSpec(shape, lambda *_: (0,) * nd, pipeline_mode=pl.Buffered(1))


def _rowsum8(a):
    r, c = a.shape
    return jnp.sum(a.reshape(r // 8, 8, c), axis=0)


def _rstd(x):
    return lax.rsqrt(jnp.mean(x * x, axis=-1, keepdims=True) + EPS)


def _rms_bwd(x, r, g, dy):
    gdy = g * dy
    return r * gdy - x * (r * r * r) * jnp.mean(gdy * x, axis=-1, keepdims=True)


def _ln_fwd(x):
    mu = jnp.mean(x, axis=-1, keepdims=True)
    xc = x - mu
    r = lax.rsqrt(jnp.mean(xc * xc, axis=-1, keepdims=True) + EPS)
    return xc * r, r


def _ln_bwd(xh, r, dxh):
    return r * (dxh - jnp.mean(dxh, axis=-1, keepdims=True) - xh * jnp.mean(dxh * xh, axis=-1, keepdims=True))


def _gelu(x):
    return 0.5 * x * (1.0 + lax.erf(x * (1.0 / math.sqrt(2.0))))


def _gelu_grad(x):
    cdf = 0.5 * (1.0 + lax.erf(x * (1.0 / math.sqrt(2.0))))
    pdf = jnp.exp(-0.5 * x * x) * (1.0 / math.sqrt(2.0 * math.pi))
    return cdf + x * pdf


def _dot(a, b):
    return jnp.dot(a, b, preferred_element_type=F32)


def _dot_nt(a, b):
    return lax.dot_general(a, b, (((1,), (1,)), ((), ())), preferred_element_type=F32)


def _dot_tn(a, b):
    return lax.dot_general(a, b, (((0,), (0,)), ((), ())), preferred_element_type=F32)


def _col_chunk(n):
    for c in (1408, 1024, 768, 512, 256, 128):
        if n % c == 0:
            return c
    raise ValueError(n)


class Rider(NamedTuple):
    name: str
    inputs: list
    out_shapes: list
    aliases: dict
    sems: tuple
    start: Callable
    wait: Callable


def _pallas(body, rider, *, name, steps, in_specs, out_specs, out_shape, scratch_shapes, args):
    if rider is None:
        res = pl.pallas_call(body, name=name, grid=(steps,), in_specs=in_specs, out_specs=out_specs, out_shape=out_shape,
                             scratch_shapes=scratch_shapes, compiler_params=_params(("arbitrary",)))(*args)
        return res, []
    n_in, n_out, n_scr = len(in_specs), len(out_specs), len(scratch_shapes)
    r_in, r_out = len(rider.inputs), len(rider.out_shapes)

    def wrapped(*refs):
        ins, rin = refs[:n_in], refs[n_in:n_in + r_in]
        at = n_in + r_in
        outs, rout = refs[at:at + n_out], refs[at + n_out:at + n_out + r_out]
        at += n_out + r_out
        scr, rsem = refs[at:at + n_scr], refs[at + n_scr:]

        @pl.when(pl.program_id(0) == 0)
        def _():
            rider.start(rin, rout, rsem)

        body(*ins, *outs, *scr)

        @pl.when(pl.program_id(0) == steps - 1)
        def _():
            rider.wait(rin, rout, rsem)

    res = pl.pallas_call(
        wrapped, name=name + "_" + rider.name, grid=(steps,), in_specs=list(in_specs) + [_ANY] * r_in,
        out_specs=list(out_specs) + [_ANY] * r_out, out_shape=list(out_shape) + list(rider.out_shapes),
        scratch_shapes=list(scratch_shapes) + [pltpu.SemaphoreType.DMA((n,)) for n in rider.sems],
        input_output_aliases={n_in + i: n_out + o for i, o in rider.aliases.items()},
        compiler_params=pltpu.CompilerParams(dimension_semantics=("arbitrary",), vmem_limit_bytes=VMEM_LIMIT,
                                             has_side_effects=True),
    )(*args, *rider.inputs)
    return res[:n_out], res[n_out:]


def run_rider(rider):
    def body(*refs):
        r_in, r_out = len(rider.inputs), len(rider.out_shapes)
        rin, rout, rsem = refs[:r_in], refs[r_in:r_in + r_out], refs[r_in + r_out:]
        rider.start(rin, rout, rsem)
        rider.wait(rin, rout, rsem)

    return pl.pallas_call(
        body, name=rider.name, in_specs=[_ANY] * len(rider.inputs), out_specs=[_ANY] * len(rider.out_shapes),
        out_shape=list(rider.out_shapes), scratch_shapes=[pltpu.SemaphoreType.DMA((n,)) for n in rider.sems],
        input_output_aliases=dict(rider.aliases), compiler_params=pltpu.CompilerParams(has_side_effects=True),
    )(*rider.inputs)


def _weight_spec(wg):
    return _const_spec(wg.shape)


def _join_col_blocks(w_ref, w_scr):
    c = w_ref.shape[2]
    for j in range(N_CHIPS):
        w_scr[:, c * j:c * (j + 1)] = w_ref[j]


def norm_matmul(x, g, wg, tm, rider=None):
    t, d = x.shape
    assert t % tm == 0, (t, tm)
    cw = wg.shape[2]
    n = N_CHIPS * cw
    aligned = cw % 128 == 0
    cn = cw if aligned else _col_chunk(n)

    def body(x_ref, g_ref, w_ref, o_ref, h_ref, *scr):
        if not aligned:
            @pl.when(pl.program_id(0) == 0)
            def _():
                _join_col_blocks(w_ref, scr[0])

        xv = x_ref[...]
        h = (xv * _rstd(xv) * g_ref[...]).astype(MM_DTYPE)
        h_ref[...] = h
        for j, c0 in enumerate(range(0, n, cn)):
            wv = w_ref[j] if aligned else scr[0][:, c0:c0 + cn]
            o_ref[:, c0:c0 + cn] = _dot(h, wv)

    return _pallas(
        body, rider, name="norm_matmul", steps=t // tm,
        in_specs=[pl.BlockSpec((tm, d), lambda i: (i, 0)), _const_spec((1, d)), _weight_spec(wg)],
        out_specs=[pl.BlockSpec((tm, n), lambda i: (i, 0)), pl.BlockSpec((tm, d), lambda i: (i, 0))],
        out_shape=[jax.ShapeDtypeStruct((t, n), F32), jax.ShapeDtypeStruct((t, d), MM_DTYPE)],
        scratch_shapes=[] if aligned else [pltpu.VMEM((d, n), MM_DTYPE)],
        args=(x, g, wg))


def matmul_nt_norm_bwd(gy, wg, x, g, dres, tm, rider=None):
    t, n = gy.shape
    assert t % tm == 0, (t, tm)
    d, cw = wg.shape[1], wg.shape[2]
    aligned = cw % 128 == 0
    cn = cw if aligned else _col_chunk(n)
    steps = t // tm

    def body(gy_ref, w_ref, x_ref, g_ref, dres_ref, dx_ref, dg_ref, acc_ref, *scr):
        i = pl.program_id(0)

        @pl.when(i == 0)
        def _():
            acc_ref[...] = jnp.zeros_like(acc_ref)
            if not aligned:
                _join_col_blocks(w_ref, scr[0])

        dh = jnp.zeros((tm, d), F32)
        for j, c0 in enumerate(range(0, n, cn)):
            wv = w_ref[j] if aligned else scr[0][:, c0:c0 + cn]
            dh = dh + _dot_nt(gy_ref[:, c0:c0 + cn], wv)
        xv = x_ref[...]
        r = _rstd(xv)
        gv = g_ref[...]
        dx_ref[...] = dres_ref[...] + _rms_bwd(xv, r, gv, dh)
        acc_ref[...] += _rowsum8(dh * xv * r)

        @pl.when(i == steps - 1)
        def _():
            dg_ref[...] = jnp.sum(acc_ref[...], axis=0, keepdims=True)

    return _pallas(
        body, rider, name="matmul_nt_norm_bwd", steps=steps,
        in_specs=[pl.BlockSpec((tm, n), lambda i: (i, 0)), _weight_spec(wg), pl.BlockSpec((tm, d), lambda i: (i, 0)),
                  _const_spec((1, d)), pl.BlockSpec((tm, d), lambda i: (i, 0))],
        out_specs=[pl.BlockSpec((tm, d), lambda i: (i, 0)), pl.BlockSpec((1, d), lambda i: (0, 0))],
        out_shape=[jax.ShapeDtypeStruct((t, d), F32), jax.ShapeDtypeStruct((1, d), F32)],
        scratch_shapes=[pltpu.VMEM((8, d), F32)] + ([] if aligned else [pltpu.VMEM((d, n), MM_DTYPE)]),
        args=(gy, wg, x, g, dres))


def matmul_tn_cols(a, b, tk):
    t, r = a.shape
    c = b.shape[1] // N_CHIPS
    assert t % tk == 0 and r % 8 == 0 and c % 128 == 0, (a.shape, b.shape, tk)

    def body(a_ref, b_ref, o_ref):
        @pl.when(pl.program_id(1) == 0)
        def _():
            o_ref[...] = jnp.zeros_like(o_ref)

        o_ref[...] += _dot_tn(a_ref[...], b_ref[...])

    a_spec = pl.BlockSpec((tk, r), lambda j, k: (k, 0))
    b_spec = pl.BlockSpec((tk, c), lambda j, k: (k, j))
    return pl.pallas_call(
        body, name="matmul_tn_cols", grid=(N_CHIPS, t // tk), in_specs=[a_spec, b_spec],
        out_specs=pl.BlockSpec((None, r, c), lambda j, k: (j, 0, 0)),
        out_shape=jax.ShapeDtypeStruct((N_CHIPS, r, c), F32),
        compiler_params=_params(("arbitrary", "arbitrary")),
    )(a, b)


def matmul_tn_down(act, dd, tk):
    t, m = act.shape
    c = dd.shape[1]
    r = m // N_CHIPS
    assert t % tk == 0, (t, tk)
    steps = t // tk

    def body(a_ref, b_ref, o_ref, acc):
        k = pl.program_id(1)

        @pl.when(k == 0)
        def _():
            acc[...] = jnp.zeros_like(acc)

        acc[...] += _dot_tn(a_ref[...], b_ref[...])

        @pl.when(k == steps - 1)
        def _():
            o_ref[0] = acc[0:r, :]
            o_ref[1] = acc[r:2 * r, :]

    return pl.pallas_call(
        body, name="matmul_tn_down", grid=(2, steps),
        in_specs=[pl.BlockSpec((tk, 2 * r), lambda p, k: (k, p)), pl.BlockSpec((tk, c), lambda p, k: (k, 0))],
        out_specs=pl.BlockSpec((2, r, c), lambda p, k: (p, 0, 0)),
        out_shape=jax.ShapeDtypeStruct((N_CHIPS, r, c), F32),
        scratch_shapes=[pltpu.VMEM((2 * r, c), F32)],
        compiler_params=_params(("arbitrary", "arbitrary")),
    )(act, dd)


def matmul_tn_in(h, dz, tk):
    t, d = h.shape
    n = dz.shape[1]
    c = n // N_CHIPS
    assert t % tk == 0, (t, tk)
    steps = t // tk

    def body(a_ref, b_ref, o_ref, acc):
        k = pl.program_id(0)

        @pl.when(k == 0)
        def _():
            acc[...] = jnp.zeros_like(acc)

        acc[...] += _dot_tn(a_ref[...], b_ref[...])

        @pl.when(k == steps - 1)
        def _():
            for j in range(N_CHIPS):
                o_ref[j] = acc[:, c * j:c * (j + 1)]

    return pl.pallas_call(
        body, name="matmul_tn_in", grid=(steps,),
        in_specs=[pl.BlockSpec((tk, d), lambda k: (k, 0)), pl.BlockSpec((tk, n), lambda k: (k, 0))],
        out_specs=pl.BlockSpec((N_CHIPS, d, c), lambda k: (0, 0, 0)),
        out_shape=jax.ShapeDtypeStruct((N_CHIPS, d, c), F32),
        scratch_shapes=[pltpu.VMEM((d, n), F32)],
        compiler_params=_params(("arbitrary",)),
    )(h, dz)


def to_tiles(a, tt):
    t = a.shape[0]
    return a.reshape((t // tt, 8, tt // 8) + a.shape[1:]).swapaxes(1, 2).reshape(a.shape)


def from_tiles(a, tt):
    t = a.shape[0]
    return a.reshape((t // tt, tt // 8, 8) + a.shape[1:]).swapaxes(1, 2).reshape(a.shape)


def _roll_sublanes(a, shift):
    n = a.shape[0] // 8
    return pltpu.roll(a.reshape(n, 8, a.shape[1]), shift, 1).reshape(a.shape)


def _halo_before(cur_last, prev_last):
    sub = lax.broadcasted_iota(jnp.int32, cur_last.shape, 0) % 8
    return jnp.where(sub == 0, _roll_sublanes(prev_last, 1), _roll_sublanes(cur_last, 1))


def _halo_after(cur_first, next_first):
    sub = lax.broadcasted_iota(jnp.int32, cur_first.shape, 0) % 8
    return jnp.where(sub == 7, _roll_sublanes(next_first, 7), _roll_sublanes(cur_first, 7))


def _conv_causal(ext, cur, prev_last, w, taps, tt, cols=None):
    hr = 8 * (taps - 1)
    cs = slice(None) if cols is None else cols
    ext[hr:hr + tt, cs] = cur
    ext[0:hr, cs] = _halo_before(cur[tt - hr:, :], prev_last)
    acc = w[0:1, :] * ext[0:tt, cs]
    for k in range(1, taps):
        acc = acc + w[k:k + 1, :] * ext[8 * k:8 * k + tt, cs]
    return acc


def _conv_anticausal(ext, cur, next_first, w, taps, tt, x=None, acc_w=None, cols=None):
    hr = 8 * (taps - 1)
    cs = slice(None) if cols is None else cols
    ext[0:tt, cs] = cur
    ext[tt:tt + hr, cs] = _halo_after(cur[0:hr, :], next_first)
    acc = None
    for k in range(taps):
        off = 8 * (taps - 1 - k)
        ld = ext[off:off + tt, cs]
        term = w[k:k + 1, :] * ld
        acc = term if acc is None else acc + term
        if x is not None:
            acc_w[k, :, cs] += _rowsum8(ld * x)
    return acc


def _dot_exact(a, b, dims):
    return lax.dot_general(a, b, (dims, ((), ())), precision=lax.Precision.HIGHEST, preferred_element_type=F32)


def _to_tile_order(perm, wt_ref, w_scr, transpose):
    pb = perm.astype(MM_DTYPE)
    for h in range(N_HEADS_B):
        half = (_dot_nt(pb, wt_ref[h]) if transpose else _dot(pb, wt_ref[h])).astype(MM_DTYPE)
        w_scr[h] = _dot_nt(half, pb).astype(MM_DTYPE)


def _project_rows(y, w_ref):
    r = w_ref.shape[1]
    acc = _dot(y[:, 0:r], w_ref[0])
    for j in range(1, N_CHIPS):
        acc = acc + _dot(y[:, r * j:r * (j + 1)], w_ref[j])
    return acc


PAIR = 2 * HEAD


def _pair_lanes(h):
    return slice(PAIR * (h // 2), PAIR * (h // 2 + 1))


def _per_head(fn):
    cols = []
    for p in range(N_HEADS_B // 2):
        lo, hi = fn(2 * p), fn(2 * p + 1)
        cols.append(jnp.where(lax.broadcasted_iota(jnp.int32, lo.shape, 1) < HEAD, lo, hi))
    return jnp.concatenate(cols, axis=1)


def _mixer_forward(z, prm, q, yc):
    _, lng, lnb, wm, bias_p, _, _, clg, clb = prm
    bg = z[:, 0:D_A]
    ya = bg * q
    o_b = 3 * D_A
    zu = z[:, o_b:o_b + D_B]
    zv = z[:, o_b + D_B:o_b + 2 * D_B]
    u = _gelu(zu)
    vh, rv = _ln_fwd(_gelu(zv))
    vnb = (vh * lng + lnb).astype(MM_DTYPE)
    s = _per_head(lambda h: _dot(wm[h], vnb[:, _pair_lanes(h)])) + bias_p
    yb = u * s
    yh, rc = _ln_fwd(yc)
    l = yh * clg + clb
    sl = jax.nn.sigmoid(l)
    return dict(bg=bg, q=q, ya=ya, zu=zu, zv=zv, u=u, vh=vh, rv=rv, vnb=vnb, s=s, yb=yb, yh=yh, rc=rc, l=l, sl=sl,
                yo=l * sl)


def _conv_inputs(z):
    o_c = 3 * D_A + 2 * D_B
    a = z[:, o_c:o_c + D_C]
    sg = jax.nn.sigmoid(z[:, o_c + D_C:o_c + 2 * D_C])
    return z[:, D_A:2 * D_A] * z[:, 2 * D_A:3 * D_A], a * sg, a, sg


def _group_norm(f, gg):
    ya, yb, yo = f["ya"], f["yb"], f["yo"]
    ra, rb, ro = _rstd(ya), _rstd(yb), _rstd(yo)
    yn = jnp.concatenate([ya * ra * gg[:, 0:D_A], yb * rb * gg[:, D_A:D_A + D_B], yo * ro * gg[:, D_A + D_B:]], axis=1)
    return yn, (ra, rb, ro)


def _mixer_prm(refs, wp_scr, bias_scr):
    caw_ref, lng_ref, lnb_ref, _, _, ccw_ref, ccb_ref, clg_ref, clb_ref = refs
    wm = [wp_scr[h] for h in range(N_HEADS_B)]
    return (caw_ref[...], lng_ref[...], lnb_ref[...], wm, bias_scr[...], ccw_ref[...], ccb_ref[...], clg_ref[...],
            clb_ref[...])


def _mixer_param_specs(tt):
    return [_const_spec((8, D_A)), _const_spec((1, D_B)), _const_spec((1, D_B)), _const_spec((N_HEADS_B, tt, tt)),
            _const_spec((tt, D_B)), _const_spec((32, D_C)), _const_spec((1, D_C)), _const_spec((1, D_C)),
            _const_spec((1, D_C))]


HR_A = 8 * (K_A - 1)
HR_C = 8 * (K_C - 1)
HR_F = 8 * (K_F - 1)


def mixer_fwd(z, x, mp, perm, grp_g, wog, post_g, tt, rider=None):
    t = z.shape[0]
    assert t % tt == 0 and tt % CHUNK == 0 and tt >= HR_C, (t, tt)

    def body(z_ref, x_ref, *rest):
        prm_refs = rest[:9]
        (perm_ref, gg_ref, wo_ref, pg_ref, o_ref, x1_ref, cv_ref, pa_ext, yg_ext, pa_last, yg_last, wp_scr, bias_scr) = rest[9:]
        i = pl.program_id(0)

        @pl.when(i == 0)
        def _():
            pa_last[...] = jnp.zeros_like(pa_last)
            yg_last[...] = jnp.zeros_like(yg_last)
            _to_tile_order(perm_ref[...], prm_refs[3], wp_scr, False)
            bias_scr[...] = _dot_exact(perm_ref[...], prm_refs[4][...], ((1,), (0,)))

        zv = z_ref[...]
        prm = _mixer_prm(prm_refs, wp_scr, bias_scr)
        pa, yg, _, _ = _conv_inputs(zv)
        q = _conv_causal(pa_ext, pa, pa_last[...], prm[0], K_A, tt)
        yc = _conv_causal(yg_ext, yg, yg_last[...], prm[5], K_C, tt) + prm[6]
        pa_last[...] = pa[tt - HR_A:, :]
        yg_last[...] = yg[tt - HR_C:, :]
        cv_ref[:, 0:D_A] = q
        cv_ref[:, D_A:] = yc
        f = _mixer_forward(zv, prm, q, yc)
        yn, _ = _group_norm(f, gg_ref[...])
        o = _project_rows(yn.astype(MM_DTYPE), wo_ref)
        o_ref[...] = o
        x1_ref[...] = x_ref[...] + o * _rstd(o) * pg_ref[...]

    row = lambda c: pl.BlockSpec((tt, c), lambda i: (i, 0))
    return _pallas(
        body, rider, name="mixer_fwd", steps=t // tt,
        in_specs=[row(D_IN), row(D_MODEL)] + _mixer_param_specs(tt)
        + [_const_spec((tt, tt)), _const_spec((1, D_MODEL)), _weight_spec(wog), _const_spec((1, D_MODEL))],
        out_specs=[row(D_MODEL), row(D_MODEL), row(D_A + D_C)],
        out_shape=[jax.ShapeDtypeStruct((t, D_MODEL), F32), jax.ShapeDtypeStruct((t, D_MODEL), F32),
                   jax.ShapeDtypeStruct((t, D_A + D_C), F32)],
        scratch_shapes=[pltpu.VMEM((HR_A + tt, D_A), F32), pltpu.VMEM((HR_C + tt, D_C), F32),
                        pltpu.VMEM((HR_A, D_A), F32), pltpu.VMEM((HR_C, D_C), F32),
                        pltpu.VMEM((N_HEADS_B, tt, tt), MM_DTYPE), pltpu.VMEM((tt, D_B), F32)],
        args=(z, x, *mp, perm, grp_g, wog, post_g))


def mixer_bwd(dx1, o, z, cv, mp, perm, grp_g, wog, post_g, tt, rider=None):
    t = z.shape[0]
    assert t % tt == 0 and tt % CHUNK == 0 and tt >= HR_C, (t, tt)
    steps = t // tt

    def body(dx1_ref, o_ref, z_ref, cv_ref, *rest):
        prm_refs = rest[:9]
        (perm_ref, gg_ref, wo_ref, pg_ref,
         dz_ref, gwo_ref, dpg_ref, dgg_ref, dcaw_ref, dlng_ref, dlnb_ref, dwm_ref, dbias_ref, dccw_ref, dccb_ref,
         dclg_ref, dclb_ref,
         dq_ext, dyc_ext, dq_first, dyc_first, a_pg, a_gg, a_caw, a_lng, a_lnb, a_ccw, a_ccb, a_clg, a_clb,
         wp_scr, wpt_scr, bias_scr, a_wm, a_bias) = rest[9:]
        i = pl.program_id(0)
        small = (a_pg, a_gg, a_caw, a_lng, a_lnb, a_ccw, a_ccb, a_clg, a_clb)

        @pl.when(i == 0)
        def _():
            for ref in small + (a_wm, a_bias, dq_first, dyc_first, gwo_ref):
                ref[...] = jnp.zeros_like(ref)
            _to_tile_order(perm_ref[...], prm_refs[3], wp_scr, False)
            _to_tile_order(perm_ref[...], prm_refs[3], wpt_scr, True)
            bias_scr[...] = _dot_exact(perm_ref[...], prm_refs[4][...], ((1,), (0,)))

        prm = _mixer_prm(prm_refs, wp_scr, bias_scr)
        caw, lng, lnb, wm, bias_p, ccw, ccb, clg, clb = prm

        zv = z_ref[...]
        pa, yg, a, sg = _conv_inputs(zv)
        f = _mixer_forward(zv, prm, cv_ref[:, 0:D_A], cv_ref[:, D_A:])
        gg = gg_ref[...]
        yn, (ra, rb, ro) = _group_norm(f, gg)

        ov = o_ref[...]
        dx1v = dx1_ref[...]
        r_o = _rstd(ov)
        pg = pg_ref[...]
        a_pg[...] += _rowsum8(dx1v * ov * r_o)
        do = _rms_bwd(ov, r_o, pg, dx1v).astype(MM_DTYPE)
        gwo = _dot_tn(yn.astype(MM_DTYPE), do)
        rows = gwo_ref.shape[1]
        for j in range(N_CHIPS):
            gwo_ref[j] += gwo[rows * j:rows * (j + 1), :]
        dyn = jnp.concatenate([_dot_nt(do, wo_ref[j]) for j in range(N_CHIPS)], axis=1)

        dyn_a, dyn_b, dyn_c = dyn[:, 0:D_A], dyn[:, D_A:D_A + D_B], dyn[:, D_A + D_B:]
        ga, gb, gc = gg[:, 0:D_A], gg[:, D_A:D_A + D_B], gg[:, D_A + D_B:]
        a_gg[...] += _rowsum8(jnp.concatenate([dyn_a * f["ya"] * ra, dyn_b * f["yb"] * rb, dyn_c * f["yo"] * ro], axis=1))
        dya = _rms_bwd(f["ya"], ra, ga, dyn_a)
        dyb = _rms_bwd(f["yb"], rb, gb, dyn_b)
        dyo = _rms_bwd(f["yo"], ro, gc, dyn_c)

        dbg = dya * f["q"]
        dq = dya * f["bg"]
        dp = _conv_anticausal(dq_ext, dq, dq_first[...], caw, K_A, tt, x=pa, acc_w=a_caw)
        dq_first[...] = dq[0:HR_A, :]
        dcg = dp * zv[:, 2 * D_A:3 * D_A]
        dxa = dp * zv[:, D_A:2 * D_A]

        du = dyb * f["s"]
        ds = dyb * f["u"]
        dsb = ds.astype(MM_DTYPE)
        a_bias[...] += ds
        odd = lax.broadcasted_iota(jnp.int32, (tt, PAIR), 1) // HEAD
        for h in range(N_HEADS_B):
            dsp = dsb[:, _pair_lanes(h)]
            a_wm[h] += _dot_nt(jnp.where(odd == h % 2, dsp, jnp.zeros_like(dsp)), f["vnb"][:, _pair_lanes(h)])
        dvn = _per_head(lambda h: _dot(wpt_scr[h], dsb[:, _pair_lanes(h)]))
        a_lng[...] += _rowsum8(dvn * f["vh"])
        a_lnb[...] += _rowsum8(dvn)
        dv = _ln_bwd(f["vh"], f["rv"], dvn * lng)
        dzu = du * _gelu_grad(f["zu"])
        dzv = dv * _gelu_grad(f["zv"])

        l, sl = f["l"], f["sl"]
        dl = dyo * (sl * (1.0 + l * (1.0 - sl)))
        a_clg[...] += _rowsum8(dl * f["yh"])
        a_clb[...] += _rowsum8(dl)
        dyc = _ln_bwd(f["yh"], f["rc"], dl * clg)
        a_ccb[...] += _rowsum8(dyc)
        dy = _conv_anticausal(dyc_ext, dyc, dyc_first[...], ccw, K_C, tt, x=yg, acc_w=a_ccw)
        dyc_first[...] = dyc[0:HR_C, :]
        da = dy * sg
        dg = dy * a * sg * (1.0 - sg)

        dz_ref[...] = jnp.concatenate([dbg, dcg, dxa, dzu, dzv, da, dg], axis=1).astype(MM_DTYPE)

        @pl.when(i == steps - 1)
        def _():
            red = lambda ref: jnp.sum(ref[...], axis=0, keepdims=True)
            dpg_ref[...] = red(a_pg)
            dgg_ref[...] = red(a_gg)
            dlng_ref[...] = red(a_lng)
            dlnb_ref[...] = red(a_lnb)
            dccb_ref[...] = red(a_ccb)
            dclg_ref[...] = red(a_clg)
            dclb_ref[...] = red(a_clb)
            dcaw_ref[...] = jnp.sum(a_caw[...], axis=1)
            dccw_ref[...] = jnp.sum(a_ccw[...], axis=1)
            pm = perm_ref[...]
            tril = lax.broadcasted_iota(jnp.int32, (CHUNK, CHUNK), 0) >= lax.broadcasted_iota(jnp.int32, (CHUNK, CHUNK), 1)
            for h in range(N_HEADS_B):
                dwt = _dot_exact(pm, _dot_exact(a_wm[h], pm, ((1,), (0,))), ((0,), (0,)))
                dw = dwt[0:CHUNK, 0:CHUNK]
                for c in range(1, tt // CHUNK):
                    dw = dw + dwt[c * CHUNK:(c + 1) * CHUNK, c * CHUNK:(c + 1) * CHUNK]
                dwm_ref[h] = jnp.where(tril, dw, 0.0)
            dbt = _dot_exact(pm, a_bias[...], ((0,), (0,)))
            db = dbt[0:CHUNK, :]
            for c in range(1, tt // CHUNK):
                db = db + dbt[c * CHUNK:(c + 1) * CHUNK, :]
            lane_head = lax.broadcasted_iota(jnp.int32, (D_B, CHUNK), 0) // HEAD
            fold = (lane_head == lax.broadcasted_iota(jnp.int32, (D_B, CHUNK), 1)).astype(F32)
            dbias_ref[...] = _dot_exact(db, fold, ((1,), (0,)))

    rev = lambda c: pl.BlockSpec((tt, c), lambda i: (steps - 1 - i, 0))
    full = lambda shape: pl.BlockSpec(shape, lambda i: (0,) * len(shape))
    sds = jax.ShapeDtypeStruct
    return _pallas(
        body, rider, name="mixer_bwd", steps=steps,
        in_specs=[rev(D_MODEL), rev(D_MODEL), rev(D_IN), rev(D_A + D_C)] + _mixer_param_specs(tt)
        + [_const_spec((tt, tt)), _const_spec((1, D_MODEL)), _weight_spec(wog), _const_spec((1, D_MODEL))],
        out_specs=[rev(D_IN), full((N_CHIPS, D_MODEL // N_CHIPS, D_MODEL)), full((1, D_MODEL)), full((1, D_MODEL)), full((8, D_A)),
                   full((1, D_B)), full((1, D_B)), full((N_HEADS_B, CHUNK, CHUNK)), full((CHUNK, CHUNK)), full((32, D_C)),
                   full((1, D_C)), full((1, D_C)), full((1, D_C))],
        out_shape=[sds((t, D_IN), MM_DTYPE), sds((N_CHIPS, D_MODEL // N_CHIPS, D_MODEL), F32),
                   sds((1, D_MODEL), F32), sds((1, D_MODEL), F32), sds((8, D_A), F32), sds((1, D_B), F32), sds((1, D_B), F32),
                   sds((N_HEADS_B, CHUNK, CHUNK), F32), sds((CHUNK, CHUNK), F32), sds((32, D_C), F32), sds((1, D_C), F32),
                   sds((1, D_C), F32), sds((1, D_C), F32)],
        scratch_shapes=[pltpu.VMEM((tt + HR_A, D_A), F32), pltpu.VMEM((tt + HR_C, D_C), F32),
                        pltpu.VMEM((HR_A, D_A), F32), pltpu.VMEM((HR_C, D_C), F32),
                        pltpu.VMEM((8, D_MODEL), F32), pltpu.VMEM((8, D_MODEL), F32), pltpu.VMEM((8, 8, D_A), F32),
                        pltpu.VMEM((8, D_B), F32), pltpu.VMEM((8, D_B), F32), pltpu.VMEM((32, 8, D_C), F32),
                        pltpu.VMEM((8, D_C), F32), pltpu.VMEM((8, D_C), F32), pltpu.VMEM((8, D_C), F32),
                        pltpu.VMEM((N_HEADS_B, tt, tt), MM_DTYPE), pltpu.VMEM((N_HEADS_B, tt, tt), MM_DTYPE),
                        pltpu.VMEM((tt, D_B), F32), pltpu.VMEM((N_HEADS_B, tt, tt), F32), pltpu.VMEM((tt, D_B), F32)],
        args=(dx1, o, z, cv, *mp, perm, grp_g, wog, post_g))


def _fetch_row_blocks(wg_ref, w_scr, sems):
    r = wg_ref.shape[1]
    copies = [pltpu.make_async_copy(wg_ref.at[j], w_scr.at[pl.ds(r * j, r), :], sems.at[j]) for j in range(N_CHIPS)]
    for cp in copies:
        cp.start()
    for cp in copies:
        cp.wait()


def _ffn_conv(ext, cw, c0, cn, tt):
    acc = cw[0:1, c0:c0 + cn] * ext[0:tt, c0:c0 + cn]
    for k in range(1, K_F):
        acc = acc + cw[k:k + 1, c0:c0 + cn] * ext[8 * k:8 * k + tt, c0:c0 + cn]
    return acc


def ffn_fwd(x1, g, wug, cw, wdg, post_g, tt, rider=None):
    t, dm = x1.shape
    assert t % tt == 0, (t, tt)
    cn = _col_chunk(D_FF)
    cu = wug.shape[2]

    def body(x1_ref, g_ref, wu_ref, cw_ref, wdg_ref, pg_ref, up0_ref, h2_ref, d_ref, x2_ref, ext, last, wd_ref, sems):
        i = pl.program_id(0)

        @pl.when(i == 0)
        def _():
            _fetch_row_blocks(wdg_ref, wd_ref, sems)
            last[...] = jnp.zeros_like(last)

        xv = x1_ref[...]
        h = (xv * _rstd(xv) * g_ref[...]).astype(MM_DTYPE)
        h2_ref[...] = h
        for j in range(N_CHIPS):
            u = _dot(h, wu_ref[j])
            up0_ref[:, cu * j:cu * (j + 1)] = u
            ext[HR_F:HR_F + tt, cu * j:cu * (j + 1)] = u
        ext[0:HR_F, :] = _halo_before(ext[tt:tt + HR_F, :], last[...])
        last[...] = ext[tt:tt + HR_F, :]
        cwv = cw_ref[...]
        d = jnp.zeros((tt, D_MODEL), F32)
        for c0 in range(0, D_FF, cn):
            gate = _ffn_conv(ext, cwv, c0, cn, tt)
            val = _ffn_conv(ext, cwv, D_FF + c0, cn, tt)
            act = (gate * jax.nn.sigmoid(gate) * val).astype(MM_DTYPE)
            d = d + _dot(act, wd_ref[c0:c0 + cn, :])
        d_ref[...] = d
        x2_ref[...] = xv + d * _rstd(d) * pg_ref[...]

    row = lambda c: pl.BlockSpec((tt, c), lambda i: (i, 0))
    return _pallas(
        body, rider, name="ffn_fwd", steps=t // tt,
        in_specs=[row(dm), _const_spec((1, dm)), _weight_spec(wug), _const_spec((8, 2 * D_FF)), _ANY, _const_spec((1, dm))],
        out_specs=[row(2 * D_FF), row(dm), row(dm), row(dm)],
        out_shape=[jax.ShapeDtypeStruct((t, 2 * D_FF), F32), jax.ShapeDtypeStruct((t, dm), MM_DTYPE),
                   jax.ShapeDtypeStruct((t, dm), F32), jax.ShapeDtypeStruct((t, dm), F32)],
        scratch_shapes=[pltpu.VMEM((HR_F + tt, 2 * D_FF), F32), pltpu.VMEM((HR_F, 2 * D_FF), F32),
                        pltpu.VMEM((D_FF, D_MODEL), MM_DTYPE), pltpu.SemaphoreType.DMA((N_CHIPS,))],
        args=(x1, g, wug, cw, wdg, post_g))


def ffn_bwd(dx2, d, up0, cw, wdg, post_g, tt, rider=None):
    t = up0.shape[0]
    assert t % tt == 0, (t, tt)
    steps = t // tt
    hb = tt // HR_F
    cn = _col_chunk(D_FF)

    def body(dx2_ref, d_ref, up0_ref, uh_ref, cw_ref, wdg_ref, pg_ref,
             dd_ref, act_ref, dup0_ref, dpg_ref, dcw_ref, ext, dup_ext, first, a_pg, a_cw, wd_ref, sems):
        i = pl.program_id(0)
        tile = steps - 1 - i

        @pl.when(i == 0)
        def _():
            _fetch_row_blocks(wdg_ref, wd_ref, sems)
            a_pg[...] = jnp.zeros_like(a_pg)
            a_cw[...] = jnp.zeros_like(a_cw)
            first[...] = jnp.zeros_like(first)

        ext[HR_F:HR_F + tt, :] = up0_ref[...]
        ext[0:HR_F, :] = _halo_before(up0_ref[tt - HR_F:, :], jnp.where(tile > 0, uh_ref[...], 0.0))
        cwv = cw_ref[...]
        dv = d_ref[...]
        dx2v = dx2_ref[...]
        r = _rstd(dv)
        a_pg[...] += _rowsum8(dx2v * dv * r)
        dd = _rms_bwd(dv, r, pg_ref[...], dx2v).astype(MM_DTYPE)
        dd_ref[...] = dd
        for c0 in range(0, D_FF, cn):
            gate = _ffn_conv(ext, cwv, c0, cn, tt)
            val = _ffn_conv(ext, cwv, D_FF + c0, cn, tt)
            sg = jax.nn.sigmoid(gate)
            sl = gate * sg
            act_ref[:, c0:c0 + cn] = (sl * val).astype(MM_DTYPE)
            da = _dot_nt(dd, wd_ref[c0:c0 + cn, :])
            dup_ext[0:tt, c0:c0 + cn] = da * val * (sg * (1.0 + gate * (1.0 - sg)))
            dup_ext[0:tt, D_FF + c0:D_FF + c0 + cn] = da * sl
        dup_ext[tt:tt + HR_F, :] = _halo_after(dup_ext[0:HR_F, :], first[...])
        first[...] = dup_ext[0:HR_F, :]
        for c0 in range(0, 2 * D_FF, cn):
            x = up0_ref[:, c0:c0 + cn]
            acc = None
            for k in range(K_F):
                off = 8 * (K_F - 1 - k)
                ld = dup_ext[off:off + tt, c0:c0 + cn]
                term = cwv[k:k + 1, c0:c0 + cn] * ld
                acc = term if acc is None else acc + term
                a_cw[k, :, c0:c0 + cn] += _rowsum8(ld * x)
            dup0_ref[:, c0:c0 + cn] = acc.astype(MM_DTYPE)

        @pl.when(i == steps - 1)
        def _():
            dpg_ref[...] = jnp.sum(a_pg[...], axis=0, keepdims=True)
            dcw_ref[...] = jnp.sum(a_cw[...], axis=1)

    rev = lambda c: pl.BlockSpec((tt, c), lambda i: (steps - 1 - i, 0))
    halo = pl.BlockSpec((HR_F, 2 * D_FF), lambda i: (jnp.maximum((steps - 1 - i) * hb - 1, 0), 0))
    full = lambda shape: pl.BlockSpec(shape, lambda i: (0,) * len(shape))
    sds = jax.ShapeDtypeStruct
    return _pallas(
        body, rider, name="ffn_bwd", steps=steps,
        in_specs=[rev(D_MODEL), rev(D_MODEL), rev(2 * D_FF), halo, _const_spec((8, 2 * D_FF)), _ANY,
                  _const_spec((1, D_MODEL))],
        out_specs=[rev(D_MODEL), rev(D_FF), rev(2 * D_FF), full((1, D_MODEL)), full((8, 2 * D_FF))],
        out_shape=[sds((t, D_MODEL), MM_DTYPE), sds((t, D_FF), MM_DTYPE), sds((t, 2 * D_FF), MM_DTYPE),
                   sds((1, D_MODEL), F32), sds((8, 2 * D_FF), F32)],
        scratch_shapes=[pltpu.VMEM((HR_F + tt, 2 * D_FF), F32), pltpu.VMEM((tt + HR_F, 2 * D_FF), F32),
                        pltpu.VMEM((HR_F, 2 * D_FF), F32), pltpu.VMEM((8, D_MODEL), F32),
                        pltpu.VMEM((8, 8, 2 * D_FF), F32), pltpu.VMEM((D_FF, D_MODEL), MM_DTYPE),
                        pltpu.SemaphoreType.DMA((N_CHIPS,))],
        args=(dx2, d, up0, up0, cw, wdg, post_g))


def loss_head(y, target, tm):
    t, d = y.shape
    assert t % tm == 0, (t, tm)
    steps = t // tm

    def body(y_ref, t_ref, dy_ref, loss_ref, acc):
        i = pl.program_id(0)

        @pl.when(i == 0)
        def _():
            acc[...] = jnp.zeros_like(acc)

        diff = y_ref[...] - t_ref[...]
        dy_ref[...] = diff * (1.0 / d)
        acc[...] += _rowsum8(diff * diff)

        @pl.when(i == steps - 1)
        def _():
            loss_ref[...] = (0.5 / d) * jnp.sum(jnp.sum(acc[...], axis=0, keepdims=True), axis=1, keepdims=True)

    row = pl.BlockSpec((tm, d), lambda i: (i, 0))
    return pl.pallas_call(
        body, name="loss_head", grid=(steps,), in_specs=[row, row],
        out_specs=[row, pl.BlockSpec((1, 1), lambda i: (0, 0))],
        out_shape=[jax.ShapeDtypeStruct((t, d), F32), jax.ShapeDtypeStruct((1, 1), F32)],
        scratch_shapes=[pltpu.VMEM((8, d), F32)],
        compiler_params=_params(("arbitrary",)),
    )(y, target)


def adamw(w, g, m, v):
    shape = w.shape
    cols = shape[-1]
    rows = w.size // cols
    tr = next((r for r in (512, 256, 128) if rows % r == 0 and rows > r), rows)
    c1 = 1.0 - ADAM_B1 ** ADAM_STEP
    c2 = 1.0 - ADAM_B2 ** ADAM_STEP

    def body(w_ref, g_ref, m_ref, v_ref, d_ref, nm_ref, nv_ref):
        gv = g_ref[...]
        nm = ADAM_B1 * m_ref[...] + (1.0 - ADAM_B1) * gv
        nv = ADAM_B2 * v_ref[...] + (1.0 - ADAM_B2) * (gv * gv)
        nm_ref[...] = nm
        nv_ref[...] = nv
        d_ref[...] = -ADAM_LR * ((nm / c1) / (jnp.sqrt(nv / c2) + ADAM_EPS) + ADAM_WD * w_ref[...])

    spec = pl.BlockSpec((tr, cols), lambda i: (i, 0))
    out = jax.ShapeDtypeStruct((rows, cols), F32)
    res = pl.pallas_call(
        body, name="adamw", grid=(rows // tr,), in_specs=[spec] * 4, out_specs=[spec] * 3, out_shape=[out] * 3,
        compiler_params=_params(("arbitrary",)),
    )(*[a.reshape(rows, cols) for a in (w, g, m, v)])
    return tuple(r.reshape(shape) for r in res)


def _place():
    return lax.axis_index("x"), lax.axis_index("y"), lax.axis_index("c")


def _other_chips(x, y):
    return [(1 - x, y, 2 * (1 - x) + y), (x, 1 - y, 2 * x + 1 - y), (1 - x, 1 - y, 2 * (1 - x) + 1 - y)]


def _sem_specs(*counts):
    return [pltpu.SemaphoreType.DMA((n,)) for n in counts]


def cast_shard(w, layer, chip):
    _, r, c = w.shape

    def body(chip_ref, w_ref, o_ref):
        del chip_ref
        o_ref[...] = w_ref[...].astype(MM_DTYPE)

    grid_spec = pltpu.PrefetchScalarGridSpec(
        num_scalar_prefetch=1, grid=(1,), in_specs=[pl.BlockSpec((None, r, c), lambda i, chip_ref: (layer, 0, 0))],
        out_specs=pl.BlockSpec((None, r, c), lambda i, chip_ref: (chip_ref[0], 0, 0)))
    return pl.pallas_call(
        body, name="cast_shard", grid_spec=grid_spec, out_shape=jax.ShapeDtypeStruct((N_CHIPS, r, c), MM_DTYPE),
        compiler_params=_params(("arbitrary",)),
    )(jnp.reshape(chip, (1,)).astype(jnp.int32), w)


def _row_half(buf, chip, mine, c):
    rh = buf.shape[1] // 2
    return buf.at[chip, pl.ds(pl.multiple_of((c if mine else 1 - c) * rh, 16), rh), :]


def spread_rider(bufs):
    n = len(bufs)

    def start(rin, rout, sems):
        x, y, c = _place()
        me = 2 * x + y
        for k, (px, py, _) in enumerate(_other_chips(x, y)):
            for i, buf in enumerate(rout):
                part = _row_half(buf, me, True, c)
                pltpu.make_async_remote_copy(
                    src_ref=part, dst_ref=part, send_sem=sems[0].at[n * k + i], recv_sem=sems[1].at[n * k + i],
                    device_id=(px, py, c), device_id_type=MESH_ID).start()

    def wait(rin, rout, sems):
        x, y, c = _place()
        for k, (_, _, pj) in enumerate(_other_chips(x, y)):
            for i, buf in enumerate(rout):
                part = _row_half(buf, pj, True, c)
                pltpu.make_async_remote_copy(
                    src_ref=part, dst_ref=part, send_sem=sems[0].at[n * k + i], recv_sem=sems[1].at[n * k + i],
                    device_id=(x, y, c), device_id_type=MESH_ID).wait()

    shapes = [jax.ShapeDtypeStruct(b.shape, b.dtype) for b in bufs]
    return Rider("spread", list(bufs), shapes, {i: i for i in range(n)}, (3 * n, 3 * n), start, wait)


def pass_rider(bufs):
    n = len(bufs)

    def start(rin, rout, sems):
        x, y, c = _place()
        for k, (_, _, pj) in enumerate(_other_chips(x, y)):
            for i, buf in enumerate(rout):
                part = _row_half(buf, pj, True, c)
                pltpu.make_async_remote_copy(
                    src_ref=part, dst_ref=part, send_sem=sems[0].at[n * k + i], recv_sem=sems[1].at[n * k + i],
                    device_id=(x, y, 1 - c), device_id_type=MESH_ID).start()

    def wait(rin, rout, sems):
        x, y, c = _place()
        for k, (_, _, pj) in enumerate(_other_chips(x, y)):
            for i, buf in enumerate(rout):
                part = _row_half(buf, pj, False, c)
                pltpu.make_async_remote_copy(
                    src_ref=part, dst_ref=part, send_sem=sems[0].at[n * k + i], recv_sem=sems[1].at[n * k + i],
                    device_id=(x, y, 1 - c), device_id_type=MESH_ID).wait()

    shapes = [jax.ShapeDtypeStruct(b.shape, b.dtype) for b in bufs]
    return Rider("pass", list(bufs), shapes, {i: i for i in range(n)}, (3 * n, 3 * n), start, wait)


def both_riders(a, b):
    na, oa, sa = len(a.inputs), len(a.out_shapes), len(a.sems)

    def start(rin, rout, sems):
        a.start(rin[:na], rout[:oa], sems[:sa])
        b.start(rin[na:], rout[oa:], sems[sa:])

    def wait(rin, rout, sems):
        a.wait(rin[:na], rout[:oa], sems[:sa])
        b.wait(rin[na:], rout[oa:], sems[sa:])

    aliases = dict(a.aliases)
    aliases.update({na + i: oa + o for i, o in b.aliases.items()})
    return Rider(a.name + "_" + b.name, a.inputs + b.inputs, a.out_shapes + b.out_shapes, aliases, a.sems + b.sems,
                 start, wait)


def gather_small(small):
    def body(small_ref, out_ref, send, recv, local):
        x, y, c = _place()
        me = 2 * x + y
        chips = _other_chips(x, y)
        own = pltpu.make_async_copy(small_ref, out_ref.at[me], local.at[0])
        own.start()
        sends = [pltpu.make_async_remote_copy(src_ref=small_ref, dst_ref=out_ref.at[me], send_sem=send.at[k],
                                              recv_sem=recv.at[k], device_id=(px, py, c), device_id_type=MESH_ID)
                 for k, (px, py, _) in enumerate(chips)]
        for cp in sends:
            cp.start()
        for k, (_, _, pj) in enumerate(chips):
            pltpu.make_async_remote_copy(src_ref=small_ref, dst_ref=out_ref.at[pj], send_sem=send.at[k], recv_sem=recv.at[k],
                                         device_id=(x, y, c), device_id_type=MESH_ID).wait_recv()
        for cp in sends:
            cp.wait_send()
        own.wait()

    return pl.pallas_call(
        body, name="gather_small", in_specs=[_ANY], out_specs=_ANY,
        out_shape=jax.ShapeDtypeStruct((N_CHIPS,) + small.shape, small.dtype), scratch_shapes=_sem_specs(3, 3, 1),
        compiler_params=pltpu.CompilerParams(has_side_effects=True),
    )(small)


def swap_rider(gs):
    n = len(gs)

    def copies(rin, rout, sems):
        x, y, c = _place()
        out = []
        for i, (g, got) in enumerate(zip(rin, rout)):
            rh = g.shape[1] // 2
            theirs = pl.ds(pl.multiple_of((1 - c) * rh, 8), rh)
            out.append(pltpu.make_async_remote_copy(
                src_ref=g.at[:, theirs, :], dst_ref=got, send_sem=sems[0].at[i], recv_sem=sems[1].at[i],
                device_id=(x, y, 1 - c), device_id_type=MESH_ID))
        return out

    def start(rin, rout, sems):
        for cp in copies(rin, rout, sems):
            cp.start()

    def wait(rin, rout, sems):
        for cp in copies(rin, rout, sems):
            cp.wait()

    shapes = [jax.ShapeDtypeStruct((g.shape[0], g.shape[1] // 2, g.shape[2]), g.dtype) for g in gs]
    return Rider("swap", list(gs), shapes, {}, (n, n), start, wait)


def scatter_rider(sbs):
    n = len(sbs)

    def start(rin, rout, sems):
        x, y, c = _place()
        me = 2 * x + y
        for k, (px, py, pj) in enumerate(_other_chips(x, y)):
            for i, (sb, got) in enumerate(zip(rin, rout)):
                pltpu.make_async_remote_copy(
                    src_ref=sb.at[pj], dst_ref=got.at[me], send_sem=sems[0].at[n * k + i], recv_sem=sems[1].at[n * k + i],
                    device_id=(px, py, c), device_id_type=MESH_ID).start()

    def wait(rin, rout, sems):
        x, y, c = _place()
        for k, (_, _, pj) in enumerate(_other_chips(x, y)):
            for i, (sb, got) in enumerate(zip(rin, rout)):
                cp = pltpu.make_async_remote_copy(
                    src_ref=sb.at[pj], dst_ref=got.at[pj], send_sem=sems[0].at[n * k + i], recv_sem=sems[1].at[n * k + i],
                    device_id=(x, y, c), device_id_type=MESH_ID)
                cp.wait_recv()
                cp.wait_send()

    shapes = [jax.ShapeDtypeStruct(sb.shape, sb.dtype) for sb in sbs]
    return Rider("scatter", list(sbs), shapes, {}, (3 * n, 3 * n), start, wait)


def join_rider(fs, layers):
    n = len(fs)

    def half(i, f, mine, place):
        x, y, c = place
        rh = f.shape[1] // 2
        block = 2 * x + y if layers[i] is None else layers[i]
        return f.at[block, pl.ds(pl.multiple_of((c if mine else 1 - c) * rh, 8), rh), :]

    def start(rin, rout, sems):
        x, y, c = _place()
        for i, f in enumerate(rout):
            part = half(i, f, True, (x, y, c))
            pltpu.make_async_remote_copy(
                src_ref=part, dst_ref=part, send_sem=sems[0].at[i], recv_sem=sems[1].at[i],
                device_id=(x, y, 1 - c), device_id_type=MESH_ID).start()

    def wait(rin, rout, sems):
        x, y, c = _place()
        for i, f in enumerate(rout):
            part = half(i, f, False, (x, y, c))
            pltpu.make_async_remote_copy(
                src_ref=part, dst_ref=part, send_sem=sems[0].at[i], recv_sem=sems[1].at[i],
                device_id=(x, y, 1 - c), device_id_type=MESH_ID).wait()

    shapes = [jax.ShapeDtypeStruct(f.shape, f.dtype) for f in fs]
    return Rider("join", list(fs), shapes, {i: i for i in range(n)}, (n, n), start, wait)


def add_halves(gs, gots, wire=BF16):
    m = len(gs)
    x, y, c = _place()

    def body(p_ref, *refs):
        ins, outs = refs[:2 * m], refs[2 * m:]
        for i in range(m):
            s = ins[2 * i][...] + ins[2 * i + 1][...]
            outs[2 * i][...] = s.astype(wire)

            @pl.when(pl.program_id(0) == p_ref[0])
            def _(s=s, own_ref=outs[2 * i + 1]):
                own_ref[...] = s

    in_specs, out_specs, out_shape = [], [], []
    for got in gots:
        n, rh, cols = got.shape
        blk = (None, rh, cols)
        in_specs += [pl.BlockSpec(blk, lambda j, p_ref: (j, p_ref[1], 0)), pl.BlockSpec(blk, lambda j, p_ref: (j, 0, 0))]
        out_specs += [pl.BlockSpec(blk, lambda j, p_ref: (j, 0, 0)), pl.BlockSpec((rh, cols), lambda j, p_ref: (0, 0))]
        out_shape += [jax.ShapeDtypeStruct(got.shape, wire), jax.ShapeDtypeStruct((rh, cols), F32)]
    grid_spec = pltpu.PrefetchScalarGridSpec(num_scalar_prefetch=1, grid=(N_CHIPS,), in_specs=in_specs, out_specs=out_specs)
    res = pl.pallas_call(
        body, name="add_halves", grid_spec=grid_spec, out_shape=out_shape, compiler_params=_params(("arbitrary",)),
    )(jnp.stack([2 * x + y, c]).astype(jnp.int32), *[a for pair in zip(gs, gots) for a in pair])
    return [(res[2 * i + 1], res[2 * i]) for i in range(m)]


def add_chips(owns, gots, fbufs, block=None):
    m = len(owns)
    x, y, c = _place()
    me = 2 * x + y

    def body(p_ref, *refs):
        ins, outs = refs[:5 * m], refs[5 * m:]
        for i in range(m):
            s_ref, g1_ref, g2_ref, g3_ref, _ = ins[5 * i:5 * i + 5]
            outs[i][...] = s_ref[...] + g1_ref[...].astype(F32) + g2_ref[...].astype(F32) + g3_ref[...].astype(F32)

    def other(blk, n, k):
        return pl.BlockSpec(blk, lambda i, p_ref: ((p_ref[0] + k) % n, 0, 0))

    in_specs, out_specs, args = [], [], []
    for own, got, fbuf in zip(owns, gots, fbufs):
        n, rh, cols = got.shape
        blk = (None, rh, cols)
        in_specs += [pl.BlockSpec((rh, cols), lambda i, p_ref: (0, 0)), other(blk, n, 1), other(blk, n, 2), other(blk, n, 3),
                     _ANY]
        out_specs.append(pl.BlockSpec(blk, lambda i, p_ref: (p_ref[2], p_ref[1], 0)))
        args += [own, got, got, got, fbuf]
    grid_spec = pltpu.PrefetchScalarGridSpec(num_scalar_prefetch=1, grid=(1,), in_specs=in_specs, out_specs=out_specs)
    return pl.pallas_call(
        body, name="add_chips", grid_spec=grid_spec, out_shape=[jax.ShapeDtypeStruct(f.shape, F32) for f in fbufs],
        input_output_aliases={5 * i + 5: i for i in range(m)}, compiler_params=_params(("arbitrary",)),
    )(jnp.stack([me, c, me if block is None else block]).astype(jnp.int32), *args)


def _pack(arrays, rows):
    flat = jnp.concatenate([a.reshape(-1) for a in arrays])
    return jnp.pad(flat, (0, rows * LANES - flat.size)).reshape(rows, LANES)


def _unpack(buf, shapes):
    flat = buf.reshape(-1)
    out, at = [], 0
    for s in shapes:
        n = math.prod(s)
        out.append(flat[at:at + n].reshape(s))
        at += n
    return out


CONV_SHARDS = [(DEPTH, K_A, D_A // N_CHIPS), (DEPTH, K_C, D_C // N_CHIPS), (DEPTH, K_F, 2 * D_FF // N_CHIPS)]
CONV_ROWS = 32
SMALL_ROWS = 640


def _join_cols(g):
    n, l, r, c = g.shape
    return jnp.transpose(g, (1, 2, 0, 3)).reshape(l, r, n * c)


BIG = ["w_in", "w_out", "w_up", "w_down"]
WIDE = ["w_up", "w_down"]
NARROW = ["w_in", "w_out"]
TILE_MM = 512
TILE_TN = 1024
TILE_EW = 256


def _pad_rows(a, rows):
    return jnp.pad(a, ((0, rows - a.shape[0]), (0, 0)))


def _row(a):
    return a.reshape(1, -1)


def _tile_perm(tt):
    p = lax.broadcasted_iota(jnp.int32, (tt, tt), 0)
    tok = lax.broadcasted_iota(jnp.int32, (tt, tt), 1)
    return ((tt // 8) * (p % 8) + p // 8 == tok).astype(F32)


def _layer_params(wl, tt):
    n = tt // CHUNK
    tril = jnp.tril(jnp.ones((CHUNK, CHUNK), bool))
    wm = jnp.where(tril[None], wl["sgu_w"], 0.0)
    eye = jnp.eye(n, dtype=F32)
    wt = (eye[None, :, None, :, None] * wm[:, None, :, None, :]).reshape(N_HEADS_B, tt, tt)
    bias_e = jnp.repeat(wl["sgu_b"].T, HEAD, axis=1)
    return (_pad_rows(wl["conv_a_w"], 8), _row(wl["sgu_ln_g"]), _row(wl["sgu_ln_b"]), wt.astype(MM_DTYPE),
            jnp.tile(bias_e, (n, 1)), _pad_rows(wl["conv_c_w"], 32), _row(wl["conv_c_b"]), _row(wl["conv_ln_g"]),
            _row(wl["conv_ln_b"]))


def layer_fwd(x, wl, gw, nxt=None, tm=TILE_MM, tt=TILE_EW):
    mp = _layer_params(wl, tt)
    ride = pass_rider([gw[n] for n in WIDE]) if gw.get("pass_wide") else None
    (z, h), done = norm_matmul(x, _row(wl["pre_mix_g"]), gw["w_in"], tm, rider=ride)
    gw = {n: (done[WIDE.index(n)] if ride and n in WIDE else gw[n]) for n in BIG}
    ride = spread_rider([nxt[n] for n in NARROW]) if nxt else None
    (o, x1, cv), done = mixer_fwd(z, x, mp, _tile_perm(tt), _row(wl["grp_norm_g"]), gw["w_out"], _row(wl["post_mix_g"]), tt,
                                  rider=ride)
    ride = both_riders(spread_rider([nxt[n] for n in WIDE]), pass_rider(list(done))) if nxt else None
    (up0, h2, d, x2), done = ffn_fwd(x1, _row(wl["pre_ffn_g"]), gw["w_up"], _pad_rows(wl["ffn_conv_w"], 8), gw["w_down"],
                                     _row(wl["post_ffn_g"]), tt, rider=ride)
    if nxt:
        nxt = dict(nxt, w_up=done[0], w_down=done[1], w_in=done[2], w_out=done[3], pass_wide=True)
    return x2, dict(x=x, z=z, h=h, o=o, x1=x1, up0=up0, h2=h2, d=d, cv=cv, gw=gw), nxt


def layer_bwd(dx2, wl, layer, sv, pend=None, exchange=True, tm=TILE_MM, tt=TILE_EW):
    mp = _layer_params(wl, tt)
    gw = sv["gw"]
    tk = min(TILE_TN, dx2.shape[0])
    at = {n: BIG.index(n) for n in BIG}
    g = {}
    ride = scatter_rider([sw for _, sw in pend["narrow"]]) if pend else None
    (dd, act, dup0, dpg, dcw), arrived = ffn_bwd(dx2, sv["d"], sv["up0"], _pad_rows(wl["ffn_conv_w"], 8), gw["w_down"],
                                                 _row(wl["post_ffn_g"]), tt, rider=ride)
    fbuf = list(pend["fbuf"]) if pend else grad_buffers()
    if pend:
        done = add_chips([own for own, _ in pend["narrow"]], arrived, [fbuf[at[n]] for n in NARROW], pend["layer"])
        for n, f in zip(NARROW, done):
            fbuf[at[n]] = f
    g["post_ffn_g"] = dpg[0]
    g["ffn_conv_w"] = dcw[:K_F]
    gl = {}
    gl["w_down"] = matmul_tn_down(act, dd, tk)
    gl["w_up"] = matmul_tn_cols(sv["h2"], dup0, tk)
    ride = swap_rider([gl[n] for n in WIDE]) if exchange else None
    (dx1, dg), got = matmul_nt_norm_bwd(dup0, gw["w_up"], sv["x1"], _row(wl["pre_ffn_g"]), dx2, tm, rider=ride)
    g["pre_ffn_g"] = dg[0]
    wide = add_halves([gl[n] for n in WIDE], got) if exchange else None
    ride = scatter_rider([sw for _, sw in wide]) if exchange else None
    if pend:
        ride = both_riders(join_rider(fbuf, [pend["layer"]] * len(fbuf)), ride)
    (dz, gl["w_out"], dpg, dgg, dcaw, dlng, dlnb, dwm, dbias, dccw, dccb, dclg, dclb), rode = mixer_bwd(
        dx1, sv["o"], sv["z"], sv["cv"], mp, _tile_perm(tt), _row(wl["grp_norm_g"]), gw["w_out"],
        _row(wl["post_mix_g"]), tt, rider=ride)
    if exchange:
        fbuf, arrived = (list(rode[:len(BIG)]), rode[len(BIG):]) if pend else (fbuf, rode)
        done = add_chips([own for own, _ in wide], arrived, [fbuf[at[n]] for n in WIDE], layer)
        for n, f in zip(WIDE, done):
            fbuf[at[n]] = f
    g["post_mix_g"] = dpg[0]
    g["grp_norm_g"] = dgg[0]
    g["conv_a_w"] = dcaw[:K_A]
    g["sgu_ln_g"] = dlng[0]
    g["sgu_ln_b"] = dlnb[0]
    g["sgu_w"] = dwm
    g["sgu_b"] = dbias[:, :N_HEADS_B].T
    g["conv_c_w"] = dccw[:K_C]
    g["conv_c_b"] = dccb[0]
    g["conv_ln_g"] = dclg[0]
    g["conv_ln_b"] = dclb[0]
    gl["w_in"] = matmul_tn_in(sv["h"], dz, tk)
    ride = swap_rider([gl[n] for n in NARROW]) if exchange else None
    (dx, dg), got = matmul_nt_norm_bwd(dz, gw["w_in"], sv["x"], _row(wl["pre_mix_g"]), dx1, tm, rider=ride)
    g["pre_mix_g"] = dg[0]
    if not exchange:
        return dx, g, gl
    narrow = add_halves([gl[n] for n in NARROW], got)
    return dx, g, dict(narrow=narrow, fbuf=fbuf, layer=layer)


def grad_buffers():
    return [lax.empty(s, F32) for s in ((DEPTH, D_MODEL, D_IN // N_CHIPS), (DEPTH, D_MODEL // N_CHIPS, D_MODEL),
                                        (DEPTH, D_MODEL, 2 * D_FF // N_CHIPS), (DEPTH, D_FF // N_CHIPS, D_MODEL))]


CONV = ["conv_a_w", "conv_c_w", "ffn_conv_w"]
REPL = ["pre_mix_g", "sgu_ln_g", "sgu_ln_b", "sgu_w", "sgu_b", "conv_c_b", "conv_ln_g", "conv_ln_b", "grp_norm_g",
        "post_mix_g", "pre_ffn_g", "post_ffn_g"]
WEIGHTS = ["pre_mix_g", "w_in", "conv_a_w", "sgu_ln_g", "sgu_ln_b", "sgu_w", "sgu_b", "conv_c_w", "conv_c_b", "conv_ln_g",
           "conv_ln_b", "grp_norm_g", "w_out", "post_mix_g", "pre_ffn_g", "w_up", "ffn_conv_w", "w_down", "post_ffn_g"]


def kernel(x, pre_mix_g, w_in, conv_a_w, sgu_ln_g, sgu_ln_b, sgu_w, sgu_b, conv_c_w, conv_c_b, conv_ln_g, conv_ln_b, grp_norm_g, w_out, post_mix_g, pre_ffn_g, w_up, ffn_conv_w, w_down, post_ffn_g, loss_target, m_pre_mix_g, m_w_in, m_conv_a_w, m_sgu_ln_g, m_sgu_ln_b, m_sgu_w, m_sgu_b, m_conv_c_w, m_conv_c_b, m_conv_ln_g, m_conv_ln_b, m_grp_norm_g, m_w_out, m_post_mix_g, m_pre_ffn_g, m_w_up, m_ffn_conv_w, m_w_down, m_post_ffn_g, v_pre_mix_g, v_w_in, v_conv_a_w, v_sgu_ln_g, v_sgu_ln_b, v_sgu_w, v_sgu_b, v_conv_c_w, v_conv_c_b, v_conv_ln_g, v_conv_ln_b, v_grp_norm_g, v_w_out, v_post_mix_g, v_pre_ffn_g, v_w_up, v_ffn_conv_w, v_w_down, v_post_ffn_g):
    w = dict(pre_mix_g=pre_mix_g, w_in=w_in, conv_a_w=conv_a_w, sgu_ln_g=sgu_ln_g, sgu_ln_b=sgu_ln_b, sgu_w=sgu_w, sgu_b=sgu_b,
             conv_c_w=conv_c_w, conv_c_b=conv_c_b, conv_ln_g=conv_ln_g, conv_ln_b=conv_ln_b, grp_norm_g=grp_norm_g,
             w_out=w_out, post_mix_g=post_mix_g, pre_ffn_g=pre_ffn_g, w_up=w_up, ffn_conv_w=ffn_conv_w, w_down=w_down,
             post_ffn_g=post_ffn_g)
    m = dict(pre_mix_g=m_pre_mix_g, w_in=m_w_in, conv_a_w=m_conv_a_w, sgu_ln_g=m_sgu_ln_g, sgu_ln_b=m_sgu_ln_b,
             sgu_w=m_sgu_w, sgu_b=m_sgu_b, conv_c_w=m_conv_c_w, conv_c_b=m_conv_c_b, conv_ln_g=m_conv_ln_g,
             conv_ln_b=m_conv_ln_b, grp_norm_g=m_grp_norm_g, w_out=m_w_out, post_mix_g=m_post_mix_g,
             pre_ffn_g=m_pre_ffn_g, w_up=m_w_up, ffn_conv_w=m_ffn_conv_w, w_down=m_w_down, post_ffn_g=m_post_ffn_g)
    v = dict(pre_mix_g=v_pre_mix_g, w_in=v_w_in, conv_a_w=v_conv_a_w, sgu_ln_g=v_sgu_ln_g, sgu_ln_b=v_sgu_ln_b,
             sgu_w=v_sgu_w, sgu_b=v_sgu_b, conv_c_w=v_conv_c_w, conv_c_b=v_conv_c_b, conv_ln_g=v_conv_ln_g,
             conv_ln_b=v_conv_ln_b, grp_norm_g=v_grp_norm_g, w_out=v_w_out, post_mix_g=v_post_mix_g,
             pre_ffn_g=v_pre_ffn_g, w_up=v_w_up, ffn_conv_w=v_ffn_conv_w, w_down=v_w_down, post_ffn_g=v_post_ffn_g)
    chip = 2 * lax.axis_index("x") + lax.axis_index("y")

    convs = gather_small(_pack([w[n] for n in CONV], CONV_ROWS))
    gws = [{n: cast_shard(w[n], layer, chip) for n in BIG} for layer in range(DEPTH)]
    first = run_rider(pass_rider(run_rider(spread_rider([gws[0][n] for n in BIG]))))
    gws[0] = dict(zip(BIG, first))
    cparts = [_unpack(convs[j], CONV_SHARDS) for j in range(N_CHIPS)]
    full = dict(w)
    for i, n in enumerate(CONV):
        full[n] = _join_cols(jnp.stack([p[i] for p in cparts]))

    xc = to_tiles(x[0], TILE_EW)
    saved = []
    for layer in range(DEPTH):
        nxt = gws[layer + 1] if layer + 1 < DEPTH else None
        xc, sv, nxt = layer_fwd(xc, {n: full[n][layer] for n in REPL + CONV}, gws[layer], nxt)
        if nxt:
            gws[layer + 1] = nxt
        saved.append(sv)
    dxc, loss_part = loss_head(xc, to_tiles(loss_target[0], TILE_EW), TILE_MM)
    loss = lax.psum(loss_part[0, 0], ("x", "y", "c"))
    small = [None] * DEPTH
    pend = None
    for layer in reversed(range(DEPTH)):
        dxc, small[layer], pend = layer_bwd(dxc, {n: full[n][layer] for n in REPL + CONV}, layer, saved[layer], pend)
    grads = {n: jnp.stack([small[layer][n] for layer in range(DEPTH)]) for n in REPL + CONV}

    gsmall = _pack([grads[n] for n in REPL + CONV], SMALL_ROWS).reshape(N_CHIPS, SMALL_ROWS // N_CHIPS, LANES)
    sums = pend["narrow"] + add_halves([gsmall], run_rider(swap_rider([gsmall])), wire=F32)
    arrived = run_rider(scatter_rider([sw for _, sw in sums]))
    fbuf = list(pend["fbuf"])
    at = [BIG.index(n) for n in NARROW]
    for i, f in zip(at, add_chips([own for own, _ in sums[:2]], arrived[:2], [fbuf[i] for i in at], 0)):
        fbuf[i] = f
    fbuf += add_chips([sums[2][0]], arrived[2:], [lax.empty(gsmall.shape, F32)])
    joined = run_rider(join_rider(fbuf, [0] * len(BIG) + [None]))
    out_g = dict(zip(BIG, joined))
    tot = run_rider(pass_rider(run_rider(spread_rider([joined[len(BIG)]]))))[0].reshape(SMALL_ROWS, LANES)
    shapes = [grads[n].shape for n in REPL + CONV]
    for n, gfull in zip(REPL + CONV, _unpack(tot, shapes)):
        if n in CONV:
            width = gfull.shape[-1] // N_CHIPS
            gfull = lax.dynamic_slice_in_dim(gfull, chip * width, width, axis=2)
        out_g[n] = gfull

    deltas, new_m, new_v = {}, {}, {}
    for n in WEIGHTS:
        deltas[n], new_m[n], new_v[n] = adamw(w[n], out_g[n], m[n], v[n])
    return (loss, from_tiles(dxc, TILE_EW)[None], *[out_g[n] for n in WEIGHTS], *[deltas[n] for n in WEIGHTS], *[new_m[n] for n in WEIGHTS],
            *[new_v[n] for n in WEIGHTS])
```

```python
import math
from typing import Callable, NamedTuple

import jax
import jax.numpy as jnp
from jax import lax
from jax.experimental import pallas as pl
from jax.experimental.pallas import tpu as pltpu

F32 = jnp.float32
BF16 = jnp.bfloat16
MM_DTYPE = BF16

D_MODEL = 1024
DEPTH = 4
D_A = 256
D_B = 384
D_C = 384
D_IN = 3 * D_A + 2 * D_B + 2 * D_C
D_FF = 2816
K_A = 3
K_C = 31
K_F = 3
CHUNK = 128
HEAD = 64
N_HEADS_B = D_B // HEAD
EPS = 1e-6
N_CHIPS = 4

ADAM_LR = 0.001
ADAM_B1 = 0.9
ADAM_B2 = 0.999
ADAM_EPS = 1e-08
ADAM_WD = 0.01
ADAM_STEP = 10

LANES = 1024
VMEM_LIMIT = 56 * 1024 * 1024

MESH_ID = pl.DeviceIdType.MESH
_ANY = pl.BlockSpec(memory_space=pl.ANY)


def _params(sem=None):
    return pltpu.CompilerParams(dimension_semantics=sem, vmem_limit_bytes=VMEM_LIMIT)


def _const_spec(shape):
    nd = len(shape)
    return pl.BlockSpec(shape, lambda *_: (0,) * nd, pipeline_mode=pl.Buffered(1))


def _rowsum8(a):
    r, c = a.shape
    return jnp.sum(a.reshape(r // 8, 8, c), axis=0)


def _rstd(x):
    return lax.rsqrt(jnp.mean(x * x, axis=-1, keepdims=True) + EPS)


def _rms_bwd(x, r, g, dy):
    gdy = g * dy
    return r * gdy - x * (r * r * r) * jnp.mean(gdy * x, axis=-1, keepdims=True)


def _ln_fwd(x):
    mu = jnp.mean(x, axis=-1, keepdims=True)
    xc = x - mu
    r = lax.rsqrt(jnp.mean(xc * xc, axis=-1, keepdims=True) + EPS)
    return xc * r, r


def _ln_bwd(xh, r, dxh):
    return r * (dxh - jnp.mean(dxh, axis=-1, keepdims=True) - xh * jnp.mean(dxh * xh, axis=-1, keepdims=True))


def _gelu(x):
    return 0.5 * x * (1.0 + lax.erf(x * (1.0 / math.sqrt(2.0))))


def _gelu_grad(x):
    cdf = 0.5 * (1.0 + lax.erf(x * (1.0 / math.sqrt(2.0))))
    pdf = jnp.exp(-0.5 * x * x) * (1.0 / math.sqrt(2.0 * math.pi))
    return cdf + x * pdf


def _dot(a, b):
    return jnp.dot(a, b, preferred_element_type=F32)


def _dot_nt(a, b):
    return lax.dot_general(a, b, (((1,), (1,)), ((), ())), preferred_element_type=F32)


def _dot_tn(a, b):
    return lax.dot_general(a, b, (((0,), (0,)), ((), ())), preferred_element_type=F32)


def _col_chunk(n):
    for c in (1408, 1024, 768, 512, 256, 128):
        if n % c == 0:
            return c
    raise ValueError(n)


class Rider(NamedTuple):
    name: str
    inputs: list
    out_shapes: list
    aliases: dict
    sems: tuple
    start: Callable
    wait: Callable


def _pallas(body, rider, *, name, steps, in_specs, out_specs, out_shape, scratch_shapes, args):
    if rider is None:
        res = pl.pallas_call(body, name=name, grid=(steps,), in_specs=in_specs, out_specs=out_specs, out_shape=out_shape,
                             scratch_shapes=scratch_shapes, compiler_params=_params(("arbitrary",)))(*args)
        return res, []
    n_in, n_out, n_scr = len(in_specs), len(out_specs), len(scratch_shapes)
    r_in, r_out = len(rider.inputs), len(rider.out_shapes)

    def wrapped(*refs):
        ins, rin = refs[:n_in], refs[n_in:n_in + r_in]
        at = n_in + r_in
        outs, rout = refs[at:at + n_out], refs[at + n_out:at + n_out + r_out]
        at += n_out + r_out
        scr, rsem = refs[at:at + n_scr], refs[at + n_scr:]

        @pl.when(pl.program_id(0) == 0)
        def _():
            rider.start(rin, rout, rsem)

        body(*ins, *outs, *scr)

        @pl.when(pl.program_id(0) == steps - 1)
        def _():
            rider.wait(rin, rout, rsem)

    res = pl.pallas_call(
        wrapped, name=name + "_" + rider.name, grid=(steps,), in_specs=list(in_specs) + [_ANY] * r_in,
        out_specs=list(out_specs) + [_ANY] * r_out, out_shape=list(out_shape) + list(rider.out_shapes),
        scratch_shapes=list(scratch_shapes) + [pltpu.SemaphoreType.DMA((n,)) for n in rider.sems],
        input_output_aliases={n_in + i: n_out + o for i, o in rider.aliases.items()},
        compiler_params=pltpu.CompilerParams(dimension_semantics=("arbitrary",), vmem_limit_bytes=VMEM_LIMIT,
                                             has_side_effects=True),
    )(*args, *rider.inputs)
    return res[:n_out], res[n_out:]


def run_rider(rider):
    def body(*refs):
        r_in, r_out = len(rider.inputs), len(rider.out_shapes)
        rin, rout, rsem = refs[:r_in], refs[r_in:r_in + r_out], refs[r_in + r_out:]
        rider.start(rin, rout, rsem)
        rider.wait(rin, rout, rsem)

    return pl.pallas_call(
        body, name=rider.name, in_specs=[_ANY] * len(rider.inputs), out_specs=[_ANY] * len(rider.out_shapes),
        out_shape=list(rider.out_shapes), scratch_shapes=[pltpu.SemaphoreType.DMA((n,)) for n in rider.sems],
        input_output_aliases=dict(rider.aliases), compiler_params=pltpu.CompilerParams(has_side_effects=True),
    )(*rider.inputs)


def _weight_spec(wg):
    return _const_spec(wg.shape)


def _join_col_blocks(w_ref, w_scr):
    c = w_ref.shape[2]
    for j in range(N_CHIPS):
        w_scr[:, c * j:c * (j + 1)] = w_ref[j]


def matmul_nt_norm_bwd(gy, wg, x, g, dres, tm, rider=None):
    t, n = gy.shape
    assert t % tm == 0, (t, tm)
    d, cw = wg.shape[1], wg.shape[2]
    aligned = cw % 128 == 0
    cn = cw if aligned else _col_chunk(n)
    steps = t // tm

    def body(gy_ref, w_ref, x_ref, g_ref, dres_ref, dx_ref, dg_ref, acc_ref, *scr):
        i = pl.program_id(0)

        @pl.when(i == 0)
        def _():
            acc_ref[...] = jnp.zeros_like(acc_ref)
            if not aligned:
                _join_col_blocks(w_ref, scr[0])

        dh = jnp.zeros((tm, d), F32)
        for j, c0 in enumerate(range(0, n, cn)):
            wv = w_ref[j] if aligned else scr[0][:, c0:c0 + cn]
            dh = dh + _dot_nt(gy_ref[:, c0:c0 + cn], wv)
        xv = x_ref[...]
        r = _rstd(xv)
        gv = g_ref[...]
        dx_ref[...] = dres_ref[...] + _rms_bwd(xv, r, gv, dh)
        acc_ref[...] += _rowsum8(dh * xv * r)

        @pl.when(i == steps - 1)
        def _():
            dg_ref[...] = jnp.sum(acc_ref[...], axis=0, keepdims=True)

    return _pallas(
        body, rider, name="matmul_nt_norm_bwd", steps=steps,
        in_specs=[pl.BlockSpec((tm, n), lambda i: (i, 0)), _weight_spec(wg), pl.BlockSpec((tm, d), lambda i: (i, 0)),
                  _const_spec((1, d)), pl.BlockSpec((tm, d), lambda i: (i, 0))],
        out_specs=[pl.BlockSpec((tm, d), lambda i: (i, 0)), pl.BlockSpec((1, d), lambda i: (0, 0))],
        out_shape=[jax.ShapeDtypeStruct((t, d), F32), jax.ShapeDtypeStruct((1, d), F32)],
        scratch_shapes=[pltpu.VMEM((8, d), F32)] + ([] if aligned else [pltpu.VMEM((d, n), MM_DTYPE)]),
        args=(gy, wg, x, g, dres))


def matmul_tn_cols(a, b, tk):
    t, r = a.shape
    c = b.shape[1] // N_CHIPS
    assert t % tk == 0 and r % 8 == 0 and c % 128 == 0, (a.shape, b.shape, tk)

    def body(a_ref, b_ref, o_ref):
        @pl.when(pl.program_id(1) == 0)
        def _():
            o_ref[...] = jnp.zeros_like(o_ref)

        o_ref[...] += _dot_tn(a_ref[...], b_ref[...])

    a_spec = pl.BlockSpec((tk, r), lambda j, k: (k, 0))
    b_spec = pl.BlockSpec((tk, c), lambda j, k: (k, j))
    return pl.pallas_call(
        body, name="matmul_tn_cols", grid=(N_CHIPS, t // tk), in_specs=[a_spec, b_spec],
        out_specs=pl.BlockSpec((None, r, c), lambda j, k: (j, 0, 0)),
        out_shape=jax.ShapeDtypeStruct((N_CHIPS, r, c), F32),
        compiler_params=_params(("arbitrary", "arbitrary")),
    )(a, b)


def matmul_tn_down(act, dd, tk):
    t, m = act.shape
    c = dd.shape[1]
    r = m // N_CHIPS
    assert t % tk == 0, (t, tk)
    steps = t // tk

    def body(a_ref, b_ref, o_ref, acc):
        k = pl.program_id(1)

        @pl.when(k == 0)
        def _():
            acc[...] = jnp.zeros_like(acc)

        acc[...] += _dot_tn(a_ref[...], b_ref[...])

        @pl.when(k == steps - 1)
        def _():
            o_ref[0] = acc[0:r, :]
            o_ref[1] = acc[r:2 * r, :]

    return pl.pallas_call(
        body, name="matmul_tn_down", grid=(2, steps),
        in_specs=[pl.BlockSpec((tk, 2 * r), lambda p, k: (k, p)), pl.BlockSpec((tk, c), lambda p, k: (k, 0))],
        out_specs=pl.BlockSpec((2, r, c), lambda p, k: (p, 0, 0)),
        out_shape=jax.ShapeDtypeStruct((N_CHIPS, r, c), F32),
        scratch_shapes=[pltpu.VMEM((2 * r, c), F32)],
        compiler_params=_params(("arbitrary", "arbitrary")),
    )(act, dd)


def matmul_tn_in(h, dz, tk):
    t, d = h.shape
    n = dz.shape[1]
    c = n // N_CHIPS
    assert t % tk == 0, (t, tk)
    steps = t // tk

    def body(a_ref, b_ref, o_ref, acc):
        k = pl.program_id(0)

        @pl.when(k == 0)
        def _():
            acc[...] = jnp.zeros_like(acc)

        acc[...] += _dot_tn(a_ref[...], b_ref[...])

        @pl.when(k == steps - 1)
        def _():
            for j in range(N_CHIPS):
                o_ref[j] = acc[:, c * j:c * (j + 1)]

    return pl.pallas_call(
        body, name="matmul_tn_in", grid=(steps,),
        in_specs=[pl.BlockSpec((tk, d), lambda k: (k, 0)), pl.BlockSpec((tk, n), lambda k: (k, 0))],
        out_specs=pl.BlockSpec((N_CHIPS, d, c), lambda k: (0, 0, 0)),
        out_shape=jax.ShapeDtypeStruct((N_CHIPS, d, c), F32),
        scratch_shapes=[pltpu.VMEM((d, n), F32)],
        compiler_params=_params(("arbitrary",)),
    )(h, dz)


def to_tiles(a, tt):
    t = a.shape[0]
    return a.reshape((t // tt, 8, tt // 8) + a.shape[1:]).swapaxes(1, 2).reshape(a.shape)


def from_tiles(a, tt):
    t = a.shape[0]
    return a.reshape((t // tt, tt // 8, 8) + a.shape[1:]).swapaxes(1, 2).reshape(a.shape)


def _roll_sublanes(a, shift):
    n = a.shape[0] // 8
    return pltpu.roll(a.reshape(n, 8, a.shape[1]), shift, 1).reshape(a.shape)


def _halo_before(cur_last, prev_last):
    sub = lax.broadcasted_iota(jnp.int32, cur_last.shape, 0) % 8
    return jnp.where(sub == 0, _roll_sublanes(prev_last, 1), _roll_sublanes(cur_last, 1))


def _halo_after(cur_first, next_first):
    sub = lax.broadcasted_iota(jnp.int32, cur_first.shape, 0) % 8
    return jnp.where(sub == 7, _roll_sublanes(next_first, 7), _roll_sublanes(cur_first, 7))


def _conv_causal(ext, cur, prev_last, w, taps, tt, cols=None):
    hr = 8 * (taps - 1)
    cs = slice(None) if cols is None else cols
    ext[hr:hr + tt, cs] = cur
    ext[0:hr, cs] = _halo_before(cur[tt - hr:, :], prev_last)
    acc = w[0:1, :] * ext[0:tt, cs]
    for k in range(1, taps):
        acc = acc + w[k:k + 1, :] * ext[8 * k:8 * k + tt, cs]
    return acc


def _conv_anticausal(ext, cur, next_first, w, taps, tt, x=None, acc_w=None, cols=None):
    hr = 8 * (taps - 1)
    cs = slice(None) if cols is None else cols
    ext[0:tt, cs] = cur
    ext[tt:tt + hr, cs] = _halo_after(cur[0:hr, :], next_first)
    acc = None
    for k in range(taps):
        off = 8 * (taps - 1 - k)
        ld = ext[off:off + tt, cs]
        term = w[k:k + 1, :] * ld
        acc = term if acc is None else acc + term
        if x is not None:
            acc_w[k, :, cs] += _rowsum8(ld * x)
    return acc


def _dot_exact(a, b, dims):
    return lax.dot_general(a, b, (dims, ((), ())), precision=lax.Precision.HIGHEST, preferred_element_type=F32)


def _to_tile_order(perm, wt_ref, w_scr, transpose):
    pb = perm.astype(MM_DTYPE)
    for h in range(N_HEADS_B):
        half = (_dot_nt(pb, wt_ref[h]) if transpose else _dot(pb, wt_ref[h])).astype(MM_DTYPE)
        w_scr[h] = _dot_nt(half, pb).astype(MM_DTYPE)


def _project_rows(y, w_ref):
    r = w_ref.shape[1]
    acc = _dot(y[:, 0:r], w_ref[0])
    for j in range(1, N_CHIPS):
        acc = acc + _dot(y[:, r * j:r * (j + 1)], w_ref[j])
    return acc


PAIR = 2 * HEAD


def _pair_lanes(h):
    return slice(PAIR * (h // 2), PAIR * (h // 2 + 1))


def _per_head(fn):
    cols = []
    for p in range(N_HEADS_B // 2):
        lo, hi = fn(2 * p), fn(2 * p + 1)
        cols.append(jnp.where(lax.broadcasted_iota(jnp.int32, lo.shape, 1) < HEAD, lo, hi))
    return jnp.concatenate(cols, axis=1)


def _mixer_forward(z, prm, q, yc):
    _, lng, lnb, wm, bias_p, _, _, clg, clb = prm
    bg = z[:, 0:D_A]
    ya = bg * q
    o_b = 3 * D_A
    zu = z[:, o_b:o_b + D_B]
    zv = z[:, o_b + D_B:o_b + 2 * D_B]
    u = _gelu(zu)
    vh, rv = _ln_fwd(_gelu(zv))
    vnb = (vh * lng + lnb).astype(MM_DTYPE)
    s = _per_head(lambda h: _dot(wm[h], vnb[:, _pair_lanes(h)])) + bias_p
    yb = u * s
    yh, rc = _ln_fwd(yc)
    l = yh * clg + clb
    sl = jax.nn.sigmoid(l)
    return dict(bg=bg, q=q, ya=ya, zu=zu, zv=zv, u=u, vh=vh, rv=rv, vnb=vnb, s=s, yb=yb, yh=yh, rc=rc, l=l, sl=sl,
                yo=l * sl)


def _conv_inputs(z):
    o_c = 3 * D_A + 2 * D_B
    a = z[:, o_c:o_c + D_C]
    sg = jax.nn.sigmoid(z[:, o_c + D_C:o_c + 2 * D_C])
    return z[:, D_A:2 * D_A] * z[:, 2 * D_A:3 * D_A], a * sg, a, sg


def _group_norm(f, gg):
    ya, yb, yo = f["ya"], f["yb"], f["yo"]
    ra, rb, ro = _rstd(ya), _rstd(yb), _rstd(yo)
    yn = jnp.concatenate([ya * ra * gg[:, 0:D_A], yb * rb * gg[:, D_A:D_A + D_B], yo * ro * gg[:, D_A + D_B:]], axis=1)
    return yn, (ra, rb, ro)


def _mixer_prm(refs, wp_scr, bias_scr):
    caw_ref, lng_ref, lnb_ref, _, _, ccw_ref, ccb_ref, clg_ref, clb_ref = refs
    wm = [wp_scr[h] for h in range(N_HEADS_B)]
    return (caw_ref[...], lng_ref[...], lnb_ref[...], wm, bias_scr[...], ccw_ref[...], ccb_ref[...], clg_ref[...],
            clb_ref[...])


def _mixer_param_specs(tt):
    return [_const_spec((8, D_A)), _const_spec((1, D_B)), _const_spec((1, D_B)), _const_spec((N_HEADS_B, tt, tt)),
            _const_spec((tt, D_B)), _const_spec((32, D_C)), _const_spec((1, D_C)), _const_spec((1, D_C)),
            _const_spec((1, D_C))]


HR_A = 8 * (K_A - 1)
HR_C = 8 * (K_C - 1)
HR_F = 8 * (K_F - 1)


def mixer_fwd(x, g, wig, mp, perm, grp_g, wog, post_g, tt, rider=None):
    t = x.shape[0]
    assert t % tt == 0 and tt % CHUNK == 0 and tt >= HR_C, (t, tt)
    cn = _col_chunk(D_IN)

    def body(x_ref, g_ref, wi_ref, *rest):
        prm_refs = rest[:9]
        (perm_ref, gg_ref, wo_ref, pg_ref, z_ref, h_ref, o_ref, x1_ref, cv_ref, pa_ext, yg_ext, pa_last, yg_last, wp_scr,
         bias_scr, wi_scr) = rest[9:]
        i = pl.program_id(0)

        @pl.when(i == 0)
        def _():
            pa_last[...] = jnp.zeros_like(pa_last)
            yg_last[...] = jnp.zeros_like(yg_last)
            _to_tile_order(perm_ref[...], prm_refs[3], wp_scr, False)
            bias_scr[...] = _dot_exact(perm_ref[...], prm_refs[4][...], ((1,), (0,)))
            _join_col_blocks(wi_ref, wi_scr)

        xv = x_ref[...]
        h = (xv * _rstd(xv) * g_ref[...]).astype(MM_DTYPE)
        h_ref[...] = h
        for c0 in range(0, D_IN, cn):
            z_ref[:, c0:c0 + cn] = _dot(h, wi_scr[:, c0:c0 + cn])
        zv = z_ref[...]
        prm = _mixer_prm(prm_refs, wp_scr, bias_scr)
        pa, yg, _, _ = _conv_inputs(zv)
        q = _conv_causal(pa_ext, pa, pa_last[...], prm[0], K_A, tt)
        yc = _conv_causal(yg_ext, yg, yg_last[...], prm[5], K_C, tt) + prm[6]
        pa_last[...] = pa[tt - HR_A:, :]
        yg_last[...] = yg[tt - HR_C:, :]
        cv_ref[:, 0:D_A] = q
        cv_ref[:, D_A:] = yc
        f = _mixer_forward(zv, prm, q, yc)
        yn, _ = _group_norm(f, gg_ref[...])
        o = _project_rows(yn.astype(MM_DTYPE), wo_ref)
        o_ref[...] = o
        x1_ref[...] = xv + o * _rstd(o) * pg_ref[...]

    row = lambda c: pl.BlockSpec((tt, c), lambda i: (i, 0))
    return _pallas(
        body, rider, name="mixer_fwd", steps=t // tt,
        in_specs=[row(D_MODEL), _const_spec((1, D_MODEL)), _weight_spec(wig)] + _mixer_param_specs(tt)
        + [_const_spec((tt, tt)), _const_spec((1, D_MODEL)), _weight_spec(wog), _const_spec((1, D_MODEL))],
        out_specs=[row(D_IN), row(D_MODEL), row(D_MODEL), row(D_MODEL), row(D_A + D_C)],
        out_shape=[jax.ShapeDtypeStruct((t, D_IN), F32), jax.ShapeDtypeStruct((t, D_MODEL), MM_DTYPE),
                   jax.ShapeDtypeStruct((t, D_MODEL), F32), jax.ShapeDtypeStruct((t, D_MODEL), F32),
                   jax.ShapeDtypeStruct((t, D_A + D_C), F32)],
        scratch_shapes=[pltpu.VMEM((HR_A + tt, D_A), F32), pltpu.VMEM((HR_C + tt, D_C), F32),
                        pltpu.VMEM((HR_A, D_A), F32), pltpu.VMEM((HR_C, D_C), F32),
                        pltpu.VMEM((N_HEADS_B, tt, tt), MM_DTYPE), pltpu.VMEM((tt, D_B), F32),
                        pltpu.VMEM((D_MODEL, D_IN), MM_DTYPE)],
        args=(x, g, wig, *mp, perm, grp_g, wog, post_g))


def mixer_bwd(dx1, o, z, cv, mp, perm, grp_g, wog, post_g, tt, rider=None):
    t = z.shape[0]
    assert t % tt == 0 and tt % CHUNK == 0 and tt >= HR_C, (t, tt)
    steps = t // tt

    def body(dx1_ref, o_ref, z_ref, cv_ref, *rest):
        prm_refs = rest[:9]
        (perm_ref, gg_ref, wo_ref, pg_ref,
         dz_ref, gwo_ref, dpg_ref, dgg_ref, dcaw_ref, dlng_ref, dlnb_ref, dwm_ref, dbias_ref, dccw_ref, dccb_ref,
         dclg_ref, dclb_ref,
         dq_ext, dyc_ext, dq_first, dyc_first, a_pg, a_gg, a_caw, a_lng, a_lnb, a_ccw, a_ccb, a_clg, a_clb,
         wp_scr, wpt_scr, bias_scr, a_wm, a_bias) = rest[9:]
        i = pl.program_id(0)
        small = (a_pg, a_gg, a_caw, a_lng, a_lnb, a_ccw, a_ccb, a_clg, a_clb)

        @pl.when(i == 0)
        def _():
            for ref in small + (a_wm, a_bias, dq_first, dyc_first, gwo_ref):
                ref[...] = jnp.zeros_like(ref)
            _to_tile_order(perm_ref[...], prm_refs[3], wp_scr, False)
            _to_tile_order(perm_ref[...], prm_refs[3], wpt_scr, True)
            bias_scr[...] = _dot_exact(perm_ref[...], prm_refs[4][...], ((1,), (0,)))

        prm = _mixer_prm(prm_refs, wp_scr, bias_scr)
        caw, lng, lnb, wm, bias_p, ccw, ccb, clg, clb = prm

        zv = z_ref[...]
        pa, yg, a, sg = _conv_inputs(zv)
        f = _mixer_forward(zv, prm, cv_ref[:, 0:D_A], cv_ref[:, D_A:])
        gg = gg_ref[...]
        yn, (ra, rb, ro) = _group_norm(f, gg)

        ov = o_ref[...]
        dx1v = dx1_ref[...]
        r_o = _rstd(ov)
        pg = pg_ref[...]
        a_pg[...] += _rowsum8(dx1v * ov * r_o)
        do = _rms_bwd(ov, r_o, pg, dx1v).astype(MM_DTYPE)
        gwo = _dot_tn(yn.astype(MM_DTYPE), do)
        rows = gwo_ref.shape[1]
        for j in range(N_CHIPS):
            gwo_ref[j] += gwo[rows * j:rows * (j + 1), :]
        dyn = jnp.concatenate([_dot_nt(do, wo_ref[j]) for j in range(N_CHIPS)], axis=1)

        dyn_a, dyn_b, dyn_c = dyn[:, 0:D_A], dyn[:, D_A:D_A + D_B], dyn[:, D_A + D_B:]
        ga, gb, gc = gg[:, 0:D_A], gg[:, D_A:D_A + D_B], gg[:, D_A + D_B:]
        a_gg[...] += _rowsum8(jnp.concatenate([dyn_a * f["ya"] * ra, dyn_b * f["yb"] * rb, dyn_c * f["yo"] * ro], axis=1))
        dya = _rms_bwd(f["ya"], ra, ga, dyn_a)
        dyb = _rms_bwd(f["yb"], rb, gb, dyn_b)
        dyo = _rms_bwd(f["yo"], ro, gc, dyn_c)

        dbg = dya * f["q"]
        dq = dya * f["bg"]
        dp = _conv_anticausal(dq_ext, dq, dq_first[...], caw, K_A, tt, x=pa, acc_w=a_caw)
        dq_first[...] = dq[0:HR_A, :]
        dcg = dp * zv[:, 2 * D_A:3 * D_A]
        dxa = dp * zv[:, D_A:2 * D_A]

        du = dyb * f["s"]
        ds = dyb * f["u"]
        dsb = ds.astype(MM_DTYPE)
        a_bias[...] += ds
        odd = lax.broadcasted_iota(jnp.int32, (tt, PAIR), 1) // HEAD
        for h in range(N_HEADS_B):
            dsp = dsb[:, _pair_lanes(h)]
            a_wm[h] += _dot_nt(jnp.where(odd == h % 2, dsp, jnp.zeros_like(dsp)), f["vnb"][:, _pair_lanes(h)])
        dvn = _per_head(lambda h: _dot(wpt_scr[h], dsb[:, _pair_lanes(h)]))
        a_lng[...] += _rowsum8(dvn * f["vh"])
        a_lnb[...] += _rowsum8(dvn)
        dv = _ln_bwd(f["vh"], f["rv"], dvn * lng)
        dzu = du * _gelu_grad(f["zu"])
        dzv = dv * _gelu_grad(f["zv"])

        l, sl = f["l"], f["sl"]
        dl = dyo * (sl * (1.0 + l * (1.0 - sl)))
        a_clg[...] += _rowsum8(dl * f["yh"])
        a_clb[...] += _rowsum8(dl)
        dyc = _ln_bwd(f["yh"], f["rc"], dl * clg)
        a_ccb[...] += _rowsum8(dyc)
        dy = _conv_anticausal(dyc_ext, dyc, dyc_first[...], ccw, K_C, tt, x=yg, acc_w=a_ccw)
        dyc_first[...] = dyc[0:HR_C, :]
        da = dy * sg
        dg = dy * a * sg * (1.0 - sg)

        dz_ref[...] = jnp.concatenate([dbg, dcg, dxa, dzu, dzv, da, dg], axis=1).astype(MM_DTYPE)

        @pl.when(i == steps - 1)
        def _():
            red = lambda ref: jnp.sum(ref[...], axis=0, keepdims=True)
            dpg_ref[...] = red(a_pg)
            dgg_ref[...] = red(a_gg)
            dlng_ref[...] = red(a_lng)
            dlnb_ref[...] = red(a_lnb)
            dccb_ref[...] = red(a_ccb)
            dclg_ref[...] = red(a_clg)
            dclb_ref[...] = red(a_clb)
            dcaw_ref[...] = jnp.sum(a_caw[...], axis=1)
            dccw_ref[...] = jnp.sum(a_ccw[...], axis=1)
            pm = perm_ref[...]
            tril = lax.broadcasted_iota(jnp.int32, (CHUNK, CHUNK), 0) >= lax.broadcasted_iota(jnp.int32, (CHUNK, CHUNK), 1)
            for h in range(N_HEADS_B):
                dwt = _dot_exact(pm, _dot_exact(a_wm[h], pm, ((1,), (0,))), ((0,), (0,)))
                dw = dwt[0:CHUNK, 0:CHUNK]
                for c in range(1, tt // CHUNK):
                    dw = dw + dwt[c * CHUNK:(c + 1) * CHUNK, c * CHUNK:(c + 1) * CHUNK]
                dwm_ref[h] = jnp.where(tril, dw, 0.0)
            dbt = _dot_exact(pm, a_bias[...], ((0,), (0,)))
            db = dbt[0:CHUNK, :]
            for c in range(1, tt // CHUNK):
                db = db + dbt[c * CHUNK:(c + 1) * CHUNK, :]
            lane_head = lax.broadcasted_iota(jnp.int32, (D_B, CHUNK), 0) // HEAD
            fold = (lane_head == lax.broadcasted_iota(jnp.int32, (D_B, CHUNK), 1)).astype(F32)
            dbias_ref[...] = _dot_exact(db, fold, ((1,), (0,)))

    rev = lambda c: pl.BlockSpec((tt, c), lambda i: (steps - 1 - i, 0))
    full = lambda shape: pl.BlockSpec(shape, lambda i: (0,) * len(shape))
    sds = jax.ShapeDtypeStruct
    return _pallas(
        body, rider, name="mixer_bwd", steps=steps,
        in_specs=[rev(D_MODEL), rev(D_MODEL), rev(D_IN), rev(D_A + D_C)] + _mixer_param_specs(tt)
        + [_const_spec((tt, tt)), _const_spec((1, D_MODEL)), _weight_spec(wog), _const_spec((1, D_MODEL))],
        out_specs=[rev(D_IN), full((N_CHIPS, D_MODEL // N_CHIPS, D_MODEL)), full((1, D_MODEL)), full((1, D_MODEL)), full((8, D_A)),
                   full((1, D_B)), full((1, D_B)), full((N_HEADS_B, CHUNK, CHUNK)), full((CHUNK, CHUNK)), full((32, D_C)),
                   full((1, D_C)), full((1, D_C)), full((1, D_C))],
        out_shape=[sds((t, D_IN), MM_DTYPE), sds((N_CHIPS, D_MODEL // N_CHIPS, D_MODEL), F32),
                   sds((1, D_MODEL), F32), sds((1, D_MODEL), F32), sds((8, D_A), F32), sds((1, D_B), F32), sds((1, D_B), F32),
                   sds((N_HEADS_B, CHUNK, CHUNK), F32), sds((CHUNK, CHUNK), F32), sds((32, D_C), F32), sds((1, D_C), F32),
                   sds((1, D_C), F32), sds((1, D_C), F32)],
        scratch_shapes=[pltpu.VMEM((tt + HR_A, D_A), F32), pltpu.VMEM((tt + HR_C, D_C), F32),
                        pltpu.VMEM((HR_A, D_A), F32), pltpu.VMEM((HR_C, D_C), F32),
                        pltpu.VMEM((8, D_MODEL), F32), pltpu.VMEM((8, D_MODEL), F32), pltpu.VMEM((8, 8, D_A), F32),
                        pltpu.VMEM((8, D_B), F32), pltpu.VMEM((8, D_B), F32), pltpu.VMEM((32, 8, D_C), F32),
                        pltpu.VMEM((8, D_C), F32), pltpu.VMEM((8, D_C), F32), pltpu.VMEM((8, D_C), F32),
                        pltpu.VMEM((N_HEADS_B, tt, tt), MM_DTYPE), pltpu.VMEM((N_HEADS_B, tt, tt), MM_DTYPE),
                        pltpu.VMEM((tt, D_B), F32), pltpu.VMEM((N_HEADS_B, tt, tt), F32), pltpu.VMEM((tt, D_B), F32)],
        args=(dx1, o, z, cv, *mp, perm, grp_g, wog, post_g))


def _fetch_row_blocks(wg_ref, w_scr, sems):
    r = wg_ref.shape[1]
    copies = [pltpu.make_async_copy(wg_ref.at[j], w_scr.at[pl.ds(r * j, r), :], sems.at[j]) for j in range(N_CHIPS)]
    for cp in copies:
        cp.start()
    for cp in copies:
        cp.wait()


def _ffn_conv(ext, cw, c0, cn, tt):
    acc = cw[0:1, c0:c0 + cn] * ext[0:tt, c0:c0 + cn]
    for k in range(1, K_F):
        acc = acc + cw[k:k + 1, c0:c0 + cn] * ext[8 * k:8 * k + tt, c0:c0 + cn]
    return acc


def ffn_fwd(x1, g, wug, cw, wdg, post_g, tt, rider=None):
    t, dm = x1.shape
    assert t % tt == 0, (t, tt)
    cn = _col_chunk(D_FF)
    cu = wug.shape[2]

    def body(x1_ref, g_ref, wu_ref, cw_ref, wdg_ref, pg_ref, up0_ref, h2_ref, d_ref, x2_ref, ext, last, wd_ref, sems):
        i = pl.program_id(0)

        @pl.when(i == 0)
        def _():
            _fetch_row_blocks(wdg_ref, wd_ref, sems)
            last[...] = jnp.zeros_like(last)

        xv = x1_ref[...]
        h = (xv * _rstd(xv) * g_ref[...]).astype(MM_DTYPE)
        h2_ref[...] = h
        for j in range(N_CHIPS):
            u = _dot(h, wu_ref[j])
            up0_ref[:, cu * j:cu * (j + 1)] = u
            ext[HR_F:HR_F + tt, cu * j:cu * (j + 1)] = u
        ext[0:HR_F, :] = _halo_before(ext[tt:tt + HR_F, :], last[...])
        last[...] = ext[tt:tt + HR_F, :]
        cwv = cw_ref[...]
        d = jnp.zeros((tt, D_MODEL), F32)
        for c0 in range(0, D_FF, cn):
            gate = _ffn_conv(ext, cwv, c0, cn, tt)
            val = _ffn_conv(ext, cwv, D_FF + c0, cn, tt)
            act = (gate * jax.nn.sigmoid(gate) * val).astype(MM_DTYPE)
            d = d + _dot(act, wd_ref[c0:c0 + cn, :])
        d_ref[...] = d
        x2_ref[...] = xv + d * _rstd(d) * pg_ref[...]

    row = lambda c: pl.BlockSpec((tt, c), lambda i: (i, 0))
    return _pallas(
        body, rider, name="ffn_fwd", steps=t // tt,
        in_specs=[row(dm), _const_spec((1, dm)), _weight_spec(wug), _const_spec((8, 2 * D_FF)), _ANY, _const_spec((1, dm))],
        out_specs=[row(2 * D_FF), row(dm), row(dm), row(dm)],
        out_shape=[jax.ShapeDtypeStruct((t, 2 * D_FF), F32), jax.ShapeDtypeStruct((t, dm), MM_DTYPE),
                   jax.ShapeDtypeStruct((t, dm), F32), jax.ShapeDtypeStruct((t, dm), F32)],
        scratch_shapes=[pltpu.VMEM((HR_F + tt, 2 * D_FF), F32), pltpu.VMEM((HR_F, 2 * D_FF), F32),
                        pltpu.VMEM((D_FF, D_MODEL), MM_DTYPE), pltpu.SemaphoreType.DMA((N_CHIPS,))],
        args=(x1, g, wug, cw, wdg, post_g))


def ffn_bwd(dx2, d, up0, cw, wdg, post_g, tt, rider=None):
    t = up0.shape[0]
    assert t % tt == 0, (t, tt)
    steps = t // tt
    hb = tt // HR_F
    cn = _col_chunk(D_FF)

    def body(dx2_ref, d_ref, up0_ref, uh_ref, cw_ref, wdg_ref, pg_ref,
             dd_ref, act_ref, dup0_ref, dpg_ref, dcw_ref, ext, dup_ext, first, a_pg, a_cw, wd_ref, sems):
        i = pl.program_id(0)
        tile = steps - 1 - i

        @pl.when(i == 0)
        def _():
            _fetch_row_blocks(wdg_ref, wd_ref, sems)
            a_pg[...] = jnp.zeros_like(a_pg)
            a_cw[...] = jnp.zeros_like(a_cw)
            first[...] = jnp.zeros_like(first)

        ext[HR_F:HR_F + tt, :] = up0_ref[...]
        ext[0:HR_F, :] = _halo_before(up0_ref[tt - HR_F:, :], jnp.where(tile > 0, uh_ref[...], 0.0))
        cwv = cw_ref[...]
        dv = d_ref[...]
        dx2v = dx2_ref[...]
        r = _rstd(dv)
        a_pg[...] += _rowsum8(dx2v * dv * r)
        dd = _rms_bwd(dv, r, pg_ref[...], dx2v).astype(MM_DTYPE)
        dd_ref[...] = dd
        for c0 in range(0, D_FF, cn):
            gate = _ffn_conv(ext, cwv, c0, cn, tt)
            val = _ffn_conv(ext, cwv, D_FF + c0, cn, tt)
            sg = jax.nn.sigmoid(gate)
            sl = gate * sg
            act_ref[:, c0:c0 + cn] = (sl * val).astype(MM_DTYPE)
            da = _dot_nt(dd, wd_ref[c0:c0 + cn, :])
            dup_ext[0:tt, c0:c0 + cn] = da * val * (sg * (1.0 + gate * (1.0 - sg)))
            dup_ext[0:tt, D_FF + c0:D_FF + c0 + cn] = da * sl
        dup_ext[tt:tt + HR_F, :] = _halo_after(dup_ext[0:HR_F, :], first[...])
        first[...] = dup_ext[0:HR_F, :]
        for c0 in range(0, 2 * D_FF, cn):
            x = up0_ref[:, c0:c0 + cn]
            acc = None
            for k in range(K_F):
                off = 8 * (K_F - 1 - k)
                ld = dup_ext[off:off + tt, c0:c0 + cn]
                term = cwv[k:k + 1, c0:c0 + cn] * ld
                acc = term if acc is None else acc + term
                a_cw[k, :, c0:c0 + cn] += _rowsum8(ld * x)
            dup0_ref[:, c0:c0 + cn] = acc.astype(MM_DTYPE)

        @pl.when(i == steps - 1)
        def _():
            dpg_ref[...] = jnp.sum(a_pg[...], axis=0, keepdims=True)
            dcw_ref[...] = jnp.sum(a_cw[...], axis=1)

    rev = lambda c: pl.BlockSpec((tt, c), lambda i: (steps - 1 - i, 0))
    halo = pl.BlockSpec((HR_F, 2 * D_FF), lambda i: (jnp.maximum((steps - 1 - i) * hb - 1, 0), 0))
    full = lambda shape: pl.BlockSpec(shape, lambda i: (0,) * len(shape))
    sds = jax.ShapeDtypeStruct
    return _pallas(
        body, rider, name="ffn_bwd", steps=steps,
        in_specs=[rev(D_MODEL), rev(D_MODEL), rev(2 * D_FF), halo, _const_spec((8, 2 * D_FF)), _ANY,
                  _const_spec((1, D_MODEL))],
        out_specs=[rev(D_MODEL), rev(D_FF), rev(2 * D_FF), full((1, D_MODEL)), full((8, 2 * D_FF))],
        out_shape=[sds((t, D_MODEL), MM_DTYPE), sds((t, D_FF), MM_DTYPE), sds((t, 2 * D_FF), MM_DTYPE),
                   sds((1, D_MODEL), F32), sds((8, 2 * D_FF), F32)],
        scratch_shapes=[pltpu.VMEM((HR_F + tt, 2 * D_FF), F32), pltpu.VMEM((tt + HR_F, 2 * D_FF), F32),
                        pltpu.VMEM((HR_F, 2 * D_FF), F32), pltpu.VMEM((8, D_MODEL), F32),
                        pltpu.VMEM((8, 8, 2 * D_FF), F32), pltpu.VMEM((D_FF, D_MODEL), MM_DTYPE),
                        pltpu.SemaphoreType.DMA((N_CHIPS,))],
        args=(dx2, d, up0, up0, cw, wdg, post_g))


def loss_head(y, target, tm):
    t, d = y.shape
    assert t % tm == 0, (t, tm)
    steps = t // tm

    def body(y_ref, t_ref, dy_ref, loss_ref, acc):
        i = pl.program_id(0)

        @pl.when(i == 0)
        def _():
            acc[...] = jnp.zeros_like(acc)

        diff = y_ref[...] - t_ref[...]
        dy_ref[...] = diff * (1.0 / d)
        acc[...] += _rowsum8(diff * diff)

        @pl.when(i == steps - 1)
        def _():
            loss_ref[...] = (0.5 / d) * jnp.sum(jnp.sum(acc[...], axis=0, keepdims=True), axis=1, keepdims=True)

    row = pl.BlockSpec((tm, d), lambda i: (i, 0))
    return pl.pallas_call(
        body, name="loss_head", grid=(steps,), in_specs=[row, row],
        out_specs=[row, pl.BlockSpec((1, 1), lambda i: (0, 0))],
        out_shape=[jax.ShapeDtypeStruct((t, d), F32), jax.ShapeDtypeStruct((1, 1), F32)],
        scratch_shapes=[pltpu.VMEM((8, d), F32)],
        compiler_params=_params(("arbitrary",)),
    )(y, target)


def adamw(w, g, m, v):
    shape = w.shape
    cols = shape[-1]
    rows = w.size // cols
    tr = next((r for r in (512, 256, 128) if rows % r == 0 and rows > r), rows)
    c1 = 1.0 - ADAM_B1 ** ADAM_STEP
    c2 = 1.0 - ADAM_B2 ** ADAM_STEP

    def body(w_ref, g_ref, m_ref, v_ref, d_ref, nm_ref, nv_ref):
        gv = g_ref[...]
        nm = ADAM_B1 * m_ref[...] + (1.0 - ADAM_B1) * gv
        nv = ADAM_B2 * v_ref[...] + (1.0 - ADAM_B2) * (gv * gv)
        nm_ref[...] = nm
        nv_ref[...] = nv
        d_ref[...] = -ADAM_LR * ((nm / c1) / (jnp.sqrt(nv / c2) + ADAM_EPS) + ADAM_WD * w_ref[...])

    spec = pl.BlockSpec((tr, cols), lambda i: (i, 0))
    out = jax.ShapeDtypeStruct((rows, cols), F32)
    res = pl.pallas_call(
        body, name="adamw", grid=(rows // tr,), in_specs=[spec] * 4, out_specs=[spec] * 3, out_shape=[out] * 3,
        compiler_params=_params(("arbitrary",)),
    )(*[a.reshape(rows, cols) for a in (w, g, m, v)])
    return tuple(r.reshape(shape) for r in res)


def _place():
    return lax.axis_index("x"), lax.axis_index("y"), lax.axis_index("c")


def _other_chips(x, y):
    return [(1 - x, y, 2 * (1 - x) + y), (x, 1 - y, 2 * x + 1 - y), (1 - x, 1 - y, 2 * (1 - x) + 1 - y)]


def _sem_specs(*counts):
    return [pltpu.SemaphoreType.DMA((n,)) for n in counts]


def cast_shard(w, layer, chip):
    _, r, c = w.shape

    def body(chip_ref, w_ref, o_ref):
        del chip_ref
        o_ref[...] = w_ref[...].astype(MM_DTYPE)

    grid_spec = pltpu.PrefetchScalarGridSpec(
        num_scalar_prefetch=1, grid=(1,), in_specs=[pl.BlockSpec((None, r, c), lambda i, chip_ref: (layer, 0, 0))],
        out_specs=pl.BlockSpec((None, r, c), lambda i, chip_ref: (chip_ref[0], 0, 0)))
    return pl.pallas_call(
        body, name="cast_shard", grid_spec=grid_spec, out_shape=jax.ShapeDtypeStruct((N_CHIPS, r, c), MM_DTYPE),
        compiler_params=_params(("arbitrary",)),
    )(jnp.reshape(chip, (1,)).astype(jnp.int32), w)


def _row_half(buf, chip, mine, c):
    rh = buf.shape[1] // 2
    return buf.at[chip, pl.ds(pl.multiple_of((c if mine else 1 - c) * rh, 16), rh), :]


def spread_rider(bufs):
    n = len(bufs)

    def start(rin, rout, sems):
        x, y, c = _place()
        me = 2 * x + y
        for k, (px, py, _) in enumerate(_other_chips(x, y)):
            for i, buf in enumerate(rout):
                part = _row_half(buf, me, True, c)
                pltpu.make_async_remote_copy(
                    src_ref=part, dst_ref=part, send_sem=sems[0].at[n * k + i], recv_sem=sems[1].at[n * k + i],
                    device_id=(px, py, c), device_id_type=MESH_ID).start()

    def wait(rin, rout, sems):
        x, y, c = _place()
        for k, (_, _, pj) in enumerate(_other_chips(x, y)):
            for i, buf in enumerate(rout):
                part = _row_half(buf, pj, True, c)
                pltpu.make_async_remote_copy(
                    src_ref=part, dst_ref=part, send_sem=sems[0].at[n * k + i], recv_sem=sems[1].at[n * k + i],
                    device_id=(x, y, c), device_id_type=MESH_ID).wait()

    shapes = [jax.ShapeDtypeStruct(b.shape, b.dtype) for b in bufs]
    return Rider("spread", list(bufs), shapes, {i: i for i in range(n)}, (3 * n, 3 * n), start, wait)


def pass_rider(bufs):
    n = len(bufs)

    def start(rin, rout, sems):
        x, y, c = _place()
        for k, (_, _, pj) in enumerate(_other_chips(x, y)):
            for i, buf in enumerate(rout):
                part = _row_half(buf, pj, True, c)
                pltpu.make_async_remote_copy(
                    src_ref=part, dst_ref=part, send_sem=sems[0].at[n * k + i], recv_sem=sems[1].at[n * k + i],
                    device_id=(x, y, 1 - c), device_id_type=MESH_ID).start()

    def wait(rin, rout, sems):
        x, y, c = _place()
        for k, (_, _, pj) in enumerate(_other_chips(x, y)):
            for i, buf in enumerate(rout):
                part = _row_half(buf, pj, False, c)
                pltpu.make_async_remote_copy(
                    src_ref=part, dst_ref=part, send_sem=sems[0].at[n * k + i], recv_sem=sems[1].at[n * k + i],
                    device_id=(x, y, 1 - c), device_id_type=MESH_ID).wait()

    shapes = [jax.ShapeDtypeStruct(b.shape, b.dtype) for b in bufs]
    return Rider("pass", list(bufs), shapes, {i: i for i in range(n)}, (3 * n, 3 * n), start, wait)


def both_riders(a, b):
    na, oa, sa = len(a.inputs), len(a.out_shapes), len(a.sems)

    def start(rin, rout, sems):
        a.start(rin[:na], rout[:oa], sems[:sa])
        b.start(rin[na:], rout[oa:], sems[sa:])

    def wait(rin, rout, sems):
        a.wait(rin[:na], rout[:oa], sems[:sa])
        b.wait(rin[na:], rout[oa:], sems[sa:])

    aliases = dict(a.aliases)
    aliases.update({na + i: oa + o for i, o in b.aliases.items()})
    return Rider(a.name + "_" + b.name, a.inputs + b.inputs, a.out_shapes + b.out_shapes, aliases, a.sems + b.sems,
                 start, wait)


def gather_small(small):
    def body(small_ref, out_ref, send, recv, local):
        x, y, c = _place()
        me = 2 * x + y
        chips = _other_chips(x, y)
        own = pltpu.make_async_copy(small_ref, out_ref.at[me], local.at[0])
        own.start()
        sends = [pltpu.make_async_remote_copy(src_ref=small_ref, dst_ref=out_ref.at[me], send_sem=send.at[k],
                                              recv_sem=recv.at[k], device_id=(px, py, c), device_id_type=MESH_ID)
                 for k, (px, py, _) in enumerate(chips)]
        for cp in sends:
            cp.start()
        for k, (_, _, pj) in enumerate(chips):
            pltpu.make_async_remote_copy(src_ref=small_ref, dst_ref=out_ref.at[pj], send_sem=send.at[k], recv_sem=recv.at[k],
                                         device_id=(x, y, c), device_id_type=MESH_ID).wait_recv()
        for cp in sends:
            cp.wait_send()
        own.wait()

    return pl.pallas_call(
        body, name="gather_small", in_specs=[_ANY], out_specs=_ANY,
        out_shape=jax.ShapeDtypeStruct((N_CHIPS,) + small.shape, small.dtype), scratch_shapes=_sem_specs(3, 3, 1),
        compiler_params=pltpu.CompilerParams(has_side_effects=True),
    )(small)


def swap_rider(gs):
    n = len(gs)

    def copies(rin, rout, sems):
        x, y, c = _place()
        out = []
        for i, (g, got) in enumerate(zip(rin, rout)):
            rh = g.shape[1] // 2
            theirs = pl.ds(pl.multiple_of((1 - c) * rh, 8), rh)
            out.append(pltpu.make_async_remote_copy(
                src_ref=g.at[:, theirs, :], dst_ref=got, send_sem=sems[0].at[i], recv_sem=sems[1].at[i],
                device_id=(x, y, 1 - c), device_id_type=MESH_ID))
        return out

    def start(rin, rout, sems):
        for cp in copies(rin, rout, sems):
            cp.start()

    def wait(rin, rout, sems):
        for cp in copies(rin, rout, sems):
            cp.wait()

    shapes = [jax.ShapeDtypeStruct((g.shape[0], g.shape[1] // 2, g.shape[2]), g.dtype) for g in gs]
    return Rider("swap", list(gs), shapes, {}, (n, n), start, wait)


def scatter_rider(sbs):
    n = len(sbs)

    def start(rin, rout, sems):
        x, y, c = _place()
        me = 2 * x + y
        for k, (px, py, pj) in enumerate(_other_chips(x, y)):
            for i, (sb, got) in enumerate(zip(rin, rout)):
                pltpu.make_async_remote_copy(
                    src_ref=sb.at[pj], dst_ref=got.at[me], send_sem=sems[0].at[n * k + i], recv_sem=sems[1].at[n * k + i],
                    device_id=(px, py, c), device_id_type=MESH_ID).start()

    def wait(rin, rout, sems):
        x, y, c = _place()
        for k, (_, _, pj) in enumerate(_other_chips(x, y)):
            for i, (sb, got) in enumerate(zip(rin, rout)):
                cp = pltpu.make_async_remote_copy(
                    src_ref=sb.at[pj], dst_ref=got.at[pj], send_sem=sems[0].at[n * k + i], recv_sem=sems[1].at[n * k + i],
                    device_id=(x, y, c), device_id_type=MESH_ID)
                cp.wait_recv()
                cp.wait_send()

    shapes = [jax.ShapeDtypeStruct(sb.shape, sb.dtype) for sb in sbs]
    return Rider("scatter", list(sbs), shapes, {}, (3 * n, 3 * n), start, wait)


def join_rider(fs, layers):
    n = len(fs)

    def half(i, f, mine, place):
        x, y, c = place
        rh = f.shape[1] // 2
        block = 2 * x + y if layers[i] is None else layers[i]
        return f.at[block, pl.ds(pl.multiple_of((c if mine else 1 - c) * rh, 8), rh), :]

    def start(rin, rout, sems):
        x, y, c = _place()
        for i, f in enumerate(rout):
            part = half(i, f, True, (x, y, c))
            pltpu.make_async_remote_copy(
                src_ref=part, dst_ref=part, send_sem=sems[0].at[i], recv_sem=sems[1].at[i],
                device_id=(x, y, 1 - c), device_id_type=MESH_ID).start()

    def wait(rin, rout, sems):
        x, y, c = _place()
        for i, f in enumerate(rout):
            part = half(i, f, False, (x, y, c))
            pltpu.make_async_remote_copy(
                src_ref=part, dst_ref=part, send_sem=sems[0].at[i], recv_sem=sems[1].at[i],
                device_id=(x, y, 1 - c), device_id_type=MESH_ID).wait()

    shapes = [jax.ShapeDtypeStruct(f.shape, f.dtype) for f in fs]
    return Rider("join", list(fs), shapes, {i: i for i in range(n)}, (n, n), start, wait)


def add_halves(gs, gots, wire=BF16):
    m = len(gs)
    x, y, c = _place()

    def body(p_ref, *refs):
        ins, outs = refs[:2 * m], refs[2 * m:]
        for i in range(m):
            s = ins[2 * i][...] + ins[2 * i + 1][...]
            outs[2 * i][...] = s.astype(wire)

            @pl.when(pl.program_id(0) == p_ref[0])
            def _(s=s, own_ref=outs[2 * i + 1]):
                own_ref[...] = s

    in_specs, out_specs, out_shape = [], [], []
    for got in gots:
        n, rh, cols = got.shape
        blk = (None, rh, cols)
        in_specs += [pl.BlockSpec(blk, lambda j, p_ref: (j, p_ref[1], 0)), pl.BlockSpec(blk, lambda j, p_ref: (j, 0, 0))]
        out_specs += [pl.BlockSpec(blk, lambda j, p_ref: (j, 0, 0)), pl.BlockSpec((rh, cols), lambda j, p_ref: (0, 0))]
        out_shape += [jax.ShapeDtypeStruct(got.shape, wire), jax.ShapeDtypeStruct((rh, cols), F32)]
    grid_spec = pltpu.PrefetchScalarGridSpec(num_scalar_prefetch=1, grid=(N_CHIPS,), in_specs=in_specs, out_specs=out_specs)
    res = pl.pallas_call(
        body, name="add_halves", grid_spec=grid_spec, out_shape=out_shape, compiler_params=_params(("arbitrary",)),
    )(jnp.stack([2 * x + y, c]).astype(jnp.int32), *[a for pair in zip(gs, gots) for a in pair])
    return [(res[2 * i + 1], res[2 * i]) for i in range(m)]


def add_chips(owns, gots, fbufs, block=None):
    m = len(owns)
    x, y, c = _place()
    me = 2 * x + y

    def body(p_ref, *refs):
        ins, outs = refs[:5 * m], refs[5 * m:]
        for i in range(m):
            s_ref, g1_ref, g2_ref, g3_ref, _ = ins[5 * i:5 * i + 5]
            outs[i][...] = s_ref[...] + g1_ref[...].astype(F32) + g2_ref[...].astype(F32) + g3_ref[...].astype(F32)

    def other(blk, n, k):
        return pl.BlockSpec(blk, lambda i, p_ref: ((p_ref[0] + k) % n, 0, 0))

    in_specs, out_specs, args = [], [], []
    for own, got, fbuf in zip(owns, gots, fbufs):
        n, rh, cols = got.shape
        blk = (None, rh, cols)
        in_specs += [pl.BlockSpec((rh, cols), lambda i, p_ref: (0, 0)), other(blk, n, 1), other(blk, n, 2), other(blk, n, 3),
                     _ANY]
        out_specs.append(pl.BlockSpec(blk, lambda i, p_ref: (p_ref[2], p_ref[1], 0)))
        args += [own, got, got, got, fbuf]
    grid_spec = pltpu.PrefetchScalarGridSpec(num_scalar_prefetch=1, grid=(1,), in_specs=in_specs, out_specs=out_specs)
    return pl.pallas_call(
        body, name="add_chips", grid_spec=grid_spec, out_shape=[jax.ShapeDtypeStruct(f.shape, F32) for f in fbufs],
        input_output_aliases={5 * i + 5: i for i in range(m)}, compiler_params=_params(("arbitrary",)),
    )(jnp.stack([me, c, me if block is None else block]).astype(jnp.int32), *args)


def _pack(arrays, rows):
    flat = jnp.concatenate([a.reshape(-1) for a in arrays])
    return jnp.pad(flat, (0, rows * LANES - flat.size)).reshape(rows, LANES)


def _unpack(buf, shapes):
    flat = buf.reshape(-1)
    out, at = [], 0
    for s in shapes:
        n = math.prod(s)
        out.append(flat[at:at + n].reshape(s))
        at += n
    return out


CONV_SHARDS = [(DEPTH, K_A, D_A // N_CHIPS), (DEPTH, K_C, D_C // N_CHIPS), (DEPTH, K_F, 2 * D_FF // N_CHIPS)]
CONV_ROWS = 32
SMALL_ROWS = 640


def _join_cols(g):
    n, l, r, c = g.shape
    return jnp.transpose(g, (1, 2, 0, 3)).reshape(l, r, n * c)


BIG = ["w_in", "w_out", "w_up", "w_down"]
WIDE = ["w_up", "w_down"]
NARROW = ["w_in", "w_out"]
TILE_MM = 512
TILE_TN = 1024
TILE_EW = 256


def _pad_rows(a, rows):
    return jnp.pad(a, ((0, rows - a.shape[0]), (0, 0)))


def _row(a):
    return a.reshape(1, -1)


def _tile_perm(tt):
    p = lax.broadcasted_iota(jnp.int32, (tt, tt), 0)
    tok = lax.broadcasted_iota(jnp.int32, (tt, tt), 1)
    return ((tt // 8) * (p % 8) + p // 8 == tok).astype(F32)


def _layer_params(wl, tt):
    n = tt // CHUNK
    tril = jnp.tril(jnp.ones((CHUNK, CHUNK), bool))
    wm = jnp.where(tril[None], wl["sgu_w"], 0.0)
    eye = jnp.eye(n, dtype=F32)
    wt = (eye[None, :, None, :, None] * wm[:, None, :, None, :]).reshape(N_HEADS_B, tt, tt)
    bias_e = jnp.repeat(wl["sgu_b"].T, HEAD, axis=1)
    return (_pad_rows(wl["conv_a_w"], 8), _row(wl["sgu_ln_g"]), _row(wl["sgu_ln_b"]), wt.astype(MM_DTYPE),
            jnp.tile(bias_e, (n, 1)), _pad_rows(wl["conv_c_w"], 32), _row(wl["conv_c_b"]), _row(wl["conv_ln_g"]),
            _row(wl["conv_ln_b"]))


def layer_fwd(x, wl, gw, nxt=None, tm=TILE_MM, tt=TILE_EW):
    mp = _layer_params(wl, tt)
    rides = ([pass_rider([gw[n] for n in WIDE])] if gw.get("pass_wide") else []) + (
        [spread_rider([nxt[n] for n in NARROW])] if nxt else [])
    ride = both_riders(*rides) if len(rides) == 2 else (rides[0] if rides else None)
    (z, h, o, x1, cv), done = mixer_fwd(x, _row(wl["pre_mix_g"]), gw["w_in"], mp, _tile_perm(tt), _row(wl["grp_norm_g"]),
                                        gw["w_out"], _row(wl["post_mix_g"]), tt, rider=ride)
    if gw.get("pass_wide"):
        gw, done = dict(gw, w_up=done[0], w_down=done[1]), done[2:]
    gw = {n: gw[n] for n in BIG}
    ride = both_riders(spread_rider([nxt[n] for n in WIDE]), pass_rider(list(done))) if nxt else None
    (up0, h2, d, x2), done = ffn_fwd(x1, _row(wl["pre_ffn_g"]), gw["w_up"], _pad_rows(wl["ffn_conv_w"], 8), gw["w_down"],
                                     _row(wl["post_ffn_g"]), tt, rider=ride)
    if nxt:
        nxt = dict(nxt, w_up=done[0], w_down=done[1], w_in=done[2], w_out=done[3], pass_wide=True)
    return x2, dict(x=x, z=z, h=h, o=o, x1=x1, up0=up0, h2=h2, d=d, cv=cv, gw=gw), nxt


def layer_bwd(dx2, wl, layer, sv, pend=None, exchange=True, tm=TILE_MM, tt=TILE_EW):
    mp = _layer_params(wl, tt)
    gw = sv["gw"]
    tk = min(TILE_TN, dx2.shape[0])
    at = {n: BIG.index(n) for n in BIG}
    g = {}
    ride = scatter_rider([sw for _, sw in pend["narrow"]]) if pend else None
    (dd, act, dup0, dpg, dcw), arrived = ffn_bwd(dx2, sv["d"], sv["up0"], _pad_rows(wl["ffn_conv_w"], 8), gw["w_down"],
                                                 _row(wl["post_ffn_g"]), tt, rider=ride)
    fbuf = list(pend["fbuf"]) if pend else grad_buffers()
    if pend:
        done = add_chips([own for own, _ in pend["narrow"]], arrived, [fbuf[at[n]] for n in NARROW], pend["layer"])
        for n, f in zip(NARROW, done):
            fbuf[at[n]] = f
    g["post_ffn_g"] = dpg[0]
    g["ffn_conv_w"] = dcw[:K_F]
    gl = {}
    gl["w_down"] = matmul_tn_down(act, dd, tk)
    gl["w_up"] = matmul_tn_cols(sv["h2"], dup0, tk)
    ride = swap_rider([gl[n] for n in WIDE]) if exchange else None
    (dx1, dg), got = matmul_nt_norm_bwd(dup0, gw["w_up"], sv["x1"], _row(wl["pre_ffn_g"]), dx2, tm, rider=ride)
    g["pre_ffn_g"] = dg[0]
    wide = add_halves([gl[n] for n in WIDE], got) if exchange else None
    ride = scatter_rider([sw for _, sw in wide]) if exchange else None
    if pend:
        ride = both_riders(join_rider(fbuf, [pend["layer"]] * len(fbuf)), ride)
    (dz, gl["w_out"], dpg, dgg, dcaw, dlng, dlnb, dwm, dbias, dccw, dccb, dclg, dclb), rode = mixer_bwd(
        dx1, sv["o"], sv["z"], sv["cv"], mp, _tile_perm(tt), _row(wl["grp_norm_g"]), gw["w_out"],
        _row(wl["post_mix_g"]), tt, rider=ride)
    if exchange:
        fbuf, arrived = (list(rode[:len(BIG)]), rode[len(BIG):]) if pend else (fbuf, rode)
        done = add_chips([own for own, _ in wide], arrived, [fbuf[at[n]] for n in WIDE], layer)
        for n, f in zip(WIDE, done):
            fbuf[at[n]] = f
    g["post_mix_g"] = dpg[0]
    g["grp_norm_g"] = dgg[0]
    g["conv_a_w"] = dcaw[:K_A]
    g["sgu_ln_g"] = dlng[0]
    g["sgu_ln_b"] = dlnb[0]
    g["sgu_w"] = dwm
    g["sgu_b"] = dbias[:, :N_HEADS_B].T
    g["conv_c_w"] = dccw[:K_C]
    g["conv_c_b"] = dccb[0]
    g["conv_ln_g"] = dclg[0]
    g["conv_ln_b"] = dclb[0]
    gl["w_in"] = matmul_tn_in(sv["h"], dz, tk)
    ride = swap_rider([gl[n] for n in NARROW]) if exchange else None
    (dx, dg), got = matmul_nt_norm_bwd(dz, gw["w_in"], sv["x"], _row(wl["pre_mix_g"]), dx1, tm, rider=ride)
    g["pre_mix_g"] = dg[0]
    if not exchange:
        return dx, g, gl
    narrow = add_halves([gl[n] for n in NARROW], got)
    return dx, g, dict(narrow=narrow, fbuf=fbuf, layer=layer)


def grad_buffers():
    return [lax.empty(s, F32) for s in ((DEPTH, D_MODEL, D_IN // N_CHIPS), (DEPTH, D_MODEL // N_CHIPS, D_MODEL),
                                        (DEPTH, D_MODEL, 2 * D_FF // N_CHIPS), (DEPTH, D_FF // N_CHIPS, D_MODEL))]


CONV = ["conv_a_w", "conv_c_w", "ffn_conv_w"]
REPL = ["pre_mix_g", "sgu_ln_g", "sgu_ln_b", "sgu_w", "sgu_b", "conv_c_b", "conv_ln_g", "conv_ln_b", "grp_norm_g",
        "post_mix_g", "pre_ffn_g", "post_ffn_g"]
WEIGHTS = ["pre_mix_g", "w_in", "conv_a_w", "sgu_ln_g", "sgu_ln_b", "sgu_w", "sgu_b", "conv_c_w", "conv_c_b", "conv_ln_g",
           "conv_ln_b", "grp_norm_g", "w_out", "post_mix_g", "pre_ffn_g", "w_up", "ffn_conv_w", "w_down", "post_ffn_g"]


def kernel(x, pre_mix_g, w_in, conv_a_w, sgu_ln_g, sgu_ln_b, sgu_w, sgu_b, conv_c_w, conv_c_b, conv_ln_g, conv_ln_b, grp_norm_g, w_out, post_mix_g, pre_ffn_g, w_up, ffn_conv_w, w_down, post_ffn_g, loss_target, m_pre_mix_g, m_w_in, m_conv_a_w, m_sgu_ln_g, m_sgu_ln_b, m_sgu_w, m_sgu_b, m_conv_c_w, m_conv_c_b, m_conv_ln_g, m_conv_ln_b, m_grp_norm_g, m_w_out, m_post_mix_g, m_pre_ffn_g, m_w_up, m_ffn_conv_w, m_w_down, m_post_ffn_g, v_pre_mix_g, v_w_in, v_conv_a_w, v_sgu_ln_g, v_sgu_ln_b, v_sgu_w, v_sgu_b, v_conv_c_w, v_conv_c_b, v_conv_ln_g, v_conv_ln_b, v_grp_norm_g, v_w_out, v_post_mix_g, v_pre_ffn_g, v_w_up, v_ffn_conv_w, v_w_down, v_post_ffn_g):
    w = dict(pre_mix_g=pre_mix_g, w_in=w_in, conv_a_w=conv_a_w, sgu_ln_g=sgu_ln_g, sgu_ln_b=sgu_ln_b, sgu_w=sgu_w, sgu_b=sgu_b,
             conv_c_w=conv_c_w, conv_c_b=conv_c_b, conv_ln_g=conv_ln_g, conv_ln_b=conv_ln_b, grp_norm_g=grp_norm_g,
             w_out=w_out, post_mix_g=post_mix_g, pre_ffn_g=pre_ffn_g, w_up=w_up, ffn_conv_w=ffn_conv_w, w_down=w_down,
             post_ffn_g=post_ffn_g)
    m = dict(pre_mix_g=m_pre_mix_g, w_in=m_w_in, conv_a_w=m_conv_a_w, sgu_ln_g=m_sgu_ln_g, sgu_ln_b=m_sgu_ln_b,
             sgu_w=m_sgu_w, sgu_b=m_sgu_b, conv_c_w=m_conv_c_w, conv_c_b=m_conv_c_b, conv_ln_g=m_conv_ln_g,
             conv_ln_b=m_conv_ln_b, grp_norm_g=m_grp_norm_g, w_out=m_w_out, post_mix_g=m_post_mix_g,
             pre_ffn_g=m_pre_ffn_g, w_up=m_w_up, ffn_conv_w=m_ffn_conv_w, w_down=m_w_down, post_ffn_g=m_post_ffn_g)
    v = dict(pre_mix_g=v_pre_mix_g, w_in=v_w_in, conv_a_w=v_conv_a_w, sgu_ln_g=v_sgu_ln_g, sgu_ln_b=v_sgu_ln_b,
             sgu_w=v_sgu_w, sgu_b=v_sgu_b, conv_c_w=v_conv_c_w, conv_c_b=v_conv_c_b, conv_ln_g=v_conv_ln_g,
             conv_ln_b=v_conv_ln_b, grp_norm_g=v_grp_norm_g, w_out=v_w_out, post_mix_g=v_post_mix_g,
             pre_ffn_g=v_pre_ffn_g, w_up=v_w_up, ffn_conv_w=v_ffn_conv_w, w_down=v_w_down, post_ffn_g=v_post_ffn_g)
    chip = 2 * lax.axis_index("x") + lax.axis_index("y")

    convs = gather_small(_pack([w[n] for n in CONV], CONV_ROWS))
    gws = [{n: cast_shard(w[n], layer, chip) for n in BIG} for layer in range(DEPTH)]
    first = run_rider(pass_rider(run_rider(spread_rider([gws[0][n] for n in BIG]))))
    gws[0] = dict(zip(BIG, first))
    cparts = [_unpack(convs[j], CONV_SHARDS) for j in range(N_CHIPS)]
    full = dict(w)
    for i, n in enumerate(CONV):
        full[n] = _join_cols(jnp.stack([p[i] for p in cparts]))

    xc = to_tiles(x[0], TILE_EW)
    saved = []
    for layer in range(DEPTH):
        nxt = gws[layer + 1] if layer + 1 < DEPTH else None
        xc, sv, nxt = layer_fwd(xc, {n: full[n][layer] for n in REPL + CONV}, gws[layer], nxt)
        if nxt:
            gws[layer + 1] = nxt
        saved.append(sv)
    dxc, loss_part = loss_head(xc, to_tiles(loss_target[0], TILE_EW), TILE_MM)
    loss = lax.psum(loss_part[0, 0], ("x", "y", "c"))
    small = [None] * DEPTH
    pend = None
    for layer in reversed(range(DEPTH)):
        dxc, small[layer], pend = layer_bwd(dxc, {n: full[n][layer] for n in REPL + CONV}, layer, saved[layer], pend)
    grads = {n: jnp.stack([small[layer][n] for layer in range(DEPTH)]) for n in REPL + CONV}

    gsmall = _pack([grads[n] for n in REPL + CONV], SMALL_ROWS).reshape(N_CHIPS, SMALL_ROWS // N_CHIPS, LANES)
    sums = pend["narrow"] + add_halves([gsmall], run_rider(swap_rider([gsmall])), wire=F32)
    arrived = run_rider(scatter_rider([sw for _, sw in sums]))
    fbuf = list(pend["fbuf"])
    at = [BIG.index(n) for n in NARROW]
    for i, f in zip(at, add_chips([own for own, _ in sums[:2]], arrived[:2], [fbuf[i] for i in at], 0)):
        fbuf[i] = f
    fbuf += add_chips([sums[2][0]], arrived[2:], [lax.empty(gsmall.shape, F32)])
    joined = run_rider(join_rider(fbuf, [0] * len(BIG) + [None]))
    out_g = dict(zip(BIG, joined))
    tot = run_rider(pass_rider(run_rider(spread_rider([joined[len(BIG)]]))))[0].reshape(SMALL_ROWS, LANES)
    shapes = [grads[n].shape for n in REPL + CONV]
    for n, gfull in zip(REPL + CONV, _unpack(tot, shapes)):
        if n in CONV:
            width = gfull.shape[-1] // N_CHIPS
            gfull = lax.dynamic_slice_in_dim(gfull, chip * width, width, axis=2)
        out_g[n] = gfull

    deltas, new_m, new_v = {}, {}, {}
    for n in WEIGHTS:
        deltas[n], new_m[n], new_v[n] = adamw(w[n], out_g[n], m[n], v[n])
    return (loss, from_tiles(dxc, TILE_EW)[None], *[out_g[n] for n in WEIGHTS], *[deltas[n] for n in WEIGHTS], *[new_m[n] for n in WEIGHTS],
            *[new_v[n] for n in WEIGHTS])
```

```python
import math
from typing import Callable, NamedTuple

import jax
import jax.numpy as jnp
from jax import lax
from jax.experimental import pallas as pl
from jax.experimental.pallas import tpu as pltpu

F32 = jnp.float32
BF16 = jnp.bfloat16
MM_DTYPE = BF16

D_MODEL = 1024
DEPTH = 4
D_A = 256
D_B = 384
D_C = 384
D_IN = 3 * D_A + 2 * D_B + 2 * D_C
D_FF = 2816
K_A = 3
K_C = 31
K_F = 3
CHUNK = 128
HEAD = 64
N_HEADS_B = D_B // HEAD
EPS = 1e-6
N_CHIPS = 4

ADAM_LR = 0.001
ADAM_B1 = 0.9
ADAM_B2 = 0.999
ADAM_EPS = 1e-08
ADAM_WD = 0.01
ADAM_STEP = 10

LANES = 1024
VMEM_LIMIT = 56 * 1024 * 1024

MESH_ID = pl.DeviceIdType.MESH
_ANY = pl.BlockSpec(memory_space=pl.ANY)


def _params(sem=None):
    return pltpu.CompilerParams(dimension_semantics=sem, vmem_limit_bytes=VMEM_LIMIT)


def _const_spec(shape):
    nd = len(shape)
    return pl.BlockSpec(shape, lambda *_: (0,) * nd, pipeline_mode=pl.Buffered(1))


def _rowsum8(a):
    r, c = a.shape
    return jnp.sum(a.reshape(r // 8, 8, c), axis=0)


def _rstd(x):
    return lax.rsqrt(jnp.mean(x * x, axis=-1, keepdims=True) + EPS)


def _rms_bwd(x, r, g, dy):
    gdy = g * dy
    return r * gdy - x * (r * r * r) * jnp.mean(gdy * x, axis=-1, keepdims=True)


def _ln_fwd(x):
    mu = jnp.mean(x, axis=-1, keepdims=True)
    xc = x - mu
    r = lax.rsqrt(jnp.mean(xc * xc, axis=-1, keepdims=True) + EPS)
    return xc * r, r


def _ln_bwd(xh, r, dxh):
    return r * (dxh - jnp.mean(dxh, axis=-1, keepdims=True) - xh * jnp.mean(dxh * xh, axis=-1, keepdims=True))


def _gelu(x):
    return 0.5 * x * (1.0 + lax.erf(x * (1.0 / math.sqrt(2.0))))


def _gelu_grad(x):
    cdf = 0.5 * (1.0 + lax.erf(x * (1.0 / math.sqrt(2.0))))
    pdf = jnp.exp(-0.5 * x * x) * (1.0 / math.sqrt(2.0 * math.pi))
    return cdf + x * pdf


def _dot(a, b):
    return jnp.dot(a, b, preferred_element_type=F32)


def _dot_nt(a, b):
    return lax.dot_general(a, b, (((1,), (1,)), ((), ())), preferred_element_type=F32)


def _dot_tn(a, b):
    return lax.dot_general(a, b, (((0,), (0,)), ((), ())), preferred_element_type=F32)


def _col_chunk(n):
    for c in (1408, 1024, 768, 512, 256, 128):
        if n % c == 0:
            return c
    raise ValueError(n)


class Rider(NamedTuple):
    name: str
    inputs: list
    out_shapes: list
    aliases: dict
    sems: tuple
    start: Callable
    wait: Callable


def _pallas(body, rider, *, name, steps, in_specs, out_specs, out_shape, scratch_shapes, args):
    if rider is None:
        res = pl.pallas_call(body, name=name, grid=(steps,), in_specs=in_specs, out_specs=out_specs, out_shape=out_shape,
                             scratch_shapes=scratch_shapes, compiler_params=_params(("arbitrary",)))(*args)
        return res, []
    n_in, n_out, n_scr = len(in_specs), len(out_specs), len(scratch_shapes)
    r_in, r_out = len(rider.inputs), len(rider.out_shapes)

    def wrapped(*refs):
        ins, rin = refs[:n_in], refs[n_in:n_in + r_in]
        at = n_in + r_in
        outs, rout = refs[at:at + n_out], refs[at + n_out:at + n_out + r_out]
        at += n_out + r_out
        scr, rsem = refs[at:at + n_scr], refs[at + n_scr:]

        @pl.when(pl.program_id(0) == 0)
        def _():
            rider.start(rin, rout, rsem)

        body(*ins, *outs, *scr)

        @pl.when(pl.program_id(0) == steps - 1)
        def _():
            rider.wait(rin, rout, rsem)

    res = pl.pallas_call(
        wrapped, name=name + "_" + rider.name, grid=(steps,), in_specs=list(in_specs) + [_ANY] * r_in,
        out_specs=list(out_specs) + [_ANY] * r_out, out_shape=list(out_shape) + list(rider.out_shapes),
        scratch_shapes=list(scratch_shapes) + [pltpu.SemaphoreType.DMA((n,)) for n in rider.sems],
        input_output_aliases={n_in + i: n_out + o for i, o in rider.aliases.items()},
        compiler_params=pltpu.CompilerParams(dimension_semantics=("arbitrary",), vmem_limit_bytes=VMEM_LIMIT,
                                             has_side_effects=True),
    )(*args, *rider.inputs)
    return res[:n_out], res[n_out:]


def run_rider(rider):
    def body(*refs):
        r_in, r_out = len(rider.inputs), len(rider.out_shapes)
        rin, rout, rsem = refs[:r_in], refs[r_in:r_in + r_out], refs[r_in + r_out:]
        rider.start(rin, rout, rsem)
        rider.wait(rin, rout, rsem)

    return pl.pallas_call(
        body, name=rider.name, in_specs=[_ANY] * len(rider.inputs), out_specs=[_ANY] * len(rider.out_shapes),
        out_shape=list(rider.out_shapes), scratch_shapes=[pltpu.SemaphoreType.DMA((n,)) for n in rider.sems],
        input_output_aliases=dict(rider.aliases), compiler_params=pltpu.CompilerParams(has_side_effects=True),
    )(*rider.inputs)


def _weight_spec(wg):
    return _const_spec(wg.shape)


def _join_col_blocks(w_ref, w_scr):
    c = w_ref.shape[2]
    for j in range(N_CHIPS):
        w_scr[:, c * j:c * (j + 1)] = w_ref[j]


def matmul_nt_norm_bwd(gy, wg, x, g, dres, tm, rider=None):
    t, n = gy.shape
    assert t % tm == 0, (t, tm)
    d, cw = wg.shape[1], wg.shape[2]
    aligned = cw % 128 == 0
    cn = cw if aligned else _col_chunk(n)
    steps = t // tm

    def body(gy_ref, w_ref, x_ref, g_ref, dres_ref, dx_ref, dg_ref, acc_ref, *scr):
        i = pl.program_id(0)

        @pl.when(i == 0)
        def _():
            acc_ref[...] = jnp.zeros_like(acc_ref)
            if not aligned:
                _join_col_blocks(w_ref, scr[0])

        dh = jnp.zeros((tm, d), F32)
        for j, c0 in enumerate(range(0, n, cn)):
            wv = w_ref[j] if aligned else scr[0][:, c0:c0 + cn]
            dh = dh + _dot_nt(gy_ref[:, c0:c0 + cn], wv)
        xv = x_ref[...]
        r = _rstd(xv)
        gv = g_ref[...]
        dx_ref[...] = dres_ref[...] + _rms_bwd(xv, r, gv, dh)
        acc_ref[...] += _rowsum8(dh * xv * r)

        @pl.when(i == steps - 1)
        def _():
            dg_ref[...] = jnp.sum(acc_ref[...], axis=0, keepdims=True)

    return _pallas(
        body, rider, name="matmul_nt_norm_bwd", steps=steps,
        in_specs=[pl.BlockSpec((tm, n), lambda i: (i, 0)), _weight_spec(wg), pl.BlockSpec((tm, d), lambda i: (i, 0)),
                  _const_spec((1, d)), pl.BlockSpec((tm, d), lambda i: (i, 0))],
        out_specs=[pl.BlockSpec((tm, d), lambda i: (i, 0)), pl.BlockSpec((1, d), lambda i: (0, 0))],
        out_shape=[jax.ShapeDtypeStruct((t, d), F32), jax.ShapeDtypeStruct((1, d), F32)],
        scratch_shapes=[pltpu.VMEM((8, d), F32)] + ([] if aligned else [pltpu.VMEM((d, n), MM_DTYPE)]),
        args=(gy, wg, x, g, dres))


def matmul_tn_cols(a, b, tk):
    t, r = a.shape
    c = b.shape[1] // N_CHIPS
    assert t % tk == 0 and r % 8 == 0 and c % 128 == 0, (a.shape, b.shape, tk)

    def body(a_ref, b_ref, o_ref):
        @pl.when(pl.program_id(1) == 0)
        def _():
            o_ref[...] = jnp.zeros_like(o_ref)

        o_ref[...] += _dot_tn(a_ref[...], b_ref[...])

    a_spec = pl.BlockSpec((tk, r), lambda j, k: (k, 0))
    b_spec = pl.BlockSpec((tk, c), lambda j, k: (k, j))
    return pl.pallas_call(
        body, name="matmul_tn_cols", grid=(N_CHIPS, t // tk), in_specs=[a_spec, b_spec],
        out_specs=pl.BlockSpec((None, r, c), lambda j, k: (j, 0, 0)),
        out_shape=jax.ShapeDtypeStruct((N_CHIPS, r, c), F32),
        compiler_params=_params(("arbitrary", "arbitrary")),
    )(a, b)


def matmul_tn_down(act, dd, tk):
    t, m = act.shape
    c = dd.shape[1]
    r = m // N_CHIPS
    assert t % tk == 0, (t, tk)
    steps = t // tk

    def body(a_ref, b_ref, o_ref, acc):
        k = pl.program_id(1)

        @pl.when(k == 0)
        def _():
            acc[...] = jnp.zeros_like(acc)

        acc[...] += _dot_tn(a_ref[...], b_ref[...])

        @pl.when(k == steps - 1)
        def _():
            o_ref[0] = acc[0:r, :]
            o_ref[1] = acc[r:2 * r, :]

    return pl.pallas_call(
        body, name="matmul_tn_down", grid=(2, steps),
        in_specs=[pl.BlockSpec((tk, 2 * r), lambda p, k: (k, p)), pl.BlockSpec((tk, c), lambda p, k: (k, 0))],
        out_specs=pl.BlockSpec((2, r, c), lambda p, k: (p, 0, 0)),
        out_shape=jax.ShapeDtypeStruct((N_CHIPS, r, c), F32),
        scratch_shapes=[pltpu.VMEM((2 * r, c), F32)],
        compiler_params=_params(("arbitrary", "arbitrary")),
    )(act, dd)


def matmul_tn_in(h, dz, tk):
    t, d = h.shape
    n = dz.shape[1]
    c = n // N_CHIPS
    assert t % tk == 0, (t, tk)
    steps = t // tk

    def body(a_ref, b_ref, o_ref, acc):
        k = pl.program_id(0)

        @pl.when(k == 0)
        def _():
            acc[...] = jnp.zeros_like(acc)

        acc[...] += _dot_tn(a_ref[...], b_ref[...])

        @pl.when(k == steps - 1)
        def _():
            for j in range(N_CHIPS):
                o_ref[j] = acc[:, c * j:c * (j + 1)]

    return pl.pallas_call(
        body, name="matmul_tn_in", grid=(steps,),
        in_specs=[pl.BlockSpec((tk, d), lambda k: (k, 0)), pl.BlockSpec((tk, n), lambda k: (k, 0))],
        out_specs=pl.BlockSpec((N_CHIPS, d, c), lambda k: (0, 0, 0)),
        out_shape=jax.ShapeDtypeStruct((N_CHIPS, d, c), F32),
        scratch_shapes=[pltpu.VMEM((d, n), F32)],
        compiler_params=_params(("arbitrary",)),
    )(h, dz)


def to_tiles(a, tt):
    t = a.shape[0]
    return a.reshape((t // tt, 8, tt // 8) + a.shape[1:]).swapaxes(1, 2).reshape(a.shape)


def from_tiles(a, tt):
    t = a.shape[0]
    return a.reshape((t // tt, tt // 8, 8) + a.shape[1:]).swapaxes(1, 2).reshape(a.shape)


def _roll_sublanes(a, shift):
    n = a.shape[0] // 8
    return pltpu.roll(a.reshape(n, 8, a.shape[1]), shift, 1).reshape(a.shape)


def _halo_before(cur_last, prev_last):
    sub = lax.broadcasted_iota(jnp.int32, cur_last.shape, 0) % 8
    return jnp.where(sub == 0, _roll_sublanes(prev_last, 1), _roll_sublanes(cur_last, 1))


def _halo_after(cur_first, next_first):
    sub = lax.broadcasted_iota(jnp.int32, cur_first.shape, 0) % 8
    return jnp.where(sub == 7, _roll_sublanes(next_first, 7), _roll_sublanes(cur_first, 7))


def _conv_causal(ext, cur, prev_last, w, taps, tt, cols=None):
    hr = 8 * (taps - 1)
    cs = slice(None) if cols is None else cols
    ext[hr:hr + tt, cs] = cur
    ext[0:hr, cs] = _halo_before(cur[tt - hr:, :], prev_last)
    acc = w[0:1, :] * ext[0:tt, cs]
    for k in range(1, taps):
        acc = acc + w[k:k + 1, :] * ext[8 * k:8 * k + tt, cs]
    return acc


def _conv_anticausal(ext, cur, next_first, w, taps, tt, x=None, acc_w=None, cols=None):
    hr = 8 * (taps - 1)
    cs = slice(None) if cols is None else cols
    ext[0:tt, cs] = cur
    ext[tt:tt + hr, cs] = _halo_after(cur[0:hr, :], next_first)
    acc = None
    for k in range(taps):
        off = 8 * (taps - 1 - k)
        ld = ext[off:off + tt, cs]
        term = w[k:k + 1, :] * ld
        acc = term if acc is None else acc + term
        if x is not None:
            acc_w[k, :, cs] += _rowsum8(ld * x)
    return acc


def _dot_exact(a, b, dims):
    return lax.dot_general(a, b, (dims, ((), ())), precision=lax.Precision.HIGHEST, preferred_element_type=F32)


def _to_tile_order(perm, wt_ref, w_scr, transpose):
    pb = perm.astype(MM_DTYPE)
    for h in range(N_HEADS_B):
        half = (_dot_nt(pb, wt_ref[h]) if transpose else _dot(pb, wt_ref[h])).astype(MM_DTYPE)
        w_scr[h] = _dot_nt(half, pb).astype(MM_DTYPE)


def _project_rows(y, w_ref):
    r = w_ref.shape[1]
    acc = _dot(y[:, 0:r], w_ref[0])
    for j in range(1, N_CHIPS):
        acc = acc + _dot(y[:, r * j:r * (j + 1)], w_ref[j])
    return acc


PAIR = 2 * HEAD


def _pair_lanes(h):
    return slice(PAIR * (h // 2), PAIR * (h // 2 + 1))


def _per_head(fn):
    cols = []
    for p in range(N_HEADS_B // 2):
        lo, hi = fn(2 * p), fn(2 * p + 1)
        cols.append(jnp.where(lax.broadcasted_iota(jnp.int32, lo.shape, 1) < HEAD, lo, hi))
    return jnp.concatenate(cols, axis=1)


def _mixer_forward(z, prm, q, yc):
    _, lng, lnb, wm, bias_p, _, _, clg, clb = prm
    bg = z[:, 0:D_A]
    ya = bg * q
    o_b = 3 * D_A
    zu = z[:, o_b:o_b + D_B]
    zv = z[:, o_b + D_B:o_b + 2 * D_B]
    u = _gelu(zu)
    vh, rv = _ln_fwd(_gelu(zv))
    vnb = (vh * lng + lnb).astype(MM_DTYPE)
    s = _per_head(lambda h: _dot(wm[h], vnb[:, _pair_lanes(h)])) + bias_p
    yb = u * s
    yh, rc = _ln_fwd(yc)
    l = yh * clg + clb
    sl = jax.nn.sigmoid(l)
    return dict(bg=bg, q=q, ya=ya, zu=zu, zv=zv, u=u, vh=vh, rv=rv, vnb=vnb, s=s, yb=yb, yh=yh, rc=rc, l=l, sl=sl,
                yo=l * sl)


def _conv_inputs(z):
    o_c = 3 * D_A + 2 * D_B
    a = z[:, o_c:o_c + D_C]
    sg = jax.nn.sigmoid(z[:, o_c + D_C:o_c + 2 * D_C])
    return z[:, D_A:2 * D_A] * z[:, 2 * D_A:3 * D_A], a * sg, a, sg


def _group_norm(f, gg):
    ya, yb, yo = f["ya"], f["yb"], f["yo"]
    ra, rb, ro = _rstd(ya), _rstd(yb), _rstd(yo)
    yn = jnp.concatenate([ya * ra * gg[:, 0:D_A], yb * rb * gg[:, D_A:D_A + D_B], yo * ro * gg[:, D_A + D_B:]], axis=1)
    return yn, (ra, rb, ro)


def _mixer_prm(refs, wp_scr, bias_scr):
    caw_ref, lng_ref, lnb_ref, _, _, ccw_ref, ccb_ref, clg_ref, clb_ref = refs
    wm = [wp_scr[h] for h in range(N_HEADS_B)]
    return (caw_ref[...], lng_ref[...], lnb_ref[...], wm, bias_scr[...], ccw_ref[...], ccb_ref[...], clg_ref[...],
            clb_ref[...])


def _mixer_param_specs(tt):
    return [_const_spec((8, D_A)), _const_spec((1, D_B)), _const_spec((1, D_B)), _const_spec((N_HEADS_B, tt, tt)),
            _const_spec((tt, D_B)), _const_spec((32, D_C)), _const_spec((1, D_C)), _const_spec((1, D_C)),
            _const_spec((1, D_C))]


HR_A = 8 * (K_A - 1)
HR_C = 8 * (K_C - 1)
HR_F = 8 * (K_F - 1)


def mixer_fwd(x, g, wig, mp, perm, grp_g, wog, post_g, tt, rider=None):
    t = x.shape[0]
    assert t % tt == 0 and tt % CHUNK == 0 and tt >= HR_C, (t, tt)
    cn = _col_chunk(D_IN)

    def body(x_ref, g_ref, wi_ref, *rest):
        prm_refs = rest[:9]
        (perm_ref, gg_ref, wo_ref, pg_ref, z_ref, h_ref, o_ref, x1_ref, cv_ref, pa_ext, yg_ext, pa_last, yg_last, wp_scr,
         bias_scr, wi_scr) = rest[9:]
        i = pl.program_id(0)

        @pl.when(i == 0)
        def _():
            pa_last[...] = jnp.zeros_like(pa_last)
            yg_last[...] = jnp.zeros_like(yg_last)
            _to_tile_order(perm_ref[...], prm_refs[3], wp_scr, False)
            bias_scr[...] = _dot_exact(perm_ref[...], prm_refs[4][...], ((1,), (0,)))
            _join_col_blocks(wi_ref, wi_scr)

        xv = x_ref[...]
        h = (xv * _rstd(xv) * g_ref[...]).astype(MM_DTYPE)
        h_ref[...] = h
        for c0 in range(0, D_IN, cn):
            z_ref[:, c0:c0 + cn] = _dot(h, wi_scr[:, c0:c0 + cn])
        zv = z_ref[...]
        prm = _mixer_prm(prm_refs, wp_scr, bias_scr)
        pa, yg, _, _ = _conv_inputs(zv)
        q = _conv_causal(pa_ext, pa, pa_last[...], prm[0], K_A, tt)
        yc = _conv_causal(yg_ext, yg, yg_last[...], prm[5], K_C, tt) + prm[6]
        pa_last[...] = pa[tt - HR_A:, :]
        yg_last[...] = yg[tt - HR_C:, :]
        cv_ref[:, 0:D_A] = q
        cv_ref[:, D_A:] = yc
        f = _mixer_forward(zv, prm, q, yc)
        yn, _ = _group_norm(f, gg_ref[...])
        o = _project_rows(yn.astype(MM_DTYPE), wo_ref)
        o_ref[...] = o
        x1_ref[...] = xv + o * _rstd(o) * pg_ref[...]

    row = lambda c: pl.BlockSpec((tt, c), lambda i: (i, 0))
    return _pallas(
        body, rider, name="mixer_fwd", steps=t // tt,
        in_specs=[row(D_MODEL), _const_spec((1, D_MODEL)), _weight_spec(wig)] + _mixer_param_specs(tt)
        + [_const_spec((tt, tt)), _const_spec((1, D_MODEL)), _weight_spec(wog), _const_spec((1, D_MODEL))],
        out_specs=[row(D_IN), row(D_MODEL), row(D_MODEL), row(D_MODEL), row(D_A + D_C)],
        out_shape=[jax.ShapeDtypeStruct((t, D_IN), F32), jax.ShapeDtypeStruct((t, D_MODEL), MM_DTYPE),
                   jax.ShapeDtypeStruct((t, D_MODEL), F32), jax.ShapeDtypeStruct((t, D_MODEL), F32),
                   jax.ShapeDtypeStruct((t, D_A + D_C), F32)],
        scratch_shapes=[pltpu.VMEM((HR_A + tt, D_A), F32), pltpu.VMEM((HR_C + tt, D_C), F32),
                        pltpu.VMEM((HR_A, D_A), F32), pltpu.VMEM((HR_C, D_C), F32),
                        pltpu.VMEM((N_HEADS_B, tt, tt), MM_DTYPE), pltpu.VMEM((tt, D_B), F32),
                        pltpu.VMEM((D_MODEL, D_IN), MM_DTYPE)],
        args=(x, g, wig, *mp, perm, grp_g, wog, post_g))


def mixer_bwd(dx1, o, z, cv, mp, perm, grp_g, wog, post_g, tt, rider=None):
    t = z.shape[0]
    assert t % tt == 0 and tt % CHUNK == 0 and tt >= HR_C, (t, tt)
    steps = t // tt

    def body(dx1_ref, o_ref, z_ref, cv_ref, *rest):
        prm_refs = rest[:9]
        (perm_ref, gg_ref, wo_ref, pg_ref,
         dz_ref, gwo_ref, dpg_ref, dgg_ref, dcaw_ref, dlng_ref, dlnb_ref, dwm_ref, dbias_ref, dccw_ref, dccb_ref,
         dclg_ref, dclb_ref,
         dq_ext, dyc_ext, dq_first, dyc_first, a_pg, a_gg, a_caw, a_lng, a_lnb, a_ccw, a_ccb, a_clg, a_clb,
         wp_scr, wpt_scr, bias_scr, a_wm, a_bias) = rest[9:]
        i = pl.program_id(0)
        small = (a_pg, a_gg, a_caw, a_lng, a_lnb, a_ccw, a_ccb, a_clg, a_clb)

        @pl.when(i == 0)
        def _():
            for ref in small + (a_wm, a_bias, dq_first, dyc_first, gwo_ref):
                ref[...] = jnp.zeros_like(ref)
            _to_tile_order(perm_ref[...], prm_refs[3], wp_scr, False)
            _to_tile_order(perm_ref[...], prm_refs[3], wpt_scr, True)
            bias_scr[...] = _dot_exact(perm_ref[...], prm_refs[4][...], ((1,), (0,)))

        prm = _mixer_prm(prm_refs, wp_scr, bias_scr)
        caw, lng, lnb, wm, bias_p, ccw, ccb, clg, clb = prm

        zv = z_ref[...]
        pa, yg, a, sg = _conv_inputs(zv)
        f = _mixer_forward(zv, prm, cv_ref[:, 0:D_A], cv_ref[:, D_A:])
        gg = gg_ref[...]
        yn, (ra, rb, ro) = _group_norm(f, gg)

        ov = o_ref[...]
        dx1v = dx1_ref[...]
        r_o = _rstd(ov)
        pg = pg_ref[...]
        a_pg[...] += _rowsum8(dx1v * ov * r_o)
        do = _rms_bwd(ov, r_o, pg, dx1v).astype(MM_DTYPE)
        gwo = _dot_tn(yn.astype(MM_DTYPE), do)
        rows = gwo_ref.shape[1]
        for j in range(N_CHIPS):
            gwo_ref[j] += gwo[rows * j:rows * (j + 1), :]
        dyn = jnp.concatenate([_dot_nt(do, wo_ref[j]) for j in range(N_CHIPS)], axis=1)

        dyn_a, dyn_b, dyn_c = dyn[:, 0:D_A], dyn[:, D_A:D_A + D_B], dyn[:, D_A + D_B:]
        ga, gb, gc = gg[:, 0:D_A], gg[:, D_A:D_A + D_B], gg[:, D_A + D_B:]
        a_gg[...] += _rowsum8(jnp.concatenate([dyn_a * f["ya"] * ra, dyn_b * f["yb"] * rb, dyn_c * f["yo"] * ro], axis=1))
        dya = _rms_bwd(f["ya"], ra, ga, dyn_a)
        dyb = _rms_bwd(f["yb"], rb, gb, dyn_b)
        dyo = _rms_bwd(f["yo"], ro, gc, dyn_c)

        dbg = dya * f["q"]
        dq = dya * f["bg"]
        dp = _conv_anticausal(dq_ext, dq, dq_first[...], caw, K_A, tt, x=pa, acc_w=a_caw)
        dq_first[...] = dq[0:HR_A, :]
        dcg = dp * zv[:, 2 * D_A:3 * D_A]
        dxa = dp * zv[:, D_A:2 * D_A]

        du = dyb * f["s"]
        ds = dyb * f["u"]
        dsb = ds.astype(MM_DTYPE)
        a_bias[...] += ds
        odd = lax.broadcasted_iota(jnp.int32, (tt, PAIR), 1) // HEAD
        for h in range(N_HEADS_B):
            dsp = dsb[:, _pair_lanes(h)]
            a_wm[h] += _dot_nt(jnp.where(odd == h % 2, dsp, jnp.zeros_like(dsp)), f["vnb"][:, _pair_lanes(h)])
        dvn = _per_head(lambda h: _dot(wpt_scr[h], dsb[:, _pair_lanes(h)]))
        a_lng[...] += _rowsum8(dvn * f["vh"])
        a_lnb[...] += _rowsum8(dvn)
        dv = _ln_bwd(f["vh"], f["rv"], dvn * lng)
        dzu = du * _gelu_grad(f["zu"])
        dzv = dv * _gelu_grad(f["zv"])

        l, sl = f["l"], f["sl"]
        dl = dyo * (sl * (1.0 + l * (1.0 - sl)))
        a_clg[...] += _rowsum8(dl * f["yh"])
        a_clb[...] += _rowsum8(dl)
        dyc = _ln_bwd(f["yh"], f["rc"], dl * clg)
        a_ccb[...] += _rowsum8(dyc)
        dy = _conv_anticausal(dyc_ext, dyc, dyc_first[...], ccw, K_C, tt, x=yg, acc_w=a_ccw)
        dyc_first[...] = dyc[0:HR_C, :]
        da = dy * sg
        dg = dy * a * sg * (1.0 - sg)

        dz_ref[...] = jnp.concatenate([dbg, dcg, dxa, dzu, dzv, da, dg], axis=1).astype(MM_DTYPE)

        @pl.when(i == steps - 1)
        def _():
            red = lambda ref: jnp.sum(ref[...], axis=0, keepdims=True)
            dpg_ref[...] = red(a_pg)
            dgg_ref[...] = red(a_gg)
            dlng_ref[...] = red(a_lng)
            dlnb_ref[...] = red(a_lnb)
            dccb_ref[...] = red(a_ccb)
            dclg_ref[...] = red(a_clg)
            dclb_ref[...] = red(a_clb)
            dcaw_ref[...] = jnp.sum(a_caw[...], axis=1)
            dccw_ref[...] = jnp.sum(a_ccw[...], axis=1)
            pm = perm_ref[...]
            tril = lax.broadcasted_iota(jnp.int32, (CHUNK, CHUNK), 0) >= lax.broadcasted_iota(jnp.int32, (CHUNK, CHUNK), 1)
            for h in range(N_HEADS_B):
                dwt = _dot_exact(pm, _dot_exact(a_wm[h], pm, ((1,), (0,))), ((0,), (0,)))
                dw = dwt[0:CHUNK, 0:CHUNK]
                for c in range(1, tt // CHUNK):
                    dw = dw + dwt[c * CHUNK:(c + 1) * CHUNK, c * CHUNK:(c + 1) * CHUNK]
                dwm_ref[h] = jnp.where(tril, dw, 0.0)
            dbt = _dot_exact(pm, a_bias[...], ((0,), (0,)))
            db = dbt[0:CHUNK, :]
            for c in range(1, tt // CHUNK):
                db = db + dbt[c * CHUNK:(c + 1) * CHUNK, :]
            lane_head = lax.broadcasted_iota(jnp.int32, (D_B, CHUNK), 0) // HEAD
            fold = (lane_head == lax.broadcasted_iota(jnp.int32, (D_B, CHUNK), 1)).astype(F32)
            dbias_ref[...] = _dot_exact(db, fold, ((1,), (0,)))

    rev = lambda c: pl.BlockSpec((tt, c), lambda i: (steps - 1 - i, 0))
    full = lambda shape: pl.BlockSpec(shape, lambda i: (0,) * len(shape))
    sds = jax.ShapeDtypeStruct
    return _pallas(
        body, rider, name="mixer_bwd", steps=steps,
        in_specs=[rev(D_MODEL), rev(D_MODEL), rev(D_IN), rev(D_A + D_C)] + _mixer_param_specs(tt)
        + [_const_spec((tt, tt)), _const_spec((1, D_MODEL)), _weight_spec(wog), _const_spec((1, D_MODEL))],
        out_specs=[rev(D_IN), full((N_CHIPS, D_MODEL // N_CHIPS, D_MODEL)), full((1, D_MODEL)), full((1, D_MODEL)), full((8, D_A)),
                   full((1, D_B)), full((1, D_B)), full((N_HEADS_B, CHUNK, CHUNK)), full((CHUNK, CHUNK)), full((32, D_C)),
                   full((1, D_C)), full((1, D_C)), full((1, D_C))],
        out_shape=[sds((t, D_IN), MM_DTYPE), sds((N_CHIPS, D_MODEL // N_CHIPS, D_MODEL), F32),
                   sds((1, D_MODEL), F32), sds((1, D_MODEL), F32), sds((8, D_A), F32), sds((1, D_B), F32), sds((1, D_B), F32),
                   sds((N_HEADS_B, CHUNK, CHUNK), F32), sds((CHUNK, CHUNK), F32), sds((32, D_C), F32), sds((1, D_C), F32),
                   sds((1, D_C), F32), sds((1, D_C), F32)],
        scratch_shapes=[pltpu.VMEM((tt + HR_A, D_A), F32), pltpu.VMEM((tt + HR_C, D_C), F32),
                        pltpu.VMEM((HR_A, D_A), F32), pltpu.VMEM((HR_C, D_C), F32),
                        pltpu.VMEM((8, D_MODEL), F32), pltpu.VMEM((8, D_MODEL), F32), pltpu.VMEM((8, 8, D_A), F32),
                        pltpu.VMEM((8, D_B), F32), pltpu.VMEM((8, D_B), F32), pltpu.VMEM((32, 8, D_C), F32),
                        pltpu.VMEM((8, D_C), F32), pltpu.VMEM((8, D_C), F32), pltpu.VMEM((8, D_C), F32),
                        pltpu.VMEM((N_HEADS_B, tt, tt), MM_DTYPE), pltpu.VMEM((N_HEADS_B, tt, tt), MM_DTYPE),
                        pltpu.VMEM((tt, D_B), F32), pltpu.VMEM((N_HEADS_B, tt, tt), F32), pltpu.VMEM((tt, D_B), F32)],
        args=(dx1, o, z, cv, *mp, perm, grp_g, wog, post_g))


def _fetch_row_blocks(wg_ref, w_scr, sems):
    r = wg_ref.shape[1]
    copies = [pltpu.make_async_copy(wg_ref.at[j], w_scr.at[pl.ds(r * j, r), :], sems.at[j]) for j in range(N_CHIPS)]
    for cp in copies:
        cp.start()
    for cp in copies:
        cp.wait()


def _ffn_conv(ext, cw, c0, cn, tt):
    acc = cw[0:1, c0:c0 + cn] * ext[0:tt, c0:c0 + cn]
    for k in range(1, K_F):
        acc = acc + cw[k:k + 1, c0:c0 + cn] * ext[8 * k:8 * k + tt, c0:c0 + cn]
    return acc


def ffn_fwd(x1, g, wug, cw, wdg, post_g, tt, rider=None):
    t, dm = x1.shape
    assert t % tt == 0, (t, tt)
    cn = _col_chunk(D_FF)
    cu = wug.shape[2]

    def body(x1_ref, g_ref, wu_ref, cw_ref, wdg_ref, pg_ref, up0_ref, h2_ref, d_ref, x2_ref, ext, last, wd_ref, sems):
        i = pl.program_id(0)

        @pl.when(i == 0)
        def _():
            _fetch_row_blocks(wdg_ref, wd_ref, sems)
            last[...] = jnp.zeros_like(last)

        xv = x1_ref[...]
        h = (xv * _rstd(xv) * g_ref[...]).astype(MM_DTYPE)
        h2_ref[...] = h
        for j in range(N_CHIPS):
            u = _dot(h, wu_ref[j])
            up0_ref[:, cu * j:cu * (j + 1)] = u
            ext[HR_F:HR_F + tt, cu * j:cu * (j + 1)] = u
        ext[0:HR_F, :] = _halo_before(ext[tt:tt + HR_F, :], last[...])
        last[...] = ext[tt:tt + HR_F, :]
        cwv = cw_ref[...]
        d = jnp.zeros((tt, D_MODEL), F32)
        for c0 in range(0, D_FF, cn):
            gate = _ffn_conv(ext, cwv, c0, cn, tt)
            val = _ffn_conv(ext, cwv, D_FF + c0, cn, tt)
            act = (gate * jax.nn.sigmoid(gate) * val).astype(MM_DTYPE)
            d = d + _dot(act, wd_ref[c0:c0 + cn, :])
        d_ref[...] = d
        x2_ref[...] = xv + d * _rstd(d) * pg_ref[...]

    row = lambda c: pl.BlockSpec((tt, c), lambda i: (i, 0))
    return _pallas(
        body, rider, name="ffn_fwd", steps=t // tt,
        in_specs=[row(dm), _const_spec((1, dm)), _weight_spec(wug), _const_spec((8, 2 * D_FF)), _ANY, _const_spec((1, dm))],
        out_specs=[row(2 * D_FF), row(dm), row(dm), row(dm)],
        out_shape=[jax.ShapeDtypeStruct((t, 2 * D_FF), F32), jax.ShapeDtypeStruct((t, dm), MM_DTYPE),
                   jax.ShapeDtypeStruct((t, dm), F32), jax.ShapeDtypeStruct((t, dm), F32)],
        scratch_shapes=[pltpu.VMEM((HR_F + tt, 2 * D_FF), F32), pltpu.VMEM((HR_F, 2 * D_FF), F32),
                        pltpu.VMEM((D_FF, D_MODEL), MM_DTYPE), pltpu.SemaphoreType.DMA((N_CHIPS,))],
        args=(x1, g, wug, cw, wdg, post_g))


def ffn_bwd(dx2, d, up0, cw, wdg, post_g, tt, rider=None):
    t = up0.shape[0]
    assert t % tt == 0, (t, tt)
    steps = t // tt
    hb = tt // HR_F
    cn = _col_chunk(D_FF)

    def body(dx2_ref, d_ref, up0_ref, uh_ref, cw_ref, wdg_ref, pg_ref,
             dd_ref, act_ref, dup0_ref, dpg_ref, dcw_ref, ext, dup_ext, first, a_pg, a_cw, wd_ref, sems):
        i = pl.program_id(0)
        tile = steps - 1 - i

        @pl.when(i == 0)
        def _():
            _fetch_row_blocks(wdg_ref, wd_ref, sems)
            a_pg[...] = jnp.zeros_like(a_pg)
            a_cw[...] = jnp.zeros_like(a_cw)
            first[...] = jnp.zeros_like(first)

        ext[HR_F:HR_F + tt, :] = up0_ref[...]
        ext[0:HR_F, :] = _halo_before(up0_ref[tt - HR_F:, :], jnp.where(tile > 0, uh_ref[...], 0.0))
        cwv = cw_ref[...]
        dv = d_ref[...]
        dx2v = dx2_ref[...]
        r = _rstd(dv)
        a_pg[...] += _rowsum8(dx2v * dv * r)
        dd = _rms_bwd(dv, r, pg_ref[...], dx2v).astype(MM_DTYPE)
        dd_ref[...] = dd
        for c0 in range(0, D_FF, cn):
            gate = _ffn_conv(ext, cwv, c0, cn, tt)
            val = _ffn_conv(ext, cwv, D_FF + c0, cn, tt)
            sg = jax.nn.sigmoid(gate)
            sl = gate * sg
            act_ref[:, c0:c0 + cn] = (sl * val).astype(MM_DTYPE)
            da = _dot_nt(dd, wd_ref[c0:c0 + cn, :])
            dup_ext[0:tt, c0:c0 + cn] = da * val * (sg * (1.0 + gate * (1.0 - sg)))
            dup_ext[0:tt, D_FF + c0:D_FF + c0 + cn] = da * sl
        dup_ext[tt:tt + HR_F, :] = _halo_after(dup_ext[0:HR_F, :], first[...])
        first[...] = dup_ext[0:HR_F, :]
        for c0 in range(0, 2 * D_FF, cn):
            x = up0_ref[:, c0:c0 + cn]
            acc = None
            for k in range(K_F):
                off = 8 * (K_F - 1 - k)
                ld = dup_ext[off:off + tt, c0:c0 + cn]
                term = cwv[k:k + 1, c0:c0 + cn] * ld
                acc = term if acc is None else acc + term
                a_cw[k, :, c0:c0 + cn] += _rowsum8(ld * x)
            dup0_ref[:, c0:c0 + cn] = acc.astype(MM_DTYPE)

        @pl.when(i == steps - 1)
        def _():
            dpg_ref[...] = jnp.sum(a_pg[...], axis=0, keepdims=True)
            dcw_ref[...] = jnp.sum(a_cw[...], axis=1)

    rev = lambda c: pl.BlockSpec((tt, c), lambda i: (steps - 1 - i, 0))
    halo = pl.BlockSpec((HR_F, 2 * D_FF), lambda i: (jnp.maximum((steps - 1 - i) * hb - 1, 0), 0))
    full = lambda shape: pl.BlockSpec(shape, lambda i: (0,) * len(shape))
    sds = jax.ShapeDtypeStruct
    return _pallas(
        body, rider, name="ffn_bwd", steps=steps,
        in_specs=[rev(D_MODEL), rev(D_MODEL), rev(2 * D_FF), halo, _const_spec((8, 2 * D_FF)), _ANY,
                  _const_spec((1, D_MODEL))],
        out_specs=[rev(D_MODEL), rev(D_FF), rev(2 * D_FF), full((1, D_MODEL)), full((8, 2 * D_FF))],
        out_shape=[sds((t, D_MODEL), MM_DTYPE), sds((t, D_FF), MM_DTYPE), sds((t, 2 * D_FF), MM_DTYPE),
                   sds((1, D_MODEL), F32), sds((8, 2 * D_FF), F32)],
        scratch_shapes=[pltpu.VMEM((HR_F + tt, 2 * D_FF), F32), pltpu.VMEM((tt + HR_F, 2 * D_FF), F32),
                        pltpu.VMEM((HR_F, 2 * D_FF), F32), pltpu.VMEM((8, D_MODEL), F32),
                        pltpu.VMEM((8, 8, 2 * D_FF), F32), pltpu.VMEM((D_FF, D_MODEL), MM_DTYPE),
                        pltpu.SemaphoreType.DMA((N_CHIPS,))],
        args=(dx2, d, up0, up0, cw, wdg, post_g))


def loss_head(y, target, tm):
    t, d = y.shape
    assert t % tm == 0, (t, tm)
    steps = t // tm

    def body(y_ref, t_ref, dy_ref, loss_ref, acc):
        i = pl.program_id(0)

        @pl.when(i == 0)
        def _():
            acc[...] = jnp.zeros_like(acc)

        diff = y_ref[...] - t_ref[...]
        dy_ref[...] = diff * (1.0 / d)
        acc[...] += _rowsum8(diff * diff)

        @pl.when(i == steps - 1)
        def _():
            loss_ref[...] = (0.5 / d) * jnp.sum(jnp.sum(acc[...], axis=0, keepdims=True), axis=1, keepdims=True)

    row = pl.BlockSpec((tm, d), lambda i: (i, 0))
    return pl.pallas_call(
        body, name="loss_head", grid=(steps,), in_specs=[row, row],
        out_specs=[row, pl.BlockSpec((1, 1), lambda i: (0, 0))],
        out_shape=[jax.ShapeDtypeStruct((t, d), F32), jax.ShapeDtypeStruct((1, 1), F32)],
        scratch_shapes=[pltpu.VMEM((8, d), F32)],
        compiler_params=_params(("arbitrary",)),
    )(y, target)


def adamw(w, g, m, v):
    shape = w.shape
    cols = shape[-1]
    rows = w.size // cols
    tr = next((r for r in (512, 256, 128) if rows % r == 0 and rows > r), rows)
    c1 = 1.0 - ADAM_B1 ** ADAM_STEP
    c2 = 1.0 - ADAM_B2 ** ADAM_STEP

    def body(w_ref, g_ref, m_ref, v_ref, d_ref, nm_ref, nv_ref):
        gv = g_ref[...]
        nm = ADAM_B1 * m_ref[...] + (1.0 - ADAM_B1) * gv
        nv = ADAM_B2 * v_ref[...] + (1.0 - ADAM_B2) * (gv * gv)
        nm_ref[...] = nm
        nv_ref[...] = nv
        d_ref[...] = -ADAM_LR * ((nm / c1) / (jnp.sqrt(nv / c2) + ADAM_EPS) + ADAM_WD * w_ref[...])

    spec = pl.BlockSpec((tr, cols), lambda i: (i, 0))
    out = jax.ShapeDtypeStruct((rows, cols), F32)
    res = pl.pallas_call(
        body, name="adamw", grid=(rows // tr,), in_specs=[spec] * 4, out_specs=[spec] * 3, out_shape=[out] * 3,
        compiler_params=_params(("arbitrary",)),
    )(*[a.reshape(rows, cols) for a in (w, g, m, v)])
    return tuple(r.reshape(shape) for r in res)


def _place():
    return lax.axis_index("x"), lax.axis_index("y"), lax.axis_index("c")


def _other_chips(x, y):
    return [(1 - x, y, 2 * (1 - x) + y), (x, 1 - y, 2 * x + 1 - y), (1 - x, 1 - y, 2 * (1 - x) + 1 - y)]


def _sem_specs(*counts):
    return [pltpu.SemaphoreType.DMA((n,)) for n in counts]


def cast_shard(w, layer, chip):
    _, r, c = w.shape

    def body(chip_ref, w_ref, o_ref):
        del chip_ref
        o_ref[...] = w_ref[...].astype(MM_DTYPE)

    grid_spec = pltpu.PrefetchScalarGridSpec(
        num_scalar_prefetch=1, grid=(1,), in_specs=[pl.BlockSpec((None, r, c), lambda i, chip_ref: (layer, 0, 0))],
        out_specs=pl.BlockSpec((None, r, c), lambda i, chip_ref: (chip_ref[0], 0, 0)))
    return pl.pallas_call(
        body, name="cast_shard", grid_spec=grid_spec, out_shape=jax.ShapeDtypeStruct((N_CHIPS, r, c), MM_DTYPE),
        compiler_params=_params(("arbitrary",)),
    )(jnp.reshape(chip, (1,)).astype(jnp.int32), w)


def _row_half(buf, chip, mine, c):
    rh = buf.shape[1] // 2
    return buf.at[chip, pl.ds(pl.multiple_of((c if mine else 1 - c) * rh, 16), rh), :]


def spread_rider(bufs):
    n = len(bufs)

    def start(rin, rout, sems):
        x, y, c = _place()
        me = 2 * x + y
        for k, (px, py, _) in enumerate(_other_chips(x, y)):
            for i, buf in enumerate(rout):
                part = _row_half(buf, me, True, c)
                pltpu.make_async_remote_copy(
                    src_ref=part, dst_ref=part, send_sem=sems[0].at[n * k + i], recv_sem=sems[1].at[n * k + i],
                    device_id=(px, py, c), device_id_type=MESH_ID).start()

    def wait(rin, rout, sems):
        x, y, c = _place()
        for k, (_, _, pj) in enumerate(_other_chips(x, y)):
            for i, buf in enumerate(rout):
                part = _row_half(buf, pj, True, c)
                pltpu.make_async_remote_copy(
                    src_ref=part, dst_ref=part, send_sem=sems[0].at[n * k + i], recv_sem=sems[1].at[n * k + i],
                    device_id=(x, y, c), device_id_type=MESH_ID).wait()

    shapes = [jax.ShapeDtypeStruct(b.shape, b.dtype) for b in bufs]
    return Rider("spread", list(bufs), shapes, {i: i for i in range(n)}, (3 * n, 3 * n), start, wait)


def pass_rider(bufs):
    n = len(bufs)

    def start(rin, rout, sems):
        x, y, c = _place()
        for k, (_, _, pj) in enumerate(_other_chips(x, y)):
            for i, buf in enumerate(rout):
                part = _row_half(buf, pj, True, c)
                pltpu.make_async_remote_copy(
                    src_ref=part, dst_ref=part, send_sem=sems[0].at[n * k + i], recv_sem=sems[1].at[n * k + i],
                    device_id=(x, y, 1 - c), device_id_type=MESH_ID).start()

    def wait(rin, rout, sems):
        x, y, c = _place()
        for k, (_, _, pj) in enumerate(_other_chips(x, y)):
            for i, buf in enumerate(rout):
                part = _row_half(buf, pj, False, c)
                pltpu.make_async_remote_copy(
                    src_ref=part, dst_ref=part, send_sem=sems[0].at[n * k + i], recv_sem=sems[1].at[n * k + i],
                    device_id=(x, y, 1 - c), device_id_type=MESH_ID).wait()

    shapes = [jax.ShapeDtypeStruct(b.shape, b.dtype) for b in bufs]
    return Rider("pass", list(bufs), shapes, {i: i for i in range(n)}, (3 * n, 3 * n), start, wait)


def both_riders(a, b):
    na, oa, sa = len(a.inputs), len(a.out_shapes), len(a.sems)

    def start(rin, rout, sems):
        a.start(rin[:na], rout[:oa], sems[:sa])
        b.start(rin[na:], rout[oa:], sems[sa:])

    def wait(rin, rout, sems):
        a.wait(rin[:na], rout[:oa], sems[:sa])
        b.wait(rin[na:], rout[oa:], sems[sa:])

    aliases = dict(a.aliases)
    aliases.update({na + i: oa + o for i, o in b.aliases.items()})
    return Rider(a.name + "_" + b.name, a.inputs + b.inputs, a.out_shapes + b.out_shapes, aliases, a.sems + b.sems,
                 start, wait)


def gather_small(small):
    def body(small_ref, out_ref, send, recv, local):
        x, y, c = _place()
        me = 2 * x + y
        chips = _other_chips(x, y)
        own = pltpu.make_async_copy(small_ref, out_ref.at[me], local.at[0])
        own.start()
        sends = [pltpu.make_async_remote_copy(src_ref=small_ref, dst_ref=out_ref.at[me], send_sem=send.at[k],
                                              recv_sem=recv.at[k], device_id=(px, py, c), device_id_type=MESH_ID)
                 for k, (px, py, _) in enumerate(chips)]
        for cp in sends:
            cp.start()
        for k, (_, _, pj) in enumerate(chips):
            pltpu.make_async_remote_copy(src_ref=small_ref, dst_ref=out_ref.at[pj], send_sem=send.at[k], recv_sem=recv.at[k],
                                         device_id=(x, y, c), device_id_type=MESH_ID).wait_recv()
        for cp in sends:
            cp.wait_send()
        own.wait()

    return pl.pallas_call(
        body, name="gather_small", in_specs=[_ANY], out_specs=_ANY,
        out_shape=jax.ShapeDtypeStruct((N_CHIPS,) + small.shape, small.dtype), scratch_shapes=_sem_specs(3, 3, 1),
        compiler_params=pltpu.CompilerParams(has_side_effects=True),
    )(small)


def swap_rider(gs):
    n = len(gs)

    def copies(rin, rout, sems):
        x, y, c = _place()
        out = []
        for i, (g, got) in enumerate(zip(rin, rout)):
            rh = g.shape[1] // 2
            theirs = pl.ds(pl.multiple_of((1 - c) * rh, 8), rh)
            out.append(pltpu.make_async_remote_copy(
                src_ref=g.at[:, theirs, :], dst_ref=got, send_sem=sems[0].at[i], recv_sem=sems[1].at[i],
                device_id=(x, y, 1 - c), device_id_type=MESH_ID))
        return out

    def start(rin, rout, sems):
        for cp in copies(rin, rout, sems):
            cp.start()

    def wait(rin, rout, sems):
        for cp in copies(rin, rout, sems):
            cp.wait()

    shapes = [jax.ShapeDtypeStruct((g.shape[0], g.shape[1] // 2, g.shape[2]), g.dtype) for g in gs]
    return Rider("swap", list(gs), shapes, {}, (n, n), start, wait)


def scatter_rider(sbs):
    n = len(sbs)

    def start(rin, rout, sems):
        x, y, c = _place()
        me = 2 * x + y
        for k, (px, py, pj) in enumerate(_other_chips(x, y)):
            for i, (sb, got) in enumerate(zip(rin, rout)):
                pltpu.make_async_remote_copy(
                    src_ref=sb.at[pj], dst_ref=got.at[me], send_sem=sems[0].at[n * k + i], recv_sem=sems[1].at[n * k + i],
                    device_id=(px, py, c), device_id_type=MESH_ID).start()

    def wait(rin, rout, sems):
        x, y, c = _place()
        for k, (_, _, pj) in enumerate(_other_chips(x, y)):
            for i, (sb, got) in enumerate(zip(rin, rout)):
                cp = pltpu.make_async_remote_copy(
                    src_ref=sb.at[pj], dst_ref=got.at[pj], send_sem=sems[0].at[n * k + i], recv_sem=sems[1].at[n * k + i],
                    device_id=(x, y, c), device_id_type=MESH_ID)
                cp.wait_recv()
                cp.wait_send()

    shapes = [jax.ShapeDtypeStruct(sb.shape, sb.dtype) for sb in sbs]
    return Rider("scatter", list(sbs), shapes, {}, (3 * n, 3 * n), start, wait)


def join_rider(fs, layers):
    n = len(fs)

    def half(i, f, mine, place):
        x, y, c = place
        rh = f.shape[1] // 2
        block = 2 * x + y if layers[i] is None else layers[i]
        return f.at[block, pl.ds(pl.multiple_of((c if mine else 1 - c) * rh, 8), rh), :]

    def start(rin, rout, sems):
        x, y, c = _place()
        for i, f in enumerate(rout):
            part = half(i, f, True, (x, y, c))
            pltpu.make_async_remote_copy(
                src_ref=part, dst_ref=part, send_sem=sems[0].at[i], recv_sem=sems[1].at[i],
                device_id=(x, y, 1 - c), device_id_type=MESH_ID).start()

    def wait(rin, rout, sems):
        x, y, c = _place()
        for i, f in enumerate(rout):
            part = half(i, f, False, (x, y, c))
            pltpu.make_async_remote_copy(
                src_ref=part, dst_ref=part, send_sem=sems[0].at[i], recv_sem=sems[1].at[i],
                device_id=(x, y, 1 - c), device_id_type=MESH_ID).wait()

    shapes = [jax.ShapeDtypeStruct(f.shape, f.dtype) for f in fs]
    return Rider("join", list(fs), shapes, {i: i for i in range(n)}, (n, n), start, wait)


def add_halves(gs, gots, wire=BF16):
    m = len(gs)
    x, y, c = _place()

    def body(p_ref, *refs):
        ins, outs = refs[:2 * m], refs[2 * m:]
        for i in range(m):
            s = ins[2 * i][...] + ins[2 * i + 1][...]
            outs[2 * i][...] = s.astype(wire)

            @pl.when(pl.program_id(0) == p_ref[0])
            def _(s=s, own_ref=outs[2 * i + 1]):
                own_ref[...] = s

    in_specs, out_specs, out_shape = [], [], []
    for got in gots:
        n, rh, cols = got.shape
        blk = (None, rh, cols)
        in_specs += [pl.BlockSpec(blk, lambda j, p_ref: (j, p_ref[1], 0)), pl.BlockSpec(blk, lambda j, p_ref: (j, 0, 0))]
        out_specs += [pl.BlockSpec(blk, lambda j, p_ref: (j, 0, 0)), pl.BlockSpec((rh, cols), lambda j, p_ref: (0, 0))]
        out_shape += [jax.ShapeDtypeStruct(got.shape, wire), jax.ShapeDtypeStruct((rh, cols), F32)]
    grid_spec = pltpu.PrefetchScalarGridSpec(num_scalar_prefetch=1, grid=(N_CHIPS,), in_specs=in_specs, out_specs=out_specs)
    res = pl.pallas_call(
        body, name="add_halves", grid_spec=grid_spec, out_shape=out_shape, compiler_params=_params(("arbitrary",)),
    )(jnp.stack([2 * x + y, c]).astype(jnp.int32), *[a for pair in zip(gs, gots) for a in pair])
    return [(res[2 * i + 1], res[2 * i]) for i in range(m)]


def add_chips(owns, gots, fbufs, block=None):
    m = len(owns)
    x, y, c = _place()
    me = 2 * x + y

    def body(p_ref, *refs):
        ins, outs = refs[:5 * m], refs[5 * m:]
        for i in range(m):
            s_ref, g1_ref, g2_ref, g3_ref, _ = ins[5 * i:5 * i + 5]
            outs[i][...] = s_ref[...] + g1_ref[...].astype(F32) + g2_ref[...].astype(F32) + g3_ref[...].astype(F32)

    def other(blk, n, k):
        return pl.BlockSpec(blk, lambda i, p_ref: ((p_ref[0] + k) % n, 0, 0))

    in_specs, out_specs, args = [], [], []
    for own, got, fbuf in zip(owns, gots, fbufs):
        n, rh, cols = got.shape
        blk = (None, rh, cols)
        in_specs += [pl.BlockSpec((rh, cols), lambda i, p_ref: (0, 0)), other(blk, n, 1), other(blk, n, 2), other(blk, n, 3),
                     _ANY]
        out_specs.append(pl.BlockSpec(blk, lambda i, p_ref: (p_ref[2], p_ref[1], 0)))
        args += [own, got, got, got, fbuf]
    grid_spec = pltpu.PrefetchScalarGridSpec(num_scalar_prefetch=1, grid=(1,), in_specs=in_specs, out_specs=out_specs)
    return pl.pallas_call(
        body, name="add_chips", grid_spec=grid_spec, out_shape=[jax.ShapeDtypeStruct(f.shape, F32) for f in fbufs],
        input_output_aliases={5 * i + 5: i for i in range(m)}, compiler_params=_params(("arbitrary",)),
    )(jnp.stack([me, c, me if block is None else block]).astype(jnp.int32), *args)


def _pack(arrays, rows):
    flat = jnp.concatenate([a.reshape(-1) for a in arrays])
    return jnp.pad(flat, (0, rows * LANES - flat.size)).reshape(rows, LANES)


def _unpack(buf, shapes):
    flat = buf.reshape(-1)
    out, at = [], 0
    for s in shapes:
        n = math.prod(s)
        out.append(flat[at:at + n].reshape(s))
        at += n
    return out


CONV_SHARDS = [(DEPTH, K_A, D_A // N_CHIPS), (DEPTH, K_C, D_C // N_CHIPS), (DEPTH, K_F, 2 * D_FF // N_CHIPS)]
CONV_ROWS = 32
SMALL_ROWS = 640


def _join_cols(g):
    n, l, r, c = g.shape
    return jnp.transpose(g, (1, 2, 0, 3)).reshape(l, r, n * c)


BIG = ["w_in", "w_out", "w_up", "w_down"]
WIDE = ["w_up", "w_down"]
NARROW = ["w_in", "w_out"]
GATHER_EARLY = ["w_in", "w_out", "w_down"]
GATHER_LATE = ["w_up"]
TILE_MM = 512
TILE_TN = 1024
TILE_EW = 256


def _pad_rows(a, rows):
    return jnp.pad(a, ((0, rows - a.shape[0]), (0, 0)))


def _row(a):
    return a.reshape(1, -1)


def _tile_perm(tt):
    p = lax.broadcasted_iota(jnp.int32, (tt, tt), 0)
    tok = lax.broadcasted_iota(jnp.int32, (tt, tt), 1)
    return ((tt // 8) * (p % 8) + p // 8 == tok).astype(F32)


def _layer_params(wl, tt):
    n = tt // CHUNK
    tril = jnp.tril(jnp.ones((CHUNK, CHUNK), bool))
    wm = jnp.where(tril[None], wl["sgu_w"], 0.0)
    eye = jnp.eye(n, dtype=F32)
    wt = (eye[None, :, None, :, None] * wm[:, None, :, None, :]).reshape(N_HEADS_B, tt, tt)
    bias_e = jnp.repeat(wl["sgu_b"].T, HEAD, axis=1)
    return (_pad_rows(wl["conv_a_w"], 8), _row(wl["sgu_ln_g"]), _row(wl["sgu_ln_b"]), wt.astype(MM_DTYPE),
            jnp.tile(bias_e, (n, 1)), _pad_rows(wl["conv_c_w"], 32), _row(wl["conv_c_b"]), _row(wl["conv_ln_g"]),
            _row(wl["conv_ln_b"]))


def layer_fwd(x, wl, gw, nxt=None, tm=TILE_MM, tt=TILE_EW):
    mp = _layer_params(wl, tt)
    late = bool(gw.get("pass_late"))
    rides = ([pass_rider([gw[n] for n in GATHER_LATE])] if late else []) + (
        [spread_rider([nxt[n] for n in GATHER_EARLY])] if nxt else [])
    ride = both_riders(*rides) if len(rides) == 2 else (rides[0] if rides else None)
    (z, h, o, x1, cv), done = mixer_fwd(x, _row(wl["pre_mix_g"]), gw["w_in"], mp, _tile_perm(tt), _row(wl["grp_norm_g"]),
                                        gw["w_out"], _row(wl["post_mix_g"]), tt, rider=ride)
    if late:
        k = len(GATHER_LATE)
        gw, done = dict(gw, **dict(zip(GATHER_LATE, done[:k]))), done[k:]
    gw = {n: gw[n] for n in BIG}
    ride = both_riders(spread_rider([nxt[n] for n in GATHER_LATE]), pass_rider(list(done))) if nxt else None
    (up0, h2, d, x2), done = ffn_fwd(x1, _row(wl["pre_ffn_g"]), gw["w_up"], _pad_rows(wl["ffn_conv_w"], 8), gw["w_down"],
                                     _row(wl["post_ffn_g"]), tt, rider=ride)
    if nxt:
        nxt = dict(nxt, **dict(zip(GATHER_LATE + GATHER_EARLY, done)), pass_late=True)
    return x2, dict(x=x, z=z, h=h, o=o, x1=x1, up0=up0, h2=h2, d=d, cv=cv, gw=gw), nxt


def layer_bwd(dx2, wl, layer, sv, pend=None, exchange=True, tm=TILE_MM, tt=TILE_EW):
    mp = _layer_params(wl, tt)
    gw = sv["gw"]
    tk = min(TILE_TN, dx2.shape[0])
    at = {n: BIG.index(n) for n in BIG}
    g = {}
    ride = scatter_rider([sw for _, sw in pend["narrow"]]) if pend else None
    (dd, act, dup0, dpg, dcw), arrived = ffn_bwd(dx2, sv["d"], sv["up0"], _pad_rows(wl["ffn_conv_w"], 8), gw["w_down"],
                                                 _row(wl["post_ffn_g"]), tt, rider=ride)
    fbuf = list(pend["fbuf"]) if pend else grad_buffers()
    if pend:
        done = add_chips([own for own, _ in pend["narrow"]], arrived, [fbuf[at[n]] for n in NARROW], pend["layer"])
        for n, f in zip(NARROW, done):
            fbuf[at[n]] = f
    g["post_ffn_g"] = dpg[0]
    g["ffn_conv_w"] = dcw[:K_F]
    gl = {}
    gl["w_down"] = matmul_tn_down(act, dd, tk)
    gl["w_up"] = matmul_tn_cols(sv["h2"], dup0, tk)
    ride = swap_rider([gl[n] for n in WIDE]) if exchange else None
    (dx1, dg), got = matmul_nt_norm_bwd(dup0, gw["w_up"], sv["x1"], _row(wl["pre_ffn_g"]), dx2, tm, rider=ride)
    g["pre_ffn_g"] = dg[0]
    wide = add_halves([gl[n] for n in WIDE], got) if exchange else None
    ride = scatter_rider([sw for _, sw in wide]) if exchange else None
    if pend:
        ride = both_riders(join_rider(fbuf, [pend["layer"]] * len(fbuf)), ride)
    (dz, gl["w_out"], dpg, dgg, dcaw, dlng, dlnb, dwm, dbias, dccw, dccb, dclg, dclb), rode = mixer_bwd(
        dx1, sv["o"], sv["z"], sv["cv"], mp, _tile_perm(tt), _row(wl["grp_norm_g"]), gw["w_out"],
        _row(wl["post_mix_g"]), tt, rider=ride)
    if exchange:
        fbuf, arrived = (list(rode[:len(BIG)]), rode[len(BIG):]) if pend else (fbuf, rode)
        done = add_chips([own for own, _ in wide], arrived, [fbuf[at[n]] for n in WIDE], layer)
        for n, f in zip(WIDE, done):
            fbuf[at[n]] = f
    g["post_mix_g"] = dpg[0]
    g["grp_norm_g"] = dgg[0]
    g["conv_a_w"] = dcaw[:K_A]
    g["sgu_ln_g"] = dlng[0]
    g["sgu_ln_b"] = dlnb[0]
    g["sgu_w"] = dwm
    g["sgu_b"] = dbias[:, :N_HEADS_B].T
    g["conv_c_w"] = dccw[:K_C]
    g["conv_c_b"] = dccb[0]
    g["conv_ln_g"] = dclg[0]
    g["conv_ln_b"] = dclb[0]
    gl["w_in"] = matmul_tn_in(sv["h"], dz, tk)
    ride = swap_rider([gl[n] for n in NARROW]) if exchange else None
    (dx, dg), got = matmul_nt_norm_bwd(dz, gw["w_in"], sv["x"], _row(wl["pre_mix_g"]), dx1, tm, rider=ride)
    g["pre_mix_g"] = dg[0]
    if not exchange:
        return dx, g, gl
    narrow = add_halves([gl[n] for n in NARROW], got)
    return dx, g, dict(narrow=narrow, fbuf=fbuf, layer=layer)


def grad_buffers():
    return [lax.empty(s, F32) for s in ((DEPTH, D_MODEL, D_IN // N_CHIPS), (DEPTH, D_MODEL // N_CHIPS, D_MODEL),
                                        (DEPTH, D_MODEL, 2 * D_FF // N_CHIPS), (DEPTH, D_FF // N_CHIPS, D_MODEL))]


CONV = ["conv_a_w", "conv_c_w", "ffn_conv_w"]
REPL = ["pre_mix_g", "sgu_ln_g", "sgu_ln_b", "sgu_w", "sgu_b", "conv_c_b", "conv_ln_g", "conv_ln_b", "grp_norm_g",
        "post_mix_g", "pre_ffn_g", "post_ffn_g"]
WEIGHTS = ["pre_mix_g", "w_in", "conv_a_w", "sgu_ln_g", "sgu_ln_b", "sgu_w", "sgu_b", "conv_c_w", "conv_c_b", "conv_ln_g",
           "conv_ln_b", "grp_norm_g", "w_out", "post_mix_g", "pre_ffn_g", "w_up", "ffn_conv_w", "w_down", "post_ffn_g"]


def kernel(x, pre_mix_g, w_in, conv_a_w, sgu_ln_g, sgu_ln_b, sgu_w, sgu_b, conv_c_w, conv_c_b, conv_ln_g, conv_ln_b, grp_norm_g, w_out, post_mix_g, pre_ffn_g, w_up, ffn_conv_w, w_down, post_ffn_g, loss_target, m_pre_mix_g, m_w_in, m_conv_a_w, m_sgu_ln_g, m_sgu_ln_b, m_sgu_w, m_sgu_b, m_conv_c_w, m_conv_c_b, m_conv_ln_g, m_conv_ln_b, m_grp_norm_g, m_w_out, m_post_mix_g, m_pre_ffn_g, m_w_up, m_ffn_conv_w, m_w_down, m_post_ffn_g, v_pre_mix_g, v_w_in, v_conv_a_w, v_sgu_ln_g, v_sgu_ln_b, v_sgu_w, v_sgu_b, v_conv_c_w, v_conv_c_b, v_conv_ln_g, v_conv_ln_b, v_grp_norm_g, v_w_out, v_post_mix_g, v_pre_ffn_g, v_w_up, v_ffn_conv_w, v_w_down, v_post_ffn_g):
    w = dict(pre_mix_g=pre_mix_g, w_in=w_in, conv_a_w=conv_a_w, sgu_ln_g=sgu_ln_g, sgu_ln_b=sgu_ln_b, sgu_w=sgu_w, sgu_b=sgu_b,
             conv_c_w=conv_c_w, conv_c_b=conv_c_b, conv_ln_g=conv_ln_g, conv_ln_b=conv_ln_b, grp_norm_g=grp_norm_g,
             w_out=w_out, post_mix_g=post_mix_g, pre_ffn_g=pre_ffn_g, w_up=w_up, ffn_conv_w=ffn_conv_w, w_down=w_down,
             post_ffn_g=post_ffn_g)
    m = dict(pre_mix_g=m_pre_mix_g, w_in=m_w_in, conv_a_w=m_conv_a_w, sgu_ln_g=m_sgu_ln_g, sgu_ln_b=m_sgu_ln_b,
             sgu_w=m_sgu_w, sgu_b=m_sgu_b, conv_c_w=m_conv_c_w, conv_c_b=m_conv_c_b, conv_ln_g=m_conv_ln_g,
             conv_ln_b=m_conv_ln_b, grp_norm_g=m_grp_norm_g, w_out=m_w_out, post_mix_g=m_post_mix_g,
             pre_ffn_g=m_pre_ffn_g, w_up=m_w_up, ffn_conv_w=m_ffn_conv_w, w_down=m_w_down, post_ffn_g=m_post_ffn_g)
    v = dict(pre_mix_g=v_pre_mix_g, w_in=v_w_in, conv_a_w=v_conv_a_w, sgu_ln_g=v_sgu_ln_g, sgu_ln_b=v_sgu_ln_b,
             sgu_w=v_sgu_w, sgu_b=v_sgu_b, conv_c_w=v_conv_c_w, conv_c_b=v_conv_c_b, conv_ln_g=v_conv_ln_g,
             conv_ln_b=v_conv_ln_b, grp_norm_g=v_grp_norm_g, w_out=v_w_out, post_mix_g=v_post_mix_g,
             pre_ffn_g=v_pre_ffn_g, w_up=v_w_up, ffn_conv_w=v_ffn_conv_w, w_down=v_w_down, post_ffn_g=v_post_ffn_g)
    chip = 2 * lax.axis_index("x") + lax.axis_index("y")

    convs = gather_small(_pack([w[n] for n in CONV], CONV_ROWS))
    gws = [{n: cast_shard(w[n], layer, chip) for n in BIG} for layer in range(DEPTH)]
    first = run_rider(pass_rider(run_rider(spread_rider([gws[0][n] for n in BIG]))))
    gws[0] = dict(zip(BIG, first))
    cparts = [_unpack(convs[j], CONV_SHARDS) for j in range(N_CHIPS)]
    full = dict(w)
    for i, n in enumerate(CONV):
        full[n] = _join_cols(jnp.stack([p[i] for p in cparts]))

    xc = to_tiles(x[0], TILE_EW)
    saved = []
    for layer in range(DEPTH):
        nxt = gws[layer + 1] if layer + 1 < DEPTH else None
        xc, sv, nxt = layer_fwd(xc, {n: full[n][layer] for n in REPL + CONV}, gws[layer], nxt)
        if nxt:
            gws[layer + 1] = nxt
        saved.append(sv)
    dxc, loss_part = loss_head(xc, to_tiles(loss_target[0], TILE_EW), TILE_MM)
    loss = lax.psum(loss_part[0, 0], ("x", "y", "c"))
    small = [None] * DEPTH
    pend = None
    for layer in reversed(range(DEPTH)):
        dxc, small[layer], pend = layer_bwd(dxc, {n: full[n][layer] for n in REPL + CONV}, layer, saved[layer], pend)
    grads = {n: jnp.stack([small[layer][n] for layer in range(DEPTH)]) for n in REPL + CONV}

    gsmall = _pack([grads[n] for n in REPL + CONV], SMALL_ROWS).reshape(N_CHIPS, SMALL_ROWS // N_CHIPS, LANES)
    sums = pend["narrow"] + add_halves([gsmall], run_rider(swap_rider([gsmall])), wire=F32)
    arrived = run_rider(scatter_rider([sw for _, sw in sums]))
    fbuf = list(pend["fbuf"])
    at = [BIG.index(n) for n in NARROW]
    for i, f in zip(at, add_chips([own for own, _ in sums[:2]], arrived[:2], [fbuf[i] for i in at], 0)):
        fbuf[i] = f
    fbuf += add_chips([sums[2][0]], arrived[2:], [lax.empty(gsmall.shape, F32)])
    joined = run_rider(join_rider(fbuf, [0] * len(BIG) + [None]))
    out_g = dict(zip(BIG, joined))
    tot = run_rider(pass_rider(run_rider(spread_rider([joined[len(BIG)]]))))[0].reshape(SMALL_ROWS, LANES)
    shapes = [grads[n].shape for n in REPL + CONV]
    for n, gfull in zip(REPL + CONV, _unpack(tot, shapes)):
        if n in CONV:
            width = gfull.shape[-1] // N_CHIPS
            gfull = lax.dynamic_slice_in_dim(gfull, chip * width, width, axis=2)
        out_g[n] = gfull

    deltas, new_m, new_v = {}, {}, {}
    for n in WEIGHTS:
        deltas[n], new_m[n], new_v[n] = adamw(w[n], out_g[n], m[n], v[n])
    return (loss, from_tiles(dxc, TILE_EW)[None], *[out_g[n] for n in WEIGHTS], *[deltas[n] for n in WEIGHTS], *[new_m[n] for n in WEIGHTS],
            *[new_v[n] for n in WEIGHTS])
```

```python
import math
from typing import Callable, NamedTuple

import jax
import jax.numpy as jnp
from jax import lax
from jax.experimental import pallas as pl
from jax.experimental.pallas import tpu as pltpu

F32 = jnp.float32
BF16 = jnp.bfloat16
MM_DTYPE = BF16

D_MODEL = 1024
DEPTH = 4
D_A = 256
D_B = 384
D_C = 384
D_IN = 3 * D_A + 2 * D_B + 2 * D_C
D_FF = 2816
K_A = 3
K_C = 31
K_F = 3
CHUNK = 128
HEAD = 64
N_HEADS_B = D_B // HEAD
EPS = 1e-6
N_CHIPS = 4

ADAM_LR = 0.001
ADAM_B1 = 0.9
ADAM_B2 = 0.999
ADAM_EPS = 1e-08
ADAM_WD = 0.01
ADAM_STEP = 10

LANES = 1024
VMEM_LIMIT = 56 * 1024 * 1024

MESH_ID = pl.DeviceIdType.MESH
_ANY = pl.BlockSpec(memory_space=pl.ANY)


def _params(sem=None):
    return pltpu.CompilerParams(dimension_semantics=sem, vmem_limit_bytes=VMEM_LIMIT)


def _const_spec(shape):
    nd = len(shape)
    return pl.BlockSpec(shape, lambda *_: (0,) * nd, pipeline_mode=pl.Buffered(1))


def _rowsum8(a):
    r, c = a.shape
    return jnp.sum(a.reshape(r // 8, 8, c), axis=0)


def _rstd(x):
    return lax.rsqrt(jnp.mean(x * x, axis=-1, keepdims=True) + EPS)


def _rms_bwd(x, r, g, dy):
    gdy = g * dy
    return r * gdy - x * (r * r * r) * jnp.mean(gdy * x, axis=-1, keepdims=True)


def _ln_fwd(x):
    mu = jnp.mean(x, axis=-1, keepdims=True)
    xc = x - mu
    r = lax.rsqrt(jnp.mean(xc * xc, axis=-1, keepdims=True) + EPS)
    return xc * r, r


def _ln_bwd(xh, r, dxh):
    return r * (dxh - jnp.mean(dxh, axis=-1, keepdims=True) - xh * jnp.mean(dxh * xh, axis=-1, keepdims=True))


def _gelu(x):
    return 0.5 * x * (1.0 + lax.erf(x * (1.0 / math.sqrt(2.0))))


def _gelu_grad(x):
    cdf = 0.5 * (1.0 + lax.erf(x * (1.0 / math.sqrt(2.0))))
    pdf = jnp.exp(-0.5 * x * x) * (1.0 / math.sqrt(2.0 * math.pi))
    return cdf + x * pdf


def _dot(a, b):
    return jnp.dot(a, b, preferred_element_type=F32)


def _dot_nt(a, b):
    return lax.dot_general(a, b, (((1,), (1,)), ((), ())), preferred_element_type=F32)


def _dot_tn(a, b):
    return lax.dot_general(a, b, (((0,), (0,)), ((), ())), preferred_element_type=F32)


def _col_chunk(n):
    for c in (1408, 1024, 768, 512, 256, 128):
        if n % c == 0:
            return c
    raise ValueError(n)


class Rider(NamedTuple):
    name: str
    inputs: list
    out_shapes: list
    aliases: dict
    sems: tuple
    start: Callable
    wait: Callable


def _pallas(body, rider, *, name, steps, in_specs, out_specs, out_shape, scratch_shapes, args, init=None):
    if rider is None:
        def plain(*refs):
            if init is not None:
                pl.when(pl.program_id(0) == 0)(lambda: init(*refs))
            body(*refs)

        res = pl.pallas_call(plain, name=name, grid=(steps,), in_specs=in_specs, out_specs=out_specs, out_shape=out_shape,
                             scratch_shapes=scratch_shapes, compiler_params=_params(("arbitrary",)))(*args)
        return res, []
    n_in, n_out, n_scr = len(in_specs), len(out_specs), len(scratch_shapes)
    r_in, r_out = len(rider.inputs), len(rider.out_shapes)

    def wrapped(*refs):
        ins, rin = refs[:n_in], refs[n_in:n_in + r_in]
        at = n_in + r_in
        outs, rout = refs[at:at + n_out], refs[at + n_out:at + n_out + r_out]
        at += n_out + r_out
        scr, rsem = refs[at:at + n_scr], refs[at + n_scr:]

        @pl.when(pl.program_id(0) == 0)
        def _():
            if init is not None:
                init(*ins, *outs, *scr)
            rider.start(rin, rout, rsem)

        body(*ins, *outs, *scr)

        @pl.when(pl.program_id(0) == steps - 1)
        def _():
            rider.wait(rin, rout, rsem)

    res = pl.pallas_call(
        wrapped, name=name + "_" + rider.name, grid=(steps,), in_specs=list(in_specs) + [_ANY] * r_in,
        out_specs=list(out_specs) + [_ANY] * r_out, out_shape=list(out_shape) + list(rider.out_shapes),
        scratch_shapes=list(scratch_shapes) + [pltpu.SemaphoreType.DMA((n,)) for n in rider.sems],
        input_output_aliases={n_in + i: n_out + o for i, o in rider.aliases.items()},
        compiler_params=pltpu.CompilerParams(dimension_semantics=("arbitrary",), vmem_limit_bytes=VMEM_LIMIT,
                                             has_side_effects=True),
    )(*args, *rider.inputs)
    return res[:n_out], res[n_out:]


def run_rider(rider):
    def body(*refs):
        r_in, r_out = len(rider.inputs), len(rider.out_shapes)
        rin, rout, rsem = refs[:r_in], refs[r_in:r_in + r_out], refs[r_in + r_out:]
        rider.start(rin, rout, rsem)
        rider.wait(rin, rout, rsem)

    return pl.pallas_call(
        body, name=rider.name, in_specs=[_ANY] * len(rider.inputs), out_specs=[_ANY] * len(rider.out_shapes),
        out_shape=list(rider.out_shapes), scratch_shapes=[pltpu.SemaphoreType.DMA((n,)) for n in rider.sems],
        input_output_aliases=dict(rider.aliases), compiler_params=pltpu.CompilerParams(has_side_effects=True),
    )(*rider.inputs)


def _weight_spec(wg):
    return _const_spec(wg.shape)


def _join_col_blocks(w_ref, w_scr):
    c = w_ref.shape[2]
    for j in range(N_CHIPS):
        w_scr[:, c * j:c * (j + 1)] = w_ref[j]


def matmul_nt_norm_bwd(gy, wg, x, g, dres, tm, rider=None):
    t, n = gy.shape
    assert t % tm == 0, (t, tm)
    d, cw = wg.shape[1], wg.shape[2]
    aligned = cw % 128 == 0
    cn = cw if aligned else _col_chunk(n)
    steps = t // tm

    def body(gy_ref, w_ref, x_ref, g_ref, dres_ref, dx_ref, dg_ref, acc_ref, *scr):
        i = pl.program_id(0)

        @pl.when(i == 0)
        def _():
            acc_ref[...] = jnp.zeros_like(acc_ref)
            if not aligned:
                _join_col_blocks(w_ref, scr[0])

        dh = jnp.zeros((tm, d), F32)
        for j, c0 in enumerate(range(0, n, cn)):
            wv = w_ref[j] if aligned else scr[0][:, c0:c0 + cn]
            dh = dh + _dot_nt(gy_ref[:, c0:c0 + cn], wv)
        xv = x_ref[...]
        r = _rstd(xv)
        gv = g_ref[...]
        dx_ref[...] = dres_ref[...] + _rms_bwd(xv, r, gv, dh)
        acc_ref[...] += _rowsum8(dh * xv * r)

        @pl.when(i == steps - 1)
        def _():
            dg_ref[...] = jnp.sum(acc_ref[...], axis=0, keepdims=True)

    return _pallas(
        body, rider, name="matmul_nt_norm_bwd", steps=steps,
        in_specs=[pl.BlockSpec((tm, n), lambda i: (i, 0)), _weight_spec(wg), pl.BlockSpec((tm, d), lambda i: (i, 0)),
                  _const_spec((1, d)), pl.BlockSpec((tm, d), lambda i: (i, 0))],
        out_specs=[pl.BlockSpec((tm, d), lambda i: (i, 0)), pl.BlockSpec((1, d), lambda i: (0, 0))],
        out_shape=[jax.ShapeDtypeStruct((t, d), F32), jax.ShapeDtypeStruct((1, d), F32)],
        scratch_shapes=[pltpu.VMEM((8, d), F32)] + ([] if aligned else [pltpu.VMEM((d, n), MM_DTYPE)]),
        args=(gy, wg, x, g, dres))


def matmul_tn_cols(a, b, tk):
    t, r = a.shape
    c = b.shape[1] // N_CHIPS
    assert t % tk == 0 and r % 8 == 0 and c % 128 == 0, (a.shape, b.shape, tk)

    def body(a_ref, b_ref, o_ref):
        @pl.when(pl.program_id(1) == 0)
        def _():
            o_ref[...] = jnp.zeros_like(o_ref)

        o_ref[...] += _dot_tn(a_ref[...], b_ref[...])

    a_spec = pl.BlockSpec((tk, r), lambda j, k: (k, 0))
    b_spec = pl.BlockSpec((tk, c), lambda j, k: (k, j))
    return pl.pallas_call(
        body, name="matmul_tn_cols", grid=(N_CHIPS, t // tk), in_specs=[a_spec, b_spec],
        out_specs=pl.BlockSpec((None, r, c), lambda j, k: (j, 0, 0)),
        out_shape=jax.ShapeDtypeStruct((N_CHIPS, r, c), F32),
        compiler_params=_params(("arbitrary", "arbitrary")),
    )(a, b)


def matmul_tn_down(act, dd, tk):
    t, m = act.shape
    c = dd.shape[1]
    r = m // N_CHIPS
    assert t % tk == 0, (t, tk)
    steps = t // tk

    def body(a_ref, b_ref, o_ref, acc):
        k = pl.program_id(1)

        @pl.when(k == 0)
        def _():
            acc[...] = jnp.zeros_like(acc)

        acc[...] += _dot_tn(a_ref[...], b_ref[...])

        @pl.when(k == steps - 1)
        def _():
            o_ref[0] = acc[0:r, :]
            o_ref[1] = acc[r:2 * r, :]

    return pl.pallas_call(
        body, name="matmul_tn_down", grid=(2, steps),
        in_specs=[pl.BlockSpec((tk, 2 * r), lambda p, k: (k, p)), pl.BlockSpec((tk, c), lambda p, k: (k, 0))],
        out_specs=pl.BlockSpec((2, r, c), lambda p, k: (p, 0, 0)),
        out_shape=jax.ShapeDtypeStruct((N_CHIPS, r, c), F32),
        scratch_shapes=[pltpu.VMEM((2 * r, c), F32)],
        compiler_params=_params(("arbitrary", "arbitrary")),
    )(act, dd)


def matmul_tn_in(h, dz, tk):
    t, d = h.shape
    n = dz.shape[1]
    c = n // N_CHIPS
    assert t % tk == 0, (t, tk)
    steps = t // tk

    def body(a_ref, b_ref, o_ref, acc):
        k = pl.program_id(0)

        @pl.when(k == 0)
        def _():
            acc[...] = jnp.zeros_like(acc)

        acc[...] += _dot_tn(a_ref[...], b_ref[...])

        @pl.when(k == steps - 1)
        def _():
            for j in range(N_CHIPS):
                o_ref[j] = acc[:, c * j:c * (j + 1)]

    return pl.pallas_call(
        body, name="matmul_tn_in", grid=(steps,),
        in_specs=[pl.BlockSpec((tk, d), lambda k: (k, 0)), pl.BlockSpec((tk, n), lambda k: (k, 0))],
        out_specs=pl.BlockSpec((N_CHIPS, d, c), lambda k: (0, 0, 0)),
        out_shape=jax.ShapeDtypeStruct((N_CHIPS, d, c), F32),
        scratch_shapes=[pltpu.VMEM((d, n), F32)],
        compiler_params=_params(("arbitrary",)),
    )(h, dz)


def to_tiles(a, tt):
    t = a.shape[0]
    return a.reshape((t // tt, 8, tt // 8) + a.shape[1:]).swapaxes(1, 2).reshape(a.shape)


def from_tiles(a, tt):
    t = a.shape[0]
    return a.reshape((t // tt, tt // 8, 8) + a.shape[1:]).swapaxes(1, 2).reshape(a.shape)


def _roll_sublanes(a, shift):
    n = a.shape[0] // 8
    return pltpu.roll(a.reshape(n, 8, a.shape[1]), shift, 1).reshape(a.shape)


def _halo_before(cur_last, prev_last):
    sub = lax.broadcasted_iota(jnp.int32, cur_last.shape, 0) % 8
    return jnp.where(sub == 0, _roll_sublanes(prev_last, 1), _roll_sublanes(cur_last, 1))


def _halo_after(cur_first, next_first):
    sub = lax.broadcasted_iota(jnp.int32, cur_first.shape, 0) % 8
    return jnp.where(sub == 7, _roll_sublanes(next_first, 7), _roll_sublanes(cur_first, 7))


def _conv_causal(ext, cur, prev_last, w, taps, tt, cols=None):
    hr = 8 * (taps - 1)
    cs = slice(None) if cols is None else cols
    ext[hr:hr + tt, cs] = cur
    ext[0:hr, cs] = _halo_before(cur[tt - hr:, :], prev_last)
    acc = w[0:1, :] * ext[0:tt, cs]
    for k in range(1, taps):
        acc = acc + w[k:k + 1, :] * ext[8 * k:8 * k + tt, cs]
    return acc


def _conv_anticausal(ext, cur, next_first, w, taps, tt, x=None, acc_w=None, cols=None):
    hr = 8 * (taps - 1)
    cs = slice(None) if cols is None else cols
    ext[0:tt, cs] = cur
    ext[tt:tt + hr, cs] = _halo_after(cur[0:hr, :], next_first)
    acc = None
    for k in range(taps):
        off = 8 * (taps - 1 - k)
        ld = ext[off:off + tt, cs]
        term = w[k:k + 1, :] * ld
        acc = term if acc is None else acc + term
        if x is not None:
            acc_w[k, :, cs] += _rowsum8(ld * x)
    return acc


def _dot_exact(a, b, dims):
    return lax.dot_general(a, b, (dims, ((), ())), precision=lax.Precision.HIGHEST, preferred_element_type=F32)


def _to_tile_order(perm, wt_ref, w_scr, transpose):
    pb = perm.astype(MM_DTYPE)
    for h in range(N_HEADS_B):
        half = (_dot_nt(pb, wt_ref[h]) if transpose else _dot(pb, wt_ref[h])).astype(MM_DTYPE)
        w_scr[h] = _dot_nt(half, pb).astype(MM_DTYPE)


def _project_rows(y, w_ref):
    r = w_ref.shape[1]
    acc = _dot(y[:, 0:r], w_ref[0])
    for j in range(1, N_CHIPS):
        acc = acc + _dot(y[:, r * j:r * (j + 1)], w_ref[j])
    return acc


PAIR = 2 * HEAD


def _pair_lanes(h):
    return slice(PAIR * (h // 2), PAIR * (h // 2 + 1))


def _per_head(fn):
    cols = []
    for p in range(N_HEADS_B // 2):
        lo, hi = fn(2 * p), fn(2 * p + 1)
        cols.append(jnp.where(lax.broadcasted_iota(jnp.int32, lo.shape, 1) < HEAD, lo, hi))
    return jnp.concatenate(cols, axis=1)


def _mixer_forward(z, prm, q, yc):
    _, lng, lnb, wm, bias_p, _, _, clg, clb = prm
    bg = z[:, 0:D_A]
    ya = bg * q
    o_b = 3 * D_A
    zu = z[:, o_b:o_b + D_B]
    zv = z[:, o_b + D_B:o_b + 2 * D_B]
    u = _gelu(zu)
    vh, rv = _ln_fwd(_gelu(zv))
    vnb = (vh * lng + lnb).astype(MM_DTYPE)
    s = _per_head(lambda h: _dot(wm[h], vnb[:, _pair_lanes(h)])) + bias_p
    yb = u * s
    yh, rc = _ln_fwd(yc)
    l = yh * clg + clb
    sl = jax.nn.sigmoid(l)
    return dict(bg=bg, q=q, ya=ya, zu=zu, zv=zv, u=u, vh=vh, rv=rv, vnb=vnb, s=s, yb=yb, yh=yh, rc=rc, l=l, sl=sl,
                yo=l * sl)


def _conv_inputs(z):
    o_c = 3 * D_A + 2 * D_B
    a = z[:, o_c:o_c + D_C]
    sg = jax.nn.sigmoid(z[:, o_c + D_C:o_c + 2 * D_C])
    return z[:, D_A:2 * D_A] * z[:, 2 * D_A:3 * D_A], a * sg, a, sg


def _group_norm(f, gg):
    ya, yb, yo = f["ya"], f["yb"], f["yo"]
    ra, rb, ro = _rstd(ya), _rstd(yb), _rstd(yo)
    yn = jnp.concatenate([ya * ra * gg[:, 0:D_A], yb * rb * gg[:, D_A:D_A + D_B], yo * ro * gg[:, D_A + D_B:]], axis=1)
    return yn, (ra, rb, ro)


def _mixer_prm(refs, wp_scr, bias_scr):
    caw_ref, lng_ref, lnb_ref, _, _, ccw_ref, ccb_ref, clg_ref, clb_ref = refs
    wm = [wp_scr[h] for h in range(N_HEADS_B)]
    return (caw_ref[...], lng_ref[...], lnb_ref[...], wm, bias_scr[...], ccw_ref[...], ccb_ref[...], clg_ref[...],
            clb_ref[...])


def _mixer_param_specs(tt):
    return [_const_spec((8, D_A)), _const_spec((1, D_B)), _const_spec((1, D_B)), _const_spec((N_HEADS_B, tt, tt)),
            _const_spec((tt, D_B)), _const_spec((32, D_C)), _const_spec((1, D_C)), _const_spec((1, D_C)),
            _const_spec((1, D_C))]


HR_A = 8 * (K_A - 1)
HR_C = 8 * (K_C - 1)
HR_F = 8 * (K_F - 1)


def mixer_fwd(x, g, wig, mp, perm, grp_g, wog, post_g, tt, rider=None):
    t = x.shape[0]
    assert t % tt == 0 and tt % CHUNK == 0 and tt >= HR_C, (t, tt)
    cn = _col_chunk(D_IN)

    def body(x_ref, g_ref, wi_ref, *rest):
        prm_refs = rest[:9]
        (perm_ref, gg_ref, wo_ref, pg_ref, z_ref, h_ref, o_ref, x1_ref, cv_ref, pa_ext, yg_ext, pa_last, yg_last, wp_scr,
         bias_scr, wi_scr) = rest[9:]
        i = pl.program_id(0)

        @pl.when(i == 0)
        def _():
            pa_last[...] = jnp.zeros_like(pa_last)
            yg_last[...] = jnp.zeros_like(yg_last)
            _to_tile_order(perm_ref[...], prm_refs[3], wp_scr, False)
            bias_scr[...] = _dot_exact(perm_ref[...], prm_refs[4][...], ((1,), (0,)))
            _join_col_blocks(wi_ref, wi_scr)

        xv = x_ref[...]
        h = (xv * _rstd(xv) * g_ref[...]).astype(MM_DTYPE)
        h_ref[...] = h
        for c0 in range(0, D_IN, cn):
            z_ref[:, c0:c0 + cn] = _dot(h, wi_scr[:, c0:c0 + cn])
        zv = z_ref[...]
        prm = _mixer_prm(prm_refs, wp_scr, bias_scr)
        pa, yg, _, _ = _conv_inputs(zv)
        q = _conv_causal(pa_ext, pa, pa_last[...], prm[0], K_A, tt)
        yc = _conv_causal(yg_ext, yg, yg_last[...], prm[5], K_C, tt) + prm[6]
        pa_last[...] = pa[tt - HR_A:, :]
        yg_last[...] = yg[tt - HR_C:, :]
        cv_ref[:, 0:D_A] = q
        cv_ref[:, D_A:] = yc
        f = _mixer_forward(zv, prm, q, yc)
        yn, _ = _group_norm(f, gg_ref[...])
        o = _project_rows(yn.astype(MM_DTYPE), wo_ref)
        o_ref[...] = o
        x1_ref[...] = xv + o * _rstd(o) * pg_ref[...]

    row = lambda c: pl.BlockSpec((tt, c), lambda i: (i, 0))
    return _pallas(
        body, rider, name="mixer_fwd", steps=t // tt,
        in_specs=[row(D_MODEL), _const_spec((1, D_MODEL)), _weight_spec(wig)] + _mixer_param_specs(tt)
        + [_const_spec((tt, tt)), _const_spec((1, D_MODEL)), _weight_spec(wog), _const_spec((1, D_MODEL))],
        out_specs=[row(D_IN), row(D_MODEL), row(D_MODEL), row(D_MODEL), row(D_A + D_C)],
        out_shape=[jax.ShapeDtypeStruct((t, D_IN), F32), jax.ShapeDtypeStruct((t, D_MODEL), MM_DTYPE),
                   jax.ShapeDtypeStruct((t, D_MODEL), F32), jax.ShapeDtypeStruct((t, D_MODEL), F32),
                   jax.ShapeDtypeStruct((t, D_A + D_C), F32)],
        scratch_shapes=[pltpu.VMEM((HR_A + tt, D_A), F32), pltpu.VMEM((HR_C + tt, D_C), F32),
                        pltpu.VMEM((HR_A, D_A), F32), pltpu.VMEM((HR_C, D_C), F32),
                        pltpu.VMEM((N_HEADS_B, tt, tt), MM_DTYPE), pltpu.VMEM((tt, D_B), F32),
                        pltpu.VMEM((D_MODEL, D_IN), MM_DTYPE)],
        args=(x, g, wig, *mp, perm, grp_g, wog, post_g))


def mixer_bwd(dx1, o, z, cv, mp, perm, grp_g, wog, post_g, tt, rider=None):
    t = z.shape[0]
    assert t % tt == 0 and tt % CHUNK == 0 and tt >= HR_C, (t, tt)
    steps = t // tt

    def body(dx1_ref, o_ref, z_ref, cv_ref, *rest):
        prm_refs = rest[:9]
        (perm_ref, gg_ref, wo_ref, pg_ref,
         dz_ref, gwo_ref, dpg_ref, dgg_ref, dcaw_ref, dlng_ref, dlnb_ref, dwm_ref, dbias_ref, dccw_ref, dccb_ref,
         dclg_ref, dclb_ref,
         dq_ext, dyc_ext, dq_first, dyc_first, a_pg, a_gg, a_caw, a_lng, a_lnb, a_ccw, a_ccb, a_clg, a_clb,
         wp_scr, wpt_scr, bias_scr, a_wm, a_bias) = rest[9:]
        i = pl.program_id(0)
        small = (a_pg, a_gg, a_caw, a_lng, a_lnb, a_ccw, a_ccb, a_clg, a_clb)

        @pl.when(i == 0)
        def _():
            for ref in small + (a_wm, a_bias, dq_first, dyc_first, gwo_ref):
                ref[...] = jnp.zeros_like(ref)
            _to_tile_order(perm_ref[...], prm_refs[3], wp_scr, False)
            _to_tile_order(perm_ref[...], prm_refs[3], wpt_scr, True)
            bias_scr[...] = _dot_exact(perm_ref[...], prm_refs[4][...], ((1,), (0,)))

        prm = _mixer_prm(prm_refs, wp_scr, bias_scr)
        caw, lng, lnb, wm, bias_p, ccw, ccb, clg, clb = prm

        zv = z_ref[...]
        pa, yg, a, sg = _conv_inputs(zv)
        f = _mixer_forward(zv, prm, cv_ref[:, 0:D_A], cv_ref[:, D_A:])
        gg = gg_ref[...]
        yn, (ra, rb, ro) = _group_norm(f, gg)

        ov = o_ref[...]
        dx1v = dx1_ref[...]
        r_o = _rstd(ov)
        pg = pg_ref[...]
        a_pg[...] += _rowsum8(dx1v * ov * r_o)
        do = _rms_bwd(ov, r_o, pg, dx1v).astype(MM_DTYPE)
        gwo = _dot_tn(yn.astype(MM_DTYPE), do)
        rows = gwo_ref.shape[1]
        for j in range(N_CHIPS):
            gwo_ref[j] += gwo[rows * j:rows * (j + 1), :]
        dyn = jnp.concatenate([_dot_nt(do, wo_ref[j]) for j in range(N_CHIPS)], axis=1)

        dyn_a, dyn_b, dyn_c = dyn[:, 0:D_A], dyn[:, D_A:D_A + D_B], dyn[:, D_A + D_B:]
        ga, gb, gc = gg[:, 0:D_A], gg[:, D_A:D_A + D_B], gg[:, D_A + D_B:]
        a_gg[...] += _rowsum8(jnp.concatenate([dyn_a * f["ya"] * ra, dyn_b * f["yb"] * rb, dyn_c * f["yo"] * ro], axis=1))
        dya = _rms_bwd(f["ya"], ra, ga, dyn_a)
        dyb = _rms_bwd(f["yb"], rb, gb, dyn_b)
        dyo = _rms_bwd(f["yo"], ro, gc, dyn_c)

        dbg = dya * f["q"]
        dq = dya * f["bg"]
        dp = _conv_anticausal(dq_ext, dq, dq_first[...], caw, K_A, tt, x=pa, acc_w=a_caw)
        dq_first[...] = dq[0:HR_A, :]
        dcg = dp * zv[:, 2 * D_A:3 * D_A]
        dxa = dp * zv[:, D_A:2 * D_A]

        du = dyb * f["s"]
        ds = dyb * f["u"]
        dsb = ds.astype(MM_DTYPE)
        a_bias[...] += ds
        odd = lax.broadcasted_iota(jnp.int32, (tt, PAIR), 1) // HEAD
        for h in range(N_HEADS_B):
            dsp = dsb[:, _pair_lanes(h)]
            a_wm[h] += _dot_nt(jnp.where(odd == h % 2, dsp, jnp.zeros_like(dsp)), f["vnb"][:, _pair_lanes(h)])
        dvn = _per_head(lambda h: _dot(wpt_scr[h], dsb[:, _pair_lanes(h)]))
        a_lng[...] += _rowsum8(dvn * f["vh"])
        a_lnb[...] += _rowsum8(dvn)
        dv = _ln_bwd(f["vh"], f["rv"], dvn * lng)
        dzu = du * _gelu_grad(f["zu"])
        dzv = dv * _gelu_grad(f["zv"])

        l, sl = f["l"], f["sl"]
        dl = dyo * (sl * (1.0 + l * (1.0 - sl)))
        a_clg[...] += _rowsum8(dl * f["yh"])
        a_clb[...] += _rowsum8(dl)
        dyc = _ln_bwd(f["yh"], f["rc"], dl * clg)
        a_ccb[...] += _rowsum8(dyc)
        dy = _conv_anticausal(dyc_ext, dyc, dyc_first[...], ccw, K_C, tt, x=yg, acc_w=a_ccw)
        dyc_first[...] = dyc[0:HR_C, :]
        da = dy * sg
        dg = dy * a * sg * (1.0 - sg)

        dz_ref[...] = jnp.concatenate([dbg, dcg, dxa, dzu, dzv, da, dg], axis=1).astype(MM_DTYPE)

        @pl.when(i == steps - 1)
        def _():
            red = lambda ref: jnp.sum(ref[...], axis=0, keepdims=True)
            dpg_ref[...] = red(a_pg)
            dgg_ref[...] = red(a_gg)
            dlng_ref[...] = red(a_lng)
            dlnb_ref[...] = red(a_lnb)
            dccb_ref[...] = red(a_ccb)
            dclg_ref[...] = red(a_clg)
            dclb_ref[...] = red(a_clb)
            dcaw_ref[...] = jnp.sum(a_caw[...], axis=1)
            dccw_ref[...] = jnp.sum(a_ccw[...], axis=1)
            pm = perm_ref[...]
            tril = lax.broadcasted_iota(jnp.int32, (CHUNK, CHUNK), 0) >= lax.broadcasted_iota(jnp.int32, (CHUNK, CHUNK), 1)
            for h in range(N_HEADS_B):
                dwt = _dot_exact(pm, _dot_exact(a_wm[h], pm, ((1,), (0,))), ((0,), (0,)))
                dw = dwt[0:CHUNK, 0:CHUNK]
                for c in range(1, tt // CHUNK):
                    dw = dw + dwt[c * CHUNK:(c + 1) * CHUNK, c * CHUNK:(c + 1) * CHUNK]
                dwm_ref[h] = jnp.where(tril, dw, 0.0)
            dbt = _dot_exact(pm, a_bias[...], ((0,), (0,)))
            db = dbt[0:CHUNK, :]
            for c in range(1, tt // CHUNK):
                db = db + dbt[c * CHUNK:(c + 1) * CHUNK, :]
            lane_head = lax.broadcasted_iota(jnp.int32, (D_B, CHUNK), 0) // HEAD
            fold = (lane_head == lax.broadcasted_iota(jnp.int32, (D_B, CHUNK), 1)).astype(F32)
            dbias_ref[...] = _dot_exact(db, fold, ((1,), (0,)))

    rev = lambda c: pl.BlockSpec((tt, c), lambda i: (steps - 1 - i, 0))
    full = lambda shape: pl.BlockSpec(shape, lambda i: (0,) * len(shape))
    sds = jax.ShapeDtypeStruct
    return _pallas(
        body, rider, name="mixer_bwd", steps=steps,
        in_specs=[rev(D_MODEL), rev(D_MODEL), rev(D_IN), rev(D_A + D_C)] + _mixer_param_specs(tt)
        + [_const_spec((tt, tt)), _const_spec((1, D_MODEL)), _weight_spec(wog), _const_spec((1, D_MODEL))],
        out_specs=[rev(D_IN), full((N_CHIPS, D_MODEL // N_CHIPS, D_MODEL)), full((1, D_MODEL)), full((1, D_MODEL)), full((8, D_A)),
                   full((1, D_B)), full((1, D_B)), full((N_HEADS_B, CHUNK, CHUNK)), full((CHUNK, CHUNK)), full((32, D_C)),
                   full((1, D_C)), full((1, D_C)), full((1, D_C))],
        out_shape=[sds((t, D_IN), MM_DTYPE), sds((N_CHIPS, D_MODEL // N_CHIPS, D_MODEL), F32),
                   sds((1, D_MODEL), F32), sds((1, D_MODEL), F32), sds((8, D_A), F32), sds((1, D_B), F32), sds((1, D_B), F32),
                   sds((N_HEADS_B, CHUNK, CHUNK), F32), sds((CHUNK, CHUNK), F32), sds((32, D_C), F32), sds((1, D_C), F32),
                   sds((1, D_C), F32), sds((1, D_C), F32)],
        scratch_shapes=[pltpu.VMEM((tt + HR_A, D_A), F32), pltpu.VMEM((tt + HR_C, D_C), F32),
                        pltpu.VMEM((HR_A, D_A), F32), pltpu.VMEM((HR_C, D_C), F32),
                        pltpu.VMEM((8, D_MODEL), F32), pltpu.VMEM((8, D_MODEL), F32), pltpu.VMEM((8, 8, D_A), F32),
                        pltpu.VMEM((8, D_B), F32), pltpu.VMEM((8, D_B), F32), pltpu.VMEM((32, 8, D_C), F32),
                        pltpu.VMEM((8, D_C), F32), pltpu.VMEM((8, D_C), F32), pltpu.VMEM((8, D_C), F32),
                        pltpu.VMEM((N_HEADS_B, tt, tt), MM_DTYPE), pltpu.VMEM((N_HEADS_B, tt, tt), MM_DTYPE),
                        pltpu.VMEM((tt, D_B), F32), pltpu.VMEM((N_HEADS_B, tt, tt), F32), pltpu.VMEM((tt, D_B), F32)],
        args=(dx1, o, z, cv, *mp, perm, grp_g, wog, post_g))


def _fetch_row_blocks(wg_ref, w_scr, sems):
    r = wg_ref.shape[1]
    copies = [pltpu.make_async_copy(wg_ref.at[j], w_scr.at[pl.ds(r * j, r), :], sems.at[j]) for j in range(N_CHIPS)]
    for cp in copies:
        cp.start()
    for cp in copies:
        cp.wait()


def _ffn_conv(ext, cw, c0, cn, tt):
    acc = cw[0:1, c0:c0 + cn] * ext[0:tt, c0:c0 + cn]
    for k in range(1, K_F):
        acc = acc + cw[k:k + 1, c0:c0 + cn] * ext[8 * k:8 * k + tt, c0:c0 + cn]
    return acc


def ffn_fwd(x1, g, wug, cw, wdg, post_g, tt, rider=None):
    t, dm = x1.shape
    assert t % tt == 0, (t, tt)
    cn = _col_chunk(D_FF)
    cu = wug.shape[2]

    def init(x1_ref, g_ref, wu_ref, cw_ref, wdg_ref, pg_ref, up0_ref, h2_ref, d_ref, x2_ref, ext, last, wd_ref, sems):
        _fetch_row_blocks(wdg_ref, wd_ref, sems)
        last[...] = jnp.zeros_like(last)

    def body(x1_ref, g_ref, wu_ref, cw_ref, wdg_ref, pg_ref, up0_ref, h2_ref, d_ref, x2_ref, ext, last, wd_ref, sems):
        xv = x1_ref[...]
        h = (xv * _rstd(xv) * g_ref[...]).astype(MM_DTYPE)
        h2_ref[...] = h
        for j in range(N_CHIPS):
            u = _dot(h, wu_ref[j])
            up0_ref[:, cu * j:cu * (j + 1)] = u
            ext[HR_F:HR_F + tt, cu * j:cu * (j + 1)] = u
        ext[0:HR_F, :] = _halo_before(ext[tt:tt + HR_F, :], last[...])
        last[...] = ext[tt:tt + HR_F, :]
        cwv = cw_ref[...]
        d = jnp.zeros((tt, D_MODEL), F32)
        for c0 in range(0, D_FF, cn):
            gate = _ffn_conv(ext, cwv, c0, cn, tt)
            val = _ffn_conv(ext, cwv, D_FF + c0, cn, tt)
            act = (gate * jax.nn.sigmoid(gate) * val).astype(MM_DTYPE)
            d = d + _dot(act, wd_ref[c0:c0 + cn, :])
        d_ref[...] = d
        x2_ref[...] = xv + d * _rstd(d) * pg_ref[...]

    row = lambda c: pl.BlockSpec((tt, c), lambda i: (i, 0))
    return _pallas(
        body, rider, name="ffn_fwd", steps=t // tt, init=init,
        in_specs=[row(dm), _const_spec((1, dm)), _weight_spec(wug), _const_spec((8, 2 * D_FF)), _ANY, _const_spec((1, dm))],
        out_specs=[row(2 * D_FF), row(dm), row(dm), row(dm)],
        out_shape=[jax.ShapeDtypeStruct((t, 2 * D_FF), F32), jax.ShapeDtypeStruct((t, dm), MM_DTYPE),
                   jax.ShapeDtypeStruct((t, dm), F32), jax.ShapeDtypeStruct((t, dm), F32)],
        scratch_shapes=[pltpu.VMEM((HR_F + tt, 2 * D_FF), F32), pltpu.VMEM((HR_F, 2 * D_FF), F32),
                        pltpu.VMEM((D_FF, D_MODEL), MM_DTYPE), pltpu.SemaphoreType.DMA((N_CHIPS,))],
        args=(x1, g, wug, cw, wdg, post_g))


def ffn_bwd(dx2, d, up0, cw, wdg, post_g, tt, rider=None):
    t = up0.shape[0]
    assert t % tt == 0, (t, tt)
    steps = t // tt
    hb = tt // HR_F
    cn = _col_chunk(D_FF)

    def init(dx2_ref, d_ref, up0_ref, uh_ref, cw_ref, wdg_ref, pg_ref,
             dd_ref, act_ref, dup0_ref, dpg_ref, dcw_ref, ext, dup_ext, first, a_pg, a_cw, wd_ref, sems):
        _fetch_row_blocks(wdg_ref, wd_ref, sems)
        a_pg[...] = jnp.zeros_like(a_pg)
        a_cw[...] = jnp.zeros_like(a_cw)
        first[...] = jnp.zeros_like(first)

    def body(dx2_ref, d_ref, up0_ref, uh_ref, cw_ref, wdg_ref, pg_ref,
             dd_ref, act_ref, dup0_ref, dpg_ref, dcw_ref, ext, dup_ext, first, a_pg, a_cw, wd_ref, sems):
        i = pl.program_id(0)
        tile = steps - 1 - i

        ext[HR_F:HR_F + tt, :] = up0_ref[...]
        ext[0:HR_F, :] = _halo_before(up0_ref[tt - HR_F:, :], jnp.where(tile > 0, uh_ref[...], 0.0))
        cwv = cw_ref[...]
        dv = d_ref[...]
        dx2v = dx2_ref[...]
        r = _rstd(dv)
        a_pg[...] += _rowsum8(dx2v * dv * r)
        dd = _rms_bwd(dv, r, pg_ref[...], dx2v).astype(MM_DTYPE)
        dd_ref[...] = dd
        for c0 in range(0, D_FF, cn):
            gate = _ffn_conv(ext, cwv, c0, cn, tt)
            val = _ffn_conv(ext, cwv, D_FF + c0, cn, tt)
            sg = jax.nn.sigmoid(gate)
            sl = gate * sg
            act_ref[:, c0:c0 + cn] = (sl * val).astype(MM_DTYPE)
            da = _dot_nt(dd, wd_ref[c0:c0 + cn, :])
            dup_ext[0:tt, c0:c0 + cn] = da * val * (sg * (1.0 + gate * (1.0 - sg)))
            dup_ext[0:tt, D_FF + c0:D_FF + c0 + cn] = da * sl
        dup_ext[tt:tt + HR_F, :] = _halo_after(dup_ext[0:HR_F, :], first[...])
        first[...] = dup_ext[0:HR_F, :]
        for c0 in range(0, 2 * D_FF, cn):
            x = up0_ref[:, c0:c0 + cn]
            acc = None
            for k in range(K_F):
                off = 8 * (K_F - 1 - k)
                ld = dup_ext[off:off + tt, c0:c0 + cn]
                term = cwv[k:k + 1, c0:c0 + cn] * ld
                acc = term if acc is None else acc + term
                a_cw[k, :, c0:c0 + cn] += _rowsum8(ld * x)
            dup0_ref[:, c0:c0 + cn] = acc.astype(MM_DTYPE)

        @pl.when(i == steps - 1)
        def _():
            dpg_ref[...] = jnp.sum(a_pg[...], axis=0, keepdims=True)
            dcw_ref[...] = jnp.sum(a_cw[...], axis=1)

    rev = lambda c: pl.BlockSpec((tt, c), lambda i: (steps - 1 - i, 0))
    halo = pl.BlockSpec((HR_F, 2 * D_FF), lambda i: (jnp.maximum((steps - 1 - i) * hb - 1, 0), 0))
    full = lambda shape: pl.BlockSpec(shape, lambda i: (0,) * len(shape))
    sds = jax.ShapeDtypeStruct
    return _pallas(
        body, rider, name="ffn_bwd", steps=steps, init=init,
        in_specs=[rev(D_MODEL), rev(D_MODEL), rev(2 * D_FF), halo, _const_spec((8, 2 * D_FF)), _ANY,
                  _const_spec((1, D_MODEL))],
        out_specs=[rev(D_MODEL), rev(D_FF), rev(2 * D_FF), full((1, D_MODEL)), full((8, 2 * D_FF))],
        out_shape=[sds((t, D_MODEL), MM_DTYPE), sds((t, D_FF), MM_DTYPE), sds((t, 2 * D_FF), MM_DTYPE),
                   sds((1, D_MODEL), F32), sds((8, 2 * D_FF), F32)],
        scratch_shapes=[pltpu.VMEM((HR_F + tt, 2 * D_FF), F32), pltpu.VMEM((tt + HR_F, 2 * D_FF), F32),
                        pltpu.VMEM((HR_F, 2 * D_FF), F32), pltpu.VMEM((8, D_MODEL), F32),
                        pltpu.VMEM((8, 8, 2 * D_FF), F32), pltpu.VMEM((D_FF, D_MODEL), MM_DTYPE),
                        pltpu.SemaphoreType.DMA((N_CHIPS,))],
        args=(dx2, d, up0, up0, cw, wdg, post_g))


def loss_head(y, target, tm):
    t, d = y.shape
    assert t % tm == 0, (t, tm)
    steps = t // tm

    def body(y_ref, t_ref, dy_ref, loss_ref, acc):
        i = pl.program_id(0)

        @pl.when(i == 0)
        def _():
            acc[...] = jnp.zeros_like(acc)

        diff = y_ref[...] - t_ref[...]
        dy_ref[...] = diff * (1.0 / d)
        acc[...] += _rowsum8(diff * diff)

        @pl.when(i == steps - 1)
        def _():
            loss_ref[...] = (0.5 / d) * jnp.sum(jnp.sum(acc[...], axis=0, keepdims=True), axis=1, keepdims=True)

    row = pl.BlockSpec((tm, d), lambda i: (i, 0))
    return pl.pallas_call(
        body, name="loss_head", grid=(steps,), in_specs=[row, row],
        out_specs=[row, pl.BlockSpec((1, 1), lambda i: (0, 0))],
        out_shape=[jax.ShapeDtypeStruct((t, d), F32), jax.ShapeDtypeStruct((1, 1), F32)],
        scratch_shapes=[pltpu.VMEM((8, d), F32)],
        compiler_params=_params(("arbitrary",)),
    )(y, target)


def adamw(w, g, m, v):
    shape = w.shape
    cols = shape[-1]
    rows = w.size // cols
    tr = next((r for r in (512, 256, 128) if rows % r == 0 and rows > r), rows)
    c1 = 1.0 - ADAM_B1 ** ADAM_STEP
    c2 = 1.0 - ADAM_B2 ** ADAM_STEP

    def body(w_ref, g_ref, m_ref, v_ref, d_ref, nm_ref, nv_ref):
        gv = g_ref[...]
        nm = ADAM_B1 * m_ref[...] + (1.0 - ADAM_B1) * gv
        nv = ADAM_B2 * v_ref[...] + (1.0 - ADAM_B2) * (gv * gv)
        nm_ref[...] = nm
        nv_ref[...] = nv
        d_ref[...] = -ADAM_LR * ((nm / c1) / (jnp.sqrt(nv / c2) + ADAM_EPS) + ADAM_WD * w_ref[...])

    spec = pl.BlockSpec((tr, cols), lambda i: (i, 0))
    out = jax.ShapeDtypeStruct((rows, cols), F32)
    res = pl.pallas_call(
        body, name="adamw", grid=(rows // tr,), in_specs=[spec] * 4, out_specs=[spec] * 3, out_shape=[out] * 3,
        compiler_params=_params(("arbitrary",)),
    )(*[a.reshape(rows, cols) for a in (w, g, m, v)])
    return tuple(r.reshape(shape) for r in res)


def _place():
    return lax.axis_index("x"), lax.axis_index("y"), lax.axis_index("c")


def _other_chips(x, y):
    return [(1 - x, y, 2 * (1 - x) + y), (x, 1 - y, 2 * x + 1 - y), (1 - x, 1 - y, 2 * (1 - x) + 1 - y)]


def _sem_specs(*counts):
    return [pltpu.SemaphoreType.DMA((n,)) for n in counts]


def cast_shard(w, layer, chip):
    _, r, c = w.shape

    def body(chip_ref, w_ref, o_ref):
        del chip_ref
        o_ref[...] = w_ref[...].astype(MM_DTYPE)

    grid_spec = pltpu.PrefetchScalarGridSpec(
        num_scalar_prefetch=1, grid=(1,), in_specs=[pl.BlockSpec((None, r, c), lambda i, chip_ref: (layer, 0, 0))],
        out_specs=pl.BlockSpec((None, r, c), lambda i, chip_ref: (chip_ref[0], 0, 0)))
    return pl.pallas_call(
        body, name="cast_shard", grid_spec=grid_spec, out_shape=jax.ShapeDtypeStruct((N_CHIPS, r, c), MM_DTYPE),
        compiler_params=_params(("arbitrary",)),
    )(jnp.reshape(chip, (1,)).astype(jnp.int32), w)


def _row_half(buf, chip, mine, c):
    rh = buf.shape[1] // 2
    return buf.at[chip, pl.ds(pl.multiple_of((c if mine else 1 - c) * rh, 16), rh), :]


def spread_rider(bufs):
    n = len(bufs)

    def start(rin, rout, sems):
        x, y, c = _place()
        me = 2 * x + y
        for k, (px, py, _) in enumerate(_other_chips(x, y)):
            for i, buf in enumerate(rout):
                part = _row_half(buf, me, True, c)
                pltpu.make_async_remote_copy(
                    src_ref=part, dst_ref=part, send_sem=sems[0].at[n * k + i], recv_sem=sems[1].at[n * k + i],
                    device_id=(px, py, c), device_id_type=MESH_ID).start()

    def wait(rin, rout, sems):
        x, y, c = _place()
        for k, (_, _, pj) in enumerate(_other_chips(x, y)):
            for i, buf in enumerate(rout):
                part = _row_half(buf, pj, True, c)
                pltpu.make_async_remote_copy(
                    src_ref=part, dst_ref=part, send_sem=sems[0].at[n * k + i], recv_sem=sems[1].at[n * k + i],
                    device_id=(x, y, c), device_id_type=MESH_ID).wait()

    shapes = [jax.ShapeDtypeStruct(b.shape, b.dtype) for b in bufs]
    return Rider("spread", list(bufs), shapes, {i: i for i in range(n)}, (3 * n, 3 * n), start, wait)


def pass_rider(bufs):
    n = len(bufs)

    def start(rin, rout, sems):
        x, y, c = _place()
        for k, (_, _, pj) in enumerate(_other_chips(x, y)):
            for i, buf in enumerate(rout):
                part = _row_half(buf, pj, True, c)
                pltpu.make_async_remote_copy(
                    src_ref=part, dst_ref=part, send_sem=sems[0].at[n * k + i], recv_sem=sems[1].at[n * k + i],
                    device_id=(x, y, 1 - c), device_id_type=MESH_ID).start()

    def wait(rin, rout, sems):
        x, y, c = _place()
        for k, (_, _, pj) in enumerate(_other_chips(x, y)):
            for i, buf in enumerate(rout):
                part = _row_half(buf, pj, False, c)
                pltpu.make_async_remote_copy(
                    src_ref=part, dst_ref=part, send_sem=sems[0].at[n * k + i], recv_sem=sems[1].at[n * k + i],
                    device_id=(x, y, 1 - c), device_id_type=MESH_ID).wait()

    shapes = [jax.ShapeDtypeStruct(b.shape, b.dtype) for b in bufs]
    return Rider("pass", list(bufs), shapes, {i: i for i in range(n)}, (3 * n, 3 * n), start, wait)


def both_riders(a, b):
    na, oa, sa = len(a.inputs), len(a.out_shapes), len(a.sems)

    def start(rin, rout, sems):
        a.start(rin[:na], rout[:oa], sems[:sa])
        b.start(rin[na:], rout[oa:], sems[sa:])

    def wait(rin, rout, sems):
        a.wait(rin[:na], rout[:oa], sems[:sa])
        b.wait(rin[na:], rout[oa:], sems[sa:])

    aliases = dict(a.aliases)
    aliases.update({na + i: oa + o for i, o in b.aliases.items()})
    return Rider(a.name + "_" + b.name, a.inputs + b.inputs, a.out_shapes + b.out_shapes, aliases, a.sems + b.sems,
                 start, wait)


def gather_small(small):
    def body(small_ref, out_ref, send, recv, local):
        x, y, c = _place()
        me = 2 * x + y
        chips = _other_chips(x, y)
        own = pltpu.make_async_copy(small_ref, out_ref.at[me], local.at[0])
        own.start()
        sends = [pltpu.make_async_remote_copy(src_ref=small_ref, dst_ref=out_ref.at[me], send_sem=send.at[k],
                                              recv_sem=recv.at[k], device_id=(px, py, c), device_id_type=MESH_ID)
                 for k, (px, py, _) in enumerate(chips)]
        for cp in sends:
            cp.start()
        for k, (_, _, pj) in enumerate(chips):
            pltpu.make_async_remote_copy(src_ref=small_ref, dst_ref=out_ref.at[pj], send_sem=send.at[k], recv_sem=recv.at[k],
                                         device_id=(x, y, c), device_id_type=MESH_ID).wait_recv()
        for cp in sends:
            cp.wait_send()
        own.wait()

    return pl.pallas_call(
        body, name="gather_small", in_specs=[_ANY], out_specs=_ANY,
        out_shape=jax.ShapeDtypeStruct((N_CHIPS,) + small.shape, small.dtype), scratch_shapes=_sem_specs(3, 3, 1),
        compiler_params=pltpu.CompilerParams(has_side_effects=True),
    )(small)


def swap_rider(gs):
    n = len(gs)

    def copies(rin, rout, sems):
        x, y, c = _place()
        out = []
        for i, (g, got) in enumerate(zip(rin, rout)):
            rh = g.shape[1] // 2
            theirs = pl.ds(pl.multiple_of((1 - c) * rh, 8), rh)
            out.append(pltpu.make_async_remote_copy(
                src_ref=g.at[:, theirs, :], dst_ref=got, send_sem=sems[0].at[i], recv_sem=sems[1].at[i],
                device_id=(x, y, 1 - c), device_id_type=MESH_ID))
        return out

    def start(rin, rout, sems):
        for cp in copies(rin, rout, sems):
            cp.start()

    def wait(rin, rout, sems):
        for cp in copies(rin, rout, sems):
            cp.wait()

    shapes = [jax.ShapeDtypeStruct((g.shape[0], g.shape[1] // 2, g.shape[2]), g.dtype) for g in gs]
    return Rider("swap", list(gs), shapes, {}, (n, n), start, wait)


def scatter_rider(sbs):
    n = len(sbs)

    def start(rin, rout, sems):
        x, y, c = _place()
        me = 2 * x + y
        for k, (px, py, pj) in enumerate(_other_chips(x, y)):
            for i, (sb, got) in enumerate(zip(rin, rout)):
                pltpu.make_async_remote_copy(
                    src_ref=sb.at[pj], dst_ref=got.at[me], send_sem=sems[0].at[n * k + i], recv_sem=sems[1].at[n * k + i],
                    device_id=(px, py, c), device_id_type=MESH_ID).start()

    def wait(rin, rout, sems):
        x, y, c = _place()
        for k, (_, _, pj) in enumerate(_other_chips(x, y)):
            for i, (sb, got) in enumerate(zip(rin, rout)):
                cp = pltpu.make_async_remote_copy(
                    src_ref=sb.at[pj], dst_ref=got.at[pj], send_sem=sems[0].at[n * k + i], recv_sem=sems[1].at[n * k + i],
                    device_id=(x, y, c), device_id_type=MESH_ID)
                cp.wait_recv()
                cp.wait_send()

    shapes = [jax.ShapeDtypeStruct(sb.shape, sb.dtype) for sb in sbs]
    return Rider("scatter", list(sbs), shapes, {}, (3 * n, 3 * n), start, wait)


def join_rider(fs, layers):
    n = len(fs)

    def half(i, f, mine, place):
        x, y, c = place
        rh = f.shape[1] // 2
        block = 2 * x + y if layers[i] is None else layers[i]
        return f.at[block, pl.ds(pl.multiple_of((c if mine else 1 - c) * rh, 8), rh), :]

    def start(rin, rout, sems):
        x, y, c = _place()
        for i, f in enumerate(rout):
            part = half(i, f, True, (x, y, c))
            pltpu.make_async_remote_copy(
                src_ref=part, dst_ref=part, send_sem=sems[0].at[i], recv_sem=sems[1].at[i],
                device_id=(x, y, 1 - c), device_id_type=MESH_ID).start()

    def wait(rin, rout, sems):
        x, y, c = _place()
        for i, f in enumerate(rout):
            part = half(i, f, False, (x, y, c))
            pltpu.make_async_remote_copy(
                src_ref=part, dst_ref=part, send_sem=sems[0].at[i], recv_sem=sems[1].at[i],
                device_id=(x, y, 1 - c), device_id_type=MESH_ID).wait()

    shapes = [jax.ShapeDtypeStruct(f.shape, f.dtype) for f in fs]
    return Rider("join", list(fs), shapes, {i: i for i in range(n)}, (n, n), start, wait)


def add_halves(gs, gots, wire=BF16):
    m = len(gs)
    x, y, c = _place()

    def body(p_ref, *refs):
        ins, outs = refs[:2 * m], refs[2 * m:]
        for i in range(m):
            s = ins[2 * i][...] + ins[2 * i + 1][...]
            outs[2 * i][...] = s.astype(wire)

            @pl.when(pl.program_id(0) == p_ref[0])
            def _(s=s, own_ref=outs[2 * i + 1]):
                own_ref[...] = s

    in_specs, out_specs, out_shape = [], [], []
    for got in gots:
        n, rh, cols = got.shape
        blk = (None, rh, cols)
        in_specs += [pl.BlockSpec(blk, lambda j, p_ref: (j, p_ref[1], 0)), pl.BlockSpec(blk, lambda j, p_ref: (j, 0, 0))]
        out_specs += [pl.BlockSpec(blk, lambda j, p_ref: (j, 0, 0)), pl.BlockSpec((rh, cols), lambda j, p_ref: (0, 0))]
        out_shape += [jax.ShapeDtypeStruct(got.shape, wire), jax.ShapeDtypeStruct((rh, cols), F32)]
    grid_spec = pltpu.PrefetchScalarGridSpec(num_scalar_prefetch=1, grid=(N_CHIPS,), in_specs=in_specs, out_specs=out_specs)
    res = pl.pallas_call(
        body, name="add_halves", grid_spec=grid_spec, out_shape=out_shape, compiler_params=_params(("arbitrary",)),
    )(jnp.stack([2 * x + y, c]).astype(jnp.int32), *[a for pair in zip(gs, gots) for a in pair])
    return [(res[2 * i + 1], res[2 * i]) for i in range(m)]


def add_chips(owns, gots, fbufs, block=None):
    m = len(owns)
    x, y, c = _place()
    me = 2 * x + y

    def body(p_ref, *refs):
        ins, outs = refs[:5 * m], refs[5 * m:]
        for i in range(m):
            s_ref, g1_ref, g2_ref, g3_ref, _ = ins[5 * i:5 * i + 5]
            outs[i][...] = s_ref[...] + g1_ref[...].astype(F32) + g2_ref[...].astype(F32) + g3_ref[...].astype(F32)

    def other(blk, n, k):
        return pl.BlockSpec(blk, lambda i, p_ref: ((p_ref[0] + k) % n, 0, 0))

    in_specs, out_specs, args = [], [], []
    for own, got, fbuf in zip(owns, gots, fbufs):
        n, rh, cols = got.shape
        blk = (None, rh, cols)
        in_specs += [pl.BlockSpec((rh, cols), lambda i, p_ref: (0, 0)), other(blk, n, 1), other(blk, n, 2), other(blk, n, 3),
                     _ANY]
        out_specs.append(pl.BlockSpec(blk, lambda i, p_ref: (p_ref[2], p_ref[1], 0)))
        args += [own, got, got, got, fbuf]
    grid_spec = pltpu.PrefetchScalarGridSpec(num_scalar_prefetch=1, grid=(1,), in_specs=in_specs, out_specs=out_specs)
    return pl.pallas_call(
        body, name="add_chips", grid_spec=grid_spec, out_shape=[jax.ShapeDtypeStruct(f.shape, F32) for f in fbufs],
        input_output_aliases={5 * i + 5: i for i in range(m)}, compiler_params=_params(("arbitrary",)),
    )(jnp.stack([me, c, me if block is None else block]).astype(jnp.int32), *args)


def _pack(arrays, rows):
    flat = jnp.concatenate([a.reshape(-1) for a in arrays])
    return jnp.pad(flat, (0, rows * LANES - flat.size)).reshape(rows, LANES)


def _unpack(buf, shapes):
    flat = buf.reshape(-1)
    out, at = [], 0
    for s in shapes:
        n = math.prod(s)
        out.append(flat[at:at + n].reshape(s))
        at += n
    return out


CONV_SHARDS = [(DEPTH, K_A, D_A // N_CHIPS), (DEPTH, K_C, D_C // N_CHIPS), (DEPTH, K_F, 2 * D_FF // N_CHIPS)]
CONV_ROWS = 32
SMALL_ROWS = 640


def _join_cols(g):
    n, l, r, c = g.shape
    return jnp.transpose(g, (1, 2, 0, 3)).reshape(l, r, n * c)


BIG = ["w_in", "w_out", "w_up", "w_down"]
WIDE = ["w_up", "w_down"]
NARROW = ["w_in", "w_out"]
TILE_MM = 512
TILE_TN = 1024
TILE_EW = 256


def _pad_rows(a, rows):
    return jnp.pad(a, ((0, rows - a.shape[0]), (0, 0)))


def _row(a):
    return a.reshape(1, -1)


def _tile_perm(tt):
    p = lax.broadcasted_iota(jnp.int32, (tt, tt), 0)
    tok = lax.broadcasted_iota(jnp.int32, (tt, tt), 1)
    return ((tt // 8) * (p % 8) + p // 8 == tok).astype(F32)


def _layer_params(wl, tt):
    n = tt // CHUNK
    tril = jnp.tril(jnp.ones((CHUNK, CHUNK), bool))
    wm = jnp.where(tril[None], wl["sgu_w"], 0.0)
    eye = jnp.eye(n, dtype=F32)
    wt = (eye[None, :, None, :, None] * wm[:, None, :, None, :]).reshape(N_HEADS_B, tt, tt)
    bias_e = jnp.repeat(wl["sgu_b"].T, HEAD, axis=1)
    return (_pad_rows(wl["conv_a_w"], 8), _row(wl["sgu_ln_g"]), _row(wl["sgu_ln_b"]), wt.astype(MM_DTYPE),
            jnp.tile(bias_e, (n, 1)), _pad_rows(wl["conv_c_w"], 32), _row(wl["conv_c_b"]), _row(wl["conv_ln_g"]),
            _row(wl["conv_ln_b"]))


def layer_fwd(x, wl, gw, nxt=None, tm=TILE_MM, tt=TILE_EW):
    mp = _layer_params(wl, tt)
    rides = ([pass_rider([gw[n] for n in WIDE])] if gw.get("pass_wide") else []) + (
        [spread_rider([nxt[n] for n in NARROW])] if nxt else [])
    ride = both_riders(*rides) if len(rides) == 2 else (rides[0] if rides else None)
    (z, h, o, x1, cv), done = mixer_fwd(x, _row(wl["pre_mix_g"]), gw["w_in"], mp, _tile_perm(tt), _row(wl["grp_norm_g"]),
                                        gw["w_out"], _row(wl["post_mix_g"]), tt, rider=ride)
    if gw.get("pass_wide"):
        gw, done = dict(gw, w_up=done[0], w_down=done[1]), done[2:]
    gw = {n: gw[n] for n in BIG}
    ride = both_riders(spread_rider([nxt[n] for n in WIDE]), pass_rider(list(done))) if nxt else None
    (up0, h2, d, x2), done = ffn_fwd(x1, _row(wl["pre_ffn_g"]), gw["w_up"], _pad_rows(wl["ffn_conv_w"], 8), gw["w_down"],
                                     _row(wl["post_ffn_g"]), tt, rider=ride)
    if nxt:
        nxt = dict(nxt, w_up=done[0], w_down=done[1], w_in=done[2], w_out=done[3], pass_wide=True)
    return x2, dict(x=x, z=z, h=h, o=o, x1=x1, up0=up0, h2=h2, d=d, cv=cv, gw=gw), nxt


def layer_bwd(dx2, wl, layer, sv, pend=None, exchange=True, tm=TILE_MM, tt=TILE_EW):
    mp = _layer_params(wl, tt)
    gw = sv["gw"]
    tk = min(TILE_TN, dx2.shape[0])
    at = {n: BIG.index(n) for n in BIG}
    g = {}
    ride = scatter_rider([sw for _, sw in pend["narrow"]]) if pend else None
    (dd, act, dup0, dpg, dcw), arrived = ffn_bwd(dx2, sv["d"], sv["up0"], _pad_rows(wl["ffn_conv_w"], 8), gw["w_down"],
                                                 _row(wl["post_ffn_g"]), tt, rider=ride)
    fbuf = list(pend["fbuf"]) if pend else grad_buffers()
    if pend:
        done = add_chips([own for own, _ in pend["narrow"]], arrived, [fbuf[at[n]] for n in NARROW], pend["layer"])
        for n, f in zip(NARROW, done):
            fbuf[at[n]] = f
    g["post_ffn_g"] = dpg[0]
    g["ffn_conv_w"] = dcw[:K_F]
    gl = {}
    gl["w_down"] = matmul_tn_down(act, dd, tk)
    gl["w_up"] = matmul_tn_cols(sv["h2"], dup0, tk)
    ride = swap_rider([gl[n] for n in WIDE]) if exchange else None
    (dx1, dg), got = matmul_nt_norm_bwd(dup0, gw["w_up"], sv["x1"], _row(wl["pre_ffn_g"]), dx2, tm, rider=ride)
    g["pre_ffn_g"] = dg[0]
    wide = add_halves([gl[n] for n in WIDE], got) if exchange else None
    ride = scatter_rider([sw for _, sw in wide]) if exchange else None
    if pend:
        ride = both_riders(join_rider(fbuf, [pend["layer"]] * len(fbuf)), ride)
    (dz, gl["w_out"], dpg, dgg, dcaw, dlng, dlnb, dwm, dbias, dccw, dccb, dclg, dclb), rode = mixer_bwd(
        dx1, sv["o"], sv["z"], sv["cv"], mp, _tile_perm(tt), _row(wl["grp_norm_g"]), gw["w_out"],
        _row(wl["post_mix_g"]), tt, rider=ride)
    if exchange:
        fbuf, arrived = (list(rode[:len(BIG)]), rode[len(BIG):]) if pend else (fbuf, rode)
        done = add_chips([own for own, _ in wide], arrived, [fbuf[at[n]] for n in WIDE], layer)
        for n, f in zip(WIDE, done):
            fbuf[at[n]] = f
    g["post_mix_g"] = dpg[0]
    g["grp_norm_g"] = dgg[0]
    g["conv_a_w"] = dcaw[:K_A]
    g["sgu_ln_g"] = dlng[0]
    g["sgu_ln_b"] = dlnb[0]
    g["sgu_w"] = dwm
    g["sgu_b"] = dbias[:, :N_HEADS_B].T
    g["conv_c_w"] = dccw[:K_C]
    g["conv_c_b"] = dccb[0]
    g["conv_ln_g"] = dclg[0]
    g["conv_ln_b"] = dclb[0]
    gl["w_in"] = matmul_tn_in(sv["h"], dz, tk)
    ride = swap_rider([gl[n] for n in NARROW]) if exchange else None
    (dx, dg), got = matmul_nt_norm_bwd(dz, gw["w_in"], sv["x"], _row(wl["pre_mix_g"]), dx1, tm, rider=ride)
    g["pre_mix_g"] = dg[0]
    if not exchange:
        return dx, g, gl
    narrow = add_halves([gl[n] for n in NARROW], got)
    return dx, g, dict(narrow=narrow, fbuf=fbuf, layer=layer)


def grad_buffers():
    return [lax.empty(s, F32) for s in ((DEPTH, D_MODEL, D_IN // N_CHIPS), (DEPTH, D_MODEL // N_CHIPS, D_MODEL),
                                        (DEPTH, D_MODEL, 2 * D_FF // N_CHIPS), (DEPTH, D_FF // N_CHIPS, D_MODEL))]


CONV = ["conv_a_w", "conv_c_w", "ffn_conv_w"]
REPL = ["pre_mix_g", "sgu_ln_g", "sgu_ln_b", "sgu_w", "sgu_b", "conv_c_b", "conv_ln_g", "conv_ln_b", "grp_norm_g",
        "post_mix_g", "pre_ffn_g", "post_ffn_g"]
WEIGHTS = ["pre_mix_g", "w_in", "conv_a_w", "sgu_ln_g", "sgu_ln_b", "sgu_w", "sgu_b", "conv_c_w", "conv_c_b", "conv_ln_g",
           "conv_ln_b", "grp_norm_g", "w_out", "post_mix_g", "pre_ffn_g", "w_up", "ffn_conv_w", "w_down", "post_ffn_g"]


def kernel(x, pre_mix_g, w_in, conv_a_w, sgu_ln_g, sgu_ln_b, sgu_w, sgu_b, conv_c_w, conv_c_b, conv_ln_g, conv_ln_b, grp_norm_g, w_out, post_mix_g, pre_ffn_g, w_up, ffn_conv_w, w_down, post_ffn_g, loss_target, m_pre_mix_g, m_w_in, m_conv_a_w, m_sgu_ln_g, m_sgu_ln_b, m_sgu_w, m_sgu_b, m_conv_c_w, m_conv_c_b, m_conv_ln_g, m_conv_ln_b, m_grp_norm_g, m_w_out, m_post_mix_g, m_pre_ffn_g, m_w_up, m_ffn_conv_w, m_w_down, m_post_ffn_g, v_pre_mix_g, v_w_in, v_conv_a_w, v_sgu_ln_g, v_sgu_ln_b, v_sgu_w, v_sgu_b, v_conv_c_w, v_conv_c_b, v_conv_ln_g, v_conv_ln_b, v_grp_norm_g, v_w_out, v_post_mix_g, v_pre_ffn_g, v_w_up, v_ffn_conv_w, v_w_down, v_post_ffn_g):
    w = dict(pre_mix_g=pre_mix_g, w_in=w_in, conv_a_w=conv_a_w, sgu_ln_g=sgu_ln_g, sgu_ln_b=sgu_ln_b, sgu_w=sgu_w, sgu_b=sgu_b,
             conv_c_w=conv_c_w, conv_c_b=conv_c_b, conv_ln_g=conv_ln_g, conv_ln_b=conv_ln_b, grp_norm_g=grp_norm_g,
             w_out=w_out, post_mix_g=post_mix_g, pre_ffn_g=pre_ffn_g, w_up=w_up, ffn_conv_w=ffn_conv_w, w_down=w_down,
             post_ffn_g=post_ffn_g)
    m = dict(pre_mix_g=m_pre_mix_g, w_in=m_w_in, conv_a_w=m_conv_a_w, sgu_ln_g=m_sgu_ln_g, sgu_ln_b=m_sgu_ln_b,
             sgu_w=m_sgu_w, sgu_b=m_sgu_b, conv_c_w=m_conv_c_w, conv_c_b=m_conv_c_b, conv_ln_g=m_conv_ln_g,
             conv_ln_b=m_conv_ln_b, grp_norm_g=m_grp_norm_g, w_out=m_w_out, post_mix_g=m_post_mix_g,
             pre_ffn_g=m_pre_ffn_g, w_up=m_w_up, ffn_conv_w=m_ffn_conv_w, w_down=m_w_down, post_ffn_g=m_post_ffn_g)
    v = dict(pre_mix_g=v_pre_mix_g, w_in=v_w_in, conv_a_w=v_conv_a_w, sgu_ln_g=v_sgu_ln_g, sgu_ln_b=v_sgu_ln_b,
             sgu_w=v_sgu_w, sgu_b=v_sgu_b, conv_c_w=v_conv_c_w, conv_c_b=v_conv_c_b, conv_ln_g=v_conv_ln_g,
             conv_ln_b=v_conv_ln_b, grp_norm_g=v_grp_norm_g, w_out=v_w_out, post_mix_g=v_post_mix_g,
             pre_ffn_g=v_pre_ffn_g, w_up=v_w_up, ffn_conv_w=v_ffn_conv_w, w_down=v_w_down, post_ffn_g=v_post_ffn_g)
    chip = 2 * lax.axis_index("x") + lax.axis_index("y")

    convs = gather_small(_pack([w[n] for n in CONV], CONV_ROWS))
    gws = [{n: cast_shard(w[n], layer, chip) for n in BIG} for layer in range(DEPTH)]
    first = run_rider(pass_rider(run_rider(spread_rider([gws[0][n] for n in BIG]))))
    gws[0] = dict(zip(BIG, first))
    cparts = [_unpack(convs[j], CONV_SHARDS) for j in range(N_CHIPS)]
    full = dict(w)
    for i, n in enumerate(CONV):
        full[n] = _join_cols(jnp.stack([p[i] for p in cparts]))

    xc = to_tiles(x[0], TILE_EW)
    saved = []
    for layer in range(DEPTH):
        nxt = gws[layer + 1] if layer + 1 < DEPTH else None
        xc, sv, nxt = layer_fwd(xc, {n: full[n][layer] for n in REPL + CONV}, gws[layer], nxt)
        if nxt:
            gws[layer + 1] = nxt
        saved.append(sv)
    dxc, loss_part = loss_head(xc, to_tiles(loss_target[0], TILE_EW), TILE_MM)
    loss = lax.psum(loss_part[0, 0], ("x", "y", "c"))
    small = [None] * DEPTH
    pend = None
    for layer in reversed(range(DEPTH)):
        dxc, small[layer], pend = layer_bwd(dxc, {n: full[n][layer] for n in REPL + CONV}, layer, saved[layer], pend)
    grads = {n: jnp.stack([small[layer][n] for layer in range(DEPTH)]) for n in REPL + CONV}

    gsmall = _pack([grads[n] for n in REPL + CONV], SMALL_ROWS).reshape(N_CHIPS, SMALL_ROWS // N_CHIPS, LANES)
    sums = pend["narrow"] + add_halves([gsmall], run_rider(swap_rider([gsmall])), wire=F32)
    arrived = run_rider(scatter_rider([sw for _, sw in sums]))
    fbuf = list(pend["fbuf"])
    at = [BIG.index(n) for n in NARROW]
    for i, f in zip(at, add_chips([own for own, _ in sums[:2]], arrived[:2], [fbuf[i] for i in at], 0)):
        fbuf[i] = f
    fbuf += add_chips([sums[2][0]], arrived[2:], [lax.empty(gsmall.shape, F32)])
    joined = run_rider(join_rider(fbuf, [0] * len(BIG) + [None]))
    out_g = dict(zip(BIG, joined))
    tot = run_rider(pass_rider(run_rider(spread_rider([joined[len(BIG)]]))))[0].reshape(SMALL_ROWS, LANES)
    shapes = [grads[n].shape for n in REPL + CONV]
    for n, gfull in zip(REPL + CONV, _unpack(tot, shapes)):
        if n in CONV:
            width = gfull.shape[-1] // N_CHIPS
            gfull = lax.dynamic_slice_in_dim(gfull, chip * width, width, axis=2)
        out_g[n] = gfull

    deltas, new_m, new_v = {}, {}, {}
    for n in WEIGHTS:
        deltas[n], new_m[n], new_v[n] = adamw(w[n], out_g[n], m[n], v[n])
    return (loss, from_tiles(dxc, TILE_EW)[None], *[out_g[n] for n in WEIGHTS], *[deltas[n] for n in WEIGHTS], *[new_m[n] for n in WEIGHTS],
            *[new_v[n] for n in WEIGHTS])
```

```python
import math
from typing import Callable, NamedTuple

import jax
import jax.numpy as jnp
from jax import lax
from jax.experimental import pallas as pl
from jax.experimental.pallas import tpu as pltpu

F32 = jnp.float32
BF16 = jnp.bfloat16
MM_DTYPE = BF16

D_MODEL = 1024
DEPTH = 4
D_A = 256
D_B = 384
D_C = 384
D_IN = 3 * D_A + 2 * D_B + 2 * D_C
D_FF = 2816
K_A = 3
K_C = 31
K_F = 3
CHUNK = 128
HEAD = 64
N_HEADS_B = D_B // HEAD
EPS = 1e-6
N_CHIPS = 4

ADAM_LR = 0.001
ADAM_B1 = 0.9
ADAM_B2 = 0.999
ADAM_EPS = 1e-08
ADAM_WD = 0.01
ADAM_STEP = 10

LANES = 1024
VMEM_LIMIT = 56 * 1024 * 1024

MESH_ID = pl.DeviceIdType.MESH
_ANY = pl.BlockSpec(memory_space=pl.ANY)


def _params(sem=None):
    return pltpu.CompilerParams(dimension_semantics=sem, vmem_limit_bytes=VMEM_LIMIT)


def _const_spec(shape):
    nd = len(shape)
    return pl.BlockSpec(shape, lambda *_: (0,) * nd, pipeline_mode=pl.Buffered(1))


def _rowsum8(a):
    r, c = a.shape
    return jnp.sum(a.reshape(r // 8, 8, c), axis=0)


def _rstd(x):
    return lax.rsqrt(jnp.mean(x * x, axis=-1, keepdims=True) + EPS)


def _rms_bwd(x, r, g, dy):
    gdy = g * dy
    return r * gdy - x * (r * r * r) * jnp.mean(gdy * x, axis=-1, keepdims=True)


def _ln_fwd(x):
    mu = jnp.mean(x, axis=-1, keepdims=True)
    xc = x - mu
    r = lax.rsqrt(jnp.mean(xc * xc, axis=-1, keepdims=True) + EPS)
    return xc * r, r


def _ln_bwd(xh, r, dxh):
    return r * (dxh - jnp.mean(dxh, axis=-1, keepdims=True) - xh * jnp.mean(dxh * xh, axis=-1, keepdims=True))


def _gelu(x):
    return 0.5 * x * (1.0 + lax.erf(x * (1.0 / math.sqrt(2.0))))


def _gelu_grad(x):
    cdf = 0.5 * (1.0 + lax.erf(x * (1.0 / math.sqrt(2.0))))
    pdf = jnp.exp(-0.5 * x * x) * (1.0 / math.sqrt(2.0 * math.pi))
    return cdf + x * pdf


def _dot(a, b):
    return jnp.dot(a, b, preferred_element_type=F32)


def _dot_nt(a, b):
    return lax.dot_general(a, b, (((1,), (1,)), ((), ())), preferred_element_type=F32)


def _dot_tn(a, b):
    return lax.dot_general(a, b, (((0,), (0,)), ((), ())), preferred_element_type=F32)


def _col_chunk(n):
    for c in (1408, 1024, 768, 512, 256, 128):
        if n % c == 0:
            return c
    raise ValueError(n)


class Rider(NamedTuple):
    name: str
    inputs: list
    out_shapes: list
    aliases: dict
    sems: tuple
    start: Callable
    wait: Callable


def _pallas(body, rider, *, name, steps, in_specs, out_specs, out_shape, scratch_shapes, args, init=None):
    if rider is None:
        def plain(*refs):
            if init is not None:
                pl.when(pl.program_id(0) == 0)(lambda: init(*refs))
            body(*refs)

        res = pl.pallas_call(plain, name=name, grid=(steps,), in_specs=in_specs, out_specs=out_specs, out_shape=out_shape,
                             scratch_shapes=scratch_shapes, compiler_params=_params(("arbitrary",)))(*args)
        return res, []
    n_in, n_out, n_scr = len(in_specs), len(out_specs), len(scratch_shapes)
    r_in, r_out = len(rider.inputs), len(rider.out_shapes)

    def wrapped(*refs):
        ins, rin = refs[:n_in], refs[n_in:n_in + r_in]
        at = n_in + r_in
        outs, rout = refs[at:at + n_out], refs[at + n_out:at + n_out + r_out]
        at += n_out + r_out
        scr, rsem = refs[at:at + n_scr], refs[at + n_scr:]

        @pl.when(pl.program_id(0) == 0)
        def _():
            if init is not None:
                init(*ins, *outs, *scr)
            rider.start(rin, rout, rsem)

        body(*ins, *outs, *scr)

        @pl.when(pl.program_id(0) == steps - 1)
        def _():
            rider.wait(rin, rout, rsem)

    res = pl.pallas_call(
        wrapped, name=name + "_" + rider.name, grid=(steps,), in_specs=list(in_specs) + [_ANY] * r_in,
        out_specs=list(out_specs) + [_ANY] * r_out, out_shape=list(out_shape) + list(rider.out_shapes),
        scratch_shapes=list(scratch_shapes) + [pltpu.SemaphoreType.DMA((n,)) for n in rider.sems],
        input_output_aliases={n_in + i: n_out + o for i, o in rider.aliases.items()},
        compiler_params=pltpu.CompilerParams(dimension_semantics=("arbitrary",), vmem_limit_bytes=VMEM_LIMIT,
                                             has_side_effects=True),
    )(*args, *rider.inputs)
    return res[:n_out], res[n_out:]


def run_rider(rider):
    def body(*refs):
        r_in, r_out = len(rider.inputs), len(rider.out_shapes)
        rin, rout, rsem = refs[:r_in], refs[r_in:r_in + r_out], refs[r_in + r_out:]
        rider.start(rin, rout, rsem)
        rider.wait(rin, rout, rsem)

    return pl.pallas_call(
        body, name=rider.name, in_specs=[_ANY] * len(rider.inputs), out_specs=[_ANY] * len(rider.out_shapes),
        out_shape=list(rider.out_shapes), scratch_shapes=[pltpu.SemaphoreType.DMA((n,)) for n in rider.sems],
        input_output_aliases=dict(rider.aliases), compiler_params=pltpu.CompilerParams(has_side_effects=True),
    )(*rider.inputs)


def _weight_spec(wg):
    return _const_spec(wg.shape)


def _join_col_blocks(w_ref, w_scr):
    c = w_ref.shape[2]
    for j in range(N_CHIPS):
        w_scr[:, c * j:c * (j + 1)] = w_ref[j]


def matmul_nt_norm_bwd(gy, wg, x, g, dres, tm, rider=None):
    t, n = gy.shape
    assert t % tm == 0, (t, tm)
    d, cw = wg.shape[1], wg.shape[2]
    aligned = cw % 128 == 0
    cn = cw if aligned else _col_chunk(n)
    steps = t // tm

    def body(gy_ref, w_ref, x_ref, g_ref, dres_ref, dx_ref, dg_ref, acc_ref, *scr):
        i = pl.program_id(0)

        @pl.when(i == 0)
        def _():
            acc_ref[...] = jnp.zeros_like(acc_ref)
            if not aligned:
                _join_col_blocks(w_ref, scr[0])

        dh = jnp.zeros((tm, d), F32)
        for j, c0 in enumerate(range(0, n, cn)):
            wv = w_ref[j] if aligned else scr[0][:, c0:c0 + cn]
            dh = dh + _dot_nt(gy_ref[:, c0:c0 + cn], wv)
        xv = x_ref[...]
        r = _rstd(xv)
        gv = g_ref[...]
        dx_ref[...] = dres_ref[...] + _rms_bwd(xv, r, gv, dh)
        acc_ref[...] += _rowsum8(dh * xv * r)

        @pl.when(i == steps - 1)
        def _():
            dg_ref[...] = jnp.sum(acc_ref[...], axis=0, keepdims=True)

    return _pallas(
        body, rider, name="matmul_nt_norm_bwd", steps=steps,
        in_specs=[pl.BlockSpec((tm, n), lambda i: (i, 0)), _weight_spec(wg), pl.BlockSpec((tm, d), lambda i: (i, 0)),
                  _const_spec((1, d)), pl.BlockSpec((tm, d), lambda i: (i, 0))],
        out_specs=[pl.BlockSpec((tm, d), lambda i: (i, 0)), pl.BlockSpec((1, d), lambda i: (0, 0))],
        out_shape=[jax.ShapeDtypeStruct((t, d), F32), jax.ShapeDtypeStruct((1, d), F32)],
        scratch_shapes=[pltpu.VMEM((8, d), F32)] + ([] if aligned else [pltpu.VMEM((d, n), MM_DTYPE)]),
        args=(gy, wg, x, g, dres))


def matmul_tn_cols(a, b, tk):
    t, r = a.shape
    c = b.shape[1] // N_CHIPS
    assert t % tk == 0 and r % 8 == 0 and c % 128 == 0, (a.shape, b.shape, tk)

    def body(a_ref, b_ref, o_ref):
        @pl.when(pl.program_id(1) == 0)
        def _():
            o_ref[...] = jnp.zeros_like(o_ref)

        o_ref[...] += _dot_tn(a_ref[...], b_ref[...])

    a_spec = pl.BlockSpec((tk, r), lambda j, k: (k, 0))
    b_spec = pl.BlockSpec((tk, c), lambda j, k: (k, j))
    return pl.pallas_call(
        body, name="matmul_tn_cols", grid=(N_CHIPS, t // tk), in_specs=[a_spec, b_spec],
        out_specs=pl.BlockSpec((None, r, c), lambda j, k: (j, 0, 0)),
        out_shape=jax.ShapeDtypeStruct((N_CHIPS, r, c), F32),
        compiler_params=_params(("arbitrary", "arbitrary")),
    )(a, b)


def matmul_tn_down(act, dd, tk):
    t, m = act.shape
    c = dd.shape[1]
    r = m // N_CHIPS
    assert t % tk == 0, (t, tk)
    steps = t // tk

    def body(a_ref, b_ref, o_ref, acc):
        k = pl.program_id(1)

        @pl.when(k == 0)
        def _():
            acc[...] = jnp.zeros_like(acc)

        acc[...] += _dot_tn(a_ref[...], b_ref[...])

        @pl.when(k == steps - 1)
        def _():
            o_ref[0] = acc[0:r, :]
            o_ref[1] = acc[r:2 * r, :]

    return pl.pallas_call(
        body, name="matmul_tn_down", grid=(2, steps),
        in_specs=[pl.BlockSpec((tk, 2 * r), lambda p, k: (k, p)), pl.BlockSpec((tk, c), lambda p, k: (k, 0))],
        out_specs=pl.BlockSpec((2, r, c), lambda p, k: (p, 0, 0)),
        out_shape=jax.ShapeDtypeStruct((N_CHIPS, r, c), F32),
        scratch_shapes=[pltpu.VMEM((2 * r, c), F32)],
        compiler_params=_params(("arbitrary", "arbitrary")),
    )(act, dd)


def matmul_tn_in(h, dz, tk):
    t, d = h.shape
    n = dz.shape[1]
    c = n // N_CHIPS
    assert t % tk == 0, (t, tk)
    steps = t // tk

    def body(a_ref, b_ref, o_ref, acc):
        k = pl.program_id(0)

        @pl.when(k == 0)
        def _():
            acc[...] = jnp.zeros_like(acc)

        acc[...] += _dot_tn(a_ref[...], b_ref[...])

        @pl.when(k == steps - 1)
        def _():
            for j in range(N_CHIPS):
                o_ref[j] = acc[:, c * j:c * (j + 1)]

    return pl.pallas_call(
        body, name="matmul_tn_in", grid=(steps,),
        in_specs=[pl.BlockSpec((tk, d), lambda k: (k, 0)), pl.BlockSpec((tk, n), lambda k: (k, 0))],
        out_specs=pl.BlockSpec((N_CHIPS, d, c), lambda k: (0, 0, 0)),
        out_shape=jax.ShapeDtypeStruct((N_CHIPS, d, c), F32),
        scratch_shapes=[pltpu.VMEM((d, n), F32)],
        compiler_params=_params(("arbitrary",)),
    )(h, dz)


def to_tiles(a, tt):
    t = a.shape[0]
    return a.reshape((t // tt, 8, tt // 8) + a.shape[1:]).swapaxes(1, 2).reshape(a.shape)


def from_tiles(a, tt):
    t = a.shape[0]
    return a.reshape((t // tt, tt // 8, 8) + a.shape[1:]).swapaxes(1, 2).reshape(a.shape)


def _roll_sublanes(a, shift):
    n = a.shape[0] // 8
    return pltpu.roll(a.reshape(n, 8, a.shape[1]), shift, 1).reshape(a.shape)


def _halo_before(cur_last, prev_last):
    sub = lax.broadcasted_iota(jnp.int32, cur_last.shape, 0) % 8
    return jnp.where(sub == 0, _roll_sublanes(prev_last, 1), _roll_sublanes(cur_last, 1))


def _halo_after(cur_first, next_first):
    sub = lax.broadcasted_iota(jnp.int32, cur_first.shape, 0) % 8
    return jnp.where(sub == 7, _roll_sublanes(next_first, 7), _roll_sublanes(cur_first, 7))


def _conv_causal(ext, cur, prev_last, w, taps, tt, cols=None):
    hr = 8 * (taps - 1)
    cs = slice(None) if cols is None else cols
    ext[hr:hr + tt, cs] = cur
    ext[0:hr, cs] = _halo_before(cur[tt - hr:, :], prev_last)
    acc = w[0:1, :] * ext[0:tt, cs]
    for k in range(1, taps):
        acc = acc + w[k:k + 1, :] * ext[8 * k:8 * k + tt, cs]
    return acc


def _conv_anticausal(ext, cur, next_first, w, taps, tt, x=None, acc_w=None, cols=None):
    hr = 8 * (taps - 1)
    cs = slice(None) if cols is None else cols
    ext[0:tt, cs] = cur
    ext[tt:tt + hr, cs] = _halo_after(cur[0:hr, :], next_first)
    acc = None
    for k in range(taps):
        off = 8 * (taps - 1 - k)
        ld = ext[off:off + tt, cs]
        term = w[k:k + 1, :] * ld
        acc = term if acc is None else acc + term
        if x is not None:
            acc_w[k, :, cs] += _rowsum8(ld * x)
    return acc


def _dot_exact(a, b, dims):
    return lax.dot_general(a, b, (dims, ((), ())), precision=lax.Precision.HIGHEST, preferred_element_type=F32)


def _to_tile_order(perm, wt_ref, w_scr, transpose):
    pb = perm.astype(MM_DTYPE)
    for h in range(N_HEADS_B):
        half = (_dot_nt(pb, wt_ref[h]) if transpose else _dot(pb, wt_ref[h])).astype(MM_DTYPE)
        w_scr[h] = _dot_nt(half, pb).astype(MM_DTYPE)


def _project_rows(y, w_ref):
    r = w_ref.shape[1]
    acc = _dot(y[:, 0:r], w_ref[0])
    for j in range(1, N_CHIPS):
        acc = acc + _dot(y[:, r * j:r * (j + 1)], w_ref[j])
    return acc


PAIR = 2 * HEAD


def _pair_lanes(h):
    return slice(PAIR * (h // 2), PAIR * (h // 2 + 1))


def _per_head(fn):
    cols = []
    for p in range(N_HEADS_B // 2):
        lo, hi = fn(2 * p), fn(2 * p + 1)
        cols.append(jnp.where(lax.broadcasted_iota(jnp.int32, lo.shape, 1) < HEAD, lo, hi))
    return jnp.concatenate(cols, axis=1)


def _mixer_forward(z, prm, q, yc):
    _, lng, lnb, wm, bias_p, _, _, clg, clb = prm
    bg = z[:, 0:D_A]
    ya = bg * q
    o_b = 3 * D_A
    zu = z[:, o_b:o_b + D_B]
    zv = z[:, o_b + D_B:o_b + 2 * D_B]
    u = _gelu(zu)
    vh, rv = _ln_fwd(_gelu(zv))
    vnb = (vh * lng + lnb).astype(MM_DTYPE)
    s = _per_head(lambda h: _dot(wm[h], vnb[:, _pair_lanes(h)])) + bias_p
    yb = u * s
    yh, rc = _ln_fwd(yc)
    l = yh * clg + clb
    sl = jax.nn.sigmoid(l)
    return dict(bg=bg, q=q, ya=ya, zu=zu, zv=zv, u=u, vh=vh, rv=rv, vnb=vnb, s=s, yb=yb, yh=yh, rc=rc, l=l, sl=sl,
                yo=l * sl)


def _conv_inputs(z):
    o_c = 3 * D_A + 2 * D_B
    a = z[:, o_c:o_c + D_C]
    sg = jax.nn.sigmoid(z[:, o_c + D_C:o_c + 2 * D_C])
    return z[:, D_A:2 * D_A] * z[:, 2 * D_A:3 * D_A], a * sg, a, sg


def _group_norm(f, gg):
    ya, yb, yo = f["ya"], f["yb"], f["yo"]
    ra, rb, ro = _rstd(ya), _rstd(yb), _rstd(yo)
    yn = jnp.concatenate([ya * ra * gg[:, 0:D_A], yb * rb * gg[:, D_A:D_A + D_B], yo * ro * gg[:, D_A + D_B:]], axis=1)
    return yn, (ra, rb, ro)


def _mixer_prm(refs, wp_scr, bias_scr):
    caw_ref, lng_ref, lnb_ref, _, _, ccw_ref, ccb_ref, clg_ref, clb_ref = refs
    wm = [wp_scr[h] for h in range(N_HEADS_B)]
    return (caw_ref[...], lng_ref[...], lnb_ref[...], wm, bias_scr[...], ccw_ref[...], ccb_ref[...], clg_ref[...],
            clb_ref[...])


def _mixer_param_specs(tt):
    return [_const_spec((8, D_A)), _const_spec((1, D_B)), _const_spec((1, D_B)), _const_spec((N_HEADS_B, tt, tt)),
            _const_spec((tt, D_B)), _const_spec((32, D_C)), _const_spec((1, D_C)), _const_spec((1, D_C)),
            _const_spec((1, D_C))]


HR_A = 8 * (K_A - 1)
HR_C = 8 * (K_C - 1)
HR_F = 8 * (K_F - 1)


def mixer_fwd(x, g, wig, mp, perm, grp_g, wog, post_g, tt, rider=None):
    t = x.shape[0]
    assert t % tt == 0 and tt % CHUNK == 0 and tt >= HR_C, (t, tt)
    cn = _col_chunk(D_IN)

    def body(x_ref, g_ref, wi_ref, *rest):
        prm_refs = rest[:9]
        (perm_ref, gg_ref, wo_ref, pg_ref, z_ref, h_ref, o_ref, x1_ref, cv_ref, pa_ext, yg_ext, pa_last, yg_last, wp_scr,
         bias_scr, wi_scr) = rest[9:]
        i = pl.program_id(0)

        @pl.when(i == 0)
        def _():
            pa_last[...] = jnp.zeros_like(pa_last)
            yg_last[...] = jnp.zeros_like(yg_last)
            _to_tile_order(perm_ref[...], prm_refs[3], wp_scr, False)
            bias_scr[...] = _dot_exact(perm_ref[...], prm_refs[4][...], ((1,), (0,)))
            _join_col_blocks(wi_ref, wi_scr)

        xv = x_ref[...]
        h = (xv * _rstd(xv) * g_ref[...]).astype(MM_DTYPE)
        h_ref[...] = h
        for c0 in range(0, D_IN, cn):
            z_ref[:, c0:c0 + cn] = _dot(h, wi_scr[:, c0:c0 + cn])
        zv = z_ref[...]
        prm = _mixer_prm(prm_refs, wp_scr, bias_scr)
        pa, yg, _, _ = _conv_inputs(zv)
        q = _conv_causal(pa_ext, pa, pa_last[...], prm[0], K_A, tt)
        yc = _conv_causal(yg_ext, yg, yg_last[...], prm[5], K_C, tt) + prm[6]
        pa_last[...] = pa[tt - HR_A:, :]
        yg_last[...] = yg[tt - HR_C:, :]
        cv_ref[:, 0:D_A] = q
        cv_ref[:, D_A:] = yc
        f = _mixer_forward(zv, prm, q, yc)
        yn, _ = _group_norm(f, gg_ref[...])
        o = _project_rows(yn.astype(MM_DTYPE), wo_ref)
        o_ref[...] = o
        x1_ref[...] = xv + o * _rstd(o) * pg_ref[...]

    row = lambda c: pl.BlockSpec((tt, c), lambda i: (i, 0))
    return _pallas(
        body, rider, name="mixer_fwd", steps=t // tt,
        in_specs=[row(D_MODEL), _const_spec((1, D_MODEL)), _weight_spec(wig)] + _mixer_param_specs(tt)
        + [_const_spec((tt, tt)), _const_spec((1, D_MODEL)), _weight_spec(wog), _const_spec((1, D_MODEL))],
        out_specs=[row(D_IN), row(D_MODEL), row(D_MODEL), row(D_MODEL), row(D_A + D_C)],
        out_shape=[jax.ShapeDtypeStruct((t, D_IN), F32), jax.ShapeDtypeStruct((t, D_MODEL), MM_DTYPE),
                   jax.ShapeDtypeStruct((t, D_MODEL), F32), jax.ShapeDtypeStruct((t, D_MODEL), F32),
                   jax.ShapeDtypeStruct((t, D_A + D_C), F32)],
        scratch_shapes=[pltpu.VMEM((HR_A + tt, D_A), F32), pltpu.VMEM((HR_C + tt, D_C), F32),
                        pltpu.VMEM((HR_A, D_A), F32), pltpu.VMEM((HR_C, D_C), F32),
                        pltpu.VMEM((N_HEADS_B, tt, tt), MM_DTYPE), pltpu.VMEM((tt, D_B), F32),
                        pltpu.VMEM((D_MODEL, D_IN), MM_DTYPE)],
        args=(x, g, wig, *mp, perm, grp_g, wog, post_g))


def mixer_bwd(dx1, o, z, cv, mp, perm, grp_g, wog, post_g, tt, rider=None):
    t = z.shape[0]
    assert t % tt == 0 and tt % CHUNK == 0 and tt >= HR_C, (t, tt)
    steps = t // tt

    def body(dx1_ref, o_ref, z_ref, cv_ref, *rest):
        prm_refs = rest[:9]
        (perm_ref, gg_ref, wo_ref, pg_ref,
         dz_ref, gwo_ref, dpg_ref, dgg_ref, dcaw_ref, dlng_ref, dlnb_ref, dwm_ref, dbias_ref, dccw_ref, dccb_ref,
         dclg_ref, dclb_ref,
         dq_ext, dyc_ext, dq_first, dyc_first, a_pg, a_gg, a_caw, a_lng, a_lnb, a_ccw, a_ccb, a_clg, a_clb,
         wp_scr, wpt_scr, bias_scr, a_wm, a_bias) = rest[9:]
        i = pl.program_id(0)
        small = (a_pg, a_gg, a_caw, a_lng, a_lnb, a_ccw, a_ccb, a_clg, a_clb)

        @pl.when(i == 0)
        def _():
            for ref in small + (a_wm, a_bias, dq_first, dyc_first, gwo_ref):
                ref[...] = jnp.zeros_like(ref)
            _to_tile_order(perm_ref[...], prm_refs[3], wp_scr, False)
            _to_tile_order(perm_ref[...], prm_refs[3], wpt_scr, True)
            bias_scr[...] = _dot_exact(perm_ref[...], prm_refs[4][...], ((1,), (0,)))

        prm = _mixer_prm(prm_refs, wp_scr, bias_scr)
        caw, lng, lnb, wm, bias_p, ccw, ccb, clg, clb = prm

        zv = z_ref[...]
        pa, yg, a, sg = _conv_inputs(zv)
        f = _mixer_forward(zv, prm, cv_ref[:, 0:D_A], cv_ref[:, D_A:])
        gg = gg_ref[...]
        yn, (ra, rb, ro) = _group_norm(f, gg)

        ov = o_ref[...]
        dx1v = dx1_ref[...]
        r_o = _rstd(ov)
        pg = pg_ref[...]
        a_pg[...] += _rowsum8(dx1v * ov * r_o)
        do = _rms_bwd(ov, r_o, pg, dx1v).astype(MM_DTYPE)
        gwo = _dot_tn(yn.astype(MM_DTYPE), do)
        rows = gwo_ref.shape[1]
        for j in range(N_CHIPS):
            gwo_ref[j] += gwo[rows * j:rows * (j + 1), :]
        dyn = jnp.concatenate([_dot_nt(do, wo_ref[j]) for j in range(N_CHIPS)], axis=1)

        dyn_a, dyn_b, dyn_c = dyn[:, 0:D_A], dyn[:, D_A:D_A + D_B], dyn[:, D_A + D_B:]
        ga, gb, gc = gg[:, 0:D_A], gg[:, D_A:D_A + D_B], gg[:, D_A + D_B:]
        a_gg[...] += _rowsum8(jnp.concatenate([dyn_a * f["ya"] * ra, dyn_b * f["yb"] * rb, dyn_c * f["yo"] * ro], axis=1))
        dya = _rms_bwd(f["ya"], ra, ga, dyn_a)
        dyb = _rms_bwd(f["yb"], rb, gb, dyn_b)
        dyo = _rms_bwd(f["yo"], ro, gc, dyn_c)

        dbg = dya * f["q"]
        dq = dya * f["bg"]
        dp = _conv_anticausal(dq_ext, dq, dq_first[...], caw, K_A, tt, x=pa, acc_w=a_caw)
        dq_first[...] = dq[0:HR_A, :]
        dcg = dp * zv[:, 2 * D_A:3 * D_A]
        dxa = dp * zv[:, D_A:2 * D_A]

        du = dyb * f["s"]
        ds = dyb * f["u"]
        dsb = ds.astype(MM_DTYPE)
        a_bias[...] += ds
        odd = lax.broadcasted_iota(jnp.int32, (tt, PAIR), 1) // HEAD
        for h in range(N_HEADS_B):
            dsp = dsb[:, _pair_lanes(h)]
            a_wm[h] += _dot_nt(jnp.where(odd == h % 2, dsp, jnp.zeros_like(dsp)), f["vnb"][:, _pair_lanes(h)])
        dvn = _per_head(lambda h: _dot(wpt_scr[h], dsb[:, _pair_lanes(h)]))
        a_lng[...] += _rowsum8(dvn * f["vh"])
        a_lnb[...] += _rowsum8(dvn)
        dv = _ln_bwd(f["vh"], f["rv"], dvn * lng)
        dzu = du * _gelu_grad(f["zu"])
        dzv = dv * _gelu_grad(f["zv"])

        l, sl = f["l"], f["sl"]
        dl = dyo * (sl * (1.0 + l * (1.0 - sl)))
        a_clg[...] += _rowsum8(dl * f["yh"])
        a_clb[...] += _rowsum8(dl)
        dyc = _ln_bwd(f["yh"], f["rc"], dl * clg)
        a_ccb[...] += _rowsum8(dyc)
        dy = _conv_anticausal(dyc_ext, dyc, dyc_first[...], ccw, K_C, tt, x=yg, acc_w=a_ccw)
        dyc_first[...] = dyc[0:HR_C, :]
        da = dy * sg
        dg = dy * a * sg * (1.0 - sg)

        dz_ref[...] = jnp.concatenate([dbg, dcg, dxa, dzu, dzv, da, dg], axis=1).astype(MM_DTYPE)

        @pl.when(i == steps - 1)
        def _():
            red = lambda ref: jnp.sum(ref[...], axis=0, keepdims=True)
            dpg_ref[...] = red(a_pg)
            dgg_ref[...] = red(a_gg)
            dlng_ref[...] = red(a_lng)
            dlnb_ref[...] = red(a_lnb)
            dccb_ref[...] = red(a_ccb)
            dclg_ref[...] = red(a_clg)
            dclb_ref[...] = red(a_clb)
            dcaw_ref[...] = jnp.sum(a_caw[...], axis=1)
            dccw_ref[...] = jnp.sum(a_ccw[...], axis=1)
            pm = perm_ref[...]
            tril = lax.broadcasted_iota(jnp.int32, (CHUNK, CHUNK), 0) >= lax.broadcasted_iota(jnp.int32, (CHUNK, CHUNK), 1)
            for h in range(N_HEADS_B):
                dwt = _dot_exact(pm, _dot_exact(a_wm[h], pm, ((1,), (0,))), ((0,), (0,)))
                dw = dwt[0:CHUNK, 0:CHUNK]
                for c in range(1, tt // CHUNK):
                    dw = dw + dwt[c * CHUNK:(c + 1) * CHUNK, c * CHUNK:(c + 1) * CHUNK]
                dwm_ref[h] = jnp.where(tril, dw, 0.0)
            dbt = _dot_exact(pm, a_bias[...], ((0,), (0,)))
            db = dbt[0:CHUNK, :]
            for c in range(1, tt // CHUNK):
                db = db + dbt[c * CHUNK:(c + 1) * CHUNK, :]
            lane_head = lax.broadcasted_iota(jnp.int32, (D_B, CHUNK), 0) // HEAD
            fold = (lane_head == lax.broadcasted_iota(jnp.int32, (D_B, CHUNK), 1)).astype(F32)
            dbias_ref[...] = _dot_exact(db, fold, ((1,), (0,)))

    rev = lambda c: pl.BlockSpec((tt, c), lambda i: (steps - 1 - i, 0))
    full = lambda shape: pl.BlockSpec(shape, lambda i: (0,) * len(shape))
    sds = jax.ShapeDtypeStruct
    return _pallas(
        body, rider, name="mixer_bwd", steps=steps,
        in_specs=[rev(D_MODEL), rev(D_MODEL), rev(D_IN), rev(D_A + D_C)] + _mixer_param_specs(tt)
        + [_const_spec((tt, tt)), _const_spec((1, D_MODEL)), _weight_spec(wog), _const_spec((1, D_MODEL))],
        out_specs=[rev(D_IN), full((N_CHIPS, D_MODEL // N_CHIPS, D_MODEL)), full((1, D_MODEL)), full((1, D_MODEL)), full((8, D_A)),
                   full((1, D_B)), full((1, D_B)), full((N_HEADS_B, CHUNK, CHUNK)), full((CHUNK, CHUNK)), full((32, D_C)),
                   full((1, D_C)), full((1, D_C)), full((1, D_C))],
        out_shape=[sds((t, D_IN), MM_DTYPE), sds((N_CHIPS, D_MODEL // N_CHIPS, D_MODEL), F32),
                   sds((1, D_MODEL), F32), sds((1, D_MODEL), F32), sds((8, D_A), F32), sds((1, D_B), F32), sds((1, D_B), F32),
                   sds((N_HEADS_B, CHUNK, CHUNK), F32), sds((CHUNK, CHUNK), F32), sds((32, D_C), F32), sds((1, D_C), F32),
                   sds((1, D_C), F32), sds((1, D_C), F32)],
        scratch_shapes=[pltpu.VMEM((tt + HR_A, D_A), F32), pltpu.VMEM((tt + HR_C, D_C), F32),
                        pltpu.VMEM((HR_A, D_A), F32), pltpu.VMEM((HR_C, D_C), F32),
                        pltpu.VMEM((8, D_MODEL), F32), pltpu.VMEM((8, D_MODEL), F32), pltpu.VMEM((8, 8, D_A), F32),
                        pltpu.VMEM((8, D_B), F32), pltpu.VMEM((8, D_B), F32), pltpu.VMEM((32, 8, D_C), F32),
                        pltpu.VMEM((8, D_C), F32), pltpu.VMEM((8, D_C), F32), pltpu.VMEM((8, D_C), F32),
                        pltpu.VMEM((N_HEADS_B, tt, tt), MM_DTYPE), pltpu.VMEM((N_HEADS_B, tt, tt), MM_DTYPE),
                        pltpu.VMEM((tt, D_B), F32), pltpu.VMEM((N_HEADS_B, tt, tt), F32), pltpu.VMEM((tt, D_B), F32)],
        args=(dx1, o, z, cv, *mp, perm, grp_g, wog, post_g))


def _fetch_row_blocks(wg_ref, w_scr, sems):
    r = wg_ref.shape[1]
    copies = [pltpu.make_async_copy(wg_ref.at[j], w_scr.at[pl.ds(r * j, r), :], sems.at[j]) for j in range(N_CHIPS)]
    for cp in copies:
        cp.start()
    for cp in copies:
        cp.wait()


def _ffn_conv(ext, cw, c0, cn, tt):
    acc = cw[0:1, c0:c0 + cn] * ext[0:tt, c0:c0 + cn]
    for k in range(1, K_F):
        acc = acc + cw[k:k + 1, c0:c0 + cn] * ext[8 * k:8 * k + tt, c0:c0 + cn]
    return acc


def ffn_fwd(x1, g, wug, cw, wdg, post_g, tt, rider=None):
    t, dm = x1.shape
    assert t % tt == 0, (t, tt)
    cn = _col_chunk(D_FF)
    cu = wug.shape[2]

    def init(x1_ref, g_ref, wu_ref, cw_ref, wdg_ref, pg_ref, up0_ref, h2_ref, d_ref, x2_ref, ext, last, wd_ref, sems):
        _fetch_row_blocks(wdg_ref, wd_ref, sems)
        last[...] = jnp.zeros_like(last)

    def body(x1_ref, g_ref, wu_ref, cw_ref, wdg_ref, pg_ref, up0_ref, h2_ref, d_ref, x2_ref, ext, last, wd_ref, sems):
        xv = x1_ref[...]
        h = (xv * _rstd(xv) * g_ref[...]).astype(MM_DTYPE)
        h2_ref[...] = h
        for j in range(N_CHIPS):
            u = _dot(h, wu_ref[j])
            up0_ref[:, cu * j:cu * (j + 1)] = u
            ext[HR_F:HR_F + tt, cu * j:cu * (j + 1)] = u
        ext[0:HR_F, :] = _halo_before(ext[tt:tt + HR_F, :], last[...])
        last[...] = ext[tt:tt + HR_F, :]
        cwv = cw_ref[...]
        d = jnp.zeros((tt, D_MODEL), F32)
        for c0 in range(0, D_FF, cn):
            gate = _ffn_conv(ext, cwv, c0, cn, tt)
            val = _ffn_conv(ext, cwv, D_FF + c0, cn, tt)
            act = (gate * jax.nn.sigmoid(gate) * val).astype(MM_DTYPE)
            d = d + _dot(act, wd_ref[c0:c0 + cn, :])
        d_ref[...] = d
        x2_ref[...] = xv + d * _rstd(d) * pg_ref[...]

    row = lambda c: pl.BlockSpec((tt, c), lambda i: (i, 0))
    return _pallas(
        body, rider, name="ffn_fwd", steps=t // tt, init=init,
        in_specs=[row(dm), _const_spec((1, dm)), _weight_spec(wug), _const_spec((8, 2 * D_FF)), _ANY, _const_spec((1, dm))],
        out_specs=[row(2 * D_FF), row(dm), row(dm), row(dm)],
        out_shape=[jax.ShapeDtypeStruct((t, 2 * D_FF), F32), jax.ShapeDtypeStruct((t, dm), MM_DTYPE),
                   jax.ShapeDtypeStruct((t, dm), F32), jax.ShapeDtypeStruct((t, dm), F32)],
        scratch_shapes=[pltpu.VMEM((HR_F + tt, 2 * D_FF), F32), pltpu.VMEM((HR_F, 2 * D_FF), F32),
                        pltpu.VMEM((D_FF, D_MODEL), MM_DTYPE), pltpu.SemaphoreType.DMA((N_CHIPS,))],
        args=(x1, g, wug, cw, wdg, post_g))


def ffn_bwd(dx2, d, up0, cw, wdg, post_g, tt, rider=None):
    t = up0.shape[0]
    assert t % tt == 0, (t, tt)
    steps = t // tt
    hb = tt // HR_F
    cn = _col_chunk(D_FF)

    def init(dx2_ref, d_ref, up0_ref, uh_ref, cw_ref, wdg_ref, pg_ref,
             dd_ref, act_ref, dup0_ref, dpg_ref, dcw_ref, ext, dup_ext, first, a_pg, a_cw, wd_ref, sems):
        _fetch_row_blocks(wdg_ref, wd_ref, sems)
        a_pg[...] = jnp.zeros_like(a_pg)
        a_cw[...] = jnp.zeros_like(a_cw)
        first[...] = jnp.zeros_like(first)

    def body(dx2_ref, d_ref, up0_ref, uh_ref, cw_ref, wdg_ref, pg_ref,
             dd_ref, act_ref, dup0_ref, dpg_ref, dcw_ref, ext, dup_ext, first, a_pg, a_cw, wd_ref, sems):
        i = pl.program_id(0)
        tile = steps - 1 - i

        ext[HR_F:HR_F + tt, :] = up0_ref[...]
        ext[0:HR_F, :] = _halo_before(up0_ref[tt - HR_F:, :], jnp.where(tile > 0, uh_ref[...], 0.0))
        cwv = cw_ref[...]
        dv = d_ref[...]
        dx2v = dx2_ref[...]
        r = _rstd(dv)
        a_pg[...] += _rowsum8(dx2v * dv * r)
        dd = _rms_bwd(dv, r, pg_ref[...], dx2v).astype(MM_DTYPE)
        dd_ref[...] = dd
        for c0 in range(0, D_FF, cn):
            gate = _ffn_conv(ext, cwv, c0, cn, tt)
            val = _ffn_conv(ext, cwv, D_FF + c0, cn, tt)
            sg = jax.nn.sigmoid(gate)
            sl = gate * sg
            act_ref[:, c0:c0 + cn] = (sl * val).astype(MM_DTYPE)
            da = _dot_nt(dd, wd_ref[c0:c0 + cn, :])
            dup_ext[0:tt, c0:c0 + cn] = da * val * (sg * (1.0 + gate * (1.0 - sg)))
            dup_ext[0:tt, D_FF + c0:D_FF + c0 + cn] = da * sl
        dup_ext[tt:tt + HR_F, :] = _halo_after(dup_ext[0:HR_F, :], first[...])
        first[...] = dup_ext[0:HR_F, :]
        for c0 in range(0, 2 * D_FF, cn):
            x = up0_ref[:, c0:c0 + cn]
            acc = None
            for k in range(K_F):
                off = 8 * (K_F - 1 - k)
                ld = dup_ext[off:off + tt, c0:c0 + cn]
                term = cwv[k:k + 1, c0:c0 + cn] * ld
                acc = term if acc is None else acc + term
                a_cw[k, :, c0:c0 + cn] += _rowsum8(ld * x)
            dup0_ref[:, c0:c0 + cn] = acc.astype(MM_DTYPE)

        @pl.when(i == steps - 1)
        def _():
            dpg_ref[...] = jnp.sum(a_pg[...], axis=0, keepdims=True)
            dcw_ref[...] = jnp.sum(a_cw[...], axis=1)

    rev = lambda c: pl.BlockSpec((tt, c), lambda i: (steps - 1 - i, 0))
    halo = pl.BlockSpec((HR_F, 2 * D_FF), lambda i: (jnp.maximum((steps - 1 - i) * hb - 1, 0), 0))
    full = lambda shape: pl.BlockSpec(shape, lambda i: (0,) * len(shape))
    sds = jax.ShapeDtypeStruct
    return _pallas(
        body, rider, name="ffn_bwd", steps=steps, init=init,
        in_specs=[rev(D_MODEL), rev(D_MODEL), rev(2 * D_FF), halo, _const_spec((8, 2 * D_FF)), _ANY,
                  _const_spec((1, D_MODEL))],
        out_specs=[rev(D_MODEL), rev(D_FF), rev(2 * D_FF), full((1, D_MODEL)), full((8, 2 * D_FF))],
        out_shape=[sds((t, D_MODEL), MM_DTYPE), sds((t, D_FF), MM_DTYPE), sds((t, 2 * D_FF), MM_DTYPE),
                   sds((1, D_MODEL), F32), sds((8, 2 * D_FF), F32)],
        scratch_shapes=[pltpu.VMEM((HR_F + tt, 2 * D_FF), F32), pltpu.VMEM((tt + HR_F, 2 * D_FF), F32),
                        pltpu.VMEM((HR_F, 2 * D_FF), F32), pltpu.VMEM((8, D_MODEL), F32),
                        pltpu.VMEM((8, 8, 2 * D_FF), F32), pltpu.VMEM((D_FF, D_MODEL), MM_DTYPE),
                        pltpu.SemaphoreType.DMA((N_CHIPS,))],
        args=(dx2, d, up0, up0, cw, wdg, post_g))


def loss_head(y, target, tm):
    t, d = y.shape
    assert t % tm == 0, (t, tm)
    steps = t // tm

    def body(y_ref, t_ref, dy_ref, loss_ref, acc):
        i = pl.program_id(0)

        @pl.when(i == 0)
        def _():
            acc[...] = jnp.zeros_like(acc)

        diff = y_ref[...] - t_ref[...]
        dy_ref[...] = diff * (1.0 / d)
        acc[...] += _rowsum8(diff * diff)

        @pl.when(i == steps - 1)
        def _():
            loss_ref[...] = (0.5 / d) * jnp.sum(jnp.sum(acc[...], axis=0, keepdims=True), axis=1, keepdims=True)

    row = pl.BlockSpec((tm, d), lambda i: (i, 0))
    return pl.pallas_call(
        body, name="loss_head", grid=(steps,), in_specs=[row, row],
        out_specs=[row, pl.BlockSpec((1, 1), lambda i: (0, 0))],
        out_shape=[jax.ShapeDtypeStruct((t, d), F32), jax.ShapeDtypeStruct((1, 1), F32)],
        scratch_shapes=[pltpu.VMEM((8, d), F32)],
        compiler_params=_params(("arbitrary",)),
    )(y, target)


def adamw(w, g, m, v):
    shape = w.shape
    cols = shape[-1]
    rows = w.size // cols
    tr = next((r for r in (512, 256, 128) if rows % r == 0 and rows > r), rows)
    c1 = 1.0 - ADAM_B1 ** ADAM_STEP
    c2 = 1.0 - ADAM_B2 ** ADAM_STEP

    def body(w_ref, g_ref, m_ref, v_ref, d_ref, nm_ref, nv_ref):
        gv = g_ref[...]
        nm = ADAM_B1 * m_ref[...] + (1.0 - ADAM_B1) * gv
        nv = ADAM_B2 * v_ref[...] + (1.0 - ADAM_B2) * (gv * gv)
        nm_ref[...] = nm
        nv_ref[...] = nv
        d_ref[...] = -ADAM_LR * ((nm / c1) / (jnp.sqrt(nv / c2) + ADAM_EPS) + ADAM_WD * w_ref[...])

    spec = pl.BlockSpec((tr, cols), lambda i: (i, 0))
    out = jax.ShapeDtypeStruct((rows, cols), F32)
    res = pl.pallas_call(
        body, name="adamw", grid=(rows // tr,), in_specs=[spec] * 4, out_specs=[spec] * 3, out_shape=[out] * 3,
        compiler_params=_params(("arbitrary",)),
    )(*[a.reshape(rows, cols) for a in (w, g, m, v)])
    return tuple(r.reshape(shape) for r in res)


def _place():
    return lax.axis_index("x"), lax.axis_index("y"), lax.axis_index("c")


def _other_chips(x, y):
    return [(1 - x, y, 2 * (1 - x) + y), (x, 1 - y, 2 * x + 1 - y), (1 - x, 1 - y, 2 * (1 - x) + 1 - y)]


def _sem_specs(*counts):
    return [pltpu.SemaphoreType.DMA((n,)) for n in counts]


def cast_shard(w, layer, chip):
    _, r, c = w.shape

    def body(chip_ref, w_ref, o_ref):
        del chip_ref
        o_ref[...] = w_ref[...].astype(MM_DTYPE)

    grid_spec = pltpu.PrefetchScalarGridSpec(
        num_scalar_prefetch=1, grid=(1,), in_specs=[pl.BlockSpec((None, r, c), lambda i, chip_ref: (layer, 0, 0))],
        out_specs=pl.BlockSpec((None, r, c), lambda i, chip_ref: (chip_ref[0], 0, 0)))
    return pl.pallas_call(
        body, name="cast_shard", grid_spec=grid_spec, out_shape=jax.ShapeDtypeStruct((N_CHIPS, r, c), MM_DTYPE),
        compiler_params=_params(("arbitrary",)),
    )(jnp.reshape(chip, (1,)).astype(jnp.int32), w)


def _row_half(buf, chip, mine, c):
    rh = buf.shape[1] // 2
    return buf.at[chip, pl.ds(pl.multiple_of((c if mine else 1 - c) * rh, 16), rh), :]


def spread_rider(bufs):
    n = len(bufs)

    def start(rin, rout, sems):
        x, y, c = _place()
        me = 2 * x + y
        for k, (px, py, _) in enumerate(_other_chips(x, y)):
            for i, buf in enumerate(rout):
                part = _row_half(buf, me, True, c)
                pltpu.make_async_remote_copy(
                    src_ref=part, dst_ref=part, send_sem=sems[0].at[n * k + i], recv_sem=sems[1].at[n * k + i],
                    device_id=(px, py, c), device_id_type=MESH_ID).start()

    def wait(rin, rout, sems):
        x, y, c = _place()
        for k, (_, _, pj) in enumerate(_other_chips(x, y)):
            for i, buf in enumerate(rout):
                part = _row_half(buf, pj, True, c)
                pltpu.make_async_remote_copy(
                    src_ref=part, dst_ref=part, send_sem=sems[0].at[n * k + i], recv_sem=sems[1].at[n * k + i],
                    device_id=(x, y, c), device_id_type=MESH_ID).wait()

    shapes = [jax.ShapeDtypeStruct(b.shape, b.dtype) for b in bufs]
    return Rider("spread", list(bufs), shapes, {i: i for i in range(n)}, (3 * n, 3 * n), start, wait)


def pass_rider(bufs):
    n = len(bufs)

    def start(rin, rout, sems):
        x, y, c = _place()
        for k, (_, _, pj) in enumerate(_other_chips(x, y)):
            for i, buf in enumerate(rout):
                part = _row_half(buf, pj, True, c)
                pltpu.make_async_remote_copy(
                    src_ref=part, dst_ref=part, send_sem=sems[0].at[n * k + i], recv_sem=sems[1].at[n * k + i],
                    device_id=(x, y, 1 - c), device_id_type=MESH_ID).start()

    def wait(rin, rout, sems):
        x, y, c = _place()
        for k, (_, _, pj) in enumerate(_other_chips(x, y)):
            for i, buf in enumerate(rout):
                part = _row_half(buf, pj, False, c)
                pltpu.make_async_remote_copy(
                    src_ref=part, dst_ref=part, send_sem=sems[0].at[n * k + i], recv_sem=sems[1].at[n * k + i],
                    device_id=(x, y, 1 - c), device_id_type=MESH_ID).wait()

    shapes = [jax.ShapeDtypeStruct(b.shape, b.dtype) for b in bufs]
    return Rider("pass", list(bufs), shapes, {i: i for i in range(n)}, (3 * n, 3 * n), start, wait)


def both_riders(a, b):
    na, oa, sa = len(a.inputs), len(a.out_shapes), len(a.sems)

    def start(rin, rout, sems):
        a.start(rin[:na], rout[:oa], sems[:sa])
        b.start(rin[na:], rout[oa:], sems[sa:])

    def wait(rin, rout, sems):
        a.wait(rin[:na], rout[:oa], sems[:sa])
        b.wait(rin[na:], rout[oa:], sems[sa:])

    aliases = dict(a.aliases)
    aliases.update({na + i: oa + o for i, o in b.aliases.items()})
    return Rider(a.name + "_" + b.name, a.inputs + b.inputs, a.out_shapes + b.out_shapes, aliases, a.sems + b.sems,
                 start, wait)


def gather_small(small):
    def body(small_ref, out_ref, send, recv, local):
        x, y, c = _place()
        me = 2 * x + y
        chips = _other_chips(x, y)
        own = pltpu.make_async_copy(small_ref, out_ref.at[me], local.at[0])
        own.start()
        sends = [pltpu.make_async_remote_copy(src_ref=small_ref, dst_ref=out_ref.at[me], send_sem=send.at[k],
                                              recv_sem=recv.at[k], device_id=(px, py, c), device_id_type=MESH_ID)
                 for k, (px, py, _) in enumerate(chips)]
        for cp in sends:
            cp.start()
        for k, (_, _, pj) in enumerate(chips):
            pltpu.make_async_remote_copy(src_ref=small_ref, dst_ref=out_ref.at[pj], send_sem=send.at[k], recv_sem=recv.at[k],
                                         device_id=(x, y, c), device_id_type=MESH_ID).wait_recv()
        for cp in sends:
            cp.wait_send()
        own.wait()

    return pl.pallas_call(
        body, name="gather_small", in_specs=[_ANY], out_specs=_ANY,
        out_shape=jax.ShapeDtypeStruct((N_CHIPS,) + small.shape, small.dtype), scratch_shapes=_sem_specs(3, 3, 1),
        compiler_params=pltpu.CompilerParams(has_side_effects=True),
    )(small)


def swap_rider(gs):
    n = len(gs)

    def copies(rin, rout, sems):
        x, y, c = _place()
        out = []
        for i, (g, got) in enumerate(zip(rin, rout)):
            rh = g.shape[1] // 2
            theirs = pl.ds(pl.multiple_of((1 - c) * rh, 8), rh)
            out.append(pltpu.make_async_remote_copy(
                src_ref=g.at[:, theirs, :], dst_ref=got, send_sem=sems[0].at[i], recv_sem=sems[1].at[i],
                device_id=(x, y, 1 - c), device_id_type=MESH_ID))
        return out

    def start(rin, rout, sems):
        for cp in copies(rin, rout, sems):
            cp.start()

    def wait(rin, rout, sems):
        for cp in copies(rin, rout, sems):
            cp.wait()

    shapes = [jax.ShapeDtypeStruct((g.shape[0], g.shape[1] // 2, g.shape[2]), g.dtype) for g in gs]
    return Rider("swap", list(gs), shapes, {}, (n, n), start, wait)


def scatter_rider(sbs):
    n = len(sbs)

    def start(rin, rout, sems):
        x, y, c = _place()
        me = 2 * x + y
        for k, (px, py, pj) in enumerate(_other_chips(x, y)):
            for i, (sb, got) in enumerate(zip(rin, rout)):
                pltpu.make_async_remote_copy(
                    src_ref=sb.at[pj], dst_ref=got.at[me], send_sem=sems[0].at[n * k + i], recv_sem=sems[1].at[n * k + i],
                    device_id=(px, py, c), device_id_type=MESH_ID).start()

    def wait(rin, rout, sems):
        x, y, c = _place()
        for k, (_, _, pj) in enumerate(_other_chips(x, y)):
            for i, (sb, got) in enumerate(zip(rin, rout)):
                cp = pltpu.make_async_remote_copy(
                    src_ref=sb.at[pj], dst_ref=got.at[pj], send_sem=sems[0].at[n * k + i], recv_sem=sems[1].at[n * k + i],
                    device_id=(x, y, c), device_id_type=MESH_ID)
                cp.wait_recv()
                cp.wait_send()

    shapes = [jax.ShapeDtypeStruct(sb.shape, sb.dtype) for sb in sbs]
    return Rider("scatter", list(sbs), shapes, {}, (3 * n, 3 * n), start, wait)


def join_rider(fs, layers):
    n = len(fs)

    def half(i, f, mine, place):
        x, y, c = place
        rh = f.shape[1] // 2
        block = 2 * x + y if layers[i] is None else layers[i]
        return f.at[block, pl.ds(pl.multiple_of((c if mine else 1 - c) * rh, 8), rh), :]

    def start(rin, rout, sems):
        x, y, c = _place()
        for i, f in enumerate(rout):
            part = half(i, f, True, (x, y, c))
            pltpu.make_async_remote_copy(
                src_ref=part, dst_ref=part, send_sem=sems[0].at[i], recv_sem=sems[1].at[i],
                device_id=(x, y, 1 - c), device_id_type=MESH_ID).start()

    def wait(rin, rout, sems):
        x, y, c = _place()
        for i, f in enumerate(rout):
            part = half(i, f, False, (x, y, c))
            pltpu.make_async_remote_copy(
                src_ref=part, dst_ref=part, send_sem=sems[0].at[i], recv_sem=sems[1].at[i],
                device_id=(x, y, 1 - c), device_id_type=MESH_ID).wait()

    shapes = [jax.ShapeDtypeStruct(f.shape, f.dtype) for f in fs]
    return Rider("join", list(fs), shapes, {i: i for i in range(n)}, (n, n), start, wait)


def add_halves(gs, gots, wire=BF16):
    m = len(gs)
    x, y, c = _place()

    def body(p_ref, *refs):
        ins, outs = refs[:2 * m], refs[2 * m:]
        for i in range(m):
            s = ins[2 * i][...] + ins[2 * i + 1][...]
            outs[2 * i][...] = s.astype(wire)

            @pl.when(pl.program_id(0) == p_ref[0])
            def _(s=s, own_ref=outs[2 * i + 1]):
                own_ref[...] = s

    in_specs, out_specs, out_shape = [], [], []
    for got in gots:
        n, rh, cols = got.shape
        blk = (None, rh, cols)
        in_specs += [pl.BlockSpec(blk, lambda j, p_ref: (j, p_ref[1], 0)), pl.BlockSpec(blk, lambda j, p_ref: (j, 0, 0))]
        out_specs += [pl.BlockSpec(blk, lambda j, p_ref: (j, 0, 0)), pl.BlockSpec((rh, cols), lambda j, p_ref: (0, 0))]
        out_shape += [jax.ShapeDtypeStruct(got.shape, wire), jax.ShapeDtypeStruct((rh, cols), F32)]
    grid_spec = pltpu.PrefetchScalarGridSpec(num_scalar_prefetch=1, grid=(N_CHIPS,), in_specs=in_specs, out_specs=out_specs)
    res = pl.pallas_call(
        body, name="add_halves", grid_spec=grid_spec, out_shape=out_shape, compiler_params=_params(("arbitrary",)),
    )(jnp.stack([2 * x + y, c]).astype(jnp.int32), *[a for pair in zip(gs, gots) for a in pair])
    return [(res[2 * i + 1], res[2 * i]) for i in range(m)]


def add_chips(owns, gots, fbufs, block=None):
    m = len(owns)
    x, y, c = _place()
    me = 2 * x + y

    def body(p_ref, *refs):
        ins, outs = refs[:5 * m], refs[5 * m:]
        for i in range(m):
            s_ref, g1_ref, g2_ref, g3_ref, _ = ins[5 * i:5 * i + 5]
            outs[i][...] = s_ref[...] + g1_ref[...].astype(F32) + g2_ref[...].astype(F32) + g3_ref[...].astype(F32)

    def other(blk, n, k):
        return pl.BlockSpec(blk, lambda i, p_ref: ((p_ref[0] + k) % n, 0, 0))

    in_specs, out_specs, args = [], [], []
    for own, got, fbuf in zip(owns, gots, fbufs):
        n, rh, cols = got.shape
        blk = (None, rh, cols)
        in_specs += [pl.BlockSpec((rh, cols), lambda i, p_ref: (0, 0)), other(blk, n, 1), other(blk, n, 2), other(blk, n, 3),
                     _ANY]
        out_specs.append(pl.BlockSpec(blk, lambda i, p_ref: (p_ref[2], p_ref[1], 0)))
        args += [own, got, got, got, fbuf]
    grid_spec = pltpu.PrefetchScalarGridSpec(num_scalar_prefetch=1, grid=(1,), in_specs=in_specs, out_specs=out_specs)
    return pl.pallas_call(
        body, name="add_chips", grid_spec=grid_spec, out_shape=[jax.ShapeDtypeStruct(f.shape, F32) for f in fbufs],
        input_output_aliases={5 * i + 5: i for i in range(m)}, compiler_params=_params(("arbitrary",)),
    )(jnp.stack([me, c, me if block is None else block]).astype(jnp.int32), *args)


def _pack(arrays, rows):
    flat = jnp.concatenate([a.reshape(-1) for a in arrays])
    return jnp.pad(flat, (0, rows * LANES - flat.size)).reshape(rows, LANES)


def _unpack(buf, shapes):
    flat = buf.reshape(-1)
    out, at = [], 0
    for s in shapes:
        n = math.prod(s)
        out.append(flat[at:at + n].reshape(s))
        at += n
    return out


CONV_SHARDS = [(DEPTH, K_A, D_A // N_CHIPS), (DEPTH, K_C, D_C // N_CHIPS), (DEPTH, K_F, 2 * D_FF // N_CHIPS)]
CONV_ROWS = 32
SMALL_ROWS = 640


def _join_cols(g):
    n, l, r, c = g.shape
    return jnp.transpose(g, (1, 2, 0, 3)).reshape(l, r, n * c)


BIG = ["w_in", "w_out", "w_up", "w_down"]
WIDE = ["w_up", "w_down"]
NARROW = ["w_in", "w_out"]
TILE_MM = 512
TILE_TN = 1024
TILE_EW = 256


def _pad_rows(a, rows):
    return jnp.pad(a, ((0, rows - a.shape[0]), (0, 0)))


def _row(a):
    return a.reshape(1, -1)


def _tile_perm(tt):
    p = lax.broadcasted_iota(jnp.int32, (tt, tt), 0)
    tok = lax.broadcasted_iota(jnp.int32, (tt, tt), 1)
    return ((tt // 8) * (p % 8) + p // 8 == tok).astype(F32)


def _layer_params(wl, tt):
    n = tt // CHUNK
    tril = jnp.tril(jnp.ones((CHUNK, CHUNK), bool))
    wm = jnp.where(tril[None], wl["sgu_w"], 0.0)
    eye = jnp.eye(n, dtype=F32)
    wt = (eye[None, :, None, :, None] * wm[:, None, :, None, :]).reshape(N_HEADS_B, tt, tt)
    bias_e = jnp.repeat(wl["sgu_b"].T, HEAD, axis=1)
    return (_pad_rows(wl["conv_a_w"], 8), _row(wl["sgu_ln_g"]), _row(wl["sgu_ln_b"]), wt.astype(MM_DTYPE),
            jnp.tile(bias_e, (n, 1)), _pad_rows(wl["conv_c_w"], 32), _row(wl["conv_c_b"]), _row(wl["conv_ln_g"]),
            _row(wl["conv_ln_b"]))


def layer_fwd(x, wl, gw, nxt=None, tm=TILE_MM, tt=TILE_EW):
    mp = _layer_params(wl, tt)
    own = bool(gw.get("spread_wide"))
    rides = ([pass_rider([gw[n] for n in WIDE])] if gw.get("pass_wide") else []) + (
        [spread_rider([gw[n] for n in WIDE])] if own else []) + (
        [spread_rider([nxt[n] for n in NARROW])] if nxt else [])
    ride = None
    for r in rides:
        ride = r if ride is None else both_riders(ride, r)
    (z, h, o, x1, cv), done = mixer_fwd(x, _row(wl["pre_mix_g"]), gw["w_in"], mp, _tile_perm(tt), _row(wl["grp_norm_g"]),
                                        gw["w_out"], _row(wl["post_mix_g"]), tt, rider=ride)
    if gw.get("pass_wide") or own:
        wide, done = done[:2], done[2:]
        if own:
            wide = run_rider(pass_rider(list(wide)))
        gw = dict(gw, w_up=wide[0], w_down=wide[1])
    gw = {n: gw[n] for n in BIG}
    ride = both_riders(spread_rider([nxt[n] for n in WIDE]), pass_rider(list(done))) if nxt else None
    (up0, h2, d, x2), done = ffn_fwd(x1, _row(wl["pre_ffn_g"]), gw["w_up"], _pad_rows(wl["ffn_conv_w"], 8), gw["w_down"],
                                     _row(wl["post_ffn_g"]), tt, rider=ride)
    if nxt:
        nxt = dict(nxt, w_up=done[0], w_down=done[1], w_in=done[2], w_out=done[3], pass_wide=True)
    return x2, dict(x=x, z=z, h=h, o=o, x1=x1, up0=up0, h2=h2, d=d, cv=cv, gw=gw), nxt


def layer_bwd(dx2, wl, layer, sv, pend=None, exchange=True, tm=TILE_MM, tt=TILE_EW):
    mp = _layer_params(wl, tt)
    gw = sv["gw"]
    tk = min(TILE_TN, dx2.shape[0])
    at = {n: BIG.index(n) for n in BIG}
    g = {}
    ride = scatter_rider([sw for _, sw in pend["narrow"]]) if pend else None
    (dd, act, dup0, dpg, dcw), arrived = ffn_bwd(dx2, sv["d"], sv["up0"], _pad_rows(wl["ffn_conv_w"], 8), gw["w_down"],
                                                 _row(wl["post_ffn_g"]), tt, rider=ride)
    fbuf = list(pend["fbuf"]) if pend else grad_buffers()
    if pend:
        done = add_chips([own for own, _ in pend["narrow"]], arrived, [fbuf[at[n]] for n in NARROW], pend["layer"])
        for n, f in zip(NARROW, done):
            fbuf[at[n]] = f
    g["post_ffn_g"] = dpg[0]
    g["ffn_conv_w"] = dcw[:K_F]
    gl = {}
    gl["w_down"] = matmul_tn_down(act, dd, tk)
    gl["w_up"] = matmul_tn_cols(sv["h2"], dup0, tk)
    ride = swap_rider([gl[n] for n in WIDE]) if exchange else None
    (dx1, dg), got = matmul_nt_norm_bwd(dup0, gw["w_up"], sv["x1"], _row(wl["pre_ffn_g"]), dx2, tm, rider=ride)
    g["pre_ffn_g"] = dg[0]
    wide = add_halves([gl[n] for n in WIDE], got) if exchange else None
    ride = scatter_rider([sw for _, sw in wide]) if exchange else None
    if pend:
        ride = both_riders(join_rider(fbuf, [pend["layer"]] * len(fbuf)), ride)
    (dz, gl["w_out"], dpg, dgg, dcaw, dlng, dlnb, dwm, dbias, dccw, dccb, dclg, dclb), rode = mixer_bwd(
        dx1, sv["o"], sv["z"], sv["cv"], mp, _tile_perm(tt), _row(wl["grp_norm_g"]), gw["w_out"],
        _row(wl["post_mix_g"]), tt, rider=ride)
    if exchange:
        fbuf, arrived = (list(rode[:len(BIG)]), rode[len(BIG):]) if pend else (fbuf, rode)
        done = add_chips([own for own, _ in wide], arrived, [fbuf[at[n]] for n in WIDE], layer)
        for n, f in zip(WIDE, done):
            fbuf[at[n]] = f
    g["post_mix_g"] = dpg[0]
    g["grp_norm_g"] = dgg[0]
    g["conv_a_w"] = dcaw[:K_A]
    g["sgu_ln_g"] = dlng[0]
    g["sgu_ln_b"] = dlnb[0]
    g["sgu_w"] = dwm
    g["sgu_b"] = dbias[:, :N_HEADS_B].T
    g["conv_c_w"] = dccw[:K_C]
    g["conv_c_b"] = dccb[0]
    g["conv_ln_g"] = dclg[0]
    g["conv_ln_b"] = dclb[0]
    gl["w_in"] = matmul_tn_in(sv["h"], dz, tk)
    ride = swap_rider([gl[n] for n in NARROW]) if exchange else None
    (dx, dg), got = matmul_nt_norm_bwd(dz, gw["w_in"], sv["x"], _row(wl["pre_mix_g"]), dx1, tm, rider=ride)
    g["pre_mix_g"] = dg[0]
    if not exchange:
        return dx, g, gl
    narrow = add_halves([gl[n] for n in NARROW], got)
    return dx, g, dict(narrow=narrow, fbuf=fbuf, layer=layer)


def grad_buffers():
    return [lax.empty(s, F32) for s in ((DEPTH, D_MODEL, D_IN // N_CHIPS), (DEPTH, D_MODEL // N_CHIPS, D_MODEL),
                                        (DEPTH, D_MODEL, 2 * D_FF // N_CHIPS), (DEPTH, D_FF // N_CHIPS, D_MODEL))]


CONV = ["conv_a_w", "conv_c_w", "ffn_conv_w"]
REPL = ["pre_mix_g", "sgu_ln_g", "sgu_ln_b", "sgu_w", "sgu_b", "conv_c_b", "conv_ln_g", "conv_ln_b", "grp_norm_g",
        "post_mix_g", "pre_ffn_g", "post_ffn_g"]
WEIGHTS = ["pre_mix_g", "w_in", "conv_a_w", "sgu_ln_g", "sgu_ln_b", "sgu_w", "sgu_b", "conv_c_w", "conv_c_b", "conv_ln_g",
           "conv_ln_b", "grp_norm_g", "w_out", "post_mix_g", "pre_ffn_g", "w_up", "ffn_conv_w", "w_down", "post_ffn_g"]


def kernel(x, pre_mix_g, w_in, conv_a_w, sgu_ln_g, sgu_ln_b, sgu_w, sgu_b, conv_c_w, conv_c_b, conv_ln_g, conv_ln_b, grp_norm_g, w_out, post_mix_g, pre_ffn_g, w_up, ffn_conv_w, w_down, post_ffn_g, loss_target, m_pre_mix_g, m_w_in, m_conv_a_w, m_sgu_ln_g, m_sgu_ln_b, m_sgu_w, m_sgu_b, m_conv_c_w, m_conv_c_b, m_conv_ln_g, m_conv_ln_b, m_grp_norm_g, m_w_out, m_post_mix_g, m_pre_ffn_g, m_w_up, m_ffn_conv_w, m_w_down, m_post_ffn_g, v_pre_mix_g, v_w_in, v_conv_a_w, v_sgu_ln_g, v_sgu_ln_b, v_sgu_w, v_sgu_b, v_conv_c_w, v_conv_c_b, v_conv_ln_g, v_conv_ln_b, v_grp_norm_g, v_w_out, v_post_mix_g, v_pre_ffn_g, v_w_up, v_ffn_conv_w, v_w_down, v_post_ffn_g):
    w = dict(pre_mix_g=pre_mix_g, w_in=w_in, conv_a_w=conv_a_w, sgu_ln_g=sgu_ln_g, sgu_ln_b=sgu_ln_b, sgu_w=sgu_w, sgu_b=sgu_b,
             conv_c_w=conv_c_w, conv_c_b=conv_c_b, conv_ln_g=conv_ln_g, conv_ln_b=conv_ln_b, grp_norm_g=grp_norm_g,
             w_out=w_out, post_mix_g=post_mix_g, pre_ffn_g=pre_ffn_g, w_up=w_up, ffn_conv_w=ffn_conv_w, w_down=w_down,
             post_ffn_g=post_ffn_g)
    m = dict(pre_mix_g=m_pre_mix_g, w_in=m_w_in, conv_a_w=m_conv_a_w, sgu_ln_g=m_sgu_ln_g, sgu_ln_b=m_sgu_ln_b,
             sgu_w=m_sgu_w, sgu_b=m_sgu_b, conv_c_w=m_conv_c_w, conv_c_b=m_conv_c_b, conv_ln_g=m_conv_ln_g,
             conv_ln_b=m_conv_ln_b, grp_norm_g=m_grp_norm_g, w_out=m_w_out, post_mix_g=m_post_mix_g,
             pre_ffn_g=m_pre_ffn_g, w_up=m_w_up, ffn_conv_w=m_ffn_conv_w, w_down=m_w_down, post_ffn_g=m_post_ffn_g)
    v = dict(pre_mix_g=v_pre_mix_g, w_in=v_w_in, conv_a_w=v_conv_a_w, sgu_ln_g=v_sgu_ln_g, sgu_ln_b=v_sgu_ln_b,
             sgu_w=v_sgu_w, sgu_b=v_sgu_b, conv_c_w=v_conv_c_w, conv_c_b=v_conv_c_b, conv_ln_g=v_conv_ln_g,
             conv_ln_b=v_conv_ln_b, grp_norm_g=v_grp_norm_g, w_out=v_w_out, post_mix_g=v_post_mix_g,
             pre_ffn_g=v_pre_ffn_g, w_up=v_w_up, ffn_conv_w=v_ffn_conv_w, w_down=v_w_down, post_ffn_g=v_post_ffn_g)
    chip = 2 * lax.axis_index("x") + lax.axis_index("y")

    convs = gather_small(_pack([w[n] for n in CONV], CONV_ROWS))
    gws = [{n: cast_shard(w[n], layer, chip) for n in BIG} for layer in range(DEPTH)]
    first = run_rider(pass_rider(run_rider(spread_rider([gws[0][n] for n in NARROW]))))
    gws[0] = dict(gws[0], **dict(zip(NARROW, first)), spread_wide=True)
    cparts = [_unpack(convs[j], CONV_SHARDS) for j in range(N_CHIPS)]
    full = dict(w)
    for i, n in enumerate(CONV):
        full[n] = _join_cols(jnp.stack([p[i] for p in cparts]))

    xc = to_tiles(x[0], TILE_EW)
    saved = []
    for layer in range(DEPTH):
        nxt = gws[layer + 1] if layer + 1 < DEPTH else None
        xc, sv, nxt = layer_fwd(xc, {n: full[n][layer] for n in REPL + CONV}, gws[layer], nxt)
        if nxt:
            gws[layer + 1] = nxt
        saved.append(sv)
    dxc, loss_part = loss_head(xc, to_tiles(loss_target[0], TILE_EW), TILE_MM)
    loss = lax.psum(loss_part[0, 0], ("x", "y", "c"))
    small = [None] * DEPTH
    pend = None
    for layer in reversed(range(DEPTH)):
        dxc, small[layer], pend = layer_bwd(dxc, {n: full[n][layer] for n in REPL + CONV}, layer, saved[layer], pend)
    grads = {n: jnp.stack([small[layer][n] for layer in range(DEPTH)]) for n in REPL + CONV}

    gsmall = _pack([grads[n] for n in REPL + CONV], SMALL_ROWS).reshape(N_CHIPS, SMALL_ROWS // N_CHIPS, LANES)
    sums = pend["narrow"] + add_halves([gsmall], run_rider(swap_rider([gsmall])), wire=F32)
    arrived = run_rider(scatter_rider([sw for _, sw in sums]))
    fbuf = list(pend["fbuf"])
    at = [BIG.index(n) for n in NARROW]
    for i, f in zip(at, add_chips([own for own, _ in sums[:2]], arrived[:2], [fbuf[i] for i in at], 0)):
        fbuf[i] = f
    fbuf += add_chips([sums[2][0]], arrived[2:], [lax.empty(gsmall.shape, F32)])
    joined = run_rider(join_rider(fbuf, [0] * len(BIG) + [None]))
    out_g = dict(zip(BIG, joined))
    tot = run_rider(pass_rider(run_rider(spread_rider([joined[len(BIG)]]))))[0].reshape(SMALL_ROWS, LANES)
    shapes = [grads[n].shape for n in REPL + CONV]
    for n, gfull in zip(REPL + CONV, _unpack(tot, shapes)):
        if n in CONV:
            width = gfull.shape[-1] // N_CHIPS
            gfull = lax.dynamic_slice_in_dim(gfull, chip * width, width, axis=2)
        out_g[n] = gfull

    deltas, new_m, new_v = {}, {}, {}
    for n in WEIGHTS:
        deltas[n], new_m[n], new_v[n] = adamw(w[n], out_g[n], m[n], v[n])
    return (loss, from_tiles(dxc, TILE_EW)[None], *[out_g[n] for n in WEIGHTS], *[deltas[n] for n in WEIGHTS], *[new_m[n] for n in WEIGHTS],
            *[new_v[n] for n in WEIGHTS])
```
